```python
import math
import jax, jax.numpy as jnp
from jax import lax
import numpy as np

D_MODEL = 1024
BATCH = 32
SEQ = 2048
DEPTH = 1

GLA_HEADS = 4
GLA_DK = 128
GLA_DV = 256
GLA_GATE_RANK = 16
GLA_TAU = 16.0
GLA_CHUNK = 64

MLA_HEADS = 8
MLA_Q_RANK = 384
MLA_KV_RANK = 256
MLA_NOPE = 128
MLA_ROPE = 64
MLA_V = 128
ROPE_THETA = 10000.0
ATTN_BLOCK = 128

D_FF = 2816
CONV_WIDTH = 3

DN_ALPHA = (2.0 * DEPTH) ** 0.25
DN_BETA = (8.0 * DEPTH) ** -0.25
LN_EPS = 1e-5
RMS_EPS = 1e-6

IN_SPLITS = (
    GLA_HEADS * GLA_DK,
    GLA_HEADS * GLA_DK,
    GLA_HEADS * GLA_DV,
    GLA_GATE_RANK,
    GLA_HEADS * GLA_DV,
    MLA_Q_RANK,
    MLA_KV_RANK,
    MLA_ROPE,
    D_MODEL,
    D_MODEL,
)
D_IN = sum(IN_SPLITS)

kernel_name = "hybrid_gla_mla_convffn_deepnorm"


def _rmsnorm(x, g):
    xf = x.astype(jnp.float32)
    y = xf * lax.rsqrt(jnp.mean(xf * xf, axis=-1, keepdims=True) + RMS_EPS)
    return (y * g.astype(jnp.float32)).astype(x.dtype)


def _layernorm(x, g, b):
    xf = x.astype(jnp.float32)
    mu = jnp.mean(xf, axis=-1, keepdims=True)
    xc = xf - mu
    var = jnp.mean(xc * xc, axis=-1, keepdims=True)
    y = xc * lax.rsqrt(var + LN_EPS) * g.astype(jnp.float32) + b.astype(jnp.float32)
    return y.astype(x.dtype)


def _rope_tables(positions):
    half = MLA_ROPE // 2
    inv_freq = ROPE_THETA ** (-jnp.arange(half, dtype=jnp.float32) / half)
    ang = positions.astype(jnp.float32)[..., None] * inv_freq
    return jnp.cos(ang), jnp.sin(ang)


def _rope(x, cos, sin):
    half = MLA_ROPE // 2
    xf = x.astype(jnp.float32)
    x1, x2 = xf[..., :half], xf[..., half:]
    out = jnp.concatenate([x1 * cos - x2 * sin, x2 * cos + x1 * sin], axis=-1)
    return out.astype(x.dtype)


def _gla_chunked(q, k, v, log_a):
    B, S, H, DK = q.shape
    DV = v.shape[-1]
    C = GLA_CHUNK
    N = S // C

    def chunk(t):
        return t.astype(jnp.float32).reshape(B, N, C, H, t.shape[-1]).transpose(1, 0, 3, 2, 4)

    qc = chunk(q) * (DK ** -0.5)
    kc = chunk(k)
    vc = chunk(v)
    b = jnp.cumsum(chunk(log_a), axis=3)
    b_last = b[:, :, :, -1:, :]
    q_in = qc * jnp.exp(b)
    k_in = kc * jnp.exp(-b)
    k_st = kc * jnp.exp(b_last - b)
    decay = jnp.exp(b_last[:, :, :, 0, :])

    causal = jnp.tril(jnp.ones((C, C), dtype=bool))
    att = jnp.einsum('nbhid,nbhjd->nbhij', q_in, k_in)
    att = jnp.where(causal, att, 0.0)
    o_intra = jnp.einsum('nbhij,nbhjv->nbhiv', att, vc)

    def step(state, inp):
        q_n, k_n, v_n, d_n = inp
        o_n = jnp.einsum('bhid,bhdv->bhiv', q_n, state)
        state = state * d_n[..., None] + jnp.einsum('bhjd,bhjv->bhdv', k_n, v_n)
        return state, o_n

    s0 = jnp.zeros((B, H, DK, DV), jnp.float32)
    _, o_inter = lax.scan(step, s0, (q_in, k_st, vc, decay))
    o = o_intra + o_inter
    return o.transpose(1, 0, 3, 2, 4).reshape(B, S, H, DV)


def _mla_attention(q_nope, q_rope, k_nope, k_rope, v):
    B, S, H, _ = q_nope.shape
    nblk = S // ATTN_BLOCK
    scale = (MLA_NOPE + MLA_ROPE) ** -0.5
    key_pos = jnp.arange(S)

    def blocks(t):
        return t.reshape((B, nblk, ATTN_BLOCK) + t.shape[2:]).swapaxes(0, 1)

    def attend(args):
        qn, qr, blk = args
        s = (jnp.einsum('bqhd,bkhd->bhqk', qn, k_nope)
             + jnp.einsum('bqhr,bkr->bhqk', qr, k_rope)).astype(jnp.float32) * scale
        q_pos = blk * ATTN_BLOCK + jnp.arange(ATTN_BLOCK)
        mask = key_pos[None, :] <= q_pos[:, None]
        s = jnp.where(mask, s, -jnp.inf)
        p = jax.nn.softmax(s, axis=-1).astype(v.dtype)
        return jnp.einsum('bhqk,bkhv->bqhv', p, v)

    out = lax.map(attend, (blocks(q_nope), blocks(q_rope), jnp.arange(nblk)))
    return out.swapaxes(0, 1).reshape(B, S, H * MLA_V)


def _fwd_setup_inputs(seed: int = 0) -> dict:
    key = jax.random.key(seed)
    ks = jax.random.split(key, 24)
    f32 = jnp.float32
    L = DEPTH

    def w(k, shape, fan_in, scale=1.0):
        return jax.random.normal(k, shape, f32) * (scale * fan_in ** -0.5)

    def gain(k, shape):
        return 1.0 + 0.02 * jax.random.normal(k, shape, f32)

    def bias(k, shape):
        return 0.02 * jax.random.normal(k, shape, f32)

    x = jax.random.normal(ks[0], (BATCH, SEQ, D_MODEL), f32)
    offs = jax.random.randint(ks[1], (BATCH, 1), 0, 4096, dtype=jnp.int32)
    positions = (offs + jnp.arange(SEQ, dtype=jnp.int32)[None, :]).astype(jnp.int32)
    return {
        "x": x,
        "positions": positions,
        "w_in": w(ks[2], (L, D_MODEL, D_IN), D_MODEL),
        "gla_w_gate_up": w(ks[3], (L, GLA_GATE_RANK, GLA_HEADS * GLA_DK), GLA_GATE_RANK),
        "gla_b_gate": bias(ks[4], (L, GLA_HEADS * GLA_DK)),
        "gla_norm_g": gain(ks[5], (L, GLA_DV)),
        "w_gla_o": w(ks[6], (L, GLA_HEADS * GLA_DV, D_MODEL), GLA_HEADS * GLA_DV),
        "mla_q_norm_g": gain(ks[7], (L, MLA_Q_RANK)),
        "mla_w_uq": w(ks[8], (L, MLA_Q_RANK, MLA_HEADS * (MLA_NOPE + MLA_ROPE)), MLA_Q_RANK),
        "mla_kv_norm_g": gain(ks[9], (L, MLA_KV_RANK)),
        "mla_w_ukv": w(ks[10], (L, MLA_KV_RANK, MLA_HEADS * (MLA_NOPE + MLA_V)), MLA_KV_RANK),
        "w_mla_o": w(ks[11], (L, MLA_HEADS * MLA_V, D_MODEL), MLA_HEADS * MLA_V),
        "w_out": w(ks[12], (L, D_MODEL, D_MODEL), D_MODEL, DN_BETA),
        "ln1_g": gain(ks[13], (L, D_MODEL)),
        "ln1_b": bias(ks[14], (L, D_MODEL)),
        "w_up": w(ks[15], (L, D_MODEL, 2 * D_FF), D_MODEL),
        "conv_w": w(ks[16], (L, CONV_WIDTH, 2 * D_FF), CONV_WIDTH),
        "conv_b": bias(ks[17], (L, 2 * D_FF)),
        "w_down": w(ks[18], (L, D_FF, D_MODEL), D_FF, DN_BETA),
        "ln2_g": gain(ks[19], (L, D_MODEL)),
        "ln2_b": bias(ks[20], (L, D_MODEL)),
    }


def _fwd_reference(x, positions, w_in, gla_w_gate_up, gla_b_gate, gla_norm_g, w_gla_o,
              mla_q_norm_g, mla_w_uq, mla_kv_norm_g, mla_w_ukv, w_mla_o, w_out,
              ln1_g, ln1_b, w_up, conv_w, conv_b, w_down, ln2_g, ln2_b):
    B, S, D = x.shape
    offsets = []
    acc = 0
    for n in IN_SPLITS[:-1]:
        acc += n
        offsets.append(acc)
    cos, sin = _rope_tables(positions)
    cos_h, sin_h = cos[:, :, None, :], sin[:, :, None, :]

    h = x
    for l in range(DEPTH):
        proj = h @ w_in[l]
        (g_q, g_k, g_v, g_r, g_og, m_cq, m_ckv, m_kr, gate_a, gate_b) = jnp.split(proj, offsets, axis=-1)

        gate_logit = (g_r @ gla_w_gate_up[l] + gla_b_gate[l]).astype(jnp.float32)
        log_a = jax.nn.log_sigmoid(gate_logit) / GLA_TAU
        o = _gla_chunked(g_q.reshape(B, S, GLA_HEADS, GLA_DK),
                         g_k.reshape(B, S, GLA_HEADS, GLA_DK),
                         g_v.reshape(B, S, GLA_HEADS, GLA_DV),
                         log_a.reshape(B, S, GLA_HEADS, GLA_DK)).astype(h.dtype)
        o = _rmsnorm(o, gla_norm_g[l]).reshape(B, S, GLA_HEADS * GLA_DV)
        y_gla = (o * jax.nn.silu(g_og)) @ w_gla_o[l]

        c_q = _rmsnorm(m_cq, mla_q_norm_g[l])
        q = (c_q @ mla_w_uq[l]).reshape(B, S, MLA_HEADS, MLA_NOPE + MLA_ROPE)
        q_nope = q[..., :MLA_NOPE]
        q_rope = _rope(q[..., MLA_NOPE:], cos_h, sin_h)
        c_kv = _rmsnorm(m_ckv, mla_kv_norm_g[l])
        kv = (c_kv @ mla_w_ukv[l]).reshape(B, S, MLA_HEADS, MLA_NOPE + MLA_V)
        k_nope = kv[..., :MLA_NOPE]
        v = kv[..., MLA_NOPE:]
        k_rope = _rope(m_kr, cos, sin)
        y_mla = _mla_attention(q_nope, q_rope, k_nope, k_rope, v) @ w_mla_o[l]

        mixed = (jax.nn.sigmoid(gate_a) * y_gla + jax.nn.sigmoid(gate_b) * y_mla) @ w_out[l]
        h = _layernorm(DN_ALPHA * h + mixed, ln1_g[l], ln1_b[l])

        u = h @ w_up[l]
        u = lax.conv_general_dilated(
            u, conv_w[l][:, None, :].astype(u.dtype), window_strides=(1,),
            padding=[(CONV_WIDTH - 1, 0)],
            dimension_numbers=('NWC', 'WIO', 'NWC'),
            feature_group_count=2 * D_FF) + conv_b[l]
        u_gate, u_val = u[..., :D_FF], u[..., D_FF:]
        f = (jax.nn.silu(u_gate) * u_val) @ w_down[l]
        h = _layernorm(DN_ALPHA * h + f, ln2_g[l], ln2_b[l])
    return h


import jax as _jax
import jax.numpy as _jnp

TWIN_FORMAT = 'train_step'
FWD_PARAMS = ['x', 'positions', 'w_in', 'gla_w_gate_up', 'gla_b_gate', 'gla_norm_g', 'w_gla_o', 'mla_q_norm_g', 'mla_w_uq', 'mla_kv_norm_g', 'mla_w_ukv', 'w_mla_o', 'w_out', 'ln1_g', 'ln1_b', 'w_up', 'conv_w', 'conv_b', 'w_down', 'ln2_g', 'ln2_b']
TWIN_WEIGHTS = ['w_in', 'gla_w_gate_up', 'gla_b_gate', 'gla_norm_g', 'w_gla_o', 'mla_q_norm_g', 'mla_w_uq', 'mla_kv_norm_g', 'mla_w_ukv', 'w_mla_o', 'w_out', 'ln1_g', 'ln1_b', 'w_up', 'conv_w', 'conv_b', 'w_down', 'ln2_g', 'ln2_b']
TWIN_DIFF_INPUT = 'x'
TWIN_INPUTS = ['x', 'positions', 'w_in', 'gla_w_gate_up', 'gla_b_gate', 'gla_norm_g', 'w_gla_o', 'mla_q_norm_g', 'mla_w_uq', 'mla_kv_norm_g', 'mla_w_ukv', 'w_mla_o', 'w_out', 'ln1_g', 'ln1_b', 'w_up', 'conv_w', 'conv_b', 'w_down', 'ln2_g', 'ln2_b', 'loss_target', 'm_w_in', 'm_gla_w_gate_up', 'm_gla_b_gate', 'm_gla_norm_g', 'm_w_gla_o', 'm_mla_q_norm_g', 'm_mla_w_uq', 'm_mla_kv_norm_g', 'm_mla_w_ukv', 'm_w_mla_o', 'm_w_out', 'm_ln1_g', 'm_ln1_b', 'm_w_up', 'm_conv_w', 'm_conv_b', 'm_w_down', 'm_ln2_g', 'm_ln2_b', 'v_w_in', 'v_gla_w_gate_up', 'v_gla_b_gate', 'v_gla_norm_g', 'v_w_gla_o', 'v_mla_q_norm_g', 'v_mla_w_uq', 'v_mla_kv_norm_g', 'v_mla_w_ukv', 'v_w_mla_o', 'v_w_out', 'v_ln1_g', 'v_ln1_b', 'v_w_up', 'v_conv_w', 'v_conv_b', 'v_w_down', 'v_ln2_g', 'v_ln2_b']
TWIN_OUTPUTS = ['loss', 'grad_x', 'grad_w_in', 'grad_gla_w_gate_up', 'grad_gla_b_gate', 'grad_gla_norm_g', 'grad_w_gla_o', 'grad_mla_q_norm_g', 'grad_mla_w_uq', 'grad_mla_kv_norm_g', 'grad_mla_w_ukv', 'grad_w_mla_o', 'grad_w_out', 'grad_ln1_g', 'grad_ln1_b', 'grad_w_up', 'grad_conv_w', 'grad_conv_b', 'grad_w_down', 'grad_ln2_g', 'grad_ln2_b', 'delta_w_in', 'delta_gla_w_gate_up', 'delta_gla_b_gate', 'delta_gla_norm_g', 'delta_w_gla_o', 'delta_mla_q_norm_g', 'delta_mla_w_uq', 'delta_mla_kv_norm_g', 'delta_mla_w_ukv', 'delta_w_mla_o', 'delta_w_out', 'delta_ln1_g', 'delta_ln1_b', 'delta_w_up', 'delta_conv_w', 'delta_conv_b', 'delta_w_down', 'delta_ln2_g', 'delta_ln2_b', 'new_m_w_in', 'new_m_gla_w_gate_up', 'new_m_gla_b_gate', 'new_m_gla_norm_g', 'new_m_w_gla_o', 'new_m_mla_q_norm_g', 'new_m_mla_w_uq', 'new_m_mla_kv_norm_g', 'new_m_mla_w_ukv', 'new_m_w_mla_o', 'new_m_w_out', 'new_m_ln1_g', 'new_m_ln1_b', 'new_m_w_up', 'new_m_conv_w', 'new_m_conv_b', 'new_m_w_down', 'new_m_ln2_g', 'new_m_ln2_b', 'new_v_w_in', 'new_v_gla_w_gate_up', 'new_v_gla_b_gate', 'new_v_gla_norm_g', 'new_v_w_gla_o', 'new_v_mla_q_norm_g', 'new_v_mla_w_uq', 'new_v_mla_kv_norm_g', 'new_v_mla_w_ukv', 'new_v_w_mla_o', 'new_v_w_out', 'new_v_ln1_g', 'new_v_ln1_b', 'new_v_w_up', 'new_v_conv_w', 'new_v_conv_b', 'new_v_w_down', 'new_v_ln2_g', 'new_v_ln2_b']
TWIN_LEAF_KINDS = {'loss': 'loss', 'grad_x': 'grad_x', 'grad_w_in': 'grad_w', 'grad_gla_w_gate_up': 'grad_w', 'grad_gla_b_gate': 'grad_w', 'grad_gla_norm_g': 'grad_w', 'grad_w_gla_o': 'grad_w', 'grad_mla_q_norm_g': 'grad_w', 'grad_mla_w_uq': 'grad_w', 'grad_mla_kv_norm_g': 'grad_w', 'grad_mla_w_ukv': 'grad_w', 'grad_w_mla_o': 'grad_w', 'grad_w_out': 'grad_w', 'grad_ln1_g': 'grad_w', 'grad_ln1_b': 'grad_w', 'grad_w_up': 'grad_w', 'grad_conv_w': 'grad_w', 'grad_conv_b': 'grad_w', 'grad_w_down': 'grad_w', 'grad_ln2_g': 'grad_w', 'grad_ln2_b': 'grad_w', 'delta_w_in': 'delta_w', 'delta_gla_w_gate_up': 'delta_w', 'delta_gla_b_gate': 'delta_w', 'delta_gla_norm_g': 'delta_w', 'delta_w_gla_o': 'delta_w', 'delta_mla_q_norm_g': 'delta_w', 'delta_mla_w_uq': 'delta_w', 'delta_mla_kv_norm_g': 'delta_w', 'delta_mla_w_ukv': 'delta_w', 'delta_w_mla_o': 'delta_w', 'delta_w_out': 'delta_w', 'delta_ln1_g': 'delta_w', 'delta_ln1_b': 'delta_w', 'delta_w_up': 'delta_w', 'delta_conv_w': 'delta_w', 'delta_conv_b': 'delta_w', 'delta_w_down': 'delta_w', 'delta_ln2_g': 'delta_w', 'delta_ln2_b': 'delta_w', 'new_m_w_in': 'new_m', 'new_m_gla_w_gate_up': 'new_m', 'new_m_gla_b_gate': 'new_m', 'new_m_gla_norm_g': 'new_m', 'new_m_w_gla_o': 'new_m', 'new_m_mla_q_norm_g': 'new_m', 'new_m_mla_w_uq': 'new_m', 'new_m_mla_kv_norm_g': 'new_m', 'new_m_mla_w_ukv': 'new_m', 'new_m_w_mla_o': 'new_m', 'new_m_w_out': 'new_m', 'new_m_ln1_g': 'new_m', 'new_m_ln1_b': 'new_m', 'new_m_w_up': 'new_m', 'new_m_conv_w': 'new_m', 'new_m_conv_b': 'new_m', 'new_m_w_down': 'new_m', 'new_m_ln2_g': 'new_m', 'new_m_ln2_b': 'new_m', 'new_v_w_in': 'new_v', 'new_v_gla_w_gate_up': 'new_v', 'new_v_gla_b_gate': 'new_v', 'new_v_gla_norm_g': 'new_v', 'new_v_w_gla_o': 'new_v', 'new_v_mla_q_norm_g': 'new_v', 'new_v_mla_w_uq': 'new_v', 'new_v_mla_kv_norm_g': 'new_v', 'new_v_mla_w_ukv': 'new_v', 'new_v_w_mla_o': 'new_v', 'new_v_w_out': 'new_v', 'new_v_ln1_g': 'new_v', 'new_v_ln1_b': 'new_v', 'new_v_w_up': 'new_v', 'new_v_conv_w': 'new_v', 'new_v_conv_b': 'new_v', 'new_v_w_down': 'new_v', 'new_v_ln2_g': 'new_v', 'new_v_ln2_b': 'new_v'}


def _forward(args):
    return _fwd_reference(*[args[k] for k in FWD_PARAMS])


def _output_shape():
    out = _jax.eval_shape(lambda: _forward(_fwd_setup_inputs(0)))
    return out.shape, out.dtype

N_MICROBATCH = 1
ADAM_LR = 0.001
ADAM_B1 = 0.9
ADAM_B2 = 0.999
ADAM_EPS = 1e-08
ADAM_WD = 0.01
ADAM_STEP = 10
PER_EXAMPLE_BATCH_AXIS = {'x': 0, 'positions': 0, 'loss_target': 0}
SHARED_INPUTS = []
_WEIGHT_DTYPES = {'w_in': _jnp.float32, 'gla_w_gate_up': _jnp.float32, 'gla_b_gate': _jnp.float32, 'gla_norm_g': _jnp.float32, 'w_gla_o': _jnp.float32, 'mla_q_norm_g': _jnp.float32, 'mla_w_uq': _jnp.float32, 'mla_kv_norm_g': _jnp.float32, 'mla_w_ukv': _jnp.float32, 'w_mla_o': _jnp.float32, 'w_out': _jnp.float32, 'ln1_g': _jnp.float32, 'ln1_b': _jnp.float32, 'w_up': _jnp.float32, 'conv_w': _jnp.float32, 'conv_b': _jnp.float32, 'w_down': _jnp.float32, 'ln2_g': _jnp.float32, 'ln2_b': _jnp.float32}
MOMENT_SCALE = {'w_in': 3.769142e-02, 'gla_w_gate_up': 7.222254e-03, 'gla_b_gate': 2.726564e-02, 'gla_norm_g': 1.321241e-01, 'w_gla_o': 4.168270e-02, 'mla_q_norm_g': 2.040837e-02, 'mla_w_uq': 1.039515e-02, 'mla_kv_norm_g': 3.958733e-02, 'mla_w_ukv': 1.315152e-02, 'w_mla_o': 1.527990e-02, 'w_out': 7.466384e-02, 'ln1_g': 1.719766e+00, 'ln1_b': 6.987369e-01, 'w_up': 4.473401e-02, 'conv_w': 4.609977e-02, 'conv_b': 5.369058e-02, 'w_down': 1.236195e-01, 'ln2_g': 6.392173e+01, 'ln2_b': 1.014923e+00}


def _to_microbatches(a, axis):
    t = _jnp.moveaxis(a, axis, 0)
    t = t.reshape((N_MICROBATCH, t.shape[0] // N_MICROBATCH) + t.shape[1:])
    return _jnp.moveaxis(t, 1, axis + 1)


def setup_inputs(seed: int = 0) -> dict:
    inp = _fwd_setup_inputs(seed)
    key = _jax.random.fold_in(_jax.random.key(seed), 7919)
    shape, _ = _output_shape()
    out = dict(inp)
    out["loss_target"] = _jax.random.normal(_jax.random.fold_in(key, 0), shape, _jnp.float32)
    for i, name in enumerate(TWIN_WEIGHTS):
        w = inp[name].astype(_jnp.float32)
        if MOMENT_SCALE is None:
            s = _jnp.sqrt(_jnp.mean(_jnp.square(w)) + 1e-30)
        else:
            s = MOMENT_SCALE[name]
        km, kv = _jax.random.split(_jax.random.fold_in(key, i + 1))
        out[name] = w
        out["m_" + name] = s * _jax.random.normal(km, w.shape, _jnp.float32)
        out["v_" + name] = (s * s) * _jax.random.uniform(kv, w.shape, _jnp.float32, 0.5, 1.5)
    if N_MICROBATCH > 1:
        for name, axis in PER_EXAMPLE_BATCH_AXIS.items():
            out[name] = _to_microbatches(out[name], axis)
    return {'x': out['x'], 'positions': out['positions'], 'w_in': out['w_in'], 'gla_w_gate_up': out['gla_w_gate_up'], 'gla_b_gate': out['gla_b_gate'], 'gla_norm_g': out['gla_norm_g'], 'w_gla_o': out['w_gla_o'], 'mla_q_norm_g': out['mla_q_norm_g'], 'mla_w_uq': out['mla_w_uq'], 'mla_kv_norm_g': out['mla_kv_norm_g'], 'mla_w_ukv': out['mla_w_ukv'], 'w_mla_o': out['w_mla_o'], 'w_out': out['w_out'], 'ln1_g': out['ln1_g'], 'ln1_b': out['ln1_b'], 'w_up': out['w_up'], 'conv_w': out['conv_w'], 'conv_b': out['conv_b'], 'w_down': out['w_down'], 'ln2_g': out['ln2_g'], 'ln2_b': out['ln2_b'], 'loss_target': out['loss_target'], 'm_w_in': out['m_w_in'], 'm_gla_w_gate_up': out['m_gla_w_gate_up'], 'm_gla_b_gate': out['m_gla_b_gate'], 'm_gla_norm_g': out['m_gla_norm_g'], 'm_w_gla_o': out['m_w_gla_o'], 'm_mla_q_norm_g': out['m_mla_q_norm_g'], 'm_mla_w_uq': out['m_mla_w_uq'], 'm_mla_kv_norm_g': out['m_mla_kv_norm_g'], 'm_mla_w_ukv': out['m_mla_w_ukv'], 'm_w_mla_o': out['m_w_mla_o'], 'm_w_out': out['m_w_out'], 'm_ln1_g': out['m_ln1_g'], 'm_ln1_b': out['m_ln1_b'], 'm_w_up': out['m_w_up'], 'm_conv_w': out['m_conv_w'], 'm_conv_b': out['m_conv_b'], 'm_w_down': out['m_w_down'], 'm_ln2_g': out['m_ln2_g'], 'm_ln2_b': out['m_ln2_b'], 'v_w_in': out['v_w_in'], 'v_gla_w_gate_up': out['v_gla_w_gate_up'], 'v_gla_b_gate': out['v_gla_b_gate'], 'v_gla_norm_g': out['v_gla_norm_g'], 'v_w_gla_o': out['v_w_gla_o'], 'v_mla_q_norm_g': out['v_mla_q_norm_g'], 'v_mla_w_uq': out['v_mla_w_uq'], 'v_mla_kv_norm_g': out['v_mla_kv_norm_g'], 'v_mla_w_ukv': out['v_mla_w_ukv'], 'v_w_mla_o': out['v_w_mla_o'], 'v_w_out': out['v_w_out'], 'v_ln1_g': out['v_ln1_g'], 'v_ln1_b': out['v_ln1_b'], 'v_w_up': out['v_w_up'], 'v_conv_w': out['v_conv_w'], 'v_conv_b': out['v_conv_b'], 'v_w_down': out['v_w_down'], 'v_ln2_g': out['v_ln2_g'], 'v_ln2_b': out['v_ln2_b']}


def _loss(weights, diff, rest, loss_target):
    with _jax.named_scope("forward"):
        args = {**rest, TWIN_DIFF_INPUT: diff, **{k: w.astype(_WEIGHT_DTYPES[k]) for k, w in weights.items()}}
        y = _forward(args)
    with _jax.named_scope("loss_head"):
        err = _jnp.square(y.astype(_jnp.float32) - loss_target)
        return 0.5 * _jnp.sum(_jnp.mean(err, axis=-1)) if err.ndim else 0.5 * err


def _adamw(w, g, m, v):
    m = ADAM_B1 * m + (1.0 - ADAM_B1) * g
    v = ADAM_B2 * v + (1.0 - ADAM_B2) * _jnp.square(g)
    m_hat = m / (1.0 - ADAM_B1 ** ADAM_STEP)
    v_hat = v / (1.0 - ADAM_B2 ** ADAM_STEP)
    delta = -ADAM_LR * (m_hat / (_jnp.sqrt(v_hat) + ADAM_EPS) + ADAM_WD * w)
    return delta, m, v


def reference(x, positions, w_in, gla_w_gate_up, gla_b_gate, gla_norm_g, w_gla_o, mla_q_norm_g, mla_w_uq, mla_kv_norm_g, mla_w_ukv, w_mla_o, w_out, ln1_g, ln1_b, w_up, conv_w, conv_b, w_down, ln2_g, ln2_b, loss_target, m_w_in, m_gla_w_gate_up, m_gla_b_gate, m_gla_norm_g, m_w_gla_o, m_mla_q_norm_g, m_mla_w_uq, m_mla_kv_norm_g, m_mla_w_ukv, m_w_mla_o, m_w_out, m_ln1_g, m_ln1_b, m_w_up, m_conv_w, m_conv_b, m_w_down, m_ln2_g, m_ln2_b, v_w_in, v_gla_w_gate_up, v_gla_b_gate, v_gla_norm_g, v_w_gla_o, v_mla_q_norm_g, v_mla_w_uq, v_mla_kv_norm_g, v_mla_w_ukv, v_w_mla_o, v_w_out, v_ln1_g, v_ln1_b, v_w_up, v_conv_w, v_conv_b, v_w_down, v_ln2_g, v_ln2_b):
    given = dict(x=x, positions=positions, w_in=w_in, gla_w_gate_up=gla_w_gate_up, gla_b_gate=gla_b_gate, gla_norm_g=gla_norm_g, w_gla_o=w_gla_o, mla_q_norm_g=mla_q_norm_g, mla_w_uq=mla_w_uq, mla_kv_norm_g=mla_kv_norm_g, mla_w_ukv=mla_w_ukv, w_mla_o=w_mla_o, w_out=w_out, ln1_g=ln1_g, ln1_b=ln1_b, w_up=w_up, conv_w=conv_w, conv_b=conv_b, w_down=w_down, ln2_g=ln2_g, ln2_b=ln2_b, loss_target=loss_target, m_w_in=m_w_in, m_gla_w_gate_up=m_gla_w_gate_up, m_gla_b_gate=m_gla_b_gate, m_gla_norm_g=m_gla_norm_g, m_w_gla_o=m_w_gla_o, m_mla_q_norm_g=m_mla_q_norm_g, m_mla_w_uq=m_mla_w_uq, m_mla_kv_norm_g=m_mla_kv_norm_g, m_mla_w_ukv=m_mla_w_ukv, m_w_mla_o=m_w_mla_o, m_w_out=m_w_out, m_ln1_g=m_ln1_g, m_ln1_b=m_ln1_b, m_w_up=m_w_up, m_conv_w=m_conv_w, m_conv_b=m_conv_b, m_w_down=m_w_down, m_ln2_g=m_ln2_g, m_ln2_b=m_ln2_b, v_w_in=v_w_in, v_gla_w_gate_up=v_gla_w_gate_up, v_gla_b_gate=v_gla_b_gate, v_gla_norm_g=v_gla_norm_g, v_w_gla_o=v_w_gla_o, v_mla_q_norm_g=v_mla_q_norm_g, v_mla_w_uq=v_mla_w_uq, v_mla_kv_norm_g=v_mla_kv_norm_g, v_mla_w_ukv=v_mla_w_ukv, v_w_mla_o=v_w_mla_o, v_w_out=v_w_out, v_ln1_g=v_ln1_g, v_ln1_b=v_ln1_b, v_w_up=v_w_up, v_conv_w=v_conv_w, v_conv_b=v_conv_b, v_w_down=v_w_down, v_ln2_g=v_ln2_g, v_ln2_b=v_ln2_b)
    weights = {n: given[n] for n in TWIN_WEIGHTS}
    shared = {n: given[n] for n in SHARED_INPUTS}
    per_example = {n: given[n] for n in ['x', 'positions']}
    grad_fn = _jax.value_and_grad(_loss, argnums=(0, 1))

    def one_microbatch(ex, loss_target):
        ex = dict(ex)
        diff = ex.pop(TWIN_DIFF_INPUT)
        return grad_fn(weights, diff, {**shared, **ex}, loss_target)

    if N_MICROBATCH == 1:
        loss, (grad_w, grad_x) = one_microbatch(per_example, given["loss_target"])
    else:
        def body(carry, xs):
            loss_sum, grad_sum = carry
            l_k, (gw_k, gx_k) = one_microbatch(xs[0], xs[1])
            with _jax.named_scope("update"):
                return (loss_sum + l_k, _jax.tree.map(_jnp.add, grad_sum, gw_k)), gx_k

        init = (_jnp.zeros((), _jnp.float32), _jax.tree.map(_jnp.zeros_like, weights))
        (loss, grad_w), grad_x = _jax.lax.scan(body, init, (per_example, given["loss_target"]))
    with _jax.named_scope("update"):
        delta_w, new_m, new_v = {}, {}, {}
        for n in TWIN_WEIGHTS:
            delta_w[n], new_m[n], new_v[n] = _adamw(weights[n], grad_w[n], given["m_" + n], given["v_" + n])
    return (loss, grad_x, *[grad_w[n] for n in TWIN_WEIGHTS], *[delta_w[n] for n in TWIN_WEIGHTS],
            *[new_m[n] for n in TWIN_WEIGHTS], *[new_v[n] for n in TWIN_WEIGHTS])
```

```python
import functools

import jax
import jax.numpy as jnp
from jax import lax
from jax.experimental import pallas as pl
from jax.experimental.pallas import tpu as pltpu

F32 = jnp.float32
BF = jnp.bfloat16

D = 1024
GH, GDK, GDV, GR, GTAU, GC = 4, 128, 256, 16, 16.0, 64
MH, MQR, MKR, NOPE, ROPE, MV = 8, 384, 256, 128, 64, 128
THETA = 10000.0
DFF = 2816
ALPHA = 2.0 ** 0.25
LN_EPS = 1e-5
RMS_EPS = 1e-6
NDEV = 8
ADAM_LR, ADAM_B1, ADAM_B2, ADAM_EPS, ADAM_WD, ADAM_STEP = 0.001, 0.9, 0.999, 1e-08, 0.01, 10

PG_W = 3200
PM_W = 768
PT_W = 2048
NEG = -1e30
MESH_ID = pl.DeviceIdType.MESH
VMEM_MB = 1024 * 1024


def _params(sem, vmem_mb=48):
    return pltpu.CompilerParams(dimension_semantics=sem, vmem_limit_bytes=vmem_mb * VMEM_MB)


def _dot(a, b):
    return lax.dot_general(a, b, (((1,), (0,)), ((), ())), preferred_element_type=F32)


def _dot_nt(a, b):
    return lax.dot_general(a, b, (((1,), (1,)), ((), ())), preferred_element_type=F32)


def _dot_tn(a, b):
    return lax.dot_general(a, b, (((0,), (0,)), ((), ())), preferred_element_type=F32)


def _iota(shape, dim):
    return lax.broadcasted_iota(jnp.int32, shape, dim)


def _sigmoid(x):
    return 1.0 / (1.0 + jnp.exp(-x))


def _tri_mm(tri_bf, x):
    hi = x.astype(BF)
    r1 = x - hi.astype(F32)
    mid = r1.astype(BF)
    lo = (r1 - mid.astype(F32)).astype(BF)
    return _dot(tri_bf, hi) + _dot(tri_bf, mid) + _dot(tri_bf, lo)


def _matmul(a, b, mode, *, name, c_in=None, out_dtype=F32, tm=512, tn=512, tk=512):
    if mode == "nn":
        (M, K), (_, N) = a.shape, b.shape
    elif mode == "nt":
        (M, K), (N, _) = a.shape, b.shape
    else:
        (K, M), (_, N) = a.shape, b.shape
    tm, tn, tk = min(tm, M), min(tn, N), min(tk, K)
    assert M % tm == 0 and N % tn == 0 and K % tk == 0, (name, M, N, K, tm, tn, tk)
    nk = K // tk
    dot = {"nn": _dot, "nt": _dot_nt, "tn": _dot_tn}[mode]

    def body(*refs):
        if c_in is None:
            a_ref, b_ref, o_ref, acc_ref = refs
        else:
            a_ref, b_ref, c_ref, o_ref, acc_ref = refs
        k = pl.program_id(2)

        @pl.when(k == 0)
        def _():
            if c_in is None:
                acc_ref[...] = jnp.zeros_like(acc_ref)
            else:
                acc_ref[...] = c_ref[...].astype(F32)

        acc_ref[...] += dot(a_ref[...].astype(BF), b_ref[...].astype(BF))

        @pl.when(k == nk - 1)
        def _():
            o_ref[...] = acc_ref[...].astype(out_dtype)

    if mode == "tn":
        a_spec = pl.BlockSpec((tk, tm), lambda i, j, k: (k, i))
    else:
        a_spec = pl.BlockSpec((tm, tk), lambda i, j, k: (i, k))
    if mode == "nt":
        b_spec = pl.BlockSpec((tn, tk), lambda i, j, k: (j, k))
    else:
        b_spec = pl.BlockSpec((tk, tn), lambda i, j, k: (k, j))
    in_specs = [a_spec, b_spec]
    args = [a, b]
    if c_in is not None:
        in_specs.append(pl.BlockSpec((tm, tn), lambda i, j, k: (i, j)))
        args.append(c_in)
    return pl.pallas_call(
        body, name=name,
        out_shape=jax.ShapeDtypeStruct((M, N), out_dtype),
        grid=(M // tm, N // tn, nk),
        in_specs=in_specs,
        out_specs=pl.BlockSpec((tm, tn), lambda i, j, k: (i, j)),
        scratch_shapes=[pltpu.VMEM((tm, tn), F32)],
        compiler_params=_params(("parallel", "parallel", "arbitrary")),
    )(*args)


def _gla_gate(pg_ref, rows, wg_ref, bg_ref):
    r = pg_ref[rows, 3072:3200].astype(BF)
    logit = _dot(r, wg_ref[...]) + bg_ref[...]
    la = (jnp.minimum(logit, 0.0) - jnp.log(1.0 + jnp.exp(-jnp.abs(logit)))) * (1.0 / GTAU)
    return r, logit, la


def _gla_fwd(pg, wg, bg, gn, ltri, *, nseq, S, tm):
    T = pg.shape[0]
    nb, nc = S // tm, tm // GC
    qscale = GDK ** -0.5

    def body(pg_ref, wg_ref, bg_ref, gn_ref, l_ref, o_ref, zg_ref, st_ref, st_scr):
        @pl.when(pl.program_id(1) == 0)
        def _():
            st_scr[...] = jnp.zeros_like(st_scr)

        ltri_v = l_ref[...]
        causal = _iota((GC, GC), 0) >= _iota((GC, GC), 1)
        last_row = _iota((GC, GDK), 0) == GC - 1
        g = gn_ref[...]

        def chunk(c, carry):
            rows = pl.ds(pl.multiple_of(c * GC, GC), GC)
            _, _, la = _gla_gate(pg_ref, rows, wg_ref, bg_ref)
            b = _tri_mm(ltri_v, la)
            for h in range(GH):
                q = pg_ref[rows, h * GDK:(h + 1) * GDK]
                k = pg_ref[rows, 512 + h * GDK:512 + (h + 1) * GDK]
                v = pg_ref[rows, 1024 + h * GDV:1024 + (h + 1) * GDV].astype(BF)
                og = pg_ref[rows, 2048 + h * GDV:2048 + (h + 1) * GDV]
                bh = b[:, h * GDK:(h + 1) * GDK]
                bl = jnp.sum(jnp.where(last_row, bh, 0.0), axis=0, keepdims=True)
                q_in = (q * (qscale * jnp.exp(bh))).astype(BF)
                k_in = (k * jnp.exp(-bh)).astype(BF)
                k_st = (k * jnp.exp(bl - bh)).astype(BF)
                dec = jnp.exp(bl)
                st = st_scr[h]
                st_ref[c, h] = st
                att = jnp.where(causal, _dot_nt(q_in, k_in), 0.0).astype(BF)
                o = _dot(att, v) + _dot_nt(q_in, st.astype(BF))
                st_scr[h] = st * dec + _dot_tn(v, k_st)
                rstd = lax.rsqrt(jnp.mean(o * o, axis=-1, keepdims=True) + RMS_EPS)
                o_ref[rows, h * GDV:(h + 1) * GDV] = o
                zg_ref[rows, h * GDV:(h + 1) * GDV] = (o * rstd * g * (og * _sigmoid(og))).astype(BF)
            return carry

        lax.fori_loop(0, nc, chunk, 0)

    full = lambda shp: pl.BlockSpec(shp, lambda b_, i: (0,) * len(shp))
    return pl.pallas_call(
        body, name="gla_fwd",
        out_shape=(jax.ShapeDtypeStruct((T, GH * GDV), F32),
                   jax.ShapeDtypeStruct((T, GH * GDV), BF),
                   jax.ShapeDtypeStruct((T // GC, GH, GDV, GDK), F32)),
        grid=(nseq, nb),
        in_specs=[pl.BlockSpec((tm, PG_W), lambda b_, i: (b_ * nb + i, 0)),
                  full((128, 512)), full((1, 512)), full((1, GDV)), full((GC, GC))],
        out_specs=(pl.BlockSpec((tm, GH * GDV), lambda b_, i: (b_ * nb + i, 0)),
                   pl.BlockSpec((tm, GH * GDV), lambda b_, i: (b_ * nb + i, 0)),
                   pl.BlockSpec((nc, GH, GDV, GDK), lambda b_, i: (b_ * nb + i, 0, 0, 0))),
        scratch_shapes=[pltpu.VMEM((GH, GDV, GDK), F32)],
        compiler_params=_params(("parallel", "arbitrary")),
    )(pg, wg, bg, gn, ltri)


def _gla_bwd(pg, wg, bg, gn, ltri, utri, o, states, dzg, *, nseq, S, tm):
    T = pg.shape[0]
    nb, nc = S // tm, tm // GC
    qscale = GDK ** -0.5

    def body(pg_ref, wg_ref, bg_ref, gn_ref, l_ref, u_ref, o_ref, st_ref, dzg_ref,
             dpg_ref, dwg_ref, dbg_ref, dgn_ref, dst_scr):
        first = jnp.logical_and(pl.program_id(0) == 0, pl.program_id(1) == 0)

        @pl.when(first)
        def _():
            dwg_ref[...] = jnp.zeros_like(dwg_ref)
            dbg_ref[...] = jnp.zeros_like(dbg_ref)
            dgn_ref[...] = jnp.zeros_like(dgn_ref)

        @pl.when(pl.program_id(1) == 0)
        def _():
            dst_scr[...] = jnp.zeros_like(dst_scr)

        ltri_v = l_ref[...]
        utri_v = u_ref[...]
        causal = _iota((GC, GC), 0) >= _iota((GC, GC), 1)
        last_row = _iota((GC, GDK), 0) == GC - 1
        g = gn_ref[...]

        def chunk(cc, carry):
            c = nc - 1 - cc
            rows = pl.ds(pl.multiple_of(c * GC, GC), GC)
            r, logit, la = _gla_gate(pg_ref, rows, wg_ref, bg_ref)
            b = _tri_mm(ltri_v, la)
            dbs = []
            for h in range(GH):
                q = pg_ref[rows, h * GDK:(h + 1) * GDK]
                k = pg_ref[rows, 512 + h * GDK:512 + (h + 1) * GDK]
                vb = pg_ref[rows, 1024 + h * GDV:1024 + (h + 1) * GDV].astype(BF)
                og = pg_ref[rows, 2048 + h * GDV:2048 + (h + 1) * GDV]
                oh = o_ref[rows, h * GDV:(h + 1) * GDV]
                dz = dzg_ref[rows, h * GDV:(h + 1) * GDV].astype(F32)
                bh = b[:, h * GDK:(h + 1) * GDK]
                bl = jnp.sum(jnp.where(last_row, bh, 0.0), axis=0, keepdims=True)
                eb = qscale * jnp.exp(bh)
                enb = jnp.exp(-bh)
                ek = jnp.exp(bl - bh)
                dec = jnp.exp(bl)
                q_in = q * eb
                k_in = k * enb
                k_st = k * ek
                q_inb, k_inb, k_stb = q_in.astype(BF), k_in.astype(BF), k_st.astype(BF)
                st = st_ref[c, h]
                dst = dst_scr[h]
                rstd = lax.rsqrt(jnp.mean(oh * oh, axis=-1, keepdims=True) + RMS_EPS)
                ohat = oh * rstd
                sg = _sigmoid(og)
                don = dz * (og * sg)
                dog = dz * (ohat * g) * (sg * (1.0 + og * (1.0 - sg)))
                dgn_ref[...] += jnp.sum(don * ohat, axis=0, keepdims=True)
                gd = don * g
                do = rstd * (gd - ohat * jnp.mean(gd * ohat, axis=-1, keepdims=True))
                dob = do.astype(BF)
                att = jnp.where(causal, _dot_nt(q_inb, k_inb), 0.0).astype(BF)
                da = jnp.where(causal, _dot_nt(dob, vb), 0.0).astype(BF)
                dstb = dst.astype(BF)
                dqi = _dot(da, k_inb) + _dot(dob, st.astype(BF))
                dki = _dot_tn(da, q_inb)
                dv = _dot_tn(att, dob) + _dot_nt(k_stb, dstb)
                dks = _dot(vb, dstb)
                dd = jnp.sum(dst * st, axis=0, keepdims=True)
                dst_scr[h] = dst * dec + _dot_tn(dob, q_inb)
                dq = dqi * eb
                dk = dki * enb + dks * ek
                kk = dks * k_st
                dbl = jnp.sum(kk, axis=0, keepdims=True) + dd * dec
                db = dqi * q_in - dki * k_in - kk
                dbs.append(db + jnp.where(last_row, dbl, 0.0))
                dpg_ref[rows, h * GDK:(h + 1) * GDK] = dq.astype(BF)
                dpg_ref[rows, 512 + h * GDK:512 + (h + 1) * GDK] = dk.astype(BF)
                dpg_ref[rows, 1024 + h * GDV:1024 + (h + 1) * GDV] = dv.astype(BF)
                dpg_ref[rows, 2048 + h * GDV:2048 + (h + 1) * GDV] = dog.astype(BF)
            dla = _tri_mm(utri_v, jnp.concatenate(dbs, axis=1))
            dlogit = dla * (1.0 / GTAU) * _sigmoid(-logit)
            dlb = dlogit.astype(BF)
            dpg_ref[rows, 3072:3200] = _dot_nt(dlb, wg_ref[...]).astype(BF)
            dwg_ref[...] += _dot_tn(r, dlb)
            dbg_ref[...] += jnp.sum(dlogit, axis=0, keepdims=True)
            return carry

        lax.fori_loop(0, nc, chunk, 0)

    full = lambda shp: pl.BlockSpec(shp, lambda b_, i: (0,) * len(shp))
    rev = lambda b_, i: (b_ * nb + nb - 1 - i, 0)
    return pl.pallas_call(
        body, name="gla_bwd",
        out_shape=(jax.ShapeDtypeStruct((T, PG_W), BF),
                   jax.ShapeDtypeStruct((128, 512), F32),
                   jax.ShapeDtypeStruct((1, 512), F32),
                   jax.ShapeDtypeStruct((1, GDV), F32)),
        grid=(nseq, nb),
        in_specs=[pl.BlockSpec((tm, PG_W), rev),
                  full((128, 512)), full((1, 512)), full((1, GDV)), full((GC, GC)), full((GC, GC)),
                  pl.BlockSpec((tm, GH * GDV), rev),
                  pl.BlockSpec((nc, GH, GDV, GDK), lambda b_, i: (b_ * nb + nb - 1 - i, 0, 0, 0)),
                  pl.BlockSpec((tm, GH * GDV), rev)],
        out_specs=(pl.BlockSpec((tm, PG_W), rev), full((128, 512)), full((1, 512)), full((1, GDV))),
        scratch_shapes=[pltpu.VMEM((GH, GDV, GDK), F32)],
        compiler_params=_params(("arbitrary", "arbitrary")),
    )(pg, wg, bg, gn, ltri, utri, o, states, dzg)


def _rope_tables(pos, invf):
    ang = pos.astype(F32) * invf
    lane = _iota(ang.shape, 1)
    sin = jnp.sin(ang)
    ssin = jnp.where(lane < 32, -sin, jnp.where(lane < 64, sin, 0.0))
    return jnp.cos(ang), ssin, lane


def _rope(x, cos, ssin, lane, sign):
    rot = jnp.where(lane < 32, pltpu.roll(x, 96, 1), pltpu.roll(x, 32, 1))
    return x * cos + sign * (rot * ssin)


def _rms_fwd(x, g):
    rstd = lax.rsqrt(jnp.mean(x * x, axis=-1, keepdims=True) + RMS_EPS)
    return x * rstd * g, x * rstd, rstd


def _rms_bwd(dy, xhat, rstd, g):
    gd = dy * g
    return rstd * (gd - xhat * jnp.mean(gd * xhat, axis=-1, keepdims=True)), jnp.sum(dy * xhat, axis=0, keepdims=True)


def _mla_prep_fwd(pm, pos, invf, gq, gkv, wuq, wukv, *, tm):
    T = pm.shape[0]

    def body(pm_ref, pos_ref, invf_ref, gq_ref, gkv_ref, wuq_ref, wukv_ref, qc_ref, kc_ref, v_ref):
        cos, ssin, lane = _rope_tables(pos_ref[...], invf_ref[...])
        cq, _, _ = _rms_fwd(pm_ref[:, 0:MQR], gq_ref[...])
        ckv, _, _ = _rms_fwd(pm_ref[:, 512:768], gkv_ref[...])
        qf = _dot(cq.astype(BF), wuq_ref[...])
        kvf = _dot(ckv.astype(BF), wukv_ref[...])
        kr = _rope(pm_ref[:, 384:512], cos, ssin, lane, 1.0).astype(BF)
        for h in range(MH):
            qc_ref[:, 256 * h:256 * h + 128] = qf[:, 128 * h:128 * h + 128].astype(BF)
            qr = qf[:, 1024 + 128 * h:1024 + 128 * h + 128]
            qc_ref[:, 256 * h + 128:256 * h + 256] = _rope(qr, cos, ssin, lane, 1.0).astype(BF)
            kc_ref[:, 256 * h:256 * h + 128] = kvf[:, 128 * h:128 * h + 128].astype(BF)
            kc_ref[:, 256 * h + 128:256 * h + 256] = kr
        v_ref[...] = kvf[:, 1024:2048].astype(BF)

    full = lambda shp: pl.BlockSpec(shp, lambda i: (0,) * len(shp))
    row = lambda w: pl.BlockSpec((tm, w), lambda i: (i, 0))
    return pl.pallas_call(
        body, name="mla_prep_fwd",
        out_shape=(jax.ShapeDtypeStruct((T, MH * 256), BF), jax.ShapeDtypeStruct((T, MH * 256), BF),
                   jax.ShapeDtypeStruct((T, MH * MV), BF)),
        grid=(T // tm,),
        in_specs=[row(PM_W), row(1), full((1, 128)), full((1, MQR)), full((1, MKR)),
                  full((MQR, 2048)), full((MKR, 2048))],
        out_specs=(row(MH * 256), row(MH * 256), row(MH * MV)),
        compiler_params=_params(("parallel",)),
    )(pm, pos, invf, gq, gkv, wuq, wukv)


def _mla_prep_bwd(pm, pos, invf, gq, gkv, wuq, wukv, dqc, dkc, dv, *, tm):
    T = pm.shape[0]

    def body(pm_ref, pos_ref, invf_ref, gq_ref, gkv_ref, wuq_ref, wukv_ref, dqc_ref, dkc_ref, dv_ref,
             dpm_ref, dwuq_ref, dwukv_ref, dgq_ref, dgkv_ref):
        @pl.when(pl.program_id(0) == 0)
        def _():
            dwuq_ref[...] = jnp.zeros_like(dwuq_ref)
            dwukv_ref[...] = jnp.zeros_like(dwukv_ref)
            dgq_ref[...] = jnp.zeros_like(dgq_ref)
            dgkv_ref[...] = jnp.zeros_like(dgkv_ref)

        cos, ssin, lane = _rope_tables(pos_ref[...], invf_ref[...])
        cq, cqh, cq_rstd = _rms_fwd(pm_ref[:, 0:MQR], gq_ref[...])
        ckv, ckvh, ckv_rstd = _rms_fwd(pm_ref[:, 512:768], gkv_ref[...])
        dqn, dqr, dkn = [], [], []
        dkr = jnp.zeros((tm, 128), F32)
        for h in range(MH):
            dqn.append(dqc_ref[:, 256 * h:256 * h + 128].astype(BF))
            dqr.append(_rope(dqc_ref[:, 256 * h + 128:256 * h + 256], cos, ssin, lane, -1.0).astype(BF))
            dkn.append(dkc_ref[:, 256 * h:256 * h + 128].astype(BF))
            dkr = dkr + dkc_ref[:, 256 * h + 128:256 * h + 256]
        dqf = jnp.concatenate(dqn + dqr, axis=1)
        dkvf = jnp.concatenate(dkn + [dv_ref[...].astype(BF)], axis=1)
        dwuq_ref[...] += _dot_tn(cq.astype(BF), dqf)
        dwukv_ref[...] += _dot_tn(ckv.astype(BF), dkvf)
        dcq, dgq = _rms_bwd(_dot_nt(dqf, wuq_ref[...]), cqh, cq_rstd, gq_ref[...])
        dckv, dgkv = _rms_bwd(_dot_nt(dkvf, wukv_ref[...]), ckvh, ckv_rstd, gkv_ref[...])
        dgq_ref[...] += dgq
        dgkv_ref[...] += dgkv
        dpm_ref[:, 0:MQR] = dcq.astype(BF)
        dpm_ref[:, 384:512] = _rope(dkr, cos, ssin, lane, -1.0).astype(BF)
        dpm_ref[:, 512:768] = dckv.astype(BF)

    full = lambda shp: pl.BlockSpec(shp, lambda i: (0,) * len(shp))
    row = lambda w: pl.BlockSpec((tm, w), lambda i: (i, 0))
    return pl.pallas_call(
        body, name="mla_prep_bwd",
        out_shape=(jax.ShapeDtypeStruct((T, PM_W), BF), jax.ShapeDtypeStruct((MQR, 2048), F32),
                   jax.ShapeDtypeStruct((MKR, 2048), F32), jax.ShapeDtypeStruct((1, MQR), F32),
                   jax.ShapeDtypeStruct((1, MKR), F32)),
        grid=(T // tm,),
        in_specs=[row(PM_W), row(1), full((1, 128)), full((1, MQR)), full((1, MKR)),
                  full((MQR, 2048)), full((MKR, 2048)), row(MH * 256), row(MH * 256), row(MH * MV)],
        out_specs=(row(PM_W), full((MQR, 2048)), full((MKR, 2048)), full((1, MQR)), full((1, MKR))),
        compiler_params=_params(("arbitrary",)),
    )(pm, pos, invf, gq, gkv, wuq, wukv, dqc, dkc, dv)


def _flash_fwd(qc, kc, v, *, nseq, S, tq):
    T = qc.shape[0]
    nq = S // tq
    scale = (NOPE + ROPE) ** -0.5

    def body(q_ref, k_ref, v_ref, o_ref, lse_ref):
        i = pl.program_id(2)
        q = q_ref[...]
        causal = _iota((tq, tq), 0) >= _iota((tq, tq), 1)

        def step(j, carry, masked):
            m, l, acc = carry
            rows = pl.ds(pl.multiple_of(j * tq, tq), tq)
            s = _dot_nt(q, k_ref[rows, :]) * scale
            if masked:
                s = jnp.where(causal, s, NEG)
            m_new = jnp.maximum(m, jnp.max(s, axis=-1, keepdims=True))
            p = jnp.exp(s - m_new)
            a = jnp.exp(m - m_new)
            l = a * l + jnp.sum(p, axis=-1, keepdims=True)
            acc = a * acc + _dot(p.astype(BF), v_ref[rows, :])
            return m_new, l, acc

        init = (jnp.full((tq, 1), NEG, F32), jnp.zeros((tq, 1), F32), jnp.zeros((tq, MV), F32))
        carry = lax.fori_loop(0, i, lambda j, c: step(j, c, False), init)
        m, l, acc = step(i, carry, True)
        o_ref[...] = (acc / l).astype(BF)
        lse_ref[...] = jnp.broadcast_to(m + jnp.log(l), (tq, 128))

    return pl.pallas_call(
        body, name="flash_fwd",
        out_shape=(jax.ShapeDtypeStruct((T, MH * MV), BF), jax.ShapeDtypeStruct((T, MH * 128), F32)),
        grid=(nseq, MH, nq),
        in_specs=[pl.BlockSpec((tq, 256), lambda b_, h, i: (b_ * nq + i, h)),
                  pl.BlockSpec((S, 256), lambda b_, h, i: (b_, h)),
                  pl.BlockSpec((S, MV), lambda b_, h, i: (b_, h))],
        out_specs=(pl.BlockSpec((tq, MV), lambda b_, h, i: (b_ * nq + i, h)),
                   pl.BlockSpec((tq, 128), lambda b_, h, i: (b_ * nq + i, h))),
        compiler_params=_params(("parallel", "parallel", "arbitrary")),
    )(qc, kc, v)


def _flash_bwd(qc, kc, v, o, do, lse, *, nseq, S, tq):
    T = qc.shape[0]
    nq = S // tq
    scale = (NOPE + ROPE) ** -0.5

    def body(q_ref, k_ref, v_ref, o_ref, do_ref, lse_ref, dq_ref, dk_ref, dv_ref, dq_scr, delta_scr):
        j = pl.program_id(2)

        @pl.when(j == 0)
        def _():
            dq_scr[...] = jnp.zeros_like(dq_scr)
            delta = jnp.sum(o_ref[...].astype(F32) * do_ref[...].astype(F32), axis=-1, keepdims=True)
            delta_scr[...] = jnp.broadcast_to(delta, (S, 128))

        kb = k_ref[...]
        vb = v_ref[...]
        causal = _iota((tq, tq), 0) >= _iota((tq, tq), 1)

        def step(i, carry, masked):
            dk, dv = carry
            rows = pl.ds(pl.multiple_of(i * tq, tq), tq)
            q = q_ref[rows, :]
            dob = do_ref[rows, :]
            s = _dot_nt(q, kb) * scale
            p = jnp.exp(s - lse_ref[rows, 0:1])
            if masked:
                p = jnp.where(causal, p, 0.0)
            dv = dv + _dot_tn(p.astype(BF), dob)
            dp = _dot_nt(dob, vb)
            ds = (p * (dp - delta_scr[rows, 0:1]) * scale).astype(BF)
            dk = dk + _dot_tn(ds, q)
            dq_scr[rows, :] += _dot(ds, kb)
            return dk, dv

        carry = step(j, (jnp.zeros((tq, 256), F32), jnp.zeros((tq, MV), F32)), True)
        dk, dv = lax.fori_loop(j + 1, nq, lambda i, c: step(i, c, False), carry)
        dk_ref[...] = dk
        dv_ref[...] = dv

        @pl.when(j == nq - 1)
        def _():
            dq_ref[...] = dq_scr[...]

    seq = lambda w: pl.BlockSpec((S, w), lambda b_, h, j: (b_, h))
    blk = lambda w: pl.BlockSpec((tq, w), lambda b_, h, j: (b_ * nq + j, h))
    return pl.pallas_call(
        body, name="flash_bwd",
        out_shape=(jax.ShapeDtypeStruct((T, MH * 256), F32), jax.ShapeDtypeStruct((T, MH * 256), F32),
                   jax.ShapeDtypeStruct((T, MH * MV), F32)),
        grid=(nseq, MH, nq),
        in_specs=[seq(256), blk(256), blk(MV), seq(MV), seq(MV), seq(128)],
        out_specs=(seq(256), blk(256), blk(MV)),
        scratch_shapes=[pltpu.VMEM((S, 256), F32), pltpu.VMEM((S, 128), F32)],
        compiler_params=_params(("parallel", "parallel", "arbitrary")),
    )(qc, kc, v, o, do, lse)


def _ln_fwd(pre, g, b):
    mu = jnp.mean(pre, axis=-1, keepdims=True)
    xc = pre - mu
    rstd = lax.rsqrt(jnp.mean(xc * xc, axis=-1, keepdims=True) + LN_EPS)
    xhat = xc * rstd
    return xhat * g + b, xhat, rstd


def _ln_bwd(dy, xhat, rstd, g):
    dxh = dy * g
    dx = rstd * (dxh - jnp.mean(dxh, axis=-1, keepdims=True) - xhat * jnp.mean(dxh * xhat, axis=-1, keepdims=True))
    return dx, jnp.sum(dy * xhat, axis=0, keepdims=True), jnp.sum(dy, axis=0, keepdims=True)


def _post_attn_fwd(zg, attn, pt, x, wgo, wmo, wout, g1, b1, *, tm):
    T = x.shape[0]

    def body(zg_ref, at_ref, pt_ref, x_ref, wgo_ref, wmo_ref, wout_ref, g_ref, b_ref,
             yg_ref, ym_ref, mix_ref, pre_ref, h_ref, hb_ref):
        yg = _dot(zg_ref[...], wgo_ref[...])
        ym = _dot(at_ref[...], wmo_ref[...])
        mix = (_sigmoid(pt_ref[:, 0:D]) * yg + _sigmoid(pt_ref[:, D:2 * D]) * ym).astype(BF)
        pre = ALPHA * x_ref[...] + _dot(mix, wout_ref[...])
        h, _, _ = _ln_fwd(pre, g_ref[...], b_ref[...])
        yg_ref[...] = yg
        ym_ref[...] = ym
        mix_ref[...] = mix
        pre_ref[...] = pre
        h_ref[...] = h
        hb_ref[...] = h.astype(BF)

    full = lambda shp: pl.BlockSpec(shp, lambda i: (0,) * len(shp))
    row = lambda w: pl.BlockSpec((tm, w), lambda i: (i, 0))
    sd = lambda dt: jax.ShapeDtypeStruct((T, D), dt)
    return pl.pallas_call(
        body, name="post_attn_fwd",
        out_shape=(sd(F32), sd(F32), sd(BF), sd(F32), sd(F32), sd(BF)),
        grid=(T // tm,),
        in_specs=[row(D), row(D), row(PT_W), row(D), full((D, D)), full((D, D)), full((D, D)),
                  full((1, D)), full((1, D))],
        out_specs=(row(D),) * 6,
        compiler_params=_params(("parallel",)),
    )(zg, attn, pt, x, wgo, wmo, wout, g1, b1)


def _post_attn_bwd(dh, pre, pt, yg, ym, wgo, wmo, wout, g1, *, tm):
    T = dh.shape[0]

    def body(dh_ref, pre_ref, pt_ref, yg_ref, ym_ref, wgo_ref, wmo_ref, wout_ref, g_ref,
             dx_ref, dpreb_ref, dpt_ref, dygb_ref, dymb_ref, dzg_ref, dat_ref, dg_ref, db_ref):
        @pl.when(pl.program_id(0) == 0)
        def _():
            dg_ref[...] = jnp.zeros_like(dg_ref)
            db_ref[...] = jnp.zeros_like(db_ref)

        pre = pre_ref[...]
        mu = jnp.mean(pre, axis=-1, keepdims=True)
        xc = pre - mu
        rstd = lax.rsqrt(jnp.mean(xc * xc, axis=-1, keepdims=True) + LN_EPS)
        dpre, dg, db = _ln_bwd(dh_ref[...], xc * rstd, rstd, g_ref[...])
        dg_ref[...] += dg
        db_ref[...] += db
        dx_ref[...] = ALPHA * dpre
        dpreb = dpre.astype(BF)
        dpreb_ref[...] = dpreb
        dmix = _dot_nt(dpreb, wout_ref[...])
        sa = _sigmoid(pt_ref[:, 0:D])
        sb = _sigmoid(pt_ref[:, D:2 * D])
        dpt_ref[:, 0:D] = (dmix * yg_ref[...] * (sa * (1.0 - sa))).astype(BF)
        dpt_ref[:, D:2 * D] = (dmix * ym_ref[...] * (sb * (1.0 - sb))).astype(BF)
        dyg = (dmix * sa).astype(BF)
        dym = (dmix * sb).astype(BF)
        dygb_ref[...] = dyg
        dymb_ref[...] = dym
        dzg_ref[...] = _dot_nt(dyg, wgo_ref[...]).astype(BF)
        dat_ref[...] = _dot_nt(dym, wmo_ref[...]).astype(BF)

    full = lambda shp: pl.BlockSpec(shp, lambda i: (0,) * len(shp))
    row = lambda w: pl.BlockSpec((tm, w), lambda i: (i, 0))
    sd = lambda w, dt: jax.ShapeDtypeStruct((T, w), dt)
    return pl.pallas_call(
        body, name="post_attn_bwd",
        out_shape=(sd(D, F32), sd(D, BF), sd(PT_W, BF), sd(D, BF), sd(D, BF), sd(D, BF), sd(D, BF),
                   jax.ShapeDtypeStruct((1, D), F32), jax.ShapeDtypeStruct((1, D), F32)),
        grid=(T // tm,),
        in_specs=[row(D), row(D), row(PT_W), row(D), row(D), full((D, D)), full((D, D)), full((D, D)),
                  full((1, D))],
        out_specs=(row(D), row(D), row(PT_W), row(D), row(D), row(D), row(D), full((1, D)), full((1, D))),
        compiler_params=_params(("arbitrary",)),
    )(dh, pre, pt, yg, ym, wgo, wmo, wout, g1)


def _shift_down(u, prev, k):
    r = pltpu.roll(u, k, 0)
    p = pltpu.roll(prev, k, 0)
    head = jnp.where(_iota(p.shape, 0) < k, p, r[0:8, :])
    return jnp.concatenate([head, r[8:, :]], axis=0)


def _shift_up(u, nxt, k):
    n = u.shape[0]
    r = pltpu.roll(u, n - k, 0)
    p = pltpu.roll(nxt, 8 - k, 0)
    tail = jnp.where(_iota(p.shape, 0) >= 8 - k, p, r[n - 8:, :])
    return jnp.concatenate([r[:n - 8, :], tail], axis=0)


def _conv3(u, prev, w_ref, b_ref):
    return (w_ref[0:1, :] * _shift_down(u, prev, 2) + w_ref[1:2, :] * _shift_down(u, prev, 1)
            + w_ref[2:3, :] * u + b_ref[...])


def _ffn_up_fwd(hb, wug, wuv, cw, cb, *, S, tm, tn):
    T = hb.shape[0]
    nj, nbs = DFF // tn, S // tm

    def body(h_ref, wg_ref, wv_ref, cwg_ref, cwv_ref, cbg_ref, cbv_ref, ug_ref, uv_ref, f_ref, pg_scr, pv_scr):
        @pl.when(pl.program_id(1) % nbs == 0)
        def _():
            pg_scr[...] = jnp.zeros_like(pg_scr)
            pv_scr[...] = jnp.zeros_like(pv_scr)

        h = h_ref[...]
        ug = _dot(h, wg_ref[...])
        uv = _dot(h, wv_ref[...])
        ucg = _conv3(ug, pg_scr[...], cwg_ref, cbg_ref)
        ucv = _conv3(uv, pv_scr[...], cwv_ref, cbv_ref)
        pg_scr[...] = ug[tm - 8:, :]
        pv_scr[...] = uv[tm - 8:, :]
        ug_ref[...] = ug
        uv_ref[...] = uv
        f_ref[...] = (ucg * _sigmoid(ucg) * ucv).astype(BF)

    tile = pl.BlockSpec((tm, tn), lambda j, i: (i, j))
    return pl.pallas_call(
        body, name="ffn_up_fwd",
        out_shape=(jax.ShapeDtypeStruct((T, DFF), F32), jax.ShapeDtypeStruct((T, DFF), F32),
                   jax.ShapeDtypeStruct((T, DFF), BF)),
        grid=(nj, T // tm),
        in_specs=[pl.BlockSpec((tm, D), lambda j, i: (i, 0)),
                  pl.BlockSpec((D, tn), lambda j, i: (0, j)), pl.BlockSpec((D, tn), lambda j, i: (0, j)),
                  pl.BlockSpec((3, tn), lambda j, i: (0, j)), pl.BlockSpec((3, tn), lambda j, i: (0, j + nj)),
                  pl.BlockSpec((1, tn), lambda j, i: (0, j)), pl.BlockSpec((1, tn), lambda j, i: (0, j + nj))],
        out_specs=(tile, tile, tile),
        scratch_shapes=[pltpu.VMEM((8, tn), F32), pltpu.VMEM((8, tn), F32)],
        compiler_params=_params(("parallel", "arbitrary")),
    )(hb, wug, wuv, cw, cw, cb, cb)


def _ffn_bwd(dpreb, wd, ug, uv, cw, cb, *, S, tm, tn):
    T = dpreb.shape[0]
    nj, nb, nbs = DFF // tn, T // tm, S // tm
    hb8 = tm // 8

    def body(dp_ref, wd_ref, ug_ref, uv_ref, hg_ref, hv_ref, cwg_ref, cwv_ref, cbg_ref, cbv_ref,
             dug_ref, duv_ref, dcg_ref, dcv_ref, ng_scr, nv_scr):
        ii = pl.program_id(1)
        i = nb - 1 - ii

        @pl.when(ii == 0)
        def _():
            dcg_ref[...] = jnp.zeros_like(dcg_ref)
            dcv_ref[...] = jnp.zeros_like(dcv_ref)

        @pl.when(i % nbs == nbs - 1)
        def _():
            ng_scr[...] = jnp.zeros_like(ng_scr)
            nv_scr[...] = jnp.zeros_like(nv_scr)

        seq_start = i % nbs == 0
        df = _dot_nt(dp_ref[...], wd_ref[...])

        def half(u_ref, halo_ref, cw_ref, cb_ref):
            u = u_ref[...]
            prev = jnp.where(seq_start, 0.0, halo_ref[...])
            u1 = _shift_down(u, prev, 1)
            u2 = _shift_down(u, prev, 2)
            return u, u1, u2, cw_ref, cw_ref[0:1, :] * u2 + cw_ref[1:2, :] * u1 + cw_ref[2:3, :] * u + cb_ref[...]

        g_u, g_u1, g_u2, g_w, ucg = half(ug_ref, hg_ref, cwg_ref, cbg_ref)
        v_u, v_u1, v_u2, v_w, ucv = half(uv_ref, hv_ref, cwv_ref, cbv_ref)
        sg = _sigmoid(ucg)
        ducg = df * ucv * (sg * (1.0 + ucg * (1.0 - sg)))
        ducv = df * (ucg * sg)

        def finish(duc, u, u1, u2, w, nxt_scr, du_ref, dc_ref):
            nxt = nxt_scr[...]
            du = w[2:3, :] * duc + w[1:2, :] * _shift_up(duc, nxt, 1) + w[0:1, :] * _shift_up(duc, nxt, 2)
            du_ref[...] = du.astype(BF)
            nxt_scr[...] = duc[0:8, :]
            for row, z in enumerate((u2 * duc, u1 * duc, u * duc, duc)):
                dc_ref[row:row + 1, :] += jnp.sum(z, axis=0, keepdims=True)

        finish(ducg, g_u, g_u1, g_u2, g_w, ng_scr, dug_ref, dcg_ref)
        finish(ducv, v_u, v_u1, v_u2, v_w, nv_scr, duv_ref, dcv_ref)

    tile = pl.BlockSpec((tm, tn), lambda j, ii: (nb - 1 - ii, j))
    halo = pl.BlockSpec((8, tn), lambda j, ii: (jnp.maximum((nb - 1 - ii) * hb8 - 1, 0), j))
    acc = pl.BlockSpec((8, tn), lambda j, ii: (0, j))
    return pl.pallas_call(
        body, name="ffn_bwd",
        out_shape=(jax.ShapeDtypeStruct((T, DFF), BF), jax.ShapeDtypeStruct((T, DFF), BF),
                   jax.ShapeDtypeStruct((8, DFF), F32), jax.ShapeDtypeStruct((8, DFF), F32)),
        grid=(nj, nb),
        in_specs=[pl.BlockSpec((tm, D), lambda j, ii: (nb - 1 - ii, 0)),
                  pl.BlockSpec((tn, D), lambda j, ii: (j, 0)),
                  tile, tile, halo, halo,
                  pl.BlockSpec((3, tn), lambda j, ii: (0, j)), pl.BlockSpec((3, tn), lambda j, ii: (0, j + nj)),
                  pl.BlockSpec((1, tn), lambda j, ii: (0, j)), pl.BlockSpec((1, tn), lambda j, ii: (0, j + nj))],
        out_specs=(tile, tile, acc, acc),
        scratch_shapes=[pltpu.VMEM((8, tn), F32), pltpu.VMEM((8, tn), F32)],
        compiler_params=_params(("parallel", "arbitrary")),
    )(dpreb, wd, ug, uv, ug, uv, cw, cw, cb, cb)


def _down_ln2_loss(f_in, wd, h, target, g2, b2, *, tm):
    T = h.shape[0]

    def body(f_ref, wd_ref, h_ref, t_ref, g_ref, b_ref, dpb_ref, dh_ref, loss_ref, dg_ref, db_ref):
        @pl.when(pl.program_id(0) == 0)
        def _():
            loss_ref[...] = jnp.zeros_like(loss_ref)
            dg_ref[...] = jnp.zeros_like(dg_ref)
            db_ref[...] = jnp.zeros_like(db_ref)

        pre = ALPHA * h_ref[...] + _dot(f_ref[...], wd_ref[...])
        out, xhat, rstd = _ln_fwd(pre, g_ref[...], b_ref[...])
        diff = out - t_ref[...]
        loss_ref[...] += 0.5 * jnp.sum(jnp.mean(diff * diff, axis=-1, keepdims=True))
        dpre, dg, db = _ln_bwd(diff * (1.0 / D), xhat, rstd, g_ref[...])
        dg_ref[...] += dg
        db_ref[...] += db
        dpb_ref[...] = dpre.astype(BF)
        dh_ref[...] = ALPHA * dpre

    full = lambda shp: pl.BlockSpec(shp, lambda i: (0,) * len(shp))
    row = lambda w: pl.BlockSpec((tm, w), lambda i: (i, 0))
    return pl.pallas_call(
        body, name="down_ln2_loss",
        out_shape=(jax.ShapeDtypeStruct((T, D), BF), jax.ShapeDtypeStruct((T, D), F32),
                   jax.ShapeDtypeStruct((8, 128), F32), jax.ShapeDtypeStruct((1, D), F32),
                   jax.ShapeDtypeStruct((1, D), F32)),
        grid=(T // tm,),
        in_specs=[row(DFF), full((DFF, D)), row(D), row(D), full((1, D)), full((1, D))],
        out_specs=(row(D), row(D), full((8, 128)), full((1, D)), full((1, D))),
        compiler_params=_params(("arbitrary",)),
    )(f_in, wd, h, target, g2, b2)


def _adamw(parts, w, m, v, *, name):
    n, R, C = parts.shape
    tr = R
    for cand in range(min(R, 256), 7, -1):
        if R % cand == 0 and cand % 8 == 0:
            tr = cand
            break
    c1 = 1.0 - ADAM_B1 ** ADAM_STEP
    c2 = 1.0 - ADAM_B2 ** ADAM_STEP

    def body(p_ref, w_ref, m_ref, v_ref, g_ref, d_ref, nm_ref, nv_ref):
        g = p_ref[0]
        for s in range(1, n):
            g = g + p_ref[s]
        nm = ADAM_B1 * m_ref[...] + (1.0 - ADAM_B1) * g
        nv = ADAM_B2 * v_ref[...] + (1.0 - ADAM_B2) * (g * g)
        g_ref[...] = g
        nm_ref[...] = nm
        nv_ref[...] = nv
        d_ref[...] = -ADAM_LR * ((nm / c1) / (jnp.sqrt(nv / c2) + ADAM_EPS) + ADAM_WD * w_ref[...])

    blk = pl.BlockSpec((tr, C), lambda i: (i, 0))
    sd = jax.ShapeDtypeStruct((R, C), F32)
    return pl.pallas_call(
        body, name=name,
        out_shape=(sd, sd, sd, sd),
        grid=(R // tr,),
        in_specs=[pl.BlockSpec((n, tr, C), lambda i: (0, i, 0)), blk, blk, blk],
        out_specs=(blk, blk, blk, blk),
        compiler_params=_params(("parallel",)),
    )(parts, w, m, v)


def _exchange(arrays, *, scatter, name):
    n = len(arrays)
    if scatter:
        out_shape = tuple(jax.ShapeDtypeStruct(a.shape, a.dtype) for a in arrays)
    else:
        out_shape = tuple(jax.ShapeDtypeStruct((NDEV,) + a.shape, a.dtype) for a in arrays)

    def body(*refs):
        ins, outs = refs[:n], refs[n:2 * n]
        send_sems, recv_sems, loc_sems = refs[2 * n:]
        x, y, c = lax.axis_index("x"), lax.axis_index("y"), lax.axis_index("c")
        me = 4 * x + 2 * y + c
        flip = lambda p, d: 1 - p if d else p
        started = []
        for a in range(n):
            loc = pltpu.make_async_copy(ins[a].at[me] if scatter else ins[a], outs[a].at[me], loc_sems.at[a])
            loc.start()
            started.append(loc)
        for k in range(1, NDEV):
            px, py, pc = flip(x, k & 4), flip(y, k & 2), flip(c, k & 1)
            peer = 4 * px + 2 * py + pc
            for a in range(n):
                cp = pltpu.make_async_remote_copy(
                    src_ref=ins[a].at[peer] if scatter else ins[a], dst_ref=outs[a].at[me],
                    send_sem=send_sems.at[a, k - 1], recv_sem=recv_sems.at[a, k - 1],
                    device_id=(px, py, pc), device_id_type=MESH_ID)
                cp.start()
                started.append(cp)
        for k in range(1, NDEV):
            px, py, pc = flip(x, k & 4), flip(y, k & 2), flip(c, k & 1)
            peer = 4 * px + 2 * py + pc
            for a in range(n):
                pltpu.make_async_remote_copy(
                    src_ref=ins[a].at[peer] if scatter else ins[a], dst_ref=outs[a].at[peer],
                    send_sem=send_sems.at[a, k - 1], recv_sem=recv_sems.at[a, k - 1],
                    device_id=(px, py, pc), device_id_type=MESH_ID).wait_recv()
        for cp in started[n:]:
            cp.wait_send()
        for loc in started[:n]:
            loc.wait()

    anyspec = pl.BlockSpec(memory_space=pl.ANY)
    return pl.pallas_call(
        body, name=name,
        out_shape=out_shape,
        in_specs=[anyspec] * n,
        out_specs=tuple([anyspec] * n),
        scratch_shapes=[pltpu.SemaphoreType.DMA((n, NDEV - 1)), pltpu.SemaphoreType.DMA((n, NDEV - 1)),
                        pltpu.SemaphoreType.DMA((n,))],
    )(*arrays)


def _tri_consts():
    r = lax.broadcasted_iota(jnp.int32, (GC, GC), 0)
    c = lax.broadcasted_iota(jnp.int32, (GC, GC), 1)
    return (r >= c).astype(BF), (r <= c).astype(BF)


def _local_step(x, positions, target, w):
    nseq, S, _ = x.shape
    T = nseq * S
    tm = min(256, S)
    tq = min(512, S)
    x2 = x.reshape(T, D)
    xb = x2.astype(BF)
    pos = positions.reshape(T, 1)
    half = ROPE // 2
    inv = THETA ** (-jnp.arange(half, dtype=F32) / half)
    invf = jnp.concatenate([inv, inv, jnp.zeros((64,), F32)]).reshape(1, 128)
    ltri, utri = _tri_consts()

    pg = _matmul(xb, w["w_g"], "nn", name="proj_g", tm=1024, tn=640, tk=1024)
    pm = _matmul(xb, w["w_m"], "nn", name="proj_m", tm=1024, tn=768, tk=1024)
    pt = _matmul(xb, w["w_t"], "nn", name="proj_t", tm=1024, tn=1024, tk=1024)
    o, zg, states = _gla_fwd(pg, w["wg"], w["bg"], w["gn"], ltri, nseq=nseq, S=S, tm=tm)
    qc, kc, v = _mla_prep_fwd(pm, pos, invf, w["gq"], w["gkv"], w["wuq"], w["wukv"], tm=tm)
    attn, lse = _flash_fwd(qc, kc, v, nseq=nseq, S=S, tq=tq)
    yg, ym, mix, pre1, h1, h1b = _post_attn_fwd(zg, attn, pt, x2, w["wgo"], w["wmo"], w["wout"],
                                                w["g1"], w["b1"], tm=tm)
    ug, uv, f_in = _ffn_up_fwd(h1b, w["wug"], w["wuv"], w["cw"], w["cb"], S=S, tm=tm, tn=1408)
    dpre2b, dh1, loss8, dg2, db2 = _down_ln2_loss(f_in, w["wd"], h1, target.reshape(T, D), w["g2"], w["b2"], tm=tm)

    dug, duv, dcg, dcv = _ffn_bwd(dpre2b, w["wd"], ug, uv, w["cw"], w["cb"], S=S, tm=tm, tn=1408)
    g = {}
    g["wd"] = _matmul(f_in, dpre2b, "tn", name="dw_down", tm=1408, tn=1024, tk=1024)
    g["wug"] = _matmul(h1b, dug, "tn", name="dw_up_g", tm=1024, tn=1408, tk=1024)
    g["wuv"] = _matmul(h1b, duv, "tn", name="dw_up_v", tm=1024, tn=1408, tk=1024)
    dh1 = _matmul(dug, w["wug"], "nt", name="dh1_g", c_in=dh1, tm=1024, tn=1024, tk=1408)
    dh1 = _matmul(duv, w["wuv"], "nt", name="dh1_v", c_in=dh1, tm=1024, tn=1024, tk=1408)
    dx, dpre1b, dpt, dygb, dymb, dzg, dattn, dg1, db1 = _post_attn_bwd(
        dh1, pre1, pt, yg, ym, w["wgo"], w["wmo"], w["wout"], w["g1"], tm=tm)
    g["wout"] = _matmul(mix, dpre1b, "tn", name="dw_out", tm=1024, tn=1024, tk=1024)
    g["wgo"] = _matmul(zg, dygb, "tn", name="dw_gla_o", tm=1024, tn=1024, tk=1024)
    g["wmo"] = _matmul(attn, dymb, "tn", name="dw_mla_o", tm=1024, tn=1024, tk=1024)
    dqc, dkc, dv = _flash_bwd(qc, kc, v, attn, dattn, lse, nseq=nseq, S=S, tq=tq)
    dpm, g["wuq"], g["wukv"], g["gq"], g["gkv"] = _mla_prep_bwd(
        pm, pos, invf, w["gq"], w["gkv"], w["wuq"], w["wukv"], dqc, dkc, dv, tm=tm)
    dpg, g["wg"], g["bg"], g["gn"] = _gla_bwd(pg, w["wg"], w["bg"], w["gn"], ltri, utri, o, states, dzg,
                                              nseq=nseq, S=S, tm=tm)
    g["w_g"] = _matmul(xb, dpg, "tn", name="dw_in_g", tm=1024, tn=640, tk=1024)
    g["w_m"] = _matmul(xb, dpm, "tn", name="dw_in_m", tm=1024, tn=768, tk=1024)
    g["w_t"] = _matmul(xb, dpt, "tn", name="dw_in_t", tm=1024, tn=1024, tk=1024)
    dx = _matmul(dpg, w["w_g"], "nt", name="dx_g", c_in=dx, tm=1024, tn=1024, tk=640)
    dx = _matmul(dpm, w["w_m"], "nt", name="dx_m", c_in=dx, tm=1024, tn=1024, tk=768)
    dx = _matmul(dpt, w["w_t"], "nt", name="dx_t", c_in=dx, tm=1024, tn=1024, tk=1024)
    g["g1"], g["b1"], g["g2"], g["b2"] = dg1, db1, dg2, db2
    g["cw"] = jnp.concatenate([dcg[0:3], dcv[0:3]], axis=1)
    g["cb"] = jnp.concatenate([dcg[3:4], dcv[3:4]], axis=1)
    return loss8[0, 0], dx.reshape(nseq, S, D), g


_IN_SPLITS = (512, 512, 1024, 16, 1024, 384, 256, 64, 1024, 1024)


def _w_in_to_groups(w_in):
    offs = [0]
    for s in _IN_SPLITS:
        offs.append(offs[-1] + s)
    q, k, v, r, og, cq, ckv, kr, ga, gb = [w_in[:, offs[i]:offs[i + 1]] for i in range(10)]
    z = lambda n: jnp.zeros((w_in.shape[0], n), w_in.dtype)
    return (jnp.concatenate([q, k, v, og, r, z(112)], axis=1),
            jnp.concatenate([cq, kr, z(64), ckv], axis=1),
            jnp.concatenate([ga, gb], axis=1))


def _groups_to_w_in(g_g, g_m, g_t):
    q, k, v, og, r = g_g[:, 0:512], g_g[:, 512:1024], g_g[:, 1024:2048], g_g[:, 2048:3072], g_g[:, 3072:3088]
    cq, kr, ckv = g_m[:, 0:384], g_m[:, 384:448], g_m[:, 512:768]
    return jnp.concatenate([q, k, v, r, og, cq, ckv, kr, g_t[:, 0:1024], g_t[:, 1024:2048]], axis=1)


def _uq_to_kernel(wuq):
    w3 = wuq.reshape(MQR, MH, NOPE + ROPE)
    rope = jnp.concatenate([w3[:, :, NOPE:], jnp.zeros((MQR, MH, 64), wuq.dtype)], axis=2)
    return jnp.concatenate([w3[:, :, :NOPE].reshape(MQR, MH * 128), rope.reshape(MQR, MH * 128)], axis=1)


def _uq_from_kernel(g):
    nope = g[:, :1024].reshape(MQR, MH, 128)
    rope = g[:, 1024:].reshape(MQR, MH, 128)[:, :, :ROPE]
    return jnp.concatenate([nope, rope], axis=2)


def _ukv_to_kernel(wukv):
    w3 = wukv.reshape(MKR, MH, NOPE + MV)
    return jnp.concatenate([w3[:, :, :NOPE].reshape(MKR, MH * 128), w3[:, :, NOPE:].reshape(MKR, MH * 128)], axis=1)


def _ukv_from_kernel(g):
    return jnp.concatenate([g[:, :1024].reshape(MKR, MH, 128), g[:, 1024:].reshape(MKR, MH, 128)], axis=2)


def _cols_gathered(a):
    return a.transpose(1, 0, 2).reshape(a.shape[1], NDEV * a.shape[2])


def _cols_scattered(a):
    R = a.shape[0]
    return a.reshape(R, NDEV, a.shape[1] // NDEV).transpose(1, 0, 2)


_SMALL = (("gla_b_gate", 512), ("gla_norm_g", 256), ("mla_q_norm_g", 384), ("mla_kv_norm_g", 256),
          ("ln1_g", 1024), ("ln1_b", 1024), ("conv_b", 5632), ("ln2_g", 1024), ("ln2_b", 1024))
_SMALL_ROWS = 88


def _pack_small(d):
    flat = jnp.concatenate([d[n].reshape(-1) for n, _ in _SMALL])
    return jnp.pad(flat, (0, _SMALL_ROWS * 128 - flat.shape[0])).reshape(_SMALL_ROWS, 128)


def _unpack_small(a):
    flat = a.reshape(-1)
    out, off = {}, 0
    for n, sz in _SMALL:
        out[n] = flat[off:off + sz].reshape(1, sz)
        off += sz
    return out


_NAMES = ['w_in', 'gla_w_gate_up', 'gla_b_gate', 'gla_norm_g', 'w_gla_o', 'mla_q_norm_g', 'mla_w_uq',
          'mla_kv_norm_g', 'mla_w_ukv', 'w_mla_o', 'w_out', 'ln1_g', 'ln1_b', 'w_up', 'conv_w', 'conv_b',
          'w_down', 'ln2_g', 'ln2_b']
_SHARDED = ['w_in', 'w_up', 'w_down', 'w_gla_o', 'w_mla_o', 'w_out', 'mla_w_uq', 'mla_w_ukv', 'gla_w_gate_up',
            'conv_w']


def kernel(x, positions, w_in, gla_w_gate_up, gla_b_gate, gla_norm_g, w_gla_o, mla_q_norm_g, mla_w_uq, mla_kv_norm_g, mla_w_ukv, w_mla_o, w_out, ln1_g, ln1_b, w_up, conv_w, conv_b, w_down, ln2_g, ln2_b, loss_target, m_w_in, m_gla_w_gate_up, m_gla_b_gate, m_gla_norm_g, m_w_gla_o, m_mla_q_norm_g, m_mla_w_uq, m_mla_kv_norm_g, m_mla_w_ukv, m_w_mla_o, m_w_out, m_ln1_g, m_ln1_b, m_w_up, m_conv_w, m_conv_b, m_w_down, m_ln2_g, m_ln2_b, v_w_in, v_gla_w_gate_up, v_gla_b_gate, v_gla_norm_g, v_w_gla_o, v_mla_q_norm_g, v_mla_w_uq, v_mla_kv_norm_g, v_mla_w_ukv, v_w_mla_o, v_w_out, v_ln1_g, v_ln1_b, v_w_up, v_conv_w, v_conv_b, v_w_down, v_ln2_g, v_ln2_b):
    W = dict(w_in=w_in, gla_w_gate_up=gla_w_gate_up, gla_b_gate=gla_b_gate, gla_norm_g=gla_norm_g, w_gla_o=w_gla_o, mla_q_norm_g=mla_q_norm_g, mla_w_uq=mla_w_uq, mla_kv_norm_g=mla_kv_norm_g, mla_w_ukv=mla_w_ukv, w_mla_o=w_mla_o, w_out=w_out, ln1_g=ln1_g, ln1_b=ln1_b, w_up=w_up, conv_w=conv_w, conv_b=conv_b, w_down=w_down, ln2_g=ln2_g, ln2_b=ln2_b)
    M = dict(w_in=m_w_in, gla_w_gate_up=m_gla_w_gate_up, gla_b_gate=m_gla_b_gate, gla_norm_g=m_gla_norm_g, w_gla_o=m_w_gla_o, mla_q_norm_g=m_mla_q_norm_g, mla_w_uq=m_mla_w_uq, mla_kv_norm_g=m_mla_kv_norm_g, mla_w_ukv=m_mla_w_ukv, w_mla_o=m_w_mla_o, w_out=m_w_out, ln1_g=m_ln1_g, ln1_b=m_ln1_b, w_up=m_w_up, conv_w=m_conv_w, conv_b=m_conv_b, w_down=m_w_down, ln2_g=m_ln2_g, ln2_b=m_ln2_b)
    V = dict(w_in=v_w_in, gla_w_gate_up=v_gla_w_gate_up, gla_b_gate=v_gla_b_gate, gla_norm_g=v_gla_norm_g, w_gla_o=v_w_gla_o, mla_q_norm_g=v_mla_q_norm_g, mla_w_uq=v_mla_w_uq, mla_kv_norm_g=v_mla_kv_norm_g, mla_w_ukv=v_mla_w_ukv, w_mla_o=v_w_mla_o, w_out=v_w_out, ln1_g=v_ln1_g, ln1_b=v_ln1_b, w_up=v_w_up, conv_w=v_conv_w, conv_b=v_conv_b, w_down=v_w_down, ln2_g=v_ln2_g, ln2_b=v_ln2_b)

    bf_names = [n for n in _SHARDED if n != 'conv_w']
    gathered = _exchange([W[n][0].astype(BF) for n in bf_names] + [W['conv_w'][0]], scatter=False, name="gather_w")
    G = dict(zip(bf_names + ['conv_w'], gathered))
    w_g, w_m, w_t = _w_in_to_groups(_cols_gathered(G['w_in']))
    w_up_full = _cols_gathered(G['w_up'])
    wg_full = _cols_gathered(G['gla_w_gate_up'])
    kw = dict(
        w_g=w_g, w_m=w_m, w_t=w_t,
        wg=jnp.pad(wg_full, ((0, 128 - GR), (0, 0))), bg=W['gla_b_gate'], gn=W['gla_norm_g'],
        gq=W['mla_q_norm_g'], gkv=W['mla_kv_norm_g'],
        wuq=_uq_to_kernel(_cols_gathered(G['mla_w_uq'])), wukv=_ukv_to_kernel(_cols_gathered(G['mla_w_ukv'])),
        wgo=G['w_gla_o'].reshape(D, D), wmo=G['w_mla_o'].reshape(D, D), wout=G['w_out'].reshape(D, D),
        g1=W['ln1_g'], b1=W['ln1_b'], g2=W['ln2_g'], b2=W['ln2_b'],
        wug=w_up_full[:, :DFF], wuv=w_up_full[:, DFF:], wd=G['w_down'].reshape(DFF, D),
        cw=_cols_gathered(G['conv_w']), cb=W['conv_b'],
    )

    loss_local, grad_x, g = _local_step(x, positions, loss_target, kw)
    loss = lax.psum(loss_local, ("x", "y", "c"))

    slabs = {
        'w_in': _cols_scattered(_groups_to_w_in(g['w_g'], g['w_m'], g['w_t'])),
        'w_up': _cols_scattered(jnp.concatenate([g['wug'], g['wuv']], axis=1)),
        'w_down': g['wd'].reshape(NDEV, DFF // NDEV, D),
        'w_gla_o': g['wgo'].reshape(NDEV, D // NDEV, D),
        'w_mla_o': g['wmo'].reshape(NDEV, D // NDEV, D),
        'w_out': g['wout'].reshape(NDEV, D // NDEV, D),
        'mla_w_uq': _uq_from_kernel(g['wuq']).transpose(1, 0, 2),
        'mla_w_ukv': _ukv_from_kernel(g['wukv']).transpose(1, 0, 2),
        'gla_w_gate_up': _cols_scattered(g['wg'][:GR]),
        'conv_w': _cols_scattered(g['cw']),
    }
    received = _exchange([slabs[n] for n in _SHARDED], scatter=True, name="scatter_g")
    small = dict(gla_b_gate=g['bg'], gla_norm_g=g['gn'], mla_q_norm_g=g['gq'], mla_kv_norm_g=g['gkv'],
                 ln1_g=g['g1'], ln1_b=g['b1'], conv_b=g['cb'], ln2_g=g['g2'], ln2_b=g['b2'])
    (small_parts,) = _exchange([_pack_small(small)], scatter=False, name="gather_small_g")

    grads, deltas, new_m, new_v = {}, {}, {}, {}
    for n, parts in zip(_SHARDED, received):
        shp = W[n].shape
        out = _adamw(parts, W[n][0], M[n][0], V[n][0], name="adamw_" + n)
        grads[n], deltas[n], new_m[n], new_v[n] = [t.reshape(shp) for t in out]
    out = _adamw(small_parts, _pack_small(W), _pack_small(M), _pack_small(V), name="adamw_small")
    for dst, packed in zip((grads, deltas, new_m, new_v), out):
        dst.update(_unpack_small(packed))

    return (loss, grad_x, *[grads[n] for n in _NAMES], *[deltas[n] for n in _NAMES],
            *[new_m[n] for n in _NAMES], *[new_v[n] for n in _NAMES])
```

```python
import functools

import jax
import jax.numpy as jnp
from jax import lax
from jax.experimental import pallas as pl
from jax.experimental.pallas import tpu as pltpu

F32 = jnp.float32
BF = jnp.bfloat16

D = 1024
GH, GDK, GDV, GR, GTAU, GC = 4, 128, 256, 16, 16.0, 64
MH, MQR, MKR, NOPE, ROPE, MV = 8, 384, 256, 128, 64, 128
THETA = 10000.0
DFF = 2816
ALPHA = 2.0 ** 0.25
LN_EPS = 1e-5
RMS_EPS = 1e-6
NDEV = 8
ADAM_LR, ADAM_B1, ADAM_B2, ADAM_EPS, ADAM_WD, ADAM_STEP = 0.001, 0.9, 0.999, 1e-08, 0.01, 10

PG_W = 3200
PM_W = 768
PT_W = 2048
NEG = -1e30
MESH_ID = pl.DeviceIdType.MESH
VMEM_MB = 1024 * 1024


def _params(sem, vmem_mb=48):
    return pltpu.CompilerParams(dimension_semantics=sem, vmem_limit_bytes=vmem_mb * VMEM_MB)


def _dot(a, b):
    return lax.dot_general(a, b, (((1,), (0,)), ((), ())), preferred_element_type=F32)


def _dot_nt(a, b):
    return lax.dot_general(a, b, (((1,), (1,)), ((), ())), preferred_element_type=F32)


def _dot_tn(a, b):
    return lax.dot_general(a, b, (((0,), (0,)), ((), ())), preferred_element_type=F32)


def _iota(shape, dim):
    return lax.broadcasted_iota(jnp.int32, shape, dim)


def _sigmoid(x):
    return 1.0 / (1.0 + jnp.exp(-x))


def _tri_mm(tri_bf, x):
    hi = x.astype(BF)
    r1 = x - hi.astype(F32)
    mid = r1.astype(BF)
    lo = (r1 - mid.astype(F32)).astype(BF)
    return _dot(tri_bf, hi) + _dot(tri_bf, mid) + _dot(tri_bf, lo)


def _matmul(a, b, mode, *, name, c_in=None, out_dtype=F32, tm=512, tn=512, tk=512, ride=None):
    if mode == "nn":
        (M, K), (_, N) = a.shape, b.shape
    elif mode == "nt":
        (M, K), (N, _) = a.shape, b.shape
    else:
        (K, M), (_, N) = a.shape, b.shape
    tm, tn, tk = min(tm, M), min(tn, N), min(tk, K)
    assert M % tm == 0 and N % tn == 0 and K % tk == 0, (name, M, N, K, tm, tn, tk)
    nk = K // tk
    dot = {"nn": _dot, "nt": _dot_nt, "tn": _dot_tn}[mode]

    def body(*refs):
        if c_in is None:
            a_ref, b_ref, o_ref, acc_ref = refs
        else:
            a_ref, b_ref, c_ref, o_ref, acc_ref = refs
        k = pl.program_id(2)

        @pl.when(k == 0)
        def _():
            if c_in is None:
                acc_ref[...] = jnp.zeros_like(acc_ref)
            else:
                acc_ref[...] = c_ref[...].astype(F32)

        acc_ref[...] += dot(a_ref[...].astype(BF), b_ref[...].astype(BF))

        @pl.when(k == nk - 1)
        def _():
            o_ref[...] = acc_ref[...].astype(out_dtype)

    if mode == "tn":
        a_spec = pl.BlockSpec((tk, tm), lambda i, j, k: (k, i))
    else:
        a_spec = pl.BlockSpec((tm, tk), lambda i, j, k: (i, k))
    if mode == "nt":
        b_spec = pl.BlockSpec((tn, tk), lambda i, j, k: (j, k))
    else:
        b_spec = pl.BlockSpec((tk, tn), lambda i, j, k: (k, j))
    in_specs = [a_spec, b_spec]
    args = [a, b]
    if c_in is not None:
        in_specs.append(pl.BlockSpec((tm, tn), lambda i, j, k: (i, j)))
        args.append(c_in)
    res = _call(
        body, name=name,
        out_shape=(jax.ShapeDtypeStruct((M, N), out_dtype),),
        grid=(M // tm, N // tn, nk),
        in_specs=in_specs,
        out_specs=(pl.BlockSpec((tm, tn), lambda i, j, k: (i, j)),),
        scratch_shapes=[pltpu.VMEM((tm, tn), F32)],
        sem=("parallel", "parallel", "arbitrary"), args=args, ride=ride)
    return res[0] if ride is None else (res[0][0], res[1])


def _gla_gate(pg_ref, rows, wg_ref, bg_ref):
    r = pg_ref[rows, 3072:3200].astype(BF)
    logit = _dot(r, wg_ref[...]) + bg_ref[...]
    la = (jnp.minimum(logit, 0.0) - jnp.log(1.0 + jnp.exp(-jnp.abs(logit)))) * (1.0 / GTAU)
    return r, logit, la


def _gla_fwd(pg, wg, bg, gn, ltri, *, nseq, S, tm):
    T = pg.shape[0]
    nb, nc = S // tm, tm // GC
    qscale = GDK ** -0.5

    def body(pg_ref, wg_ref, bg_ref, gn_ref, l_ref, o_ref, zg_ref, st_ref, st_scr):
        @pl.when(pl.program_id(1) == 0)
        def _():
            st_scr[...] = jnp.zeros_like(st_scr)

        ltri_v = l_ref[...]
        causal = _iota((GC, GC), 0) >= _iota((GC, GC), 1)
        last_row = _iota((GC, GDK), 0) == GC - 1
        g = gn_ref[...]

        def chunk(c, carry):
            rows = pl.ds(pl.multiple_of(c * GC, GC), GC)
            _, _, la = _gla_gate(pg_ref, rows, wg_ref, bg_ref)
            b = _tri_mm(ltri_v, la)
            for h in range(GH):
                q = pg_ref[rows, h * GDK:(h + 1) * GDK]
                k = pg_ref[rows, 512 + h * GDK:512 + (h + 1) * GDK]
                v = pg_ref[rows, 1024 + h * GDV:1024 + (h + 1) * GDV].astype(BF)
                og = pg_ref[rows, 2048 + h * GDV:2048 + (h + 1) * GDV]
                bh = b[:, h * GDK:(h + 1) * GDK]
                bl = jnp.sum(jnp.where(last_row, bh, 0.0), axis=0, keepdims=True)
                q_in = (q * (qscale * jnp.exp(bh))).astype(BF)
                k_in = (k * jnp.exp(-bh)).astype(BF)
                k_st = (k * jnp.exp(bl - bh)).astype(BF)
                dec = jnp.exp(bl)
                st = st_scr[h]
                st_ref[c, h] = st
                att = jnp.where(causal, _dot_nt(q_in, k_in), 0.0).astype(BF)
                o = _dot(att, v) + _dot_nt(q_in, st.astype(BF))
                st_scr[h] = st * dec + _dot_tn(v, k_st)
                rstd = lax.rsqrt(jnp.mean(o * o, axis=-1, keepdims=True) + RMS_EPS)
                o_ref[rows, h * GDV:(h + 1) * GDV] = o
                zg_ref[rows, h * GDV:(h + 1) * GDV] = (o * rstd * g * (og * _sigmoid(og))).astype(BF)
            return carry

        lax.fori_loop(0, nc, chunk, 0)

    full = lambda shp: pl.BlockSpec(shp, lambda b_, i: (0,) * len(shp))
    return pl.pallas_call(
        body, name="gla_fwd",
        out_shape=(jax.ShapeDtypeStruct((T, GH * GDV), F32),
                   jax.ShapeDtypeStruct((T, GH * GDV), BF),
                   jax.ShapeDtypeStruct((T // GC, GH, GDV, GDK), F32)),
        grid=(nseq, nb),
        in_specs=[pl.BlockSpec((tm, PG_W), lambda b_, i: (b_ * nb + i, 0)),
                  full((128, 512)), full((1, 512)), full((1, GDV)), full((GC, GC))],
        out_specs=(pl.BlockSpec((tm, GH * GDV), lambda b_, i: (b_ * nb + i, 0)),
                   pl.BlockSpec((tm, GH * GDV), lambda b_, i: (b_ * nb + i, 0)),
                   pl.BlockSpec((nc, GH, GDV, GDK), lambda b_, i: (b_ * nb + i, 0, 0, 0))),
        scratch_shapes=[pltpu.VMEM((GH, GDV, GDK), F32)],
        compiler_params=_params(("parallel", "arbitrary")),
    )(pg, wg, bg, gn, ltri)


def _gla_bwd(pg, wg, bg, gn, ltri, utri, o, states, dzg, *, nseq, S, tm, ride=None):
    T = pg.shape[0]
    nb, nc = S // tm, tm // GC
    qscale = GDK ** -0.5

    def body(pg_ref, wg_ref, bg_ref, gn_ref, l_ref, u_ref, o_ref, st_ref, dzg_ref,
             dpg_ref, dwg_ref, dbg_ref, dgn_ref, dst_scr):
        first = jnp.logical_and(pl.program_id(0) == 0, pl.program_id(1) == 0)

        @pl.when(first)
        def _():
            dwg_ref[...] = jnp.zeros_like(dwg_ref)
            dbg_ref[...] = jnp.zeros_like(dbg_ref)
            dgn_ref[...] = jnp.zeros_like(dgn_ref)

        @pl.when(pl.program_id(1) == 0)
        def _():
            dst_scr[...] = jnp.zeros_like(dst_scr)

        ltri_v = l_ref[...]
        utri_v = u_ref[...]
        causal = _iota((GC, GC), 0) >= _iota((GC, GC), 1)
        last_row = _iota((GC, GDK), 0) == GC - 1
        g = gn_ref[...]

        def chunk(cc, carry):
            c = nc - 1 - cc
            rows = pl.ds(pl.multiple_of(c * GC, GC), GC)
            r, logit, la = _gla_gate(pg_ref, rows, wg_ref, bg_ref)
            b = _tri_mm(ltri_v, la)
            dbs = []
            for h in range(GH):
                q = pg_ref[rows, h * GDK:(h + 1) * GDK]
                k = pg_ref[rows, 512 + h * GDK:512 + (h + 1) * GDK]
                vb = pg_ref[rows, 1024 + h * GDV:1024 + (h + 1) * GDV].astype(BF)
                og = pg_ref[rows, 2048 + h * GDV:2048 + (h + 1) * GDV]
                oh = o_ref[rows, h * GDV:(h + 1) * GDV]
                dz = dzg_ref[rows, h * GDV:(h + 1) * GDV].astype(F32)
                bh = b[:, h * GDK:(h + 1) * GDK]
                bl = jnp.sum(jnp.where(last_row, bh, 0.0), axis=0, keepdims=True)
                eb = qscale * jnp.exp(bh)
                enb = jnp.exp(-bh)
                ek = jnp.exp(bl - bh)
                dec = jnp.exp(bl)
                q_in = q * eb
                k_in = k * enb
                k_st = k * ek
                q_inb, k_inb, k_stb = q_in.astype(BF), k_in.astype(BF), k_st.astype(BF)
                st = st_ref[c, h]
                dst = dst_scr[h]
                rstd = lax.rsqrt(jnp.mean(oh * oh, axis=-1, keepdims=True) + RMS_EPS)
                ohat = oh * rstd
                sg = _sigmoid(og)
                don = dz * (og * sg)
                dog = dz * (ohat * g) * (sg * (1.0 + og * (1.0 - sg)))
                dgn_ref[...] += jnp.sum(don * ohat, axis=0, keepdims=True)
                gd = don * g
                do = rstd * (gd - ohat * jnp.mean(gd * ohat, axis=-1, keepdims=True))
                dob = do.astype(BF)
                att = jnp.where(causal, _dot_nt(q_inb, k_inb), 0.0).astype(BF)
                da = jnp.where(causal, _dot_nt(dob, vb), 0.0).astype(BF)
                dstb = dst.astype(BF)
                dqi = _dot(da, k_inb) + _dot(dob, st.astype(BF))
                dki = _dot_tn(da, q_inb)
                dv = _dot_tn(att, dob) + _dot_nt(k_stb, dstb)
                dks = _dot(vb, dstb)
                dd = jnp.sum(dst * st, axis=0, keepdims=True)
                dst_scr[h] = dst * dec + _dot_tn(dob, q_inb)
                dq = dqi * eb
                dk = dki * enb + dks * ek
                kk = dks * k_st
                dbl = jnp.sum(kk, axis=0, keepdims=True) + dd * dec
                db = dqi * q_in - dki * k_in - kk
                dbs.append(db + jnp.where(last_row, dbl, 0.0))
                dpg_ref[rows, h * GDK:(h + 1) * GDK] = dq.astype(BF)
                dpg_ref[rows, 512 + h * GDK:512 + (h + 1) * GDK] = dk.astype(BF)
                dpg_ref[rows, 1024 + h * GDV:1024 + (h + 1) * GDV] = dv.astype(BF)
                dpg_ref[rows, 2048 + h * GDV:2048 + (h + 1) * GDV] = dog.astype(BF)
            dla = _tri_mm(utri_v, jnp.concatenate(dbs, axis=1))
            dlogit = dla * (1.0 / GTAU) * _sigmoid(-logit)
            dlb = dlogit.astype(BF)
            dpg_ref[rows, 3072:3200] = _dot_nt(dlb, wg_ref[...]).astype(BF)
            dwg_ref[...] += _dot_tn(r, dlb)
            dbg_ref[...] += jnp.sum(dlogit, axis=0, keepdims=True)
            return carry

        lax.fori_loop(0, nc, chunk, 0)

    full = lambda shp: pl.BlockSpec(shp, lambda b_, i: (0,) * len(shp))
    rev = lambda b_, i: (b_ * nb + nb - 1 - i, 0)
    return _call(
        body, name="gla_bwd", ride=ride, sem=("arbitrary", "arbitrary"),
        args=(pg, wg, bg, gn, ltri, utri, o, states, dzg),
        out_shape=(jax.ShapeDtypeStruct((T, PG_W), BF),
                   jax.ShapeDtypeStruct((128, 512), F32),
                   jax.ShapeDtypeStruct((1, 512), F32),
                   jax.ShapeDtypeStruct((1, GDV), F32)),
        grid=(nseq, nb),
        in_specs=[pl.BlockSpec((tm, PG_W), rev),
                  full((128, 512)), full((1, 512)), full((1, GDV)), full((GC, GC)), full((GC, GC)),
                  pl.BlockSpec((tm, GH * GDV), rev),
                  pl.BlockSpec((nc, GH, GDV, GDK), lambda b_, i: (b_ * nb + nb - 1 - i, 0, 0, 0)),
                  pl.BlockSpec((tm, GH * GDV), rev)],
        out_specs=(pl.BlockSpec((tm, PG_W), rev), full((128, 512)), full((1, 512)), full((1, GDV))),
        scratch_shapes=[pltpu.VMEM((GH, GDV, GDK), F32)])


def _rope_tables(pos, invf):
    ang = pos.astype(F32) * invf
    lane = _iota(ang.shape, 1)
    sin = jnp.sin(ang)
    ssin = jnp.where(lane < 32, -sin, jnp.where(lane < 64, sin, 0.0))
    return jnp.cos(ang), ssin, lane


def _rope(x, cos, ssin, lane, sign):
    rot = jnp.where(lane < 32, pltpu.roll(x, 96, 1), pltpu.roll(x, 32, 1))
    return x * cos + sign * (rot * ssin)


def _rms_fwd(x, g):
    rstd = lax.rsqrt(jnp.mean(x * x, axis=-1, keepdims=True) + RMS_EPS)
    return x * rstd * g, x * rstd, rstd


def _rms_bwd(dy, xhat, rstd, g):
    gd = dy * g
    return rstd * (gd - xhat * jnp.mean(gd * xhat, axis=-1, keepdims=True)), jnp.sum(dy * xhat, axis=0, keepdims=True)


def _mla_prep_fwd(pm, pos, invf, gq, gkv, wuq, wukv, *, tm):
    T = pm.shape[0]

    def body(pm_ref, pos_ref, invf_ref, gq_ref, gkv_ref, wuq_ref, wukv_ref, qc_ref, kc_ref, v_ref):
        cos, ssin, lane = _rope_tables(pos_ref[...], invf_ref[...])
        cq, _, _ = _rms_fwd(pm_ref[:, 0:MQR], gq_ref[...])
        ckv, _, _ = _rms_fwd(pm_ref[:, 512:768], gkv_ref[...])
        qf = _dot(cq.astype(BF), wuq_ref[...])
        kvf = _dot(ckv.astype(BF), wukv_ref[...])
        kr = _rope(pm_ref[:, 384:512], cos, ssin, lane, 1.0).astype(BF)
        for h in range(MH):
            qc_ref[:, 256 * h:256 * h + 128] = qf[:, 128 * h:128 * h + 128].astype(BF)
            qr = qf[:, 1024 + 128 * h:1024 + 128 * h + 128]
            qc_ref[:, 256 * h + 128:256 * h + 256] = _rope(qr, cos, ssin, lane, 1.0).astype(BF)
            kc_ref[:, 256 * h:256 * h + 128] = kvf[:, 128 * h:128 * h + 128].astype(BF)
            kc_ref[:, 256 * h + 128:256 * h + 256] = kr
        v_ref[...] = kvf[:, 1024:2048].astype(BF)

    full = lambda shp: pl.BlockSpec(shp, lambda i: (0,) * len(shp))
    row = lambda w: pl.BlockSpec((tm, w), lambda i: (i, 0))
    return pl.pallas_call(
        body, name="mla_prep_fwd",
        out_shape=(jax.ShapeDtypeStruct((T, MH * 256), BF), jax.ShapeDtypeStruct((T, MH * 256), BF),
                   jax.ShapeDtypeStruct((T, MH * MV), BF)),
        grid=(T // tm,),
        in_specs=[row(PM_W), row(1), full((1, 128)), full((1, MQR)), full((1, MKR)),
                  full((MQR, 2048)), full((MKR, 2048))],
        out_specs=(row(MH * 256), row(MH * 256), row(MH * MV)),
        compiler_params=_params(("parallel",)),
    )(pm, pos, invf, gq, gkv, wuq, wukv)


def _mla_prep_bwd(pm, pos, invf, gq, gkv, wuq, wukv, dqc, dkc, dv, *, tm):
    T = pm.shape[0]

    def body(pm_ref, pos_ref, invf_ref, gq_ref, gkv_ref, wuq_ref, wukv_ref, dqc_ref, dkc_ref, dv_ref,
             dpm_ref, dwuq_ref, dwukv_ref, dgq_ref, dgkv_ref):
        @pl.when(pl.program_id(0) == 0)
        def _():
            dwuq_ref[...] = jnp.zeros_like(dwuq_ref)
            dwukv_ref[...] = jnp.zeros_like(dwukv_ref)
            dgq_ref[...] = jnp.zeros_like(dgq_ref)
            dgkv_ref[...] = jnp.zeros_like(dgkv_ref)

        cos, ssin, lane = _rope_tables(pos_ref[...], invf_ref[...])
        cq, cqh, cq_rstd = _rms_fwd(pm_ref[:, 0:MQR], gq_ref[...])
        ckv, ckvh, ckv_rstd = _rms_fwd(pm_ref[:, 512:768], gkv_ref[...])
        dqn, dqr, dkn = [], [], []
        dkr = jnp.zeros((tm, 128), F32)
        for h in range(MH):
            dqn.append(dqc_ref[:, 256 * h:256 * h + 128].astype(BF))
            dqr.append(_rope(dqc_ref[:, 256 * h + 128:256 * h + 256], cos, ssin, lane, -1.0).astype(BF))
            dkn.append(dkc_ref[:, 256 * h:256 * h + 128].astype(BF))
            dkr = dkr + dkc_ref[:, 256 * h + 128:256 * h + 256]
        dqf = jnp.concatenate(dqn + dqr, axis=1)
        dkvf = jnp.concatenate(dkn + [dv_ref[...].astype(BF)], axis=1)
        dwuq_ref[...] += _dot_tn(cq.astype(BF), dqf)
        dwukv_ref[...] += _dot_tn(ckv.astype(BF), dkvf)
        dcq, dgq = _rms_bwd(_dot_nt(dqf, wuq_ref[...]), cqh, cq_rstd, gq_ref[...])
        dckv, dgkv = _rms_bwd(_dot_nt(dkvf, wukv_ref[...]), ckvh, ckv_rstd, gkv_ref[...])
        dgq_ref[...] += dgq
        dgkv_ref[...] += dgkv
        dpm_ref[:, 0:MQR] = dcq.astype(BF)
        dpm_ref[:, 384:512] = _rope(dkr, cos, ssin, lane, -1.0).astype(BF)
        dpm_ref[:, 512:768] = dckv.astype(BF)

    full = lambda shp: pl.BlockSpec(shp, lambda i: (0,) * len(shp))
    row = lambda w: pl.BlockSpec((tm, w), lambda i: (i, 0))
    return pl.pallas_call(
        body, name="mla_prep_bwd",
        out_shape=(jax.ShapeDtypeStruct((T, PM_W), BF), jax.ShapeDtypeStruct((MQR, 2048), F32),
                   jax.ShapeDtypeStruct((MKR, 2048), F32), jax.ShapeDtypeStruct((1, MQR), F32),
                   jax.ShapeDtypeStruct((1, MKR), F32)),
        grid=(T // tm,),
        in_specs=[row(PM_W), row(1), full((1, 128)), full((1, MQR)), full((1, MKR)),
                  full((MQR, 2048)), full((MKR, 2048)), row(MH * 256), row(MH * 256), row(MH * MV)],
        out_specs=(row(PM_W), full((MQR, 2048)), full((MKR, 2048)), full((1, MQR)), full((1, MKR))),
        compiler_params=_params(("arbitrary",)),
    )(pm, pos, invf, gq, gkv, wuq, wukv, dqc, dkc, dv)


def _flash_fwd(qc, kc, v, *, nseq, S, tq, ride=None):
    T = qc.shape[0]
    nq = S // tq
    scale = (NOPE + ROPE) ** -0.5

    def body(q_ref, k_ref, v_ref, o_ref, lse_ref):
        i = pl.program_id(2)
        q = q_ref[...]
        causal = _iota((tq, tq), 0) >= _iota((tq, tq), 1)

        def step(j, carry, masked):
            m, l, acc = carry
            rows = pl.ds(pl.multiple_of(j * tq, tq), tq)
            s = _dot_nt(q, k_ref[rows, :]) * scale
            if masked:
                s = jnp.where(causal, s, NEG)
            m_new = jnp.maximum(m, jnp.max(s, axis=-1, keepdims=True))
            p = jnp.exp(s - m_new)
            a = jnp.exp(m - m_new)
            l = a * l + jnp.sum(p, axis=-1, keepdims=True)
            acc = a * acc + _dot(p.astype(BF), v_ref[rows, :])
            return m_new, l, acc

        init = (jnp.full((tq, 1), NEG, F32), jnp.zeros((tq, 1), F32), jnp.zeros((tq, MV), F32))
        carry = lax.fori_loop(0, i, lambda j, c: step(j, c, False), init)
        m, l, acc = step(i, carry, True)
        o_ref[...] = (acc / l).astype(BF)
        lse_ref[...] = jnp.broadcast_to(m + jnp.log(l), (tq, 128))

    return _call(
        body, name="flash_fwd", ride=ride, sem=("parallel", "parallel", "arbitrary"), args=(qc, kc, v),
        out_shape=(jax.ShapeDtypeStruct((T, MH * MV), BF), jax.ShapeDtypeStruct((T, MH * 128), F32)),
        grid=(nseq, MH, nq),
        in_specs=[pl.BlockSpec((tq, 256), lambda b_, h, i: (b_ * nq + i, h)),
                  pl.BlockSpec((S, 256), lambda b_, h, i: (b_, h)),
                  pl.BlockSpec((S, MV), lambda b_, h, i: (b_, h))],
        out_specs=(pl.BlockSpec((tq, MV), lambda b_, h, i: (b_ * nq + i, h)),
                   pl.BlockSpec((tq, 128), lambda b_, h, i: (b_ * nq + i, h))))


def _flash_bwd(qc, kc, v, o, do, lse, *, nseq, S, tq, ride=None):
    T = qc.shape[0]
    nq = S // tq
    scale = (NOPE + ROPE) ** -0.5

    def body(q_ref, k_ref, v_ref, o_ref, do_ref, lse_ref, dq_ref, dk_ref, dv_ref, dq_scr, delta_scr):
        j = pl.program_id(2)

        @pl.when(j == 0)
        def _():
            dq_scr[...] = jnp.zeros_like(dq_scr)
            delta = jnp.sum(o_ref[...].astype(F32) * do_ref[...].astype(F32), axis=-1, keepdims=True)
            delta_scr[...] = jnp.broadcast_to(delta, (S, 128))

        kb = k_ref[...]
        vb = v_ref[...]
        causal = _iota((tq, tq), 0) >= _iota((tq, tq), 1)

        def step(i, carry, masked):
            dk, dv = carry
            rows = pl.ds(pl.multiple_of(i * tq, tq), tq)
            q = q_ref[rows, :]
            dob = do_ref[rows, :]
            s = _dot_nt(q, kb) * scale
            p = jnp.exp(s - lse_ref[rows, 0:1])
            if masked:
                p = jnp.where(causal, p, 0.0)
            dv = dv + _dot_tn(p.astype(BF), dob)
            dp = _dot_nt(dob, vb)
            ds = (p * (dp - delta_scr[rows, 0:1]) * scale).astype(BF)
            dk = dk + _dot_tn(ds, q)
            dq_scr[rows, :] += _dot(ds, kb)
            return dk, dv

        carry = step(j, (jnp.zeros((tq, 256), F32), jnp.zeros((tq, MV), F32)), True)
        dk, dv = lax.fori_loop(j + 1, nq, lambda i, c: step(i, c, False), carry)
        dk_ref[...] = dk
        dv_ref[...] = dv

        @pl.when(j == nq - 1)
        def _():
            dq_ref[...] = dq_scr[...]

    seq = lambda w: pl.BlockSpec((S, w), lambda b_, h, j: (b_, h))
    blk = lambda w: pl.BlockSpec((tq, w), lambda b_, h, j: (b_ * nq + j, h))
    return _call(
        body, name="flash_bwd", ride=ride, sem=("parallel", "parallel", "arbitrary"), args=(qc, kc, v, o, do, lse),
        out_shape=(jax.ShapeDtypeStruct((T, MH * 256), F32), jax.ShapeDtypeStruct((T, MH * 256), F32),
                   jax.ShapeDtypeStruct((T, MH * MV), F32)),
        grid=(nseq, MH, nq),
        in_specs=[seq(256), blk(256), blk(MV), seq(MV), seq(MV), seq(128)],
        out_specs=(seq(256), blk(256), blk(MV)),
        scratch_shapes=[pltpu.VMEM((S, 256), F32), pltpu.VMEM((S, 128), F32)])


def _ln_fwd(pre, g, b):
    mu = jnp.mean(pre, axis=-1, keepdims=True)
    xc = pre - mu
    rstd = lax.rsqrt(jnp.mean(xc * xc, axis=-1, keepdims=True) + LN_EPS)
    xhat = xc * rstd
    return xhat * g + b, xhat, rstd


def _ln_bwd(dy, xhat, rstd, g):
    dxh = dy * g
    dx = rstd * (dxh - jnp.mean(dxh, axis=-1, keepdims=True) - xhat * jnp.mean(dxh * xhat, axis=-1, keepdims=True))
    return dx, jnp.sum(dy * xhat, axis=0, keepdims=True), jnp.sum(dy, axis=0, keepdims=True)


def _post_attn_fwd(zg, attn, pt, x, wgo, wmo, wout, g1, b1, *, tm):
    T = x.shape[0]

    def body(zg_ref, at_ref, pt_ref, x_ref, wgo_ref, wmo_ref, wout_ref, g_ref, b_ref,
             yg_ref, ym_ref, mix_ref, pre_ref, h_ref, hb_ref):
        yg = _dot(zg_ref[...], wgo_ref[...])
        ym = _dot(at_ref[...], wmo_ref[...])
        mix = (_sigmoid(pt_ref[:, 0:D]) * yg + _sigmoid(pt_ref[:, D:2 * D]) * ym).astype(BF)
        pre = ALPHA * x_ref[...] + _dot(mix, wout_ref[...])
        h, _, _ = _ln_fwd(pre, g_ref[...], b_ref[...])
        yg_ref[...] = yg
        ym_ref[...] = ym
        mix_ref[...] = mix
        pre_ref[...] = pre
        h_ref[...] = h
        hb_ref[...] = h.astype(BF)

    full = lambda shp: pl.BlockSpec(shp, lambda i: (0,) * len(shp))
    row = lambda w: pl.BlockSpec((tm, w), lambda i: (i, 0))
    sd = lambda dt: jax.ShapeDtypeStruct((T, D), dt)
    return pl.pallas_call(
        body, name="post_attn_fwd",
        out_shape=(sd(F32), sd(F32), sd(BF), sd(F32), sd(F32), sd(BF)),
        grid=(T // tm,),
        in_specs=[row(D), row(D), row(PT_W), row(D), full((D, D)), full((D, D)), full((D, D)),
                  full((1, D)), full((1, D))],
        out_specs=(row(D),) * 6,
        compiler_params=_params(("parallel",)),
    )(zg, attn, pt, x, wgo, wmo, wout, g1, b1)


def _post_attn_bwd(dh, pre, pt, yg, ym, wgo, wmo, wout, g1, *, tm):
    T = dh.shape[0]

    def body(dh_ref, pre_ref, pt_ref, yg_ref, ym_ref, wgo_ref, wmo_ref, wout_ref, g_ref,
             dx_ref, dpreb_ref, dpt_ref, dygb_ref, dymb_ref, dzg_ref, dat_ref, dg_ref, db_ref):
        @pl.when(pl.program_id(0) == 0)
        def _():
            dg_ref[...] = jnp.zeros_like(dg_ref)
            db_ref[...] = jnp.zeros_like(db_ref)

        pre = pre_ref[...]
        mu = jnp.mean(pre, axis=-1, keepdims=True)
        xc = pre - mu
        rstd = lax.rsqrt(jnp.mean(xc * xc, axis=-1, keepdims=True) + LN_EPS)
        dpre, dg, db = _ln_bwd(dh_ref[...], xc * rstd, rstd, g_ref[...])
        dg_ref[...] += dg
        db_ref[...] += db
        dx_ref[...] = ALPHA * dpre
        dpreb = dpre.astype(BF)
        dpreb_ref[...] = dpreb
        dmix = _dot_nt(dpreb, wout_ref[...])
        sa = _sigmoid(pt_ref[:, 0:D])
        sb = _sigmoid(pt_ref[:, D:2 * D])
        dpt_ref[:, 0:D] = (dmix * yg_ref[...] * (sa * (1.0 - sa))).astype(BF)
        dpt_ref[:, D:2 * D] = (dmix * ym_ref[...] * (sb * (1.0 - sb))).astype(BF)
        dyg = (dmix * sa).astype(BF)
        dym = (dmix * sb).astype(BF)
        dygb_ref[...] = dyg
        dymb_ref[...] = dym
        dzg_ref[...] = _dot_nt(dyg, wgo_ref[...]).astype(BF)
        dat_ref[...] = _dot_nt(dym, wmo_ref[...]).astype(BF)

    full = lambda shp: pl.BlockSpec(shp, lambda i: (0,) * len(shp))
    row = lambda w: pl.BlockSpec((tm, w), lambda i: (i, 0))
    sd = lambda w, dt: jax.ShapeDtypeStruct((T, w), dt)
    return pl.pallas_call(
        body, name="post_attn_bwd",
        out_shape=(sd(D, F32), sd(D, BF), sd(PT_W, BF), sd(D, BF), sd(D, BF), sd(D, BF), sd(D, BF),
                   jax.ShapeDtypeStruct((1, D), F32), jax.ShapeDtypeStruct((1, D), F32)),
        grid=(T // tm,),
        in_specs=[row(D), row(D), row(PT_W), row(D), row(D), full((D, D)), full((D, D)), full((D, D)),
                  full((1, D))],
        out_specs=(row(D), row(D), row(PT_W), row(D), row(D), row(D), row(D), full((1, D)), full((1, D))),
        compiler_params=_params(("arbitrary",)),
    )(dh, pre, pt, yg, ym, wgo, wmo, wout, g1)


def _shift_down(u, prev, k):
    r = pltpu.roll(u, k, 0)
    p = pltpu.roll(prev, k, 0)
    head = jnp.where(_iota(p.shape, 0) < k, p, r[0:8, :])
    return jnp.concatenate([head, r[8:, :]], axis=0)


def _shift_up(u, nxt, k):
    n = u.shape[0]
    r = pltpu.roll(u, n - k, 0)
    p = pltpu.roll(nxt, 8 - k, 0)
    tail = jnp.where(_iota(p.shape, 0) >= 8 - k, p, r[n - 8:, :])
    return jnp.concatenate([r[:n - 8, :], tail], axis=0)


def _conv3(u, prev, w_ref, b_ref):
    return (w_ref[0:1, :] * _shift_down(u, prev, 2) + w_ref[1:2, :] * _shift_down(u, prev, 1)
            + w_ref[2:3, :] * u + b_ref[...])


def _ffn_up_fwd(hb, wug, wuv, cw, cb, *, S, tm, tn):
    T = hb.shape[0]
    nj, nbs = DFF // tn, S // tm

    def body(h_ref, wg_ref, wv_ref, cwg_ref, cwv_ref, cbg_ref, cbv_ref, ug_ref, uv_ref, f_ref, pg_scr, pv_scr):
        @pl.when(pl.program_id(1) % nbs == 0)
        def _():
            pg_scr[...] = jnp.zeros_like(pg_scr)
            pv_scr[...] = jnp.zeros_like(pv_scr)

        h = h_ref[...]
        ug = _dot(h, wg_ref[...])
        uv = _dot(h, wv_ref[...])
        ucg = _conv3(ug, pg_scr[...], cwg_ref, cbg_ref)
        ucv = _conv3(uv, pv_scr[...], cwv_ref, cbv_ref)
        pg_scr[...] = ug[tm - 8:, :]
        pv_scr[...] = uv[tm - 8:, :]
        ug_ref[...] = ug
        uv_ref[...] = uv
        f_ref[...] = (ucg * _sigmoid(ucg) * ucv).astype(BF)

    tile = pl.BlockSpec((tm, tn), lambda j, i: (i, j))
    return pl.pallas_call(
        body, name="ffn_up_fwd",
        out_shape=(jax.ShapeDtypeStruct((T, DFF), F32), jax.ShapeDtypeStruct((T, DFF), F32),
                   jax.ShapeDtypeStruct((T, DFF), BF)),
        grid=(nj, T // tm),
        in_specs=[pl.BlockSpec((tm, D), lambda j, i: (i, 0)),
                  pl.BlockSpec((D, tn), lambda j, i: (0, j)), pl.BlockSpec((D, tn), lambda j, i: (0, j)),
                  pl.BlockSpec((3, tn), lambda j, i: (0, j)), pl.BlockSpec((3, tn), lambda j, i: (0, j + nj)),
                  pl.BlockSpec((1, tn), lambda j, i: (0, j)), pl.BlockSpec((1, tn), lambda j, i: (0, j + nj))],
        out_specs=(tile, tile, tile),
        scratch_shapes=[pltpu.VMEM((8, tn), F32), pltpu.VMEM((8, tn), F32)],
        compiler_params=_params(("parallel", "arbitrary")),
    )(hb, wug, wuv, cw, cw, cb, cb)


def _ffn_bwd(dpreb, wd, ug, uv, cw, cb, *, S, tm, tn):
    T = dpreb.shape[0]
    nj, nb, nbs = DFF // tn, T // tm, S // tm
    hb8 = tm // 8

    def body(dp_ref, wd_ref, ug_ref, uv_ref, hg_ref, hv_ref, cwg_ref, cwv_ref, cbg_ref, cbv_ref,
             dug_ref, duv_ref, dcg_ref, dcv_ref, ng_scr, nv_scr):
        ii = pl.program_id(1)
        i = nb - 1 - ii

        @pl.when(ii == 0)
        def _():
            dcg_ref[...] = jnp.zeros_like(dcg_ref)
            dcv_ref[...] = jnp.zeros_like(dcv_ref)

        @pl.when(i % nbs == nbs - 1)
        def _():
            ng_scr[...] = jnp.zeros_like(ng_scr)
            nv_scr[...] = jnp.zeros_like(nv_scr)

        seq_start = i % nbs == 0
        df = _dot_nt(dp_ref[...], wd_ref[...])

        def half(u_ref, halo_ref, cw_ref, cb_ref):
            u = u_ref[...]
            prev = jnp.where(seq_start, 0.0, halo_ref[...])
            u1 = _shift_down(u, prev, 1)
            u2 = _shift_down(u, prev, 2)
            return u, u1, u2, cw_ref, cw_ref[0:1, :] * u2 + cw_ref[1:2, :] * u1 + cw_ref[2:3, :] * u + cb_ref[...]

        g_u, g_u1, g_u2, g_w, ucg = half(ug_ref, hg_ref, cwg_ref, cbg_ref)
        v_u, v_u1, v_u2, v_w, ucv = half(uv_ref, hv_ref, cwv_ref, cbv_ref)
        sg = _sigmoid(ucg)
        ducg = df * ucv * (sg * (1.0 + ucg * (1.0 - sg)))
        ducv = df * (ucg * sg)

        def finish(duc, u, u1, u2, w, nxt_scr, du_ref, dc_ref):
            nxt = nxt_scr[...]
            du = w[2:3, :] * duc + w[1:2, :] * _shift_up(duc, nxt, 1) + w[0:1, :] * _shift_up(duc, nxt, 2)
            du_ref[...] = du.astype(BF)
            nxt_scr[...] = duc[0:8, :]
            for row, z in enumerate((u2 * duc, u1 * duc, u * duc, duc)):
                dc_ref[row:row + 1, :] += jnp.sum(z, axis=0, keepdims=True)

        finish(ducg, g_u, g_u1, g_u2, g_w, ng_scr, dug_ref, dcg_ref)
        finish(ducv, v_u, v_u1, v_u2, v_w, nv_scr, duv_ref, dcv_ref)

    tile = pl.BlockSpec((tm, tn), lambda j, ii: (nb - 1 - ii, j))
    halo = pl.BlockSpec((8, tn), lambda j, ii: (jnp.maximum((nb - 1 - ii) * hb8 - 1, 0), j))
    acc = pl.BlockSpec((8, tn), lambda j, ii: (0, j))
    return pl.pallas_call(
        body, name="ffn_bwd",
        out_shape=(jax.ShapeDtypeStruct((T, DFF), BF), jax.ShapeDtypeStruct((T, DFF), BF),
                   jax.ShapeDtypeStruct((8, DFF), F32), jax.ShapeDtypeStruct((8, DFF), F32)),
        grid=(nj, nb),
        in_specs=[pl.BlockSpec((tm, D), lambda j, ii: (nb - 1 - ii, 0)),
                  pl.BlockSpec((tn, D), lambda j, ii: (j, 0)),
                  tile, tile, halo, halo,
                  pl.BlockSpec((3, tn), lambda j, ii: (0, j)), pl.BlockSpec((3, tn), lambda j, ii: (0, j + nj)),
                  pl.BlockSpec((1, tn), lambda j, ii: (0, j)), pl.BlockSpec((1, tn), lambda j, ii: (0, j + nj))],
        out_specs=(tile, tile, acc, acc),
        scratch_shapes=[pltpu.VMEM((8, tn), F32), pltpu.VMEM((8, tn), F32)],
        compiler_params=_params(("parallel", "arbitrary")),
    )(dpreb, wd, ug, uv, ug, uv, cw, cw, cb, cb)


def _down_ln2_loss(f_in, wd, h, target, g2, b2, *, tm):
    T = h.shape[0]

    def body(f_ref, wd_ref, h_ref, t_ref, g_ref, b_ref, dpb_ref, dh_ref, loss_ref, dg_ref, db_ref):
        @pl.when(pl.program_id(0) == 0)
        def _():
            loss_ref[...] = jnp.zeros_like(loss_ref)
            dg_ref[...] = jnp.zeros_like(dg_ref)
            db_ref[...] = jnp.zeros_like(db_ref)

        pre = ALPHA * h_ref[...] + _dot(f_ref[...], wd_ref[...])
        out, xhat, rstd = _ln_fwd(pre, g_ref[...], b_ref[...])
        diff = out - t_ref[...]
        loss_ref[...] += 0.5 * jnp.sum(jnp.mean(diff * diff, axis=-1, keepdims=True))
        dpre, dg, db = _ln_bwd(diff * (1.0 / D), xhat, rstd, g_ref[...])
        dg_ref[...] += dg
        db_ref[...] += db
        dpb_ref[...] = dpre.astype(BF)
        dh_ref[...] = ALPHA * dpre

    full = lambda shp: pl.BlockSpec(shp, lambda i: (0,) * len(shp))
    row = lambda w: pl.BlockSpec((tm, w), lambda i: (i, 0))
    return pl.pallas_call(
        body, name="down_ln2_loss",
        out_shape=(jax.ShapeDtypeStruct((T, D), BF), jax.ShapeDtypeStruct((T, D), F32),
                   jax.ShapeDtypeStruct((8, 128), F32), jax.ShapeDtypeStruct((1, D), F32),
                   jax.ShapeDtypeStruct((1, D), F32)),
        grid=(T // tm,),
        in_specs=[row(DFF), full((DFF, D)), row(D), row(D), full((1, D)), full((1, D))],
        out_specs=(row(D), row(D), full((8, 128)), full((1, D)), full((1, D))),
        compiler_params=_params(("arbitrary",)),
    )(f_in, wd, h, target, g2, b2)


def _adamw(parts, w, m, v, *, name):
    n, R, C = parts.shape
    tr = R
    for cand in range(min(R, 256), 7, -1):
        if R % cand == 0 and cand % 8 == 0:
            tr = cand
            break
    c1 = 1.0 - ADAM_B1 ** ADAM_STEP
    c2 = 1.0 - ADAM_B2 ** ADAM_STEP

    def body(p_ref, w_ref, m_ref, v_ref, g_ref, d_ref, nm_ref, nv_ref):
        g = p_ref[0].astype(F32)
        for s in range(1, n):
            g = g + p_ref[s].astype(F32)
        nm = ADAM_B1 * m_ref[...] + (1.0 - ADAM_B1) * g
        nv = ADAM_B2 * v_ref[...] + (1.0 - ADAM_B2) * (g * g)
        g_ref[...] = g
        nm_ref[...] = nm
        nv_ref[...] = nv
        d_ref[...] = -ADAM_LR * ((nm / c1) / (jnp.sqrt(nv / c2) + ADAM_EPS) + ADAM_WD * w_ref[...])

    blk = pl.BlockSpec((tr, C), lambda i: (i, 0))
    sd = jax.ShapeDtypeStruct((R, C), F32)
    return pl.pallas_call(
        body, name=name,
        out_shape=(sd, sd, sd, sd),
        grid=(R // tr,),
        in_specs=[pl.BlockSpec((n, tr, C), lambda i: (0, i, 0)), blk, blk, blk],
        out_specs=(blk, blk, blk, blk),
        compiler_params=_params(("parallel",)),
    )(parts, w, m, v)


class _Exchange:
    def __init__(self, items):
        self.items = items
        self.n = len(items)

    def out_shape(self):
        return tuple(jax.ShapeDtypeStruct(a.shape if sc else (NDEV,) + a.shape, a.dtype) for a, sc in self.items)

    def scratch(self):
        return [pltpu.SemaphoreType.DMA((self.n, NDEV - 1)), pltpu.SemaphoreType.DMA((self.n, NDEV - 1)),
                pltpu.SemaphoreType.DMA((self.n,))]

    def _copies(self, ins, outs, sems):
        send_sems, recv_sems, loc_sems = sems
        x, y, c = lax.axis_index("x"), lax.axis_index("y"), lax.axis_index("c")
        me = 4 * x + 2 * y + c
        flip = lambda p, d: 1 - p if d else p
        local, sent, landed = [], [], []
        for a, (_, sc) in enumerate(self.items):
            local.append(pltpu.make_async_copy(ins[a].at[me] if sc else ins[a], outs[a].at[me], loc_sems.at[a]))
        for k in range(1, NDEV):
            px, py, pc = flip(x, k & 4), flip(y, k & 2), flip(c, k & 1)
            peer = 4 * px + 2 * py + pc
            for a, (_, sc) in enumerate(self.items):
                src = ins[a].at[peer] if sc else ins[a]
                mk = functools.partial(pltpu.make_async_remote_copy, src_ref=src,
                                       send_sem=send_sems.at[a, k - 1], recv_sem=recv_sems.at[a, k - 1],
                                       device_id=(px, py, pc), device_id_type=MESH_ID)
                sent.append(mk(dst_ref=outs[a].at[me]))
                landed.append(mk(dst_ref=outs[a].at[peer]))
        return local, sent, landed

    def start(self, ins, outs, sems):
        local, sent, _ = self._copies(ins, outs, sems)
        for cp in local + sent:
            cp.start()

    def wait(self, ins, outs, sems):
        local, sent, landed = self._copies(ins, outs, sems)
        for cp in landed:
            cp.wait_recv()
        for cp in sent:
            cp.wait_send()
        for cp in local:
            cp.wait()


def _call(body, *, name, grid, in_specs, out_specs, out_shape, args, scratch_shapes=(), sem=None, ride=None):
    if ride is None:
        return pl.pallas_call(body, name=name, grid=grid, in_specs=list(in_specs), out_specs=tuple(out_specs),
                              out_shape=tuple(out_shape), scratch_shapes=list(scratch_shapes),
                              compiler_params=_params(sem))(*args)
    n_in, n_out, n_scr, ne = len(args), len(out_shape), len(scratch_shapes), ride.n

    def ride_body(*refs):
        ins, ex_in = refs[:n_in], refs[n_in:n_in + ne]
        o0 = n_in + ne
        outs, ex_out = refs[o0:o0 + n_out], refs[o0 + n_out:o0 + n_out + ne]
        scr = refs[o0 + n_out + ne:o0 + n_out + ne + n_scr]
        sems = refs[o0 + n_out + ne + n_scr:]
        first = functools.reduce(jnp.logical_and, [pl.program_id(d) == 0 for d in range(len(grid))])
        last = functools.reduce(jnp.logical_and, [pl.program_id(d) == grid[d] - 1 for d in range(len(grid))])

        @pl.when(first)
        def _():
            ride.start(ex_in, ex_out, sems)

        body(*ins, *outs, *scr)

        @pl.when(last)
        def _():
            ride.wait(ex_in, ex_out, sems)

    anyspec = pl.BlockSpec(memory_space=pl.ANY)
    res = pl.pallas_call(
        ride_body, name=name, grid=grid,
        in_specs=list(in_specs) + [anyspec] * ne,
        out_specs=tuple(out_specs) + (anyspec,) * ne,
        out_shape=tuple(out_shape) + ride.out_shape(),
        scratch_shapes=list(scratch_shapes) + ride.scratch(),
        compiler_params=_params(("arbitrary",) * len(grid)),
    )(*args, *[a for a, _ in ride.items])
    return tuple(res[:n_out]), tuple(res[n_out:])


def _exchange(items, *, name):
    ex = _Exchange(items)

    def body(*refs):
        ex.start(refs[:ex.n], refs[ex.n:2 * ex.n], refs[2 * ex.n:])
        ex.wait(refs[:ex.n], refs[ex.n:2 * ex.n], refs[2 * ex.n:])

    anyspec = pl.BlockSpec(memory_space=pl.ANY)
    return pl.pallas_call(
        body, name=name, out_shape=ex.out_shape(), in_specs=[anyspec] * ex.n, out_specs=(anyspec,) * ex.n,
        scratch_shapes=ex.scratch(),
    )(*[a for a, _ in items])


def _tri_consts():
    r = lax.broadcasted_iota(jnp.int32, (GC, GC), 0)
    c = lax.broadcasted_iota(jnp.int32, (GC, GC), 1)
    return (r >= c).astype(BF), (r <= c).astype(BF)


def _local_step(x, positions, target, w, hooks=None):
    g = {}

    def run(host, fn, *a, **kw):
        h = None if hooks is None else hooks.get(host)
        if h is None:
            return fn(*a, **kw)
        out, received = fn(*a, ride=_Exchange(h[0](w, g)), **kw)
        h[1](received, w, g)
        return out

    nseq, S, _ = x.shape
    T = nseq * S
    tm = min(256, S)
    tq = min(512, S)
    x2 = x.reshape(T, D)
    xb = x2.astype(BF)
    pos = positions.reshape(T, 1)
    half = ROPE // 2
    inv = THETA ** (-jnp.arange(half, dtype=F32) / half)
    invf = jnp.concatenate([inv, inv, jnp.zeros((64,), F32)]).reshape(1, 128)
    ltri, utri = _tri_consts()

    pg = run("proj_g", _matmul, xb, w["w_g"], "nn", name="proj_g", tm=1024, tn=640, tk=1024)
    pm = _matmul(xb, w["w_m"], "nn", name="proj_m", tm=1024, tn=768, tk=1024)
    pt = _matmul(xb, w["w_t"], "nn", name="proj_t", tm=1024, tn=1024, tk=1024)
    o, zg, states = _gla_fwd(pg, w["wg"], w["bg"], w["gn"], ltri, nseq=nseq, S=S, tm=tm)
    qc, kc, v = _mla_prep_fwd(pm, pos, invf, w["gq"], w["gkv"], w["wuq"], w["wukv"], tm=tm)
    attn, lse = run("flash_fwd", _flash_fwd, qc, kc, v, nseq=nseq, S=S, tq=tq)
    yg, ym, mix, pre1, h1, h1b = _post_attn_fwd(zg, attn, pt, x2, w["wgo"], w["wmo"], w["wout"],
                                                w["g1"], w["b1"], tm=tm)
    ug, uv, f_in = _ffn_up_fwd(h1b, w["wug"], w["wuv"], w["cw"], w["cb"], S=S, tm=tm, tn=1408)
    dpre2b, dh1, loss8, dg2, db2 = _down_ln2_loss(f_in, w["wd"], h1, target.reshape(T, D), w["g2"], w["b2"], tm=tm)

    dug, duv, dcg, dcv = _ffn_bwd(dpre2b, w["wd"], ug, uv, w["cw"], w["cb"], S=S, tm=tm, tn=1408)
    g["g2"], g["b2"] = dg2, db2
    g["cw"] = jnp.concatenate([dcg[0:3], dcv[0:3]], axis=1)
    g["cb"] = jnp.concatenate([dcg[3:4], dcv[3:4]], axis=1)
    g["wd"] = _matmul(f_in, dpre2b, "tn", name="dw_down", tm=1408, tn=1024, tk=1024)
    g["wug"] = _matmul(h1b, dug, "tn", name="dw_up_g", tm=1024, tn=1408, tk=1024)
    g["wuv"] = _matmul(h1b, duv, "tn", name="dw_up_v", tm=1024, tn=1408, tk=1024)
    dh1 = _matmul(dug, w["wug"], "nt", name="dh1_g", c_in=dh1, tm=1024, tn=1024, tk=1408)
    dh1 = _matmul(duv, w["wuv"], "nt", name="dh1_v", c_in=dh1, tm=1024, tn=1024, tk=1408)
    dx, dpre1b, dpt, dygb, dymb, dzg, dattn, dg1, db1 = _post_attn_bwd(
        dh1, pre1, pt, yg, ym, w["wgo"], w["wmo"], w["wout"], w["g1"], tm=tm)
    g["g1"], g["b1"] = dg1, db1
    g["wout"] = _matmul(mix, dpre1b, "tn", name="dw_out", tm=1024, tn=1024, tk=1024)
    g["wgo"] = _matmul(zg, dygb, "tn", name="dw_gla_o", tm=1024, tn=1024, tk=1024)
    g["wmo"] = _matmul(attn, dymb, "tn", name="dw_mla_o", tm=1024, tn=1024, tk=1024)
    dqc, dkc, dv = run("flash_bwd", _flash_bwd, qc, kc, v, attn, dattn, lse, nseq=nseq, S=S, tq=tq)
    dpm, g["wuq"], g["wukv"], g["gq"], g["gkv"] = _mla_prep_bwd(
        pm, pos, invf, w["gq"], w["gkv"], w["wuq"], w["wukv"], dqc, dkc, dv, tm=tm)
    dpg, g["wg"], g["bg"], g["gn"] = run("gla_bwd", _gla_bwd, pg, w["wg"], w["bg"], w["gn"], ltri, utri, o, states,
                                         dzg, nseq=nseq, S=S, tm=tm)
    g["w_g"] = _matmul(xb, dpg, "tn", name="dw_in_g", tm=1024, tn=640, tk=1024)
    g["w_m"] = _matmul(xb, dpm, "tn", name="dw_in_m", tm=1024, tn=768, tk=1024)
    g["w_t"] = _matmul(xb, dpt, "tn", name="dw_in_t", tm=1024, tn=1024, tk=1024)
    dx = run("dx_g", _matmul, dpg, w["w_g"], "nt", name="dx_g", c_in=dx, tm=1024, tn=1024, tk=640)
    dx = _matmul(dpm, w["w_m"], "nt", name="dx_m", c_in=dx, tm=1024, tn=1024, tk=768)
    dx = _matmul(dpt, w["w_t"], "nt", name="dx_t", c_in=dx, tm=1024, tn=1024, tk=1024)
    return loss8[0, 0], dx.reshape(nseq, S, D), g


_IN_SPLITS = (512, 512, 1024, 16, 1024, 384, 256, 64, 1024, 1024)


def _w_in_to_groups(w_in):
    offs = [0]
    for s in _IN_SPLITS:
        offs.append(offs[-1] + s)
    q, k, v, r, og, cq, ckv, kr, ga, gb = [w_in[:, offs[i]:offs[i + 1]] for i in range(10)]
    z = lambda n: jnp.zeros((w_in.shape[0], n), w_in.dtype)
    return (jnp.concatenate([q, k, v, og, r, z(112)], axis=1),
            jnp.concatenate([cq, kr, z(64), ckv], axis=1),
            jnp.concatenate([ga, gb], axis=1))


def _groups_to_w_in(g_g, g_m, g_t):
    q, k, v, og, r = g_g[:, 0:512], g_g[:, 512:1024], g_g[:, 1024:2048], g_g[:, 2048:3072], g_g[:, 3072:3088]
    cq, kr, ckv = g_m[:, 0:384], g_m[:, 384:448], g_m[:, 512:768]
    return jnp.concatenate([q, k, v, r, og, cq, ckv, kr, g_t[:, 0:1024], g_t[:, 1024:2048]], axis=1)


def _uq_to_kernel(wuq):
    w3 = wuq.reshape(MQR, MH, NOPE + ROPE)
    rope = jnp.concatenate([w3[:, :, NOPE:], jnp.zeros((MQR, MH, 64), wuq.dtype)], axis=2)
    return jnp.concatenate([w3[:, :, :NOPE].reshape(MQR, MH * 128), rope.reshape(MQR, MH * 128)], axis=1)


def _uq_from_kernel(g):
    nope = g[:, :1024].reshape(MQR, MH, 128)
    rope = g[:, 1024:].reshape(MQR, MH, 128)[:, :, :ROPE]
    return jnp.concatenate([nope, rope], axis=2)


def _ukv_to_kernel(wukv):
    w3 = wukv.reshape(MKR, MH, NOPE + MV)
    return jnp.concatenate([w3[:, :, :NOPE].reshape(MKR, MH * 128), w3[:, :, NOPE:].reshape(MKR, MH * 128)], axis=1)


def _ukv_from_kernel(g):
    return jnp.concatenate([g[:, :1024].reshape(MKR, MH, 128), g[:, 1024:].reshape(MKR, MH, 128)], axis=2)


def _cols_gathered(a):
    return a.transpose(1, 0, 2).reshape(a.shape[1], NDEV * a.shape[2])


def _cols_scattered(a):
    R = a.shape[0]
    return a.reshape(R, NDEV, a.shape[1] // NDEV).transpose(1, 0, 2)


_SMALL = (("gla_b_gate", 512), ("gla_norm_g", 256), ("mla_q_norm_g", 384), ("mla_kv_norm_g", 256),
          ("ln1_g", 1024), ("ln1_b", 1024), ("conv_b", 5632), ("ln2_g", 1024), ("ln2_b", 1024))
_SMALL_ROWS = 88


def _pack_small(d):
    flat = jnp.concatenate([d[n].reshape(-1) for n, _ in _SMALL])
    return jnp.pad(flat, (0, _SMALL_ROWS * 128 - flat.shape[0])).reshape(_SMALL_ROWS, 128)


def _unpack_small(a):
    flat = a.reshape(-1)
    out, off = {}, 0
    for n, sz in _SMALL:
        out[n] = flat[off:off + sz].reshape(1, sz)
        off += sz
    return out


_NAMES = ['w_in', 'gla_w_gate_up', 'gla_b_gate', 'gla_norm_g', 'w_gla_o', 'mla_q_norm_g', 'mla_w_uq',
          'mla_kv_norm_g', 'mla_w_ukv', 'w_mla_o', 'w_out', 'ln1_g', 'ln1_b', 'w_up', 'conv_w', 'conv_b',
          'w_down', 'ln2_g', 'ln2_b']
_SHARDED = ['w_in', 'w_up', 'w_down', 'w_gla_o', 'w_mla_o', 'w_out', 'mla_w_uq', 'mla_w_ukv', 'gla_w_gate_up',
            'conv_w']


def kernel(x, positions, w_in, gla_w_gate_up, gla_b_gate, gla_norm_g, w_gla_o, mla_q_norm_g, mla_w_uq, mla_kv_norm_g, mla_w_ukv, w_mla_o, w_out, ln1_g, ln1_b, w_up, conv_w, conv_b, w_down, ln2_g, ln2_b, loss_target, m_w_in, m_gla_w_gate_up, m_gla_b_gate, m_gla_norm_g, m_w_gla_o, m_mla_q_norm_g, m_mla_w_uq, m_mla_kv_norm_g, m_mla_w_ukv, m_w_mla_o, m_w_out, m_ln1_g, m_ln1_b, m_w_up, m_conv_w, m_conv_b, m_w_down, m_ln2_g, m_ln2_b, v_w_in, v_gla_w_gate_up, v_gla_b_gate, v_gla_norm_g, v_w_gla_o, v_mla_q_norm_g, v_mla_w_uq, v_mla_kv_norm_g, v_mla_w_ukv, v_w_mla_o, v_w_out, v_ln1_g, v_ln1_b, v_w_up, v_conv_w, v_conv_b, v_w_down, v_ln2_g, v_ln2_b):
    W = dict(w_in=w_in, gla_w_gate_up=gla_w_gate_up, gla_b_gate=gla_b_gate, gla_norm_g=gla_norm_g, w_gla_o=w_gla_o, mla_q_norm_g=mla_q_norm_g, mla_w_uq=mla_w_uq, mla_kv_norm_g=mla_kv_norm_g, mla_w_ukv=mla_w_ukv, w_mla_o=w_mla_o, w_out=w_out, ln1_g=ln1_g, ln1_b=ln1_b, w_up=w_up, conv_w=conv_w, conv_b=conv_b, w_down=w_down, ln2_g=ln2_g, ln2_b=ln2_b)
    M = dict(w_in=m_w_in, gla_w_gate_up=m_gla_w_gate_up, gla_b_gate=m_gla_b_gate, gla_norm_g=m_gla_norm_g, w_gla_o=m_w_gla_o, mla_q_norm_g=m_mla_q_norm_g, mla_w_uq=m_mla_w_uq, mla_kv_norm_g=m_mla_kv_norm_g, mla_w_ukv=m_mla_w_ukv, w_mla_o=m_w_mla_o, w_out=m_w_out, ln1_g=m_ln1_g, ln1_b=m_ln1_b, w_up=m_w_up, conv_w=m_conv_w, conv_b=m_conv_b, w_down=m_w_down, ln2_g=m_ln2_g, ln2_b=m_ln2_b)
    V = dict(w_in=v_w_in, gla_w_gate_up=v_gla_w_gate_up, gla_b_gate=v_gla_b_gate, gla_norm_g=v_gla_norm_g, w_gla_o=v_w_gla_o, mla_q_norm_g=v_mla_q_norm_g, mla_w_uq=v_mla_w_uq, mla_kv_norm_g=v_mla_kv_norm_g, mla_w_ukv=v_mla_w_ukv, w_mla_o=v_w_mla_o, w_out=v_w_out, ln1_g=v_ln1_g, ln1_b=v_ln1_b, w_up=v_w_up, conv_w=v_conv_w, conv_b=v_conv_b, w_down=v_w_down, ln2_g=v_ln2_g, ln2_b=v_ln2_b)

    shard = lambda n: (W[n][0].astype(BF), False)
    first = ['w_in', 'mla_w_uq', 'mla_w_ukv', 'gla_w_gate_up']
    G = dict(zip(first, _exchange([shard(n) for n in first], name="gather_w0")))
    w_g, w_m, w_t = _w_in_to_groups(_cols_gathered(G['w_in']))
    kw = dict(
        w_g=w_g, w_m=w_m, w_t=w_t,
        wg=jnp.pad(_cols_gathered(G['gla_w_gate_up']), ((0, 128 - GR), (0, 0))), bg=W['gla_b_gate'],
        gn=W['gla_norm_g'], gq=W['mla_q_norm_g'], gkv=W['mla_kv_norm_g'],
        wuq=_uq_to_kernel(_cols_gathered(G['mla_w_uq'])), wukv=_ukv_to_kernel(_cols_gathered(G['mla_w_ukv'])),
        g1=W['ln1_g'], b1=W['ln1_b'], g2=W['ln2_g'], b2=W['ln2_b'], cb=W['conv_b'],
    )
    received = {}

    def got_out_proj(ex, w, g):
        w.update(wgo=ex[0].reshape(D, D), wmo=ex[1].reshape(D, D), wout=ex[2].reshape(D, D))

    def got_ffn(ex, w, g):
        w_up_full = _cols_gathered(ex[0])
        w.update(wug=w_up_full[:, :DFF], wuv=w_up_full[:, DFF:], wd=ex[1].reshape(DFF, D), cw=_cols_gathered(ex[2]))

    slab = lambda a: (a.astype(BF), True)
    rows = lambda a: a.reshape(NDEV, a.shape[0] // NDEV, a.shape[1])

    def keep(names):
        return lambda ex, w, g: received.update(zip(names, ex))

    def small_grads(g):
        return _pack_small(dict(gla_b_gate=g['bg'], gla_norm_g=g['gn'], mla_q_norm_g=g['gq'], mla_kv_norm_g=g['gkv'],
                                ln1_g=g['g1'], ln1_b=g['b1'], conv_b=g['cb'], ln2_g=g['g2'], ln2_b=g['b2']))

    hooks = {
        "proj_g": (lambda w, g: [shard('w_gla_o'), shard('w_mla_o'), shard('w_out')], got_out_proj),
        "flash_fwd": (lambda w, g: [shard('w_up'), shard('w_down'), (W['conv_w'][0], False)], got_ffn),
        "flash_bwd": (lambda w, g: [slab(rows(g['wd'])),
                                    slab(_cols_scattered(jnp.concatenate([g['wug'], g['wuv']], axis=1)))],
                      keep(['w_down', 'w_up'])),
        "gla_bwd": (lambda w, g: [slab(rows(g['wout'])), slab(rows(g['wgo'])), slab(rows(g['wmo'])),
                                  slab(_uq_from_kernel(g['wuq']).transpose(1, 0, 2)),
                                  slab(_ukv_from_kernel(g['wukv']).transpose(1, 0, 2))],
                    keep(['w_out', 'w_gla_o', 'w_mla_o', 'mla_w_uq', 'mla_w_ukv'])),
        "dx_g": (lambda w, g: [slab(_cols_scattered(_groups_to_w_in(g['w_g'], g['w_m'], g['w_t']))),
                               (_cols_scattered(g['wg'][:GR]), True), (_cols_scattered(g['cw']), True),
                               (small_grads(g), False)],
                 keep(['w_in', 'gla_w_gate_up', 'conv_w', 'small'])),
    }

    loss_local, grad_x, _ = _local_step(x, positions, loss_target, kw, hooks)
    loss = lax.psum(loss_local, ("x", "y", "c"))

    grads, deltas, new_m, new_v = {}, {}, {}, {}
    small_parts = received['small']
    for n in _SHARDED:
        shp = W[n].shape
        out = _adamw(received[n], W[n][0], M[n][0], V[n][0], name="adamw_" + n)
        grads[n], deltas[n], new_m[n], new_v[n] = [t.reshape(shp) for t in out]
    out = _adamw(small_parts, _pack_small(W), _pack_small(M), _pack_small(V), name="adamw_small")
    for dst, packed in zip((grads, deltas, new_m, new_v), out):
        dst.update(_unpack_small(packed))

    return (loss, grad_x, *[grads[n] for n in _NAMES], *[deltas[n] for n in _NAMES],
            *[new_m[n] for n in _NAMES], *[new_v[n] for n in _NAMES])
```

```python
import functools

import jax
import jax.numpy as jnp
from jax import lax
from jax.experimental import pallas as pl
from jax.experimental.pallas import tpu as pltpu

F32 = jnp.float32
BF = jnp.bfloat16

D = 1024
GH, GDK, GDV, GR, GTAU, GC = 4, 128, 256, 16, 16.0, 64
MH, MQR, MKR, NOPE, ROPE, MV = 8, 384, 256, 128, 64, 128
THETA = 10000.0
DFF = 2816
ALPHA = 2.0 ** 0.25
LN_EPS = 1e-5
RMS_EPS = 1e-6
NDEV = 8
ADAM_LR, ADAM_B1, ADAM_B2, ADAM_EPS, ADAM_WD, ADAM_STEP = 0.001, 0.9, 0.999, 1e-08, 0.01, 10

PG_W = 3200
PM_W = 768
PT_W = 2048
NEG = -1e30
MESH_ID = pl.DeviceIdType.MESH
VMEM_MB = 1024 * 1024


def _params(sem, vmem_mb=48):
    return pltpu.CompilerParams(dimension_semantics=sem, vmem_limit_bytes=vmem_mb * VMEM_MB)


def _dot(a, b):
    return lax.dot_general(a, b, (((1,), (0,)), ((), ())), preferred_element_type=F32)


def _dot_nt(a, b):
    return lax.dot_general(a, b, (((1,), (1,)), ((), ())), preferred_element_type=F32)


def _dot_tn(a, b):
    return lax.dot_general(a, b, (((0,), (0,)), ((), ())), preferred_element_type=F32)


def _iota(shape, dim):
    return lax.broadcasted_iota(jnp.int32, shape, dim)


def _sigmoid(x):
    return 1.0 / (1.0 + jnp.exp(-x))


def _tri_mm(tri_bf, x):
    hi = x.astype(BF)
    r1 = x - hi.astype(F32)
    mid = r1.astype(BF)
    lo = (r1 - mid.astype(F32)).astype(BF)
    return _dot(tri_bf, hi) + _dot(tri_bf, mid) + _dot(tri_bf, lo)


def _matmul(a, b, mode, *, name, c_in=None, out_dtype=F32, tm=512, tn=512, tk=512, ride=None):
    if mode == "nn":
        (M, K), (_, N) = a.shape, b.shape
    elif mode == "nt":
        (M, K), (N, _) = a.shape, b.shape
    else:
        (K, M), (_, N) = a.shape, b.shape
    tm, tn, tk = min(tm, M), min(tn, N), min(tk, K)
    assert M % tm == 0 and N % tn == 0 and K % tk == 0, (name, M, N, K, tm, tn, tk)
    nk = K // tk
    dot = {"nn": _dot, "nt": _dot_nt, "tn": _dot_tn}[mode]

    def body(*refs):
        if c_in is None:
            a_ref, b_ref, o_ref, acc_ref = refs
        else:
            a_ref, b_ref, c_ref, o_ref, acc_ref = refs
        k = pl.program_id(2)

        @pl.when(k == 0)
        def _():
            if c_in is None:
                acc_ref[...] = jnp.zeros_like(acc_ref)
            else:
                acc_ref[...] = c_ref[...].astype(F32)

        acc_ref[...] += dot(a_ref[...].astype(BF), b_ref[...].astype(BF))

        @pl.when(k == nk - 1)
        def _():
            o_ref[...] = acc_ref[...].astype(out_dtype)

    if mode == "tn":
        a_spec = pl.BlockSpec((tk, tm), lambda i, j, k: (k, i))
    else:
        a_spec = pl.BlockSpec((tm, tk), lambda i, j, k: (i, k))
    if mode == "nt":
        b_spec = pl.BlockSpec((tn, tk), lambda i, j, k: (j, k))
    else:
        b_spec = pl.BlockSpec((tk, tn), lambda i, j, k: (k, j))
    in_specs = [a_spec, b_spec]
    args = [a, b]
    if c_in is not None:
        in_specs.append(pl.BlockSpec((tm, tn), lambda i, j, k: (i, j)))
        args.append(c_in)
    res = _call(
        body, name=name,
        out_shape=(jax.ShapeDtypeStruct((M, N), out_dtype),),
        grid=(M // tm, N // tn, nk),
        in_specs=in_specs,
        out_specs=(pl.BlockSpec((tm, tn), lambda i, j, k: (i, j)),),
        scratch_shapes=[pltpu.VMEM((tm, tn), F32)],
        sem=("parallel", "parallel", "arbitrary"), args=args, ride=ride)
    return res[0] if ride is None else (res[0][0], res[1])


def _matmul_sum(c_in, parts, *, name, tm=512, ride=None):
    M, N = c_in.shape
    tm = min(tm, M)
    n_p = len(parts)
    counts = [a.shape[1] // tk for a, _, tk in parts]
    starts = [sum(counts[:p]) for p in range(n_p)]
    nk = sum(counts)

    def body(*refs):
        a_refs, w_refs = refs[:n_p], refs[n_p:2 * n_p]
        c_ref, o_ref, acc_ref = refs[2 * n_p:]
        k = pl.program_id(1)

        @pl.when(k == 0)
        def _():
            acc_ref[...] = c_ref[...]

        for p in range(n_p):
            @pl.when(jnp.logical_and(k >= starts[p], k < starts[p] + counts[p]))
            def _(p=p):
                acc_ref[...] += _dot(a_refs[p][...].astype(BF), w_refs[p][...].astype(BF))

        @pl.when(k == nk - 1)
        def _():
            o_ref[...] = acc_ref[...]

    def kidx(p):
        return lambda k: jnp.clip(k - starts[p], 0, counts[p] - 1)

    in_specs = [pl.BlockSpec((tm, tk), lambda i, k, f=kidx(p): (i, f(k))) for p, (_, _, tk) in enumerate(parts)]
    in_specs += [pl.BlockSpec((tk, N), lambda i, k, f=kidx(p): (f(k), 0)) for p, (_, _, tk) in enumerate(parts)]
    in_specs.append(pl.BlockSpec((tm, N), lambda i, k: (i, 0)))
    res = _call(
        body, name=name, out_shape=(jax.ShapeDtypeStruct((M, N), F32),), grid=(M // tm, nk),
        in_specs=in_specs, out_specs=(pl.BlockSpec((tm, N), lambda i, k: (i, 0)),),
        scratch_shapes=[pltpu.VMEM((tm, N), F32)], sem=("parallel", "arbitrary"),
        args=[a for a, _, _ in parts] + [w for _, w, _ in parts] + [c_in], ride=ride)
    return res[0] if ride is None else (res[0][0], res[1])


def _gla_gate(pg_ref, rows, wg_ref, bg_ref):
    r = pg_ref[rows, 3072:3200].astype(BF)
    logit = _dot(r, wg_ref[...]) + bg_ref[...]
    la = (jnp.minimum(logit, 0.0) - jnp.log(1.0 + jnp.exp(-jnp.abs(logit)))) * (1.0 / GTAU)
    return r, logit, la


def _gla_fwd(pg, wg, bg, gn, ltri, *, nseq, S, tm):
    T = pg.shape[0]
    nb, nc = S // tm, tm // GC
    qscale = GDK ** -0.5

    def body(pg_ref, wg_ref, bg_ref, gn_ref, l_ref, o_ref, zg_ref, st_ref, st_scr):
        @pl.when(pl.program_id(1) == 0)
        def _():
            st_scr[...] = jnp.zeros_like(st_scr)

        ltri_v = l_ref[...]
        causal = _iota((GC, GC), 0) >= _iota((GC, GC), 1)
        last_row = _iota((GC, GDK), 0) == GC - 1
        g = gn_ref[...]

        def chunk(c, carry):
            rows = pl.ds(pl.multiple_of(c * GC, GC), GC)
            _, _, la = _gla_gate(pg_ref, rows, wg_ref, bg_ref)
            b = _tri_mm(ltri_v, la)
            for h in range(GH):
                q = pg_ref[rows, h * GDK:(h + 1) * GDK]
                k = pg_ref[rows, 512 + h * GDK:512 + (h + 1) * GDK]
                v = pg_ref[rows, 1024 + h * GDV:1024 + (h + 1) * GDV].astype(BF)
                og = pg_ref[rows, 2048 + h * GDV:2048 + (h + 1) * GDV]
                bh = b[:, h * GDK:(h + 1) * GDK]
                bl = jnp.sum(jnp.where(last_row, bh, 0.0), axis=0, keepdims=True)
                q_in = (q * (qscale * jnp.exp(bh))).astype(BF)
                k_in = (k * jnp.exp(-bh)).astype(BF)
                k_st = (k * jnp.exp(bl - bh)).astype(BF)
                dec = jnp.exp(bl)
                st = st_scr[h]
                st_ref[c, h] = st
                att = jnp.where(causal, _dot_nt(q_in, k_in), 0.0).astype(BF)
                o = _dot(att, v) + _dot_nt(q_in, st.astype(BF))
                st_scr[h] = st * dec + _dot_tn(v, k_st)
                rstd = lax.rsqrt(jnp.mean(o * o, axis=-1, keepdims=True) + RMS_EPS)
                o_ref[rows, h * GDV:(h + 1) * GDV] = o
                zg_ref[rows, h * GDV:(h + 1) * GDV] = (o * rstd * g * (og * _sigmoid(og))).astype(BF)
            return carry

        lax.fori_loop(0, nc, chunk, 0)

    full = lambda shp: pl.BlockSpec(shp, lambda b_, i: (0,) * len(shp))
    return pl.pallas_call(
        body, name="gla_fwd",
        out_shape=(jax.ShapeDtypeStruct((T, GH * GDV), F32),
                   jax.ShapeDtypeStruct((T, GH * GDV), BF),
                   jax.ShapeDtypeStruct((T // GC, GH, GDV, GDK), F32)),
        grid=(nseq, nb),
        in_specs=[pl.BlockSpec((tm, PG_W), lambda b_, i: (b_ * nb + i, 0)),
                  full((128, 512)), full((1, 512)), full((1, GDV)), full((GC, GC))],
        out_specs=(pl.BlockSpec((tm, GH * GDV), lambda b_, i: (b_ * nb + i, 0)),
                   pl.BlockSpec((tm, GH * GDV), lambda b_, i: (b_ * nb + i, 0)),
                   pl.BlockSpec((nc, GH, GDV, GDK), lambda b_, i: (b_ * nb + i, 0, 0, 0))),
        scratch_shapes=[pltpu.VMEM((GH, GDV, GDK), F32)],
        compiler_params=_params(("parallel", "arbitrary")),
    )(pg, wg, bg, gn, ltri)


def _gla_bwd(pg, wg, bg, gn, ltri, utri, o, states, dzg, *, nseq, S, tm, ride=None):
    T = pg.shape[0]
    nb, nc = S // tm, tm // GC
    qscale = GDK ** -0.5

    def body(pg_ref, wg_ref, bg_ref, gn_ref, l_ref, u_ref, o_ref, st_ref, dzg_ref,
             dpg_ref, dwg_ref, dbg_ref, dgn_ref, dst_scr):
        first = jnp.logical_and(pl.program_id(0) == 0, pl.program_id(1) == 0)

        @pl.when(first)
        def _():
            dwg_ref[...] = jnp.zeros_like(dwg_ref)
            dbg_ref[...] = jnp.zeros_like(dbg_ref)
            dgn_ref[...] = jnp.zeros_like(dgn_ref)

        @pl.when(pl.program_id(1) == 0)
        def _():
            dst_scr[...] = jnp.zeros_like(dst_scr)

        ltri_v = l_ref[...]
        utri_v = u_ref[...]
        causal = _iota((GC, GC), 0) >= _iota((GC, GC), 1)
        last_row = _iota((GC, GDK), 0) == GC - 1
        g = gn_ref[...]

        def chunk(cc, carry):
            c = nc - 1 - cc
            rows = pl.ds(pl.multiple_of(c * GC, GC), GC)
            r, logit, la = _gla_gate(pg_ref, rows, wg_ref, bg_ref)
            b = _tri_mm(ltri_v, la)
            dbs = []
            for h in range(GH):
                q = pg_ref[rows, h * GDK:(h + 1) * GDK]
                k = pg_ref[rows, 512 + h * GDK:512 + (h + 1) * GDK]
                vb = pg_ref[rows, 1024 + h * GDV:1024 + (h + 1) * GDV].astype(BF)
                og = pg_ref[rows, 2048 + h * GDV:2048 + (h + 1) * GDV]
                oh = o_ref[rows, h * GDV:(h + 1) * GDV]
                dz = dzg_ref[rows, h * GDV:(h + 1) * GDV].astype(F32)
                bh = b[:, h * GDK:(h + 1) * GDK]
                bl = jnp.sum(jnp.where(last_row, bh, 0.0), axis=0, keepdims=True)
                eb = qscale * jnp.exp(bh)
                enb = jnp.exp(-bh)
                ek = jnp.exp(bl - bh)
                dec = jnp.exp(bl)
                q_in = q * eb
                k_in = k * enb
                k_st = k * ek
                q_inb, k_inb, k_stb = q_in.astype(BF), k_in.astype(BF), k_st.astype(BF)
                st = st_ref[c, h]
                dst = dst_scr[h]
                rstd = lax.rsqrt(jnp.mean(oh * oh, axis=-1, keepdims=True) + RMS_EPS)
                ohat = oh * rstd
                sg = _sigmoid(og)
                don = dz * (og * sg)
                dog = dz * (ohat * g) * (sg * (1.0 + og * (1.0 - sg)))
                dgn_ref[...] += jnp.sum(don * ohat, axis=0, keepdims=True)
                gd = don * g
                do = rstd * (gd - ohat * jnp.mean(gd * ohat, axis=-1, keepdims=True))
                dob = do.astype(BF)
                att = jnp.where(causal, _dot_nt(q_inb, k_inb), 0.0).astype(BF)
                da = jnp.where(causal, _dot_nt(dob, vb), 0.0).astype(BF)
                dstb = dst.astype(BF)
                dqi = _dot(da, k_inb) + _dot(dob, st.astype(BF))
                dki = _dot_tn(da, q_inb)
                dv = _dot_tn(att, dob) + _dot_nt(k_stb, dstb)
                dks = _dot(vb, dstb)
                dd = jnp.sum(dst * st, axis=0, keepdims=True)
                dst_scr[h] = dst * dec + _dot_tn(dob, q_inb)
                dq = dqi * eb
                dk = dki * enb + dks * ek
                kk = dks * k_st
                dbl = jnp.sum(kk, axis=0, keepdims=True) + dd * dec
                db = dqi * q_in - dki * k_in - kk
                dbs.append(db + jnp.where(last_row, dbl, 0.0))
                dpg_ref[rows, h * GDK:(h + 1) * GDK] = dq.astype(BF)
                dpg_ref[rows, 512 + h * GDK:512 + (h + 1) * GDK] = dk.astype(BF)
                dpg_ref[rows, 1024 + h * GDV:1024 + (h + 1) * GDV] = dv.astype(BF)
                dpg_ref[rows, 2048 + h * GDV:2048 + (h + 1) * GDV] = dog.astype(BF)
            dla = _tri_mm(utri_v, jnp.concatenate(dbs, axis=1))
            dlogit = dla * (1.0 / GTAU) * _sigmoid(-logit)
            dlb = dlogit.astype(BF)
            dpg_ref[rows, 3072:3200] = _dot_nt(dlb, wg_ref[...]).astype(BF)
            dwg_ref[...] += _dot_tn(r, dlb)
            dbg_ref[...] += jnp.sum(dlogit, axis=0, keepdims=True)
            return carry

        lax.fori_loop(0, nc, chunk, 0)

    full = lambda shp: pl.BlockSpec(shp, lambda b_, i: (0,) * len(shp))
    rev = lambda b_, i: (b_ * nb + nb - 1 - i, 0)
    return _call(
        body, name="gla_bwd", ride=ride, sem=("arbitrary", "arbitrary"),
        args=(pg, wg, bg, gn, ltri, utri, o, states, dzg),
        out_shape=(jax.ShapeDtypeStruct((T, PG_W), BF),
                   jax.ShapeDtypeStruct((128, 512), F32),
                   jax.ShapeDtypeStruct((1, 512), F32),
                   jax.ShapeDtypeStruct((1, GDV), F32)),
        grid=(nseq, nb),
        in_specs=[pl.BlockSpec((tm, PG_W), rev),
                  full((128, 512)), full((1, 512)), full((1, GDV)), full((GC, GC)), full((GC, GC)),
                  pl.BlockSpec((tm, GH * GDV), rev),
                  pl.BlockSpec((nc, GH, GDV, GDK), lambda b_, i: (b_ * nb + nb - 1 - i, 0, 0, 0)),
                  pl.BlockSpec((tm, GH * GDV), rev)],
        out_specs=(pl.BlockSpec((tm, PG_W), rev), full((128, 512)), full((1, 512)), full((1, GDV))),
        scratch_shapes=[pltpu.VMEM((GH, GDV, GDK), F32)])


def _rope_tables(pos, invf):
    ang = pos.astype(F32) * invf
    lane = _iota(ang.shape, 1)
    sin = jnp.sin(ang)
    ssin = jnp.where(lane < 32, -sin, jnp.where(lane < 64, sin, 0.0))
    return jnp.cos(ang), ssin, lane


def _rope(x, cos, ssin, lane, sign):
    rot = jnp.where(lane < 32, pltpu.roll(x, 96, 1), pltpu.roll(x, 32, 1))
    return x * cos + sign * (rot * ssin)


def _rms_fwd(x, g):
    rstd = lax.rsqrt(jnp.mean(x * x, axis=-1, keepdims=True) + RMS_EPS)
    return x * rstd * g, x * rstd, rstd


def _rms_bwd(dy, xhat, rstd, g):
    gd = dy * g
    return rstd * (gd - xhat * jnp.mean(gd * xhat, axis=-1, keepdims=True)), jnp.sum(dy * xhat, axis=0, keepdims=True)


def _mla_prep_fwd(pm, pos, invf, gq, gkv, wuq, wukv, *, tm):
    T = pm.shape[0]

    def body(pm_ref, pos_ref, invf_ref, gq_ref, gkv_ref, wuq_ref, wukv_ref, qc_ref, kc_ref, v_ref):
        cos, ssin, lane = _rope_tables(pos_ref[...], invf_ref[...])
        cq, _, _ = _rms_fwd(pm_ref[:, 0:MQR], gq_ref[...])
        ckv, _, _ = _rms_fwd(pm_ref[:, 512:768], gkv_ref[...])
        qf = _dot(cq.astype(BF), wuq_ref[...])
        kvf = _dot(ckv.astype(BF), wukv_ref[...])
        kr = _rope(pm_ref[:, 384:512], cos, ssin, lane, 1.0).astype(BF)
        for h in range(MH):
            qc_ref[:, 256 * h:256 * h + 128] = qf[:, 128 * h:128 * h + 128].astype(BF)
            qr = qf[:, 1024 + 128 * h:1024 + 128 * h + 128]
            qc_ref[:, 256 * h + 128:256 * h + 256] = _rope(qr, cos, ssin, lane, 1.0).astype(BF)
            kc_ref[:, 256 * h:256 * h + 128] = kvf[:, 128 * h:128 * h + 128].astype(BF)
            kc_ref[:, 256 * h + 128:256 * h + 256] = kr
        v_ref[...] = kvf[:, 1024:2048].astype(BF)

    full = lambda shp: pl.BlockSpec(shp, lambda i: (0,) * len(shp))
    row = lambda w: pl.BlockSpec((tm, w), lambda i: (i, 0))
    return pl.pallas_call(
        body, name="mla_prep_fwd",
        out_shape=(jax.ShapeDtypeStruct((T, MH * 256), BF), jax.ShapeDtypeStruct((T, MH * 256), BF),
                   jax.ShapeDtypeStruct((T, MH * MV), BF)),
        grid=(T // tm,),
        in_specs=[row(PM_W), row(1), full((1, 128)), full((1, MQR)), full((1, MKR)),
                  full((MQR, 2048)), full((MKR, 2048))],
        out_specs=(row(MH * 256), row(MH * 256), row(MH * MV)),
        compiler_params=_params(("parallel",)),
    )(pm, pos, invf, gq, gkv, wuq, wukv)


def _mla_prep_bwd(pm, pos, invf, gq, gkv, wuq, wukv, dqc, dkc, dv, *, tm):
    T = pm.shape[0]

    def body(pm_ref, pos_ref, invf_ref, gq_ref, gkv_ref, wuq_ref, wukv_ref, dqc_ref, dkc_ref, dv_ref,
             dpm_ref, dwuq_ref, dwukv_ref, dgq_ref, dgkv_ref):
        @pl.when(pl.program_id(0) == 0)
        def _():
            dwuq_ref[...] = jnp.zeros_like(dwuq_ref)
            dwukv_ref[...] = jnp.zeros_like(dwukv_ref)
            dgq_ref[...] = jnp.zeros_like(dgq_ref)
            dgkv_ref[...] = jnp.zeros_like(dgkv_ref)

        cos, ssin, lane = _rope_tables(pos_ref[...], invf_ref[...])
        cq, cqh, cq_rstd = _rms_fwd(pm_ref[:, 0:MQR], gq_ref[...])
        ckv, ckvh, ckv_rstd = _rms_fwd(pm_ref[:, 512:768], gkv_ref[...])
        dqn, dqr, dkn = [], [], []
        dkr = jnp.zeros((tm, 128), F32)
        for h in range(MH):
            dqn.append(dqc_ref[:, 256 * h:256 * h + 128].astype(BF))
            dqr.append(_rope(dqc_ref[:, 256 * h + 128:256 * h + 256], cos, ssin, lane, -1.0).astype(BF))
            dkn.append(dkc_ref[:, 256 * h:256 * h + 128].astype(BF))
            dkr = dkr + dkc_ref[:, 256 * h + 128:256 * h + 256]
        dqf = jnp.concatenate(dqn + dqr, axis=1)
        dkvf = jnp.concatenate(dkn + [dv_ref[...].astype(BF)], axis=1)
        dwuq_ref[...] += _dot_tn(cq.astype(BF), dqf)
        dwukv_ref[...] += _dot_tn(ckv.astype(BF), dkvf)
        dcq, dgq = _rms_bwd(_dot_nt(dqf, wuq_ref[...]), cqh, cq_rstd, gq_ref[...])
        dckv, dgkv = _rms_bwd(_dot_nt(dkvf, wukv_ref[...]), ckvh, ckv_rstd, gkv_ref[...])
        dgq_ref[...] += dgq
        dgkv_ref[...] += dgkv
        dpm_ref[:, 0:MQR] = dcq.astype(BF)
        dpm_ref[:, 384:512] = _rope(dkr, cos, ssin, lane, -1.0).astype(BF)
        dpm_ref[:, 512:768] = dckv.astype(BF)

    full = lambda shp: pl.BlockSpec(shp, lambda i: (0,) * len(shp))
    row = lambda w: pl.BlockSpec((tm, w), lambda i: (i, 0))
    return pl.pallas_call(
        body, name="mla_prep_bwd",
        out_shape=(jax.ShapeDtypeStruct((T, PM_W), BF), jax.ShapeDtypeStruct((MQR, 2048), F32),
                   jax.ShapeDtypeStruct((MKR, 2048), F32), jax.ShapeDtypeStruct((1, MQR), F32),
                   jax.ShapeDtypeStruct((1, MKR), F32)),
        grid=(T // tm,),
        in_specs=[row(PM_W), row(1), full((1, 128)), full((1, MQR)), full((1, MKR)),
                  full((MQR, 2048)), full((MKR, 2048)), row(MH * 256), row(MH * 256), row(MH * MV)],
        out_specs=(row(PM_W), full((MQR, 2048)), full((MKR, 2048)), full((1, MQR)), full((1, MKR))),
        compiler_params=_params(("arbitrary",)),
    )(pm, pos, invf, gq, gkv, wuq, wukv, dqc, dkc, dv)


def _flash_fwd(qc, kc, v, *, nseq, S, tq, ride=None):
    T = qc.shape[0]
    nq = S // tq
    scale = (NOPE + ROPE) ** -0.5

    def body(q_ref, k_ref, v_ref, o_ref, lse_ref):
        i = pl.program_id(2)
        q = q_ref[...]
        causal = _iota((tq, tq), 0) >= _iota((tq, tq), 1)

        def step(j, carry, masked):
            m, l, acc = carry
            rows = pl.ds(pl.multiple_of(j * tq, tq), tq)
            s = _dot_nt(q, k_ref[rows, :]) * scale
            if masked:
                s = jnp.where(causal, s, NEG)
            m_new = jnp.maximum(m, jnp.max(s, axis=-1, keepdims=True))
            p = jnp.exp(s - m_new)
            a = jnp.exp(m - m_new)
            l = a * l + jnp.sum(p, axis=-1, keepdims=True)
            acc = a * acc + _dot(p.astype(BF), v_ref[rows, :])
            return m_new, l, acc

        init = (jnp.full((tq, 1), NEG, F32), jnp.zeros((tq, 1), F32), jnp.zeros((tq, MV), F32))
        carry = lax.fori_loop(0, i, lambda j, c: step(j, c, False), init)
        m, l, acc = step(i, carry, True)
        o_ref[...] = (acc / l).astype(BF)
        lse_ref[...] = jnp.broadcast_to(m + jnp.log(l), (tq, 128))

    return _call(
        body, name="flash_fwd", ride=ride, sem=("parallel", "parallel", "arbitrary"), args=(qc, kc, v),
        out_shape=(jax.ShapeDtypeStruct((T, MH * MV), BF), jax.ShapeDtypeStruct((T, MH * 128), F32)),
        grid=(nseq, MH, nq),
        in_specs=[pl.BlockSpec((tq, 256), lambda b_, h, i: (b_ * nq + i, h)),
                  pl.BlockSpec((S, 256), lambda b_, h, i: (b_, h)),
                  pl.BlockSpec((S, MV), lambda b_, h, i: (b_, h))],
        out_specs=(pl.BlockSpec((tq, MV), lambda b_, h, i: (b_ * nq + i, h)),
                   pl.BlockSpec((tq, 128), lambda b_, h, i: (b_ * nq + i, h))))


def _flash_bwd(qc, kc, v, o, do, lse, *, nseq, S, tq, ride=None):
    T = qc.shape[0]
    nq = S // tq
    scale = (NOPE + ROPE) ** -0.5

    def body(q_ref, k_ref, v_ref, o_ref, do_ref, lse_ref, dq_ref, dk_ref, dv_ref, dq_scr, delta_scr):
        j = pl.program_id(2)

        @pl.when(j == 0)
        def _():
            dq_scr[...] = jnp.zeros_like(dq_scr)
            delta = jnp.sum(o_ref[...].astype(F32) * do_ref[...].astype(F32), axis=-1, keepdims=True)
            delta_scr[...] = jnp.broadcast_to(delta, (S, 128))

        kb = k_ref[...]
        vb = v_ref[...]
        causal = _iota((tq, tq), 0) >= _iota((tq, tq), 1)

        def step(i, carry, masked):
            dk, dv = carry
            rows = pl.ds(pl.multiple_of(i * tq, tq), tq)
            q = q_ref[rows, :]
            dob = do_ref[rows, :]
            s = _dot_nt(q, kb) * scale
            p = jnp.exp(s - lse_ref[rows, 0:1])
            if masked:
                p = jnp.where(causal, p, 0.0)
            dv = dv + _dot_tn(p.astype(BF), dob)
            dp = _dot_nt(dob, vb)
            ds = (p * (dp - delta_scr[rows, 0:1]) * scale).astype(BF)
            dk = dk + _dot_tn(ds, q)
            dq_scr[rows, :] += _dot(ds, kb)
            return dk, dv

        carry = step(j, (jnp.zeros((tq, 256), F32), jnp.zeros((tq, MV), F32)), True)
        dk, dv = lax.fori_loop(j + 1, nq, lambda i, c: step(i, c, False), carry)
        dk_ref[...] = dk
        dv_ref[...] = dv

        @pl.when(j == nq - 1)
        def _():
            dq_ref[...] = dq_scr[...]

    seq = lambda w: pl.BlockSpec((S, w), lambda b_, h, j: (b_, h))
    blk = lambda w: pl.BlockSpec((tq, w), lambda b_, h, j: (b_ * nq + j, h))
    return _call(
        body, name="flash_bwd", ride=ride, sem=("parallel", "parallel", "arbitrary"), args=(qc, kc, v, o, do, lse),
        out_shape=(jax.ShapeDtypeStruct((T, MH * 256), F32), jax.ShapeDtypeStruct((T, MH * 256), F32),
                   jax.ShapeDtypeStruct((T, MH * MV), F32)),
        grid=(nseq, MH, nq),
        in_specs=[seq(256), blk(256), blk(MV), seq(MV), seq(MV), seq(128)],
        out_specs=(seq(256), blk(256), blk(MV)),
        scratch_shapes=[pltpu.VMEM((S, 256), F32), pltpu.VMEM((S, 128), F32)])


def _ln_fwd(pre, g, b):
    mu = jnp.mean(pre, axis=-1, keepdims=True)
    xc = pre - mu
    rstd = lax.rsqrt(jnp.mean(xc * xc, axis=-1, keepdims=True) + LN_EPS)
    xhat = xc * rstd
    return xhat * g + b, xhat, rstd


def _ln_bwd(dy, xhat, rstd, g):
    dxh = dy * g
    dx = rstd * (dxh - jnp.mean(dxh, axis=-1, keepdims=True) - xhat * jnp.mean(dxh * xhat, axis=-1, keepdims=True))
    return dx, jnp.sum(dy * xhat, axis=0, keepdims=True), jnp.sum(dy, axis=0, keepdims=True)


def _post_attn_fwd(zg, attn, pt, x, wgo, wmo, wout, g1, b1, *, tm):
    T = x.shape[0]

    def body(zg_ref, at_ref, pt_ref, x_ref, wgo_ref, wmo_ref, wout_ref, g_ref, b_ref,
             yg_ref, ym_ref, mix_ref, pre_ref, h_ref, hb_ref):
        yg = _dot(zg_ref[...], wgo_ref[...])
        ym = _dot(at_ref[...], wmo_ref[...])
        mix = (_sigmoid(pt_ref[:, 0:D]) * yg + _sigmoid(pt_ref[:, D:2 * D]) * ym).astype(BF)
        pre = ALPHA * x_ref[...] + _dot(mix, wout_ref[...])
        h, _, _ = _ln_fwd(pre, g_ref[...], b_ref[...])
        yg_ref[...] = yg
        ym_ref[...] = ym
        mix_ref[...] = mix
        pre_ref[...] = pre
        h_ref[...] = h
        hb_ref[...] = h.astype(BF)

    full = lambda shp: pl.BlockSpec(shp, lambda i: (0,) * len(shp))
    row = lambda w: pl.BlockSpec((tm, w), lambda i: (i, 0))
    sd = lambda dt: jax.ShapeDtypeStruct((T, D), dt)
    return pl.pallas_call(
        body, name="post_attn_fwd",
        out_shape=(sd(F32), sd(F32), sd(BF), sd(F32), sd(F32), sd(BF)),
        grid=(T // tm,),
        in_specs=[row(D), row(D), row(PT_W), row(D), full((D, D)), full((D, D)), full((D, D)),
                  full((1, D)), full((1, D))],
        out_specs=(row(D),) * 6,
        compiler_params=_params(("parallel",)),
    )(zg, attn, pt, x, wgo, wmo, wout, g1, b1)


def _post_attn_bwd(dh, pre, pt, yg, ym, wgo, wmo, wout, g1, *, tm):
    T = dh.shape[0]

    def body(dh_ref, pre_ref, pt_ref, yg_ref, ym_ref, wgo_ref, wmo_ref, wout_ref, g_ref,
             dx_ref, dpreb_ref, dpt_ref, dygb_ref, dymb_ref, dzg_ref, dat_ref, dg_ref, db_ref):
        @pl.when(pl.program_id(0) == 0)
        def _():
            dg_ref[...] = jnp.zeros_like(dg_ref)
            db_ref[...] = jnp.zeros_like(db_ref)

        pre = pre_ref[...]
        mu = jnp.mean(pre, axis=-1, keepdims=True)
        xc = pre - mu
        rstd = lax.rsqrt(jnp.mean(xc * xc, axis=-1, keepdims=True) + LN_EPS)
        dpre, dg, db = _ln_bwd(dh_ref[...], xc * rstd, rstd, g_ref[...])
        dg_ref[...] += dg
        db_ref[...] += db
        dx_ref[...] = ALPHA * dpre
        dpreb = dpre.astype(BF)
        dpreb_ref[...] = dpreb
        dmix = _dot_nt(dpreb, wout_ref[...])
        sa = _sigmoid(pt_ref[:, 0:D])
        sb = _sigmoid(pt_ref[:, D:2 * D])
        dpt_ref[:, 0:D] = (dmix * yg_ref[...] * (sa * (1.0 - sa))).astype(BF)
        dpt_ref[:, D:2 * D] = (dmix * ym_ref[...] * (sb * (1.0 - sb))).astype(BF)
        dyg = (dmix * sa).astype(BF)
        dym = (dmix * sb).astype(BF)
        dygb_ref[...] = dyg
        dymb_ref[...] = dym
        dzg_ref[...] = _dot_nt(dyg, wgo_ref[...]).astype(BF)
        dat_ref[...] = _dot_nt(dym, wmo_ref[...]).astype(BF)

    full = lambda shp: pl.BlockSpec(shp, lambda i: (0,) * len(shp))
    row = lambda w: pl.BlockSpec((tm, w), lambda i: (i, 0))
    sd = lambda w, dt: jax.ShapeDtypeStruct((T, w), dt)
    return pl.pallas_call(
        body, name="post_attn_bwd",
        out_shape=(sd(D, F32), sd(D, BF), sd(PT_W, BF), sd(D, BF), sd(D, BF), sd(D, BF), sd(D, BF),
                   jax.ShapeDtypeStruct((1, D), F32), jax.ShapeDtypeStruct((1, D), F32)),
        grid=(T // tm,),
        in_specs=[row(D), row(D), row(PT_W), row(D), row(D), full((D, D)), full((D, D)), full((D, D)),
                  full((1, D))],
        out_specs=(row(D), row(D), row(PT_W), row(D), row(D), row(D), row(D), full((1, D)), full((1, D))),
        compiler_params=_params(("arbitrary",)),
    )(dh, pre, pt, yg, ym, wgo, wmo, wout, g1)


def _shift_down(u, prev, k):
    r = pltpu.roll(u, k, 0)
    p = pltpu.roll(prev, k, 0)
    head = jnp.where(_iota(p.shape, 0) < k, p, r[0:8, :])
    return jnp.concatenate([head, r[8:, :]], axis=0)


def _shift_up(u, nxt, k):
    n = u.shape[0]
    r = pltpu.roll(u, n - k, 0)
    p = pltpu.roll(nxt, 8 - k, 0)
    tail = jnp.where(_iota(p.shape, 0) >= 8 - k, p, r[n - 8:, :])
    return jnp.concatenate([r[:n - 8, :], tail], axis=0)


def _conv3(u, prev, w_ref, b_ref):
    return (w_ref[0:1, :] * _shift_down(u, prev, 2) + w_ref[1:2, :] * _shift_down(u, prev, 1)
            + w_ref[2:3, :] * u + b_ref[...])


def _ffn_up_fwd(hb, wugt, wuvt, cw, cb, *, S, tm, tn):
    T = hb.shape[0]
    nj, nbs = DFF // tn, S // tm

    def body(h_ref, wg_ref, wv_ref, cwg_ref, cwv_ref, cbg_ref, cbv_ref, ug_ref, uv_ref, f_ref, pg_scr, pv_scr):
        @pl.when(pl.program_id(1) % nbs == 0)
        def _():
            pg_scr[...] = jnp.zeros_like(pg_scr)
            pv_scr[...] = jnp.zeros_like(pv_scr)

        h = h_ref[...]
        ug = _dot_nt(h, wg_ref[...])
        uv = _dot_nt(h, wv_ref[...])
        ucg = _conv3(ug, pg_scr[...], cwg_ref, cbg_ref)
        ucv = _conv3(uv, pv_scr[...], cwv_ref, cbv_ref)
        pg_scr[...] = ug[tm - 8:, :]
        pv_scr[...] = uv[tm - 8:, :]
        ug_ref[...] = ug
        uv_ref[...] = uv
        f_ref[...] = (ucg * _sigmoid(ucg) * ucv).astype(BF)

    tile = pl.BlockSpec((tm, tn), lambda j, i: (i, j))
    return pl.pallas_call(
        body, name="ffn_up_fwd",
        out_shape=(jax.ShapeDtypeStruct((T, DFF), F32), jax.ShapeDtypeStruct((T, DFF), F32),
                   jax.ShapeDtypeStruct((T, DFF), BF)),
        grid=(nj, T // tm),
        in_specs=[pl.BlockSpec((tm, D), lambda j, i: (i, 0)),
                  pl.BlockSpec((tn, D), lambda j, i: (j, 0)), pl.BlockSpec((tn, D), lambda j, i: (j, 0)),
                  pl.BlockSpec((3, tn), lambda j, i: (0, j)), pl.BlockSpec((3, tn), lambda j, i: (0, j + nj)),
                  pl.BlockSpec((1, tn), lambda j, i: (0, j)), pl.BlockSpec((1, tn), lambda j, i: (0, j + nj))],
        out_specs=(tile, tile, tile),
        scratch_shapes=[pltpu.VMEM((8, tn), F32), pltpu.VMEM((8, tn), F32)],
        compiler_params=_params(("parallel", "arbitrary")),
    )(hb, wugt, wuvt, cw, cw, cb, cb)


def _ffn_bwd(dpreb, wd, ug, uv, cw, cb, *, S, tm, tn):
    T = dpreb.shape[0]
    nj, nb, nbs = DFF // tn, T // tm, S // tm
    hb8 = tm // 8

    def body(dp_ref, wd_ref, ug_ref, uv_ref, hg_ref, hv_ref, cwg_ref, cwv_ref, cbg_ref, cbv_ref,
             dug_ref, duv_ref, dcg_ref, dcv_ref, ng_scr, nv_scr):
        ii = pl.program_id(1)
        i = nb - 1 - ii

        @pl.when(ii == 0)
        def _():
            dcg_ref[...] = jnp.zeros_like(dcg_ref)
            dcv_ref[...] = jnp.zeros_like(dcv_ref)

        @pl.when(i % nbs == nbs - 1)
        def _():
            ng_scr[...] = jnp.zeros_like(ng_scr)
            nv_scr[...] = jnp.zeros_like(nv_scr)

        seq_start = i % nbs == 0
        df = _dot_nt(dp_ref[...], wd_ref[...])

        def half(u_ref, halo_ref, cw_ref, cb_ref):
            u = u_ref[...]
            prev = jnp.where(seq_start, 0.0, halo_ref[...])
            u1 = _shift_down(u, prev, 1)
            u2 = _shift_down(u, prev, 2)
            return u, u1, u2, cw_ref, cw_ref[0:1, :] * u2 + cw_ref[1:2, :] * u1 + cw_ref[2:3, :] * u + cb_ref[...]

        g_u, g_u1, g_u2, g_w, ucg = half(ug_ref, hg_ref, cwg_ref, cbg_ref)
        v_u, v_u1, v_u2, v_w, ucv = half(uv_ref, hv_ref, cwv_ref, cbv_ref)
        sg = _sigmoid(ucg)
        ducg = df * ucv * (sg * (1.0 + ucg * (1.0 - sg)))
        ducv = df * (ucg * sg)

        def finish(duc, u, u1, u2, w, nxt_scr, du_ref, dc_ref):
            nxt = nxt_scr[...]
            du = w[2:3, :] * duc + w[1:2, :] * _shift_up(duc, nxt, 1) + w[0:1, :] * _shift_up(duc, nxt, 2)
            du_ref[...] = du.astype(BF)
            nxt_scr[...] = duc[0:8, :]
            for row, z in enumerate((u2 * duc, u1 * duc, u * duc, duc)):
                dc_ref[row:row + 1, :] += jnp.sum(z, axis=0, keepdims=True)

        finish(ducg, g_u, g_u1, g_u2, g_w, ng_scr, dug_ref, dcg_ref)
        finish(ducv, v_u, v_u1, v_u2, v_w, nv_scr, duv_ref, dcv_ref)

    tile = pl.BlockSpec((tm, tn), lambda j, ii: (nb - 1 - ii, j))
    halo = pl.BlockSpec((8, tn), lambda j, ii: (jnp.maximum((nb - 1 - ii) * hb8 - 1, 0), j))
    acc = pl.BlockSpec((8, tn), lambda j, ii: (0, j))
    return pl.pallas_call(
        body, name="ffn_bwd",
        out_shape=(jax.ShapeDtypeStruct((T, DFF), BF), jax.ShapeDtypeStruct((T, DFF), BF),
                   jax.ShapeDtypeStruct((8, DFF), F32), jax.ShapeDtypeStruct((8, DFF), F32)),
        grid=(nj, nb),
        in_specs=[pl.BlockSpec((tm, D), lambda j, ii: (nb - 1 - ii, 0)),
                  pl.BlockSpec((tn, D), lambda j, ii: (j, 0)),
                  tile, tile, halo, halo,
                  pl.BlockSpec((3, tn), lambda j, ii: (0, j)), pl.BlockSpec((3, tn), lambda j, ii: (0, j + nj)),
                  pl.BlockSpec((1, tn), lambda j, ii: (0, j)), pl.BlockSpec((1, tn), lambda j, ii: (0, j + nj))],
        out_specs=(tile, tile, acc, acc),
        scratch_shapes=[pltpu.VMEM((8, tn), F32), pltpu.VMEM((8, tn), F32)],
        compiler_params=_params(("parallel", "arbitrary")),
    )(dpreb, wd, ug, uv, ug, uv, cw, cw, cb, cb)


def _down_ln2_loss(f_in, wd, h, target, g2, b2, *, tm):
    T = h.shape[0]

    def body(f_ref, wd_ref, h_ref, t_ref, g_ref, b_ref, dpb_ref, dh_ref, loss_ref, dg_ref, db_ref):
        @pl.when(pl.program_id(0) == 0)
        def _():
            loss_ref[...] = jnp.zeros_like(loss_ref)
            dg_ref[...] = jnp.zeros_like(dg_ref)
            db_ref[...] = jnp.zeros_like(db_ref)

        pre = ALPHA * h_ref[...] + _dot(f_ref[...], wd_ref[...])
        out, xhat, rstd = _ln_fwd(pre, g_ref[...], b_ref[...])
        diff = out - t_ref[...]
        loss_ref[...] += 0.5 * jnp.sum(jnp.mean(diff * diff, axis=-1, keepdims=True))
        dpre, dg, db = _ln_bwd(diff * (1.0 / D), xhat, rstd, g_ref[...])
        dg_ref[...] += dg
        db_ref[...] += db
        dpb_ref[...] = dpre.astype(BF)
        dh_ref[...] = ALPHA * dpre

    full = lambda shp: pl.BlockSpec(shp, lambda i: (0,) * len(shp))
    row = lambda w: pl.BlockSpec((tm, w), lambda i: (i, 0))
    return pl.pallas_call(
        body, name="down_ln2_loss",
        out_shape=(jax.ShapeDtypeStruct((T, D), BF), jax.ShapeDtypeStruct((T, D), F32),
                   jax.ShapeDtypeStruct((8, 128), F32), jax.ShapeDtypeStruct((1, D), F32),
                   jax.ShapeDtypeStruct((1, D), F32)),
        grid=(T // tm,),
        in_specs=[row(DFF), full((DFF, D)), row(D), row(D), full((1, D)), full((1, D))],
        out_specs=(row(D), row(D), full((8, 128)), full((1, D)), full((1, D))),
        compiler_params=_params(("arbitrary",)),
    )(f_in, wd, h, target, g2, b2)


def _adamw(parts, w, m, v, *, name):
    n, R, C = parts.shape
    tr, tc = R, C
    for cand in range(min(R, 256), 15, -1):
        if R % cand == 0 and cand % 16 == 0:
            tr = cand
            break
    if tr == R and R * C > 65536 and C % 256 == 0:
        tc = 256
    c1 = 1.0 - ADAM_B1 ** ADAM_STEP
    c2 = 1.0 - ADAM_B2 ** ADAM_STEP

    def body(p_ref, w_ref, m_ref, v_ref, g_ref, d_ref, nm_ref, nv_ref):
        g = p_ref[0].astype(F32)
        for s in range(1, n):
            g = g + p_ref[s].astype(F32)
        nm = ADAM_B1 * m_ref[...] + (1.0 - ADAM_B1) * g
        nv = ADAM_B2 * v_ref[...] + (1.0 - ADAM_B2) * (g * g)
        g_ref[...] = g
        nm_ref[...] = nm
        nv_ref[...] = nv
        d_ref[...] = -ADAM_LR * ((nm / c1) / (jnp.sqrt(nv / c2) + ADAM_EPS) + ADAM_WD * w_ref[...])

    blk = pl.BlockSpec((tr, tc), lambda i, j: (i, j))
    sd = jax.ShapeDtypeStruct((R, C), F32)
    return pl.pallas_call(
        body, name=name,
        out_shape=(sd, sd, sd, sd),
        grid=(R // tr, C // tc),
        in_specs=[pl.BlockSpec((n, tr, tc), lambda i, j: (0, i, j)), blk, blk, blk],
        out_specs=(blk, blk, blk, blk),
        compiler_params=_params(("parallel", "parallel")),
    )(parts, w, m, v)


class _Exchange:
    def __init__(self, items):
        self.items = items
        self.n = len(items)

    def out_shape(self):
        return tuple(jax.ShapeDtypeStruct(a.shape if sc else (NDEV,) + a.shape, a.dtype) for a, sc in self.items)

    def scratch(self):
        return [pltpu.SemaphoreType.DMA((self.n, NDEV - 1)), pltpu.SemaphoreType.DMA((self.n, NDEV - 1)),
                pltpu.SemaphoreType.DMA((self.n,))]

    def _copies(self, ins, outs, sems):
        send_sems, recv_sems, loc_sems = sems
        x, y, c = lax.axis_index("x"), lax.axis_index("y"), lax.axis_index("c")
        me = 4 * x + 2 * y + c
        flip = lambda p, d: 1 - p if d else p
        local, sent, landed = [], [], []
        for a, (_, sc) in enumerate(self.items):
            local.append(pltpu.make_async_copy(ins[a].at[me] if sc else ins[a], outs[a].at[me], loc_sems.at[a]))
        for k in range(1, NDEV):
            px, py, pc = flip(x, k & 4), flip(y, k & 2), flip(c, k & 1)
            peer = 4 * px + 2 * py + pc
            for a, (_, sc) in enumerate(self.items):
                src = ins[a].at[peer] if sc else ins[a]
                mk = functools.partial(pltpu.make_async_remote_copy, src_ref=src,
                                       send_sem=send_sems.at[a, k - 1], recv_sem=recv_sems.at[a, k - 1],
                                       device_id=(px, py, pc), device_id_type=MESH_ID)
                sent.append(mk(dst_ref=outs[a].at[me]))
                landed.append(mk(dst_ref=outs[a].at[peer]))
        return local, sent, landed

    def start(self, ins, outs, sems):
        local, sent, _ = self._copies(ins, outs, sems)
        for cp in local + sent:
            cp.start()

    def wait(self, ins, outs, sems):
        local, sent, landed = self._copies(ins, outs, sems)
        for cp in landed:
            cp.wait_recv()
        for cp in sent:
            cp.wait_send()
        for cp in local:
            cp.wait()


def _call(body, *, name, grid, in_specs, out_specs, out_shape, args, scratch_shapes=(), sem=None, ride=None):
    if ride is None:
        return pl.pallas_call(body, name=name, grid=grid, in_specs=list(in_specs), out_specs=tuple(out_specs),
                              out_shape=tuple(out_shape), scratch_shapes=list(scratch_shapes),
                              compiler_params=_params(sem))(*args)
    n_in, n_out, n_scr, ne = len(args), len(out_shape), len(scratch_shapes), ride.n

    def ride_body(*refs):
        ins, ex_in = refs[:n_in], refs[n_in:n_in + ne]
        o0 = n_in + ne
        outs, ex_out = refs[o0:o0 + n_out], refs[o0 + n_out:o0 + n_out + ne]
        scr = refs[o0 + n_out + ne:o0 + n_out + ne + n_scr]
        sems = refs[o0 + n_out + ne + n_scr:]
        first = functools.reduce(jnp.logical_and, [pl.program_id(d) == 0 for d in range(len(grid))])
        last = functools.reduce(jnp.logical_and, [pl.program_id(d) == grid[d] - 1 for d in range(len(grid))])

        @pl.when(first)
        def _():
            ride.start(ex_in, ex_out, sems)

        body(*ins, *outs, *scr)

        @pl.when(last)
        def _():
            ride.wait(ex_in, ex_out, sems)

    anyspec = pl.BlockSpec(memory_space=pl.ANY)
    res = pl.pallas_call(
        ride_body, name=name, grid=grid,
        in_specs=list(in_specs) + [anyspec] * ne,
        out_specs=tuple(out_specs) + (anyspec,) * ne,
        out_shape=tuple(out_shape) + ride.out_shape(),
        scratch_shapes=list(scratch_shapes) + ride.scratch(),
        compiler_params=_params(("arbitrary",) * len(grid)),
    )(*args, *[a for a, _ in ride.items])
    return tuple(res[:n_out]), tuple(res[n_out:])


def _gather_two_level(arrays, *, name):
    n = len(arrays)

    def body(*refs):
        ins, outs = refs[:n], refs[n:2 * n]
        send_sems, recv_sems, loc_sems = refs[2 * n:]
        x, y, c = lax.axis_index("x"), lax.axis_index("y"), lax.axis_index("c")
        sibling = (x, y, 1 - c)
        chips = [(1 - x, y), (x, 1 - y), (1 - x, 1 - y)]
        idx = lambda px, py, pc: 4 * px + 2 * py + pc
        me = idx(x, y, c)

        def copy(a, k, block, to, src=None):
            return pltpu.make_async_remote_copy(
                src_ref=outs[a].at[block] if src is None else src, dst_ref=outs[a].at[block],
                send_sem=send_sems.at[a, k], recv_sem=recv_sems.at[a, k], device_id=to, device_id_type=MESH_ID)

        local = [pltpu.make_async_copy(ins[a], outs[a].at[me], loc_sems.at[a]) for a in range(n)]
        sent = []
        for a in range(n):
            sent.append(copy(a, 0, me, sibling, src=ins[a]))
            sent += [copy(a, 1 + j, me, (*chip, c), src=ins[a]) for j, chip in enumerate(chips)]
        for cp in local + sent:
            cp.start()
        for j, chip in enumerate(chips):
            for a in range(n):
                copy(a, 1 + j, idx(*chip, c), sibling).wait_recv()
                passed = copy(a, 4 + j, idx(*chip, c), sibling)
                passed.start()
                sent.append(passed)
        for a in range(n):
            copy(a, 0, idx(x, y, 1 - c), sibling).wait_recv()
            for j, chip in enumerate(chips):
                copy(a, 4 + j, idx(*chip, 1 - c), sibling).wait_recv()
        for cp in sent:
            cp.wait_send()
        for cp in local:
            cp.wait()

    anyspec = pl.BlockSpec(memory_space=pl.ANY)
    return pl.pallas_call(
        body, name=name,
        out_shape=tuple(jax.ShapeDtypeStruct((NDEV,) + a.shape, a.dtype) for a in arrays),
        in_specs=[anyspec] * n, out_specs=(anyspec,) * n,
        scratch_shapes=[pltpu.SemaphoreType.DMA((n, NDEV - 1)), pltpu.SemaphoreType.DMA((n, NDEV - 1)),
                        pltpu.SemaphoreType.DMA((n,))],
    )(*arrays)


def _exchange(items, *, name):
    ex = _Exchange(items)

    def body(*refs):
        ex.start(refs[:ex.n], refs[ex.n:2 * ex.n], refs[2 * ex.n:])
        ex.wait(refs[:ex.n], refs[ex.n:2 * ex.n], refs[2 * ex.n:])

    anyspec = pl.BlockSpec(memory_space=pl.ANY)
    return pl.pallas_call(
        body, name=name, out_shape=ex.out_shape(), in_specs=[anyspec] * ex.n, out_specs=(anyspec,) * ex.n,
        scratch_shapes=ex.scratch(),
    )(*[a for a, _ in items])


def _tri_consts():
    r = lax.broadcasted_iota(jnp.int32, (GC, GC), 0)
    c = lax.broadcasted_iota(jnp.int32, (GC, GC), 1)
    return (r >= c).astype(BF), (r <= c).astype(BF)


def _local_step(x, positions, target, w, hooks=None):
    g = {}

    def run(host, fn, *a, **kw):
        h = None if hooks is None else hooks.get(host)
        if h is None:
            return fn(*a, **kw)
        out, received = fn(*a, ride=_Exchange(h[0](w, g)), **kw)
        h[1](received, w, g)
        return out

    nseq, S, _ = x.shape
    T = nseq * S
    tm = min(256, S)
    tq = min(512, S)
    x2 = x.reshape(T, D)
    pos = positions.reshape(T, 1)
    half = ROPE // 2
    inv = THETA ** (-jnp.arange(half, dtype=F32) / half)
    invf = jnp.concatenate([inv, inv, jnp.zeros((64,), F32)]).reshape(1, 128)
    ltri, utri = _tri_consts()

    pg = run("proj_g", _matmul, x2, w["w_gt"], "nt", name="proj_g", tm=1024, tn=640, tk=1024)
    pm = _matmul(x2, w["w_mt"], "nt", name="proj_m", tm=1024, tn=768, tk=1024)
    pt = _matmul(x2, w["w_tt"], "nt", name="proj_t", tm=1024, tn=1024, tk=1024)
    o, zg, states = _gla_fwd(pg, w["wg"], w["bg"], w["gn"], ltri, nseq=nseq, S=S, tm=tm)
    qc, kc, v = _mla_prep_fwd(pm, pos, invf, w["gq"], w["gkv"], w["wuq"], w["wukv"], tm=tm)
    attn, lse = run("flash_fwd", _flash_fwd, qc, kc, v, nseq=nseq, S=S, tq=tq)
    yg, ym, mix, pre1, h1, h1b = _post_attn_fwd(zg, attn, pt, x2, w["wgo"], w["wmo"], w["wout"],
                                                w["g1"], w["b1"], tm=tm)
    ug, uv, f_in = _ffn_up_fwd(h1b, w["wugt"], w["wuvt"], w["cw"], w["cb"], S=S, tm=tm, tn=1408)
    dpre2b, dh1, loss8, dg2, db2 = _down_ln2_loss(f_in, w["wd"], h1, target.reshape(T, D), w["g2"], w["b2"], tm=tm)

    dug, duv, dcg, dcv = _ffn_bwd(dpre2b, w["wd"], ug, uv, w["cw"], w["cb"], S=S, tm=tm, tn=1408)
    g["g2"], g["b2"], g["loss"] = dg2, db2, loss8[0:1, 0:1]
    g["cw"] = jnp.concatenate([dcg[0:3], dcv[0:3]], axis=1)
    g["cb"] = jnp.concatenate([dcg[3:4], dcv[3:4]], axis=1)
    g["wd"] = _matmul(f_in, dpre2b, "tn", name="dw_down", out_dtype=BF, tm=1408, tn=1024, tk=1024)
    g["wugt"] = _matmul(dug, h1b, "tn", name="dw_up_g", out_dtype=BF, tm=1408, tn=1024, tk=1024)
    g["wuvt"] = _matmul(duv, h1b, "tn", name="dw_up_v", out_dtype=BF, tm=1408, tn=1024, tk=1024)
    dh1 = _matmul(dug, w["wugt"], "nn", name="dh1_g", c_in=dh1, tm=1024, tn=1024, tk=1408)
    dh1 = _matmul(duv, w["wuvt"], "nn", name="dh1_v", c_in=dh1, tm=1024, tn=1024, tk=1408)
    dx, dpre1b, dpt, dygb, dymb, dzg, dattn, dg1, db1 = _post_attn_bwd(
        dh1, pre1, pt, yg, ym, w["wgo"], w["wmo"], w["wout"], w["g1"], tm=tm)
    g["g1"], g["b1"] = dg1, db1
    g["wout"] = _matmul(mix, dpre1b, "tn", name="dw_out", out_dtype=BF, tm=1024, tn=1024, tk=1024)
    g["wgo"] = _matmul(zg, dygb, "tn", name="dw_gla_o", out_dtype=BF, tm=1024, tn=1024, tk=1024)
    g["wmo"] = _matmul(attn, dymb, "tn", name="dw_mla_o", out_dtype=BF, tm=1024, tn=1024, tk=1024)
    dqc, dkc, dv = run("flash_bwd", _flash_bwd, qc, kc, v, attn, dattn, lse, nseq=nseq, S=S, tq=tq)
    dpm, g["wuq"], g["wukv"], g["gq"], g["gkv"] = _mla_prep_bwd(
        pm, pos, invf, w["gq"], w["gkv"], w["wuq"], w["wukv"], dqc, dkc, dv, tm=tm)
    dpg, g["wg"], g["bg"], g["gn"] = run("gla_bwd", _gla_bwd, pg, w["wg"], w["bg"], w["gn"], ltri, utri, o, states,
                                         dzg, nseq=nseq, S=S, tm=tm)
    g["w_gt"] = _matmul(dpg, x2, "tn", name="dw_in_g", out_dtype=BF, tm=640, tn=1024, tk=1024)
    g["w_mt"] = _matmul(dpm, x2, "tn", name="dw_in_m", out_dtype=BF, tm=768, tn=1024, tk=1024)
    g["w_tt"] = _matmul(dpt, x2, "tn", name="dw_in_t", out_dtype=BF, tm=1024, tn=1024, tk=1024)
    dx = run("dx", _matmul_sum, dx, [(dpg, w["w_gt"], 640), (dpm, w["w_mt"], 768), (dpt, w["w_tt"], 1024)],
             name="dx")
    return loss8[0, 0], dx.reshape(nseq, S, D), g


_IN_SPLITS = (512, 512, 1024, 16, 1024, 384, 256, 64, 1024, 1024)


def _w_in_to_groups(wt):
    offs = [0]
    for s in _IN_SPLITS:
        offs.append(offs[-1] + s)
    q, k, v, r, og, cq, ckv, kr, ga, gb = [wt[offs[i]:offs[i + 1]] for i in range(10)]
    z = lambda n: jnp.zeros((n, wt.shape[1]), wt.dtype)
    return (jnp.concatenate([q, k, v, og, r, z(112)], axis=0),
            jnp.concatenate([cq, kr, z(64), ckv], axis=0),
            jnp.concatenate([ga, gb], axis=0))


def _groups_to_w_in(g_g, g_m, g_t):
    q, k, v, og, r = g_g[0:512], g_g[512:1024], g_g[1024:2048], g_g[2048:3072], g_g[3072:3088]
    cq, kr, ckv = g_m[0:384], g_m[384:448], g_m[512:768]
    return jnp.concatenate([q, k, v, r, og, cq, ckv, kr, g_t], axis=0)


def _uq_to_kernel(wuq):
    w3 = wuq.reshape(MQR, MH, NOPE + ROPE)
    rope = jnp.concatenate([w3[:, :, NOPE:], jnp.zeros((MQR, MH, 64), wuq.dtype)], axis=2)
    return jnp.concatenate([w3[:, :, :NOPE].reshape(MQR, MH * 128), rope.reshape(MQR, MH * 128)], axis=1)


def _uq_from_kernel(g):
    nope = g[:, :1024].reshape(MQR, MH, 128)
    rope = g[:, 1024:].reshape(MQR, MH, 128)[:, :, :ROPE]
    return jnp.concatenate([nope, rope], axis=2)


def _ukv_to_kernel(wukv):
    w3 = wukv.reshape(MKR, MH, NOPE + MV)
    return jnp.concatenate([w3[:, :, :NOPE].reshape(MKR, MH * 128), w3[:, :, NOPE:].reshape(MKR, MH * 128)], axis=1)


def _ukv_from_kernel(g):
    return jnp.concatenate([g[:, :1024].reshape(MKR, MH, 128), g[:, 1024:].reshape(MKR, MH, 128)], axis=2)


def _cols_gathered(a):
    return a.transpose(1, 0, 2).reshape(a.shape[1], NDEV * a.shape[2])


def _cols_scattered(a):
    R = a.shape[0]
    return a.reshape(R, NDEV, a.shape[1] // NDEV).transpose(1, 0, 2)


_SMALL = (("gla_b_gate", 512), ("gla_norm_g", 256), ("mla_q_norm_g", 384), ("mla_kv_norm_g", 256),
          ("ln1_g", 1024), ("ln1_b", 1024), ("conv_b", 5632), ("ln2_g", 1024), ("ln2_b", 1024))
_SMALL_ROWS = 88
_SMALL_USED = sum(sz for _, sz in _SMALL)


def _pack_small(d):
    flat = jnp.concatenate([d[n].reshape(-1) for n, _ in _SMALL] + ([d['loss'].reshape(-1)] if 'loss' in d else []))
    return jnp.pad(flat, (0, _SMALL_ROWS * 128 - flat.shape[0])).reshape(_SMALL_ROWS, 128)


def _unpack_small(a):
    flat = a.reshape(-1)
    out, off = {}, 0
    for n, sz in _SMALL:
        out[n] = flat[off:off + sz].reshape(1, sz)
        off += sz
    return out


_NAMES = ['w_in', 'gla_w_gate_up', 'gla_b_gate', 'gla_norm_g', 'w_gla_o', 'mla_q_norm_g', 'mla_w_uq',
          'mla_kv_norm_g', 'mla_w_ukv', 'w_mla_o', 'w_out', 'ln1_g', 'ln1_b', 'w_up', 'conv_w', 'conv_b',
          'w_down', 'ln2_g', 'ln2_b']
_SHARDED = ['w_in', 'w_up', 'w_down', 'w_gla_o', 'w_mla_o', 'w_out', 'mla_w_uq', 'mla_w_ukv', 'gla_w_gate_up',
            'conv_w']


def kernel(x, positions, w_in, gla_w_gate_up, gla_b_gate, gla_norm_g, w_gla_o, mla_q_norm_g, mla_w_uq, mla_kv_norm_g, mla_w_ukv, w_mla_o, w_out, ln1_g, ln1_b, w_up, conv_w, conv_b, w_down, ln2_g, ln2_b, loss_target, m_w_in, m_gla_w_gate_up, m_gla_b_gate, m_gla_norm_g, m_w_gla_o, m_mla_q_norm_g, m_mla_w_uq, m_mla_kv_norm_g, m_mla_w_ukv, m_w_mla_o, m_w_out, m_ln1_g, m_ln1_b, m_w_up, m_conv_w, m_conv_b, m_w_down, m_ln2_g, m_ln2_b, v_w_in, v_gla_w_gate_up, v_gla_b_gate, v_gla_norm_g, v_w_gla_o, v_mla_q_norm_g, v_mla_w_uq, v_mla_kv_norm_g, v_mla_w_ukv, v_w_mla_o, v_w_out, v_ln1_g, v_ln1_b, v_w_up, v_conv_w, v_conv_b, v_w_down, v_ln2_g, v_ln2_b):
    W = dict(w_in=w_in, gla_w_gate_up=gla_w_gate_up, gla_b_gate=gla_b_gate, gla_norm_g=gla_norm_g, w_gla_o=w_gla_o, mla_q_norm_g=mla_q_norm_g, mla_w_uq=mla_w_uq, mla_kv_norm_g=mla_kv_norm_g, mla_w_ukv=mla_w_ukv, w_mla_o=w_mla_o, w_out=w_out, ln1_g=ln1_g, ln1_b=ln1_b, w_up=w_up, conv_w=conv_w, conv_b=conv_b, w_down=w_down, ln2_g=ln2_g, ln2_b=ln2_b)
    M = dict(w_in=m_w_in, gla_w_gate_up=m_gla_w_gate_up, gla_b_gate=m_gla_b_gate, gla_norm_g=m_gla_norm_g, w_gla_o=m_w_gla_o, mla_q_norm_g=m_mla_q_norm_g, mla_w_uq=m_mla_w_uq, mla_kv_norm_g=m_mla_kv_norm_g, mla_w_ukv=m_mla_w_ukv, w_mla_o=m_w_mla_o, w_out=m_w_out, ln1_g=m_ln1_g, ln1_b=m_ln1_b, w_up=m_w_up, conv_w=m_conv_w, conv_b=m_conv_b, w_down=m_w_down, ln2_g=m_ln2_g, ln2_b=m_ln2_b)
    V = dict(w_in=v_w_in, gla_w_gate_up=v_gla_w_gate_up, gla_b_gate=v_gla_b_gate, gla_norm_g=v_gla_norm_g, w_gla_o=v_w_gla_o, mla_q_norm_g=v_mla_q_norm_g, mla_w_uq=v_mla_w_uq, mla_kv_norm_g=v_mla_kv_norm_g, mla_w_ukv=v_mla_w_ukv, w_mla_o=v_w_mla_o, w_out=v_w_out, ln1_g=v_ln1_g, ln1_b=v_ln1_b, w_up=v_w_up, conv_w=v_conv_w, conv_b=v_conv_b, w_down=v_w_down, ln2_g=v_ln2_g, ln2_b=v_ln2_b)

    tshard = lambda d, n: d[n][0].T
    shard = lambda n: (W[n][0].astype(BF), False)
    first = ['w_in', 'mla_w_uq', 'mla_w_ukv', 'gla_w_gate_up']
    G = dict(zip(first, _gather_two_level(
        [tshard(W, 'w_in').astype(BF)] + [shard(n)[0] for n in first[1:]], name="gather_w0")))
    w_gt, w_mt, w_tt = _w_in_to_groups(G['w_in'].reshape(NDEV * 730, D))
    kw = dict(
        w_gt=w_gt, w_mt=w_mt, w_tt=w_tt,
        wg=jnp.pad(_cols_gathered(G['gla_w_gate_up']), ((0, 128 - GR), (0, 0))), bg=W['gla_b_gate'],
        gn=W['gla_norm_g'], gq=W['mla_q_norm_g'], gkv=W['mla_kv_norm_g'],
        wuq=_uq_to_kernel(_cols_gathered(G['mla_w_uq'])), wukv=_ukv_to_kernel(_cols_gathered(G['mla_w_ukv'])),
        g1=W['ln1_g'], b1=W['ln1_b'], g2=W['ln2_g'], b2=W['ln2_b'], cb=W['conv_b'],
    )
    received = {}

    def got_out_proj(ex, w, g):
        w.update(wgo=ex[0].reshape(D, D), wmo=ex[1].reshape(D, D), wout=ex[2].reshape(D, D))

    def got_ffn(ex, w, g):
        w_upt = ex[0].reshape(2 * DFF, D)
        w.update(wugt=w_upt[:DFF], wuvt=w_upt[DFF:], wd=ex[1].reshape(DFF, D), cw=_cols_gathered(ex[2]))

    slab = lambda a: (a.astype(BF), True)
    rows = lambda a: a.reshape(NDEV, a.shape[0] // NDEV, a.shape[1])

    def keep(names):
        return lambda ex, w, g: received.update(zip(names, ex))

    def small_grads(g):
        return _pack_small(dict(gla_b_gate=g['bg'], gla_norm_g=g['gn'], mla_q_norm_g=g['gq'], mla_kv_norm_g=g['gkv'],
                                ln1_g=g['g1'], ln1_b=g['b1'], conv_b=g['cb'], ln2_g=g['g2'], ln2_b=g['b2'],
                                loss=g['loss']))

    hooks = {
        "proj_g": (lambda w, g: [shard('w_gla_o'), shard('w_mla_o'), shard('w_out')], got_out_proj),
        "flash_fwd": (lambda w, g: [(tshard(W, 'w_up').astype(BF), False), shard('w_down'), (W['conv_w'][0], False)],
                      got_ffn),
        "flash_bwd": (lambda w, g: [slab(rows(g['wd'])), slab(rows(jnp.concatenate([g['wugt'], g['wuvt']], axis=0)))],
                      keep(['w_down', 'w_up'])),
        "gla_bwd": (lambda w, g: [slab(rows(g['wout'])), slab(rows(g['wgo'])), slab(rows(g['wmo'])),
                                  slab(_uq_from_kernel(g['wuq']).transpose(1, 0, 2)),
                                  slab(_ukv_from_kernel(g['wukv']).transpose(1, 0, 2))],
                    keep(['w_out', 'w_gla_o', 'w_mla_o', 'mla_w_uq', 'mla_w_ukv'])),
        "dx": (lambda w, g: [slab(rows(_groups_to_w_in(g['w_gt'], g['w_mt'], g['w_tt']))),
                             (_cols_scattered(g['wg'][:GR]), True), (_cols_scattered(g['cw']), True),
                             (small_grads(g), False)],
               keep(['w_in', 'gla_w_gate_up', 'conv_w', 'small'])),
    }

    _, grad_x, _ = _local_step(x, positions, loss_target, kw, hooks)

    grads, deltas, new_m, new_v = {}, {}, {}, {}
    small_parts = received['small']
    loss = jnp.sum(small_parts.reshape(NDEV, -1)[:, _SMALL_USED])
    for n in _SHARDED:
        shp = W[n].shape
        if n in ('w_in', 'w_up'):
            out = _adamw(received[n], tshard(W, n), tshard(M, n), tshard(V, n), name="adamw_" + n)
            grads[n], deltas[n], new_m[n], new_v[n] = [t.T.reshape(shp) for t in out]
            continue
        out = _adamw(received[n], W[n][0], M[n][0], V[n][0], name="adamw_" + n)
        grads[n], deltas[n], new_m[n], new_v[n] = [t.reshape(shp) for t in out]
    out = _adamw(small_parts, _pack_small(W), _pack_small(M), _pack_small(V), name="adamw_small")
    for dst, packed in zip((grads, deltas, new_m, new_v), out):
        dst.update(_unpack_small(packed))

    return (loss, grad_x, *[grads[n] for n in _NAMES], *[deltas[n] for n in _NAMES],
            *[new_m[n] for n in _NAMES], *[new_v[n] for n in _NAMES])
```

```python
import functools

import jax
import jax.numpy as jnp
from jax import lax
from jax.experimental import pallas as pl
from jax.experimental.pallas import tpu as pltpu

F32 = jnp.float32
BF = jnp.bfloat16

D = 1024
GH, GDK, GDV, GR, GTAU, GC = 4, 128, 256, 16, 16.0, 64
MH, MQR, MKR, NOPE, ROPE, MV = 8, 384, 256, 128, 64, 128
THETA = 10000.0
DFF = 2816
ALPHA = 2.0 ** 0.25
LN_EPS = 1e-5
RMS_EPS = 1e-6
NDEV = 8
ADAM_LR, ADAM_B1, ADAM_B2, ADAM_EPS, ADAM_WD, ADAM_STEP = 0.001, 0.9, 0.999, 1e-08, 0.01, 10

PG_W = 3200
PM_W = 768
PT_W = 2048
NEG = -1e30
MESH_ID = pl.DeviceIdType.MESH
VMEM_MB = 1024 * 1024


def _params(sem, vmem_mb=48):
    return pltpu.CompilerParams(dimension_semantics=sem, vmem_limit_bytes=vmem_mb * VMEM_MB)


def _dot(a, b):
    return lax.dot_general(a, b, (((1,), (0,)), ((), ())), preferred_element_type=F32)


def _dot_nt(a, b):
    return lax.dot_general(a, b, (((1,), (1,)), ((), ())), preferred_element_type=F32)


def _dot_tn(a, b):
    return lax.dot_general(a, b, (((0,), (0,)), ((), ())), preferred_element_type=F32)


def _iota(shape, dim):
    return lax.broadcasted_iota(jnp.int32, shape, dim)


FLASH_HP = 2


def _sigmoid(x):
    return 0.5 * jnp.tanh(0.5 * x) + 0.5


def _tri_mm(tri_bf, x):
    hi = x.astype(BF)
    r1 = x - hi.astype(F32)
    mid = r1.astype(BF)
    lo = (r1 - mid.astype(F32)).astype(BF)
    return _dot(tri_bf, hi) + _dot(tri_bf, mid) + _dot(tri_bf, lo)


def _matmul(a, b, mode, *, name, c_in=None, out_dtype=F32, tm=512, tn=512, tk=512, ride=None):
    if mode == "nn":
        (M, K), (_, N) = a.shape, b.shape
    elif mode == "nt":
        (M, K), (N, _) = a.shape, b.shape
    else:
        (K, M), (_, N) = a.shape, b.shape
    tm, tn, tk = min(tm, M), min(tn, N), min(tk, K)
    assert M % tm == 0 and N % tn == 0 and K % tk == 0, (name, M, N, K, tm, tn, tk)
    nk = K // tk
    dot = {"nn": _dot, "nt": _dot_nt, "tn": _dot_tn}[mode]

    def body(*refs):
        if c_in is None:
            a_ref, b_ref, o_ref, acc_ref = refs
        else:
            a_ref, b_ref, c_ref, o_ref, acc_ref = refs
        k = pl.program_id(2)

        @pl.when(k == 0)
        def _():
            if c_in is None:
                acc_ref[...] = jnp.zeros_like(acc_ref)
            else:
                acc_ref[...] = c_ref[...].astype(F32)

        acc_ref[...] += dot(a_ref[...].astype(BF), b_ref[...].astype(BF))

        @pl.when(k == nk - 1)
        def _():
            o_ref[...] = acc_ref[...].astype(out_dtype)

    if mode == "tn":
        a_spec = pl.BlockSpec((tk, tm), lambda i, j, k: (k, i))
    else:
        a_spec = pl.BlockSpec((tm, tk), lambda i, j, k: (i, k))
    if mode == "nt":
        b_spec = pl.BlockSpec((tn, tk), lambda i, j, k: (j, k))
    else:
        b_spec = pl.BlockSpec((tk, tn), lambda i, j, k: (k, j))
    in_specs = [a_spec, b_spec]
    args = [a, b]
    if c_in is not None:
        in_specs.append(pl.BlockSpec((tm, tn), lambda i, j, k: (i, j)))
        args.append(c_in)
    res = _call(
        body, name=name,
        out_shape=(jax.ShapeDtypeStruct((M, N), out_dtype),),
        grid=(M // tm, N // tn, nk),
        in_specs=in_specs,
        out_specs=(pl.BlockSpec((tm, tn), lambda i, j, k: (i, j)),),
        scratch_shapes=[pltpu.VMEM((tm, tn), F32)],
        sem=("parallel", "parallel", "arbitrary"), args=args, ride=ride)
    return res[0] if ride is None else (res[0][0], res[1])


def _matmul_sum(c_in, parts, *, name, tm=512, ride=None):
    M, N = c_in.shape
    tm = min(tm, M)
    n_p = len(parts)
    counts = [a.shape[1] // tk for a, _, tk in parts]
    starts = [sum(counts[:p]) for p in range(n_p)]
    nk = sum(counts)

    def body(*refs):
        a_refs, w_refs = refs[:n_p], refs[n_p:2 * n_p]
        c_ref, o_ref, acc_ref = refs[2 * n_p:]
        k = pl.program_id(1)

        @pl.when(k == 0)
        def _():
            acc_ref[...] = c_ref[...]

        for p in range(n_p):
            @pl.when(jnp.logical_and(k >= starts[p], k < starts[p] + counts[p]))
            def _(p=p):
                acc_ref[...] += _dot(a_refs[p][...].astype(BF), w_refs[p][...].astype(BF))

        @pl.when(k == nk - 1)
        def _():
            o_ref[...] = acc_ref[...]

    def kidx(p):
        return lambda k: jnp.clip(k - starts[p], 0, counts[p] - 1)

    in_specs = [pl.BlockSpec((tm, tk), lambda i, k, f=kidx(p): (i, f(k))) for p, (_, _, tk) in enumerate(parts)]
    in_specs += [pl.BlockSpec((tk, N), lambda i, k, f=kidx(p): (f(k), 0)) for p, (_, _, tk) in enumerate(parts)]
    in_specs.append(pl.BlockSpec((tm, N), lambda i, k: (i, 0)))
    res = _call(
        body, name=name, out_shape=(jax.ShapeDtypeStruct((M, N), F32),), grid=(M // tm, nk),
        in_specs=in_specs, out_specs=(pl.BlockSpec((tm, N), lambda i, k: (i, 0)),),
        scratch_shapes=[pltpu.VMEM((tm, N), F32)], sem=("parallel", "arbitrary"),
        args=[a for a, _, _ in parts] + [w for _, w, _ in parts] + [c_in], ride=ride)
    return res[0] if ride is None else (res[0][0], res[1])


def _gla_gate(pg_ref, rows, wg_ref, bg_ref):
    r = pg_ref[rows, 3072:3200].astype(BF)
    logit = _dot(r, wg_ref[...]) + bg_ref[...]
    la = (jnp.minimum(logit, 0.0) - jnp.log(1.0 + jnp.exp(-jnp.abs(logit)))) * (1.0 / GTAU)
    return r, logit, la


def _gla_fwd(pg, wg, bg, gn, ltri, *, nseq, S, tm):
    T = pg.shape[0]
    nb, nc = S // tm, tm // GC
    qscale = GDK ** -0.5

    def body(pg_ref, wg_ref, bg_ref, gn_ref, l_ref, o_ref, zg_ref, st_ref, st_scr):
        @pl.when(pl.program_id(1) == 0)
        def _():
            st_scr[...] = jnp.zeros_like(st_scr)

        ltri_v = l_ref[...]
        causal = _iota((GC, GC), 0) >= _iota((GC, GC), 1)
        last_row = _iota((GC, GDK), 0) == GC - 1
        g = gn_ref[...]

        def chunk(c, carry):
            rows = pl.ds(pl.multiple_of(c * GC, GC), GC)
            _, _, la = _gla_gate(pg_ref, rows, wg_ref, bg_ref)
            b = _tri_mm(ltri_v, la)
            for h in range(GH):
                q = pg_ref[rows, h * GDK:(h + 1) * GDK]
                k = pg_ref[rows, 512 + h * GDK:512 + (h + 1) * GDK]
                v = pg_ref[rows, 1024 + h * GDV:1024 + (h + 1) * GDV].astype(BF)
                og = pg_ref[rows, 2048 + h * GDV:2048 + (h + 1) * GDV]
                bh = b[:, h * GDK:(h + 1) * GDK]
                bl = jnp.sum(jnp.where(last_row, bh, 0.0), axis=0, keepdims=True)
                q_in = (q * (qscale * jnp.exp(bh))).astype(BF)
                k_in = (k * jnp.exp(-bh)).astype(BF)
                k_st = (k * jnp.exp(bl - bh)).astype(BF)
                dec = jnp.exp(bl)
                st = st_scr[h]
                st_ref[c, h] = st
                att = jnp.where(causal, _dot_nt(q_in, k_in), 0.0).astype(BF)
                o = _dot(att, v) + _dot_nt(q_in, st.astype(BF))
                st_scr[h] = st * dec + _dot_tn(v, k_st)
                rstd = lax.rsqrt(jnp.mean(o * o, axis=-1, keepdims=True) + RMS_EPS)
                o_ref[rows, h * GDV:(h + 1) * GDV] = o
                zg_ref[rows, h * GDV:(h + 1) * GDV] = (o * rstd * g * (og * _sigmoid(og))).astype(BF)
            return carry

        lax.fori_loop(0, nc, chunk, 0)

    full = lambda shp: pl.BlockSpec(shp, lambda b_, i: (0,) * len(shp))
    return pl.pallas_call(
        body, name="gla_fwd",
        out_shape=(jax.ShapeDtypeStruct((T, GH * GDV), F32),
                   jax.ShapeDtypeStruct((T, GH * GDV), BF),
                   jax.ShapeDtypeStruct((T // GC, GH, GDV, GDK), F32)),
        grid=(nseq, nb),
        in_specs=[pl.BlockSpec((tm, PG_W), lambda b_, i: (b_ * nb + i, 0)),
                  full((128, 512)), full((1, 512)), full((1, GDV)), full((GC, GC))],
        out_specs=(pl.BlockSpec((tm, GH * GDV), lambda b_, i: (b_ * nb + i, 0)),
                   pl.BlockSpec((tm, GH * GDV), lambda b_, i: (b_ * nb + i, 0)),
                   pl.BlockSpec((nc, GH, GDV, GDK), lambda b_, i: (b_ * nb + i, 0, 0, 0))),
        scratch_shapes=[pltpu.VMEM((GH, GDV, GDK), F32)],
        compiler_params=_params(("parallel", "arbitrary")),
    )(pg, wg, bg, gn, ltri)


def _gla_bwd(pg, wg, bg, gn, ltri, utri, o, states, dzg, *, nseq, S, tm, ride=None):
    T = pg.shape[0]
    nb, nc = S // tm, tm // GC
    qscale = GDK ** -0.5

    def body(pg_ref, wg_ref, bg_ref, gn_ref, l_ref, u_ref, o_ref, st_ref, dzg_ref,
             dpg_ref, dwg_ref, dbg_ref, dgn_ref, dst_scr):
        first = jnp.logical_and(pl.program_id(0) == 0, pl.program_id(1) == 0)

        @pl.when(first)
        def _():
            dwg_ref[...] = jnp.zeros_like(dwg_ref)
            dbg_ref[...] = jnp.zeros_like(dbg_ref)
            dgn_ref[...] = jnp.zeros_like(dgn_ref)

        @pl.when(pl.program_id(1) == 0)
        def _():
            dst_scr[...] = jnp.zeros_like(dst_scr)

        ltri_v = l_ref[...]
        utri_v = u_ref[...]
        causal = _iota((GC, GC), 0) >= _iota((GC, GC), 1)
        last_row = _iota((GC, GDK), 0) == GC - 1
        g = gn_ref[...]

        def chunk(cc, carry):
            c = nc - 1 - cc
            rows = pl.ds(pl.multiple_of(c * GC, GC), GC)
            r, logit, la = _gla_gate(pg_ref, rows, wg_ref, bg_ref)
            b = _tri_mm(ltri_v, la)
            dbs = []
            for h in range(GH):
                q = pg_ref[rows, h * GDK:(h + 1) * GDK]
                k = pg_ref[rows, 512 + h * GDK:512 + (h + 1) * GDK]
                vb = pg_ref[rows, 1024 + h * GDV:1024 + (h + 1) * GDV].astype(BF)
                og = pg_ref[rows, 2048 + h * GDV:2048 + (h + 1) * GDV]
                oh = o_ref[rows, h * GDV:(h + 1) * GDV]
                dz = dzg_ref[rows, h * GDV:(h + 1) * GDV].astype(F32)
                bh = b[:, h * GDK:(h + 1) * GDK]
                bl = jnp.sum(jnp.where(last_row, bh, 0.0), axis=0, keepdims=True)
                eb = qscale * jnp.exp(bh)
                enb = jnp.exp(-bh)
                ek = jnp.exp(bl - bh)
                dec = jnp.exp(bl)
                q_in = q * eb
                k_in = k * enb
                k_st = k * ek
                q_inb, k_inb, k_stb = q_in.astype(BF), k_in.astype(BF), k_st.astype(BF)
                st = st_ref[c, h]
                dst = dst_scr[h]
                rstd = lax.rsqrt(jnp.mean(oh * oh, axis=-1, keepdims=True) + RMS_EPS)
                ohat = oh * rstd
                sg = _sigmoid(og)
                don = dz * (og * sg)
                dog = dz * (ohat * g) * (sg * (1.0 + og * (1.0 - sg)))
                dgn_ref[...] += jnp.sum(don * ohat, axis=0, keepdims=True)
                gd = don * g
                do = rstd * (gd - ohat * jnp.mean(gd * ohat, axis=-1, keepdims=True))
                dob = do.astype(BF)
                att = jnp.where(causal, _dot_nt(q_inb, k_inb), 0.0).astype(BF)
                da = jnp.where(causal, _dot_nt(dob, vb), 0.0).astype(BF)
                dstb = dst.astype(BF)
                dqi = _dot(da, k_inb) + _dot(dob, st.astype(BF))
                dki = _dot_tn(da, q_inb)
                dv = _dot_tn(att, dob) + _dot_nt(k_stb, dstb)
                dks = _dot(vb, dstb)
                dd = jnp.sum(dst * st, axis=0, keepdims=True)
                dst_scr[h] = dst * dec + _dot_tn(dob, q_inb)
                dq = dqi * eb
                dk = dki * enb + dks * ek
                kk = dks * k_st
                dbl = jnp.sum(kk, axis=0, keepdims=True) + dd * dec
                db = dqi * q_in - dki * k_in - kk
                dbs.append(db + jnp.where(last_row, dbl, 0.0))
                dpg_ref[rows, h * GDK:(h + 1) * GDK] = dq.astype(BF)
                dpg_ref[rows, 512 + h * GDK:512 + (h + 1) * GDK] = dk.astype(BF)
                dpg_ref[rows, 1024 + h * GDV:1024 + (h + 1) * GDV] = dv.astype(BF)
                dpg_ref[rows, 2048 + h * GDV:2048 + (h + 1) * GDV] = dog.astype(BF)
            dla = _tri_mm(utri_v, jnp.concatenate(dbs, axis=1))
            dlogit = dla * (1.0 / GTAU) * _sigmoid(-logit)
            dlb = dlogit.astype(BF)
            dpg_ref[rows, 3072:3200] = _dot_nt(dlb, wg_ref[...]).astype(BF)
            dwg_ref[...] += _dot_tn(r, dlb)
            dbg_ref[...] += jnp.sum(dlogit, axis=0, keepdims=True)
            return carry

        lax.fori_loop(0, nc, chunk, 0)

    full = lambda shp: pl.BlockSpec(shp, lambda b_, i: (0,) * len(shp))
    rev = lambda b_, i: (b_ * nb + nb - 1 - i, 0)
    return _call(
        body, name="gla_bwd", ride=ride, sem=("arbitrary", "arbitrary"),
        args=(pg, wg, bg, gn, ltri, utri, o, states, dzg),
        out_shape=(jax.ShapeDtypeStruct((T, PG_W), BF),
                   jax.ShapeDtypeStruct((128, 512), F32),
                   jax.ShapeDtypeStruct((1, 512), F32),
                   jax.ShapeDtypeStruct((1, GDV), F32)),
        grid=(nseq, nb),
        in_specs=[pl.BlockSpec((tm, PG_W), rev),
                  full((128, 512)), full((1, 512)), full((1, GDV)), full((GC, GC)), full((GC, GC)),
                  pl.BlockSpec((tm, GH * GDV), rev),
                  pl.BlockSpec((nc, GH, GDV, GDK), lambda b_, i: (b_ * nb + nb - 1 - i, 0, 0, 0)),
                  pl.BlockSpec((tm, GH * GDV), rev)],
        out_specs=(pl.BlockSpec((tm, PG_W), rev), full((128, 512)), full((1, 512)), full((1, GDV))),
        scratch_shapes=[pltpu.VMEM((GH, GDV, GDK), F32)])


def _rope_tables(pos, invf):
    ang = pos.astype(F32) * invf
    lane = _iota(ang.shape, 1)
    sin = jnp.sin(ang)
    ssin = jnp.where(lane < 32, -sin, jnp.where(lane < 64, sin, 0.0))
    return jnp.cos(ang), ssin, lane


def _rope(x, cos, ssin, lane, sign):
    rot = jnp.where(lane < 32, pltpu.roll(x, 96, 1), pltpu.roll(x, 32, 1))
    return x * cos + sign * (rot * ssin)


def _rms_fwd(x, g):
    rstd = lax.rsqrt(jnp.mean(x * x, axis=-1, keepdims=True) + RMS_EPS)
    return x * rstd * g, x * rstd, rstd


def _rms_bwd(dy, xhat, rstd, g):
    gd = dy * g
    return rstd * (gd - xhat * jnp.mean(gd * xhat, axis=-1, keepdims=True)), jnp.sum(dy * xhat, axis=0, keepdims=True)


def _mla_prep_fwd(pm, pos, invf, gq, gkv, wuq, wukv, *, tm):
    T = pm.shape[0]

    def body(pm_ref, pos_ref, invf_ref, gq_ref, gkv_ref, wuq_ref, wukv_ref, qc_ref, kc_ref, v_ref):
        cos, ssin, lane = _rope_tables(pos_ref[...], invf_ref[...])
        cq, _, _ = _rms_fwd(pm_ref[:, 0:MQR], gq_ref[...])
        ckv, _, _ = _rms_fwd(pm_ref[:, 512:768], gkv_ref[...])
        qf = _dot(cq.astype(BF), wuq_ref[...])
        kvf = _dot(ckv.astype(BF), wukv_ref[...])
        kr = _rope(pm_ref[:, 384:512], cos, ssin, lane, 1.0).astype(BF)
        for h in range(MH):
            qc_ref[:, 256 * h:256 * h + 128] = qf[:, 128 * h:128 * h + 128].astype(BF)
            qr = qf[:, 1024 + 128 * h:1024 + 128 * h + 128]
            qc_ref[:, 256 * h + 128:256 * h + 256] = _rope(qr, cos, ssin, lane, 1.0).astype(BF)
            kc_ref[:, 256 * h:256 * h + 128] = kvf[:, 128 * h:128 * h + 128].astype(BF)
            kc_ref[:, 256 * h + 128:256 * h + 256] = kr
        v_ref[...] = kvf[:, 1024:2048].astype(BF)

    full = lambda shp: pl.BlockSpec(shp, lambda i: (0,) * len(shp))
    row = lambda w: pl.BlockSpec((tm, w), lambda i: (i, 0))
    return pl.pallas_call(
        body, name="mla_prep_fwd",
        out_shape=(jax.ShapeDtypeStruct((T, MH * 256), BF), jax.ShapeDtypeStruct((T, MH * 256), BF),
                   jax.ShapeDtypeStruct((T, MH * MV), BF)),
        grid=(T // tm,),
        in_specs=[row(PM_W), row(1), full((1, 128)), full((1, MQR)), full((1, MKR)),
                  full((MQR, 2048)), full((MKR, 2048))],
        out_specs=(row(MH * 256), row(MH * 256), row(MH * MV)),
        compiler_params=_params(("parallel",)),
    )(pm, pos, invf, gq, gkv, wuq, wukv)


def _mla_prep_bwd(pm, pos, invf, gq, gkv, wuq, wukv, dqc, dkc, dv, *, tm):
    T = pm.shape[0]

    def body(pm_ref, pos_ref, invf_ref, gq_ref, gkv_ref, wuq_ref, wukv_ref, dqc_ref, dkc_ref, dv_ref,
             dpm_ref, dwuq_ref, dwukv_ref, dgq_ref, dgkv_ref):
        @pl.when(pl.program_id(0) == 0)
        def _():
            dwuq_ref[...] = jnp.zeros_like(dwuq_ref)
            dwukv_ref[...] = jnp.zeros_like(dwukv_ref)
            dgq_ref[...] = jnp.zeros_like(dgq_ref)
            dgkv_ref[...] = jnp.zeros_like(dgkv_ref)

        cos, ssin, lane = _rope_tables(pos_ref[...], invf_ref[...])
        cq, cqh, cq_rstd = _rms_fwd(pm_ref[:, 0:MQR], gq_ref[...])
        ckv, ckvh, ckv_rstd = _rms_fwd(pm_ref[:, 512:768], gkv_ref[...])
        dqn, dqr, dkn = [], [], []
        dkr = jnp.zeros((tm, 128), F32)
        for h in range(MH):
            dqn.append(dqc_ref[:, 256 * h:256 * h + 128].astype(BF))
            dqr.append(_rope(dqc_ref[:, 256 * h + 128:256 * h + 256], cos, ssin, lane, -1.0).astype(BF))
            dkn.append(dkc_ref[:, 256 * h:256 * h + 128].astype(BF))
            dkr = dkr + dkc_ref[:, 256 * h + 128:256 * h + 256]
        dqf = jnp.concatenate(dqn + dqr, axis=1)
        dkvf = jnp.concatenate(dkn + [dv_ref[...].astype(BF)], axis=1)
        dwuq_ref[...] += _dot_tn(cq.astype(BF), dqf)
        dwukv_ref[...] += _dot_tn(ckv.astype(BF), dkvf)
        dcq, dgq = _rms_bwd(_dot_nt(dqf, wuq_ref[...]), cqh, cq_rstd, gq_ref[...])
        dckv, dgkv = _rms_bwd(_dot_nt(dkvf, wukv_ref[...]), ckvh, ckv_rstd, gkv_ref[...])
        dgq_ref[...] += dgq
        dgkv_ref[...] += dgkv
        dpm_ref[:, 0:MQR] = dcq.astype(BF)
        dpm_ref[:, 384:512] = _rope(dkr, cos, ssin, lane, -1.0).astype(BF)
        dpm_ref[:, 512:768] = dckv.astype(BF)

    full = lambda shp: pl.BlockSpec(shp, lambda i: (0,) * len(shp))
    row = lambda w: pl.BlockSpec((tm, w), lambda i: (i, 0))
    return pl.pallas_call(
        body, name="mla_prep_bwd",
        out_shape=(jax.ShapeDtypeStruct((T, PM_W), BF), jax.ShapeDtypeStruct((MQR, 2048), F32),
                   jax.ShapeDtypeStruct((MKR, 2048), F32), jax.ShapeDtypeStruct((1, MQR), F32),
                   jax.ShapeDtypeStruct((1, MKR), F32)),
        grid=(T // tm,),
        in_specs=[row(PM_W), row(1), full((1, 128)), full((1, MQR)), full((1, MKR)),
                  full((MQR, 2048)), full((MKR, 2048)), row(MH * 256), row(MH * 256), row(MH * MV)],
        out_specs=(row(PM_W), full((MQR, 2048)), full((MKR, 2048)), full((1, MQR)), full((1, MKR))),
        compiler_params=_params(("arbitrary",)),
    )(pm, pos, invf, gq, gkv, wuq, wukv, dqc, dkc, dv)


def _flash_fwd(qc, kc, v, *, nseq, S, tq, ride=None):
    T = qc.shape[0]
    nq = S // tq
    scale = (NOPE + ROPE) ** -0.5

    def body(q_ref, k_ref, v_ref, o_ref, lse_ref):
        i = pl.program_id(2)
        causal = _iota((tq, tq), 0) >= _iota((tq, tq), 1)

        def step(j, carry, masked):
            rows = pl.ds(pl.multiple_of(j * tq, tq), tq)
            out = []
            for hh in range(FLASH_HP):
                m, l, acc = carry[hh]
                s = _dot_nt(q_ref[:, 256 * hh:256 * hh + 256], k_ref[rows, 256 * hh:256 * hh + 256]) * scale
                if masked:
                    s = jnp.where(causal, s, NEG)
                m_new = jnp.maximum(m, jnp.max(s, axis=-1, keepdims=True))
                p = jnp.exp(s - m_new)
                a = jnp.exp(m - m_new)
                l = a * l + jnp.sum(p, axis=-1, keepdims=True)
                acc = a * acc + _dot(p.astype(BF), v_ref[rows, MV * hh:MV * hh + MV])
                out.append((m_new, l, acc))
            return tuple(out)

        init = ((jnp.full((tq, 1), NEG, F32), jnp.zeros((tq, 1), F32), jnp.zeros((tq, MV), F32)),) * FLASH_HP
        carry = lax.fori_loop(0, i, lambda j, c: step(j, c, False), init)
        for hh, (m, l, acc) in enumerate(step(i, carry, True)):
            o_ref[:, MV * hh:MV * hh + MV] = (acc / l).astype(BF)
            lse_ref[:, 128 * hh:128 * hh + 128] = jnp.broadcast_to(m + jnp.log(l), (tq, 128))

    hp = FLASH_HP
    return _call(
        body, name="flash_fwd", ride=ride, sem=("parallel", "parallel", "arbitrary"), args=(qc, kc, v),
        out_shape=(jax.ShapeDtypeStruct((T, MH * MV), BF), jax.ShapeDtypeStruct((T, MH * 128), F32)),
        grid=(nseq, MH // hp, nq),
        in_specs=[pl.BlockSpec((tq, 256 * hp), lambda b_, h, i: (b_ * nq + i, h)),
                  pl.BlockSpec((S, 256 * hp), lambda b_, h, i: (b_, h)),
                  pl.BlockSpec((S, MV * hp), lambda b_, h, i: (b_, h))],
        out_specs=(pl.BlockSpec((tq, MV * hp), lambda b_, h, i: (b_ * nq + i, h)),
                   pl.BlockSpec((tq, 128 * hp), lambda b_, h, i: (b_ * nq + i, h))))


def _flash_bwd(qc, kc, v, o, do, lse, *, nseq, S, tq, ride=None):
    T = qc.shape[0]
    nq = S // tq
    scale = (NOPE + ROPE) ** -0.5

    def body(q_ref, k_ref, v_ref, o_ref, do_ref, lse_ref, dq_ref, dk_ref, dv_ref, dq_scr, delta_scr):
        j = pl.program_id(2)

        @pl.when(j == 0)
        def _():
            dq_scr[...] = jnp.zeros_like(dq_scr)
            for hh in range(FLASH_HP):
                od = o_ref[:, MV * hh:MV * hh + MV].astype(F32) * do_ref[:, MV * hh:MV * hh + MV].astype(F32)
                delta_scr[:, 128 * hh:128 * hh + 128] = jnp.broadcast_to(jnp.sum(od, axis=-1, keepdims=True), (S, 128))

        causal = _iota((tq, tq), 0) >= _iota((tq, tq), 1)

        def step(i, carry, masked):
            rows = pl.ds(pl.multiple_of(i * tq, tq), tq)
            out = []
            for hh in range(FLASH_HP):
                dk, dv = carry[hh]
                qs, vs, ls = slice(256 * hh, 256 * hh + 256), slice(MV * hh, MV * hh + MV), slice(128 * hh, 128 * hh + 1)
                q = q_ref[rows, qs]
                dob = do_ref[rows, vs]
                kb = k_ref[:, qs]
                s = _dot_nt(q, kb) * scale
                p = jnp.exp(s - lse_ref[rows, ls])
                if masked:
                    p = jnp.where(causal, p, 0.0)
                dv = dv + _dot_tn(p.astype(BF), dob)
                dp = _dot_nt(dob, v_ref[:, vs])
                ds = (p * (dp - delta_scr[rows, ls]) * scale).astype(BF)
                dk = dk + _dot_tn(ds, q)
                dq_scr[rows, qs] += _dot(ds, kb)
                out.append((dk, dv))
            return tuple(out)

        init = ((jnp.zeros((tq, 256), F32), jnp.zeros((tq, MV), F32)),) * FLASH_HP
        carry = step(j, init, True)
        carry = lax.fori_loop(j + 1, nq, lambda i, c: step(i, c, False), carry)
        for hh, (dk, dv) in enumerate(carry):
            dk_ref[:, 256 * hh:256 * hh + 256] = dk
            dv_ref[:, MV * hh:MV * hh + MV] = dv

        @pl.when(j == nq - 1)
        def _():
            dq_ref[...] = dq_scr[...]

    hp = FLASH_HP
    seq = lambda w: pl.BlockSpec((S, w * hp), lambda b_, h, j: (b_, h))
    blk = lambda w: pl.BlockSpec((tq, w * hp), lambda b_, h, j: (b_ * nq + j, h))
    return _call(
        body, name="flash_bwd", ride=ride, sem=("parallel", "parallel", "arbitrary"), args=(qc, kc, v, o, do, lse),
        out_shape=(jax.ShapeDtypeStruct((T, MH * 256), F32), jax.ShapeDtypeStruct((T, MH * 256), F32),
                   jax.ShapeDtypeStruct((T, MH * MV), F32)),
        grid=(nseq, MH // hp, nq),
        in_specs=[seq(256), blk(256), blk(MV), seq(MV), seq(MV), seq(128)],
        out_specs=(seq(256), blk(256), blk(MV)),
        scratch_shapes=[pltpu.VMEM((S, 256 * hp), F32), pltpu.VMEM((S, 128 * hp), F32)])


def _ln_fwd(pre, g, b):
    mu = jnp.mean(pre, axis=-1, keepdims=True)
    xc = pre - mu
    rstd = lax.rsqrt(jnp.mean(xc * xc, axis=-1, keepdims=True) + LN_EPS)
    xhat = xc * rstd
    return xhat * g + b, xhat, rstd


def _ln_bwd(dy, xhat, rstd, g):
    dxh = dy * g
    dx = rstd * (dxh - jnp.mean(dxh, axis=-1, keepdims=True) - xhat * jnp.mean(dxh * xhat, axis=-1, keepdims=True))
    return dx, jnp.sum(dy * xhat, axis=0, keepdims=True), jnp.sum(dy, axis=0, keepdims=True)


def _post_attn_fwd(zg, attn, pt, x, wgo, wmo, wout, g1, b1, *, tm):
    T = x.shape[0]

    def body(zg_ref, at_ref, pt_ref, x_ref, wgo_ref, wmo_ref, wout_ref, g_ref, b_ref,
             yg_ref, ym_ref, mix_ref, pre_ref, h_ref, hb_ref):
        yg = _dot(zg_ref[...], wgo_ref[...])
        ym = _dot(at_ref[...], wmo_ref[...])
        mix = (_sigmoid(pt_ref[:, 0:D]) * yg + _sigmoid(pt_ref[:, D:2 * D]) * ym).astype(BF)
        pre = ALPHA * x_ref[...] + _dot(mix, wout_ref[...])
        h, _, _ = _ln_fwd(pre, g_ref[...], b_ref[...])
        yg_ref[...] = yg
        ym_ref[...] = ym
        mix_ref[...] = mix
        pre_ref[...] = pre
        h_ref[...] = h
        hb_ref[...] = h.astype(BF)

    full = lambda shp: pl.BlockSpec(shp, lambda i: (0,) * len(shp))
    row = lambda w: pl.BlockSpec((tm, w), lambda i: (i, 0))
    sd = lambda dt: jax.ShapeDtypeStruct((T, D), dt)
    return pl.pallas_call(
        body, name="post_attn_fwd",
        out_shape=(sd(F32), sd(F32), sd(BF), sd(F32), sd(F32), sd(BF)),
        grid=(T // tm,),
        in_specs=[row(D), row(D), row(PT_W), row(D), full((D, D)), full((D, D)), full((D, D)),
                  full((1, D)), full((1, D))],
        out_specs=(row(D),) * 6,
        compiler_params=_params(("parallel",)),
    )(zg, attn, pt, x, wgo, wmo, wout, g1, b1)


def _post_attn_bwd(dh, pre, pt, yg, ym, wgo, wmo, wout, g1, *, tm):
    T = dh.shape[0]

    def body(dh_ref, pre_ref, pt_ref, yg_ref, ym_ref, wgo_ref, wmo_ref, wout_ref, g_ref,
             dx_ref, dpreb_ref, dpt_ref, dygb_ref, dymb_ref, dzg_ref, dat_ref, dg_ref, db_ref):
        @pl.when(pl.program_id(0) == 0)
        def _():
            dg_ref[...] = jnp.zeros_like(dg_ref)
            db_ref[...] = jnp.zeros_like(db_ref)

        pre = pre_ref[...]
        mu = jnp.mean(pre, axis=-1, keepdims=True)
        xc = pre - mu
        rstd = lax.rsqrt(jnp.mean(xc * xc, axis=-1, keepdims=True) + LN_EPS)
        dpre, dg, db = _ln_bwd(dh_ref[...], xc * rstd, rstd, g_ref[...])
        dg_ref[...] += dg
        db_ref[...] += db
        dx_ref[...] = ALPHA * dpre
        dpreb = dpre.astype(BF)
        dpreb_ref[...] = dpreb
        dmix = _dot_nt(dpreb, wout_ref[...])
        sa = _sigmoid(pt_ref[:, 0:D])
        sb = _sigmoid(pt_ref[:, D:2 * D])
        dpt_ref[:, 0:D] = (dmix * yg_ref[...] * (sa * (1.0 - sa))).astype(BF)
        dpt_ref[:, D:2 * D] = (dmix * ym_ref[...] * (sb * (1.0 - sb))).astype(BF)
        dyg = (dmix * sa).astype(BF)
        dym = (dmix * sb).astype(BF)
        dygb_ref[...] = dyg
        dymb_ref[...] = dym
        dzg_ref[...] = _dot_nt(dyg, wgo_ref[...]).astype(BF)
        dat_ref[...] = _dot_nt(dym, wmo_ref[...]).astype(BF)

    full = lambda shp: pl.BlockSpec(shp, lambda i: (0,) * len(shp))
    row = lambda w: pl.BlockSpec((tm, w), lambda i: (i, 0))
    sd = lambda w, dt: jax.ShapeDtypeStruct((T, w), dt)
    return pl.pallas_call(
        body, name="post_attn_bwd",
        out_shape=(sd(D, F32), sd(D, BF), sd(PT_W, BF), sd(D, BF), sd(D, BF), sd(D, BF), sd(D, BF),
                   jax.ShapeDtypeStruct((1, D), F32), jax.ShapeDtypeStruct((1, D), F32)),
        grid=(T // tm,),
        in_specs=[row(D), row(D), row(PT_W), row(D), row(D), full((D, D)), full((D, D)), full((D, D)),
                  full((1, D))],
        out_specs=(row(D), row(D), row(PT_W), row(D), row(D), row(D), row(D), full((1, D)), full((1, D))),
        compiler_params=_params(("arbitrary",)),
    )(dh, pre, pt, yg, ym, wgo, wmo, wout, g1)


def _shift_down(u, prev, k):
    r = pltpu.roll(u, k, 0)
    p = pltpu.roll(prev, k, 0)
    head = jnp.where(_iota(p.shape, 0) < k, p, r[0:8, :])
    return jnp.concatenate([head, r[8:, :]], axis=0)


def _shift_up(u, nxt, k):
    n = u.shape[0]
    r = pltpu.roll(u, n - k, 0)
    p = pltpu.roll(nxt, 8 - k, 0)
    tail = jnp.where(_iota(p.shape, 0) >= 8 - k, p, r[n - 8:, :])
    return jnp.concatenate([r[:n - 8, :], tail], axis=0)


def _conv3(u, prev, w_ref, b_ref):
    return (w_ref[0:1, :] * _shift_down(u, prev, 2) + w_ref[1:2, :] * _shift_down(u, prev, 1)
            + w_ref[2:3, :] * u + b_ref[...])


def _ffn_up_fwd(hb, wugt, wuvt, cw, cb, *, S, tm, tn):
    T = hb.shape[0]
    nj, nbs = DFF // tn, S // tm

    def body(h_ref, wg_ref, wv_ref, cwg_ref, cwv_ref, cbg_ref, cbv_ref,
             ug_ref, uv_ref, ucg_ref, ucv_ref, f_ref, pg_scr, pv_scr):
        @pl.when(pl.program_id(1) % nbs == 0)
        def _():
            pg_scr[...] = jnp.zeros_like(pg_scr)
            pv_scr[...] = jnp.zeros_like(pv_scr)

        h = h_ref[...]
        ug = _dot_nt(h, wg_ref[...])
        uv = _dot_nt(h, wv_ref[...])
        ucg = _conv3(ug, pg_scr[...], cwg_ref, cbg_ref)
        ucv = _conv3(uv, pv_scr[...], cwv_ref, cbv_ref)
        pg_scr[...] = ug[tm - 8:, :]
        pv_scr[...] = uv[tm - 8:, :]
        ug_ref[...] = ug.astype(BF)
        uv_ref[...] = uv.astype(BF)
        ucg_ref[...] = ucg
        ucv_ref[...] = ucv
        f_ref[...] = (ucg * _sigmoid(ucg) * ucv).astype(BF)

    tile = pl.BlockSpec((tm, tn), lambda j, i: (i, j))
    return pl.pallas_call(
        body, name="ffn_up_fwd",
        out_shape=(jax.ShapeDtypeStruct((T, DFF), BF), jax.ShapeDtypeStruct((T, DFF), BF),
                   jax.ShapeDtypeStruct((T, DFF), F32), jax.ShapeDtypeStruct((T, DFF), F32),
                   jax.ShapeDtypeStruct((T, DFF), BF)),
        grid=(nj, T // tm),
        in_specs=[pl.BlockSpec((tm, D), lambda j, i: (i, 0)),
                  pl.BlockSpec((tn, D), lambda j, i: (j, 0)), pl.BlockSpec((tn, D), lambda j, i: (j, 0)),
                  pl.BlockSpec((3, tn), lambda j, i: (0, j)), pl.BlockSpec((3, tn), lambda j, i: (0, j + nj)),
                  pl.BlockSpec((1, tn), lambda j, i: (0, j)), pl.BlockSpec((1, tn), lambda j, i: (0, j + nj))],
        out_specs=(tile, tile, tile, tile, tile),
        scratch_shapes=[pltpu.VMEM((8, tn), F32), pltpu.VMEM((8, tn), F32)],
        compiler_params=_params(("parallel", "arbitrary")),
    )(hb, wugt, wuvt, cw, cw, cb, cb)


def _ffn_bwd(dpreb, wd, ug, uv, ucg, ucv, cw, *, S, tm, tn):
    T = dpreb.shape[0]
    nj, nb, nbs = DFF // tn, T // tm, S // tm

    def body(dp_ref, wd_ref, ug_ref, uv_ref, ucg_ref, ucv_ref, cwg_ref, cwv_ref,
             dug_ref, duv_ref, dcg_ref, dcv_ref, ng_scr, nv_scr):
        ii = pl.program_id(1)
        i = nb - 1 - ii

        @pl.when(ii == 0)
        def _():
            dcg_ref[...] = jnp.zeros_like(dcg_ref)
            dcv_ref[...] = jnp.zeros_like(dcv_ref)

        @pl.when(i % nbs == nbs - 1)
        def _():
            ng_scr[...] = jnp.zeros_like(ng_scr)
            nv_scr[...] = jnp.zeros_like(nv_scr)

        df = _dot_nt(dp_ref[...], wd_ref[...])
        ucg = ucg_ref[...]
        sg = _sigmoid(ucg)
        ducg = df * ucv_ref[...] * (sg * (1.0 + ucg * (1.0 - sg)))
        ducv = df * (ucg * sg)

        def finish(duc, u_ref, w, nxt_scr, du_ref, dc_ref):
            nxt = nxt_scr[...]
            up1 = _shift_up(duc, nxt, 1)
            up2 = _shift_up(duc, nxt, 2)
            du_ref[...] = (w[2:3, :] * duc + w[1:2, :] * up1 + w[0:1, :] * up2).astype(BF)
            nxt_scr[...] = duc[0:8, :]
            u = u_ref[...].astype(F32)
            for row, z in enumerate((u * up2, u * up1, u * duc, duc)):
                dc_ref[row:row + 1, :] += jnp.sum(z, axis=0, keepdims=True)

        finish(ducg, ug_ref, cwg_ref, ng_scr, dug_ref, dcg_ref)
        finish(ducv, uv_ref, cwv_ref, nv_scr, duv_ref, dcv_ref)

    tile = pl.BlockSpec((tm, tn), lambda j, ii: (nb - 1 - ii, j))
    acc = pl.BlockSpec((8, tn), lambda j, ii: (0, j))
    return pl.pallas_call(
        body, name="ffn_bwd",
        out_shape=(jax.ShapeDtypeStruct((T, DFF), BF), jax.ShapeDtypeStruct((T, DFF), BF),
                   jax.ShapeDtypeStruct((8, DFF), F32), jax.ShapeDtypeStruct((8, DFF), F32)),
        grid=(nj, nb),
        in_specs=[pl.BlockSpec((tm, D), lambda j, ii: (nb - 1 - ii, 0)),
                  pl.BlockSpec((tn, D), lambda j, ii: (j, 0)),
                  tile, tile, tile, tile,
                  pl.BlockSpec((3, tn), lambda j, ii: (0, j)), pl.BlockSpec((3, tn), lambda j, ii: (0, j + nj))],
        out_specs=(tile, tile, acc, acc),
        scratch_shapes=[pltpu.VMEM((8, tn), F32), pltpu.VMEM((8, tn), F32)],
        compiler_params=_params(("parallel", "arbitrary")),
    )(dpreb, wd, ug, uv, ucg, ucv, cw, cw)


def _down_ln2_loss(f_in, wd, h, target, g2, b2, *, tm):
    T = h.shape[0]

    def body(f_ref, wd_ref, h_ref, t_ref, g_ref, b_ref, dpb_ref, dh_ref, loss_ref, dg_ref, db_ref):
        @pl.when(pl.program_id(0) == 0)
        def _():
            loss_ref[...] = jnp.zeros_like(loss_ref)
            dg_ref[...] = jnp.zeros_like(dg_ref)
            db_ref[...] = jnp.zeros_like(db_ref)

        pre = ALPHA * h_ref[...] + _dot(f_ref[...], wd_ref[...])
        out, xhat, rstd = _ln_fwd(pre, g_ref[...], b_ref[...])
        diff = out - t_ref[...]
        loss_ref[...] += 0.5 * jnp.sum(jnp.mean(diff * diff, axis=-1, keepdims=True))
        dpre, dg, db = _ln_bwd(diff * (1.0 / D), xhat, rstd, g_ref[...])
        dg_ref[...] += dg
        db_ref[...] += db
        dpb_ref[...] = dpre.astype(BF)
        dh_ref[...] = ALPHA * dpre

    full = lambda shp: pl.BlockSpec(shp, lambda i: (0,) * len(shp))
    row = lambda w: pl.BlockSpec((tm, w), lambda i: (i, 0))
    return pl.pallas_call(
        body, name="down_ln2_loss",
        out_shape=(jax.ShapeDtypeStruct((T, D), BF), jax.ShapeDtypeStruct((T, D), F32),
                   jax.ShapeDtypeStruct((8, 128), F32), jax.ShapeDtypeStruct((1, D), F32),
                   jax.ShapeDtypeStruct((1, D), F32)),
        grid=(T // tm,),
        in_specs=[row(DFF), full((DFF, D)), row(D), row(D), full((1, D)), full((1, D))],
        out_specs=(row(D), row(D), full((8, 128)), full((1, D)), full((1, D))),
        compiler_params=_params(("arbitrary",)),
    )(f_in, wd, h, target, g2, b2)


def _adamw(parts, w, m, v, *, name):
    n, R, C = parts.shape
    tr, tc = R, C
    for cand in range(min(R, 256), 15, -1):
        if R % cand == 0 and cand % 16 == 0:
            tr = cand
            break
    if tr == R and R * C > 65536 and C % 256 == 0:
        tc = 256
    c1 = 1.0 - ADAM_B1 ** ADAM_STEP
    c2 = 1.0 - ADAM_B2 ** ADAM_STEP

    def body(p_ref, w_ref, m_ref, v_ref, g_ref, d_ref, nm_ref, nv_ref):
        g = p_ref[0].astype(F32)
        for s in range(1, n):
            g = g + p_ref[s].astype(F32)
        nm = ADAM_B1 * m_ref[...] + (1.0 - ADAM_B1) * g
        nv = ADAM_B2 * v_ref[...] + (1.0 - ADAM_B2) * (g * g)
        g_ref[...] = g
        nm_ref[...] = nm
        nv_ref[...] = nv
        d_ref[...] = -ADAM_LR * ((nm / c1) / (jnp.sqrt(nv / c2) + ADAM_EPS) + ADAM_WD * w_ref[...])

    blk = pl.BlockSpec((tr, tc), lambda i, j: (i, j))
    sd = jax.ShapeDtypeStruct((R, C), F32)
    return pl.pallas_call(
        body, name=name,
        out_shape=(sd, sd, sd, sd),
        grid=(R // tr, C // tc),
        in_specs=[pl.BlockSpec((n, tr, tc), lambda i, j: (0, i, j)), blk, blk, blk],
        out_specs=(blk, blk, blk, blk),
        compiler_params=_params(("parallel", "parallel")),
    )(parts, w, m, v)


class _Exchange:
    def __init__(self, items):
        self.items = items
        self.n = len(items)

    def out_shape(self):
        return tuple(jax.ShapeDtypeStruct(a.shape if sc else (NDEV,) + a.shape, a.dtype) for a, sc in self.items)

    def scratch(self):
        return [pltpu.SemaphoreType.DMA((self.n, NDEV - 1)), pltpu.SemaphoreType.DMA((self.n, NDEV - 1)),
                pltpu.SemaphoreType.DMA((self.n,))]

    def _copies(self, ins, outs, sems, with_landed):
        send_sems, recv_sems, loc_sems = sems
        x, y, c = lax.axis_index("x"), lax.axis_index("y"), lax.axis_index("c")
        me = 4 * x + 2 * y + c
        flip = lambda p, d: 1 - p if d else p
        local, sent, landed = [], [], []
        for a, (_, sc) in enumerate(self.items):
            local.append(pltpu.make_async_copy(ins[a].at[me] if sc else ins[a], outs[a].at[me], loc_sems.at[a]))
        for k in range(1, NDEV):
            px, py, pc = flip(x, k & 4), flip(y, k & 2), flip(c, k & 1)
            peer = 4 * px + 2 * py + pc
            for a, (_, sc) in enumerate(self.items):
                src = ins[a].at[peer] if sc else ins[a]
                mk = functools.partial(pltpu.make_async_remote_copy, src_ref=src,
                                       send_sem=send_sems.at[a, k - 1], recv_sem=recv_sems.at[a, k - 1],
                                       device_id=(px, py, pc), device_id_type=MESH_ID)
                sent.append(mk(dst_ref=outs[a].at[me]))
                if with_landed:
                    landed.append(mk(dst_ref=outs[a].at[peer]))
        return local, sent, landed

    def start(self, ins, outs, sems):
        local, sent, _ = self._copies(ins, outs, sems, False)
        for cp in local + sent:
            cp.start()

    def wait(self, ins, outs, sems):
        local, sent, landed = self._copies(ins, outs, sems, True)
        for cp in landed:
            cp.wait_recv()
        for cp in sent:
            cp.wait_send()
        for cp in local:
            cp.wait()


def _call(body, *, name, grid, in_specs, out_specs, out_shape, args, scratch_shapes=(), sem=None, ride=None):
    if ride is None:
        return pl.pallas_call(body, name=name, grid=grid, in_specs=list(in_specs), out_specs=tuple(out_specs),
                              out_shape=tuple(out_shape), scratch_shapes=list(scratch_shapes),
                              compiler_params=_params(sem))(*args)
    n_in, n_out, n_scr, ne = len(args), len(out_shape), len(scratch_shapes), ride.n

    def ride_body(*refs):
        ins, ex_in = refs[:n_in], refs[n_in:n_in + ne]
        o0 = n_in + ne
        outs, ex_out = refs[o0:o0 + n_out], refs[o0 + n_out:o0 + n_out + ne]
        scr = refs[o0 + n_out + ne:o0 + n_out + ne + n_scr]
        sems = refs[o0 + n_out + ne + n_scr:]
        first = functools.reduce(jnp.logical_and, [pl.program_id(d) == 0 for d in range(len(grid))])
        last = functools.reduce(jnp.logical_and, [pl.program_id(d) == grid[d] - 1 for d in range(len(grid))])

        @pl.when(first)
        def _():
            ride.start(ex_in, ex_out, sems)

        body(*ins, *outs, *scr)

        @pl.when(last)
        def _():
            ride.wait(ex_in, ex_out, sems)

    anyspec = pl.BlockSpec(memory_space=pl.ANY)
    res = pl.pallas_call(
        ride_body, name=name, grid=grid,
        in_specs=list(in_specs) + [anyspec] * ne,
        out_specs=tuple(out_specs) + (anyspec,) * ne,
        out_shape=tuple(out_shape) + ride.out_shape(),
        scratch_shapes=list(scratch_shapes) + ride.scratch(),
        compiler_params=_params(("arbitrary",) * len(grid)),
    )(*args, *[a for a, _ in ride.items])
    return tuple(res[:n_out]), tuple(res[n_out:])


def _gather_two_level(arrays, *, name):
    n = len(arrays)

    def body(*refs):
        ins, outs = refs[:n], refs[n:2 * n]
        send_sems, recv_sems, loc_sems = refs[2 * n:]
        x, y, c = lax.axis_index("x"), lax.axis_index("y"), lax.axis_index("c")
        sibling = (x, y, 1 - c)
        chips = [(1 - x, y), (x, 1 - y), (1 - x, 1 - y)]
        idx = lambda px, py, pc: 4 * px + 2 * py + pc
        me = idx(x, y, c)

        def copy(a, k, block, to, src=None):
            return pltpu.make_async_remote_copy(
                src_ref=outs[a].at[block] if src is None else src, dst_ref=outs[a].at[block],
                send_sem=send_sems.at[a, k], recv_sem=recv_sems.at[a, k], device_id=to, device_id_type=MESH_ID)

        local = [pltpu.make_async_copy(ins[a], outs[a].at[me], loc_sems.at[a]) for a in range(n)]
        sent = []
        for a in range(n):
            sent.append(copy(a, 0, me, sibling, src=ins[a]))
            sent += [copy(a, 1 + j, me, (*chip, c), src=ins[a]) for j, chip in enumerate(chips)]
        for cp in local + sent:
            cp.start()
        for j, chip in enumerate(chips):
            for a in range(n):
                copy(a, 1 + j, idx(*chip, c), sibling).wait_recv()
                passed = copy(a, 4 + j, idx(*chip, c), sibling)
                passed.start()
                sent.append(passed)
        for a in range(n):
            copy(a, 0, idx(x, y, 1 - c), sibling).wait_recv()
            for j, chip in enumerate(chips):
                copy(a, 4 + j, idx(*chip, 1 - c), sibling).wait_recv()
        for cp in sent:
            cp.wait_send()
        for cp in local:
            cp.wait()

    anyspec = pl.BlockSpec(memory_space=pl.ANY)
    return pl.pallas_call(
        body, name=name,
        out_shape=tuple(jax.ShapeDtypeStruct((NDEV,) + a.shape, a.dtype) for a in arrays),
        in_specs=[anyspec] * n, out_specs=(anyspec,) * n,
        scratch_shapes=[pltpu.SemaphoreType.DMA((n, NDEV - 1)), pltpu.SemaphoreType.DMA((n, NDEV - 1)),
                        pltpu.SemaphoreType.DMA((n,))],
    )(*arrays)


def _exchange(items, *, name):
    ex = _Exchange(items)

    def body(*refs):
        ex.start(refs[:ex.n], refs[ex.n:2 * ex.n], refs[2 * ex.n:])
        ex.wait(refs[:ex.n], refs[ex.n:2 * ex.n], refs[2 * ex.n:])

    anyspec = pl.BlockSpec(memory_space=pl.ANY)
    return pl.pallas_call(
        body, name=name, out_shape=ex.out_shape(), in_specs=[anyspec] * ex.n, out_specs=(anyspec,) * ex.n,
        scratch_shapes=ex.scratch(),
    )(*[a for a, _ in items])


def _tri_consts():
    r = lax.broadcasted_iota(jnp.int32, (GC, GC), 0)
    c = lax.broadcasted_iota(jnp.int32, (GC, GC), 1)
    return (r >= c).astype(BF), (r <= c).astype(BF)


def _local_step(x, positions, target, w, hooks=None):
    g = {}

    def run(host, fn, *a, **kw):
        h = None if hooks is None else hooks.get(host)
        if h is None:
            return fn(*a, **kw)
        out, received = fn(*a, ride=_Exchange(h[0](w, g)), **kw)
        h[1](received, w, g)
        return out

    nseq, S, _ = x.shape
    T = nseq * S
    tm = min(256, S)
    tq = min(512, S)
    x2 = x.reshape(T, D)
    pos = positions.reshape(T, 1)
    half = ROPE // 2
    inv = THETA ** (-jnp.arange(half, dtype=F32) / half)
    invf = jnp.concatenate([inv, inv, jnp.zeros((64,), F32)]).reshape(1, 128)
    ltri, utri = _tri_consts()

    pg = run("proj_g", _matmul, x2, w["w_gt"], "nt", name="proj_g", tm=1024, tn=640, tk=1024)
    pm = _matmul(x2, w["w_mt"], "nt", name="proj_m", tm=1024, tn=768, tk=1024)
    pt = _matmul(x2, w["w_tt"], "nt", name="proj_t", tm=1024, tn=1024, tk=1024)
    o, zg, states = _gla_fwd(pg, w["wg"], w["bg"], w["gn"], ltri, nseq=nseq, S=S, tm=tm)
    qc, kc, v = _mla_prep_fwd(pm, pos, invf, w["gq"], w["gkv"], w["wuq"], w["wukv"], tm=tm)
    attn, lse = run("flash_fwd", _flash_fwd, qc, kc, v, nseq=nseq, S=S, tq=tq)
    yg, ym, mix, pre1, h1, h1b = _post_attn_fwd(zg, attn, pt, x2, w["wgo"], w["wmo"], w["wout"],
                                                w["g1"], w["b1"], tm=tm)
    ug, uv, ucg, ucv, f_in = _ffn_up_fwd(h1b, w["wugt"], w["wuvt"], w["cw"], w["cb"], S=S, tm=tm, tn=1408)
    dpre2b, dh1, loss8, dg2, db2 = _down_ln2_loss(f_in, w["wd"], h1, target.reshape(T, D), w["g2"], w["b2"], tm=tm)

    dug, duv, dcg, dcv = _ffn_bwd(dpre2b, w["wd"], ug, uv, ucg, ucv, w["cw"], S=S, tm=tm, tn=1408)
    g["g2"], g["b2"], g["loss"] = dg2, db2, loss8[0:1, 0:1]
    g["cw"] = jnp.concatenate([dcg[0:3], dcv[0:3]], axis=1)
    g["cb"] = jnp.concatenate([dcg[3:4], dcv[3:4]], axis=1)
    g["wd"] = _matmul(f_in, dpre2b, "tn", name="dw_down", out_dtype=BF, tm=1408, tn=1024, tk=1024)
    g["wugt"] = _matmul(dug, h1b, "tn", name="dw_up_g", out_dtype=BF, tm=1408, tn=1024, tk=1024)
    g["wuvt"] = _matmul(duv, h1b, "tn", name="dw_up_v", out_dtype=BF, tm=1408, tn=1024, tk=1024)
    dh1 = _matmul(dug, w["wugt"], "nn", name="dh1_g", c_in=dh1, tm=1024, tn=1024, tk=1408)
    dh1 = _matmul(duv, w["wuvt"], "nn", name="dh1_v", c_in=dh1, tm=1024, tn=1024, tk=1408)
    dx, dpre1b, dpt, dygb, dymb, dzg, dattn, dg1, db1 = _post_attn_bwd(
        dh1, pre1, pt, yg, ym, w["wgo"], w["wmo"], w["wout"], w["g1"], tm=tm)
    g["g1"], g["b1"] = dg1, db1
    g["wout"] = _matmul(mix, dpre1b, "tn", name="dw_out", out_dtype=BF, tm=1024, tn=1024, tk=1024)
    g["wgo"] = _matmul(zg, dygb, "tn", name="dw_gla_o", out_dtype=BF, tm=1024, tn=1024, tk=1024)
    g["wmo"] = _matmul(attn, dymb, "tn", name="dw_mla_o", out_dtype=BF, tm=1024, tn=1024, tk=1024)
    dqc, dkc, dv = run("flash_bwd", _flash_bwd, qc, kc, v, attn, dattn, lse, nseq=nseq, S=S, tq=tq)
    dpm, g["wuq"], g["wukv"], g["gq"], g["gkv"] = _mla_prep_bwd(
        pm, pos, invf, w["gq"], w["gkv"], w["wuq"], w["wukv"], dqc, dkc, dv, tm=tm)
    dpg, g["wg"], g["bg"], g["gn"] = run("gla_bwd", _gla_bwd, pg, w["wg"], w["bg"], w["gn"], ltri, utri, o, states,
                                         dzg, nseq=nseq, S=S, tm=tm)
    g["w_gt"] = _matmul(dpg, x2, "tn", name="dw_in_g", out_dtype=BF, tm=640, tn=1024, tk=1024)
    g["w_mt"] = _matmul(dpm, x2, "tn", name="dw_in_m", out_dtype=BF, tm=768, tn=1024, tk=1024)
    g["w_tt"] = _matmul(dpt, x2, "tn", name="dw_in_t", out_dtype=BF, tm=1024, tn=1024, tk=1024)
    dx = run("dx", _matmul_sum, dx, [(dpg, w["w_gt"], 640), (dpm, w["w_mt"], 768), (dpt, w["w_tt"], 1024)],
             name="dx")
    return loss8[0, 0], dx.reshape(nseq, S, D), g


_IN_SPLITS = (512, 512, 1024, 16, 1024, 384, 256, 64, 1024, 1024)


def _w_in_to_groups(wt):
    offs = [0]
    for s in _IN_SPLITS:
        offs.append(offs[-1] + s)
    q, k, v, r, og, cq, ckv, kr, ga, gb = [wt[offs[i]:offs[i + 1]] for i in range(10)]
    z = lambda n: jnp.zeros((n, wt.shape[1]), wt.dtype)
    return (jnp.concatenate([q, k, v, og, r, z(112)], axis=0),
            jnp.concatenate([cq, kr, z(64), ckv], axis=0),
            jnp.concatenate([ga, gb], axis=0))


def _groups_to_w_in(g_g, g_m, g_t):
    q, k, v, og, r = g_g[0:512], g_g[512:1024], g_g[1024:2048], g_g[2048:3072], g_g[3072:3088]
    cq, kr, ckv = g_m[0:384], g_m[384:448], g_m[512:768]
    return jnp.concatenate([q, k, v, r, og, cq, ckv, kr, g_t], axis=0)


def _uq_to_kernel(wuq):
    w3 = wuq.reshape(MQR, MH, NOPE + ROPE)
    rope = jnp.concatenate([w3[:, :, NOPE:], jnp.zeros((MQR, MH, 64), wuq.dtype)], axis=2)
    return jnp.concatenate([w3[:, :, :NOPE].reshape(MQR, MH * 128), rope.reshape(MQR, MH * 128)], axis=1)


def _uq_from_kernel(g):
    nope = g[:, :1024].reshape(MQR, MH, 128)
    rope = g[:, 1024:].reshape(MQR, MH, 128)[:, :, :ROPE]
    return jnp.concatenate([nope, rope], axis=2)


def _ukv_to_kernel(wukv):
    w3 = wukv.reshape(MKR, MH, NOPE + MV)
    return jnp.concatenate([w3[:, :, :NOPE].reshape(MKR, MH * 128), w3[:, :, NOPE:].reshape(MKR, MH * 128)], axis=1)


def _ukv_from_kernel(g):
    return jnp.concatenate([g[:, :1024].reshape(MKR, MH, 128), g[:, 1024:].reshape(MKR, MH, 128)], axis=2)


def _cols_gathered(a):
    return a.transpose(1, 0, 2).reshape(a.shape[1], NDEV * a.shape[2])


def _cols_scattered(a):
    R = a.shape[0]
    return a.reshape(R, NDEV, a.shape[1] // NDEV).transpose(1, 0, 2)


_SMALL = (("gla_b_gate", 512), ("gla_norm_g", 256), ("mla_q_norm_g", 384), ("mla_kv_norm_g", 256),
          ("ln1_g", 1024), ("ln1_b", 1024), ("conv_b", 5632), ("ln2_g", 1024), ("ln2_b", 1024))
_SMALL_ROWS = 88
_SMALL_USED = sum(sz for _, sz in _SMALL)


def _pack_small(d):
    flat = jnp.concatenate([d[n].reshape(-1) for n, _ in _SMALL] + ([d['loss'].reshape(-1)] if 'loss' in d else []))
    return jnp.pad(flat, (0, _SMALL_ROWS * 128 - flat.shape[0])).reshape(_SMALL_ROWS, 128)


def _unpack_small(a):
    flat = a.reshape(-1)
    out, off = {}, 0
    for n, sz in _SMALL:
        out[n] = flat[off:off + sz].reshape(1, sz)
        off += sz
    return out


_NAMES = ['w_in', 'gla_w_gate_up', 'gla_b_gate', 'gla_norm_g', 'w_gla_o', 'mla_q_norm_g', 'mla_w_uq',
          'mla_kv_norm_g', 'mla_w_ukv', 'w_mla_o', 'w_out', 'ln1_g', 'ln1_b', 'w_up', 'conv_w', 'conv_b',
          'w_down', 'ln2_g', 'ln2_b']
_SHARDED = ['w_in', 'w_up', 'w_down', 'w_gla_o', 'w_mla_o', 'w_out', 'mla_w_uq', 'mla_w_ukv', 'gla_w_gate_up',
            'conv_w']


def kernel(x, positions, w_in, gla_w_gate_up, gla_b_gate, gla_norm_g, w_gla_o, mla_q_norm_g, mla_w_uq, mla_kv_norm_g, mla_w_ukv, w_mla_o, w_out, ln1_g, ln1_b, w_up, conv_w, conv_b, w_down, ln2_g, ln2_b, loss_target, m_w_in, m_gla_w_gate_up, m_gla_b_gate, m_gla_norm_g, m_w_gla_o, m_mla_q_norm_g, m_mla_w_uq, m_mla_kv_norm_g, m_mla_w_ukv, m_w_mla_o, m_w_out, m_ln1_g, m_ln1_b, m_w_up, m_conv_w, m_conv_b, m_w_down, m_ln2_g, m_ln2_b, v_w_in, v_gla_w_gate_up, v_gla_b_gate, v_gla_norm_g, v_w_gla_o, v_mla_q_norm_g, v_mla_w_uq, v_mla_kv_norm_g, v_mla_w_ukv, v_w_mla_o, v_w_out, v_ln1_g, v_ln1_b, v_w_up, v_conv_w, v_conv_b, v_w_down, v_ln2_g, v_ln2_b):
    W = dict(w_in=w_in, gla_w_gate_up=gla_w_gate_up, gla_b_gate=gla_b_gate, gla_norm_g=gla_norm_g, w_gla_o=w_gla_o, mla_q_norm_g=mla_q_norm_g, mla_w_uq=mla_w_uq, mla_kv_norm_g=mla_kv_norm_g, mla_w_ukv=mla_w_ukv, w_mla_o=w_mla_o, w_out=w_out, ln1_g=ln1_g, ln1_b=ln1_b, w_up=w_up, conv_w=conv_w, conv_b=conv_b, w_down=w_down, ln2_g=ln2_g, ln2_b=ln2_b)
    M = dict(w_in=m_w_in, gla_w_gate_up=m_gla_w_gate_up, gla_b_gate=m_gla_b_gate, gla_norm_g=m_gla_norm_g, w_gla_o=m_w_gla_o, mla_q_norm_g=m_mla_q_norm_g, mla_w_uq=m_mla_w_uq, mla_kv_norm_g=m_mla_kv_norm_g, mla_w_ukv=m_mla_w_ukv, w_mla_o=m_w_mla_o, w_out=m_w_out, ln1_g=m_ln1_g, ln1_b=m_ln1_b, w_up=m_w_up, conv_w=m_conv_w, conv_b=m_conv_b, w_down=m_w_down, ln2_g=m_ln2_g, ln2_b=m_ln2_b)
    V = dict(w_in=v_w_in, gla_w_gate_up=v_gla_w_gate_up, gla_b_gate=v_gla_b_gate, gla_norm_g=v_gla_norm_g, w_gla_o=v_w_gla_o, mla_q_norm_g=v_mla_q_norm_g, mla_w_uq=v_mla_w_uq, mla_kv_norm_g=v_mla_kv_norm_g, mla_w_ukv=v_mla_w_ukv, w_mla_o=v_w_mla_o, w_out=v_w_out, ln1_g=v_ln1_g, ln1_b=v_ln1_b, w_up=v_w_up, conv_w=v_conv_w, conv_b=v_conv_b, w_down=v_w_down, ln2_g=v_ln2_g, ln2_b=v_ln2_b)

    tshard = lambda d, n: d[n][0].T
    shard = lambda n: (W[n][0].astype(BF), False)
    first = ['w_in', 'mla_w_uq', 'mla_w_ukv', 'gla_w_gate_up']
    G = dict(zip(first, _gather_two_level(
        [tshard(W, 'w_in').astype(BF)] + [shard(n)[0] for n in first[1:]], name="gather_w0")))
    w_gt, w_mt, w_tt = _w_in_to_groups(G['w_in'].reshape(NDEV * 730, D))
    kw = dict(
        w_gt=w_gt, w_mt=w_mt, w_tt=w_tt,
        wg=jnp.pad(_cols_gathered(G['gla_w_gate_up']), ((0, 128 - GR), (0, 0))), bg=W['gla_b_gate'],
        gn=W['gla_norm_g'], gq=W['mla_q_norm_g'], gkv=W['mla_kv_norm_g'],
        wuq=_uq_to_kernel(_cols_gathered(G['mla_w_uq'])), wukv=_ukv_to_kernel(_cols_gathered(G['mla_w_ukv'])),
        g1=W['ln1_g'], b1=W['ln1_b'], g2=W['ln2_g'], b2=W['ln2_b'], cb=W['conv_b'],
    )
    received = {}

    def got_out_proj(ex, w, g):
        w.update(wgo=ex[0].reshape(D, D), wmo=ex[1].reshape(D, D), wout=ex[2].reshape(D, D))

    def got_ffn(ex, w, g):
        w_upt = ex[0].reshape(2 * DFF, D)
        w.update(wugt=w_upt[:DFF], wuvt=w_upt[DFF:], wd=ex[1].reshape(DFF, D), cw=_cols_gathered(ex[2]))

    slab = lambda a: (a.astype(BF), True)
    rows = lambda a: a.reshape(NDEV, a.shape[0] // NDEV, a.shape[1])

    def keep(names):
        return lambda ex, w, g: received.update(zip(names, ex))

    def small_grads(g):
        return _pack_small(dict(gla_b_gate=g['bg'], gla_norm_g=g['gn'], mla_q_norm_g=g['gq'], mla_kv_norm_g=g['gkv'],
                                ln1_g=g['g1'], ln1_b=g['b1'], conv_b=g['cb'], ln2_g=g['g2'], ln2_b=g['b2'],
                                loss=g['loss']))

    hooks = {
        "proj_g": (lambda w, g: [shard('w_gla_o'), shard('w_mla_o'), shard('w_out')], got_out_proj),
        "flash_fwd": (lambda w, g: [(tshard(W, 'w_up').astype(BF), False), shard('w_down'), (W['conv_w'][0], False)],
                      got_ffn),
        "flash_bwd": (lambda w, g: [slab(rows(g['wd'])), slab(rows(jnp.concatenate([g['wugt'], g['wuvt']], axis=0)))],
                      keep(['w_down', 'w_up'])),
        "gla_bwd": (lambda w, g: [slab(rows(g['wout'])), slab(rows(g['wgo'])), slab(rows(g['wmo'])),
                                  slab(_uq_from_kernel(g['wuq']).transpose(1, 0, 2)),
                                  slab(_ukv_from_kernel(g['wukv']).transpose(1, 0, 2))],
                    keep(['w_out', 'w_gla_o', 'w_mla_o', 'mla_w_uq', 'mla_w_ukv'])),
        "dx": (lambda w, g: [slab(rows(_groups_to_w_in(g['w_gt'], g['w_mt'], g['w_tt']))),
                             (_cols_scattered(g['wg'][:GR]), True), (_cols_scattered(g['cw']), True),
                             (small_grads(g), False)],
               keep(['w_in', 'gla_w_gate_up', 'conv_w', 'small'])),
    }

    _, grad_x, _ = _local_step(x, positions, loss_target, kw, hooks)

    grads, deltas, new_m, new_v = {}, {}, {}, {}
    small_parts = received['small']
    loss = jnp.sum(small_parts.reshape(NDEV, -1)[:, _SMALL_USED])
    for n in _SHARDED:
        shp = W[n].shape
        if n in ('w_in', 'w_up'):
            out = _adamw(received[n], tshard(W, n), tshard(M, n), tshard(V, n), name="adamw_" + n)
            grads[n], deltas[n], new_m[n], new_v[n] = [t.T.reshape(shp) for t in out]
            continue
        out = _adamw(received[n], W[n][0], M[n][0], V[n][0], name="adamw_" + n)
        grads[n], deltas[n], new_m[n], new_v[n] = [t.reshape(shp) for t in out]
    out = _adamw(small_parts, _pack_small(W), _pack_small(M), _pack_small(V), name="adamw_small")
    for dst, packed in zip((grads, deltas, new_m, new_v), out):
        dst.update(_unpack_small(packed))

    return (loss, grad_x, *[grads[n] for n in _NAMES], *[deltas[n] for n in _NAMES],
            *[new_m[n] for n in _NAMES], *[new_v[n] for n in _NAMES])
```

```python
import functools

import jax
import jax.numpy as jnp
from jax import lax
from jax.experimental import pallas as pl
from jax.experimental.pallas import tpu as pltpu

F32 = jnp.float32
BF = jnp.bfloat16

D = 1024
GH, GDK, GDV, GR, GTAU, GC = 4, 128, 256, 16, 16.0, 64
MH, MQR, MKR, NOPE, ROPE, MV = 8, 384, 256, 128, 64, 128
THETA = 10000.0
DFF = 2816
ALPHA = 2.0 ** 0.25
LN_EPS = 1e-5
RMS_EPS = 1e-6
NDEV = 8
ADAM_LR, ADAM_B1, ADAM_B2, ADAM_EPS, ADAM_WD, ADAM_STEP = 0.001, 0.9, 0.999, 1e-08, 0.01, 10

PG_W = 3200
PM_W = 768
PT_W = 2048
NEG = -1e30
MESH_ID = pl.DeviceIdType.MESH
VMEM_MB = 1024 * 1024


def _params(sem, vmem_mb=48):
    return pltpu.CompilerParams(dimension_semantics=sem, vmem_limit_bytes=vmem_mb * VMEM_MB)


def _dot(a, b):
    return lax.dot_general(a, b, (((1,), (0,)), ((), ())), preferred_element_type=F32)


def _dot_nt(a, b):
    return lax.dot_general(a, b, (((1,), (1,)), ((), ())), preferred_element_type=F32)


def _dot_tn(a, b):
    return lax.dot_general(a, b, (((0,), (0,)), ((), ())), preferred_element_type=F32)


def _iota(shape, dim):
    return lax.broadcasted_iota(jnp.int32, shape, dim)


FLASH_HP = 2
QK_SCALE = (NOPE + ROPE) ** -0.5
LOG2E = 1.4426950408889634
QK_SCALE_LOG2 = QK_SCALE * LOG2E


def _sigmoid(x):
    return 0.5 * jnp.tanh(0.5 * x) + 0.5


def _tri_mm(tri_bf, x):
    hi = x.astype(BF)
    r1 = x - hi.astype(F32)
    mid = r1.astype(BF)
    lo = (r1 - mid.astype(F32)).astype(BF)
    return _dot(tri_bf, hi) + _dot(tri_bf, mid) + _dot(tri_bf, lo)


def _matmul(a, b, mode, *, name, c_in=None, out_dtype=F32, tm=512, tn=512, tk=512, ride=None):
    if mode == "nn":
        (M, K), (_, N) = a.shape, b.shape
    elif mode == "nt":
        (M, K), (N, _) = a.shape, b.shape
    else:
        (K, M), (_, N) = a.shape, b.shape
    tm, tn, tk = min(tm, M), min(tn, N), min(tk, K)
    assert M % tm == 0 and N % tn == 0 and K % tk == 0, (name, M, N, K, tm, tn, tk)
    nk = K // tk
    dot = {"nn": _dot, "nt": _dot_nt, "tn": _dot_tn}[mode]

    def body(*refs):
        if c_in is None:
            a_ref, b_ref, o_ref, acc_ref = refs
        else:
            a_ref, b_ref, c_ref, o_ref, acc_ref = refs
        k = pl.program_id(2)

        @pl.when(k == 0)
        def _():
            if c_in is None:
                acc_ref[...] = jnp.zeros_like(acc_ref)
            else:
                acc_ref[...] = c_ref[...].astype(F32)

        acc_ref[...] += dot(a_ref[...].astype(BF), b_ref[...].astype(BF))

        @pl.when(k == nk - 1)
        def _():
            o_ref[...] = acc_ref[...].astype(out_dtype)

    if mode == "tn":
        a_spec = pl.BlockSpec((tk, tm), lambda i, j, k: (k, i))
    else:
        a_spec = pl.BlockSpec((tm, tk), lambda i, j, k: (i, k))
    if mode == "nt":
        b_spec = pl.BlockSpec((tn, tk), lambda i, j, k: (j, k))
    else:
        b_spec = pl.BlockSpec((tk, tn), lambda i, j, k: (k, j))
    in_specs = [a_spec, b_spec]
    args = [a, b]
    if c_in is not None:
        in_specs.append(pl.BlockSpec((tm, tn), lambda i, j, k: (i, j)))
        args.append(c_in)
    res = _call(
        body, name=name,
        out_shape=(jax.ShapeDtypeStruct((M, N), out_dtype),),
        grid=(M // tm, N // tn, nk),
        in_specs=in_specs,
        out_specs=(pl.BlockSpec((tm, tn), lambda i, j, k: (i, j)),),
        scratch_shapes=[pltpu.VMEM((tm, tn), F32)],
        sem=("parallel", "parallel", "arbitrary"), args=args, ride=ride)
    return res[0] if ride is None else (res[0][0], res[1])


def _matmul_sum(c_in, parts, *, name, tm=512, ride=None):
    M, N = c_in.shape
    tm = min(tm, M)
    n_p = len(parts)
    counts = [a.shape[1] // tk for a, _, tk in parts]
    starts = [sum(counts[:p]) for p in range(n_p)]
    nk = sum(counts)

    def body(*refs):
        a_refs, w_refs = refs[:n_p], refs[n_p:2 * n_p]
        c_ref, o_ref, acc_ref = refs[2 * n_p:]
        k = pl.program_id(1)

        @pl.when(k == 0)
        def _():
            acc_ref[...] = c_ref[...]

        for p in range(n_p):
            @pl.when(jnp.logical_and(k >= starts[p], k < starts[p] + counts[p]))
            def _(p=p):
                acc_ref[...] += _dot(a_refs[p][...].astype(BF), w_refs[p][...].astype(BF))

        @pl.when(k == nk - 1)
        def _():
            o_ref[...] = acc_ref[...]

    def kidx(p):
        return lambda k: jnp.clip(k - starts[p], 0, counts[p] - 1)

    in_specs = [pl.BlockSpec((tm, tk), lambda i, k, f=kidx(p): (i, f(k))) for p, (_, _, tk) in enumerate(parts)]
    in_specs += [pl.BlockSpec((tk, N), lambda i, k, f=kidx(p): (f(k), 0)) for p, (_, _, tk) in enumerate(parts)]
    in_specs.append(pl.BlockSpec((tm, N), lambda i, k: (i, 0)))
    res = _call(
        body, name=name, out_shape=(jax.ShapeDtypeStruct((M, N), F32),), grid=(M // tm, nk),
        in_specs=in_specs, out_specs=(pl.BlockSpec((tm, N), lambda i, k: (i, 0)),),
        scratch_shapes=[pltpu.VMEM((tm, N), F32)], sem=("parallel", "arbitrary"),
        args=[a for a, _, _ in parts] + [w for _, w, _ in parts] + [c_in], ride=ride)
    return res[0] if ride is None else (res[0][0], res[1])


def _gla_gate(pg_ref, rows, wg_ref, bg_ref):
    r = pg_ref[rows, 3072:3200].astype(BF)
    logit = _dot(r, wg_ref[...]) + bg_ref[...]
    la = (jnp.minimum(logit, 0.0) - jnp.log(1.0 + jnp.exp(-jnp.abs(logit)))) * (1.0 / GTAU)
    return r, logit, la


def _gla_fwd(pg, wg, bg, gn, ltri, *, nseq, S, tm):
    T = pg.shape[0]
    nb, nc = S // tm, tm // GC
    qscale = GDK ** -0.5

    def body(pg_ref, wg_ref, bg_ref, gn_ref, l_ref, o_ref, zg_ref, st_ref, st_scr):
        @pl.when(pl.program_id(1) == 0)
        def _():
            st_scr[...] = jnp.zeros_like(st_scr)

        ltri_v = l_ref[...]
        causal = _iota((GC, GC), 0) >= _iota((GC, GC), 1)
        last_row = _iota((GC, GDK), 0) == GC - 1
        g = gn_ref[...]

        def chunk(c, carry):
            rows = pl.ds(pl.multiple_of(c * GC, GC), GC)
            _, _, la = _gla_gate(pg_ref, rows, wg_ref, bg_ref)
            b = _tri_mm(ltri_v, la)
            for h in range(GH):
                q = pg_ref[rows, h * GDK:(h + 1) * GDK]
                k = pg_ref[rows, 512 + h * GDK:512 + (h + 1) * GDK]
                v = pg_ref[rows, 1024 + h * GDV:1024 + (h + 1) * GDV].astype(BF)
                og = pg_ref[rows, 2048 + h * GDV:2048 + (h + 1) * GDV]
                bh = b[:, h * GDK:(h + 1) * GDK]
                bl = jnp.sum(jnp.where(last_row, bh, 0.0), axis=0, keepdims=True)
                q_in = (q * (qscale * jnp.exp(bh))).astype(BF)
                k_in = (k * jnp.exp(-bh)).astype(BF)
                k_st = (k * jnp.exp(bl - bh)).astype(BF)
                dec = jnp.exp(bl)
                st = st_scr[h]
                st_ref[c, h] = st
                att = jnp.where(causal, _dot_nt(q_in, k_in), 0.0).astype(BF)
                o = _dot(att, v) + _dot_nt(q_in, st.astype(BF))
                st_scr[h] = st * dec + _dot_tn(v, k_st)
                rstd = lax.rsqrt(jnp.mean(o * o, axis=-1, keepdims=True) + RMS_EPS)
                o_ref[rows, h * GDV:(h + 1) * GDV] = o
                zg_ref[rows, h * GDV:(h + 1) * GDV] = (o * rstd * g * (og * _sigmoid(og))).astype(BF)
            return carry

        lax.fori_loop(0, nc, chunk, 0, unroll=True)

    full = lambda shp: pl.BlockSpec(shp, lambda b_, i: (0,) * len(shp))
    return pl.pallas_call(
        body, name="gla_fwd",
        out_shape=(jax.ShapeDtypeStruct((T, GH * GDV), F32),
                   jax.ShapeDtypeStruct((T, GH * GDV), BF),
                   jax.ShapeDtypeStruct((T // GC, GH, GDV, GDK), F32)),
        grid=(nseq, nb),
        in_specs=[pl.BlockSpec((tm, PG_W), lambda b_, i: (b_ * nb + i, 0)),
                  full((128, 512)), full((1, 512)), full((1, GDV)), full((GC, GC))],
        out_specs=(pl.BlockSpec((tm, GH * GDV), lambda b_, i: (b_ * nb + i, 0)),
                   pl.BlockSpec((tm, GH * GDV), lambda b_, i: (b_ * nb + i, 0)),
                   pl.BlockSpec((nc, GH, GDV, GDK), lambda b_, i: (b_ * nb + i, 0, 0, 0))),
        scratch_shapes=[pltpu.VMEM((GH, GDV, GDK), F32)],
        compiler_params=_params(("parallel", "arbitrary")),
    )(pg, wg, bg, gn, ltri)


def _gla_bwd(pg, wg, bg, gn, ltri, utri, o, states, dzg, *, nseq, S, tm, ride=None):
    T = pg.shape[0]
    nb, nc = S // tm, tm // GC
    qscale = GDK ** -0.5

    def body(pg_ref, wg_ref, bg_ref, gn_ref, l_ref, u_ref, o_ref, st_ref, dzg_ref,
             dpg_ref, dwg_ref, dbg_ref, dgn_ref, dst_scr):
        first = jnp.logical_and(pl.program_id(0) == 0, pl.program_id(1) == 0)

        @pl.when(first)
        def _():
            dwg_ref[...] = jnp.zeros_like(dwg_ref)
            dbg_ref[...] = jnp.zeros_like(dbg_ref)
            dgn_ref[...] = jnp.zeros_like(dgn_ref)

        @pl.when(pl.program_id(1) == 0)
        def _():
            dst_scr[...] = jnp.zeros_like(dst_scr)

        ltri_v = l_ref[...]
        utri_v = u_ref[...]
        causal = _iota((GC, GC), 0) >= _iota((GC, GC), 1)
        last_row = _iota((GC, GDK), 0) == GC - 1
        g = gn_ref[...]

        def chunk(cc, carry):
            c = nc - 1 - cc
            rows = pl.ds(pl.multiple_of(c * GC, GC), GC)
            r, logit, la = _gla_gate(pg_ref, rows, wg_ref, bg_ref)
            b = _tri_mm(ltri_v, la)
            dbs = []
            for h in range(GH):
                q = pg_ref[rows, h * GDK:(h + 1) * GDK]
                k = pg_ref[rows, 512 + h * GDK:512 + (h + 1) * GDK]
                vb = pg_ref[rows, 1024 + h * GDV:1024 + (h + 1) * GDV].astype(BF)
                og = pg_ref[rows, 2048 + h * GDV:2048 + (h + 1) * GDV]
                oh = o_ref[rows, h * GDV:(h + 1) * GDV]
                dz = dzg_ref[rows, h * GDV:(h + 1) * GDV].astype(F32)
                bh = b[:, h * GDK:(h + 1) * GDK]
                bl = jnp.sum(jnp.where(last_row, bh, 0.0), axis=0, keepdims=True)
                eb = qscale * jnp.exp(bh)
                enb = jnp.exp(-bh)
                ek = jnp.exp(bl - bh)
                dec = jnp.exp(bl)
                q_in = q * eb
                k_in = k * enb
                k_st = k * ek
                q_inb, k_inb, k_stb = q_in.astype(BF), k_in.astype(BF), k_st.astype(BF)
                st = st_ref[c, h]
                dst = dst_scr[h]
                rstd = lax.rsqrt(jnp.mean(oh * oh, axis=-1, keepdims=True) + RMS_EPS)
                ohat = oh * rstd
                sg = _sigmoid(og)
                don = dz * (og * sg)
                dog = dz * (ohat * g) * (sg * (1.0 + og * (1.0 - sg)))
                dgn_ref[...] += jnp.sum(don * ohat, axis=0, keepdims=True)
                gd = don * g
                do = rstd * (gd - ohat * jnp.mean(gd * ohat, axis=-1, keepdims=True))
                dob = do.astype(BF)
                att = jnp.where(causal, _dot_nt(q_inb, k_inb), 0.0).astype(BF)
                da = jnp.where(causal, _dot_nt(dob, vb), 0.0).astype(BF)
                dstb = dst.astype(BF)
                dqi = _dot(da, k_inb) + _dot(dob, st.astype(BF))
                dki = _dot_tn(da, q_inb)
                dv = _dot_tn(att, dob) + _dot_nt(k_stb, dstb)
                dks = _dot(vb, dstb)
                dd = jnp.sum(dst * st, axis=0, keepdims=True)
                dst_scr[h] = dst * dec + _dot_tn(dob, q_inb)
                dq = dqi * eb
                dk = dki * enb + dks * ek
                kk = dks * k_st
                dbl = jnp.sum(kk, axis=0, keepdims=True) + dd * dec
                db = dqi * q_in - dki * k_in - kk
                dbs.append(db + jnp.where(last_row, dbl, 0.0))
                dpg_ref[rows, h * GDK:(h + 1) * GDK] = dq.astype(BF)
                dpg_ref[rows, 512 + h * GDK:512 + (h + 1) * GDK] = dk.astype(BF)
                dpg_ref[rows, 1024 + h * GDV:1024 + (h + 1) * GDV] = dv.astype(BF)
                dpg_ref[rows, 2048 + h * GDV:2048 + (h + 1) * GDV] = dog.astype(BF)
            dla = _tri_mm(utri_v, jnp.concatenate(dbs, axis=1))
            dlogit = dla * (1.0 / GTAU) * _sigmoid(-logit)
            dlb = dlogit.astype(BF)
            dpg_ref[rows, 3072:3200] = _dot_nt(dlb, wg_ref[...]).astype(BF)
            dwg_ref[...] += _dot_tn(r, dlb)
            dbg_ref[...] += jnp.sum(dlogit, axis=0, keepdims=True)
            return carry

        lax.fori_loop(0, nc, chunk, 0)

    full = lambda shp: pl.BlockSpec(shp, lambda b_, i: (0,) * len(shp))
    rev = lambda b_, i: (b_ * nb + nb - 1 - i, 0)
    return _call(
        body, name="gla_bwd", ride=ride, sem=("arbitrary", "arbitrary"),
        args=(pg, wg, bg, gn, ltri, utri, o, states, dzg),
        out_shape=(jax.ShapeDtypeStruct((T, PG_W), BF),
                   jax.ShapeDtypeStruct((128, 512), F32),
                   jax.ShapeDtypeStruct((1, 512), F32),
                   jax.ShapeDtypeStruct((1, GDV), F32)),
        grid=(nseq, nb),
        in_specs=[pl.BlockSpec((tm, PG_W), rev),
                  full((128, 512)), full((1, 512)), full((1, GDV)), full((GC, GC)), full((GC, GC)),
                  pl.BlockSpec((tm, GH * GDV), rev),
                  pl.BlockSpec((nc, GH, GDV, GDK), lambda b_, i: (b_ * nb + nb - 1 - i, 0, 0, 0)),
                  pl.BlockSpec((tm, GH * GDV), rev)],
        out_specs=(pl.BlockSpec((tm, PG_W), rev), full((128, 512)), full((1, 512)), full((1, GDV))),
        scratch_shapes=[pltpu.VMEM((GH, GDV, GDK), F32)])


def _rope_tables(pos, invf):
    ang = pos.astype(F32) * invf
    lane = _iota(ang.shape, 1)
    sin = jnp.sin(ang)
    ssin = jnp.where(lane < 32, -sin, jnp.where(lane < 64, sin, 0.0))
    return jnp.cos(ang), ssin, lane


def _rope(x, cos, ssin, lane, sign):
    rot = jnp.where(lane < 32, pltpu.roll(x, 96, 1), pltpu.roll(x, 32, 1))
    return x * cos + sign * (rot * ssin)


def _rms_fwd(x, g):
    rstd = lax.rsqrt(jnp.mean(x * x, axis=-1, keepdims=True) + RMS_EPS)
    return x * rstd * g, x * rstd, rstd


def _rms_bwd(dy, xhat, rstd, g):
    gd = dy * g
    return rstd * (gd - xhat * jnp.mean(gd * xhat, axis=-1, keepdims=True)), jnp.sum(dy * xhat, axis=0, keepdims=True)


def _mla_prep_fwd(pm, pos, invf, gq, gkv, wuq, wukv, *, tm):
    T = pm.shape[0]

    def body(pm_ref, pos_ref, invf_ref, gq_ref, gkv_ref, wuq_ref, wukv_ref, qc_ref, kc_ref, v_ref):
        cos, ssin, lane = _rope_tables(pos_ref[...], invf_ref[...])
        cq, _, _ = _rms_fwd(pm_ref[:, 0:MQR], gq_ref[...])
        ckv, _, _ = _rms_fwd(pm_ref[:, 512:768], gkv_ref[...])
        qf = _dot(cq.astype(BF), wuq_ref[...])
        kvf = _dot(ckv.astype(BF), wukv_ref[...])
        kr = _rope(pm_ref[:, 384:512], cos, ssin, lane, 1.0).astype(BF)
        for h in range(MH):
            qc_ref[:, 256 * h:256 * h + 128] = (QK_SCALE_LOG2 * qf[:, 128 * h:128 * h + 128]).astype(BF)
            qr = qf[:, 1024 + 128 * h:1024 + 128 * h + 128]
            qc_ref[:, 256 * h + 128:256 * h + 256] = (QK_SCALE_LOG2 * _rope(qr, cos, ssin, lane, 1.0)).astype(BF)
            kc_ref[:, 256 * h:256 * h + 128] = kvf[:, 128 * h:128 * h + 128].astype(BF)
            kc_ref[:, 256 * h + 128:256 * h + 256] = kr
        v_ref[...] = kvf[:, 1024:2048].astype(BF)

    full = lambda shp: pl.BlockSpec(shp, lambda i: (0,) * len(shp))
    row = lambda w: pl.BlockSpec((tm, w), lambda i: (i, 0))
    return pl.pallas_call(
        body, name="mla_prep_fwd",
        out_shape=(jax.ShapeDtypeStruct((T, MH * 256), BF), jax.ShapeDtypeStruct((T, MH * 256), BF),
                   jax.ShapeDtypeStruct((T, MH * MV), BF)),
        grid=(T // tm,),
        in_specs=[row(PM_W), row(1), full((1, 128)), full((1, MQR)), full((1, MKR)),
                  full((MQR, 2048)), full((MKR, 2048))],
        out_specs=(row(MH * 256), row(MH * 256), row(MH * MV)),
        compiler_params=_params(("parallel",)),
    )(pm, pos, invf, gq, gkv, wuq, wukv)


def _mla_prep_bwd(pm, pos, invf, gq, gkv, wuq, wukv, dqc, dkc, dv, *, tm):
    T = pm.shape[0]

    def body(pm_ref, pos_ref, invf_ref, gq_ref, gkv_ref, wuq_ref, wukv_ref, dqc_ref, dkc_ref, dv_ref,
             dpm_ref, dwuq_ref, dwukv_ref, dgq_ref, dgkv_ref):
        @pl.when(pl.program_id(0) == 0)
        def _():
            dwuq_ref[...] = jnp.zeros_like(dwuq_ref)
            dwukv_ref[...] = jnp.zeros_like(dwukv_ref)
            dgq_ref[...] = jnp.zeros_like(dgq_ref)
            dgkv_ref[...] = jnp.zeros_like(dgkv_ref)

        cos, ssin, lane = _rope_tables(pos_ref[...], invf_ref[...])
        cq, cqh, cq_rstd = _rms_fwd(pm_ref[:, 0:MQR], gq_ref[...])
        ckv, ckvh, ckv_rstd = _rms_fwd(pm_ref[:, 512:768], gkv_ref[...])
        dqn, dqr, dkn = [], [], []
        dkr = jnp.zeros((tm, 128), F32)
        for h in range(MH):
            dqn.append(dqc_ref[:, 256 * h:256 * h + 128].astype(BF))
            dqr.append(_rope(dqc_ref[:, 256 * h + 128:256 * h + 256], cos, ssin, lane, -1.0).astype(BF))
            dkn.append(dkc_ref[:, 256 * h:256 * h + 128].astype(BF))
            dkr = dkr + dkc_ref[:, 256 * h + 128:256 * h + 256]
        dqf = jnp.concatenate(dqn + dqr, axis=1)
        dkvf = jnp.concatenate(dkn + [dv_ref[...].astype(BF)], axis=1)
        dwuq_ref[...] += _dot_tn(cq.astype(BF), dqf)
        dwukv_ref[...] += _dot_tn(ckv.astype(BF), dkvf)
        dcq, dgq = _rms_bwd(_dot_nt(dqf, wuq_ref[...]), cqh, cq_rstd, gq_ref[...])
        dckv, dgkv = _rms_bwd(_dot_nt(dkvf, wukv_ref[...]), ckvh, ckv_rstd, gkv_ref[...])
        dgq_ref[...] += dgq
        dgkv_ref[...] += dgkv
        dpm_ref[:, 0:MQR] = dcq.astype(BF)
        dpm_ref[:, 384:512] = _rope(dkr, cos, ssin, lane, -1.0).astype(BF)
        dpm_ref[:, 512:768] = dckv.astype(BF)

    full = lambda shp: pl.BlockSpec(shp, lambda i: (0,) * len(shp))
    row = lambda w: pl.BlockSpec((tm, w), lambda i: (i, 0))
    return pl.pallas_call(
        body, name="mla_prep_bwd",
        out_shape=(jax.ShapeDtypeStruct((T, PM_W), BF), jax.ShapeDtypeStruct((MQR, 2048), F32),
                   jax.ShapeDtypeStruct((MKR, 2048), F32), jax.ShapeDtypeStruct((1, MQR), F32),
                   jax.ShapeDtypeStruct((1, MKR), F32)),
        grid=(T // tm,),
        in_specs=[row(PM_W), row(1), full((1, 128)), full((1, MQR)), full((1, MKR)),
                  full((MQR, 2048)), full((MKR, 2048)), row(MH * 256), row(MH * 256), row(MH * MV)],
        out_specs=(row(PM_W), full((MQR, 2048)), full((MKR, 2048)), full((1, MQR)), full((1, MKR))),
        compiler_params=_params(("arbitrary",)),
    )(pm, pos, invf, gq, gkv, wuq, wukv, dqc, dkc, dv)


def _flash_fwd(qc, kc, v, *, nseq, S, tq, ride=None):
    T = qc.shape[0]
    nq = S // tq

    def body(q_ref, k_ref, v_ref, o_ref, lse_ref):
        i = pl.program_id(2)
        causal = _iota((tq, tq), 0) >= _iota((tq, tq), 1)

        def step(j, carry, masked):
            rows = pl.ds(pl.multiple_of(j * tq, tq), tq)
            out = []
            for hh in range(FLASH_HP):
                m, l, acc = carry[hh]
                s = _dot_nt(q_ref[:, 256 * hh:256 * hh + 256], k_ref[rows, 256 * hh:256 * hh + 256])
                if masked:
                    s = jnp.where(causal, s, NEG)
                m_new = jnp.maximum(m, jnp.max(s, axis=-1, keepdims=True))
                p = jnp.exp2(s - m_new)
                a = jnp.exp2(m - m_new)
                l = a * l + jnp.sum(p, axis=-1, keepdims=True)
                acc = a * acc + _dot(p.astype(BF), v_ref[rows, MV * hh:MV * hh + MV])
                out.append((m_new, l, acc))
            return tuple(out)

        init = ((jnp.full((tq, 1), NEG, F32), jnp.zeros((tq, 1), F32), jnp.zeros((tq, MV), F32)),) * FLASH_HP
        carry = lax.fori_loop(0, i, lambda j, c: step(j, c, False), init)
        for hh, (m, l, acc) in enumerate(step(i, carry, True)):
            o_ref[:, MV * hh:MV * hh + MV] = (acc / l).astype(BF)
            lse_ref[:, 128 * hh:128 * hh + 128] = jnp.broadcast_to(m + jnp.log2(l), (tq, 128))

    hp = FLASH_HP
    return _call(
        body, name="flash_fwd", ride=ride, sem=("parallel", "parallel", "arbitrary"), args=(qc, kc, v),
        out_shape=(jax.ShapeDtypeStruct((T, MH * MV), BF), jax.ShapeDtypeStruct((T, MH * 128), F32)),
        grid=(nseq, MH // hp, nq),
        in_specs=[pl.BlockSpec((tq, 256 * hp), lambda b_, h, i: (b_ * nq + i, h)),
                  pl.BlockSpec((S, 256 * hp), lambda b_, h, i: (b_, h)),
                  pl.BlockSpec((S, MV * hp), lambda b_, h, i: (b_, h))],
        out_specs=(pl.BlockSpec((tq, MV * hp), lambda b_, h, i: (b_ * nq + i, h)),
                   pl.BlockSpec((tq, 128 * hp), lambda b_, h, i: (b_ * nq + i, h))))


def _flash_bwd(qc, kc, v, o, do, lse, *, nseq, S, tq, ride=None):
    T = qc.shape[0]
    nq = S // tq

    def body(q_ref, k_ref, v_ref, o_ref, do_ref, lse_ref, dq_ref, dk_ref, dv_ref, dq_scr, delta_scr):
        j = pl.program_id(2)

        @pl.when(j == 0)
        def _():
            dq_scr[...] = jnp.zeros_like(dq_scr)
            for hh in range(FLASH_HP):
                od = o_ref[:, MV * hh:MV * hh + MV].astype(F32) * do_ref[:, MV * hh:MV * hh + MV].astype(F32)
                delta_scr[:, 128 * hh:128 * hh + 128] = jnp.broadcast_to(jnp.sum(od, axis=-1, keepdims=True), (S, 128))

        causal = _iota((tq, tq), 0) >= _iota((tq, tq), 1)

        def step(i, carry, masked):
            rows = pl.ds(pl.multiple_of(i * tq, tq), tq)
            out = []
            for hh in range(FLASH_HP):
                dk, dv = carry[hh]
                qs, vs, ls = slice(256 * hh, 256 * hh + 256), slice(MV * hh, MV * hh + MV), slice(128 * hh, 128 * hh + 1)
                q = q_ref[rows, qs]
                dob = do_ref[rows, vs]
                kb = k_ref[:, qs]
                p = jnp.exp2(_dot_nt(q, kb) - lse_ref[rows, ls])
                if masked:
                    p = jnp.where(causal, p, 0.0)
                dv = dv + _dot_tn(p.astype(BF), dob)
                dp = _dot_nt(dob, v_ref[:, vs])
                ds = (p * (dp - delta_scr[rows, ls])).astype(BF)
                dk = dk + _dot_tn(ds, q)
                dq_scr[rows, qs] += _dot(ds, kb)
                out.append((dk, dv))
            return tuple(out)

        init = ((jnp.zeros((tq, 256), F32), jnp.zeros((tq, MV), F32)),) * FLASH_HP
        carry = step(j, init, True)
        carry = lax.fori_loop(j + 1, nq, lambda i, c: step(i, c, False), carry)
        for hh, (dk, dv) in enumerate(carry):
            dk_ref[:, 256 * hh:256 * hh + 256] = dk * (1.0 / LOG2E)
            dv_ref[:, MV * hh:MV * hh + MV] = dv

        @pl.when(j == nq - 1)
        def _():
            dq_ref[...] = dq_scr[...] * QK_SCALE

    hp = FLASH_HP
    seq = lambda w: pl.BlockSpec((S, w * hp), lambda b_, h, j: (b_, h))
    blk = lambda w: pl.BlockSpec((tq, w * hp), lambda b_, h, j: (b_ * nq + j, h))
    return _call(
        body, name="flash_bwd", ride=ride, sem=("parallel", "parallel", "arbitrary"), args=(qc, kc, v, o, do, lse),
        out_shape=(jax.ShapeDtypeStruct((T, MH * 256), F32), jax.ShapeDtypeStruct((T, MH * 256), F32),
                   jax.ShapeDtypeStruct((T, MH * MV), F32)),
        grid=(nseq, MH // hp, nq),
        in_specs=[seq(256), blk(256), blk(MV), seq(MV), seq(MV), seq(128)],
        out_specs=(seq(256), blk(256), blk(MV)),
        scratch_shapes=[pltpu.VMEM((S, 256 * hp), F32), pltpu.VMEM((S, 128 * hp), F32)])


def _ln_fwd(pre, g, b):
    mu = jnp.mean(pre, axis=-1, keepdims=True)
    xc = pre - mu
    rstd = lax.rsqrt(jnp.mean(xc * xc, axis=-1, keepdims=True) + LN_EPS)
    xhat = xc * rstd
    return xhat * g + b, xhat, rstd


def _ln_bwd(dy, xhat, rstd, g):
    dxh = dy * g
    dx = rstd * (dxh - jnp.mean(dxh, axis=-1, keepdims=True) - xhat * jnp.mean(dxh * xhat, axis=-1, keepdims=True))
    return dx, jnp.sum(dy * xhat, axis=0, keepdims=True), jnp.sum(dy, axis=0, keepdims=True)


def _post_attn_fwd(zg, attn, pt, x, wgo, wmo, wout, g1, b1, *, tm):
    T = x.shape[0]

    def body(zg_ref, at_ref, pt_ref, x_ref, wgo_ref, wmo_ref, wout_ref, g_ref, b_ref,
             yg_ref, ym_ref, mix_ref, pre_ref, h_ref, hb_ref):
        yg = _dot(zg_ref[...], wgo_ref[...])
        ym = _dot(at_ref[...], wmo_ref[...])
        mix = (_sigmoid(pt_ref[:, 0:D]) * yg + _sigmoid(pt_ref[:, D:2 * D]) * ym).astype(BF)
        pre = ALPHA * x_ref[...] + _dot(mix, wout_ref[...])
        h, _, _ = _ln_fwd(pre, g_ref[...], b_ref[...])
        yg_ref[...] = yg
        ym_ref[...] = ym
        mix_ref[...] = mix
        pre_ref[...] = pre
        h_ref[...] = h
        hb_ref[...] = h.astype(BF)

    full = lambda shp: pl.BlockSpec(shp, lambda i: (0,) * len(shp))
    row = lambda w: pl.BlockSpec((tm, w), lambda i: (i, 0))
    sd = lambda dt: jax.ShapeDtypeStruct((T, D), dt)
    return pl.pallas_call(
        body, name="post_attn_fwd",
        out_shape=(sd(F32), sd(F32), sd(BF), sd(F32), sd(F32), sd(BF)),
        grid=(T // tm,),
        in_specs=[row(D), row(D), row(PT_W), row(D), full((D, D)), full((D, D)), full((D, D)),
                  full((1, D)), full((1, D))],
        out_specs=(row(D),) * 6,
        compiler_params=_params(("parallel",)),
    )(zg, attn, pt, x, wgo, wmo, wout, g1, b1)


def _post_attn_bwd(dh, pre, pt, yg, ym, wgo, wmo, wout, g1, *, tm):
    T = dh.shape[0]

    def body(dh_ref, pre_ref, pt_ref, yg_ref, ym_ref, wgo_ref, wmo_ref, wout_ref, g_ref,
             dx_ref, dpreb_ref, dpt_ref, dygb_ref, dymb_ref, dzg_ref, dat_ref, dg_ref, db_ref):
        @pl.when(pl.program_id(0) == 0)
        def _():
            dg_ref[...] = jnp.zeros_like(dg_ref)
            db_ref[...] = jnp.zeros_like(db_ref)

        pre = pre_ref[...]
        mu = jnp.mean(pre, axis=-1, keepdims=True)
        xc = pre - mu
        rstd = lax.rsqrt(jnp.mean(xc * xc, axis=-1, keepdims=True) + LN_EPS)
        dpre, dg, db = _ln_bwd(dh_ref[...], xc * rstd, rstd, g_ref[...])
        dg_ref[...] += dg
        db_ref[...] += db
        dx_ref[...] = ALPHA * dpre
        dpreb = dpre.astype(BF)
        dpreb_ref[...] = dpreb
        dmix = _dot_nt(dpreb, wout_ref[...])
        sa = _sigmoid(pt_ref[:, 0:D])
        sb = _sigmoid(pt_ref[:, D:2 * D])
        dpt_ref[:, 0:D] = (dmix * yg_ref[...] * (sa * (1.0 - sa))).astype(BF)
        dpt_ref[:, D:2 * D] = (dmix * ym_ref[...] * (sb * (1.0 - sb))).astype(BF)
        dyg = (dmix * sa).astype(BF)
        dym = (dmix * sb).astype(BF)
        dygb_ref[...] = dyg
        dymb_ref[...] = dym
        dzg_ref[...] = _dot_nt(dyg, wgo_ref[...]).astype(BF)
        dat_ref[...] = _dot_nt(dym, wmo_ref[...]).astype(BF)

    full = lambda shp: pl.BlockSpec(shp, lambda i: (0,) * len(shp))
    row = lambda w: pl.BlockSpec((tm, w), lambda i: (i, 0))
    sd = lambda w, dt: jax.ShapeDtypeStruct((T, w), dt)
    return pl.pallas_call(
        body, name="post_attn_bwd",
        out_shape=(sd(D, F32), sd(D, BF), sd(PT_W, BF), sd(D, BF), sd(D, BF), sd(D, BF), sd(D, BF),
                   jax.ShapeDtypeStruct((1, D), F32), jax.ShapeDtypeStruct((1, D), F32)),
        grid=(T // tm,),
        in_specs=[row(D), row(D), row(PT_W), row(D), row(D), full((D, D)), full((D, D)), full((D, D)),
                  full((1, D))],
        out_specs=(row(D), row(D), row(PT_W), row(D), row(D), row(D), row(D), full((1, D)), full((1, D))),
        compiler_params=_params(("arbitrary",)),
    )(dh, pre, pt, yg, ym, wgo, wmo, wout, g1)


def _shift_down(u, prev, k):
    r = pltpu.roll(u, k, 0)
    p = pltpu.roll(prev, k, 0)
    head = jnp.where(_iota(p.shape, 0) < k, p, r[0:8, :])
    return jnp.concatenate([head, r[8:, :]], axis=0)


def _shift_up(u, nxt, k):
    n = u.shape[0]
    r = pltpu.roll(u, n - k, 0)
    p = pltpu.roll(nxt, 8 - k, 0)
    tail = jnp.where(_iota(p.shape, 0) >= 8 - k, p, r[n - 8:, :])
    return jnp.concatenate([r[:n - 8, :], tail], axis=0)


def _conv3(u, prev, w_ref, b_ref):
    return (w_ref[0:1, :] * _shift_down(u, prev, 2) + w_ref[1:2, :] * _shift_down(u, prev, 1)
            + w_ref[2:3, :] * u + b_ref[...])


def _ffn_up_fwd(hb, wugt, wuvt, cw, cb, *, S, tm, tn):
    T = hb.shape[0]
    nj, nbs = DFF // tn, S // tm

    def body(h_ref, wg_ref, wv_ref, cwg_ref, cwv_ref, cbg_ref, cbv_ref,
             ug_ref, uv_ref, ucg_ref, ucv_ref, f_ref, pg_scr, pv_scr):
        @pl.when(pl.program_id(1) % nbs == 0)
        def _():
            pg_scr[...] = jnp.zeros_like(pg_scr)
            pv_scr[...] = jnp.zeros_like(pv_scr)

        h = h_ref[...]
        ug = _dot_nt(h, wg_ref[...])
        uv = _dot_nt(h, wv_ref[...])
        ucg = _conv3(ug, pg_scr[...], cwg_ref, cbg_ref)
        ucv = _conv3(uv, pv_scr[...], cwv_ref, cbv_ref)
        pg_scr[...] = ug[tm - 8:, :]
        pv_scr[...] = uv[tm - 8:, :]
        ug_ref[...] = ug.astype(BF)
        uv_ref[...] = uv.astype(BF)
        ucg_ref[...] = ucg
        ucv_ref[...] = ucv
        f_ref[...] = (ucg * _sigmoid(ucg) * ucv).astype(BF)

    tile = pl.BlockSpec((tm, tn), lambda j, i: (i, j))
    return pl.pallas_call(
        body, name="ffn_up_fwd",
        out_shape=(jax.ShapeDtypeStruct((T, DFF), BF), jax.ShapeDtypeStruct((T, DFF), BF),
                   jax.ShapeDtypeStruct((T, DFF), F32), jax.ShapeDtypeStruct((T, DFF), F32),
                   jax.ShapeDtypeStruct((T, DFF), BF)),
        grid=(nj, T // tm),
        in_specs=[pl.BlockSpec((tm, D), lambda j, i: (i, 0)),
                  pl.BlockSpec((tn, D), lambda j, i: (j, 0)), pl.BlockSpec((tn, D), lambda j, i: (j, 0)),
                  pl.BlockSpec((3, tn), lambda j, i: (0, j)), pl.BlockSpec((3, tn), lambda j, i: (0, j + nj)),
                  pl.BlockSpec((1, tn), lambda j, i: (0, j)), pl.BlockSpec((1, tn), lambda j, i: (0, j + nj))],
        out_specs=(tile, tile, tile, tile, tile),
        scratch_shapes=[pltpu.VMEM((8, tn), F32), pltpu.VMEM((8, tn), F32)],
        compiler_params=_params(("parallel", "arbitrary")),
    )(hb, wugt, wuvt, cw, cw, cb, cb)


def _ffn_bwd(dpreb, wd, ug, uv, ucg, ucv, cw, *, S, tm, tn):
    T = dpreb.shape[0]
    nj, nb, nbs = DFF // tn, T // tm, S // tm

    def body(dp_ref, wd_ref, ug_ref, uv_ref, ucg_ref, ucv_ref, cwg_ref, cwv_ref,
             dug_ref, duv_ref, dcg_ref, dcv_ref, ng_scr, nv_scr):
        ii = pl.program_id(1)
        i = nb - 1 - ii

        @pl.when(ii == 0)
        def _():
            dcg_ref[...] = jnp.zeros_like(dcg_ref)
            dcv_ref[...] = jnp.zeros_like(dcv_ref)

        @pl.when(i % nbs == nbs - 1)
        def _():
            ng_scr[...] = jnp.zeros_like(ng_scr)
            nv_scr[...] = jnp.zeros_like(nv_scr)

        df = _dot_nt(dp_ref[...], wd_ref[...])
        ucg = ucg_ref[...]
        sg = _sigmoid(ucg)
        ducg = df * ucv_ref[...] * (sg * (1.0 + ucg * (1.0 - sg)))
        ducv = df * (ucg * sg)

        def finish(duc, u_ref, w, nxt_scr, du_ref, dc_ref):
            nxt = nxt_scr[...]
            up1 = _shift_up(duc, nxt, 1)
            up2 = _shift_up(duc, nxt, 2)
            du_ref[...] = (w[2:3, :] * duc + w[1:2, :] * up1 + w[0:1, :] * up2).astype(BF)
            nxt_scr[...] = duc[0:8, :]
            u = u_ref[...].astype(F32)
            for row, z in enumerate((u * up2, u * up1, u * duc, duc)):
                dc_ref[row:row + 1, :] += jnp.sum(z, axis=0, keepdims=True)

        finish(ducg, ug_ref, cwg_ref, ng_scr, dug_ref, dcg_ref)
        finish(ducv, uv_ref, cwv_ref, nv_scr, duv_ref, dcv_ref)

    tile = pl.BlockSpec((tm, tn), lambda j, ii: (nb - 1 - ii, j))
    acc = pl.BlockSpec((8, tn), lambda j, ii: (0, j))
    return pl.pallas_call(
        body, name="ffn_bwd",
        out_shape=(jax.ShapeDtypeStruct((T, DFF), BF), jax.ShapeDtypeStruct((T, DFF), BF),
                   jax.ShapeDtypeStruct((8, DFF), F32), jax.ShapeDtypeStruct((8, DFF), F32)),
        grid=(nj, nb),
        in_specs=[pl.BlockSpec((tm, D), lambda j, ii: (nb - 1 - ii, 0)),
                  pl.BlockSpec((tn, D), lambda j, ii: (j, 0)),
                  tile, tile, tile, tile,
                  pl.BlockSpec((3, tn), lambda j, ii: (0, j)), pl.BlockSpec((3, tn), lambda j, ii: (0, j + nj))],
        out_specs=(tile, tile, acc, acc),
        scratch_shapes=[pltpu.VMEM((8, tn), F32), pltpu.VMEM((8, tn), F32)],
        compiler_params=_params(("parallel", "arbitrary")),
    )(dpreb, wd, ug, uv, ucg, ucv, cw, cw)


def _down_ln2_loss(f_in, wd, h, target, g2, b2, *, tm):
    T = h.shape[0]

    def body(f_ref, wd_ref, h_ref, t_ref, g_ref, b_ref, dpb_ref, dh_ref, loss_ref, dg_ref, db_ref):
        @pl.when(pl.program_id(0) == 0)
        def _():
            loss_ref[...] = jnp.zeros_like(loss_ref)
            dg_ref[...] = jnp.zeros_like(dg_ref)
            db_ref[...] = jnp.zeros_like(db_ref)

        pre = ALPHA * h_ref[...] + _dot(f_ref[...], wd_ref[...])
        out, xhat, rstd = _ln_fwd(pre, g_ref[...], b_ref[...])
        diff = out - t_ref[...]
        loss_ref[...] += 0.5 * jnp.sum(jnp.mean(diff * diff, axis=-1, keepdims=True))
        dpre, dg, db = _ln_bwd(diff * (1.0 / D), xhat, rstd, g_ref[...])
        dg_ref[...] += dg
        db_ref[...] += db
        dpb_ref[...] = dpre.astype(BF)
        dh_ref[...] = ALPHA * dpre

    full = lambda shp: pl.BlockSpec(shp, lambda i: (0,) * len(shp))
    row = lambda w: pl.BlockSpec((tm, w), lambda i: (i, 0))
    return pl.pallas_call(
        body, name="down_ln2_loss",
        out_shape=(jax.ShapeDtypeStruct((T, D), BF), jax.ShapeDtypeStruct((T, D), F32),
                   jax.ShapeDtypeStruct((8, 128), F32), jax.ShapeDtypeStruct((1, D), F32),
                   jax.ShapeDtypeStruct((1, D), F32)),
        grid=(T // tm,),
        in_specs=[row(DFF), full((DFF, D)), row(D), row(D), full((1, D)), full((1, D))],
        out_specs=(row(D), row(D), full((8, 128)), full((1, D)), full((1, D))),
        compiler_params=_params(("arbitrary",)),
    )(f_in, wd, h, target, g2, b2)


def _adamw(parts, w, m, v, *, name):
    n, R, C = parts.shape
    tr, tc = R, C
    for cand in range(min(R, 256), 15, -1):
        if R % cand == 0 and cand % 16 == 0:
            tr = cand
            break
    if tr == R and R * C > 65536 and C % 256 == 0:
        tc = 256
    c1 = 1.0 - ADAM_B1 ** ADAM_STEP
    c2 = 1.0 - ADAM_B2 ** ADAM_STEP

    def body(p_ref, w_ref, m_ref, v_ref, g_ref, d_ref, nm_ref, nv_ref):
        g = p_ref[0].astype(F32)
        for s in range(1, n):
            g = g + p_ref[s].astype(F32)
        nm = ADAM_B1 * m_ref[...] + (1.0 - ADAM_B1) * g
        nv = ADAM_B2 * v_ref[...] + (1.0 - ADAM_B2) * (g * g)
        g_ref[...] = g
        nm_ref[...] = nm
        nv_ref[...] = nv
        d_ref[...] = -ADAM_LR * ((nm / c1) / (jnp.sqrt(nv / c2) + ADAM_EPS) + ADAM_WD * w_ref[...])

    blk = pl.BlockSpec((tr, tc), lambda i, j: (i, j))
    sd = jax.ShapeDtypeStruct((R, C), F32)
    return pl.pallas_call(
        body, name=name,
        out_shape=(sd, sd, sd, sd),
        grid=(R // tr, C // tc),
        in_specs=[pl.BlockSpec((n, tr, tc), lambda i, j: (0, i, j)), blk, blk, blk],
        out_specs=(blk, blk, blk, blk),
        compiler_params=_params(("parallel", "parallel")),
    )(parts, w, m, v)


class _Exchange:
    def __init__(self, items):
        self.items = [(src if sc else [(src, 0)], sc) for src, sc in items]
        self.arrays = [arr for srcs, _ in self.items for arr, _ in srcs]
        self.n = len(self.items)
        self.n_in = len(self.arrays)

    def out_shape(self):
        return tuple(jax.ShapeDtypeStruct((NDEV,) + (srcs[0][0].shape[1:] if sc else srcs[0][0].shape),
                                          srcs[0][0].dtype) for srcs, sc in self.items)

    def scratch(self):
        return [pltpu.SemaphoreType.DMA((self.n, NDEV - 1)), pltpu.SemaphoreType.DMA((self.n, NDEV - 1)),
                pltpu.SemaphoreType.DMA((self.n,))]

    def _emit(self, ins, outs, sems, phase):
        send_sems, recv_sems, loc_sems = sems
        x, y, c = lax.axis_index("x"), lax.axis_index("y"), lax.axis_index("c")
        me = 4 * x + 2 * y + c
        flip = lambda p, d: 1 - p if d else p

        def inside(p, lo, n):
            return None if (lo, n) == (0, NDEV) else jnp.logical_and(p >= lo, p < lo + n)

        def when(cond, fn):
            if cond is None:
                fn()
            else:
                pl.when(cond)(fn)

        pos = 0
        for a, (srcs, sc) in enumerate(self.items):
            refs = ins[pos:pos + len(srcs)]
            pos += len(srcs)
            ranges = [(lo, arr.shape[0]) if sc else (0, NDEV) for arr, lo in srcs]
            mine = [inside(me, lo, n) for lo, n in ranges]
            i_receive = None if None in mine else functools.reduce(jnp.logical_or, mine)
            for ref, (lo, n), cond in zip(refs, ranges, mine):
                def local(ref=ref, lo=lo):
                    cp = pltpu.make_async_copy(ref.at[me - lo] if sc else ref, outs[a].at[me], loc_sems.at[a])
                    cp.start() if phase == 0 else cp.wait()
                if phase != 1:
                    when(cond, local)
            for k in range(1, NDEV):
                px, py, pc = flip(x, k & 4), flip(y, k & 2), flip(c, k & 1)
                peer = 4 * px + 2 * py + pc
                mk = functools.partial(pltpu.make_async_remote_copy,
                                       send_sem=send_sems.at[a, k - 1], recv_sem=recv_sems.at[a, k - 1],
                                       device_id=(px, py, pc), device_id_type=MESH_ID)
                if phase == 1:
                    def arrival(mk=mk, peer=peer):
                        mk(src_ref=refs[0].at[0] if sc else refs[0], dst_ref=outs[a].at[peer]).wait_recv()
                    when(i_receive, arrival)
                    continue
                for ref, (lo, n) in zip(refs, ranges):
                    def send(mk=mk, ref=ref, lo=lo, peer=peer):
                        cp = mk(src_ref=ref.at[peer - lo] if sc else ref, dst_ref=outs[a].at[me])
                        cp.start() if phase == 0 else cp.wait_send()
                    when(inside(peer, lo, n), send)

    def start(self, ins, outs, sems):
        self._emit(ins, outs, sems, 0)

    def wait(self, ins, outs, sems):
        self._emit(ins, outs, sems, 1)
        self._emit(ins, outs, sems, 2)


def _call(body, *, name, grid, in_specs, out_specs, out_shape, args, scratch_shapes=(), sem=None, ride=None):
    if ride is None:
        return pl.pallas_call(body, name=name, grid=grid, in_specs=list(in_specs), out_specs=tuple(out_specs),
                              out_shape=tuple(out_shape), scratch_shapes=list(scratch_shapes),
                              compiler_params=_params(sem))(*args)
    n_in, n_out, n_scr, ne, ne_in = len(args), len(out_shape), len(scratch_shapes), ride.n, ride.n_in

    def ride_body(*refs):
        ins, ex_in = refs[:n_in], refs[n_in:n_in + ne_in]
        o0 = n_in + ne_in
        outs, ex_out = refs[o0:o0 + n_out], refs[o0 + n_out:o0 + n_out + ne]
        scr = refs[o0 + n_out + ne:o0 + n_out + ne + n_scr]
        sems = refs[o0 + n_out + ne + n_scr:]
        first = functools.reduce(jnp.logical_and, [pl.program_id(d) == 0 for d in range(len(grid))])
        last = functools.reduce(jnp.logical_and, [pl.program_id(d) == grid[d] - 1 for d in range(len(grid))])

        @pl.when(first)
        def _():
            ride.start(ex_in, ex_out, sems)

        body(*ins, *outs, *scr)

        @pl.when(last)
        def _():
            ride.wait(ex_in, ex_out, sems)

    anyspec = pl.BlockSpec(memory_space=pl.ANY)
    res = pl.pallas_call(
        ride_body, name=name, grid=grid,
        in_specs=list(in_specs) + [anyspec] * ne_in,
        out_specs=tuple(out_specs) + (anyspec,) * ne,
        out_shape=tuple(out_shape) + ride.out_shape(),
        scratch_shapes=list(scratch_shapes) + ride.scratch(),
        compiler_params=_params(("arbitrary",) * len(grid)),
    )(*args, *ride.arrays)
    return tuple(res[:n_out]), tuple(res[n_out:])


def _gather_two_level(arrays, *, name):
    n = len(arrays)

    def body(*refs):
        ins, outs = refs[:n], refs[n:2 * n]
        send_sems, recv_sems, loc_sems = refs[2 * n:]
        x, y, c = lax.axis_index("x"), lax.axis_index("y"), lax.axis_index("c")
        sibling = (x, y, 1 - c)
        chips = [(1 - x, y), (x, 1 - y), (1 - x, 1 - y)]
        idx = lambda px, py, pc: 4 * px + 2 * py + pc
        me = idx(x, y, c)

        def copy(a, k, block, to, src=None):
            return pltpu.make_async_remote_copy(
                src_ref=outs[a].at[block] if src is None else src, dst_ref=outs[a].at[block],
                send_sem=send_sems.at[a, k], recv_sem=recv_sems.at[a, k], device_id=to, device_id_type=MESH_ID)

        local = [pltpu.make_async_copy(ins[a], outs[a].at[me], loc_sems.at[a]) for a in range(n)]
        sent = []
        for a in range(n):
            sent.append(copy(a, 0, me, sibling, src=ins[a]))
            sent += [copy(a, 1 + j, me, (*chip, c), src=ins[a]) for j, chip in enumerate(chips)]
        for cp in local + sent:
            cp.start()
        for j, chip in enumerate(chips):
            for a in range(n):
                copy(a, 1 + j, idx(*chip, c), sibling).wait_recv()
                passed = copy(a, 4 + j, idx(*chip, c), sibling)
                passed.start()
                sent.append(passed)
        for a in range(n):
            copy(a, 0, idx(x, y, 1 - c), sibling).wait_recv()
            for j, chip in enumerate(chips):
                copy(a, 4 + j, idx(*chip, 1 - c), sibling).wait_recv()
        for cp in sent:
            cp.wait_send()
        for cp in local:
            cp.wait()

    anyspec = pl.BlockSpec(memory_space=pl.ANY)
    return pl.pallas_call(
        body, name=name,
        out_shape=tuple(jax.ShapeDtypeStruct((NDEV,) + a.shape, a.dtype) for a in arrays),
        in_specs=[anyspec] * n, out_specs=(anyspec,) * n,
        scratch_shapes=[pltpu.SemaphoreType.DMA((n, NDEV - 1)), pltpu.SemaphoreType.DMA((n, NDEV - 1)),
                        pltpu.SemaphoreType.DMA((n,))],
    )(*arrays)


def _tri_consts():
    r = lax.broadcasted_iota(jnp.int32, (GC, GC), 0)
    c = lax.broadcasted_iota(jnp.int32, (GC, GC), 1)
    return (r >= c).astype(BF), (r <= c).astype(BF)


def _local_step(x, positions, target, w, hooks=None):
    g = {}

    def run(host, fn, *a, **kw):
        h = None if hooks is None else hooks.get(host)
        if h is None:
            return fn(*a, **kw)
        out, received = fn(*a, ride=_Exchange(h[0](w, g)), **kw)
        h[1](received, w, g)
        return out

    nseq, S, _ = x.shape
    T = nseq * S
    tm = min(256, S)
    tq = min(512, S)
    x2 = x.reshape(T, D)
    pos = positions.reshape(T, 1)
    half = ROPE // 2
    inv = THETA ** (-jnp.arange(half, dtype=F32) / half)
    invf = jnp.concatenate([inv, inv, jnp.zeros((64,), F32)]).reshape(1, 128)
    ltri, utri = _tri_consts()

    pg = run("proj_g", _matmul, x2, w["w_gt"], "nt", name="proj_g", tm=1024, tn=640, tk=1024)
    pm = _matmul(x2, w["w_mt"], "nt", name="proj_m", tm=1024, tn=768, tk=1024)
    pt = _matmul(x2, w["w_tt"], "nt", name="proj_t", tm=1024, tn=1024, tk=1024)
    o, zg, states = _gla_fwd(pg, w["wg"], w["bg"], w["gn"], ltri, nseq=nseq, S=S, tm=tm)
    qc, kc, v = _mla_prep_fwd(pm, pos, invf, w["gq"], w["gkv"], w["wuq"], w["wukv"], tm=tm)
    attn, lse = run("flash_fwd", _flash_fwd, qc, kc, v, nseq=nseq, S=S, tq=tq)
    yg, ym, mix, pre1, h1, h1b = _post_attn_fwd(zg, attn, pt, x2, w["wgo"], w["wmo"], w["wout"],
                                                w["g1"], w["b1"], tm=tm)
    ug, uv, ucg, ucv, f_in = _ffn_up_fwd(h1b, w["wugt"], w["wuvt"], w["cw"], w["cb"], S=S, tm=tm, tn=1408)
    dpre2b, dh1, loss8, dg2, db2 = _down_ln2_loss(f_in, w["wd"], h1, target.reshape(T, D), w["g2"], w["b2"], tm=tm)

    dug, duv, dcg, dcv = _ffn_bwd(dpre2b, w["wd"], ug, uv, ucg, ucv, w["cw"], S=S, tm=tm, tn=1408)
    g["g2"], g["b2"], g["loss"] = dg2, db2, loss8[0:1, 0:1]
    g["cw"] = jnp.concatenate([dcg[0:3], dcv[0:3]], axis=1)
    g["cb"] = jnp.concatenate([dcg[3:4], dcv[3:4]], axis=1)
    g["wd"] = _matmul(f_in, dpre2b, "tn", name="dw_down", out_dtype=BF, tm=1408, tn=1024, tk=1024)
    g["wugt"] = _matmul(dug, h1b, "tn", name="dw_up_g", out_dtype=BF, tm=1408, tn=1024, tk=1024)
    g["wuvt"] = _matmul(duv, h1b, "tn", name="dw_up_v", out_dtype=BF, tm=1408, tn=1024, tk=1024)
    dh1 = _matmul(dug, w["wugt"], "nn", name="dh1_g", c_in=dh1, tm=1024, tn=1024, tk=1408)
    dh1 = _matmul(duv, w["wuvt"], "nn", name="dh1_v", c_in=dh1, tm=1024, tn=1024, tk=1408)
    dx, dpre1b, dpt, dygb, dymb, dzg, dattn, dg1, db1 = _post_attn_bwd(
        dh1, pre1, pt, yg, ym, w["wgo"], w["wmo"], w["wout"], w["g1"], tm=tm)
    g["g1"], g["b1"] = dg1, db1
    g["wout"] = _matmul(mix, dpre1b, "tn", name="dw_out", out_dtype=BF, tm=1024, tn=1024, tk=1024)
    g["wgo"] = _matmul(zg, dygb, "tn", name="dw_gla_o", out_dtype=BF, tm=1024, tn=1024, tk=1024)
    g["wmo"] = _matmul(attn, dymb, "tn", name="dw_mla_o", out_dtype=BF, tm=1024, tn=1024, tk=1024)
    dqc, dkc, dv = run("flash_bwd", _flash_bwd, qc, kc, v, attn, dattn, lse, nseq=nseq, S=S, tq=tq)
    dpm, g["wuq"], g["wukv"], g["gq"], g["gkv"] = _mla_prep_bwd(
        pm, pos, invf, w["gq"], w["gkv"], w["wuq"], w["wukv"], dqc, dkc, dv, tm=tm)
    g["w_mt"] = _matmul(dpm, x2, "tn", name="dw_in_m", out_dtype=BF, tm=768, tn=1024, tk=1024)
    g["w_tt"] = _matmul(dpt, x2, "tn", name="dw_in_t", out_dtype=BF, tm=1024, tn=1024, tk=1024)
    dpg, g["wg"], g["bg"], g["gn"] = run("gla_bwd", _gla_bwd, pg, w["wg"], w["bg"], w["gn"], ltri, utri, o, states,
                                         dzg, nseq=nseq, S=S, tm=tm)
    g["w_gt"] = _matmul(dpg, x2, "tn", name="dw_in_g", out_dtype=BF, tm=640, tn=1024, tk=1024)
    dx = run("dx", _matmul_sum, dx, [(dpg, w["w_gt"], 640), (dpm, w["w_mt"], 768)], name="dx_gm")
    dx = _matmul_sum(dx, [(dpt, w["w_tt"], 1024)], name="dx_t")
    return loss8[0, 0], dx.reshape(nseq, S, D), g


_IN_SPLITS = (512, 512, 1024, 16, 1024, 384, 256, 64, 1024, 1024)


def _w_in_to_groups(wt):
    offs = [0]
    for s in _IN_SPLITS:
        offs.append(offs[-1] + s)
    q, k, v, r, og, cq, ckv, kr, ga, gb = [wt[offs[i]:offs[i + 1]] for i in range(10)]
    z = lambda n: jnp.zeros((n, wt.shape[1]), wt.dtype)
    return (jnp.concatenate([q, k, v, og, r, z(112)], axis=0),
            jnp.concatenate([cq, kr, z(64), ckv], axis=0),
            jnp.concatenate([ga, gb], axis=0))


def _groups_to_w_in(g_g, g_m, g_t):
    q, k, v, og, r = g_g[0:512], g_g[512:1024], g_g[1024:2048], g_g[2048:3072], g_g[3072:3088]
    cq, kr, ckv = g_m[0:384], g_m[384:448], g_m[512:768]
    return jnp.concatenate([q, k, v, r, og, cq, ckv, kr, g_t], axis=0)


_W_IN_LO = 5
_W_IN_SPLIT = _W_IN_LO * 730 - 3472


def _w_in_rows_lo(g_g, g_m):
    q, k, v, og, r = g_g[0:512], g_g[512:1024], g_g[1024:2048], g_g[2048:3072], g_g[3072:3088]
    return jnp.concatenate([q, k, v, r, og, g_m[0:384], g_m[512:768]], axis=0)[:3472 + _W_IN_SPLIT]


def _w_in_rows_hi(g_m, g_t):
    return jnp.concatenate([g_m[512:768], g_m[384:448], g_t], axis=0)[_W_IN_SPLIT:]


def _uq_to_kernel(wuq):
    w3 = wuq.reshape(MQR, MH, NOPE + ROPE)
    rope = jnp.concatenate([w3[:, :, NOPE:], jnp.zeros((MQR, MH, 64), wuq.dtype)], axis=2)
    return jnp.concatenate([w3[:, :, :NOPE].reshape(MQR, MH * 128), rope.reshape(MQR, MH * 128)], axis=1)


def _uq_from_kernel(g):
    nope = g[:, :1024].reshape(MQR, MH, 128)
    rope = g[:, 1024:].reshape(MQR, MH, 128)[:, :, :ROPE]
    return jnp.concatenate([nope, rope], axis=2)


def _ukv_to_kernel(wukv):
    w3 = wukv.reshape(MKR, MH, NOPE + MV)
    return jnp.concatenate([w3[:, :, :NOPE].reshape(MKR, MH * 128), w3[:, :, NOPE:].reshape(MKR, MH * 128)], axis=1)


def _ukv_from_kernel(g):
    return jnp.concatenate([g[:, :1024].reshape(MKR, MH, 128), g[:, 1024:].reshape(MKR, MH, 128)], axis=2)


def _cols_gathered(a):
    return a.transpose(1, 0, 2).reshape(a.shape[1], NDEV * a.shape[2])


def _cols_scattered(a):
    R = a.shape[0]
    return a.reshape(R, NDEV, a.shape[1] // NDEV).transpose(1, 0, 2)


_SMALL = (("gla_b_gate", 512), ("gla_norm_g", 256), ("mla_q_norm_g", 384), ("mla_kv_norm_g", 256),
          ("ln1_g", 1024), ("ln1_b", 1024), ("conv_b", 5632), ("ln2_g", 1024), ("ln2_b", 1024))
_SMALL_ROWS = 88
_SMALL_USED = sum(sz for _, sz in _SMALL)


def _pack_small(d):
    flat = jnp.concatenate([d[n].reshape(-1) for n, _ in _SMALL] + ([d['loss'].reshape(-1)] if 'loss' in d else []))
    return jnp.pad(flat, (0, _SMALL_ROWS * 128 - flat.shape[0])).reshape(_SMALL_ROWS, 128)


def _unpack_small(a):
    flat = a.reshape(-1)
    out, off = {}, 0
    for n, sz in _SMALL:
        out[n] = flat[off:off + sz].reshape(1, sz)
        off += sz
    return out


_NAMES = ['w_in', 'gla_w_gate_up', 'gla_b_gate', 'gla_norm_g', 'w_gla_o', 'mla_q_norm_g', 'mla_w_uq',
          'mla_kv_norm_g', 'mla_w_ukv', 'w_mla_o', 'w_out', 'ln1_g', 'ln1_b', 'w_up', 'conv_w', 'conv_b',
          'w_down', 'ln2_g', 'ln2_b']
_SHARDED = ['w_in', 'w_up', 'w_down', 'w_gla_o', 'w_mla_o', 'w_out', 'mla_w_uq', 'mla_w_ukv', 'gla_w_gate_up',
            'conv_w']


def kernel(x, positions, w_in, gla_w_gate_up, gla_b_gate, gla_norm_g, w_gla_o, mla_q_norm_g, mla_w_uq, mla_kv_norm_g, mla_w_ukv, w_mla_o, w_out, ln1_g, ln1_b, w_up, conv_w, conv_b, w_down, ln2_g, ln2_b, loss_target, m_w_in, m_gla_w_gate_up, m_gla_b_gate, m_gla_norm_g, m_w_gla_o, m_mla_q_norm_g, m_mla_w_uq, m_mla_kv_norm_g, m_mla_w_ukv, m_w_mla_o, m_w_out, m_ln1_g, m_ln1_b, m_w_up, m_conv_w, m_conv_b, m_w_down, m_ln2_g, m_ln2_b, v_w_in, v_gla_w_gate_up, v_gla_b_gate, v_gla_norm_g, v_w_gla_o, v_mla_q_norm_g, v_mla_w_uq, v_mla_kv_norm_g, v_mla_w_ukv, v_w_mla_o, v_w_out, v_ln1_g, v_ln1_b, v_w_up, v_conv_w, v_conv_b, v_w_down, v_ln2_g, v_ln2_b):
    W = dict(w_in=w_in, gla_w_gate_up=gla_w_gate_up, gla_b_gate=gla_b_gate, gla_norm_g=gla_norm_g, w_gla_o=w_gla_o, mla_q_norm_g=mla_q_norm_g, mla_w_uq=mla_w_uq, mla_kv_norm_g=mla_kv_norm_g, mla_w_ukv=mla_w_ukv, w_mla_o=w_mla_o, w_out=w_out, ln1_g=ln1_g, ln1_b=ln1_b, w_up=w_up, conv_w=conv_w, conv_b=conv_b, w_down=w_down, ln2_g=ln2_g, ln2_b=ln2_b)
    M = dict(w_in=m_w_in, gla_w_gate_up=m_gla_w_gate_up, gla_b_gate=m_gla_b_gate, gla_norm_g=m_gla_norm_g, w_gla_o=m_w_gla_o, mla_q_norm_g=m_mla_q_norm_g, mla_w_uq=m_mla_w_uq, mla_kv_norm_g=m_mla_kv_norm_g, mla_w_ukv=m_mla_w_ukv, w_mla_o=m_w_mla_o, w_out=m_w_out, ln1_g=m_ln1_g, ln1_b=m_ln1_b, w_up=m_w_up, conv_w=m_conv_w, conv_b=m_conv_b, w_down=m_w_down, ln2_g=m_ln2_g, ln2_b=m_ln2_b)
    V = dict(w_in=v_w_in, gla_w_gate_up=v_gla_w_gate_up, gla_b_gate=v_gla_b_gate, gla_norm_g=v_gla_norm_g, w_gla_o=v_w_gla_o, mla_q_norm_g=v_mla_q_norm_g, mla_w_uq=v_mla_w_uq, mla_kv_norm_g=v_mla_kv_norm_g, mla_w_ukv=v_mla_w_ukv, w_mla_o=v_w_mla_o, w_out=v_w_out, ln1_g=v_ln1_g, ln1_b=v_ln1_b, w_up=v_w_up, conv_w=v_conv_w, conv_b=v_conv_b, w_down=v_w_down, ln2_g=v_ln2_g, ln2_b=v_ln2_b)

    tshard = lambda d, n: d[n][0].T
    shard = lambda n: (W[n][0].astype(BF), False)
    first = ['w_in', 'mla_w_uq', 'mla_w_ukv', 'gla_w_gate_up']
    G = dict(zip(first, _gather_two_level(
        [tshard(W, 'w_in').astype(BF)] + [shard(n)[0] for n in first[1:]], name="gather_w0")))
    w_gt, w_mt, w_tt = _w_in_to_groups(G['w_in'].reshape(NDEV * 730, D))
    kw = dict(
        w_gt=w_gt, w_mt=w_mt, w_tt=w_tt,
        wg=jnp.pad(_cols_gathered(G['gla_w_gate_up']), ((0, 128 - GR), (0, 0))), bg=W['gla_b_gate'],
        gn=W['gla_norm_g'], gq=W['mla_q_norm_g'], gkv=W['mla_kv_norm_g'],
        wuq=_uq_to_kernel(_cols_gathered(G['mla_w_uq'])), wukv=_ukv_to_kernel(_cols_gathered(G['mla_w_ukv'])),
        g1=W['ln1_g'], b1=W['ln1_b'], g2=W['ln2_g'], b2=W['ln2_b'], cb=W['conv_b'],
    )
    received = {}

    def got_out_proj(ex, w, g):
        w.update(wgo=ex[0].reshape(D, D), wmo=ex[1].reshape(D, D), wout=ex[2].reshape(D, D))

    def got_ffn(ex, w, g):
        w_upt = ex[0].reshape(2 * DFF, D)
        w.update(wugt=w_upt[:DFF], wuvt=w_upt[DFF:], wd=ex[1].reshape(DFF, D), cw=_cols_gathered(ex[2]))

    slab = lambda a, lo=0: ([(a.astype(BF), lo)], True)
    rows = lambda a, n=NDEV: a.reshape(n, a.shape[0] // n, a.shape[1])

    def keep(names):
        return lambda ex, w, g: received.update(zip(names, ex))

    def small_grads(g):
        return _pack_small(dict(gla_b_gate=g['bg'], gla_norm_g=g['gn'], mla_q_norm_g=g['gq'], mla_kv_norm_g=g['gkv'],
                                ln1_g=g['g1'], ln1_b=g['b1'], conv_b=g['cb'], ln2_g=g['g2'], ln2_b=g['b2'],
                                loss=g['loss']))

    hooks = {
        "proj_g": (lambda w, g: [shard('w_gla_o'), shard('w_mla_o'), shard('w_out')], got_out_proj),
        "flash_fwd": (lambda w, g: [(tshard(W, 'w_up').astype(BF), False), shard('w_down'), (W['conv_w'][0], False)],
                      got_ffn),
        "flash_bwd": (lambda w, g: [slab(rows(g['wd'])),
                                    ([(rows(g['wugt'], 4), 0), (rows(g['wuvt'], 4), 4)], True)],
                      keep(['w_down', 'w_up'])),
        "gla_bwd": (lambda w, g: [slab(rows(g['wout'])), slab(rows(g['wgo'])), slab(rows(g['wmo'])),
                                  slab(_uq_from_kernel(g['wuq']).transpose(1, 0, 2)),
                                  slab(_ukv_from_kernel(g['wukv']).transpose(1, 0, 2)),
                                  slab(rows(_w_in_rows_hi(g['w_mt'], g['w_tt']), NDEV - _W_IN_LO), _W_IN_LO)],
                    keep(['w_out', 'w_gla_o', 'w_mla_o', 'mla_w_uq', 'mla_w_ukv', 'w_in_hi'])),
        "dx": (lambda w, g: [slab(rows(_w_in_rows_lo(g['w_gt'], g['w_mt']), _W_IN_LO)),
                             ([(_cols_scattered(g['wg'][:GR]), 0)], True), ([(_cols_scattered(g['cw']), 0)], True),
                             (small_grads(g), False)],
               keep(['w_in_lo', 'gla_w_gate_up', 'conv_w', 'small'])),
    }

    _, grad_x, _ = _local_step(x, positions, loss_target, kw, hooks)

    grads, deltas, new_m, new_v = {}, {}, {}, {}
    small_parts = received['small']
    loss = jnp.sum(small_parts.reshape(NDEV, -1)[:, _SMALL_USED])
    me = 4 * lax.axis_index("x") + 2 * lax.axis_index("y") + lax.axis_index("c")
    received['w_in'] = jnp.where(me >= _W_IN_LO, received['w_in_hi'], received['w_in_lo'])
    for n in _SHARDED:
        shp = W[n].shape
        if n in ('w_in', 'w_up'):
            out = _adamw(received[n], tshard(W, n), tshard(M, n), tshard(V, n), name="adamw_" + n)
            grads[n], deltas[n], new_m[n], new_v[n] = [t.T.reshape(shp) for t in out]
            continue
        out = _adamw(received[n], W[n][0], M[n][0], V[n][0], name="adamw_" + n)
        grads[n], deltas[n], new_m[n], new_v[n] = [t.reshape(shp) for t in out]
    out = _adamw(small_parts, _pack_small(W), _pack_small(M), _pack_small(V), name="adamw_small")
    for dst, packed in zip((grads, deltas, new_m, new_v), out):
        dst.update(_unpack_small(packed))

    return (loss, grad_x, *[grads[n] for n in _NAMES], *[deltas[n] for n in _NAMES],
            *[new_m[n] for n in _NAMES], *[new_v[n] for n in _NAMES])
```

```python
import functools

import jax
import jax.numpy as jnp
from jax import lax
from jax.experimental import pallas as pl
from jax.experimental.pallas import tpu as pltpu

F32 = jnp.float32
BF = jnp.bfloat16

D = 1024
GH, GDK, GDV, GR, GTAU, GC = 4, 128, 256, 16, 16.0, 64
MH, MQR, MKR, NOPE, ROPE, MV = 8, 384, 256, 128, 64, 128
THETA = 10000.0
DFF = 2816
ALPHA = 2.0 ** 0.25
LN_EPS = 1e-5
RMS_EPS = 1e-6
NDEV = 8
ADAM_LR, ADAM_B1, ADAM_B2, ADAM_EPS, ADAM_WD, ADAM_STEP = 0.001, 0.9, 0.999, 1e-08, 0.01, 10

PG_W = 3200
PM_W = 768
PT_W = 2048
NEG = -1e30
MESH_ID = pl.DeviceIdType.MESH
VMEM_MB = 1024 * 1024


def _params(sem, vmem_mb=48):
    return pltpu.CompilerParams(dimension_semantics=sem, vmem_limit_bytes=vmem_mb * VMEM_MB)


def _dot(a, b):
    return lax.dot_general(a, b, (((1,), (0,)), ((), ())), preferred_element_type=F32)


def _dot_nt(a, b):
    return lax.dot_general(a, b, (((1,), (1,)), ((), ())), preferred_element_type=F32)


def _dot_tn(a, b):
    return lax.dot_general(a, b, (((0,), (0,)), ((), ())), preferred_element_type=F32)


def _iota(shape, dim):
    return lax.broadcasted_iota(jnp.int32, shape, dim)


FLASH_HP = 2
QK_SCALE = (NOPE + ROPE) ** -0.5
LOG2E = 1.4426950408889634
QK_SCALE_LOG2 = QK_SCALE * LOG2E


def _sigmoid(x):
    return 0.5 * jnp.tanh(0.5 * x) + 0.5


def _tri_mm(tri_bf, x):
    hi = x.astype(BF)
    r1 = x - hi.astype(F32)
    mid = r1.astype(BF)
    lo = (r1 - mid.astype(F32)).astype(BF)
    return _dot(tri_bf, hi) + _dot(tri_bf, mid) + _dot(tri_bf, lo)


def _matmul(a, b, mode, *, name, c_in=None, out_dtype=F32, tm=512, tn=512, tk=512, ride=None):
    if mode == "nn":
        (M, K), (_, N) = a.shape, b.shape
    elif mode == "nt":
        (M, K), (N, _) = a.shape, b.shape
    else:
        (K, M), (_, N) = a.shape, b.shape
    tm, tn, tk = min(tm, M), min(tn, N), min(tk, K)
    assert M % tm == 0 and N % tn == 0 and K % tk == 0, (name, M, N, K, tm, tn, tk)
    nk = K // tk
    dot = {"nn": _dot, "nt": _dot_nt, "tn": _dot_tn}[mode]

    def body(*refs):
        if c_in is None:
            a_ref, b_ref, o_ref, acc_ref = refs
        else:
            a_ref, b_ref, c_ref, o_ref, acc_ref = refs
        k = pl.program_id(2)

        @pl.when(k == 0)
        def _():
            if c_in is None:
                acc_ref[...] = jnp.zeros_like(acc_ref)
            else:
                acc_ref[...] = c_ref[...].astype(F32)

        acc_ref[...] += dot(a_ref[...].astype(BF), b_ref[...].astype(BF))

        @pl.when(k == nk - 1)
        def _():
            o_ref[...] = acc_ref[...].astype(out_dtype)

    if mode == "tn":
        a_spec = pl.BlockSpec((tk, tm), lambda i, j, k: (k, i))
    else:
        a_spec = pl.BlockSpec((tm, tk), lambda i, j, k: (i, k))
    if mode == "nt":
        b_spec = pl.BlockSpec((tn, tk), lambda i, j, k: (j, k))
    else:
        b_spec = pl.BlockSpec((tk, tn), lambda i, j, k: (k, j))
    in_specs = [a_spec, b_spec]
    args = [a, b]
    if c_in is not None:
        in_specs.append(pl.BlockSpec((tm, tn), lambda i, j, k: (i, j)))
        args.append(c_in)
    res = _call(
        body, name=name,
        out_shape=(jax.ShapeDtypeStruct((M, N), out_dtype),),
        grid=(M // tm, N // tn, nk),
        in_specs=in_specs,
        out_specs=(pl.BlockSpec((tm, tn), lambda i, j, k: (i, j)),),
        scratch_shapes=[pltpu.VMEM((tm, tn), F32)],
        sem=("parallel", "parallel", "arbitrary"), args=args, ride=ride)
    return res[0] if ride is None else (res[0][0], res[1])


def _matmul_sum(c_in, parts, *, name, tm=1024, ride=None):
    M, N = c_in.shape
    tm = min(tm, M)
    n_p = len(parts)
    counts = [a.shape[1] // tk for a, _, tk in parts]
    starts = [sum(counts[:p]) for p in range(n_p)]
    nk = sum(counts)

    def body(*refs):
        a_refs, w_refs = refs[:n_p], refs[n_p:2 * n_p]
        c_ref, o_ref, acc_ref = refs[2 * n_p:]
        k = pl.program_id(1)

        @pl.when(k == 0)
        def _():
            acc_ref[...] = c_ref[...]

        for p in range(n_p):
            @pl.when(jnp.logical_and(k >= starts[p], k < starts[p] + counts[p]))
            def _(p=p):
                acc_ref[...] += _dot(a_refs[p][...].astype(BF), w_refs[p][...].astype(BF))

        @pl.when(k == nk - 1)
        def _():
            o_ref[...] = acc_ref[...]

    def kidx(p):
        return lambda k: jnp.clip(k - starts[p], 0, counts[p] - 1)

    in_specs = [pl.BlockSpec((tm, tk), lambda i, k, f=kidx(p): (i, f(k))) for p, (_, _, tk) in enumerate(parts)]
    in_specs += [pl.BlockSpec((tk, N), lambda i, k, f=kidx(p): (f(k), 0)) for p, (_, _, tk) in enumerate(parts)]
    in_specs.append(pl.BlockSpec((tm, N), lambda i, k: (i, 0)))
    res = _call(
        body, name=name, out_shape=(jax.ShapeDtypeStruct((M, N), F32),), grid=(M // tm, nk),
        in_specs=in_specs, out_specs=(pl.BlockSpec((tm, N), lambda i, k: (i, 0)),),
        scratch_shapes=[pltpu.VMEM((tm, N), F32)], sem=("parallel", "arbitrary"),
        args=[a for a, _, _ in parts] + [w for _, w, _ in parts] + [c_in], ride=ride)
    return res[0] if ride is None else (res[0][0], res[1])


def _gla_gate(pg_ref, rows, wg_ref, bg_ref):
    r = pg_ref[rows, 3072:3200].astype(BF)
    logit = _dot(r, wg_ref[...]) + bg_ref[...]
    la = (jnp.minimum(logit, 0.0) - jnp.log(1.0 + jnp.exp(-jnp.abs(logit)))) * (1.0 / GTAU)
    return r, logit, la


def _gla_fwd(pg, wg, bg, gn, ltri, *, nseq, S, tm):
    T = pg.shape[0]
    nb, nc = S // tm, tm // GC
    qscale = GDK ** -0.5

    def body(pg_ref, wg_ref, bg_ref, gn_ref, l_ref, o_ref, zg_ref, st_ref, st_scr):
        @pl.when(pl.program_id(1) == 0)
        def _():
            st_scr[...] = jnp.zeros_like(st_scr)

        ltri_v = l_ref[...]
        causal = _iota((GC, GC), 0) >= _iota((GC, GC), 1)
        last_row = _iota((GC, GDK), 0) == GC - 1
        g = gn_ref[...]

        def chunk(c, carry):
            rows = pl.ds(pl.multiple_of(c * GC, GC), GC)
            _, _, la = _gla_gate(pg_ref, rows, wg_ref, bg_ref)
            b = _tri_mm(ltri_v, la)
            hs = range(GH)
            v, q_in, k_st, dec, st, a_raw, o_st, kv = [], [], [], [], [], [], [], []
            for h in hs:
                q = pg_ref[rows, h * GDK:(h + 1) * GDK]
                k = pg_ref[rows, 512 + h * GDK:512 + (h + 1) * GDK]
                v.append(pg_ref[rows, 1024 + h * GDV:1024 + (h + 1) * GDV].astype(BF))
                bh = b[:, h * GDK:(h + 1) * GDK]
                bl = jnp.sum(jnp.where(last_row, bh, 0.0), axis=0, keepdims=True)
                q_in.append((q * (qscale * jnp.exp(bh))).astype(BF))
                k_in = (k * jnp.exp(-bh)).astype(BF)
                k_st.append((k * jnp.exp(bl - bh)).astype(BF))
                dec.append(jnp.exp(bl))
                st.append(st_scr[h])
                st_ref[c, h] = st[h]
                a_raw.append(_dot_nt(q_in[h], k_in))
            for h in hs:
                o_st.append(_dot_nt(q_in[h], st[h].astype(BF)))
                kv.append(_dot_tn(v[h], k_st[h]))
            att = [jnp.where(causal, a_raw[h], 0.0).astype(BF) for h in hs]
            o = [_dot(att[h], v[h]) + o_st[h] for h in hs]
            for h in hs:
                st_scr[h] = st[h] * dec[h] + kv[h]
                og = pg_ref[rows, 2048 + h * GDV:2048 + (h + 1) * GDV]
                rstd = lax.rsqrt(jnp.mean(o[h] * o[h], axis=-1, keepdims=True) + RMS_EPS)
                o_ref[rows, h * GDV:(h + 1) * GDV] = o[h]
                zg_ref[rows, h * GDV:(h + 1) * GDV] = (o[h] * rstd * g * (og * _sigmoid(og))).astype(BF)
            return carry

        lax.fori_loop(0, nc, chunk, 0, unroll=True)

    full = lambda shp: pl.BlockSpec(shp, lambda b_, i: (0,) * len(shp))
    return pl.pallas_call(
        body, name="gla_fwd",
        out_shape=(jax.ShapeDtypeStruct((T, GH * GDV), F32),
                   jax.ShapeDtypeStruct((T, GH * GDV), BF),
                   jax.ShapeDtypeStruct((T // GC, GH, GDV, GDK), F32)),
        grid=(nseq, nb),
        in_specs=[pl.BlockSpec((tm, PG_W), lambda b_, i: (b_ * nb + i, 0)),
                  full((128, 512)), full((1, 512)), full((1, GDV)), full((GC, GC))],
        out_specs=(pl.BlockSpec((tm, GH * GDV), lambda b_, i: (b_ * nb + i, 0)),
                   pl.BlockSpec((tm, GH * GDV), lambda b_, i: (b_ * nb + i, 0)),
                   pl.BlockSpec((nc, GH, GDV, GDK), lambda b_, i: (b_ * nb + i, 0, 0, 0))),
        scratch_shapes=[pltpu.VMEM((GH, GDV, GDK), F32)],
        compiler_params=_params(("parallel", "arbitrary")),
    )(pg, wg, bg, gn, ltri)


def _gla_bwd(pg, wg, bg, gn, ltri, utri, o, states, dzg, *, nseq, S, tm, ride=None):
    T = pg.shape[0]
    nb, nc = S // tm, tm // GC
    qscale = GDK ** -0.5

    def body(pg_ref, wg_ref, bg_ref, gn_ref, l_ref, u_ref, o_ref, st_ref, dzg_ref,
             dpg_ref, dwg_ref, dbg_ref, dgn_ref, dst_scr):
        first = jnp.logical_and(pl.program_id(0) == 0, pl.program_id(1) == 0)

        @pl.when(first)
        def _():
            dwg_ref[...] = jnp.zeros_like(dwg_ref)
            dbg_ref[...] = jnp.zeros_like(dbg_ref)
            dgn_ref[...] = jnp.zeros_like(dgn_ref)

        @pl.when(pl.program_id(1) == 0)
        def _():
            dst_scr[...] = jnp.zeros_like(dst_scr)

        ltri_v = l_ref[...]
        utri_v = u_ref[...]
        causal = _iota((GC, GC), 0) >= _iota((GC, GC), 1)
        last_row = _iota((GC, GDK), 0) == GC - 1
        g = gn_ref[...]

        def chunk(cc, carry):
            c = nc - 1 - cc
            rows = pl.ds(pl.multiple_of(c * GC, GC), GC)
            r, logit, la = _gla_gate(pg_ref, rows, wg_ref, bg_ref)
            b = _tri_mm(ltri_v, la)
            hs = range(GH)
            L = lambda: [None] * GH
            vb, eb, enb, ek, dec, q_in, k_in, k_st, q_inb, k_inb, st, dst, dob = (L() for _ in range(13))
            a_raw, da_raw, dq_st, dks, dv_st, dst_new, dbs, dgn = (L() for _ in range(8))
            for h in hs:
                q = pg_ref[rows, h * GDK:(h + 1) * GDK]
                k = pg_ref[rows, 512 + h * GDK:512 + (h + 1) * GDK]
                vb[h] = pg_ref[rows, 1024 + h * GDV:1024 + (h + 1) * GDV].astype(BF)
                og = pg_ref[rows, 2048 + h * GDV:2048 + (h + 1) * GDV]
                oh = o_ref[rows, h * GDV:(h + 1) * GDV]
                dz = dzg_ref[rows, h * GDV:(h + 1) * GDV].astype(F32)
                bh = b[:, h * GDK:(h + 1) * GDK]
                bl = jnp.sum(jnp.where(last_row, bh, 0.0), axis=0, keepdims=True)
                eb[h] = qscale * jnp.exp(bh)
                enb[h] = jnp.exp(-bh)
                ek[h] = jnp.exp(bl - bh)
                dec[h] = jnp.exp(bl)
                q_in[h], k_in[h], k_st[h] = q * eb[h], k * enb[h], k * ek[h]
                q_inb[h], k_inb[h] = q_in[h].astype(BF), k_in[h].astype(BF)
                st[h] = st_ref[c, h]
                dst[h] = dst_scr[h]
                rstd = lax.rsqrt(jnp.mean(oh * oh, axis=-1, keepdims=True) + RMS_EPS)
                ohat = oh * rstd
                sg = _sigmoid(og)
                don = dz * (og * sg)
                dpg_ref[rows, 2048 + h * GDV:2048 + (h + 1) * GDV] = (
                    dz * (ohat * g) * (sg * (1.0 + og * (1.0 - sg)))).astype(BF)
                dgn[h] = jnp.sum(don * ohat, axis=0, keepdims=True)
                gd = don * g
                dob[h] = (rstd * (gd - ohat * jnp.mean(gd * ohat, axis=-1, keepdims=True))).astype(BF)
                a_raw[h] = _dot_nt(q_inb[h], k_inb[h])
                da_raw[h] = _dot_nt(dob[h], vb[h])
            dgn_ref[...] += dgn[0] + dgn[1] + dgn[2] + dgn[3]
            for h in hs:
                dstb = dst[h].astype(BF)
                dq_st[h] = _dot(dob[h], st[h].astype(BF))
                dks[h] = _dot(vb[h], dstb)
                dv_st[h] = _dot_nt(k_st[h].astype(BF), dstb)
                dst_new[h] = _dot_tn(dob[h], q_inb[h])
            att = [jnp.where(causal, a_raw[h], 0.0).astype(BF) for h in hs]
            da = [jnp.where(causal, da_raw[h], 0.0).astype(BF) for h in hs]
            dqi = [_dot(da[h], k_inb[h]) + dq_st[h] for h in hs]
            dki = [_dot_tn(da[h], q_inb[h]) for h in hs]
            dv = [_dot_tn(att[h], dob[h]) + dv_st[h] for h in hs]
            for h in hs:
                dd = jnp.sum(dst[h] * st[h], axis=0, keepdims=True)
                dst_scr[h] = dst[h] * dec[h] + dst_new[h]
                kk = dks[h] * k_st[h]
                dbl = jnp.sum(kk, axis=0, keepdims=True) + dd * dec[h]
                db = dqi[h] * q_in[h] - dki[h] * k_in[h] - kk
                dbs[h] = db + jnp.where(last_row, dbl, 0.0)
                dpg_ref[rows, h * GDK:(h + 1) * GDK] = (dqi[h] * eb[h]).astype(BF)
                dpg_ref[rows, 512 + h * GDK:512 + (h + 1) * GDK] = (dki[h] * enb[h] + dks[h] * ek[h]).astype(BF)
                dpg_ref[rows, 1024 + h * GDV:1024 + (h + 1) * GDV] = dv[h].astype(BF)
            dla = _tri_mm(utri_v, jnp.concatenate(dbs, axis=1))
            dlogit = dla * (1.0 / GTAU) * _sigmoid(-logit)
            dlb = dlogit.astype(BF)
            dpg_ref[rows, 3072:3200] = _dot_nt(dlb, wg_ref[...]).astype(BF)
            dwg_ref[...] += _dot_tn(r, dlb)
            dbg_ref[...] += jnp.sum(dlogit, axis=0, keepdims=True)
            return carry

        lax.fori_loop(0, nc, chunk, 0)

    full = lambda shp: pl.BlockSpec(shp, lambda b_, i: (0,) * len(shp))
    rev = lambda b_, i: (b_ * nb + nb - 1 - i, 0)
    return _call(
        body, name="gla_bwd", ride=ride, sem=("arbitrary", "arbitrary"),
        args=(pg, wg, bg, gn, ltri, utri, o, states, dzg),
        out_shape=(jax.ShapeDtypeStruct((T, PG_W), BF),
                   jax.ShapeDtypeStruct((128, 512), F32),
                   jax.ShapeDtypeStruct((1, 512), F32),
                   jax.ShapeDtypeStruct((1, GDV), F32)),
        grid=(nseq, nb),
        in_specs=[pl.BlockSpec((tm, PG_W), rev),
                  full((128, 512)), full((1, 512)), full((1, GDV)), full((GC, GC)), full((GC, GC)),
                  pl.BlockSpec((tm, GH * GDV), rev),
                  pl.BlockSpec((nc, GH, GDV, GDK), lambda b_, i: (b_ * nb + nb - 1 - i, 0, 0, 0)),
                  pl.BlockSpec((tm, GH * GDV), rev)],
        out_specs=(pl.BlockSpec((tm, PG_W), rev), full((128, 512)), full((1, 512)), full((1, GDV))),
        scratch_shapes=[pltpu.VMEM((GH, GDV, GDK), F32)])


def _rope_tables(pos, invf):
    ang = pos.astype(F32) * invf
    lane = _iota(ang.shape, 1)
    sin = jnp.sin(ang)
    ssin = jnp.where(lane < 32, -sin, jnp.where(lane < 64, sin, 0.0))
    return jnp.cos(ang), ssin, lane


def _rope(x, cos, ssin, lane, sign):
    rot = jnp.where(lane < 32, pltpu.roll(x, 96, 1), pltpu.roll(x, 32, 1))
    return x * cos + sign * (rot * ssin)


def _rms_fwd(x, g):
    rstd = lax.rsqrt(jnp.mean(x * x, axis=-1, keepdims=True) + RMS_EPS)
    return x * rstd * g, x * rstd, rstd


def _rms_bwd(dy, xhat, rstd, g):
    gd = dy * g
    return rstd * (gd - xhat * jnp.mean(gd * xhat, axis=-1, keepdims=True)), jnp.sum(dy * xhat, axis=0, keepdims=True)


def _mla_prep_fwd(pm, pos, invf, gq, gkv, wuq, wukv, *, tm):
    T = pm.shape[0]

    def body(pm_ref, pos_ref, invf_ref, gq_ref, gkv_ref, wuq_ref, wukv_ref, qc_ref, kc_ref, v_ref):
        cos, ssin, lane = _rope_tables(pos_ref[...], invf_ref[...])
        cq, _, _ = _rms_fwd(pm_ref[:, 0:MQR], gq_ref[...])
        ckv, _, _ = _rms_fwd(pm_ref[:, 512:768], gkv_ref[...])
        qf = _dot(cq.astype(BF), wuq_ref[...])
        kvf = _dot(ckv.astype(BF), wukv_ref[...])
        kr = _rope(pm_ref[:, 384:512], cos, ssin, lane, 1.0).astype(BF)
        for h in range(MH):
            qc_ref[:, 256 * h:256 * h + 128] = (QK_SCALE_LOG2 * qf[:, 128 * h:128 * h + 128]).astype(BF)
            qr = qf[:, 1024 + 128 * h:1024 + 128 * h + 128]
            qc_ref[:, 256 * h + 128:256 * h + 256] = (QK_SCALE_LOG2 * _rope(qr, cos, ssin, lane, 1.0)).astype(BF)
            kc_ref[:, 256 * h:256 * h + 128] = kvf[:, 128 * h:128 * h + 128].astype(BF)
            kc_ref[:, 256 * h + 128:256 * h + 256] = kr
        v_ref[...] = kvf[:, 1024:2048].astype(BF)

    full = lambda shp: pl.BlockSpec(shp, lambda i: (0,) * len(shp))
    row = lambda w: pl.BlockSpec((tm, w), lambda i: (i, 0))
    return pl.pallas_call(
        body, name="mla_prep_fwd",
        out_shape=(jax.ShapeDtypeStruct((T, MH * 256), BF), jax.ShapeDtypeStruct((T, MH * 256), BF),
                   jax.ShapeDtypeStruct((T, MH * MV), BF)),
        grid=(T // tm,),
        in_specs=[row(PM_W), row(1), full((1, 128)), full((1, MQR)), full((1, MKR)),
                  full((MQR, 2048)), full((MKR, 2048))],
        out_specs=(row(MH * 256), row(MH * 256), row(MH * MV)),
        compiler_params=_params(("parallel",)),
    )(pm, pos, invf, gq, gkv, wuq, wukv)


def _mla_prep_bwd(pm, pos, invf, gq, gkv, wuq, wukv, dqc, dkc, dv, *, tm):
    T = pm.shape[0]

    def body(pm_ref, pos_ref, invf_ref, gq_ref, gkv_ref, wuq_ref, wukv_ref, dqc_ref, dkc_ref, dv_ref,
             dpm_ref, dwuq_ref, dwukv_ref, dgq_ref, dgkv_ref):
        @pl.when(pl.program_id(0) == 0)
        def _():
            dwuq_ref[...] = jnp.zeros_like(dwuq_ref)
            dwukv_ref[...] = jnp.zeros_like(dwukv_ref)
            dgq_ref[...] = jnp.zeros_like(dgq_ref)
            dgkv_ref[...] = jnp.zeros_like(dgkv_ref)

        cos, ssin, lane = _rope_tables(pos_ref[...], invf_ref[...])
        cq, cqh, cq_rstd = _rms_fwd(pm_ref[:, 0:MQR], gq_ref[...])
        ckv, ckvh, ckv_rstd = _rms_fwd(pm_ref[:, 512:768], gkv_ref[...])
        dqn, dqr, dkn = [], [], []
        dkr = jnp.zeros((tm, 128), F32)
        for h in range(MH):
            dqn.append(dqc_ref[:, 256 * h:256 * h + 128].astype(BF))
            dqr.append(_rope(dqc_ref[:, 256 * h + 128:256 * h + 256], cos, ssin, lane, -1.0).astype(BF))
            dkn.append(dkc_ref[:, 256 * h:256 * h + 128].astype(BF))
            dkr = dkr + dkc_ref[:, 256 * h + 128:256 * h + 256]
        dqf = jnp.concatenate(dqn + dqr, axis=1)
        dkvf = jnp.concatenate(dkn + [dv_ref[...].astype(BF)], axis=1)
        dwuq_ref[...] += _dot_tn(cq.astype(BF), dqf)
        dwukv_ref[...] += _dot_tn(ckv.astype(BF), dkvf)
        dcq, dgq = _rms_bwd(_dot_nt(dqf, wuq_ref[...]), cqh, cq_rstd, gq_ref[...])
        dckv, dgkv = _rms_bwd(_dot_nt(dkvf, wukv_ref[...]), ckvh, ckv_rstd, gkv_ref[...])
        dgq_ref[...] += dgq
        dgkv_ref[...] += dgkv
        dpm_ref[:, 0:MQR] = dcq.astype(BF)
        dpm_ref[:, 384:512] = _rope(dkr, cos, ssin, lane, -1.0).astype(BF)
        dpm_ref[:, 512:768] = dckv.astype(BF)

    full = lambda shp: pl.BlockSpec(shp, lambda i: (0,) * len(shp))
    row = lambda w: pl.BlockSpec((tm, w), lambda i: (i, 0))
    return pl.pallas_call(
        body, name="mla_prep_bwd",
        out_shape=(jax.ShapeDtypeStruct((T, PM_W), BF), jax.ShapeDtypeStruct((MQR, 2048), F32),
                   jax.ShapeDtypeStruct((MKR, 2048), F32), jax.ShapeDtypeStruct((1, MQR), F32),
                   jax.ShapeDtypeStruct((1, MKR), F32)),
        grid=(T // tm,),
        in_specs=[row(PM_W), row(1), full((1, 128)), full((1, MQR)), full((1, MKR)),
                  full((MQR, 2048)), full((MKR, 2048)), row(MH * 256), row(MH * 256), row(MH * MV)],
        out_specs=(row(PM_W), full((MQR, 2048)), full((MKR, 2048)), full((1, MQR)), full((1, MKR))),
        compiler_params=_params(("arbitrary",)),
    )(pm, pos, invf, gq, gkv, wuq, wukv, dqc, dkc, dv)


def _flash_fwd(qc, kc, v, *, nseq, S, tq, ride=None):
    T = qc.shape[0]
    nq = S // tq

    def body(q_ref, k_ref, v_ref, o_ref, lse_ref):
        i = pl.program_id(2)
        causal = _iota((tq, tq), 0) >= _iota((tq, tq), 1)

        def step(j, carry, masked):
            rows = pl.ds(pl.multiple_of(j * tq, tq), tq)
            out = []
            for hh in range(FLASH_HP):
                m, l, acc = carry[hh]
                s = _dot_nt(q_ref[:, 256 * hh:256 * hh + 256], k_ref[rows, 256 * hh:256 * hh + 256])
                if masked:
                    s = jnp.where(causal, s, NEG)
                m_new = jnp.maximum(m, jnp.max(s, axis=-1, keepdims=True))
                p = jnp.exp2(s - m_new)
                a = jnp.exp2(m - m_new)
                l = a * l + jnp.sum(p, axis=-1, keepdims=True)
                acc = a * acc + _dot(p.astype(BF), v_ref[rows, MV * hh:MV * hh + MV])
                out.append((m_new, l, acc))
            return tuple(out)

        init = ((jnp.full((tq, 1), NEG, F32), jnp.zeros((tq, 1), F32), jnp.zeros((tq, MV), F32)),) * FLASH_HP
        carry = lax.fori_loop(0, i, lambda j, c: step(j, c, False), init)
        for hh, (m, l, acc) in enumerate(step(i, carry, True)):
            o_ref[:, MV * hh:MV * hh + MV] = (acc / l).astype(BF)
            lse_ref[:, 128 * hh:128 * hh + 128] = jnp.broadcast_to(m + jnp.log2(l), (tq, 128))

    hp = FLASH_HP
    return _call(
        body, name="flash_fwd", ride=ride, sem=("parallel", "parallel", "arbitrary"), args=(qc, kc, v),
        out_shape=(jax.ShapeDtypeStruct((T, MH * MV), BF), jax.ShapeDtypeStruct((T, MH * 128), F32)),
        grid=(nseq, MH // hp, nq),
        in_specs=[pl.BlockSpec((tq, 256 * hp), lambda b_, h, i: (b_ * nq + i, h)),
                  pl.BlockSpec((S, 256 * hp), lambda b_, h, i: (b_, h)),
                  pl.BlockSpec((S, MV * hp), lambda b_, h, i: (b_, h))],
        out_specs=(pl.BlockSpec((tq, MV * hp), lambda b_, h, i: (b_ * nq + i, h)),
                   pl.BlockSpec((tq, 128 * hp), lambda b_, h, i: (b_ * nq + i, h))))


def _flash_bwd(qc, kc, v, o, do, lse, *, nseq, S, tq, ride=None):
    T = qc.shape[0]
    nq = S // tq

    def body(q_ref, k_ref, v_ref, o_ref, do_ref, lse_ref, dq_ref, dk_ref, dv_ref, dq_scr, delta_scr):
        j = pl.program_id(2)

        @pl.when(j == 0)
        def _():
            dq_scr[...] = jnp.zeros_like(dq_scr)
            for hh in range(FLASH_HP):
                od = o_ref[:, MV * hh:MV * hh + MV].astype(F32) * do_ref[:, MV * hh:MV * hh + MV].astype(F32)
                delta_scr[:, 128 * hh:128 * hh + 128] = jnp.broadcast_to(jnp.sum(od, axis=-1, keepdims=True), (S, 128))

        causal = _iota((tq, tq), 0) >= _iota((tq, tq), 1)

        def step(i, carry, masked):
            rows = pl.ds(pl.multiple_of(i * tq, tq), tq)
            out = []
            for hh in range(FLASH_HP):
                dk, dv = carry[hh]
                qs, vs, ls = slice(256 * hh, 256 * hh + 256), slice(MV * hh, MV * hh + MV), slice(128 * hh, 128 * hh + 1)
                q = q_ref[rows, qs]
                dob = do_ref[rows, vs]
                kb = k_ref[:, qs]
                p = jnp.exp2(_dot_nt(q, kb) - lse_ref[rows, ls])
                if masked:
                    p = jnp.where(causal, p, 0.0)
                dv = dv + _dot_tn(p.astype(BF), dob)
                dp = _dot_nt(dob, v_ref[:, vs])
                ds = (p * (dp - delta_scr[rows, ls])).astype(BF)
                dk = dk + _dot_tn(ds, q)
                dq_scr[rows, qs] += _dot(ds, kb)
                out.append((dk, dv))
            return tuple(out)

        init = ((jnp.zeros((tq, 256), F32), jnp.zeros((tq, MV), F32)),) * FLASH_HP
        carry = step(j, init, True)
        carry = lax.fori_loop(j + 1, nq, lambda i, c: step(i, c, False), carry)
        for hh, (dk, dv) in enumerate(carry):
            dk_ref[:, 256 * hh:256 * hh + 256] = dk * (1.0 / LOG2E)
            dv_ref[:, MV * hh:MV * hh + MV] = dv

        @pl.when(j == nq - 1)
        def _():
            dq_ref[...] = dq_scr[...] * QK_SCALE

    hp = FLASH_HP
    seq = lambda w: pl.BlockSpec((S, w * hp), lambda b_, h, j: (b_, h))
    blk = lambda w: pl.BlockSpec((tq, w * hp), lambda b_, h, j: (b_ * nq + j, h))
    return _call(
        body, name="flash_bwd", ride=ride, sem=("parallel", "parallel", "arbitrary"), args=(qc, kc, v, o, do, lse),
        out_shape=(jax.ShapeDtypeStruct((T, MH * 256), F32), jax.ShapeDtypeStruct((T, MH * 256), F32),
                   jax.ShapeDtypeStruct((T, MH * MV), F32)),
        grid=(nseq, MH // hp, nq),
        in_specs=[seq(256), blk(256), blk(MV), seq(MV), seq(MV), seq(128)],
        out_specs=(seq(256), blk(256), blk(MV)),
        scratch_shapes=[pltpu.VMEM((S, 256 * hp), F32), pltpu.VMEM((S, 128 * hp), F32)])


def _ln_fwd(pre, g, b):
    mu = jnp.mean(pre, axis=-1, keepdims=True)
    xc = pre - mu
    rstd = lax.rsqrt(jnp.mean(xc * xc, axis=-1, keepdims=True) + LN_EPS)
    xhat = xc * rstd
    return xhat * g + b, xhat, rstd


def _ln_bwd(dy, xhat, rstd, g):
    dxh = dy * g
    dx = rstd * (dxh - jnp.mean(dxh, axis=-1, keepdims=True) - xhat * jnp.mean(dxh * xhat, axis=-1, keepdims=True))
    return dx, jnp.sum(dy * xhat, axis=0, keepdims=True), jnp.sum(dy, axis=0, keepdims=True)


def _post_attn_fwd(zg, attn, pt, x, wgo, wmo, wout, g1, b1, *, tm):
    T = x.shape[0]

    def body(zg_ref, at_ref, pt_ref, x_ref, wgo_ref, wmo_ref, wout_ref, g_ref, b_ref,
             yg_ref, ym_ref, mix_ref, pre_ref, h_ref, hb_ref):
        yg = _dot(zg_ref[...], wgo_ref[...])
        ym = _dot(at_ref[...], wmo_ref[...])
        mix = (_sigmoid(pt_ref[:, 0:D]) * yg + _sigmoid(pt_ref[:, D:2 * D]) * ym).astype(BF)
        pre = ALPHA * x_ref[...] + _dot(mix, wout_ref[...])
        h, _, _ = _ln_fwd(pre, g_ref[...], b_ref[...])
        yg_ref[...] = yg
        ym_ref[...] = ym
        mix_ref[...] = mix
        pre_ref[...] = pre
        h_ref[...] = h
        hb_ref[...] = h.astype(BF)

    full = lambda shp: pl.BlockSpec(shp, lambda i: (0,) * len(shp))
    row = lambda w: pl.BlockSpec((tm, w), lambda i: (i, 0))
    sd = lambda dt: jax.ShapeDtypeStruct((T, D), dt)
    return pl.pallas_call(
        body, name="post_attn_fwd",
        out_shape=(sd(F32), sd(F32), sd(BF), sd(F32), sd(F32), sd(BF)),
        grid=(T // tm,),
        in_specs=[row(D), row(D), row(PT_W), row(D), full((D, D)), full((D, D)), full((D, D)),
                  full((1, D)), full((1, D))],
        out_specs=(row(D),) * 6,
        compiler_params=_params(("parallel",)),
    )(zg, attn, pt, x, wgo, wmo, wout, g1, b1)


def _post_attn_bwd(dh, pre, pt, yg, ym, wgo, wmo, wout, g1, *, tm):
    T = dh.shape[0]

    def body(dh_ref, pre_ref, pt_ref, yg_ref, ym_ref, wgo_ref, wmo_ref, wout_ref, g_ref,
             dx_ref, dpreb_ref, dpt_ref, dygb_ref, dymb_ref, dzg_ref, dat_ref, dg_ref, db_ref):
        @pl.when(pl.program_id(0) == 0)
        def _():
            dg_ref[...] = jnp.zeros_like(dg_ref)
            db_ref[...] = jnp.zeros_like(db_ref)

        pre = pre_ref[...]
        mu = jnp.mean(pre, axis=-1, keepdims=True)
        xc = pre - mu
        rstd = lax.rsqrt(jnp.mean(xc * xc, axis=-1, keepdims=True) + LN_EPS)
        dpre, dg, db = _ln_bwd(dh_ref[...], xc * rstd, rstd, g_ref[...])
        dg_ref[...] += dg
        db_ref[...] += db
        dx_ref[...] = ALPHA * dpre
        dpreb = dpre.astype(BF)
        dpreb_ref[...] = dpreb
        dmix = _dot_nt(dpreb, wout_ref[...])
        sa = _sigmoid(pt_ref[:, 0:D])
        sb = _sigmoid(pt_ref[:, D:2 * D])
        dpt_ref[:, 0:D] = (dmix * yg_ref[...] * (sa * (1.0 - sa))).astype(BF)
        dpt_ref[:, D:2 * D] = (dmix * ym_ref[...] * (sb * (1.0 - sb))).astype(BF)
        dyg = (dmix * sa).astype(BF)
        dym = (dmix * sb).astype(BF)
        dygb_ref[...] = dyg
        dymb_ref[...] = dym
        dzg_ref[...] = _dot_nt(dyg, wgo_ref[...]).astype(BF)
        dat_ref[...] = _dot_nt(dym, wmo_ref[...]).astype(BF)

    full = lambda shp: pl.BlockSpec(shp, lambda i: (0,) * len(shp))
    row = lambda w: pl.BlockSpec((tm, w), lambda i: (i, 0))
    sd = lambda w, dt: jax.ShapeDtypeStruct((T, w), dt)
    return pl.pallas_call(
        body, name="post_attn_bwd",
        out_shape=(sd(D, F32), sd(D, BF), sd(PT_W, BF), sd(D, BF), sd(D, BF), sd(D, BF), sd(D, BF),
                   jax.ShapeDtypeStruct((1, D), F32), jax.ShapeDtypeStruct((1, D), F32)),
        grid=(T // tm,),
        in_specs=[row(D), row(D), row(PT_W), row(D), row(D), full((D, D)), full((D, D)), full((D, D)),
                  full((1, D))],
        out_specs=(row(D), row(D), row(PT_W), row(D), row(D), row(D), row(D), full((1, D)), full((1, D))),
        compiler_params=_params(("arbitrary",)),
    )(dh, pre, pt, yg, ym, wgo, wmo, wout, g1)


def _shift_down(u, prev, k):
    r = pltpu.roll(u, k, 0)
    p = pltpu.roll(prev, k, 0)
    head = jnp.where(_iota(p.shape, 0) < k, p, r[0:8, :])
    return jnp.concatenate([head, r[8:, :]], axis=0)


def _shift_up(u, nxt, k):
    n = u.shape[0]
    r = pltpu.roll(u, n - k, 0)
    p = pltpu.roll(nxt, 8 - k, 0)
    tail = jnp.where(_iota(p.shape, 0) >= 8 - k, p, r[n - 8:, :])
    return jnp.concatenate([r[:n - 8, :], tail], axis=0)


def _conv3(u, prev, w_ref, b_ref):
    return (w_ref[0:1, :] * _shift_down(u, prev, 2) + w_ref[1:2, :] * _shift_down(u, prev, 1)
            + w_ref[2:3, :] * u + b_ref[...])


def _ffn_up_fwd(hb, wug, wuv, cw, cb, *, S, tm, tn):
    T = hb.shape[0]
    nj, nbs = DFF // tn, S // tm

    def body(h_ref, wg_ref, wv_ref, cwg_ref, cwv_ref, cbg_ref, cbv_ref,
             ug_ref, uv_ref, ucg_ref, ucv_ref, f_ref, pg_scr, pv_scr):
        @pl.when(pl.program_id(1) % nbs == 0)
        def _():
            pg_scr[...] = jnp.zeros_like(pg_scr)
            pv_scr[...] = jnp.zeros_like(pv_scr)

        h = h_ref[...]
        ug = _dot(h, wg_ref[...])
        uv = _dot(h, wv_ref[...])
        ucg = _conv3(ug, pg_scr[...], cwg_ref, cbg_ref)
        ucv = _conv3(uv, pv_scr[...], cwv_ref, cbv_ref)
        pg_scr[...] = ug[tm - 8:, :]
        pv_scr[...] = uv[tm - 8:, :]
        ug_ref[...] = ug.astype(BF)
        uv_ref[...] = uv.astype(BF)
        ucg_ref[...] = ucg
        ucv_ref[...] = ucv
        f_ref[...] = (ucg * _sigmoid(ucg) * ucv).astype(BF)

    tile = pl.BlockSpec((tm, tn), lambda j, i: (i, j))
    return pl.pallas_call(
        body, name="ffn_up_fwd",
        out_shape=(jax.ShapeDtypeStruct((T, DFF), BF), jax.ShapeDtypeStruct((T, DFF), BF),
                   jax.ShapeDtypeStruct((T, DFF), F32), jax.ShapeDtypeStruct((T, DFF), F32),
                   jax.ShapeDtypeStruct((T, DFF), BF)),
        grid=(nj, T // tm),
        in_specs=[pl.BlockSpec((tm, D), lambda j, i: (i, 0)),
                  pl.BlockSpec((D, tn), lambda j, i: (0, j)), pl.BlockSpec((D, tn), lambda j, i: (0, j)),
                  pl.BlockSpec((3, tn), lambda j, i: (0, j)), pl.BlockSpec((3, tn), lambda j, i: (0, j + nj)),
                  pl.BlockSpec((1, tn), lambda j, i: (0, j)), pl.BlockSpec((1, tn), lambda j, i: (0, j + nj))],
        out_specs=(tile, tile, tile, tile, tile),
        scratch_shapes=[pltpu.VMEM((8, tn), F32), pltpu.VMEM((8, tn), F32)],
        compiler_params=_params(("parallel", "arbitrary")),
    )(hb, wug, wuv, cw, cw, cb, cb)


def _ffn_bwd(dpreb, wd, ug, uv, ucg, ucv, cw, *, S, tm, tn):
    T = dpreb.shape[0]
    nj, nb, nbs = DFF // tn, T // tm, S // tm

    def body(dp_ref, wd_ref, ug_ref, uv_ref, ucg_ref, ucv_ref, cwg_ref, cwv_ref,
             dug_ref, duv_ref, dcg_ref, dcv_ref, ng_scr, nv_scr):
        ii = pl.program_id(1)
        i = nb - 1 - ii

        @pl.when(ii == 0)
        def _():
            dcg_ref[...] = jnp.zeros_like(dcg_ref)
            dcv_ref[...] = jnp.zeros_like(dcv_ref)

        @pl.when(i % nbs == nbs - 1)
        def _():
            ng_scr[...] = jnp.zeros_like(ng_scr)
            nv_scr[...] = jnp.zeros_like(nv_scr)

        df = _dot_nt(dp_ref[...], wd_ref[...])
        ucg = ucg_ref[...]
        sg = _sigmoid(ucg)
        ducg = df * ucv_ref[...] * (sg * (1.0 + ucg * (1.0 - sg)))
        ducv = df * (ucg * sg)

        def finish(duc, u_ref, w, nxt_scr, du_ref, dc_ref):
            nxt = nxt_scr[...]
            up1 = _shift_up(duc, nxt, 1)
            up2 = _shift_up(duc, nxt, 2)
            du_ref[...] = (w[2:3, :] * duc + w[1:2, :] * up1 + w[0:1, :] * up2).astype(BF)
            nxt_scr[...] = duc[0:8, :]
            u = u_ref[...].astype(F32)
            for row, z in enumerate((u * up2, u * up1, u * duc, duc)):
                dc_ref[row:row + 1, :] += jnp.sum(z, axis=0, keepdims=True)

        finish(ducg, ug_ref, cwg_ref, ng_scr, dug_ref, dcg_ref)
        finish(ducv, uv_ref, cwv_ref, nv_scr, duv_ref, dcv_ref)

    tile = pl.BlockSpec((tm, tn), lambda j, ii: (nb - 1 - ii, j))
    acc = pl.BlockSpec((8, tn), lambda j, ii: (0, j))
    return pl.pallas_call(
        body, name="ffn_bwd",
        out_shape=(jax.ShapeDtypeStruct((T, DFF), BF), jax.ShapeDtypeStruct((T, DFF), BF),
                   jax.ShapeDtypeStruct((8, DFF), F32), jax.ShapeDtypeStruct((8, DFF), F32)),
        grid=(nj, nb),
        in_specs=[pl.BlockSpec((tm, D), lambda j, ii: (nb - 1 - ii, 0)),
                  pl.BlockSpec((tn, D), lambda j, ii: (j, 0)),
                  tile, tile, tile, tile,
                  pl.BlockSpec((3, tn), lambda j, ii: (0, j)), pl.BlockSpec((3, tn), lambda j, ii: (0, j + nj))],
        out_specs=(tile, tile, acc, acc),
        scratch_shapes=[pltpu.VMEM((8, tn), F32), pltpu.VMEM((8, tn), F32)],
        compiler_params=_params(("parallel", "arbitrary")),
    )(dpreb, wd, ug, uv, ucg, ucv, cw, cw)


def _down_ln2_loss(f_in, wd, h, target, g2, b2, *, tm):
    T = h.shape[0]

    def body(f_ref, wd_ref, h_ref, t_ref, g_ref, b_ref, dpb_ref, dh_ref, loss_ref, dg_ref, db_ref):
        @pl.when(pl.program_id(0) == 0)
        def _():
            loss_ref[...] = jnp.zeros_like(loss_ref)
            dg_ref[...] = jnp.zeros_like(dg_ref)
            db_ref[...] = jnp.zeros_like(db_ref)

        pre = ALPHA * h_ref[...] + _dot(f_ref[...], wd_ref[...])
        out, xhat, rstd = _ln_fwd(pre, g_ref[...], b_ref[...])
        diff = out - t_ref[...]
        loss_ref[...] += 0.5 * jnp.sum(jnp.mean(diff * diff, axis=-1, keepdims=True))
        dpre, dg, db = _ln_bwd(diff * (1.0 / D), xhat, rstd, g_ref[...])
        dg_ref[...] += dg
        db_ref[...] += db
        dpb_ref[...] = dpre.astype(BF)
        dh_ref[...] = ALPHA * dpre

    full = lambda shp: pl.BlockSpec(shp, lambda i: (0,) * len(shp))
    row = lambda w: pl.BlockSpec((tm, w), lambda i: (i, 0))
    return pl.pallas_call(
        body, name="down_ln2_loss",
        out_shape=(jax.ShapeDtypeStruct((T, D), BF), jax.ShapeDtypeStruct((T, D), F32),
                   jax.ShapeDtypeStruct((8, 128), F32), jax.ShapeDtypeStruct((1, D), F32),
                   jax.ShapeDtypeStruct((1, D), F32)),
        grid=(T // tm,),
        in_specs=[row(DFF), full((DFF, D)), row(D), row(D), full((1, D)), full((1, D))],
        out_specs=(row(D), row(D), full((8, 128)), full((1, D)), full((1, D))),
        compiler_params=_params(("arbitrary",)),
    )(f_in, wd, h, target, g2, b2)


def _adamw(parts, w, m, v, *, name):
    n, R, C = parts.shape
    tr, tc = R, C
    for cand in range(min(R, 256), 15, -1):
        if R % cand == 0 and cand % 16 == 0:
            tr = cand
            break
    if tr == R and R * C > 65536 and C % 256 == 0:
        tc = 256
    c1 = 1.0 - ADAM_B1 ** ADAM_STEP
    c2 = 1.0 - ADAM_B2 ** ADAM_STEP

    def body(p_ref, w_ref, m_ref, v_ref, g_ref, d_ref, nm_ref, nv_ref):
        g = p_ref[0].astype(F32)
        for s in range(1, n):
            g = g + p_ref[s].astype(F32)
        nm = ADAM_B1 * m_ref[...] + (1.0 - ADAM_B1) * g
        nv = ADAM_B2 * v_ref[...] + (1.0 - ADAM_B2) * (g * g)
        g_ref[...] = g
        nm_ref[...] = nm
        nv_ref[...] = nv
        d_ref[...] = -ADAM_LR * ((nm / c1) / (jnp.sqrt(nv / c2) + ADAM_EPS) + ADAM_WD * w_ref[...])

    blk = pl.BlockSpec((tr, tc), lambda i, j: (i, j))
    sd = jax.ShapeDtypeStruct((R, C), F32)
    return pl.pallas_call(
        body, name=name,
        out_shape=(sd, sd, sd, sd),
        grid=(R // tr, C // tc),
        in_specs=[pl.BlockSpec((n, tr, tc), lambda i, j: (0, i, j)), blk, blk, blk],
        out_specs=(blk, blk, blk, blk),
        compiler_params=_params(("parallel", "parallel")),
    )(parts, w, m, v)


class _Exchange:
    def __init__(self, items):
        self.items = [(src if sc else [(src, 0)], sc) for src, sc in items]
        self.arrays = [arr for srcs, _ in self.items for arr, _ in srcs]
        self.n = len(self.items)
        self.n_in = len(self.arrays)

    def out_shape(self):
        return tuple(jax.ShapeDtypeStruct((NDEV,) + (srcs[0][0].shape[1:] if sc else srcs[0][0].shape),
                                          srcs[0][0].dtype) for srcs, sc in self.items)

    def scratch(self):
        return [pltpu.SemaphoreType.DMA((self.n, NDEV - 1)), pltpu.SemaphoreType.DMA((self.n, NDEV - 1)),
                pltpu.SemaphoreType.DMA((self.n,))]

    def _emit(self, ins, outs, sems, phase):
        send_sems, recv_sems, loc_sems = sems
        x, y, c = lax.axis_index("x"), lax.axis_index("y"), lax.axis_index("c")
        me = 4 * x + 2 * y + c
        flip = lambda p, d: 1 - p if d else p

        def inside(p, lo, n):
            return None if (lo, n) == (0, NDEV) else jnp.logical_and(p >= lo, p < lo + n)

        def when(cond, fn):
            if cond is None:
                fn()
            else:
                pl.when(cond)(fn)

        pos = 0
        for a, (srcs, sc) in enumerate(self.items):
            refs = ins[pos:pos + len(srcs)]
            pos += len(srcs)
            ranges = [(lo, arr.shape[0]) if sc else (0, NDEV) for arr, lo in srcs]
            mine = [inside(me, lo, n) for lo, n in ranges]
            i_receive = None if None in mine else functools.reduce(jnp.logical_or, mine)
            for ref, (lo, n), cond in zip(refs, ranges, mine):
                def local(ref=ref, lo=lo):
                    cp = pltpu.make_async_copy(ref.at[me - lo] if sc else ref, outs[a].at[me], loc_sems.at[a])
                    cp.start() if phase == 0 else cp.wait()
                if phase != 1:
                    when(cond, local)
            for k in range(1, NDEV):
                px, py, pc = flip(x, k & 4), flip(y, k & 2), flip(c, k & 1)
                peer = 4 * px + 2 * py + pc
                mk = functools.partial(pltpu.make_async_remote_copy,
                                       send_sem=send_sems.at[a, k - 1], recv_sem=recv_sems.at[a, k - 1],
                                       device_id=(px, py, pc), device_id_type=MESH_ID)
                if phase == 1:
                    def arrival(mk=mk, peer=peer):
                        mk(src_ref=refs[0].at[0] if sc else refs[0], dst_ref=outs[a].at[peer]).wait_recv()
                    when(i_receive, arrival)
                    continue
                for ref, (lo, n) in zip(refs, ranges):
                    def send(mk=mk, ref=ref, lo=lo, peer=peer):
                        cp = mk(src_ref=ref.at[peer - lo] if sc else ref, dst_ref=outs[a].at[me])
                        cp.start() if phase == 0 else cp.wait_send()
                    when(inside(peer, lo, n), send)

    def start(self, ins, outs, sems):
        self._emit(ins, outs, sems, 0)

    def wait(self, ins, outs, sems):
        self._emit(ins, outs, sems, 1)
        self._emit(ins, outs, sems, 2)


def _call(body, *, name, grid, in_specs, out_specs, out_shape, args, scratch_shapes=(), sem=None, ride=None):
    if ride is None:
        return pl.pallas_call(body, name=name, grid=grid, in_specs=list(in_specs), out_specs=tuple(out_specs),
                              out_shape=tuple(out_shape), scratch_shapes=list(scratch_shapes),
                              compiler_params=_params(sem))(*args)
    n_in, n_out, n_scr, ne, ne_in = len(args), len(out_shape), len(scratch_shapes), ride.n, ride.n_in

    def ride_body(*refs):
        ins, ex_in = refs[:n_in], refs[n_in:n_in + ne_in]
        o0 = n_in + ne_in
        outs, ex_out = refs[o0:o0 + n_out], refs[o0 + n_out:o0 + n_out + ne]
        scr = refs[o0 + n_out + ne:o0 + n_out + ne + n_scr]
        sems = refs[o0 + n_out + ne + n_scr:]
        first = functools.reduce(jnp.logical_and, [pl.program_id(d) == 0 for d in range(len(grid))])
        last = functools.reduce(jnp.logical_and, [pl.program_id(d) == grid[d] - 1 for d in range(len(grid))])

        @pl.when(first)
        def _():
            ride.start(ex_in, ex_out, sems)

        body(*ins, *outs, *scr)

        @pl.when(last)
        def _():
            ride.wait(ex_in, ex_out, sems)

    anyspec = pl.BlockSpec(memory_space=pl.ANY)
    res = pl.pallas_call(
        ride_body, name=name, grid=grid,
        in_specs=list(in_specs) + [anyspec] * ne_in,
        out_specs=tuple(out_specs) + (anyspec,) * ne,
        out_shape=tuple(out_shape) + ride.out_shape(),
        scratch_shapes=list(scratch_shapes) + ride.scratch(),
        compiler_params=_params(("arbitrary",) * len(grid)),
    )(*args, *ride.arrays)
    return tuple(res[:n_out]), tuple(res[n_out:])


def _gather_two_level(arrays, *, name):
    n = len(arrays)

    def body(*refs):
        ins, outs = refs[:n], refs[n:2 * n]
        send_sems, recv_sems, loc_sems = refs[2 * n:]
        x, y, c = lax.axis_index("x"), lax.axis_index("y"), lax.axis_index("c")
        sibling = (x, y, 1 - c)
        chips = [(1 - x, y), (x, 1 - y), (1 - x, 1 - y)]
        idx = lambda px, py, pc: 4 * px + 2 * py + pc
        me = idx(x, y, c)

        def copy(a, k, block, to, src=None):
            return pltpu.make_async_remote_copy(
                src_ref=outs[a].at[block] if src is None else src, dst_ref=outs[a].at[block],
                send_sem=send_sems.at[a, k], recv_sem=recv_sems.at[a, k], device_id=to, device_id_type=MESH_ID)

        local = [pltpu.make_async_copy(ins[a], outs[a].at[me], loc_sems.at[a]) for a in range(n)]
        sent = []
        for a in range(n):
            sent.append(copy(a, 0, me, sibling, src=ins[a]))
            sent += [copy(a, 1 + j, me, (*chip, c), src=ins[a]) for j, chip in enumerate(chips)]
        for cp in local + sent:
            cp.start()
        for j, chip in enumerate(chips):
            for a in range(n):
                copy(a, 1 + j, idx(*chip, c), sibling).wait_recv()
                passed = copy(a, 4 + j, idx(*chip, c), sibling)
                passed.start()
                sent.append(passed)
        for a in range(n):
            copy(a, 0, idx(x, y, 1 - c), sibling).wait_recv()
            for j, chip in enumerate(chips):
                copy(a, 4 + j, idx(*chip, 1 - c), sibling).wait_recv()
        for cp in sent:
            cp.wait_send()
        for cp in local:
            cp.wait()

    anyspec = pl.BlockSpec(memory_space=pl.ANY)
    return pl.pallas_call(
        body, name=name,
        out_shape=tuple(jax.ShapeDtypeStruct((NDEV,) + a.shape, a.dtype) for a in arrays),
        in_specs=[anyspec] * n, out_specs=(anyspec,) * n,
        scratch_shapes=[pltpu.SemaphoreType.DMA((n, NDEV - 1)), pltpu.SemaphoreType.DMA((n, NDEV - 1)),
                        pltpu.SemaphoreType.DMA((n,))],
    )(*arrays)


def _tri_consts():
    r = lax.broadcasted_iota(jnp.int32, (GC, GC), 0)
    c = lax.broadcasted_iota(jnp.int32, (GC, GC), 1)
    return (r >= c).astype(BF), (r <= c).astype(BF)


def _local_step(x, positions, target, w, hooks=None):
    g = {}

    def run(host, fn, *a, **kw):
        h = None if hooks is None else hooks.get(host)
        if h is None:
            return fn(*a, **kw)
        out, received = fn(*a, ride=_Exchange(h[0](w, g)), **kw)
        h[1](received, w, g)
        return out

    nseq, S, _ = x.shape
    T = nseq * S
    tm = min(256, S)
    tq = min(512, S)
    x2 = x.reshape(T, D)
    pos = positions.reshape(T, 1)
    half = ROPE // 2
    inv = THETA ** (-jnp.arange(half, dtype=F32) / half)
    invf = jnp.concatenate([inv, inv, jnp.zeros((64,), F32)]).reshape(1, 128)
    ltri, utri = _tri_consts()

    pg = run("proj_g", _matmul, x2, w["w_gt"], "nt", name="proj_g", tm=1024, tn=640, tk=1024)
    pm = _matmul(x2, w["w_mt"], "nt", name="proj_m", tm=1024, tn=768, tk=1024)
    pt = _matmul(x2, w["w_tt"], "nt", name="proj_t", tm=1024, tn=1024, tk=1024)
    o, zg, states = _gla_fwd(pg, w["wg"], w["bg"], w["gn"], ltri, nseq=nseq, S=S, tm=tm)
    qc, kc, v = _mla_prep_fwd(pm, pos, invf, w["gq"], w["gkv"], w["wuq"], w["wukv"], tm=tm)
    attn, lse = run("flash_fwd", _flash_fwd, qc, kc, v, nseq=nseq, S=S, tq=tq)
    yg, ym, mix, pre1, h1, h1b = _post_attn_fwd(zg, attn, pt, x2, w["wgo"], w["wmo"], w["wout"],
                                                w["g1"], w["b1"], tm=tm)
    ug, uv, ucg, ucv, f_in = _ffn_up_fwd(h1b, w["wug"], w["wuv"], w["cw"], w["cb"], S=S, tm=tm, tn=1408)
    dpre2b, dh1, loss8, dg2, db2 = _down_ln2_loss(f_in, w["wd"], h1, target.reshape(T, D), w["g2"], w["b2"], tm=tm)

    dug, duv, dcg, dcv = _ffn_bwd(dpre2b, w["wd"], ug, uv, ucg, ucv, w["cw"], S=S, tm=tm, tn=1408)
    g["g2"], g["b2"], g["loss"] = dg2, db2, loss8[0:1, 0:1]
    g["cw"] = jnp.concatenate([dcg[0:3], dcv[0:3]], axis=1)
    g["cb"] = jnp.concatenate([dcg[3:4], dcv[3:4]], axis=1)
    g["wd"] = _matmul(f_in, dpre2b, "tn", name="dw_down", out_dtype=BF, tm=1408, tn=1024, tk=1024)
    g["wugt"] = _matmul(dug, h1b, "tn", name="dw_up_g", out_dtype=BF, tm=1408, tn=1024, tk=1024)
    g["wuvt"] = _matmul(duv, h1b, "tn", name="dw_up_v", out_dtype=BF, tm=1408, tn=1024, tk=1024)
    dh1 = _matmul(dug, w["wugt"], "nn", name="dh1_g", c_in=dh1, tm=1024, tn=1024, tk=1408)
    dh1 = _matmul(duv, w["wuvt"], "nn", name="dh1_v", c_in=dh1, tm=1024, tn=1024, tk=1408)
    dx, dpre1b, dpt, dygb, dymb, dzg, dattn, dg1, db1 = _post_attn_bwd(
        dh1, pre1, pt, yg, ym, w["wgo"], w["wmo"], w["wout"], w["g1"], tm=tm)
    g["g1"], g["b1"] = dg1, db1
    g["wout"] = _matmul(mix, dpre1b, "tn", name="dw_out", out_dtype=BF, tm=1024, tn=1024, tk=1024)
    g["wgo"] = _matmul(zg, dygb, "tn", name="dw_gla_o", out_dtype=BF, tm=1024, tn=1024, tk=1024)
    g["wmo"] = _matmul(attn, dymb, "tn", name="dw_mla_o", out_dtype=BF, tm=1024, tn=1024, tk=1024)
    dqc, dkc, dv = run("flash_bwd", _flash_bwd, qc, kc, v, attn, dattn, lse, nseq=nseq, S=S, tq=tq)
    dpm, g["wuq"], g["wukv"], g["gq"], g["gkv"] = _mla_prep_bwd(
        pm, pos, invf, w["gq"], w["gkv"], w["wuq"], w["wukv"], dqc, dkc, dv, tm=tm)
    g["w_mt"] = _matmul(dpm, x2, "tn", name="dw_in_m", out_dtype=BF, tm=768, tn=1024, tk=1024)
    g["w_tt"] = _matmul(dpt, x2, "tn", name="dw_in_t", out_dtype=BF, tm=1024, tn=1024, tk=1024)
    dpg, g["wg"], g["bg"], g["gn"] = run("gla_bwd", _gla_bwd, pg, w["wg"], w["bg"], w["gn"], ltri, utri, o, states,
                                         dzg, nseq=nseq, S=S, tm=tm)
    g["w_gt"] = _matmul(dpg, x2, "tn", name="dw_in_g", out_dtype=BF, tm=640, tn=1024, tk=1024)
    dx = run("dx", _matmul_sum, dx, [(dpg, w["w_gt"], 640), (dpm, w["w_mt"], 768)], name="dx_gm")
    dx = _matmul_sum(dx, [(dpt, w["w_tt"], 1024)], name="dx_t")
    return loss8[0, 0], dx.reshape(nseq, S, D), g


_IN_SPLITS = (512, 512, 1024, 16, 1024, 384, 256, 64, 1024, 1024)


def _w_in_to_groups(wt):
    offs = [0]
    for s in _IN_SPLITS:
        offs.append(offs[-1] + s)
    q, k, v, r, og, cq, ckv, kr, ga, gb = [wt[offs[i]:offs[i + 1]] for i in range(10)]
    z = lambda n: jnp.zeros((n, wt.shape[1]), wt.dtype)
    return (jnp.concatenate([q, k, v, og, r, z(112)], axis=0),
            jnp.concatenate([cq, kr, z(64), ckv], axis=0),
            jnp.concatenate([ga, gb], axis=0))


def _groups_to_w_in(g_g, g_m, g_t):
    q, k, v, og, r = g_g[0:512], g_g[512:1024], g_g[1024:2048], g_g[2048:3072], g_g[3072:3088]
    cq, kr, ckv = g_m[0:384], g_m[384:448], g_m[512:768]
    return jnp.concatenate([q, k, v, r, og, cq, ckv, kr, g_t], axis=0)


_W_IN_LO = 5
_W_IN_SPLIT = _W_IN_LO * 730 - 3472


def _w_in_rows_lo(g_g, g_m):
    q, k, v, og, r = g_g[0:512], g_g[512:1024], g_g[1024:2048], g_g[2048:3072], g_g[3072:3088]
    return jnp.concatenate([q, k, v, r, og, g_m[0:384], g_m[512:768]], axis=0)[:3472 + _W_IN_SPLIT]


def _w_in_rows_hi(g_m, g_t):
    return jnp.concatenate([g_m[512:768], g_m[384:448], g_t], axis=0)[_W_IN_SPLIT:]


def _uq_to_kernel(wuq):
    w3 = wuq.reshape(MQR, MH, NOPE + ROPE)
    rope = jnp.concatenate([w3[:, :, NOPE:], jnp.zeros((MQR, MH, 64), wuq.dtype)], axis=2)
    return jnp.concatenate([w3[:, :, :NOPE].reshape(MQR, MH * 128), rope.reshape(MQR, MH * 128)], axis=1)


def _uq_from_kernel(g):
    nope = g[:, :1024].reshape(MQR, MH, 128)
    rope = g[:, 1024:].reshape(MQR, MH, 128)[:, :, :ROPE]
    return jnp.concatenate([nope, rope], axis=2)


def _ukv_to_kernel(wukv):
    w3 = wukv.reshape(MKR, MH, NOPE + MV)
    return jnp.concatenate([w3[:, :, :NOPE].reshape(MKR, MH * 128), w3[:, :, NOPE:].reshape(MKR, MH * 128)], axis=1)


def _ukv_from_kernel(g):
    return jnp.concatenate([g[:, :1024].reshape(MKR, MH, 128), g[:, 1024:].reshape(MKR, MH, 128)], axis=2)


def _cols_gathered(a):
    return a.transpose(1, 0, 2).reshape(a.shape[1], NDEV * a.shape[2])


def _cols_scattered(a):
    R = a.shape[0]
    return a.reshape(R, NDEV, a.shape[1] // NDEV).transpose(1, 0, 2)


_SMALL = (("gla_b_gate", 512), ("gla_norm_g", 256), ("mla_q_norm_g", 384), ("mla_kv_norm_g", 256),
          ("ln1_g", 1024), ("ln1_b", 1024), ("conv_b", 5632), ("ln2_g", 1024), ("ln2_b", 1024))
_SMALL_ROWS = 88
_SMALL_USED = sum(sz for _, sz in _SMALL)


def _pack_small(d):
    flat = jnp.concatenate([d[n].reshape(-1) for n, _ in _SMALL] + ([d['loss'].reshape(-1)] if 'loss' in d else []))
    return jnp.pad(flat, (0, _SMALL_ROWS * 128 - flat.shape[0])).reshape(_SMALL_ROWS, 128)


def _unpack_small(a):
    flat = a.reshape(-1)
    out, off = {}, 0
    for n, sz in _SMALL:
        out[n] = flat[off:off + sz].reshape(1, sz)
        off += sz
    return out


_NAMES = ['w_in', 'gla_w_gate_up', 'gla_b_gate', 'gla_norm_g', 'w_gla_o', 'mla_q_norm_g', 'mla_w_uq',
          'mla_kv_norm_g', 'mla_w_ukv', 'w_mla_o', 'w_out', 'ln1_g', 'ln1_b', 'w_up', 'conv_w', 'conv_b',
          'w_down', 'ln2_g', 'ln2_b']
_SHARDED = ['w_in', 'w_up', 'w_down', 'w_gla_o', 'w_mla_o', 'w_out', 'mla_w_uq', 'mla_w_ukv', 'gla_w_gate_up',
            'conv_w']


def kernel(x, positions, w_in, gla_w_gate_up, gla_b_gate, gla_norm_g, w_gla_o, mla_q_norm_g, mla_w_uq, mla_kv_norm_g, mla_w_ukv, w_mla_o, w_out, ln1_g, ln1_b, w_up, conv_w, conv_b, w_down, ln2_g, ln2_b, loss_target, m_w_in, m_gla_w_gate_up, m_gla_b_gate, m_gla_norm_g, m_w_gla_o, m_mla_q_norm_g, m_mla_w_uq, m_mla_kv_norm_g, m_mla_w_ukv, m_w_mla_o, m_w_out, m_ln1_g, m_ln1_b, m_w_up, m_conv_w, m_conv_b, m_w_down, m_ln2_g, m_ln2_b, v_w_in, v_gla_w_gate_up, v_gla_b_gate, v_gla_norm_g, v_w_gla_o, v_mla_q_norm_g, v_mla_w_uq, v_mla_kv_norm_g, v_mla_w_ukv, v_w_mla_o, v_w_out, v_ln1_g, v_ln1_b, v_w_up, v_conv_w, v_conv_b, v_w_down, v_ln2_g, v_ln2_b):
    W = dict(w_in=w_in, gla_w_gate_up=gla_w_gate_up, gla_b_gate=gla_b_gate, gla_norm_g=gla_norm_g, w_gla_o=w_gla_o, mla_q_norm_g=mla_q_norm_g, mla_w_uq=mla_w_uq, mla_kv_norm_g=mla_kv_norm_g, mla_w_ukv=mla_w_ukv, w_mla_o=w_mla_o, w_out=w_out, ln1_g=ln1_g, ln1_b=ln1_b, w_up=w_up, conv_w=conv_w, conv_b=conv_b, w_down=w_down, ln2_g=ln2_g, ln2_b=ln2_b)
    M = dict(w_in=m_w_in, gla_w_gate_up=m_gla_w_gate_up, gla_b_gate=m_gla_b_gate, gla_norm_g=m_gla_norm_g, w_gla_o=m_w_gla_o, mla_q_norm_g=m_mla_q_norm_g, mla_w_uq=m_mla_w_uq, mla_kv_norm_g=m_mla_kv_norm_g, mla_w_ukv=m_mla_w_ukv, w_mla_o=m_w_mla_o, w_out=m_w_out, ln1_g=m_ln1_g, ln1_b=m_ln1_b, w_up=m_w_up, conv_w=m_conv_w, conv_b=m_conv_b, w_down=m_w_down, ln2_g=m_ln2_g, ln2_b=m_ln2_b)
    V = dict(w_in=v_w_in, gla_w_gate_up=v_gla_w_gate_up, gla_b_gate=v_gla_b_gate, gla_norm_g=v_gla_norm_g, w_gla_o=v_w_gla_o, mla_q_norm_g=v_mla_q_norm_g, mla_w_uq=v_mla_w_uq, mla_kv_norm_g=v_mla_kv_norm_g, mla_w_ukv=v_mla_w_ukv, w_mla_o=v_w_mla_o, w_out=v_w_out, ln1_g=v_ln1_g, ln1_b=v_ln1_b, w_up=v_w_up, conv_w=v_conv_w, conv_b=v_conv_b, w_down=v_w_down, ln2_g=v_ln2_g, ln2_b=v_ln2_b)

    tshard = lambda d, n: d[n][0].T
    shard = lambda n: (W[n][0].astype(BF), False)
    first = ['w_in', 'mla_w_uq', 'mla_w_ukv', 'gla_w_gate_up']
    G = dict(zip(first, _gather_two_level(
        [tshard(W, 'w_in').astype(BF)] + [shard(n)[0] for n in first[1:]], name="gather_w0")))
    w_gt, w_mt, w_tt = _w_in_to_groups(G['w_in'].reshape(NDEV * 730, D))
    kw = dict(
        w_gt=w_gt, w_mt=w_mt, w_tt=w_tt,
        wg=jnp.pad(_cols_gathered(G['gla_w_gate_up']), ((0, 128 - GR), (0, 0))), bg=W['gla_b_gate'],
        gn=W['gla_norm_g'], gq=W['mla_q_norm_g'], gkv=W['mla_kv_norm_g'],
        wuq=_uq_to_kernel(_cols_gathered(G['mla_w_uq'])), wukv=_ukv_to_kernel(_cols_gathered(G['mla_w_ukv'])),
        g1=W['ln1_g'], b1=W['ln1_b'], g2=W['ln2_g'], b2=W['ln2_b'], cb=W['conv_b'],
    )
    received = {}

    def got_out_proj(ex, w, g):
        w.update(wgo=ex[0].reshape(D, D), wmo=ex[1].reshape(D, D), wout=ex[2].reshape(D, D))

    def got_ffn(ex, w, g):
        w_upt = ex[0].reshape(2 * DFF, D)
        w.update(wugt=w_upt[:DFF], wuvt=w_upt[DFF:], wug=w_upt[:DFF].T, wuv=w_upt[DFF:].T,
                 wd=ex[1].reshape(DFF, D), cw=_cols_gathered(ex[2]))

    slab = lambda a, lo=0: ([(a.astype(BF), lo)], True)
    rows = lambda a, n=NDEV: a.reshape(n, a.shape[0] // n, a.shape[1])

    def keep(names):
        return lambda ex, w, g: received.update(zip(names, ex))

    def small_grads(g):
        return _pack_small(dict(gla_b_gate=g['bg'], gla_norm_g=g['gn'], mla_q_norm_g=g['gq'], mla_kv_norm_g=g['gkv'],
                                ln1_g=g['g1'], ln1_b=g['b1'], conv_b=g['cb'], ln2_g=g['g2'], ln2_b=g['b2'],
                                loss=g['loss']))

    hooks = {
        "proj_g": (lambda w, g: [shard('w_gla_o'), shard('w_mla_o'), shard('w_out')], got_out_proj),
        "flash_fwd": (lambda w, g: [(tshard(W, 'w_up').astype(BF), False), shard('w_down'), (W['conv_w'][0], False)],
                      got_ffn),
        "flash_bwd": (lambda w, g: [slab(rows(g['wd'])),
                                    ([(rows(g['wugt'], 4), 0), (rows(g['wuvt'], 4), 4)], True)],
                      keep(['w_down', 'w_up'])),
        "gla_bwd": (lambda w, g: [slab(rows(g['wout'])), slab(rows(g['wgo'])), slab(rows(g['wmo'])),
                                  slab(_uq_from_kernel(g['wuq']).transpose(1, 0, 2)),
                                  slab(_ukv_from_kernel(g['wukv']).transpose(1, 0, 2)),
                                  slab(rows(_w_in_rows_hi(g['w_mt'], g['w_tt']), NDEV - _W_IN_LO), _W_IN_LO)],
                    keep(['w_out', 'w_gla_o', 'w_mla_o', 'mla_w_uq', 'mla_w_ukv', 'w_in_hi'])),
        "dx": (lambda w, g: [slab(rows(_w_in_rows_lo(g['w_gt'], g['w_mt']), _W_IN_LO)),
                             ([(_cols_scattered(g['wg'][:GR]), 0)], True), ([(_cols_scattered(g['cw']), 0)], True),
                             (small_grads(g), False)],
               keep(['w_in_lo', 'gla_w_gate_up', 'conv_w', 'small'])),
    }

    _, grad_x, _ = _local_step(x, positions, loss_target, kw, hooks)

    grads, deltas, new_m, new_v = {}, {}, {}, {}
    small_parts = received['small']
    loss = jnp.sum(small_parts.reshape(NDEV, -1)[:, _SMALL_USED])
    me = 4 * lax.axis_index("x") + 2 * lax.axis_index("y") + lax.axis_index("c")
    received['w_in'] = jnp.where(me >= _W_IN_LO, received['w_in_hi'], received['w_in_lo'])
    for n in _SHARDED:
        shp = W[n].shape
        if n in ('w_in', 'w_up'):
            out = _adamw(received[n], tshard(W, n), tshard(M, n), tshard(V, n), name="adamw_" + n)
            grads[n], deltas[n], new_m[n], new_v[n] = [t.T.reshape(shp) for t in out]
            continue
        out = _adamw(received[n], W[n][0], M[n][0], V[n][0], name="adamw_" + n)
        grads[n], deltas[n], new_m[n], new_v[n] = [t.reshape(shp) for t in out]
    out = _adamw(small_parts, _pack_small(W), _pack_small(M), _pack_small(V), name="adamw_small")
    for dst, packed in zip((grads, deltas, new_m, new_v), out):
        dst.update(_unpack_small(packed))

    return (loss, grad_x, *[grads[n] for n in _NAMES], *[deltas[n] for n in _NAMES],
            *[new_m[n] for n in _NAMES], *[new_v[n] for n in _NAMES])
```

```python
import functools

import jax
import jax.numpy as jnp
from jax import lax
from jax.experimental import pallas as pl
from jax.experimental.pallas import tpu as pltpu

F32 = jnp.float32
BF = jnp.bfloat16

D = 1024
GH, GDK, GDV, GR, GTAU, GC = 4, 128, 256, 16, 16.0, 64
MH, MQR, MKR, NOPE, ROPE, MV = 8, 384, 256, 128, 64, 128
THETA = 10000.0
DFF = 2816
ALPHA = 2.0 ** 0.25
LN_EPS = 1e-5
RMS_EPS = 1e-6
NDEV = 8
ADAM_LR, ADAM_B1, ADAM_B2, ADAM_EPS, ADAM_WD, ADAM_STEP = 0.001, 0.9, 0.999, 1e-08, 0.01, 10

PG_W = 3200
PM_W = 768
PT_W = 2048
NEG = -1e30
MESH_ID = pl.DeviceIdType.MESH
VMEM_MB = 1024 * 1024


def _params(sem, vmem_mb=48):
    return pltpu.CompilerParams(dimension_semantics=sem, vmem_limit_bytes=vmem_mb * VMEM_MB)


def _dot(a, b):
    return lax.dot_general(a, b, (((1,), (0,)), ((), ())), preferred_element_type=F32)


def _dot_nt(a, b):
    return lax.dot_general(a, b, (((1,), (1,)), ((), ())), preferred_element_type=F32)


def _dot_tn(a, b):
    return lax.dot_general(a, b, (((0,), (0,)), ((), ())), preferred_element_type=F32)


def _iota(shape, dim):
    return lax.broadcasted_iota(jnp.int32, shape, dim)


FLASH_HP = 2
FLASH_HP_FWD = 4
QK_SCALE = (NOPE + ROPE) ** -0.5
LOG2E = 1.4426950408889634
QK_SCALE_LOG2 = QK_SCALE * LOG2E


def _sigmoid(x):
    return 0.5 * jnp.tanh(0.5 * x) + 0.5


def _tri_mm(tri_bf, x):
    hi = x.astype(BF)
    r1 = x - hi.astype(F32)
    mid = r1.astype(BF)
    lo = (r1 - mid.astype(F32)).astype(BF)
    return _dot(tri_bf, hi) + _dot(tri_bf, mid) + _dot(tri_bf, lo)


def _matmul(a, b, mode, *, name, c_in=None, out_dtype=F32, tm=512, tn=512, tk=512, ride=None):
    if mode == "nn":
        (M, K), (_, N) = a.shape, b.shape
    elif mode == "nt":
        (M, K), (N, _) = a.shape, b.shape
    else:
        (K, M), (_, N) = a.shape, b.shape
    tm, tn, tk = min(tm, M), min(tn, N), min(tk, K)
    assert M % tm == 0 and N % tn == 0 and K % tk == 0, (name, M, N, K, tm, tn, tk)
    nk = K // tk
    dot = {"nn": _dot, "nt": _dot_nt, "tn": _dot_tn}[mode]

    def body(*refs):
        if c_in is None:
            a_ref, b_ref, o_ref, acc_ref = refs
        else:
            a_ref, b_ref, c_ref, o_ref, acc_ref = refs
        k = pl.program_id(2)

        @pl.when(k == 0)
        def _():
            if c_in is None:
                acc_ref[...] = jnp.zeros_like(acc_ref)
            else:
                acc_ref[...] = c_ref[...].astype(F32)

        acc_ref[...] += dot(a_ref[...].astype(BF), b_ref[...].astype(BF))

        @pl.when(k == nk - 1)
        def _():
            o_ref[...] = acc_ref[...].astype(out_dtype)

    if mode == "tn":
        a_spec = pl.BlockSpec((tk, tm), lambda i, j, k: (k, i))
    else:
        a_spec = pl.BlockSpec((tm, tk), lambda i, j, k: (i, k))
    if mode == "nt":
        b_spec = pl.BlockSpec((tn, tk), lambda i, j, k: (j, k))
    else:
        b_spec = pl.BlockSpec((tk, tn), lambda i, j, k: (k, j))
    in_specs = [a_spec, b_spec]
    args = [a, b]
    if c_in is not None:
        in_specs.append(pl.BlockSpec((tm, tn), lambda i, j, k: (i, j)))
        args.append(c_in)
    res = _call(
        body, name=name,
        out_shape=(jax.ShapeDtypeStruct((M, N), out_dtype),),
        grid=(M // tm, N // tn, nk),
        in_specs=in_specs,
        out_specs=(pl.BlockSpec((tm, tn), lambda i, j, k: (i, j)),),
        scratch_shapes=[pltpu.VMEM((tm, tn), F32)],
        sem=("parallel", "parallel", "arbitrary"), args=args, ride=ride)
    return res[0] if ride is None else (res[0][0], res[1])


def _matmul_sum(c_in, parts, *, name, tm=1024, ride=None):
    M, N = c_in.shape
    tm = min(tm, M)
    n_p = len(parts)
    counts = [a.shape[1] // tk for a, _, tk in parts]
    starts = [sum(counts[:p]) for p in range(n_p)]
    nk = sum(counts)

    def body(*refs):
        a_refs, w_refs = refs[:n_p], refs[n_p:2 * n_p]
        c_ref, o_ref, acc_ref = refs[2 * n_p:]
        k = pl.program_id(1)

        @pl.when(k == 0)
        def _():
            acc_ref[...] = c_ref[...]

        for p in range(n_p):
            @pl.when(jnp.logical_and(k >= starts[p], k < starts[p] + counts[p]))
            def _(p=p):
                acc_ref[...] += _dot(a_refs[p][...].astype(BF), w_refs[p][...].astype(BF))

        @pl.when(k == nk - 1)
        def _():
            o_ref[...] = acc_ref[...]

    def kidx(p):
        return lambda k: jnp.clip(k - starts[p], 0, counts[p] - 1)

    in_specs = [pl.BlockSpec((tm, tk), lambda i, k, f=kidx(p): (i, f(k))) for p, (_, _, tk) in enumerate(parts)]
    in_specs += [pl.BlockSpec((tk, N), lambda i, k, f=kidx(p): (f(k), 0)) for p, (_, _, tk) in enumerate(parts)]
    in_specs.append(pl.BlockSpec((tm, N), lambda i, k: (i, 0)))
    res = _call(
        body, name=name, out_shape=(jax.ShapeDtypeStruct((M, N), F32),), grid=(M // tm, nk),
        in_specs=in_specs, out_specs=(pl.BlockSpec((tm, N), lambda i, k: (i, 0)),),
        scratch_shapes=[pltpu.VMEM((tm, N), F32)], sem=("parallel", "arbitrary"),
        args=[a for a, _, _ in parts] + [w for _, w, _ in parts] + [c_in], ride=ride)
    return res[0] if ride is None else (res[0][0], res[1])


def _gla_gate(pg_ref, rows, wg_ref, bg_ref):
    r = pg_ref[rows, 3072:3200].astype(BF)
    logit = _dot(r, wg_ref[...]) + bg_ref[...]
    la = (jnp.minimum(logit, 0.0) - jnp.log(1.0 + jnp.exp(-jnp.abs(logit)))) * (1.0 / GTAU)
    return r, logit, la


def _gla_fwd(pg, wg, bg, gn, ltri, *, nseq, S, tm):
    T = pg.shape[0]
    nb, nc = S // tm, tm // GC
    qscale = GDK ** -0.5

    def body(pg_ref, wg_ref, bg_ref, gn_ref, l_ref, o_ref, zg_ref, st_ref, st_scr):
        @pl.when(pl.program_id(1) == 0)
        def _():
            st_scr[...] = jnp.zeros_like(st_scr)

        ltri_v = l_ref[...]
        causal = _iota((GC, GC), 0) >= _iota((GC, GC), 1)
        last_row = _iota((GC, GDK), 0) == GC - 1
        g = gn_ref[...]

        def chunk(c, carry):
            rows = pl.ds(pl.multiple_of(c * GC, GC), GC)
            _, _, la = _gla_gate(pg_ref, rows, wg_ref, bg_ref)
            b = _tri_mm(ltri_v, la)
            hs = range(GH)
            v, q_in, k_st, dec, st, a_raw, o_st, kv = [], [], [], [], [], [], [], []
            for h in hs:
                q = pg_ref[rows, h * GDK:(h + 1) * GDK]
                k = pg_ref[rows, 512 + h * GDK:512 + (h + 1) * GDK]
                v.append(pg_ref[rows, 1024 + h * GDV:1024 + (h + 1) * GDV].astype(BF))
                bh = b[:, h * GDK:(h + 1) * GDK]
                bl = jnp.sum(jnp.where(last_row, bh, 0.0), axis=0, keepdims=True)
                q_in.append((q * (qscale * jnp.exp(bh))).astype(BF))
                k_in = (k * jnp.exp(-bh)).astype(BF)
                k_st.append((k * jnp.exp(bl - bh)).astype(BF))
                dec.append(jnp.exp(bl))
                st.append(st_scr[h])
                st_ref[c, h] = st[h]
                a_raw.append(_dot_nt(q_in[h], k_in))
            for h in hs:
                o_st.append(_dot_nt(q_in[h], st[h].astype(BF)))
                kv.append(_dot_tn(v[h], k_st[h]))
            att = [jnp.where(causal, a_raw[h], 0.0).astype(BF) for h in hs]
            o = [_dot(att[h], v[h]) + o_st[h] for h in hs]
            for h in hs:
                st_scr[h] = st[h] * dec[h] + kv[h]
                og = pg_ref[rows, 2048 + h * GDV:2048 + (h + 1) * GDV]
                rstd = lax.rsqrt(jnp.mean(o[h] * o[h], axis=-1, keepdims=True) + RMS_EPS)
                o_ref[rows, h * GDV:(h + 1) * GDV] = o[h]
                zg_ref[rows, h * GDV:(h + 1) * GDV] = (o[h] * rstd * g * (og * _sigmoid(og))).astype(BF)
            return carry

        lax.fori_loop(0, nc, chunk, 0, unroll=True)

    full = lambda shp: pl.BlockSpec(shp, lambda b_, i: (0,) * len(shp))
    return pl.pallas_call(
        body, name="gla_fwd",
        out_shape=(jax.ShapeDtypeStruct((T, GH * GDV), F32),
                   jax.ShapeDtypeStruct((T, GH * GDV), BF),
                   jax.ShapeDtypeStruct((T // GC, GH, GDV, GDK), F32)),
        grid=(nseq, nb),
        in_specs=[pl.BlockSpec((tm, PG_W), lambda b_, i: (b_ * nb + i, 0)),
                  full((128, 512)), full((1, 512)), full((1, GDV)), full((GC, GC))],
        out_specs=(pl.BlockSpec((tm, GH * GDV), lambda b_, i: (b_ * nb + i, 0)),
                   pl.BlockSpec((tm, GH * GDV), lambda b_, i: (b_ * nb + i, 0)),
                   pl.BlockSpec((nc, GH, GDV, GDK), lambda b_, i: (b_ * nb + i, 0, 0, 0))),
        scratch_shapes=[pltpu.VMEM((GH, GDV, GDK), F32)],
        compiler_params=_params(("parallel", "arbitrary")),
    )(pg, wg, bg, gn, ltri)


def _gla_bwd(pg, wg, bg, gn, ltri, utri, o, states, dzg, *, nseq, S, tm, ride=None):
    T = pg.shape[0]
    nb, nc = S // tm, tm // GC
    qscale = GDK ** -0.5

    def body(pg_ref, wg_ref, bg_ref, gn_ref, l_ref, u_ref, o_ref, st_ref, dzg_ref,
             dpg_ref, dwg_ref, dbg_ref, dgn_ref, dst_scr):
        first = jnp.logical_and(pl.program_id(0) == 0, pl.program_id(1) == 0)

        @pl.when(first)
        def _():
            dwg_ref[...] = jnp.zeros_like(dwg_ref)
            dbg_ref[...] = jnp.zeros_like(dbg_ref)
            dgn_ref[...] = jnp.zeros_like(dgn_ref)

        @pl.when(pl.program_id(1) == 0)
        def _():
            dst_scr[...] = jnp.zeros_like(dst_scr)

        ltri_v = l_ref[...]
        utri_v = u_ref[...]
        causal = _iota((GC, GC), 0) >= _iota((GC, GC), 1)
        last_row = _iota((GC, GDK), 0) == GC - 1
        g = gn_ref[...]

        def chunk(cc, carry):
            c = nc - 1 - cc
            rows = pl.ds(pl.multiple_of(c * GC, GC), GC)
            r, logit, la = _gla_gate(pg_ref, rows, wg_ref, bg_ref)
            b = _tri_mm(ltri_v, la)
            hs = range(GH)
            L = lambda: [None] * GH
            vb, eb, enb, ek, dec, q_in, k_in, k_st, q_inb, k_inb, st, dst, dob = (L() for _ in range(13))
            a_raw, da_raw, dq_st, dks, dv_st, dst_new, dbs, dgn = (L() for _ in range(8))
            for h in hs:
                q = pg_ref[rows, h * GDK:(h + 1) * GDK]
                k = pg_ref[rows, 512 + h * GDK:512 + (h + 1) * GDK]
                vb[h] = pg_ref[rows, 1024 + h * GDV:1024 + (h + 1) * GDV].astype(BF)
                og = pg_ref[rows, 2048 + h * GDV:2048 + (h + 1) * GDV]
                oh = o_ref[rows, h * GDV:(h + 1) * GDV]
                dz = dzg_ref[rows, h * GDV:(h + 1) * GDV].astype(F32)
                bh = b[:, h * GDK:(h + 1) * GDK]
                bl = jnp.sum(jnp.where(last_row, bh, 0.0), axis=0, keepdims=True)
                eb[h] = qscale * jnp.exp(bh)
                enb[h] = jnp.exp(-bh)
                ek[h] = jnp.exp(bl - bh)
                dec[h] = jnp.exp(bl)
                q_in[h], k_in[h], k_st[h] = q * eb[h], k * enb[h], k * ek[h]
                q_inb[h], k_inb[h] = q_in[h].astype(BF), k_in[h].astype(BF)
                st[h] = st_ref[c, h]
                dst[h] = dst_scr[h]
                rstd = lax.rsqrt(jnp.mean(oh * oh, axis=-1, keepdims=True) + RMS_EPS)
                ohat = oh * rstd
                sg = _sigmoid(og)
                don = dz * (og * sg)
                dpg_ref[rows, 2048 + h * GDV:2048 + (h + 1) * GDV] = (
                    dz * (ohat * g) * (sg * (1.0 + og * (1.0 - sg)))).astype(BF)
                dgn[h] = jnp.sum(don * ohat, axis=0, keepdims=True)
                gd = don * g
                dob[h] = (rstd * (gd - ohat * jnp.mean(gd * ohat, axis=-1, keepdims=True))).astype(BF)
                a_raw[h] = _dot_nt(q_inb[h], k_inb[h])
                da_raw[h] = _dot_nt(dob[h], vb[h])
            dgn_ref[...] += dgn[0] + dgn[1] + dgn[2] + dgn[3]
            for h in hs:
                dstb = dst[h].astype(BF)
                dq_st[h] = _dot(dob[h], st[h].astype(BF))
                dks[h] = _dot(vb[h], dstb)
                dv_st[h] = _dot_nt(k_st[h].astype(BF), dstb)
                dst_new[h] = _dot_tn(dob[h], q_inb[h])
            att = [jnp.where(causal, a_raw[h], 0.0).astype(BF) for h in hs]
            da = [jnp.where(causal, da_raw[h], 0.0).astype(BF) for h in hs]
            dqi = [_dot(da[h], k_inb[h]) + dq_st[h] for h in hs]
            dki = [_dot_tn(da[h], q_inb[h]) for h in hs]
            dv = [_dot_tn(att[h], dob[h]) + dv_st[h] for h in hs]
            for h in hs:
                dd = jnp.sum(dst[h] * st[h], axis=0, keepdims=True)
                dst_scr[h] = dst[h] * dec[h] + dst_new[h]
                kk = dks[h] * k_st[h]
                dbl = jnp.sum(kk, axis=0, keepdims=True) + dd * dec[h]
                db = dqi[h] * q_in[h] - dki[h] * k_in[h] - kk
                dbs[h] = db + jnp.where(last_row, dbl, 0.0)
                dpg_ref[rows, h * GDK:(h + 1) * GDK] = (dqi[h] * eb[h]).astype(BF)
                dpg_ref[rows, 512 + h * GDK:512 + (h + 1) * GDK] = (dki[h] * enb[h] + dks[h] * ek[h]).astype(BF)
                dpg_ref[rows, 1024 + h * GDV:1024 + (h + 1) * GDV] = dv[h].astype(BF)
            dla = _tri_mm(utri_v, jnp.concatenate(dbs, axis=1))
            dlogit = dla * (1.0 / GTAU) * _sigmoid(-logit)
            dlb = dlogit.astype(BF)
            dpg_ref[rows, 3072:3200] = _dot_nt(dlb, wg_ref[...]).astype(BF)
            dwg_ref[...] += _dot_tn(r, dlb)
            dbg_ref[...] += jnp.sum(dlogit, axis=0, keepdims=True)
            return carry

        lax.fori_loop(0, nc, chunk, 0, unroll=True)

    full = lambda shp: pl.BlockSpec(shp, lambda b_, i: (0,) * len(shp))
    rev = lambda b_, i: (b_ * nb + nb - 1 - i, 0)
    return _call(
        body, name="gla_bwd", ride=ride, sem=("arbitrary", "arbitrary"),
        args=(pg, wg, bg, gn, ltri, utri, o, states, dzg),
        out_shape=(jax.ShapeDtypeStruct((T, PG_W), BF),
                   jax.ShapeDtypeStruct((128, 512), F32),
                   jax.ShapeDtypeStruct((1, 512), F32),
                   jax.ShapeDtypeStruct((1, GDV), F32)),
        grid=(nseq, nb),
        in_specs=[pl.BlockSpec((tm, PG_W), rev),
                  full((128, 512)), full((1, 512)), full((1, GDV)), full((GC, GC)), full((GC, GC)),
                  pl.BlockSpec((tm, GH * GDV), rev),
                  pl.BlockSpec((nc, GH, GDV, GDK), lambda b_, i: (b_ * nb + nb - 1 - i, 0, 0, 0)),
                  pl.BlockSpec((tm, GH * GDV), rev)],
        out_specs=(pl.BlockSpec((tm, PG_W), rev), full((128, 512)), full((1, 512)), full((1, GDV))),
        scratch_shapes=[pltpu.VMEM((GH, GDV, GDK), F32)])


def _rope_tables(pos, invf):
    ang = pos.astype(F32) * invf
    lane = _iota(ang.shape, 1)
    sin = jnp.sin(ang)
    ssin = jnp.where(lane < 32, -sin, jnp.where(lane < 64, sin, 0.0))
    return jnp.cos(ang), ssin, lane


def _rope(x, cos, ssin, lane, sign):
    rot = jnp.where(lane < 32, pltpu.roll(x, 96, 1), pltpu.roll(x, 32, 1))
    return x * cos + sign * (rot * ssin)


def _rms_fwd(x, g):
    rstd = lax.rsqrt(jnp.mean(x * x, axis=-1, keepdims=True) + RMS_EPS)
    return x * rstd * g, x * rstd, rstd


def _rms_bwd(dy, xhat, rstd, g):
    gd = dy * g
    return rstd * (gd - xhat * jnp.mean(gd * xhat, axis=-1, keepdims=True)), jnp.sum(dy * xhat, axis=0, keepdims=True)


def _mla_prep_fwd(pm, pos, invf, gq, gkv, wuq, wukv, *, tm):
    T = pm.shape[0]

    def body(pm_ref, pos_ref, invf_ref, gq_ref, gkv_ref, wuq_ref, wukv_ref, qc_ref, kc_ref, v_ref):
        cos, ssin, lane = _rope_tables(pos_ref[...], invf_ref[...])
        cq, _, _ = _rms_fwd(pm_ref[:, 0:MQR], gq_ref[...])
        ckv, _, _ = _rms_fwd(pm_ref[:, 512:768], gkv_ref[...])
        qf = _dot(cq.astype(BF), wuq_ref[...])
        kvf = _dot(ckv.astype(BF), wukv_ref[...])
        kr = _rope(pm_ref[:, 384:512], cos, ssin, lane, 1.0).astype(BF)
        for h in range(MH):
            qc_ref[:, 256 * h:256 * h + 128] = (QK_SCALE_LOG2 * qf[:, 128 * h:128 * h + 128]).astype(BF)
            qr = qf[:, 1024 + 128 * h:1024 + 128 * h + 128]
            qc_ref[:, 256 * h + 128:256 * h + 256] = (QK_SCALE_LOG2 * _rope(qr, cos, ssin, lane, 1.0)).astype(BF)
            kc_ref[:, 256 * h:256 * h + 128] = kvf[:, 128 * h:128 * h + 128].astype(BF)
            kc_ref[:, 256 * h + 128:256 * h + 256] = kr
        v_ref[...] = kvf[:, 1024:2048].astype(BF)

    full = lambda shp: pl.BlockSpec(shp, lambda i: (0,) * len(shp))
    row = lambda w: pl.BlockSpec((tm, w), lambda i: (i, 0))
    return pl.pallas_call(
        body, name="mla_prep_fwd",
        out_shape=(jax.ShapeDtypeStruct((T, MH * 256), BF), jax.ShapeDtypeStruct((T, MH * 256), BF),
                   jax.ShapeDtypeStruct((T, MH * MV), BF)),
        grid=(T // tm,),
        in_specs=[row(PM_W), row(1), full((1, 128)), full((1, MQR)), full((1, MKR)),
                  full((MQR, 2048)), full((MKR, 2048))],
        out_specs=(row(MH * 256), row(MH * 256), row(MH * MV)),
        compiler_params=_params(("parallel",)),
    )(pm, pos, invf, gq, gkv, wuq, wukv)


def _mla_prep_bwd(pm, pos, invf, gq, gkv, wuq, wukv, dqc, dkc, dv, *, tm):
    T = pm.shape[0]

    def body(pm_ref, pos_ref, invf_ref, gq_ref, gkv_ref, wuq_ref, wukv_ref, dqc_ref, dkc_ref, dv_ref,
             dpm_ref, dwuq_ref, dwukv_ref, dgq_ref, dgkv_ref):
        @pl.when(pl.program_id(0) == 0)
        def _():
            dwuq_ref[...] = jnp.zeros_like(dwuq_ref)
            dwukv_ref[...] = jnp.zeros_like(dwukv_ref)
            dgq_ref[...] = jnp.zeros_like(dgq_ref)
            dgkv_ref[...] = jnp.zeros_like(dgkv_ref)

        cos, ssin, lane = _rope_tables(pos_ref[...], invf_ref[...])
        cq, cqh, cq_rstd = _rms_fwd(pm_ref[:, 0:MQR], gq_ref[...])
        ckv, ckvh, ckv_rstd = _rms_fwd(pm_ref[:, 512:768], gkv_ref[...])
        dqn, dqr, dkn = [], [], []
        dkr = jnp.zeros((tm, 128), F32)
        for h in range(MH):
            dqn.append(dqc_ref[:, 256 * h:256 * h + 128].astype(BF))
            dqr.append(_rope(dqc_ref[:, 256 * h + 128:256 * h + 256], cos, ssin, lane, -1.0).astype(BF))
            dkn.append(dkc_ref[:, 256 * h:256 * h + 128].astype(BF))
            dkr = dkr + dkc_ref[:, 256 * h + 128:256 * h + 256]
        dqf = jnp.concatenate(dqn + dqr, axis=1)
        dkvf = jnp.concatenate(dkn + [dv_ref[...].astype(BF)], axis=1)
        dwuq_ref[...] += _dot_tn(cq.astype(BF), dqf)
        dwukv_ref[...] += _dot_tn(ckv.astype(BF), dkvf)
        dcq, dgq = _rms_bwd(_dot_nt(dqf, wuq_ref[...]), cqh, cq_rstd, gq_ref[...])
        dckv, dgkv = _rms_bwd(_dot_nt(dkvf, wukv_ref[...]), ckvh, ckv_rstd, gkv_ref[...])
        dgq_ref[...] += dgq
        dgkv_ref[...] += dgkv
        dpm_ref[:, 0:MQR] = dcq.astype(BF)
        dpm_ref[:, 384:512] = _rope(dkr, cos, ssin, lane, -1.0).astype(BF)
        dpm_ref[:, 512:768] = dckv.astype(BF)

    full = lambda shp: pl.BlockSpec(shp, lambda i: (0,) * len(shp))
    row = lambda w: pl.BlockSpec((tm, w), lambda i: (i, 0))
    return pl.pallas_call(
        body, name="mla_prep_bwd",
        out_shape=(jax.ShapeDtypeStruct((T, PM_W), BF), jax.ShapeDtypeStruct((MQR, 2048), F32),
                   jax.ShapeDtypeStruct((MKR, 2048), F32), jax.ShapeDtypeStruct((1, MQR), F32),
                   jax.ShapeDtypeStruct((1, MKR), F32)),
        grid=(T // tm,),
        in_specs=[row(PM_W), row(1), full((1, 128)), full((1, MQR)), full((1, MKR)),
                  full((MQR, 2048)), full((MKR, 2048)), row(MH * 256), row(MH * 256), row(MH * MV)],
        out_specs=(row(PM_W), full((MQR, 2048)), full((MKR, 2048)), full((1, MQR)), full((1, MKR))),
        compiler_params=_params(("arbitrary",)),
    )(pm, pos, invf, gq, gkv, wuq, wukv, dqc, dkc, dv)


def _flash_fwd(qc, kc, v, *, nseq, S, tq, ride=None):
    T = qc.shape[0]
    nq = S // tq
    hp = FLASH_HP_FWD

    def body(q_ref, k_ref, v_ref, o_ref, lse_ref):
        i = pl.program_id(2)
        causal = _iota((tq, tq), 0) >= _iota((tq, tq), 1)

        def step(j, carry, masked):
            rows = pl.ds(pl.multiple_of(j * tq, tq), tq)
            hs = range(hp)
            s = [_dot_nt(q_ref[:, 256 * hh:256 * hh + 256], k_ref[rows, 256 * hh:256 * hh + 256]) for hh in hs]
            p, stats = [], []
            for hh in hs:
                m, l, _ = carry[hh]
                sh = jnp.where(causal, s[hh], NEG) if masked else s[hh]
                m_new = jnp.maximum(m, jnp.max(sh, axis=-1, keepdims=True))
                ph = jnp.exp2(sh - m_new)
                a = jnp.exp2(m - m_new)
                stats.append((m_new, a * l + jnp.sum(ph, axis=-1, keepdims=True), a))
                p.append(ph.astype(BF))
            pv = [_dot(p[hh], v_ref[rows, MV * hh:MV * hh + MV]) for hh in hs]
            return tuple((stats[hh][0], stats[hh][1], stats[hh][2] * carry[hh][2] + pv[hh]) for hh in hs)

        init = ((jnp.full((tq, 1), NEG, F32), jnp.zeros((tq, 1), F32), jnp.zeros((tq, MV), F32)),) * hp
        carry = lax.fori_loop(0, i, lambda j, c: step(j, c, False), init)
        for hh, (m, l, acc) in enumerate(step(i, carry, True)):
            o_ref[:, MV * hh:MV * hh + MV] = (acc / l).astype(BF)
            lse_ref[:, 128 * hh:128 * hh + 128] = jnp.broadcast_to(m + jnp.log2(l), (tq, 128))

    return _call(
        body, name="flash_fwd", ride=ride, sem=("parallel", "parallel", "arbitrary"), args=(qc, kc, v),
        out_shape=(jax.ShapeDtypeStruct((T, MH * MV), BF), jax.ShapeDtypeStruct((T, MH * 128), F32)),
        grid=(nseq, MH // hp, nq),
        in_specs=[pl.BlockSpec((tq, 256 * hp), lambda b_, h, i: (b_ * nq + i, h)),
                  pl.BlockSpec((S, 256 * hp), lambda b_, h, i: (b_, h)),
                  pl.BlockSpec((S, MV * hp), lambda b_, h, i: (b_, h))],
        out_specs=(pl.BlockSpec((tq, MV * hp), lambda b_, h, i: (b_ * nq + i, h)),
                   pl.BlockSpec((tq, 128 * hp), lambda b_, h, i: (b_ * nq + i, h))))


def _flash_bwd(qc, kc, v, o, do, lse, *, nseq, S, tq, ride=None):
    T = qc.shape[0]
    nq = S // tq

    def body(q_ref, k_ref, v_ref, o_ref, do_ref, lse_ref, dq_ref, dk_ref, dv_ref, dq_scr, delta_scr):
        j = pl.program_id(2)

        @pl.when(j == 0)
        def _():
            dq_scr[...] = jnp.zeros_like(dq_scr)
            for hh in range(FLASH_HP):
                od = o_ref[:, MV * hh:MV * hh + MV].astype(F32) * do_ref[:, MV * hh:MV * hh + MV].astype(F32)
                delta_scr[:, 128 * hh:128 * hh + 128] = jnp.broadcast_to(jnp.sum(od, axis=-1, keepdims=True), (S, 128))

        causal = _iota((tq, tq), 0) >= _iota((tq, tq), 1)

        def step(i, carry, masked):
            rows = pl.ds(pl.multiple_of(i * tq, tq), tq)
            hs = range(FLASH_HP)
            qs = [slice(256 * hh, 256 * hh + 256) for hh in hs]
            vs = [slice(MV * hh, MV * hh + MV) for hh in hs]
            ls = [slice(128 * hh, 128 * hh + 1) for hh in hs]
            s = [_dot_nt(q_ref[rows, qs[hh]], k_ref[:, qs[hh]]) for hh in hs]
            dp = [_dot_nt(do_ref[rows, vs[hh]], v_ref[:, vs[hh]]) for hh in hs]
            pb, ds = [], []
            for hh in hs:
                p = jnp.exp2(s[hh] - lse_ref[rows, ls[hh]])
                if masked:
                    p = jnp.where(causal, p, 0.0)
                pb.append(p.astype(BF))
                ds.append((p * (dp[hh] - delta_scr[rows, ls[hh]])).astype(BF))
            dv = [carry[hh][1] + _dot_tn(pb[hh], do_ref[rows, vs[hh]]) for hh in hs]
            dk = [carry[hh][0] + _dot_tn(ds[hh], q_ref[rows, qs[hh]]) for hh in hs]
            for hh in hs:
                dq_scr[rows, qs[hh]] += _dot(ds[hh], k_ref[:, qs[hh]])
            return tuple((dk[hh], dv[hh]) for hh in hs)

        init = ((jnp.zeros((tq, 256), F32), jnp.zeros((tq, MV), F32)),) * FLASH_HP
        carry = step(j, init, True)
        carry = lax.fori_loop(j + 1, nq, lambda i, c: step(i, c, False), carry)
        for hh, (dk, dv) in enumerate(carry):
            dk_ref[:, 256 * hh:256 * hh + 256] = dk * (1.0 / LOG2E)
            dv_ref[:, MV * hh:MV * hh + MV] = dv

        @pl.when(j == nq - 1)
        def _():
            dq_ref[...] = dq_scr[...] * QK_SCALE

    hp = FLASH_HP
    seq = lambda w: pl.BlockSpec((S, w * hp), lambda b_, h, j: (b_, h))
    blk = lambda w: pl.BlockSpec((tq, w * hp), lambda b_, h, j: (b_ * nq + j, h))
    return _call(
        body, name="flash_bwd", ride=ride, sem=("parallel", "parallel", "arbitrary"), args=(qc, kc, v, o, do, lse),
        out_shape=(jax.ShapeDtypeStruct((T, MH * 256), F32), jax.ShapeDtypeStruct((T, MH * 256), F32),
                   jax.ShapeDtypeStruct((T, MH * MV), F32)),
        grid=(nseq, MH // hp, nq),
        in_specs=[seq(256), blk(256), blk(MV), seq(MV), seq(MV), seq(128)],
        out_specs=(seq(256), blk(256), blk(MV)),
        scratch_shapes=[pltpu.VMEM((S, 256 * hp), F32), pltpu.VMEM((S, 128 * hp), F32)])


def _ln_fwd(pre, g, b):
    mu = jnp.mean(pre, axis=-1, keepdims=True)
    xc = pre - mu
    rstd = lax.rsqrt(jnp.mean(xc * xc, axis=-1, keepdims=True) + LN_EPS)
    xhat = xc * rstd
    return xhat * g + b, xhat, rstd


def _ln_bwd(dy, xhat, rstd, g):
    dxh = dy * g
    dx = rstd * (dxh - jnp.mean(dxh, axis=-1, keepdims=True) - xhat * jnp.mean(dxh * xhat, axis=-1, keepdims=True))
    return dx, jnp.sum(dy * xhat, axis=0, keepdims=True), jnp.sum(dy, axis=0, keepdims=True)


def _post_attn_fwd(zg, attn, pt, x, wgo, wmo, wout, g1, b1, *, tm):
    T = x.shape[0]

    def body(zg_ref, at_ref, pt_ref, x_ref, wgo_ref, wmo_ref, wout_ref, g_ref, b_ref,
             yg_ref, ym_ref, mix_ref, pre_ref, h_ref, hb_ref):
        yg = _dot(zg_ref[...], wgo_ref[...])
        ym = _dot(at_ref[...], wmo_ref[...])
        mix = (_sigmoid(pt_ref[:, 0:D]) * yg + _sigmoid(pt_ref[:, D:2 * D]) * ym).astype(BF)
        pre = ALPHA * x_ref[...] + _dot(mix, wout_ref[...])
        h, _, _ = _ln_fwd(pre, g_ref[...], b_ref[...])
        yg_ref[...] = yg.astype(BF)
        ym_ref[...] = ym.astype(BF)
        mix_ref[...] = mix
        pre_ref[...] = pre
        h_ref[...] = h
        hb_ref[...] = h.astype(BF)

    full = lambda shp: pl.BlockSpec(shp, lambda i: (0,) * len(shp))
    row = lambda w: pl.BlockSpec((tm, w), lambda i: (i, 0))
    sd = lambda dt: jax.ShapeDtypeStruct((T, D), dt)
    return pl.pallas_call(
        body, name="post_attn_fwd",
        out_shape=(sd(BF), sd(BF), sd(BF), sd(F32), sd(F32), sd(BF)),
        grid=(T // tm,),
        in_specs=[row(D), row(D), row(PT_W), row(D), full((D, D)), full((D, D)), full((D, D)),
                  full((1, D)), full((1, D))],
        out_specs=(row(D),) * 6,
        compiler_params=_params(("parallel",)),
    )(zg, attn, pt, x, wgo, wmo, wout, g1, b1)


def _post_attn_bwd(dh, pre, pt, yg, ym, wgo, wmo, wout, g1, *, tm):
    T = dh.shape[0]

    def body(dh_ref, pre_ref, pt_ref, yg_ref, ym_ref, wgo_ref, wmo_ref, wout_ref, g_ref,
             dx_ref, dpreb_ref, dpt_ref, dygb_ref, dymb_ref, dzg_ref, dat_ref, dg_ref, db_ref):
        @pl.when(pl.program_id(0) == 0)
        def _():
            dg_ref[...] = jnp.zeros_like(dg_ref)
            db_ref[...] = jnp.zeros_like(db_ref)

        pre = pre_ref[...]
        mu = jnp.mean(pre, axis=-1, keepdims=True)
        xc = pre - mu
        rstd = lax.rsqrt(jnp.mean(xc * xc, axis=-1, keepdims=True) + LN_EPS)
        dpre, dg, db = _ln_bwd(dh_ref[...], xc * rstd, rstd, g_ref[...])
        dg_ref[...] += dg
        db_ref[...] += db
        dx_ref[...] = ALPHA * dpre
        dpreb = dpre.astype(BF)
        dpreb_ref[...] = dpreb
        dmix = _dot_nt(dpreb, wout_ref[...])
        sa = _sigmoid(pt_ref[:, 0:D])
        sb = _sigmoid(pt_ref[:, D:2 * D])
        dpt_ref[:, 0:D] = (dmix * yg_ref[...].astype(F32) * (sa * (1.0 - sa))).astype(BF)
        dpt_ref[:, D:2 * D] = (dmix * ym_ref[...].astype(F32) * (sb * (1.0 - sb))).astype(BF)
        dyg = (dmix * sa).astype(BF)
        dym = (dmix * sb).astype(BF)
        dygb_ref[...] = dyg
        dymb_ref[...] = dym
        dzg_ref[...] = _dot_nt(dyg, wgo_ref[...]).astype(BF)
        dat_ref[...] = _dot_nt(dym, wmo_ref[...]).astype(BF)

    full = lambda shp: pl.BlockSpec(shp, lambda i: (0,) * len(shp))
    row = lambda w: pl.BlockSpec((tm, w), lambda i: (i, 0))
    sd = lambda w, dt: jax.ShapeDtypeStruct((T, w), dt)
    return pl.pallas_call(
        body, name="post_attn_bwd",
        out_shape=(sd(D, F32), sd(D, BF), sd(PT_W, BF), sd(D, BF), sd(D, BF), sd(D, BF), sd(D, BF),
                   jax.ShapeDtypeStruct((1, D), F32), jax.ShapeDtypeStruct((1, D), F32)),
        grid=(T // tm,),
        in_specs=[row(D), row(D), row(PT_W), row(D), row(D), full((D, D)), full((D, D)), full((D, D)),
                  full((1, D))],
        out_specs=(row(D), row(D), row(PT_W), row(D), row(D), row(D), row(D), full((1, D)), full((1, D))),
        compiler_params=_params(("arbitrary",)),
    )(dh, pre, pt, yg, ym, wgo, wmo, wout, g1)


def _shift_down(u, prev, k):
    r = pltpu.roll(u, k, 0)
    p = pltpu.roll(prev, k, 0)
    head = jnp.where(_iota(p.shape, 0) < k, p, r[0:8, :])
    return jnp.concatenate([head, r[8:, :]], axis=0)


def _shift_up(u, nxt, k):
    n = u.shape[0]
    r = pltpu.roll(u, n - k, 0)
    p = pltpu.roll(nxt, 8 - k, 0)
    tail = jnp.where(_iota(p.shape, 0) >= 8 - k, p, r[n - 8:, :])
    return jnp.concatenate([r[:n - 8, :], tail], axis=0)


def _conv3(u, prev, w_ref, b_ref):
    return (w_ref[0:1, :] * _shift_down(u, prev, 2) + w_ref[1:2, :] * _shift_down(u, prev, 1)
            + w_ref[2:3, :] * u + b_ref[...])


def _ffn_up_fwd(hb, wug, wuv, cw, cb, *, S, tm, tn):
    T = hb.shape[0]
    nj, nbs = DFF // tn, S // tm

    def body(h_ref, wg_ref, wv_ref, cwg_ref, cwv_ref, cbg_ref, cbv_ref,
             ug_ref, uv_ref, ucg_ref, ucv_ref, f_ref, pg_scr, pv_scr):
        @pl.when(pl.program_id(1) % nbs == 0)
        def _():
            pg_scr[...] = jnp.zeros_like(pg_scr)
            pv_scr[...] = jnp.zeros_like(pv_scr)

        h = h_ref[...]
        ug = _dot(h, wg_ref[...])
        uv = _dot(h, wv_ref[...])
        ucg = _conv3(ug, pg_scr[...], cwg_ref, cbg_ref)
        ucv = _conv3(uv, pv_scr[...], cwv_ref, cbv_ref)
        pg_scr[...] = ug[tm - 8:, :]
        pv_scr[...] = uv[tm - 8:, :]
        ug_ref[...] = ug.astype(BF)
        uv_ref[...] = uv.astype(BF)
        ucg_ref[...] = ucg
        ucv_ref[...] = ucv
        f_ref[...] = (ucg * _sigmoid(ucg) * ucv).astype(BF)

    tile = pl.BlockSpec((tm, tn), lambda j, i: (i, j))
    return pl.pallas_call(
        body, name="ffn_up_fwd",
        out_shape=(jax.ShapeDtypeStruct((T, DFF), BF), jax.ShapeDtypeStruct((T, DFF), BF),
                   jax.ShapeDtypeStruct((T, DFF), F32), jax.ShapeDtypeStruct((T, DFF), F32),
                   jax.ShapeDtypeStruct((T, DFF), BF)),
        grid=(nj, T // tm),
        in_specs=[pl.BlockSpec((tm, D), lambda j, i: (i, 0)),
                  pl.BlockSpec((D, tn), lambda j, i: (0, j)), pl.BlockSpec((D, tn), lambda j, i: (0, j)),
                  pl.BlockSpec((3, tn), lambda j, i: (0, j)), pl.BlockSpec((3, tn), lambda j, i: (0, j + nj)),
                  pl.BlockSpec((1, tn), lambda j, i: (0, j)), pl.BlockSpec((1, tn), lambda j, i: (0, j + nj))],
        out_specs=(tile, tile, tile, tile, tile),
        scratch_shapes=[pltpu.VMEM((8, tn), F32), pltpu.VMEM((8, tn), F32)],
        compiler_params=_params(("parallel", "arbitrary")),
    )(hb, wug, wuv, cw, cw, cb, cb)


def _ffn_bwd(dpreb, wd, ug, uv, ucg, ucv, cw, *, S, tm, tn):
    T = dpreb.shape[0]
    nj, nb, nbs = DFF // tn, T // tm, S // tm

    def body(dp_ref, wd_ref, ug_ref, uv_ref, ucg_ref, ucv_ref, cwg_ref, cwv_ref,
             dug_ref, duv_ref, dcg_ref, dcv_ref, ng_scr, nv_scr):
        ii = pl.program_id(1)
        i = nb - 1 - ii

        @pl.when(ii == 0)
        def _():
            dcg_ref[...] = jnp.zeros_like(dcg_ref)
            dcv_ref[...] = jnp.zeros_like(dcv_ref)

        @pl.when(i % nbs == nbs - 1)
        def _():
            ng_scr[...] = jnp.zeros_like(ng_scr)
            nv_scr[...] = jnp.zeros_like(nv_scr)

        df = _dot_nt(dp_ref[...], wd_ref[...])
        ucg = ucg_ref[...]
        sg = _sigmoid(ucg)
        ducg = df * ucv_ref[...] * (sg * (1.0 + ucg * (1.0 - sg)))
        ducv = df * (ucg * sg)

        def finish(duc, u_ref, w, nxt_scr, du_ref, dc_ref):
            nxt = nxt_scr[...]
            up1 = _shift_up(duc, nxt, 1)
            up2 = _shift_up(duc, nxt, 2)
            du_ref[...] = (w[2:3, :] * duc + w[1:2, :] * up1 + w[0:1, :] * up2).astype(BF)
            nxt_scr[...] = duc[0:8, :]
            u = u_ref[...].astype(F32)
            for row, z in enumerate((u * up2, u * up1, u * duc, duc)):
                dc_ref[row:row + 1, :] += jnp.sum(z, axis=0, keepdims=True)

        finish(ducg, ug_ref, cwg_ref, ng_scr, dug_ref, dcg_ref)
        finish(ducv, uv_ref, cwv_ref, nv_scr, duv_ref, dcv_ref)

    tile = pl.BlockSpec((tm, tn), lambda j, ii: (nb - 1 - ii, j))
    acc = pl.BlockSpec((8, tn), lambda j, ii: (0, j))
    return pl.pallas_call(
        body, name="ffn_bwd",
        out_shape=(jax.ShapeDtypeStruct((T, DFF), BF), jax.ShapeDtypeStruct((T, DFF), BF),
                   jax.ShapeDtypeStruct((8, DFF), F32), jax.ShapeDtypeStruct((8, DFF), F32)),
        grid=(nj, nb),
        in_specs=[pl.BlockSpec((tm, D), lambda j, ii: (nb - 1 - ii, 0)),
                  pl.BlockSpec((tn, D), lambda j, ii: (j, 0)),
                  tile, tile, tile, tile,
                  pl.BlockSpec((3, tn), lambda j, ii: (0, j)), pl.BlockSpec((3, tn), lambda j, ii: (0, j + nj))],
        out_specs=(tile, tile, acc, acc),
        scratch_shapes=[pltpu.VMEM((8, tn), F32), pltpu.VMEM((8, tn), F32)],
        compiler_params=_params(("parallel", "arbitrary")),
    )(dpreb, wd, ug, uv, ucg, ucv, cw, cw)


def _down_ln2_loss(f_in, wd, h, target, g2, b2, *, tm):
    T = h.shape[0]

    def body(f_ref, wd_ref, h_ref, t_ref, g_ref, b_ref, dpb_ref, dh_ref, loss_ref, dg_ref, db_ref):
        @pl.when(pl.program_id(0) == 0)
        def _():
            loss_ref[...] = jnp.zeros_like(loss_ref)
            dg_ref[...] = jnp.zeros_like(dg_ref)
            db_ref[...] = jnp.zeros_like(db_ref)

        pre = ALPHA * h_ref[...] + _dot(f_ref[...], wd_ref[...])
        out, xhat, rstd = _ln_fwd(pre, g_ref[...], b_ref[...])
        diff = out - t_ref[...]
        loss_ref[...] += 0.5 * jnp.sum(jnp.mean(diff * diff, axis=-1, keepdims=True))
        dpre, dg, db = _ln_bwd(diff * (1.0 / D), xhat, rstd, g_ref[...])
        dg_ref[...] += dg
        db_ref[...] += db
        dpb_ref[...] = dpre.astype(BF)
        dh_ref[...] = ALPHA * dpre

    full = lambda shp: pl.BlockSpec(shp, lambda i: (0,) * len(shp))
    row = lambda w: pl.BlockSpec((tm, w), lambda i: (i, 0))
    return pl.pallas_call(
        body, name="down_ln2_loss",
        out_shape=(jax.ShapeDtypeStruct((T, D), BF), jax.ShapeDtypeStruct((T, D), F32),
                   jax.ShapeDtypeStruct((8, 128), F32), jax.ShapeDtypeStruct((1, D), F32),
                   jax.ShapeDtypeStruct((1, D), F32)),
        grid=(T // tm,),
        in_specs=[row(DFF), full((DFF, D)), row(D), row(D), full((1, D)), full((1, D))],
        out_specs=(row(D), row(D), full((8, 128)), full((1, D)), full((1, D))),
        compiler_params=_params(("arbitrary",)),
    )(f_in, wd, h, target, g2, b2)


def _adamw(parts, w, m, v, *, name):
    n, R, C = parts.shape
    tr, tc = R, C
    for cand in range(min(R, 256), 15, -1):
        if R % cand == 0 and cand % 16 == 0:
            tr = cand
            break
    if tr == R and R * C > 65536 and C % 256 == 0:
        tc = 256
    c1 = 1.0 - ADAM_B1 ** ADAM_STEP
    c2 = 1.0 - ADAM_B2 ** ADAM_STEP

    def body(p_ref, w_ref, m_ref, v_ref, g_ref, d_ref, nm_ref, nv_ref):
        g = p_ref[0].astype(F32)
        for s in range(1, n):
            g = g + p_ref[s].astype(F32)
        nm = ADAM_B1 * m_ref[...] + (1.0 - ADAM_B1) * g
        nv = ADAM_B2 * v_ref[...] + (1.0 - ADAM_B2) * (g * g)
        g_ref[...] = g
        nm_ref[...] = nm
        nv_ref[...] = nv
        d_ref[...] = -ADAM_LR * ((nm / c1) / (jnp.sqrt(nv / c2) + ADAM_EPS) + ADAM_WD * w_ref[...])

    blk = pl.BlockSpec((tr, tc), lambda i, j: (i, j))
    sd = jax.ShapeDtypeStruct((R, C), F32)
    return pl.pallas_call(
        body, name=name,
        out_shape=(sd, sd, sd, sd),
        grid=(R // tr, C // tc),
        in_specs=[pl.BlockSpec((n, tr, tc), lambda i, j: (0, i, j)), blk, blk, blk],
        out_specs=(blk, blk, blk, blk),
        compiler_params=_params(("parallel", "parallel")),
    )(parts, w, m, v)


class _Exchange:
    def __init__(self, items):
        self.items = [(src if sc else [(src, 0)], sc) for src, sc in items]
        self.arrays = [arr for srcs, _ in self.items for arr, _ in srcs]
        self.n = len(self.items)
        self.n_in = len(self.arrays)

    def out_shape(self):
        return tuple(jax.ShapeDtypeStruct((NDEV,) + (srcs[0][0].shape[1:] if sc else srcs[0][0].shape),
                                          srcs[0][0].dtype) for srcs, sc in self.items)

    def scratch(self):
        return [pltpu.SemaphoreType.DMA((self.n, NDEV - 1)), pltpu.SemaphoreType.DMA((self.n, NDEV - 1)),
                pltpu.SemaphoreType.DMA((self.n,))]

    def _emit(self, ins, outs, sems, phase):
        send_sems, recv_sems, loc_sems = sems
        x, y, c = lax.axis_index("x"), lax.axis_index("y"), lax.axis_index("c")
        me = 4 * x + 2 * y + c
        flip = lambda p, d: 1 - p if d else p

        def inside(p, lo, n):
            return None if (lo, n) == (0, NDEV) else jnp.logical_and(p >= lo, p < lo + n)

        def when(cond, fn):
            if cond is None:
                fn()
            else:
                pl.when(cond)(fn)

        pos = 0
        for a, (srcs, sc) in enumerate(self.items):
            refs = ins[pos:pos + len(srcs)]
            pos += len(srcs)
            ranges = [(lo, arr.shape[0]) if sc else (0, NDEV) for arr, lo in srcs]
            mine = [inside(me, lo, n) for lo, n in ranges]
            i_receive = None if None in mine else functools.reduce(jnp.logical_or, mine)
            for ref, (lo, n), cond in zip(refs, ranges, mine):
                def local(ref=ref, lo=lo):
                    cp = pltpu.make_async_copy(ref.at[me - lo] if sc else ref, outs[a].at[me], loc_sems.at[a])
                    cp.start() if phase == 0 else cp.wait()
                if phase != 1:
                    when(cond, local)
            for k in range(1, NDEV):
                px, py, pc = flip(x, k & 4), flip(y, k & 2), flip(c, k & 1)
                peer = 4 * px + 2 * py + pc
                mk = functools.partial(pltpu.make_async_remote_copy,
                                       send_sem=send_sems.at[a, k - 1], recv_sem=recv_sems.at[a, k - 1],
                                       device_id=(px, py, pc), device_id_type=MESH_ID)
                if phase == 1:
                    def arrival(mk=mk, peer=peer):
                        mk(src_ref=refs[0].at[0] if sc else refs[0], dst_ref=outs[a].at[peer]).wait_recv()
                    when(i_receive, arrival)
                    continue
                for ref, (lo, n) in zip(refs, ranges):
                    def send(mk=mk, ref=ref, lo=lo, peer=peer):
                        cp = mk(src_ref=ref.at[peer - lo] if sc else ref, dst_ref=outs[a].at[me])
                        cp.start() if phase == 0 else cp.wait_send()
                    when(inside(peer, lo, n), send)

    def start(self, ins, outs, sems):
        self._emit(ins, outs, sems, 0)

    def wait(self, ins, outs, sems):
        self._emit(ins, outs, sems, 1)
        self._emit(ins, outs, sems, 2)


def _call(body, *, name, grid, in_specs, out_specs, out_shape, args, scratch_shapes=(), sem=None, ride=None):
    if ride is None:
        return pl.pallas_call(body, name=name, grid=grid, in_specs=list(in_specs), out_specs=tuple(out_specs),
                              out_shape=tuple(out_shape), scratch_shapes=list(scratch_shapes),
                              compiler_params=_params(sem))(*args)
    n_in, n_out, n_scr, ne, ne_in = len(args), len(out_shape), len(scratch_shapes), ride.n, ride.n_in

    def ride_body(*refs):
        ins, ex_in = refs[:n_in], refs[n_in:n_in + ne_in]
        o0 = n_in + ne_in
        outs, ex_out = refs[o0:o0 + n_out], refs[o0 + n_out:o0 + n_out + ne]
        scr = refs[o0 + n_out + ne:o0 + n_out + ne + n_scr]
        sems = refs[o0 + n_out + ne + n_scr:]
        first = functools.reduce(jnp.logical_and, [pl.program_id(d) == 0 for d in range(len(grid))])
        last = functools.reduce(jnp.logical_and, [pl.program_id(d) == grid[d] - 1 for d in range(len(grid))])

        @pl.when(first)
        def _():
            ride.start(ex_in, ex_out, sems)

        body(*ins, *outs, *scr)

        @pl.when(last)
        def _():
            ride.wait(ex_in, ex_out, sems)

    anyspec = pl.BlockSpec(memory_space=pl.ANY)
    res = pl.pallas_call(
        ride_body, name=name, grid=grid,
        in_specs=list(in_specs) + [anyspec] * ne_in,
        out_specs=tuple(out_specs) + (anyspec,) * ne,
        out_shape=tuple(out_shape) + ride.out_shape(),
        scratch_shapes=list(scratch_shapes) + ride.scratch(),
        compiler_params=_params(("arbitrary",) * len(grid)),
    )(*args, *ride.arrays)
    return tuple(res[:n_out]), tuple(res[n_out:])


def _gather_two_level(arrays, *, name):
    n = len(arrays)

    def body(*refs):
        ins, outs = refs[:n], refs[n:2 * n]
        send_sems, recv_sems, loc_sems = refs[2 * n:]
        x, y, c = lax.axis_index("x"), lax.axis_index("y"), lax.axis_index("c")
        sibling = (x, y, 1 - c)
        chips = [(1 - x, y), (x, 1 - y), (1 - x, 1 - y)]
        idx = lambda px, py, pc: 4 * px + 2 * py + pc
        me = idx(x, y, c)

        def copy(a, k, block, to, src=None):
            return pltpu.make_async_remote_copy(
                src_ref=outs[a].at[block] if src is None else src, dst_ref=outs[a].at[block],
                send_sem=send_sems.at[a, k], recv_sem=recv_sems.at[a, k], device_id=to, device_id_type=MESH_ID)

        local = [pltpu.make_async_copy(ins[a], outs[a].at[me], loc_sems.at[a]) for a in range(n)]
        sent = []
        for a in range(n):
            sent.append(copy(a, 0, me, sibling, src=ins[a]))
            sent += [copy(a, 1 + j, me, (*chip, c), src=ins[a]) for j, chip in enumerate(chips)]
        for cp in local + sent:
            cp.start()
        for j, chip in enumerate(chips):
            for a in range(n):
                copy(a, 1 + j, idx(*chip, c), sibling).wait_recv()
                passed = copy(a, 4 + j, idx(*chip, c), sibling)
                passed.start()
                sent.append(passed)
        for a in range(n):
            copy(a, 0, idx(x, y, 1 - c), sibling).wait_recv()
            for j, chip in enumerate(chips):
                copy(a, 4 + j, idx(*chip, 1 - c), sibling).wait_recv()
        for cp in sent:
            cp.wait_send()
        for cp in local:
            cp.wait()

    anyspec = pl.BlockSpec(memory_space=pl.ANY)
    return pl.pallas_call(
        body, name=name,
        out_shape=tuple(jax.ShapeDtypeStruct((NDEV,) + a.shape, a.dtype) for a in arrays),
        in_specs=[anyspec] * n, out_specs=(anyspec,) * n,
        scratch_shapes=[pltpu.SemaphoreType.DMA((n, NDEV - 1)), pltpu.SemaphoreType.DMA((n, NDEV - 1)),
                        pltpu.SemaphoreType.DMA((n,))],
    )(*arrays)


def _tri_consts():
    r = lax.broadcasted_iota(jnp.int32, (GC, GC), 0)
    c = lax.broadcasted_iota(jnp.int32, (GC, GC), 1)
    return (r >= c).astype(BF), (r <= c).astype(BF)


def _local_step(x, positions, target, w, hooks=None):
    g = {}

    def run(host, fn, *a, **kw):
        h = None if hooks is None else hooks.get(host)
        if h is None:
            return fn(*a, **kw)
        out, received = fn(*a, ride=_Exchange(h[0](w, g)), **kw)
        h[1](received, w, g)
        return out

    nseq, S, _ = x.shape
    T = nseq * S
    tm = min(256, S)
    tq = min(512, S)
    x2 = x.reshape(T, D)
    pos = positions.reshape(T, 1)
    half = ROPE // 2
    inv = THETA ** (-jnp.arange(half, dtype=F32) / half)
    invf = jnp.concatenate([inv, inv, jnp.zeros((64,), F32)]).reshape(1, 128)
    ltri, utri = _tri_consts()

    pg = run("proj_g", _matmul, x2, w["w_gt"], "nt", name="proj_g", tm=1024, tn=640, tk=1024)
    pm = _matmul(x2, w["w_mt"], "nt", name="proj_m", tm=1024, tn=768, tk=1024)
    pt = _matmul(x2, w["w_tt"], "nt", name="proj_t", tm=1024, tn=1024, tk=1024)
    o, zg, states = _gla_fwd(pg, w["wg"], w["bg"], w["gn"], ltri, nseq=nseq, S=S, tm=tm)
    qc, kc, v = _mla_prep_fwd(pm, pos, invf, w["gq"], w["gkv"], w["wuq"], w["wukv"], tm=tm)
    attn, lse = run("flash_fwd", _flash_fwd, qc, kc, v, nseq=nseq, S=S, tq=tq)
    yg, ym, mix, pre1, h1, h1b = _post_attn_fwd(zg, attn, pt, x2, w["wgo"], w["wmo"], w["wout"],
                                                w["g1"], w["b1"], tm=tm)
    ug, uv, ucg, ucv, f_in = _ffn_up_fwd(h1b, w["wug"], w["wuv"], w["cw"], w["cb"], S=S, tm=tm, tn=1408)
    dpre2b, dh1, loss8, dg2, db2 = _down_ln2_loss(f_in, w["wd"], h1, target.reshape(T, D), w["g2"], w["b2"], tm=tm)

    dug, duv, dcg, dcv = _ffn_bwd(dpre2b, w["wd"], ug, uv, ucg, ucv, w["cw"], S=S, tm=tm, tn=1408)
    g["g2"], g["b2"], g["loss"] = dg2, db2, loss8[0:1, 0:1]
    g["cw"] = jnp.concatenate([dcg[0:3], dcv[0:3]], axis=1)
    g["cb"] = jnp.concatenate([dcg[3:4], dcv[3:4]], axis=1)
    g["wd"] = _matmul(f_in, dpre2b, "tn", name="dw_down", out_dtype=BF, tm=1408, tn=1024, tk=1024)
    g["wugt"] = _matmul(dug, h1b, "tn", name="dw_up_g", out_dtype=BF, tm=1408, tn=1024, tk=1024)
    g["wuvt"] = _matmul(duv, h1b, "tn", name="dw_up_v", out_dtype=BF, tm=1408, tn=1024, tk=1024)
    dh1 = _matmul(dug, w["wugt"], "nn", name="dh1_g", c_in=dh1, tm=1024, tn=1024, tk=1408)
    dh1 = _matmul(duv, w["wuvt"], "nn", name="dh1_v", c_in=dh1, tm=1024, tn=1024, tk=1408)
    dx, dpre1b, dpt, dygb, dymb, dzg, dattn, dg1, db1 = _post_attn_bwd(
        dh1, pre1, pt, yg, ym, w["wgo"], w["wmo"], w["wout"], w["g1"], tm=tm)
    g["g1"], g["b1"] = dg1, db1
    g["wout"] = _matmul(mix, dpre1b, "tn", name="dw_out", out_dtype=BF, tm=1024, tn=1024, tk=1024)
    g["wgo"] = _matmul(zg, dygb, "tn", name="dw_gla_o", out_dtype=BF, tm=1024, tn=1024, tk=1024)
    g["wmo"] = _matmul(attn, dymb, "tn", name="dw_mla_o", out_dtype=BF, tm=1024, tn=1024, tk=1024)
    dqc, dkc, dv = run("flash_bwd", _flash_bwd, qc, kc, v, attn, dattn, lse, nseq=nseq, S=S, tq=tq)
    dpm, g["wuq"], g["wukv"], g["gq"], g["gkv"] = _mla_prep_bwd(
        pm, pos, invf, w["gq"], w["gkv"], w["wuq"], w["wukv"], dqc, dkc, dv, tm=tm)
    g["w_mt"] = _matmul(dpm, x2, "tn", name="dw_in_m", out_dtype=BF, tm=768, tn=1024, tk=1024)
    g["w_tt"] = _matmul(dpt, x2, "tn", name="dw_in_t", out_dtype=BF, tm=1024, tn=1024, tk=1024)
    dpg, g["wg"], g["bg"], g["gn"] = run("gla_bwd", _gla_bwd, pg, w["wg"], w["bg"], w["gn"], ltri, utri, o, states,
                                         dzg, nseq=nseq, S=S, tm=tm)
    g["w_gt"] = _matmul(dpg, x2, "tn", name="dw_in_g", out_dtype=BF, tm=640, tn=1024, tk=1024)
    dx = run("dx", _matmul_sum, dx, [(dpg, w["w_gt"], 640), (dpm, w["w_mt"], 768)], name="dx_gm")
    dx = _matmul_sum(dx, [(dpt, w["w_tt"], 1024)], name="dx_t")
    return loss8[0, 0], dx.reshape(nseq, S, D), g


_IN_SPLITS = (512, 512, 1024, 16, 1024, 384, 256, 64, 1024, 1024)


def _w_in_to_groups(wt):
    offs = [0]
    for s in _IN_SPLITS:
        offs.append(offs[-1] + s)
    q, k, v, r, og, cq, ckv, kr, ga, gb = [wt[offs[i]:offs[i + 1]] for i in range(10)]
    z = lambda n: jnp.zeros((n, wt.shape[1]), wt.dtype)
    return (jnp.concatenate([q, k, v, og, r, z(112)], axis=0),
            jnp.concatenate([cq, kr, z(64), ckv], axis=0),
            jnp.concatenate([ga, gb], axis=0))


def _groups_to_w_in(g_g, g_m, g_t):
    q, k, v, og, r = g_g[0:512], g_g[512:1024], g_g[1024:2048], g_g[2048:3072], g_g[3072:3088]
    cq, kr, ckv = g_m[0:384], g_m[384:448], g_m[512:768]
    return jnp.concatenate([q, k, v, r, og, cq, ckv, kr, g_t], axis=0)


_W_IN_LO = 5
_W_IN_SPLIT = _W_IN_LO * 730 - 3472


def _w_in_rows_lo(g_g, g_m):
    q, k, v, og, r = g_g[0:512], g_g[512:1024], g_g[1024:2048], g_g[2048:3072], g_g[3072:3088]
    return jnp.concatenate([q, k, v, r, og, g_m[0:384], g_m[512:768]], axis=0)[:3472 + _W_IN_SPLIT]


def _w_in_rows_hi(g_m, g_t):
    return jnp.concatenate([g_m[512:768], g_m[384:448], g_t], axis=0)[_W_IN_SPLIT:]


def _uq_to_kernel(wuq):
    w3 = wuq.reshape(MQR, MH, NOPE + ROPE)
    rope = jnp.concatenate([w3[:, :, NOPE:], jnp.zeros((MQR, MH, 64), wuq.dtype)], axis=2)
    return jnp.concatenate([w3[:, :, :NOPE].reshape(MQR, MH * 128), rope.reshape(MQR, MH * 128)], axis=1)


def _uq_from_kernel(g):
    nope = g[:, :1024].reshape(MQR, MH, 128)
    rope = g[:, 1024:].reshape(MQR, MH, 128)[:, :, :ROPE]
    return jnp.concatenate([nope, rope], axis=2)


def _ukv_to_kernel(wukv):
    w3 = wukv.reshape(MKR, MH, NOPE + MV)
    return jnp.concatenate([w3[:, :, :NOPE].reshape(MKR, MH * 128), w3[:, :, NOPE:].reshape(MKR, MH * 128)], axis=1)


def _ukv_from_kernel(g):
    return jnp.concatenate([g[:, :1024].reshape(MKR, MH, 128), g[:, 1024:].reshape(MKR, MH, 128)], axis=2)


def _cols_gathered(a):
    return a.transpose(1, 0, 2).reshape(a.shape[1], NDEV * a.shape[2])


def _cols_scattered(a):
    R = a.shape[0]
    return a.reshape(R, NDEV, a.shape[1] // NDEV).transpose(1, 0, 2)


_SMALL = (("gla_b_gate", 512), ("gla_norm_g", 256), ("mla_q_norm_g", 384), ("mla_kv_norm_g", 256),
          ("ln1_g", 1024), ("ln1_b", 1024), ("conv_b", 5632), ("ln2_g", 1024), ("ln2_b", 1024))
_SMALL_ROWS = 88
_SMALL_USED = sum(sz for _, sz in _SMALL)


def _pack_small(d):
    flat = jnp.concatenate([d[n].reshape(-1) for n, _ in _SMALL] + ([d['loss'].reshape(-1)] if 'loss' in d else []))
    return jnp.pad(flat, (0, _SMALL_ROWS * 128 - flat.shape[0])).reshape(_SMALL_ROWS, 128)


def _unpack_small(a):
    flat = a.reshape(-1)
    out, off = {}, 0
    for n, sz in _SMALL:
        out[n] = flat[off:off + sz].reshape(1, sz)
        off += sz
    return out


_NAMES = ['w_in', 'gla_w_gate_up', 'gla_b_gate', 'gla_norm_g', 'w_gla_o', 'mla_q_norm_g', 'mla_w_uq',
          'mla_kv_norm_g', 'mla_w_ukv', 'w_mla_o', 'w_out', 'ln1_g', 'ln1_b', 'w_up', 'conv_w', 'conv_b',
          'w_down', 'ln2_g', 'ln2_b']
_SHARDED = ['w_in', 'w_up', 'w_down', 'w_gla_o', 'w_mla_o', 'w_out', 'mla_w_uq', 'mla_w_ukv', 'gla_w_gate_up',
            'conv_w']


def kernel(x, positions, w_in, gla_w_gate_up, gla_b_gate, gla_norm_g, w_gla_o, mla_q_norm_g, mla_w_uq, mla_kv_norm_g, mla_w_ukv, w_mla_o, w_out, ln1_g, ln1_b, w_up, conv_w, conv_b, w_down, ln2_g, ln2_b, loss_target, m_w_in, m_gla_w_gate_up, m_gla_b_gate, m_gla_norm_g, m_w_gla_o, m_mla_q_norm_g, m_mla_w_uq, m_mla_kv_norm_g, m_mla_w_ukv, m_w_mla_o, m_w_out, m_ln1_g, m_ln1_b, m_w_up, m_conv_w, m_conv_b, m_w_down, m_ln2_g, m_ln2_b, v_w_in, v_gla_w_gate_up, v_gla_b_gate, v_gla_norm_g, v_w_gla_o, v_mla_q_norm_g, v_mla_w_uq, v_mla_kv_norm_g, v_mla_w_ukv, v_w_mla_o, v_w_out, v_ln1_g, v_ln1_b, v_w_up, v_conv_w, v_conv_b, v_w_down, v_ln2_g, v_ln2_b):
    W = dict(w_in=w_in, gla_w_gate_up=gla_w_gate_up, gla_b_gate=gla_b_gate, gla_norm_g=gla_norm_g, w_gla_o=w_gla_o, mla_q_norm_g=mla_q_norm_g, mla_w_uq=mla_w_uq, mla_kv_norm_g=mla_kv_norm_g, mla_w_ukv=mla_w_ukv, w_mla_o=w_mla_o, w_out=w_out, ln1_g=ln1_g, ln1_b=ln1_b, w_up=w_up, conv_w=conv_w, conv_b=conv_b, w_down=w_down, ln2_g=ln2_g, ln2_b=ln2_b)
    M = dict(w_in=m_w_in, gla_w_gate_up=m_gla_w_gate_up, gla_b_gate=m_gla_b_gate, gla_norm_g=m_gla_norm_g, w_gla_o=m_w_gla_o, mla_q_norm_g=m_mla_q_norm_g, mla_w_uq=m_mla_w_uq, mla_kv_norm_g=m_mla_kv_norm_g, mla_w_ukv=m_mla_w_ukv, w_mla_o=m_w_mla_o, w_out=m_w_out, ln1_g=m_ln1_g, ln1_b=m_ln1_b, w_up=m_w_up, conv_w=m_conv_w, conv_b=m_conv_b, w_down=m_w_down, ln2_g=m_ln2_g, ln2_b=m_ln2_b)
    V = dict(w_in=v_w_in, gla_w_gate_up=v_gla_w_gate_up, gla_b_gate=v_gla_b_gate, gla_norm_g=v_gla_norm_g, w_gla_o=v_w_gla_o, mla_q_norm_g=v_mla_q_norm_g, mla_w_uq=v_mla_w_uq, mla_kv_norm_g=v_mla_kv_norm_g, mla_w_ukv=v_mla_w_ukv, w_mla_o=v_w_mla_o, w_out=v_w_out, ln1_g=v_ln1_g, ln1_b=v_ln1_b, w_up=v_w_up, conv_w=v_conv_w, conv_b=v_conv_b, w_down=v_w_down, ln2_g=v_ln2_g, ln2_b=v_ln2_b)

    tshard = lambda d, n: d[n][0].T
    shard = lambda n: (W[n][0].astype(BF), False)
    first = ['w_in', 'mla_w_uq', 'mla_w_ukv', 'gla_w_gate_up']
    G = dict(zip(first, _gather_two_level(
        [tshard(W, 'w_in').astype(BF)] + [shard(n)[0] for n in first[1:]], name="gather_w0")))
    w_gt, w_mt, w_tt = _w_in_to_groups(G['w_in'].reshape(NDEV * 730, D))
    kw = dict(
        w_gt=w_gt, w_mt=w_mt, w_tt=w_tt,
        wg=jnp.pad(_cols_gathered(G['gla_w_gate_up']), ((0, 128 - GR), (0, 0))), bg=W['gla_b_gate'],
        gn=W['gla_norm_g'], gq=W['mla_q_norm_g'], gkv=W['mla_kv_norm_g'],
        wuq=_uq_to_kernel(_cols_gathered(G['mla_w_uq'])), wukv=_ukv_to_kernel(_cols_gathered(G['mla_w_ukv'])),
        g1=W['ln1_g'], b1=W['ln1_b'], g2=W['ln2_g'], b2=W['ln2_b'], cb=W['conv_b'],
    )
    received = {}

    def got_out_proj(ex, w, g):
        w.update(wgo=ex[0].reshape(D, D), wmo=ex[1].reshape(D, D), wout=ex[2].reshape(D, D))

    def got_ffn(ex, w, g):
        w_upt = ex[0].reshape(2 * DFF, D)
        w.update(wugt=w_upt[:DFF], wuvt=w_upt[DFF:], wug=w_upt[:DFF].T, wuv=w_upt[DFF:].T,
                 wd=ex[1].reshape(DFF, D), cw=_cols_gathered(ex[2]))

    slab = lambda a, lo=0: ([(a.astype(BF), lo)], True)
    rows = lambda a, n=NDEV: a.reshape(n, a.shape[0] // n, a.shape[1])

    def keep(names):
        return lambda ex, w, g: received.update(zip(names, ex))

    def small_grads(g):
        return _pack_small(dict(gla_b_gate=g['bg'], gla_norm_g=g['gn'], mla_q_norm_g=g['gq'], mla_kv_norm_g=g['gkv'],
                                ln1_g=g['g1'], ln1_b=g['b1'], conv_b=g['cb'], ln2_g=g['g2'], ln2_b=g['b2'],
                                loss=g['loss']))

    hooks = {
        "proj_g": (lambda w, g: [shard('w_gla_o'), shard('w_mla_o'), shard('w_out')], got_out_proj),
        "flash_fwd": (lambda w, g: [(tshard(W, 'w_up').astype(BF), False), shard('w_down'), (W['conv_w'][0], False)],
                      got_ffn),
        "flash_bwd": (lambda w, g: [slab(rows(g['wd'])),
                                    ([(rows(g['wugt'], 4), 0), (rows(g['wuvt'], 4), 4)], True)],
                      keep(['w_down', 'w_up'])),
        "gla_bwd": (lambda w, g: [slab(rows(g['wout'])), slab(rows(g['wgo'])), slab(rows(g['wmo'])),
                                  slab(_uq_from_kernel(g['wuq']).transpose(1, 0, 2)),
                                  slab(_ukv_from_kernel(g['wukv']).transpose(1, 0, 2)),
                                  slab(rows(_w_in_rows_hi(g['w_mt'], g['w_tt']), NDEV - _W_IN_LO), _W_IN_LO)],
                    keep(['w_out', 'w_gla_o', 'w_mla_o', 'mla_w_uq', 'mla_w_ukv', 'w_in_hi'])),
        "dx": (lambda w, g: [slab(rows(_w_in_rows_lo(g['w_gt'], g['w_mt']), _W_IN_LO)),
                             ([(_cols_scattered(g['wg'][:GR]), 0)], True), ([(_cols_scattered(g['cw']), 0)], True),
                             (small_grads(g), False)],
               keep(['w_in_lo', 'gla_w_gate_up', 'conv_w', 'small'])),
    }

    _, grad_x, _ = _local_step(x, positions, loss_target, kw, hooks)

    grads, deltas, new_m, new_v = {}, {}, {}, {}
    small_parts = received['small']
    loss = jnp.sum(small_parts.reshape(NDEV, -1)[:, _SMALL_USED])
    me = 4 * lax.axis_index("x") + 2 * lax.axis_index("y") + lax.axis_index("c")
    received['w_in'] = jnp.where(me >= _W_IN_LO, received['w_in_hi'], received['w_in_lo'])
    for n in _SHARDED:
        shp = W[n].shape
        if n in ('w_in', 'w_up'):
            out = _adamw(received[n], tshard(W, n), tshard(M, n), tshard(V, n), name="adamw_" + n)
            grads[n], deltas[n], new_m[n], new_v[n] = [t.T.reshape(shp) for t in out]
            continue
        out = _adamw(received[n], W[n][0], M[n][0], V[n][0], name="adamw_" + n)
        grads[n], deltas[n], new_m[n], new_v[n] = [t.reshape(shp) for t in out]
    out = _adamw(small_parts, _pack_small(W), _pack_small(M), _pack_small(V), name="adamw_small")
    for dst, packed in zip((grads, deltas, new_m, new_v), out):
        dst.update(_unpack_small(packed))

    return (loss, grad_x, *[grads[n] for n in _NAMES], *[deltas[n] for n in _NAMES],
            *[new_m[n] for n in _NAMES], *[new_v[n] for n in _NAMES])
```

```python
import functools

import jax
import jax.numpy as jnp
from jax import lax
from jax.experimental import pallas as pl
from jax.experimental.pallas import tpu as pltpu

F32 = jnp.float32
BF = jnp.bfloat16

D = 1024
GH, GDK, GDV, GR, GTAU, GC = 4, 128, 256, 16, 16.0, 64
MH, MQR, MKR, NOPE, ROPE, MV = 8, 384, 256, 128, 64, 128
THETA = 10000.0
DFF = 2816
ALPHA = 2.0 ** 0.25
LN_EPS = 1e-5
RMS_EPS = 1e-6
NDEV = 8
ADAM_LR, ADAM_B1, ADAM_B2, ADAM_EPS, ADAM_WD, ADAM_STEP = 0.001, 0.9, 0.999, 1e-08, 0.01, 10

PG_W = 3200
PM_W = 768
PT_W = 2048
NEG = -1e30
MESH_ID = pl.DeviceIdType.MESH
VMEM_MB = 1024 * 1024


def _params(sem, vmem_mb=48):
    return pltpu.CompilerParams(dimension_semantics=sem, vmem_limit_bytes=vmem_mb * VMEM_MB)


def _dot(a, b):
    return lax.dot_general(a, b, (((1,), (0,)), ((), ())), preferred_element_type=F32)


def _dot_nt(a, b):
    return lax.dot_general(a, b, (((1,), (1,)), ((), ())), preferred_element_type=F32)


def _dot_tn(a, b):
    return lax.dot_general(a, b, (((0,), (0,)), ((), ())), preferred_element_type=F32)


def _iota(shape, dim):
    return lax.broadcasted_iota(jnp.int32, shape, dim)


FLASH_HP = 2
FLASH_HP_FWD = 4
QK_SCALE = (NOPE + ROPE) ** -0.5
LOG2E = 1.4426950408889634
QK_SCALE_LOG2 = QK_SCALE * LOG2E


def _sigmoid(x):
    return 0.5 * jnp.tanh(0.5 * x) + 0.5


def _tri_mm(tri_bf, x):
    hi = x.astype(BF)
    r1 = x - hi.astype(F32)
    mid = r1.astype(BF)
    lo = (r1 - mid.astype(F32)).astype(BF)
    return _dot(tri_bf, hi) + _dot(tri_bf, mid) + _dot(tri_bf, lo)


def _matmul(a, b, mode, *, name, c_in=None, out_dtype=F32, tm=512, tn=512, tk=512, ride=None, emit_a=False):
    if mode == "nn":
        (M, K), (_, N) = a.shape, b.shape
    elif mode == "nt":
        (M, K), (N, _) = a.shape, b.shape
    else:
        (K, M), (_, N) = a.shape, b.shape
    tm, tn, tk = min(tm, M), min(tn, N), min(tk, K)
    assert M % tm == 0 and N % tn == 0 and K % tk == 0, (name, M, N, K, tm, tn, tk)
    nk = K // tk
    assert not emit_a or (nk == 1 and mode != "tn" and c_in is None and ride is None)
    dot = {"nn": _dot, "nt": _dot_nt, "tn": _dot_tn}[mode]

    def body(*refs):
        if emit_a:
            a_ref, b_ref, o_ref, xa_ref, acc_ref = refs
        elif c_in is None:
            a_ref, b_ref, o_ref, acc_ref = refs
        else:
            a_ref, b_ref, c_ref, o_ref, acc_ref = refs
        k = pl.program_id(2)

        @pl.when(k == 0)
        def _():
            if c_in is None:
                acc_ref[...] = jnp.zeros_like(acc_ref)
            else:
                acc_ref[...] = c_ref[...].astype(F32)

        if emit_a:
            @pl.when(pl.program_id(1) == 0)
            def _():
                xa_ref[...] = a_ref[...].astype(BF)

        acc_ref[...] += dot(a_ref[...].astype(BF), b_ref[...].astype(BF))

        @pl.when(k == nk - 1)
        def _():
            o_ref[...] = acc_ref[...].astype(out_dtype)

    if mode == "tn":
        a_spec = pl.BlockSpec((tk, tm), lambda i, j, k: (k, i))
    else:
        a_spec = pl.BlockSpec((tm, tk), lambda i, j, k: (i, k))
    if mode == "nt":
        b_spec = pl.BlockSpec((tn, tk), lambda i, j, k: (j, k))
    else:
        b_spec = pl.BlockSpec((tk, tn), lambda i, j, k: (k, j))
    in_specs = [a_spec, b_spec]
    args = [a, b]
    if c_in is not None:
        in_specs.append(pl.BlockSpec((tm, tn), lambda i, j, k: (i, j)))
        args.append(c_in)
    out_shape = (jax.ShapeDtypeStruct((M, N), out_dtype),)
    out_specs = (pl.BlockSpec((tm, tn), lambda i, j, k: (i, j)),)
    if emit_a:
        out_shape += (jax.ShapeDtypeStruct((M, K), BF),)
        out_specs += (pl.BlockSpec((tm, tk), lambda i, j, k: (i, k)),)
    res = _call(
        body, name=name, out_shape=out_shape, grid=(M // tm, N // tn, nk), in_specs=in_specs, out_specs=out_specs,
        scratch_shapes=[pltpu.VMEM((tm, tn), F32)],
        sem=("parallel", "arbitrary", "arbitrary"), args=args, ride=ride)
    if emit_a:
        return res[0], res[1]
    return res[0] if ride is None else (res[0][0], res[1])


def _matmul_sum(c_in, parts, *, name, tm=1024, ride=None):
    M, N = c_in.shape
    tm = min(tm, M)
    n_p = len(parts)
    counts = [a.shape[1] // tk for a, _, tk in parts]
    starts = [sum(counts[:p]) for p in range(n_p)]
    nk = sum(counts)

    def body(*refs):
        a_refs, w_refs = refs[:n_p], refs[n_p:2 * n_p]
        c_ref, o_ref, acc_ref = refs[2 * n_p:]
        k = pl.program_id(1)

        @pl.when(k == 0)
        def _():
            acc_ref[...] = c_ref[...]

        for p in range(n_p):
            @pl.when(jnp.logical_and(k >= starts[p], k < starts[p] + counts[p]))
            def _(p=p):
                acc_ref[...] += _dot(a_refs[p][...].astype(BF), w_refs[p][...].astype(BF))

        @pl.when(k == nk - 1)
        def _():
            o_ref[...] = acc_ref[...]

    def kidx(p):
        return lambda k: jnp.clip(k - starts[p], 0, counts[p] - 1)

    in_specs = [pl.BlockSpec((tm, tk), lambda i, k, f=kidx(p): (i, f(k))) for p, (_, _, tk) in enumerate(parts)]
    in_specs += [pl.BlockSpec((tk, N), lambda i, k, f=kidx(p): (f(k), 0)) for p, (_, _, tk) in enumerate(parts)]
    in_specs.append(pl.BlockSpec((tm, N), lambda i, k: (i, 0)))
    res = _call(
        body, name=name, out_shape=(jax.ShapeDtypeStruct((M, N), F32),), grid=(M // tm, nk),
        in_specs=in_specs, out_specs=(pl.BlockSpec((tm, N), lambda i, k: (i, 0)),),
        scratch_shapes=[pltpu.VMEM((tm, N), F32)], sem=("parallel", "arbitrary"),
        args=[a for a, _, _ in parts] + [w for _, w, _ in parts] + [c_in], ride=ride)
    return res[0] if ride is None else (res[0][0], res[1])


def _gla_gate(pg_ref, rows, wg_ref, bg_ref):
    r = pg_ref[rows, 3072:3200].astype(BF)
    logit = _dot(r, wg_ref[...]) + bg_ref[...]
    la = (jnp.minimum(logit, 0.0) - jnp.log(1.0 + jnp.exp(-jnp.abs(logit)))) * (1.0 / GTAU)
    return r, logit, la


def _gla_fwd(pg, wg, bg, gn, ltri, *, nseq, S, tm):
    T = pg.shape[0]
    nb, nc = S // tm, tm // GC
    qscale = GDK ** -0.5

    def body(pg_ref, wg_ref, bg_ref, gn_ref, l_ref, o_ref, zg_ref, st_ref, st_scr):
        @pl.when(pl.program_id(1) == 0)
        def _():
            st_scr[...] = jnp.zeros_like(st_scr)

        ltri_v = l_ref[...]
        causal = _iota((GC, GC), 0) >= _iota((GC, GC), 1)
        last_row = _iota((GC, GDK), 0) == GC - 1
        g = gn_ref[...]

        def chunk(c, carry):
            rows = pl.ds(pl.multiple_of(c * GC, GC), GC)
            _, _, la = _gla_gate(pg_ref, rows, wg_ref, bg_ref)
            b = _tri_mm(ltri_v, la)
            hs = range(GH)
            v, q_in, k_st, dec, st, a_raw, o_st, kv = [], [], [], [], [], [], [], []
            for h in hs:
                q = pg_ref[rows, h * GDK:(h + 1) * GDK]
                k = pg_ref[rows, 512 + h * GDK:512 + (h + 1) * GDK]
                v.append(pg_ref[rows, 1024 + h * GDV:1024 + (h + 1) * GDV].astype(BF))
                bh = b[:, h * GDK:(h + 1) * GDK]
                bl = jnp.sum(jnp.where(last_row, bh, 0.0), axis=0, keepdims=True)
                q_in.append((q * (qscale * jnp.exp(bh))).astype(BF))
                k_in = (k * jnp.exp(-bh)).astype(BF)
                k_st.append((k * jnp.exp(bl - bh)).astype(BF))
                dec.append(jnp.exp(bl))
                st.append(st_scr[h])
                st_ref[c, h] = st[h]
                a_raw.append(_dot_nt(q_in[h], k_in))
            for h in hs:
                o_st.append(_dot_nt(q_in[h], st[h].astype(BF)))
                kv.append(_dot_tn(v[h], k_st[h]))
            att = [jnp.where(causal, a_raw[h], 0.0).astype(BF) for h in hs]
            o = [_dot(att[h], v[h]) + o_st[h] for h in hs]
            for h in hs:
                st_scr[h] = st[h] * dec[h] + kv[h]
                og = pg_ref[rows, 2048 + h * GDV:2048 + (h + 1) * GDV]
                rstd = lax.rsqrt(jnp.mean(o[h] * o[h], axis=-1, keepdims=True) + RMS_EPS)
                o_ref[rows, h * GDV:(h + 1) * GDV] = o[h]
                zg_ref[rows, h * GDV:(h + 1) * GDV] = (o[h] * rstd * g * (og * _sigmoid(og))).astype(BF)
            return carry

        lax.fori_loop(0, nc, chunk, 0, unroll=True)

    full = lambda shp: pl.BlockSpec(shp, lambda b_, i: (0,) * len(shp))
    return pl.pallas_call(
        body, name="gla_fwd",
        out_shape=(jax.ShapeDtypeStruct((T, GH * GDV), F32),
                   jax.ShapeDtypeStruct((T, GH * GDV), BF),
                   jax.ShapeDtypeStruct((T // GC, GH, GDV, GDK), F32)),
        grid=(nseq, nb),
        in_specs=[pl.BlockSpec((tm, PG_W), lambda b_, i: (b_ * nb + i, 0)),
                  full((128, 512)), full((1, 512)), full((1, GDV)), full((GC, GC))],
        out_specs=(pl.BlockSpec((tm, GH * GDV), lambda b_, i: (b_ * nb + i, 0)),
                   pl.BlockSpec((tm, GH * GDV), lambda b_, i: (b_ * nb + i, 0)),
                   pl.BlockSpec((nc, GH, GDV, GDK), lambda b_, i: (b_ * nb + i, 0, 0, 0))),
        scratch_shapes=[pltpu.VMEM((GH, GDV, GDK), F32)],
        compiler_params=_params(("parallel", "arbitrary")),
    )(pg, wg, bg, gn, ltri)


def _gla_bwd(pg, wg, bg, gn, ltri, utri, o, states, dzg, *, nseq, S, tm, ride=None):
    T = pg.shape[0]
    nb, nc = S // tm, tm // GC
    qscale = GDK ** -0.5

    def body(pg_ref, wg_ref, bg_ref, gn_ref, l_ref, u_ref, o_ref, st_ref, dzg_ref,
             dpg_ref, dwg_ref, dbg_ref, dgn_ref, dst_scr):
        first = jnp.logical_and(pl.program_id(0) == 0, pl.program_id(1) == 0)

        @pl.when(first)
        def _():
            dwg_ref[...] = jnp.zeros_like(dwg_ref)
            dbg_ref[...] = jnp.zeros_like(dbg_ref)
            dgn_ref[...] = jnp.zeros_like(dgn_ref)

        @pl.when(pl.program_id(1) == 0)
        def _():
            dst_scr[...] = jnp.zeros_like(dst_scr)

        ltri_v = l_ref[...]
        utri_v = u_ref[...]
        causal = _iota((GC, GC), 0) >= _iota((GC, GC), 1)
        last_row = _iota((GC, GDK), 0) == GC - 1
        g = gn_ref[...]

        def chunk(cc, carry):
            c = nc - 1 - cc
            rows = pl.ds(pl.multiple_of(c * GC, GC), GC)
            r, logit, la = _gla_gate(pg_ref, rows, wg_ref, bg_ref)
            b = _tri_mm(ltri_v, la)
            hs = range(GH)
            L = lambda: [None] * GH
            vb, eb, enb, ek, dec, q_in, k_in, k_st, q_inb, k_inb, st, dst, dob = (L() for _ in range(13))
            a_raw, da_raw, dq_st, dks, dv_st, dst_new, dbs, dgn = (L() for _ in range(8))
            for h in hs:
                q = pg_ref[rows, h * GDK:(h + 1) * GDK]
                k = pg_ref[rows, 512 + h * GDK:512 + (h + 1) * GDK]
                vb[h] = pg_ref[rows, 1024 + h * GDV:1024 + (h + 1) * GDV].astype(BF)
                og = pg_ref[rows, 2048 + h * GDV:2048 + (h + 1) * GDV]
                oh = o_ref[rows, h * GDV:(h + 1) * GDV]
                dz = dzg_ref[rows, h * GDV:(h + 1) * GDV].astype(F32)
                bh = b[:, h * GDK:(h + 1) * GDK]
                bl = jnp.sum(jnp.where(last_row, bh, 0.0), axis=0, keepdims=True)
                eb[h] = qscale * jnp.exp(bh)
                enb[h] = jnp.exp(-bh)
                ek[h] = jnp.exp(bl - bh)
                dec[h] = jnp.exp(bl)
                q_in[h], k_in[h], k_st[h] = q * eb[h], k * enb[h], k * ek[h]
                q_inb[h], k_inb[h] = q_in[h].astype(BF), k_in[h].astype(BF)
                st[h] = st_ref[c, h]
                dst[h] = dst_scr[h]
                rstd = lax.rsqrt(jnp.mean(oh * oh, axis=-1, keepdims=True) + RMS_EPS)
                ohat = oh * rstd
                sg = _sigmoid(og)
                don = dz * (og * sg)
                dpg_ref[rows, 2048 + h * GDV:2048 + (h + 1) * GDV] = (
                    dz * (ohat * g) * (sg * (1.0 + og * (1.0 - sg)))).astype(BF)
                dgn[h] = jnp.sum(don * ohat, axis=0, keepdims=True)
                gd = don * g
                dob[h] = (rstd * (gd - ohat * jnp.mean(gd * ohat, axis=-1, keepdims=True))).astype(BF)
                a_raw[h] = _dot_nt(q_inb[h], k_inb[h])
                da_raw[h] = _dot_nt(dob[h], vb[h])
            dgn_ref[...] += dgn[0] + dgn[1] + dgn[2] + dgn[3]
            for h in hs:
                dstb = dst[h].astype(BF)
                dq_st[h] = _dot(dob[h], st[h].astype(BF))
                dks[h] = _dot(vb[h], dstb)
                dv_st[h] = _dot_nt(k_st[h].astype(BF), dstb)
                dst_new[h] = _dot_tn(dob[h], q_inb[h])
            att = [jnp.where(causal, a_raw[h], 0.0).astype(BF) for h in hs]
            da = [jnp.where(causal, da_raw[h], 0.0).astype(BF) for h in hs]
            dqi = [_dot(da[h], k_inb[h]) + dq_st[h] for h in hs]
            dki = [_dot_tn(da[h], q_inb[h]) for h in hs]
            dv = [_dot_tn(att[h], dob[h]) + dv_st[h] for h in hs]
            for h in hs:
                dd = jnp.sum(dst[h] * st[h], axis=0, keepdims=True)
                dst_scr[h] = dst[h] * dec[h] + dst_new[h]
                kk = dks[h] * k_st[h]
                dbl = jnp.sum(kk, axis=0, keepdims=True) + dd * dec[h]
                db = dqi[h] * q_in[h] - dki[h] * k_in[h] - kk
                dbs[h] = db + jnp.where(last_row, dbl, 0.0)
                dpg_ref[rows, h * GDK:(h + 1) * GDK] = (dqi[h] * eb[h]).astype(BF)
                dpg_ref[rows, 512 + h * GDK:512 + (h + 1) * GDK] = (dki[h] * enb[h] + dks[h] * ek[h]).astype(BF)
                dpg_ref[rows, 1024 + h * GDV:1024 + (h + 1) * GDV] = dv[h].astype(BF)
            dla = _tri_mm(utri_v, jnp.concatenate(dbs, axis=1))
            dlogit = dla * (1.0 / GTAU) * _sigmoid(-logit)
            dlb = dlogit.astype(BF)
            dpg_ref[rows, 3072:3200] = _dot_nt(dlb, wg_ref[...]).astype(BF)
            dwg_ref[...] += _dot_tn(r, dlb)
            dbg_ref[...] += jnp.sum(dlogit, axis=0, keepdims=True)
            return carry

        lax.fori_loop(0, nc, chunk, 0, unroll=True)

    full = lambda shp: pl.BlockSpec(shp, lambda b_, i: (0,) * len(shp))
    rev = lambda b_, i: (b_ * nb + nb - 1 - i, 0)
    return _call(
        body, name="gla_bwd", ride=ride, sem=("arbitrary", "arbitrary"),
        args=(pg, wg, bg, gn, ltri, utri, o, states, dzg),
        out_shape=(jax.ShapeDtypeStruct((T, PG_W), BF),
                   jax.ShapeDtypeStruct((128, 512), F32),
                   jax.ShapeDtypeStruct((1, 512), F32),
                   jax.ShapeDtypeStruct((1, GDV), F32)),
        grid=(nseq, nb),
        in_specs=[pl.BlockSpec((tm, PG_W), rev),
                  full((128, 512)), full((1, 512)), full((1, GDV)), full((GC, GC)), full((GC, GC)),
                  pl.BlockSpec((tm, GH * GDV), rev),
                  pl.BlockSpec((nc, GH, GDV, GDK), lambda b_, i: (b_ * nb + nb - 1 - i, 0, 0, 0)),
                  pl.BlockSpec((tm, GH * GDV), rev)],
        out_specs=(pl.BlockSpec((tm, PG_W), rev), full((128, 512)), full((1, 512)), full((1, GDV))),
        scratch_shapes=[pltpu.VMEM((GH, GDV, GDK), F32)])


def _rope_tables(pos, invf):
    ang = pos.astype(F32) * invf
    lane = _iota(ang.shape, 1)
    sin = jnp.sin(ang)
    ssin = jnp.where(lane < 32, -sin, jnp.where(lane < 64, sin, 0.0))
    return jnp.cos(ang), ssin, lane


def _rope(x, cos, ssin, lane, sign):
    rot = jnp.where(lane < 32, pltpu.roll(x, 96, 1), pltpu.roll(x, 32, 1))
    return x * cos + sign * (rot * ssin)


def _rms_fwd(x, g):
    rstd = lax.rsqrt(jnp.mean(x * x, axis=-1, keepdims=True) + RMS_EPS)
    return x * rstd * g, x * rstd, rstd


def _rms_bwd(dy, xhat, rstd, g):
    gd = dy * g
    return rstd * (gd - xhat * jnp.mean(gd * xhat, axis=-1, keepdims=True)), jnp.sum(dy * xhat, axis=0, keepdims=True)


def _mla_prep_fwd(pm, pos, invf, gq, gkv, wuq, wukv, *, tm):
    T = pm.shape[0]

    def body(pm_ref, pos_ref, invf_ref, gq_ref, gkv_ref, wuq_ref, wukv_ref, qc_ref, kc_ref, v_ref):
        cos, ssin, lane = _rope_tables(pos_ref[...], invf_ref[...])
        cq, _, _ = _rms_fwd(pm_ref[:, 0:MQR], gq_ref[...])
        ckv, _, _ = _rms_fwd(pm_ref[:, 512:768], gkv_ref[...])
        qf = _dot(cq.astype(BF), wuq_ref[...])
        kvf = _dot(ckv.astype(BF), wukv_ref[...])
        kr = _rope(pm_ref[:, 384:512], cos, ssin, lane, 1.0).astype(BF)
        for h in range(MH):
            qc_ref[:, 256 * h:256 * h + 128] = (QK_SCALE_LOG2 * qf[:, 128 * h:128 * h + 128]).astype(BF)
            qr = qf[:, 1024 + 128 * h:1024 + 128 * h + 128]
            qc_ref[:, 256 * h + 128:256 * h + 256] = (QK_SCALE_LOG2 * _rope(qr, cos, ssin, lane, 1.0)).astype(BF)
            kc_ref[:, 256 * h:256 * h + 128] = kvf[:, 128 * h:128 * h + 128].astype(BF)
            kc_ref[:, 256 * h + 128:256 * h + 256] = kr
        v_ref[...] = kvf[:, 1024:2048].astype(BF)

    full = lambda shp: pl.BlockSpec(shp, lambda i: (0,) * len(shp))
    row = lambda w: pl.BlockSpec((tm, w), lambda i: (i, 0))
    return pl.pallas_call(
        body, name="mla_prep_fwd",
        out_shape=(jax.ShapeDtypeStruct((T, MH * 256), BF), jax.ShapeDtypeStruct((T, MH * 256), BF),
                   jax.ShapeDtypeStruct((T, MH * MV), BF)),
        grid=(T // tm,),
        in_specs=[row(PM_W), row(1), full((1, 128)), full((1, MQR)), full((1, MKR)),
                  full((MQR, 2048)), full((MKR, 2048))],
        out_specs=(row(MH * 256), row(MH * 256), row(MH * MV)),
        compiler_params=_params(("parallel",)),
    )(pm, pos, invf, gq, gkv, wuq, wukv)


def _mla_prep_bwd(pm, pos, invf, gq, gkv, wuq, wukv, dqc, dkc, dv, *, tm):
    T = pm.shape[0]

    def body(pm_ref, pos_ref, invf_ref, gq_ref, gkv_ref, wuq_ref, wukv_ref, dqc_ref, dkc_ref, dv_ref,
             dpm_ref, dwuq_ref, dwukv_ref, dgq_ref, dgkv_ref):
        @pl.when(pl.program_id(0) == 0)
        def _():
            dwuq_ref[...] = jnp.zeros_like(dwuq_ref)
            dwukv_ref[...] = jnp.zeros_like(dwukv_ref)
            dgq_ref[...] = jnp.zeros_like(dgq_ref)
            dgkv_ref[...] = jnp.zeros_like(dgkv_ref)

        cos, ssin, lane = _rope_tables(pos_ref[...], invf_ref[...])
        cq, cqh, cq_rstd = _rms_fwd(pm_ref[:, 0:MQR], gq_ref[...])
        ckv, ckvh, ckv_rstd = _rms_fwd(pm_ref[:, 512:768], gkv_ref[...])
        dqn, dqr, dkn = [], [], []
        dkr = jnp.zeros((tm, 128), F32)
        for h in range(MH):
            dqn.append(dqc_ref[:, 256 * h:256 * h + 128].astype(BF))
            dqr.append(_rope(dqc_ref[:, 256 * h + 128:256 * h + 256], cos, ssin, lane, -1.0).astype(BF))
            dkn.append(dkc_ref[:, 256 * h:256 * h + 128].astype(BF))
            dkr = dkr + dkc_ref[:, 256 * h + 128:256 * h + 256]
        dqf = jnp.concatenate(dqn + dqr, axis=1)
        dkvf = jnp.concatenate(dkn + [dv_ref[...].astype(BF)], axis=1)
        dwuq_ref[...] += _dot_tn(cq.astype(BF), dqf)
        dwukv_ref[...] += _dot_tn(ckv.astype(BF), dkvf)
        dcq, dgq = _rms_bwd(_dot_nt(dqf, wuq_ref[...]), cqh, cq_rstd, gq_ref[...])
        dckv, dgkv = _rms_bwd(_dot_nt(dkvf, wukv_ref[...]), ckvh, ckv_rstd, gkv_ref[...])
        dgq_ref[...] += dgq
        dgkv_ref[...] += dgkv
        dpm_ref[:, 0:MQR] = dcq.astype(BF)
        dpm_ref[:, 384:512] = _rope(dkr, cos, ssin, lane, -1.0).astype(BF)
        dpm_ref[:, 512:768] = dckv.astype(BF)

    full = lambda shp: pl.BlockSpec(shp, lambda i: (0,) * len(shp))
    row = lambda w: pl.BlockSpec((tm, w), lambda i: (i, 0))
    return pl.pallas_call(
        body, name="mla_prep_bwd",
        out_shape=(jax.ShapeDtypeStruct((T, PM_W), BF), jax.ShapeDtypeStruct((MQR, 2048), F32),
                   jax.ShapeDtypeStruct((MKR, 2048), F32), jax.ShapeDtypeStruct((1, MQR), F32),
                   jax.ShapeDtypeStruct((1, MKR), F32)),
        grid=(T // tm,),
        in_specs=[row(PM_W), row(1), full((1, 128)), full((1, MQR)), full((1, MKR)),
                  full((MQR, 2048)), full((MKR, 2048)), row(MH * 256), row(MH * 256), row(MH * MV)],
        out_specs=(row(PM_W), full((MQR, 2048)), full((MKR, 2048)), full((1, MQR)), full((1, MKR))),
        compiler_params=_params(("arbitrary",)),
    )(pm, pos, invf, gq, gkv, wuq, wukv, dqc, dkc, dv)


def _flash_fwd(qc, kc, v, *, nseq, S, tq, ride=None):
    T = qc.shape[0]
    nq = S // tq
    hp = FLASH_HP_FWD

    def body(q_ref, k_ref, v_ref, o_ref, lse_ref):
        i = pl.program_id(2)
        causal = _iota((tq, tq), 0) >= _iota((tq, tq), 1)

        def step(j, carry, masked):
            rows = pl.ds(pl.multiple_of(j * tq, tq), tq)
            hs = range(hp)
            s = [_dot_nt(q_ref[:, 256 * hh:256 * hh + 256], k_ref[rows, 256 * hh:256 * hh + 256]) for hh in hs]
            p, stats = [], []
            for hh in hs:
                m, l, _ = carry[hh]
                sh = jnp.where(causal, s[hh], NEG) if masked else s[hh]
                m_new = jnp.maximum(m, jnp.max(sh, axis=-1, keepdims=True))
                ph = jnp.exp2(sh - m_new)
                a = jnp.exp2(m - m_new)
                stats.append((m_new, a * l + jnp.sum(ph, axis=-1, keepdims=True), a))
                p.append(ph.astype(BF))
            pv = [_dot(p[hh], v_ref[rows, MV * hh:MV * hh + MV]) for hh in hs]
            return tuple((stats[hh][0], stats[hh][1], stats[hh][2] * carry[hh][2] + pv[hh]) for hh in hs)

        init = ((jnp.full((tq, 1), NEG, F32), jnp.zeros((tq, 1), F32), jnp.zeros((tq, MV), F32)),) * hp
        carry = lax.fori_loop(0, i, lambda j, c: step(j, c, False), init)
        for hh, (m, l, acc) in enumerate(step(i, carry, True)):
            o_ref[:, MV * hh:MV * hh + MV] = (acc / l).astype(BF)
            lse_ref[:, 128 * hh:128 * hh + 128] = jnp.broadcast_to(m + jnp.log2(l), (tq, 128))

    return _call(
        body, name="flash_fwd", ride=ride, sem=("parallel", "parallel", "arbitrary"), args=(qc, kc, v),
        out_shape=(jax.ShapeDtypeStruct((T, MH * MV), BF), jax.ShapeDtypeStruct((T, MH * 128), F32)),
        grid=(nseq, MH // hp, nq),
        in_specs=[pl.BlockSpec((tq, 256 * hp), lambda b_, h, i: (b_ * nq + i, h)),
                  pl.BlockSpec((S, 256 * hp), lambda b_, h, i: (b_, h)),
                  pl.BlockSpec((S, MV * hp), lambda b_, h, i: (b_, h))],
        out_specs=(pl.BlockSpec((tq, MV * hp), lambda b_, h, i: (b_ * nq + i, h)),
                   pl.BlockSpec((tq, 128 * hp), lambda b_, h, i: (b_ * nq + i, h))))


def _flash_bwd(qc, kc, v, o, do, lse, *, nseq, S, tq, ride=None):
    T = qc.shape[0]
    nq = S // tq

    def body(q_ref, k_ref, v_ref, o_ref, do_ref, lse_ref, dq_ref, dk_ref, dv_ref, dq_scr, delta_scr):
        j = pl.program_id(2)

        @pl.when(j == 0)
        def _():
            dq_scr[...] = jnp.zeros_like(dq_scr)
            for hh in range(FLASH_HP):
                od = o_ref[:, MV * hh:MV * hh + MV].astype(F32) * do_ref[:, MV * hh:MV * hh + MV].astype(F32)
                delta_scr[:, 128 * hh:128 * hh + 128] = jnp.broadcast_to(jnp.sum(od, axis=-1, keepdims=True), (S, 128))

        causal = _iota((tq, tq), 0) >= _iota((tq, tq), 1)

        def step(i, carry, masked):
            rows = pl.ds(pl.multiple_of(i * tq, tq), tq)
            hs = range(FLASH_HP)
            qs = [slice(256 * hh, 256 * hh + 256) for hh in hs]
            vs = [slice(MV * hh, MV * hh + MV) for hh in hs]
            ls = [slice(128 * hh, 128 * hh + 1) for hh in hs]
            s = [_dot_nt(q_ref[rows, qs[hh]], k_ref[:, qs[hh]]) for hh in hs]
            dp = [_dot_nt(do_ref[rows, vs[hh]], v_ref[:, vs[hh]]) for hh in hs]
            pb, ds = [], []
            for hh in hs:
                p = jnp.exp2(s[hh] - lse_ref[rows, ls[hh]])
                if masked:
                    p = jnp.where(causal, p, 0.0)
                pb.append(p.astype(BF))
                ds.append((p * (dp[hh] - delta_scr[rows, ls[hh]])).astype(BF))
            dv = [carry[hh][1] + _dot_tn(pb[hh], do_ref[rows, vs[hh]]) for hh in hs]
            dk = [carry[hh][0] + _dot_tn(ds[hh], q_ref[rows, qs[hh]]) for hh in hs]
            for hh in hs:
                dq_scr[rows, qs[hh]] += _dot(ds[hh], k_ref[:, qs[hh]])
            return tuple((dk[hh], dv[hh]) for hh in hs)

        init = ((jnp.zeros((tq, 256), F32), jnp.zeros((tq, MV), F32)),) * FLASH_HP
        carry = step(j, init, True)
        carry = lax.fori_loop(j + 1, nq, lambda i, c: step(i, c, False), carry)
        for hh, (dk, dv) in enumerate(carry):
            dk_ref[:, 256 * hh:256 * hh + 256] = dk * (1.0 / LOG2E)
            dv_ref[:, MV * hh:MV * hh + MV] = dv

        @pl.when(j == nq - 1)
        def _():
            dq_ref[...] = dq_scr[...] * QK_SCALE

    hp = FLASH_HP
    seq = lambda w: pl.BlockSpec((S, w * hp), lambda b_, h, j: (b_, h))
    blk = lambda w: pl.BlockSpec((tq, w * hp), lambda b_, h, j: (b_ * nq + j, h))
    return _call(
        body, name="flash_bwd", ride=ride, sem=("parallel", "parallel", "arbitrary"), args=(qc, kc, v, o, do, lse),
        out_shape=(jax.ShapeDtypeStruct((T, MH * 256), F32), jax.ShapeDtypeStruct((T, MH * 256), F32),
                   jax.ShapeDtypeStruct((T, MH * MV), F32)),
        grid=(nseq, MH // hp, nq),
        in_specs=[seq(256), blk(256), blk(MV), seq(MV), seq(MV), seq(128)],
        out_specs=(seq(256), blk(256), blk(MV)),
        scratch_shapes=[pltpu.VMEM((S, 256 * hp), F32), pltpu.VMEM((S, 128 * hp), F32)])


def _ln_fwd(pre, g, b):
    mu = jnp.mean(pre, axis=-1, keepdims=True)
    xc = pre - mu
    rstd = lax.rsqrt(jnp.mean(xc * xc, axis=-1, keepdims=True) + LN_EPS)
    xhat = xc * rstd
    return xhat * g + b, xhat, rstd


def _ln_bwd(dy, xhat, rstd, g):
    dxh = dy * g
    dx = rstd * (dxh - jnp.mean(dxh, axis=-1, keepdims=True) - xhat * jnp.mean(dxh * xhat, axis=-1, keepdims=True))
    return dx, jnp.sum(dy * xhat, axis=0, keepdims=True), jnp.sum(dy, axis=0, keepdims=True)


def _post_attn_fwd(zg, attn, pt, x, wgo, wmo, wout, g1, b1, *, tm):
    T = x.shape[0]

    def body(zg_ref, at_ref, pt_ref, x_ref, wgo_ref, wmo_ref, wout_ref, g_ref, b_ref,
             yg_ref, ym_ref, mix_ref, pre_ref, h_ref, hb_ref):
        yg = _dot(zg_ref[...], wgo_ref[...])
        ym = _dot(at_ref[...], wmo_ref[...])
        mix = (_sigmoid(pt_ref[:, 0:D]) * yg + _sigmoid(pt_ref[:, D:2 * D]) * ym).astype(BF)
        pre = ALPHA * x_ref[...] + _dot(mix, wout_ref[...])
        h, _, _ = _ln_fwd(pre, g_ref[...], b_ref[...])
        yg_ref[...] = yg.astype(BF)
        ym_ref[...] = ym.astype(BF)
        mix_ref[...] = mix
        pre_ref[...] = pre
        h_ref[...] = h
        hb_ref[...] = h.astype(BF)

    full = lambda shp: pl.BlockSpec(shp, lambda i: (0,) * len(shp))
    row = lambda w: pl.BlockSpec((tm, w), lambda i: (i, 0))
    sd = lambda dt: jax.ShapeDtypeStruct((T, D), dt)
    return pl.pallas_call(
        body, name="post_attn_fwd",
        out_shape=(sd(BF), sd(BF), sd(BF), sd(F32), sd(F32), sd(BF)),
        grid=(T // tm,),
        in_specs=[row(D), row(D), row(PT_W), row(D), full((D, D)), full((D, D)), full((D, D)),
                  full((1, D)), full((1, D))],
        out_specs=(row(D),) * 6,
        compiler_params=_params(("parallel",)),
    )(zg, attn, pt, x, wgo, wmo, wout, g1, b1)


def _post_attn_bwd(dh, pre, pt, yg, ym, wgo, wmo, wout, g1, *, tm):
    T = dh.shape[0]

    def body(dh_ref, pre_ref, pt_ref, yg_ref, ym_ref, wgo_ref, wmo_ref, wout_ref, g_ref,
             dx_ref, dpreb_ref, dpt_ref, dygb_ref, dymb_ref, dzg_ref, dat_ref, dg_ref, db_ref):
        @pl.when(pl.program_id(0) == 0)
        def _():
            dg_ref[...] = jnp.zeros_like(dg_ref)
            db_ref[...] = jnp.zeros_like(db_ref)

        pre = pre_ref[...]
        mu = jnp.mean(pre, axis=-1, keepdims=True)
        xc = pre - mu
        rstd = lax.rsqrt(jnp.mean(xc * xc, axis=-1, keepdims=True) + LN_EPS)
        dpre, dg, db = _ln_bwd(dh_ref[...], xc * rstd, rstd, g_ref[...])
        dg_ref[...] += dg
        db_ref[...] += db
        dx_ref[...] = ALPHA * dpre
        dpreb = dpre.astype(BF)
        dpreb_ref[...] = dpreb
        dmix = _dot_nt(dpreb, wout_ref[...])
        sa = _sigmoid(pt_ref[:, 0:D])
        sb = _sigmoid(pt_ref[:, D:2 * D])
        dpt_ref[:, 0:D] = (dmix * yg_ref[...].astype(F32) * (sa * (1.0 - sa))).astype(BF)
        dpt_ref[:, D:2 * D] = (dmix * ym_ref[...].astype(F32) * (sb * (1.0 - sb))).astype(BF)
        dyg = (dmix * sa).astype(BF)
        dym = (dmix * sb).astype(BF)
        dygb_ref[...] = dyg
        dymb_ref[...] = dym
        dzg_ref[...] = _dot_nt(dyg, wgo_ref[...]).astype(BF)
        dat_ref[...] = _dot_nt(dym, wmo_ref[...]).astype(BF)

    full = lambda shp: pl.BlockSpec(shp, lambda i: (0,) * len(shp))
    row = lambda w: pl.BlockSpec((tm, w), lambda i: (i, 0))
    sd = lambda w, dt: jax.ShapeDtypeStruct((T, w), dt)
    return pl.pallas_call(
        body, name="post_attn_bwd",
        out_shape=(sd(D, F32), sd(D, BF), sd(PT_W, BF), sd(D, BF), sd(D, BF), sd(D, BF), sd(D, BF),
                   jax.ShapeDtypeStruct((1, D), F32), jax.ShapeDtypeStruct((1, D), F32)),
        grid=(T // tm,),
        in_specs=[row(D), row(D), row(PT_W), row(D), row(D), full((D, D)), full((D, D)), full((D, D)),
                  full((1, D))],
        out_specs=(row(D), row(D), row(PT_W), row(D), row(D), row(D), row(D), full((1, D)), full((1, D))),
        compiler_params=_params(("arbitrary",)),
    )(dh, pre, pt, yg, ym, wgo, wmo, wout, g1)


def _shift_down(u, prev, k):
    r = pltpu.roll(u, k, 0)
    p = pltpu.roll(prev, k, 0)
    head = jnp.where(_iota(p.shape, 0) < k, p, r[0:8, :])
    return jnp.concatenate([head, r[8:, :]], axis=0)


def _shift_up(u, nxt, k):
    n = u.shape[0]
    r = pltpu.roll(u, n - k, 0)
    p = pltpu.roll(nxt, 8 - k, 0)
    tail = jnp.where(_iota(p.shape, 0) >= 8 - k, p, r[n - 8:, :])
    return jnp.concatenate([r[:n - 8, :], tail], axis=0)


def _conv3(u, prev, w_ref, b_ref):
    return (w_ref[0:1, :] * _shift_down(u, prev, 2) + w_ref[1:2, :] * _shift_down(u, prev, 1)
            + w_ref[2:3, :] * u + b_ref[...])


def _ffn_up_fwd(hb, wug, wuv, cw, cb, *, S, tm, tn):
    T = hb.shape[0]
    nj, nbs = DFF // tn, S // tm

    def body(h_ref, wg_ref, wv_ref, cwg_ref, cwv_ref, cbg_ref, cbv_ref,
             ug_ref, uv_ref, ucg_ref, ucv_ref, f_ref, pg_scr, pv_scr):
        @pl.when(pl.program_id(1) % nbs == 0)
        def _():
            pg_scr[...] = jnp.zeros_like(pg_scr)
            pv_scr[...] = jnp.zeros_like(pv_scr)

        h = h_ref[...]
        ug = _dot(h, wg_ref[...])
        uv = _dot(h, wv_ref[...])
        ucg = _conv3(ug, pg_scr[...], cwg_ref, cbg_ref)
        ucv = _conv3(uv, pv_scr[...], cwv_ref, cbv_ref)
        pg_scr[...] = ug[tm - 8:, :]
        pv_scr[...] = uv[tm - 8:, :]
        ug_ref[...] = ug.astype(BF)
        uv_ref[...] = uv.astype(BF)
        ucg_ref[...] = ucg
        ucv_ref[...] = ucv
        f_ref[...] = (ucg * _sigmoid(ucg) * ucv).astype(BF)

    tile = pl.BlockSpec((tm, tn), lambda j, i: (i, j))
    return pl.pallas_call(
        body, name="ffn_up_fwd",
        out_shape=(jax.ShapeDtypeStruct((T, DFF), BF), jax.ShapeDtypeStruct((T, DFF), BF),
                   jax.ShapeDtypeStruct((T, DFF), F32), jax.ShapeDtypeStruct((T, DFF), F32),
                   jax.ShapeDtypeStruct((T, DFF), BF)),
        grid=(nj, T // tm),
        in_specs=[pl.BlockSpec((tm, D), lambda j, i: (i, 0)),
                  pl.BlockSpec((D, tn), lambda j, i: (0, j)), pl.BlockSpec((D, tn), lambda j, i: (0, j)),
                  pl.BlockSpec((3, tn), lambda j, i: (0, j)), pl.BlockSpec((3, tn), lambda j, i: (0, j + nj)),
                  pl.BlockSpec((1, tn), lambda j, i: (0, j)), pl.BlockSpec((1, tn), lambda j, i: (0, j + nj))],
        out_specs=(tile, tile, tile, tile, tile),
        scratch_shapes=[pltpu.VMEM((8, tn), F32), pltpu.VMEM((8, tn), F32)],
        compiler_params=_params(("parallel", "arbitrary")),
    )(hb, wug, wuv, cw, cw, cb, cb)


def _ffn_bwd(dpreb, wd, ug, uv, ucg, ucv, cw, *, S, tm, tn):
    T = dpreb.shape[0]
    nj, nb, nbs = DFF // tn, T // tm, S // tm
    r_, c_ = lax.broadcasted_iota(jnp.int32, (tm, tm), 0), lax.broadcasted_iota(jnp.int32, (tm, tm), 1)
    s1, s2 = (c_ == r_ + 1).astype(BF), (c_ == r_ + 2).astype(BF)

    def body(dp_ref, wd_ref, ug_ref, uv_ref, ucg_ref, ucv_ref, cwg_ref, cwv_ref, s1_ref, s2_ref,
             dug_ref, duv_ref, dcg_ref, dcv_ref, ng_scr, nv_scr):
        ii = pl.program_id(1)
        i = nb - 1 - ii
        tail_row = _iota((8, tn), 0)

        @pl.when(ii == 0)
        def _():
            dcg_ref[...] = jnp.zeros_like(dcg_ref)
            dcv_ref[...] = jnp.zeros_like(dcv_ref)

        @pl.when(i % nbs == nbs - 1)
        def _():
            ng_scr[...] = jnp.zeros_like(ng_scr)
            nv_scr[...] = jnp.zeros_like(nv_scr)

        df = _dot_nt(dp_ref[...], wd_ref[...])
        ucg = ucg_ref[...]
        sg = _sigmoid(ucg)
        ducg = df * ucv_ref[...] * (sg * (1.0 + ucg * (1.0 - sg)))
        ducv = df * (ucg * sg)

        def finish(duc, u_ref, w, nxt_scr, du_ref, dc_ref):
            nxt = nxt_scr[...]
            db = duc.astype(BF)

            def shifted(s_ref, k):
                r = _dot(s_ref[...], db)
                tail = jnp.where(tail_row >= 8 - k, pltpu.roll(nxt, 8 - k, 0), r[tm - 8:, :])
                return jnp.concatenate([r[:tm - 8, :], tail], axis=0)

            up1 = shifted(s1_ref, 1)
            up2 = shifted(s2_ref, 2)
            du_ref[...] = (w[2:3, :] * duc + w[1:2, :] * up1 + w[0:1, :] * up2).astype(BF)
            nxt_scr[...] = duc[0:8, :]
            u = u_ref[...].astype(F32)
            for row, z in enumerate((u * up2, u * up1, u * duc, duc)):
                dc_ref[row:row + 1, :] += jnp.sum(z, axis=0, keepdims=True)

        finish(ducg, ug_ref, cwg_ref, ng_scr, dug_ref, dcg_ref)
        finish(ducv, uv_ref, cwv_ref, nv_scr, duv_ref, dcv_ref)

    tile = pl.BlockSpec((tm, tn), lambda j, ii: (nb - 1 - ii, j))
    acc = pl.BlockSpec((8, tn), lambda j, ii: (0, j))
    return pl.pallas_call(
        body, name="ffn_bwd",
        out_shape=(jax.ShapeDtypeStruct((T, DFF), BF), jax.ShapeDtypeStruct((T, DFF), BF),
                   jax.ShapeDtypeStruct((8, DFF), F32), jax.ShapeDtypeStruct((8, DFF), F32)),
        grid=(nj, nb),
        in_specs=[pl.BlockSpec((tm, D), lambda j, ii: (nb - 1 - ii, 0)),
                  pl.BlockSpec((tn, D), lambda j, ii: (j, 0)),
                  tile, tile, tile, tile,
                  pl.BlockSpec((3, tn), lambda j, ii: (0, j)), pl.BlockSpec((3, tn), lambda j, ii: (0, j + nj)),
                  pl.BlockSpec((tm, tm), lambda j, ii: (0, 0)), pl.BlockSpec((tm, tm), lambda j, ii: (0, 0))],
        out_specs=(tile, tile, acc, acc),
        scratch_shapes=[pltpu.VMEM((8, tn), F32), pltpu.VMEM((8, tn), F32)],
        compiler_params=_params(("parallel", "arbitrary")),
    )(dpreb, wd, ug, uv, ucg, ucv, cw, cw, s1, s2)


def _down_ln2_loss(f_in, wd, h, target, g2, b2, *, tm):
    T = h.shape[0]

    def body(f_ref, wd_ref, h_ref, t_ref, g_ref, b_ref, dpb_ref, dh_ref, loss_ref, dg_ref, db_ref):
        @pl.when(pl.program_id(0) == 0)
        def _():
            loss_ref[...] = jnp.zeros_like(loss_ref)
            dg_ref[...] = jnp.zeros_like(dg_ref)
            db_ref[...] = jnp.zeros_like(db_ref)

        pre = ALPHA * h_ref[...] + _dot(f_ref[...], wd_ref[...])
        out, xhat, rstd = _ln_fwd(pre, g_ref[...], b_ref[...])
        diff = out - t_ref[...]
        loss_ref[...] += 0.5 * jnp.sum(jnp.mean(diff * diff, axis=-1, keepdims=True))
        dpre, dg, db = _ln_bwd(diff * (1.0 / D), xhat, rstd, g_ref[...])
        dg_ref[...] += dg
        db_ref[...] += db
        dpb_ref[...] = dpre.astype(BF)
        dh_ref[...] = ALPHA * dpre

    full = lambda shp: pl.BlockSpec(shp, lambda i: (0,) * len(shp))
    row = lambda w: pl.BlockSpec((tm, w), lambda i: (i, 0))
    return pl.pallas_call(
        body, name="down_ln2_loss",
        out_shape=(jax.ShapeDtypeStruct((T, D), BF), jax.ShapeDtypeStruct((T, D), F32),
                   jax.ShapeDtypeStruct((8, 128), F32), jax.ShapeDtypeStruct((1, D), F32),
                   jax.ShapeDtypeStruct((1, D), F32)),
        grid=(T // tm,),
        in_specs=[row(DFF), full((DFF, D)), row(D), row(D), full((1, D)), full((1, D))],
        out_specs=(row(D), row(D), full((8, 128)), full((1, D)), full((1, D))),
        compiler_params=_params(("arbitrary",)),
    )(f_in, wd, h, target, g2, b2)


def _adamw(parts, w, m, v, *, name):
    n, R, C = parts.shape
    tr, tc = R, C
    for cand in range(min(R, 256), 15, -1):
        if R % cand == 0 and cand % 16 == 0:
            tr = cand
            break
    if tr == R and R * C > 65536 and C % 256 == 0:
        tc = 256
    c1 = 1.0 - ADAM_B1 ** ADAM_STEP
    c2 = 1.0 - ADAM_B2 ** ADAM_STEP

    def body(p_ref, w_ref, m_ref, v_ref, g_ref, d_ref, nm_ref, nv_ref):
        g = p_ref[0].astype(F32)
        for s in range(1, n):
            g = g + p_ref[s].astype(F32)
        nm = ADAM_B1 * m_ref[...] + (1.0 - ADAM_B1) * g
        nv = ADAM_B2 * v_ref[...] + (1.0 - ADAM_B2) * (g * g)
        g_ref[...] = g
        nm_ref[...] = nm
        nv_ref[...] = nv
        d_ref[...] = -ADAM_LR * ((nm / c1) / (jnp.sqrt(nv / c2) + ADAM_EPS) + ADAM_WD * w_ref[...])

    blk = pl.BlockSpec((tr, tc), lambda i, j: (i, j))
    sd = jax.ShapeDtypeStruct((R, C), F32)
    return pl.pallas_call(
        body, name=name,
        out_shape=(sd, sd, sd, sd),
        grid=(R // tr, C // tc),
        in_specs=[pl.BlockSpec((n, tr, tc), lambda i, j: (0, i, j)), blk, blk, blk],
        out_specs=(blk, blk, blk, blk),
        compiler_params=_params(("parallel", "parallel")),
    )(parts, w, m, v)


class _Exchange:
    def __init__(self, items):
        self.items = [(src if sc else [(src, 0)], sc) for src, sc in items]
        self.arrays = [arr for srcs, _ in self.items for arr, _ in srcs]
        self.n = len(self.items)
        self.n_in = len(self.arrays)

    def out_shape(self):
        return tuple(jax.ShapeDtypeStruct((NDEV,) + (srcs[0][0].shape[1:] if sc else srcs[0][0].shape),
                                          srcs[0][0].dtype) for srcs, sc in self.items)

    def scratch(self):
        return [pltpu.SemaphoreType.DMA((self.n, NDEV - 1)), pltpu.SemaphoreType.DMA((self.n, NDEV - 1)),
                pltpu.SemaphoreType.DMA((self.n,))]

    def _emit(self, ins, outs, sems, phase):
        send_sems, recv_sems, loc_sems = sems
        x, y, c = lax.axis_index("x"), lax.axis_index("y"), lax.axis_index("c")
        me = 4 * x + 2 * y + c
        flip = lambda p, d: 1 - p if d else p

        def inside(p, lo, n):
            return None if (lo, n) == (0, NDEV) else jnp.logical_and(p >= lo, p < lo + n)

        def when(cond, fn):
            if cond is None:
                fn()
            else:
                pl.when(cond)(fn)

        pos = 0
        for a, (srcs, sc) in enumerate(self.items):
            refs = ins[pos:pos + len(srcs)]
            pos += len(srcs)
            ranges = [(lo, arr.shape[0]) if sc else (0, NDEV) for arr, lo in srcs]
            mine = [inside(me, lo, n) for lo, n in ranges]
            i_receive = None if None in mine else functools.reduce(jnp.logical_or, mine)
            for ref, (lo, n), cond in zip(refs, ranges, mine):
                def local(ref=ref, lo=lo):
                    cp = pltpu.make_async_copy(ref.at[me - lo] if sc else ref, outs[a].at[me], loc_sems.at[a])
                    cp.start() if phase == 0 else cp.wait()
                if phase != 1:
                    when(cond, local)
            for k in range(1, NDEV):
                px, py, pc = flip(x, k & 4), flip(y, k & 2), flip(c, k & 1)
                peer = 4 * px + 2 * py + pc
                mk = functools.partial(pltpu.make_async_remote_copy,
                                       send_sem=send_sems.at[a, k - 1], recv_sem=recv_sems.at[a, k - 1],
                                       device_id=(px, py, pc), device_id_type=MESH_ID)
                if phase == 1:
                    def arrival(mk=mk, peer=peer):
                        mk(src_ref=refs[0].at[0] if sc else refs[0], dst_ref=outs[a].at[peer]).wait_recv()
                    when(i_receive, arrival)
                    continue
                for ref, (lo, n) in zip(refs, ranges):
                    def send(mk=mk, ref=ref, lo=lo, peer=peer):
                        cp = mk(src_ref=ref.at[peer - lo] if sc else ref, dst_ref=outs[a].at[me])
                        cp.start() if phase == 0 else cp.wait_send()
                    when(inside(peer, lo, n), send)

    def start(self, ins, outs, sems):
        self._emit(ins, outs, sems, 0)

    def wait(self, ins, outs, sems):
        self._emit(ins, outs, sems, 1)
        self._emit(ins, outs, sems, 2)


def _call(body, *, name, grid, in_specs, out_specs, out_shape, args, scratch_shapes=(), sem=None, ride=None):
    if ride is None:
        return pl.pallas_call(body, name=name, grid=grid, in_specs=list(in_specs), out_specs=tuple(out_specs),
                              out_shape=tuple(out_shape), scratch_shapes=list(scratch_shapes),
                              compiler_params=_params(sem))(*args)
    n_in, n_out, n_scr, ne, ne_in = len(args), len(out_shape), len(scratch_shapes), ride.n, ride.n_in

    def ride_body(*refs):
        ins, ex_in = refs[:n_in], refs[n_in:n_in + ne_in]
        o0 = n_in + ne_in
        outs, ex_out = refs[o0:o0 + n_out], refs[o0 + n_out:o0 + n_out + ne]
        scr = refs[o0 + n_out + ne:o0 + n_out + ne + n_scr]
        sems = refs[o0 + n_out + ne + n_scr:]
        first = functools.reduce(jnp.logical_and, [pl.program_id(d) == 0 for d in range(len(grid))])
        last = functools.reduce(jnp.logical_and, [pl.program_id(d) == grid[d] - 1 for d in range(len(grid))])

        @pl.when(first)
        def _():
            ride.start(ex_in, ex_out, sems)

        body(*ins, *outs, *scr)

        @pl.when(last)
        def _():
            ride.wait(ex_in, ex_out, sems)

    anyspec = pl.BlockSpec(memory_space=pl.ANY)
    res = pl.pallas_call(
        ride_body, name=name, grid=grid,
        in_specs=list(in_specs) + [anyspec] * ne_in,
        out_specs=tuple(out_specs) + (anyspec,) * ne,
        out_shape=tuple(out_shape) + ride.out_shape(),
        scratch_shapes=list(scratch_shapes) + ride.scratch(),
        compiler_params=_params(("arbitrary",) * len(grid)),
    )(*args, *ride.arrays)
    return tuple(res[:n_out]), tuple(res[n_out:])


def _gather_two_level(arrays, *, name):
    n = len(arrays)

    def body(*refs):
        ins, outs = refs[:n], refs[n:2 * n]
        send_sems, recv_sems, loc_sems = refs[2 * n:]
        x, y, c = lax.axis_index("x"), lax.axis_index("y"), lax.axis_index("c")
        sibling = (x, y, 1 - c)
        chips = [(1 - x, y), (x, 1 - y), (1 - x, 1 - y)]
        idx = lambda px, py, pc: 4 * px + 2 * py + pc
        me = idx(x, y, c)

        def copy(a, k, block, to, src=None):
            return pltpu.make_async_remote_copy(
                src_ref=outs[a].at[block] if src is None else src, dst_ref=outs[a].at[block],
                send_sem=send_sems.at[a, k], recv_sem=recv_sems.at[a, k], device_id=to, device_id_type=MESH_ID)

        local = [pltpu.make_async_copy(ins[a], outs[a].at[me], loc_sems.at[a]) for a in range(n)]
        sent = []
        for a in range(n):
            sent.append(copy(a, 0, me, sibling, src=ins[a]))
            sent += [copy(a, 1 + j, me, (*chip, c), src=ins[a]) for j, chip in enumerate(chips)]
        for cp in local + sent:
            cp.start()
        for j, chip in enumerate(chips):
            for a in range(n):
                copy(a, 1 + j, idx(*chip, c), sibling).wait_recv()
                passed = copy(a, 4 + j, idx(*chip, c), sibling)
                passed.start()
                sent.append(passed)
        for a in range(n):
            copy(a, 0, idx(x, y, 1 - c), sibling).wait_recv()
            for j, chip in enumerate(chips):
                copy(a, 4 + j, idx(*chip, 1 - c), sibling).wait_recv()
        for cp in sent:
            cp.wait_send()
        for cp in local:
            cp.wait()

    anyspec = pl.BlockSpec(memory_space=pl.ANY)
    return pl.pallas_call(
        body, name=name,
        out_shape=tuple(jax.ShapeDtypeStruct((NDEV,) + a.shape, a.dtype) for a in arrays),
        in_specs=[anyspec] * n, out_specs=(anyspec,) * n,
        scratch_shapes=[pltpu.SemaphoreType.DMA((n, NDEV - 1)), pltpu.SemaphoreType.DMA((n, NDEV - 1)),
                        pltpu.SemaphoreType.DMA((n,))],
    )(*arrays)


def _tri_consts():
    r = lax.broadcasted_iota(jnp.int32, (GC, GC), 0)
    c = lax.broadcasted_iota(jnp.int32, (GC, GC), 1)
    return (r >= c).astype(BF), (r <= c).astype(BF)


def _local_step(x, positions, target, w, hooks=None):
    g = {}

    def run(host, fn, *a, **kw):
        h = None if hooks is None else hooks.get(host)
        if h is None:
            return fn(*a, **kw)
        out, received = fn(*a, ride=_Exchange(h[0](w, g)), **kw)
        h[1](received, w, g)
        return out

    nseq, S, _ = x.shape
    T = nseq * S
    tm = min(256, S)
    tq = min(512, S)
    x2 = x.reshape(T, D)
    pos = positions.reshape(T, 1)
    half = ROPE // 2
    inv = THETA ** (-jnp.arange(half, dtype=F32) / half)
    invf = jnp.concatenate([inv, inv, jnp.zeros((64,), F32)]).reshape(1, 128)
    ltri, utri = _tri_consts()

    pt, xb = _matmul(x2, w["w_tt"], "nt", name="proj_t", tm=1024, tn=1024, tk=1024, emit_a=True)
    pg = run("proj_g", _matmul, xb, w["w_gt"], "nt", name="proj_g", tm=1024, tn=640, tk=1024)
    pm = _matmul(xb, w["w_mt"], "nt", name="proj_m", tm=1024, tn=768, tk=1024)
    o, zg, states = _gla_fwd(pg, w["wg"], w["bg"], w["gn"], ltri, nseq=nseq, S=S, tm=tm)
    qc, kc, v = _mla_prep_fwd(pm, pos, invf, w["gq"], w["gkv"], w["wuq"], w["wukv"], tm=tm)
    attn, lse = run("flash_fwd", _flash_fwd, qc, kc, v, nseq=nseq, S=S, tq=tq)
    yg, ym, mix, pre1, h1, h1b = _post_attn_fwd(zg, attn, pt, x2, w["wgo"], w["wmo"], w["wout"],
                                                w["g1"], w["b1"], tm=tm)
    ug, uv, ucg, ucv, f_in = _ffn_up_fwd(h1b, w["wug"], w["wuv"], w["cw"], w["cb"], S=S, tm=tm, tn=1408)
    dpre2b, dh1, loss8, dg2, db2 = _down_ln2_loss(f_in, w["wd"], h1, target.reshape(T, D), w["g2"], w["b2"], tm=tm)

    dug, duv, dcg, dcv = _ffn_bwd(dpre2b, w["wd"], ug, uv, ucg, ucv, w["cw"], S=S, tm=tm, tn=1408)
    g["g2"], g["b2"], g["loss"] = dg2, db2, loss8[0:1, 0:1]
    g["cw"] = jnp.concatenate([dcg[0:3], dcv[0:3]], axis=1)
    g["cb"] = jnp.concatenate([dcg[3:4], dcv[3:4]], axis=1)
    g["wd"] = _matmul(f_in, dpre2b, "tn", name="dw_down", out_dtype=BF, tm=1408, tn=1024, tk=1024)
    g["wugt"] = _matmul(dug, h1b, "tn", name="dw_up_g", out_dtype=BF, tm=1408, tn=1024, tk=1024)
    g["wuvt"] = _matmul(duv, h1b, "tn", name="dw_up_v", out_dtype=BF, tm=1408, tn=1024, tk=1024)
    dh1 = _matmul(dug, w["wugt"], "nn", name="dh1_g", c_in=dh1, tm=1024, tn=1024, tk=1408)
    dh1 = _matmul(duv, w["wuvt"], "nn", name="dh1_v", c_in=dh1, tm=1024, tn=1024, tk=1408)
    dx, dpre1b, dpt, dygb, dymb, dzg, dattn, dg1, db1 = _post_attn_bwd(
        dh1, pre1, pt, yg, ym, w["wgo"], w["wmo"], w["wout"], w["g1"], tm=tm)
    g["g1"], g["b1"] = dg1, db1
    g["wout"] = _matmul(mix, dpre1b, "tn", name="dw_out", out_dtype=BF, tm=1024, tn=1024, tk=1024)
    g["wgo"] = _matmul(zg, dygb, "tn", name="dw_gla_o", out_dtype=BF, tm=1024, tn=1024, tk=1024)
    g["wmo"] = _matmul(attn, dymb, "tn", name="dw_mla_o", out_dtype=BF, tm=1024, tn=1024, tk=1024)
    dqc, dkc, dv = run("flash_bwd", _flash_bwd, qc, kc, v, attn, dattn, lse, nseq=nseq, S=S, tq=tq)
    dpm, g["wuq"], g["wukv"], g["gq"], g["gkv"] = _mla_prep_bwd(
        pm, pos, invf, w["gq"], w["gkv"], w["wuq"], w["wukv"], dqc, dkc, dv, tm=tm)
    g["w_mt"] = _matmul(dpm, xb, "tn", name="dw_in_m", out_dtype=BF, tm=768, tn=1024, tk=1024)
    g["w_tt"] = _matmul(dpt, xb, "tn", name="dw_in_t", out_dtype=BF, tm=1024, tn=1024, tk=1024)
    dpg, g["wg"], g["bg"], g["gn"] = run("gla_bwd", _gla_bwd, pg, w["wg"], w["bg"], w["gn"], ltri, utri, o, states,
                                         dzg, nseq=nseq, S=S, tm=tm)
    g["w_gt"] = _matmul(dpg, xb, "tn", name="dw_in_g", out_dtype=BF, tm=640, tn=1024, tk=1024)
    dx = run("dx", _matmul_sum, dx, [(dpg, w["w_gt"], 640), (dpm, w["w_mt"], 768)], name="dx_gm")
    dx = _matmul_sum(dx, [(dpt, w["w_tt"], 1024)], name="dx_t")
    return loss8[0, 0], dx.reshape(nseq, S, D), g


_IN_SPLITS = (512, 512, 1024, 16, 1024, 384, 256, 64, 1024, 1024)


def _w_in_to_groups(wt):
    offs = [0]
    for s in _IN_SPLITS:
        offs.append(offs[-1] + s)
    q, k, v, r, og, cq, ckv, kr, ga, gb = [wt[offs[i]:offs[i + 1]] for i in range(10)]
    z = lambda n: jnp.zeros((n, wt.shape[1]), wt.dtype)
    return (jnp.concatenate([q, k, v, og, r, z(112)], axis=0),
            jnp.concatenate([cq, kr, z(64), ckv], axis=0),
            jnp.concatenate([ga, gb], axis=0))


def _groups_to_w_in(g_g, g_m, g_t):
    q, k, v, og, r = g_g[0:512], g_g[512:1024], g_g[1024:2048], g_g[2048:3072], g_g[3072:3088]
    cq, kr, ckv = g_m[0:384], g_m[384:448], g_m[512:768]
    return jnp.concatenate([q, k, v, r, og, cq, ckv, kr, g_t], axis=0)


_W_IN_LO = 5
_W_IN_SPLIT = _W_IN_LO * 730 - 3472


def _w_in_rows_lo(g_g, g_m):
    q, k, v, og, r = g_g[0:512], g_g[512:1024], g_g[1024:2048], g_g[2048:3072], g_g[3072:3088]
    return jnp.concatenate([q, k, v, r, og, g_m[0:384], g_m[512:768]], axis=0)[:3472 + _W_IN_SPLIT]


def _w_in_rows_hi(g_m, g_t):
    return jnp.concatenate([g_m[512:768], g_m[384:448], g_t], axis=0)[_W_IN_SPLIT:]


def _uq_to_kernel(wuq):
    w3 = wuq.reshape(MQR, MH, NOPE + ROPE)
    rope = jnp.concatenate([w3[:, :, NOPE:], jnp.zeros((MQR, MH, 64), wuq.dtype)], axis=2)
    return jnp.concatenate([w3[:, :, :NOPE].reshape(MQR, MH * 128), rope.reshape(MQR, MH * 128)], axis=1)


def _uq_from_kernel(g):
    nope = g[:, :1024].reshape(MQR, MH, 128)
    rope = g[:, 1024:].reshape(MQR, MH, 128)[:, :, :ROPE]
    return jnp.concatenate([nope, rope], axis=2)


def _ukv_to_kernel(wukv):
    w3 = wukv.reshape(MKR, MH, NOPE + MV)
    return jnp.concatenate([w3[:, :, :NOPE].reshape(MKR, MH * 128), w3[:, :, NOPE:].reshape(MKR, MH * 128)], axis=1)


def _ukv_from_kernel(g):
    return jnp.concatenate([g[:, :1024].reshape(MKR, MH, 128), g[:, 1024:].reshape(MKR, MH, 128)], axis=2)


def _cols_gathered(a):
    return a.transpose(1, 0, 2).reshape(a.shape[1], NDEV * a.shape[2])


def _cols_scattered(a):
    R = a.shape[0]
    return a.reshape(R, NDEV, a.shape[1] // NDEV).transpose(1, 0, 2)


_SMALL = (("gla_b_gate", 512), ("gla_norm_g", 256), ("mla_q_norm_g", 384), ("mla_kv_norm_g", 256),
          ("ln1_g", 1024), ("ln1_b", 1024), ("conv_b", 5632), ("ln2_g", 1024), ("ln2_b", 1024))
_SMALL_ROWS = 88
_SMALL_USED = sum(sz for _, sz in _SMALL)


def _pack_small(d):
    flat = jnp.concatenate([d[n].reshape(-1) for n, _ in _SMALL] + ([d['loss'].reshape(-1)] if 'loss' in d else []))
    return jnp.pad(flat, (0, _SMALL_ROWS * 128 - flat.shape[0])).reshape(_SMALL_ROWS, 128)


def _unpack_small(a):
    flat = a.reshape(-1)
    out, off = {}, 0
    for n, sz in _SMALL:
        out[n] = flat[off:off + sz].reshape(1, sz)
        off += sz
    return out


_NAMES = ['w_in', 'gla_w_gate_up', 'gla_b_gate', 'gla_norm_g', 'w_gla_o', 'mla_q_norm_g', 'mla_w_uq',
          'mla_kv_norm_g', 'mla_w_ukv', 'w_mla_o', 'w_out', 'ln1_g', 'ln1_b', 'w_up', 'conv_w', 'conv_b',
          'w_down', 'ln2_g', 'ln2_b']
_SHARDED = ['w_in', 'w_up', 'w_down', 'w_gla_o', 'w_mla_o', 'w_out', 'mla_w_uq', 'mla_w_ukv', 'gla_w_gate_up',
            'conv_w']


def kernel(x, positions, w_in, gla_w_gate_up, gla_b_gate, gla_norm_g, w_gla_o, mla_q_norm_g, mla_w_uq, mla_kv_norm_g, mla_w_ukv, w_mla_o, w_out, ln1_g, ln1_b, w_up, conv_w, conv_b, w_down, ln2_g, ln2_b, loss_target, m_w_in, m_gla_w_gate_up, m_gla_b_gate, m_gla_norm_g, m_w_gla_o, m_mla_q_norm_g, m_mla_w_uq, m_mla_kv_norm_g, m_mla_w_ukv, m_w_mla_o, m_w_out, m_ln1_g, m_ln1_b, m_w_up, m_conv_w, m_conv_b, m_w_down, m_ln2_g, m_ln2_b, v_w_in, v_gla_w_gate_up, v_gla_b_gate, v_gla_norm_g, v_w_gla_o, v_mla_q_norm_g, v_mla_w_uq, v_mla_kv_norm_g, v_mla_w_ukv, v_w_mla_o, v_w_out, v_ln1_g, v_ln1_b, v_w_up, v_conv_w, v_conv_b, v_w_down, v_ln2_g, v_ln2_b):
    W = dict(w_in=w_in, gla_w_gate_up=gla_w_gate_up, gla_b_gate=gla_b_gate, gla_norm_g=gla_norm_g, w_gla_o=w_gla_o, mla_q_norm_g=mla_q_norm_g, mla_w_uq=mla_w_uq, mla_kv_norm_g=mla_kv_norm_g, mla_w_ukv=mla_w_ukv, w_mla_o=w_mla_o, w_out=w_out, ln1_g=ln1_g, ln1_b=ln1_b, w_up=w_up, conv_w=conv_w, conv_b=conv_b, w_down=w_down, ln2_g=ln2_g, ln2_b=ln2_b)
    M = dict(w_in=m_w_in, gla_w_gate_up=m_gla_w_gate_up, gla_b_gate=m_gla_b_gate, gla_norm_g=m_gla_norm_g, w_gla_o=m_w_gla_o, mla_q_norm_g=m_mla_q_norm_g, mla_w_uq=m_mla_w_uq, mla_kv_norm_g=m_mla_kv_norm_g, mla_w_ukv=m_mla_w_ukv, w_mla_o=m_w_mla_o, w_out=m_w_out, ln1_g=m_ln1_g, ln1_b=m_ln1_b, w_up=m_w_up, conv_w=m_conv_w, conv_b=m_conv_b, w_down=m_w_down, ln2_g=m_ln2_g, ln2_b=m_ln2_b)
    V = dict(w_in=v_w_in, gla_w_gate_up=v_gla_w_gate_up, gla_b_gate=v_gla_b_gate, gla_norm_g=v_gla_norm_g, w_gla_o=v_w_gla_o, mla_q_norm_g=v_mla_q_norm_g, mla_w_uq=v_mla_w_uq, mla_kv_norm_g=v_mla_kv_norm_g, mla_w_ukv=v_mla_w_ukv, w_mla_o=v_w_mla_o, w_out=v_w_out, ln1_g=v_ln1_g, ln1_b=v_ln1_b, w_up=v_w_up, conv_w=v_conv_w, conv_b=v_conv_b, w_down=v_w_down, ln2_g=v_ln2_g, ln2_b=v_ln2_b)

    tshard = lambda d, n: d[n][0].T
    shard = lambda n: (W[n][0].astype(BF), False)
    first = ['w_in', 'mla_w_uq', 'mla_w_ukv', 'gla_w_gate_up']
    G = dict(zip(first, _gather_two_level(
        [tshard(W, 'w_in').astype(BF)] + [shard(n)[0] for n in first[1:]], name="gather_w0")))
    w_gt, w_mt, w_tt = _w_in_to_groups(G['w_in'].reshape(NDEV * 730, D))
    kw = dict(
        w_gt=w_gt, w_mt=w_mt, w_tt=w_tt,
        wg=jnp.pad(_cols_gathered(G['gla_w_gate_up']), ((0, 128 - GR), (0, 0))), bg=W['gla_b_gate'],
        gn=W['gla_norm_g'], gq=W['mla_q_norm_g'], gkv=W['mla_kv_norm_g'],
        wuq=_uq_to_kernel(_cols_gathered(G['mla_w_uq'])), wukv=_ukv_to_kernel(_cols_gathered(G['mla_w_ukv'])),
        g1=W['ln1_g'], b1=W['ln1_b'], g2=W['ln2_g'], b2=W['ln2_b'], cb=W['conv_b'],
    )
    received = {}

    def got_out_proj(ex, w, g):
        w.update(wgo=ex[0].reshape(D, D), wmo=ex[1].reshape(D, D), wout=ex[2].reshape(D, D))

    def got_ffn(ex, w, g):
        w_upt = ex[0].reshape(2 * DFF, D)
        w.update(wugt=w_upt[:DFF], wuvt=w_upt[DFF:], wug=w_upt[:DFF].T, wuv=w_upt[DFF:].T,
                 wd=ex[1].reshape(DFF, D), cw=_cols_gathered(ex[2]))

    slab = lambda a, lo=0: ([(a.astype(BF), lo)], True)
    rows = lambda a, n=NDEV: a.reshape(n, a.shape[0] // n, a.shape[1])

    def keep(names):
        return lambda ex, w, g: received.update(zip(names, ex))

    def small_grads(g):
        return _pack_small(dict(gla_b_gate=g['bg'], gla_norm_g=g['gn'], mla_q_norm_g=g['gq'], mla_kv_norm_g=g['gkv'],
                                ln1_g=g['g1'], ln1_b=g['b1'], conv_b=g['cb'], ln2_g=g['g2'], ln2_b=g['b2'],
                                loss=g['loss']))

    hooks = {
        "proj_g": (lambda w, g: [shard('w_gla_o'), shard('w_mla_o'), shard('w_out')], got_out_proj),
        "flash_fwd": (lambda w, g: [(tshard(W, 'w_up').astype(BF), False), shard('w_down'), (W['conv_w'][0], False)],
                      got_ffn),
        "flash_bwd": (lambda w, g: [slab(rows(g['wd'])),
                                    ([(rows(g['wugt'], 4), 0), (rows(g['wuvt'], 4), 4)], True),
                                    slab(rows(g['wout'])), slab(rows(g['wgo'])), slab(rows(g['wmo']))],
                      keep(['w_down', 'w_up', 'w_out', 'w_gla_o', 'w_mla_o'])),
        "gla_bwd": (lambda w, g: [slab(_uq_from_kernel(g['wuq']).transpose(1, 0, 2)),
                                  slab(_ukv_from_kernel(g['wukv']).transpose(1, 0, 2)),
                                  slab(rows(_w_in_rows_hi(g['w_mt'], g['w_tt']), NDEV - _W_IN_LO), _W_IN_LO)],
                    keep(['mla_w_uq', 'mla_w_ukv', 'w_in_hi'])),
        "dx": (lambda w, g: [slab(rows(_w_in_rows_lo(g['w_gt'], g['w_mt']), _W_IN_LO)),
                             ([(_cols_scattered(g['wg'][:GR]), 0)], True), ([(_cols_scattered(g['cw']), 0)], True),
                             (small_grads(g), False)],
               keep(['w_in_lo', 'gla_w_gate_up', 'conv_w', 'small'])),
    }

    _, grad_x, _ = _local_step(x, positions, loss_target, kw, hooks)

    grads, deltas, new_m, new_v = {}, {}, {}, {}
    small_parts = received['small']
    loss = jnp.sum(small_parts.reshape(NDEV, -1)[:, _SMALL_USED])
    me = 4 * lax.axis_index("x") + 2 * lax.axis_index("y") + lax.axis_index("c")
    received['w_in'] = jnp.where(me >= _W_IN_LO, received['w_in_hi'], received['w_in_lo'])
    for n in _SHARDED:
        shp = W[n].shape
        if n in ('w_in', 'w_up'):
            out = _adamw(received[n], tshard(W, n), tshard(M, n), tshard(V, n), name="adamw_" + n)
            grads[n], deltas[n], new_m[n], new_v[n] = [t.T.reshape(shp) for t in out]
            continue
        out = _adamw(received[n], W[n][0], M[n][0], V[n][0], name="adamw_" + n)
        grads[n], deltas[n], new_m[n], new_v[n] = [t.reshape(shp) for t in out]
    out = _adamw(small_parts, _pack_small(W), _pack_small(M), _pack_small(V), name="adamw_small")
    for dst, packed in zip((grads, deltas, new_m, new_v), out):
        dst.update(_unpack_small(packed))

    return (loss, grad_x, *[grads[n] for n in _NAMES], *[deltas[n] for n in _NAMES],
            *[new_m[n] for n in _NAMES], *[new_v[n] for n in _NAMES])
```

```python
import functools

import jax
import jax.numpy as jnp
from jax import lax
from jax.experimental import pallas as pl
from jax.experimental.pallas import tpu as pltpu

F32 = jnp.float32
BF = jnp.bfloat16

D = 1024
GH, GDK, GDV, GR, GTAU, GC = 4, 128, 256, 16, 16.0, 64
MH, MQR, MKR, NOPE, ROPE, MV = 8, 384, 256, 128, 64, 128
THETA = 10000.0
DFF = 2816
ALPHA = 2.0 ** 0.25
LN_EPS = 1e-5
RMS_EPS = 1e-6
NDEV = 8
ADAM_LR, ADAM_B1, ADAM_B2, ADAM_EPS, ADAM_WD, ADAM_STEP = 0.001, 0.9, 0.999, 1e-08, 0.01, 10

PG_W = 3200
PM_W = 768
PT_W = 2048
NEG = -1e30
MESH_ID = pl.DeviceIdType.MESH
VMEM_MB = 1024 * 1024


def _params(sem, vmem_mb=48):
    return pltpu.CompilerParams(dimension_semantics=sem, vmem_limit_bytes=vmem_mb * VMEM_MB)


def _dot(a, b):
    return lax.dot_general(a, b, (((1,), (0,)), ((), ())), preferred_element_type=F32)


def _dot_nt(a, b):
    return lax.dot_general(a, b, (((1,), (1,)), ((), ())), preferred_element_type=F32)


def _dot_tn(a, b):
    return lax.dot_general(a, b, (((0,), (0,)), ((), ())), preferred_element_type=F32)


def _iota(shape, dim):
    return lax.broadcasted_iota(jnp.int32, shape, dim)


FLASH_HP = 2
FLASH_HP_FWD = 4
QK_SCALE = (NOPE + ROPE) ** -0.5
LOG2E = 1.4426950408889634
QK_SCALE_LOG2 = QK_SCALE * LOG2E


def _sigmoid(x):
    return 0.5 * jnp.tanh(0.5 * x) + 0.5


def _tri_mm(tri_bf, x):
    hi = x.astype(BF)
    r1 = x - hi.astype(F32)
    mid = r1.astype(BF)
    lo = (r1 - mid.astype(F32)).astype(BF)
    return _dot(tri_bf, hi) + _dot(tri_bf, mid) + _dot(tri_bf, lo)


def _matmul(a, b, mode, *, name, c_in=None, out_dtype=F32, tm=512, tn=512, tk=512, ride=None, emit_a=False):
    if mode == "nn":
        (M, K), (_, N) = a.shape, b.shape
    elif mode == "nt":
        (M, K), (N, _) = a.shape, b.shape
    else:
        (K, M), (_, N) = a.shape, b.shape
    tm, tn, tk = min(tm, M), min(tn, N), min(tk, K)
    assert M % tm == 0 and N % tn == 0 and K % tk == 0, (name, M, N, K, tm, tn, tk)
    nk = K // tk
    assert not emit_a or (nk == 1 and mode != "tn" and c_in is None and ride is None)
    dot = {"nn": _dot, "nt": _dot_nt, "tn": _dot_tn}[mode]

    def body(*refs):
        if emit_a:
            a_ref, b_ref, o_ref, xa_ref, acc_ref = refs
        elif c_in is None:
            a_ref, b_ref, o_ref, acc_ref = refs
        else:
            a_ref, b_ref, c_ref, o_ref, acc_ref = refs
        k = pl.program_id(2)

        @pl.when(k == 0)
        def _():
            if c_in is None:
                acc_ref[...] = jnp.zeros_like(acc_ref)
            else:
                acc_ref[...] = c_ref[...].astype(F32)

        if emit_a:
            @pl.when(pl.program_id(1) == 0)
            def _():
                xa_ref[...] = a_ref[...].astype(BF)

        acc_ref[...] += dot(a_ref[...].astype(BF), b_ref[...].astype(BF))

        @pl.when(k == nk - 1)
        def _():
            o_ref[...] = acc_ref[...].astype(out_dtype)

    if mode == "tn":
        a_spec = pl.BlockSpec((tk, tm), lambda i, j, k: (k, i))
    else:
        a_spec = pl.BlockSpec((tm, tk), lambda i, j, k: (i, k))
    if mode == "nt":
        b_spec = pl.BlockSpec((tn, tk), lambda i, j, k: (j, k))
    else:
        b_spec = pl.BlockSpec((tk, tn), lambda i, j, k: (k, j))
    in_specs = [a_spec, b_spec]
    args = [a, b]
    if c_in is not None:
        in_specs.append(pl.BlockSpec((tm, tn), lambda i, j, k: (i, j)))
        args.append(c_in)
    out_shape = (jax.ShapeDtypeStruct((M, N), out_dtype),)
    out_specs = (pl.BlockSpec((tm, tn), lambda i, j, k: (i, j)),)
    if emit_a:
        out_shape += (jax.ShapeDtypeStruct((M, K), BF),)
        out_specs += (pl.BlockSpec((tm, tk), lambda i, j, k: (i, k)),)
    res = _call(
        body, name=name, out_shape=out_shape, grid=(M // tm, N // tn, nk), in_specs=in_specs, out_specs=out_specs,
        scratch_shapes=[pltpu.VMEM((tm, tn), F32)],
        sem=("parallel", "arbitrary", "arbitrary"), args=args, ride=ride)
    if emit_a:
        return res[0], res[1]
    return res[0] if ride is None else (res[0][0], res[1])


def _matmul_sum(c_in, parts, *, name, tm=1024, ride=None):
    M, N = c_in.shape
    tm = min(tm, M)
    n_p = len(parts)
    counts = [a.shape[1] // tk for a, _, tk in parts]
    starts = [sum(counts[:p]) for p in range(n_p)]
    nk = sum(counts)

    def body(*refs):
        a_refs, w_refs = refs[:n_p], refs[n_p:2 * n_p]
        c_ref, o_ref, acc_ref = refs[2 * n_p:]
        k = pl.program_id(1)

        @pl.when(k == 0)
        def _():
            acc_ref[...] = c_ref[...]

        for p in range(n_p):
            @pl.when(jnp.logical_and(k >= starts[p], k < starts[p] + counts[p]))
            def _(p=p):
                acc_ref[...] += _dot(a_refs[p][...].astype(BF), w_refs[p][...].astype(BF))

        @pl.when(k == nk - 1)
        def _():
            o_ref[...] = acc_ref[...]

    def kidx(p):
        return lambda k: jnp.clip(k - starts[p], 0, counts[p] - 1)

    in_specs = [pl.BlockSpec((tm, tk), lambda i, k, f=kidx(p): (i, f(k))) for p, (_, _, tk) in enumerate(parts)]
    in_specs += [pl.BlockSpec((tk, N), lambda i, k, f=kidx(p): (f(k), 0)) for p, (_, _, tk) in enumerate(parts)]
    in_specs.append(pl.BlockSpec((tm, N), lambda i, k: (i, 0)))
    res = _call(
        body, name=name, out_shape=(jax.ShapeDtypeStruct((M, N), F32),), grid=(M // tm, nk),
        in_specs=in_specs, out_specs=(pl.BlockSpec((tm, N), lambda i, k: (i, 0)),),
        scratch_shapes=[pltpu.VMEM((tm, N), F32)], sem=("parallel", "arbitrary"),
        args=[a for a, _, _ in parts] + [w for _, w, _ in parts] + [c_in], ride=ride)
    return res[0] if ride is None else (res[0][0], res[1])


def _gla_gate(pg_ref, rows, wg_ref, bg_ref):
    r = pg_ref[rows, 3072:3200].astype(BF)
    logit = _dot(r, wg_ref[...]) + bg_ref[...]
    la = (jnp.minimum(logit, 0.0) - jnp.log(1.0 + jnp.exp(-jnp.abs(logit)))) * (1.0 / GTAU)
    return r, logit, la


def _gla_fwd(pg, wg, bg, gn, ltri, *, nseq, S, tm):
    T = pg.shape[0]
    nb, nc = S // tm, tm // GC
    qscale = GDK ** -0.5

    def body(pg_ref, wg_ref, bg_ref, gn_ref, l_ref, o_ref, zg_ref, st_ref, st_scr):
        @pl.when(pl.program_id(1) == 0)
        def _():
            st_scr[...] = jnp.zeros_like(st_scr)

        ltri_v = l_ref[...]
        causal = _iota((GC, GC), 0) >= _iota((GC, GC), 1)
        last_row = _iota((GC, GDK), 0) == GC - 1
        g = gn_ref[...]

        def chunk(c, carry):
            rows = pl.ds(pl.multiple_of(c * GC, GC), GC)
            _, _, la = _gla_gate(pg_ref, rows, wg_ref, bg_ref)
            b = _tri_mm(ltri_v, la)
            hs = range(GH)
            v, q_in, k_st, dec, st, a_raw, o_st, kv = [], [], [], [], [], [], [], []
            for h in hs:
                q = pg_ref[rows, h * GDK:(h + 1) * GDK]
                k = pg_ref[rows, 512 + h * GDK:512 + (h + 1) * GDK]
                v.append(pg_ref[rows, 1024 + h * GDV:1024 + (h + 1) * GDV].astype(BF))
                bh = b[:, h * GDK:(h + 1) * GDK]
                bl = jnp.sum(jnp.where(last_row, bh, 0.0), axis=0, keepdims=True)
                q_in.append((q * (qscale * jnp.exp(bh))).astype(BF))
                k_in = (k * jnp.exp(-bh)).astype(BF)
                k_st.append((k * jnp.exp(bl - bh)).astype(BF))
                dec.append(jnp.exp(bl))
                st.append(st_scr[h])
                st_ref[c, h] = st[h]
                a_raw.append(_dot_nt(q_in[h], k_in))
            for h in hs:
                o_st.append(_dot_nt(q_in[h], st[h].astype(BF)))
                kv.append(_dot_tn(v[h], k_st[h]))
            att = [jnp.where(causal, a_raw[h], 0.0).astype(BF) for h in hs]
            o = [_dot(att[h], v[h]) + o_st[h] for h in hs]
            for h in hs:
                st_scr[h] = st[h] * dec[h] + kv[h]
                og = pg_ref[rows, 2048 + h * GDV:2048 + (h + 1) * GDV]
                rstd = lax.rsqrt(jnp.mean(o[h] * o[h], axis=-1, keepdims=True) + RMS_EPS)
                o_ref[rows, h * GDV:(h + 1) * GDV] = o[h]
                zg_ref[rows, h * GDV:(h + 1) * GDV] = (o[h] * rstd * g * (og * _sigmoid(og))).astype(BF)
            return carry

        lax.fori_loop(0, nc, chunk, 0, unroll=True)

    full = lambda shp: pl.BlockSpec(shp, lambda b_, i: (0,) * len(shp))
    return pl.pallas_call(
        body, name="gla_fwd",
        out_shape=(jax.ShapeDtypeStruct((T, GH * GDV), F32),
                   jax.ShapeDtypeStruct((T, GH * GDV), BF),
                   jax.ShapeDtypeStruct((T // GC, GH, GDV, GDK), F32)),
        grid=(nseq, nb),
        in_specs=[pl.BlockSpec((tm, PG_W), lambda b_, i: (b_ * nb + i, 0)),
                  full((128, 512)), full((1, 512)), full((1, GDV)), full((GC, GC))],
        out_specs=(pl.BlockSpec((tm, GH * GDV), lambda b_, i: (b_ * nb + i, 0)),
                   pl.BlockSpec((tm, GH * GDV), lambda b_, i: (b_ * nb + i, 0)),
                   pl.BlockSpec((nc, GH, GDV, GDK), lambda b_, i: (b_ * nb + i, 0, 0, 0))),
        scratch_shapes=[pltpu.VMEM((GH, GDV, GDK), F32)],
        compiler_params=_params(("parallel", "arbitrary")),
    )(pg, wg, bg, gn, ltri)


def _gla_bwd(pg, wg, bg, gn, ltri, utri, o, states, dzg, *, nseq, S, tm, ride=None):
    T = pg.shape[0]
    nb, nc = S // tm, tm // GC
    qscale = GDK ** -0.5

    def body(pg_ref, wg_ref, bg_ref, gn_ref, l_ref, u_ref, o_ref, st_ref, dzg_ref,
             dpg_ref, dwg_ref, dbg_ref, dgn_ref, dst_scr):
        first = jnp.logical_and(pl.program_id(0) == 0, pl.program_id(1) == 0)

        @pl.when(first)
        def _():
            dwg_ref[...] = jnp.zeros_like(dwg_ref)
            dbg_ref[...] = jnp.zeros_like(dbg_ref)
            dgn_ref[...] = jnp.zeros_like(dgn_ref)

        @pl.when(pl.program_id(1) == 0)
        def _():
            dst_scr[...] = jnp.zeros_like(dst_scr)

        ltri_v = l_ref[...]
        utri_v = u_ref[...]
        causal = _iota((GC, GC), 0) >= _iota((GC, GC), 1)
        last_row = _iota((GC, GDK), 0) == GC - 1
        g = gn_ref[...]

        def chunk(cc, carry):
            c = nc - 1 - cc
            rows = pl.ds(pl.multiple_of(c * GC, GC), GC)
            r, logit, la = _gla_gate(pg_ref, rows, wg_ref, bg_ref)
            b = _tri_mm(ltri_v, la)
            hs = range(GH)
            L = lambda: [None] * GH
            vb, eb, enb, ek, dec, q_in, k_in, k_st, q_inb, k_inb, st, dst, dob = (L() for _ in range(13))
            a_raw, da_raw, dq_st, dks, dv_st, dst_new, dbs, dgn = (L() for _ in range(8))
            for h in hs:
                q = pg_ref[rows, h * GDK:(h + 1) * GDK]
                k = pg_ref[rows, 512 + h * GDK:512 + (h + 1) * GDK]
                vb[h] = pg_ref[rows, 1024 + h * GDV:1024 + (h + 1) * GDV].astype(BF)
                og = pg_ref[rows, 2048 + h * GDV:2048 + (h + 1) * GDV]
                oh = o_ref[rows, h * GDV:(h + 1) * GDV]
                dz = dzg_ref[rows, h * GDV:(h + 1) * GDV].astype(F32)
                bh = b[:, h * GDK:(h + 1) * GDK]
                bl = jnp.sum(jnp.where(last_row, bh, 0.0), axis=0, keepdims=True)
                eb[h] = qscale * jnp.exp(bh)
                enb[h] = jnp.exp(-bh)
                ek[h] = jnp.exp(bl - bh)
                dec[h] = jnp.exp(bl)
                q_in[h], k_in[h], k_st[h] = q * eb[h], k * enb[h], k * ek[h]
                q_inb[h], k_inb[h] = q_in[h].astype(BF), k_in[h].astype(BF)
                st[h] = st_ref[c, h]
                dst[h] = dst_scr[h]
                rstd = lax.rsqrt(jnp.mean(oh * oh, axis=-1, keepdims=True) + RMS_EPS)
                ohat = oh * rstd
                sg = _sigmoid(og)
                don = dz * (og * sg)
                dpg_ref[rows, 2048 + h * GDV:2048 + (h + 1) * GDV] = (
                    dz * (ohat * g) * (sg * (1.0 + og * (1.0 - sg)))).astype(BF)
                dgn[h] = jnp.sum(don * ohat, axis=0, keepdims=True)
                gd = don * g
                dob[h] = (rstd * (gd - ohat * jnp.mean(gd * ohat, axis=-1, keepdims=True))).astype(BF)
                a_raw[h] = _dot_nt(q_inb[h], k_inb[h])
                da_raw[h] = _dot_nt(dob[h], vb[h])
            dgn_ref[...] += dgn[0] + dgn[1] + dgn[2] + dgn[3]
            for h in hs:
                dstb = dst[h].astype(BF)
                dq_st[h] = _dot(dob[h], st[h].astype(BF))
                dks[h] = _dot(vb[h], dstb)
                dv_st[h] = _dot_nt(k_st[h].astype(BF), dstb)
                dst_new[h] = _dot_tn(dob[h], q_inb[h])
            att = [jnp.where(causal, a_raw[h], 0.0).astype(BF) for h in hs]
            da = [jnp.where(causal, da_raw[h], 0.0).astype(BF) for h in hs]
            dqi = [_dot(da[h], k_inb[h]) + dq_st[h] for h in hs]
            dki = [_dot_tn(da[h], q_inb[h]) for h in hs]
            dv = [_dot_tn(att[h], dob[h]) + dv_st[h] for h in hs]
            for h in hs:
                dd = jnp.sum(dst[h] * st[h], axis=0, keepdims=True)
                dst_scr[h] = dst[h] * dec[h] + dst_new[h]
                kk = dks[h] * k_st[h]
                dbl = jnp.sum(kk, axis=0, keepdims=True) + dd * dec[h]
                db = dqi[h] * q_in[h] - dki[h] * k_in[h] - kk
                dbs[h] = db + jnp.where(last_row, dbl, 0.0)
                dpg_ref[rows, h * GDK:(h + 1) * GDK] = (dqi[h] * eb[h]).astype(BF)
                dpg_ref[rows, 512 + h * GDK:512 + (h + 1) * GDK] = (dki[h] * enb[h] + dks[h] * ek[h]).astype(BF)
                dpg_ref[rows, 1024 + h * GDV:1024 + (h + 1) * GDV] = dv[h].astype(BF)
            dla = _tri_mm(utri_v, jnp.concatenate(dbs, axis=1))
            dlogit = dla * (1.0 / GTAU) * _sigmoid(-logit)
            dlb = dlogit.astype(BF)
            dpg_ref[rows, 3072:3200] = _dot_nt(dlb, wg_ref[...]).astype(BF)
            dwg_ref[...] += _dot_tn(r, dlb)
            dbg_ref[...] += jnp.sum(dlogit, axis=0, keepdims=True)
            return carry

        lax.fori_loop(0, nc, chunk, 0, unroll=True)

    full = lambda shp: pl.BlockSpec(shp, lambda b_, i: (0,) * len(shp))
    rev = lambda b_, i: (b_ * nb + nb - 1 - i, 0)
    return _call(
        body, name="gla_bwd", ride=ride, sem=("arbitrary", "arbitrary"),
        args=(pg, wg, bg, gn, ltri, utri, o, states, dzg),
        out_shape=(jax.ShapeDtypeStruct((T, PG_W), BF),
                   jax.ShapeDtypeStruct((128, 512), F32),
                   jax.ShapeDtypeStruct((1, 512), F32),
                   jax.ShapeDtypeStruct((1, GDV), F32)),
        grid=(nseq, nb),
        in_specs=[pl.BlockSpec((tm, PG_W), rev),
                  full((128, 512)), full((1, 512)), full((1, GDV)), full((GC, GC)), full((GC, GC)),
                  pl.BlockSpec((tm, GH * GDV), rev),
                  pl.BlockSpec((nc, GH, GDV, GDK), lambda b_, i: (b_ * nb + nb - 1 - i, 0, 0, 0)),
                  pl.BlockSpec((tm, GH * GDV), rev)],
        out_specs=(pl.BlockSpec((tm, PG_W), rev), full((128, 512)), full((1, 512)), full((1, GDV))),
        scratch_shapes=[pltpu.VMEM((GH, GDV, GDK), F32)])


def _rope_tables(pos, invf):
    ang = pos.astype(F32) * invf
    lane = _iota(ang.shape, 1)
    sin = jnp.sin(ang)
    ssin = jnp.where(lane < 32, -sin, jnp.where(lane < 64, sin, 0.0))
    return jnp.cos(ang), ssin, lane


def _rope(x, cos, ssin, lane, sign):
    rot = jnp.where(lane < 32, pltpu.roll(x, 96, 1), pltpu.roll(x, 32, 1))
    return x * cos + sign * (rot * ssin)


def _rms_fwd(x, g):
    rstd = lax.rsqrt(jnp.mean(x * x, axis=-1, keepdims=True) + RMS_EPS)
    return x * rstd * g, x * rstd, rstd


def _rms_bwd(dy, xhat, rstd, g):
    gd = dy * g
    return rstd * (gd - xhat * jnp.mean(gd * xhat, axis=-1, keepdims=True)), jnp.sum(dy * xhat, axis=0, keepdims=True)


def _mla_prep_fwd(pm, pos, invf, gq, gkv, wuq, wukv, *, tm):
    T = pm.shape[0]

    def body(pm_ref, pos_ref, invf_ref, gq_ref, gkv_ref, wuq_ref, wukv_ref, qc_ref, kc_ref, v_ref):
        cos, ssin, lane = _rope_tables(pos_ref[...], invf_ref[...])
        cq, _, _ = _rms_fwd(pm_ref[:, 0:MQR], gq_ref[...])
        ckv, _, _ = _rms_fwd(pm_ref[:, 512:768], gkv_ref[...])
        qf = _dot(cq.astype(BF), wuq_ref[...])
        kvf = _dot(ckv.astype(BF), wukv_ref[...])
        kr = _rope(pm_ref[:, 384:512], cos, ssin, lane, 1.0).astype(BF)
        for h in range(MH):
            qc_ref[:, 256 * h:256 * h + 128] = (QK_SCALE_LOG2 * qf[:, 128 * h:128 * h + 128]).astype(BF)
            qr = qf[:, 1024 + 128 * h:1024 + 128 * h + 128]
            qc_ref[:, 256 * h + 128:256 * h + 256] = (QK_SCALE_LOG2 * _rope(qr, cos, ssin, lane, 1.0)).astype(BF)
            kc_ref[:, 256 * h:256 * h + 128] = kvf[:, 128 * h:128 * h + 128].astype(BF)
            kc_ref[:, 256 * h + 128:256 * h + 256] = kr
        v_ref[...] = kvf[:, 1024:2048].astype(BF)

    full = lambda shp: pl.BlockSpec(shp, lambda i: (0,) * len(shp))
    row = lambda w: pl.BlockSpec((tm, w), lambda i: (i, 0))
    return pl.pallas_call(
        body, name="mla_prep_fwd",
        out_shape=(jax.ShapeDtypeStruct((T, MH * 256), BF), jax.ShapeDtypeStruct((T, MH * 256), BF),
                   jax.ShapeDtypeStruct((T, MH * MV), BF)),
        grid=(T // tm,),
        in_specs=[row(PM_W), row(1), full((1, 128)), full((1, MQR)), full((1, MKR)),
                  full((MQR, 2048)), full((MKR, 2048))],
        out_specs=(row(MH * 256), row(MH * 256), row(MH * MV)),
        compiler_params=_params(("parallel",)),
    )(pm, pos, invf, gq, gkv, wuq, wukv)


def _mla_prep_bwd(pm, pos, invf, gq, gkv, wuq, wukv, dqc, dkc, dv, *, tm):
    T = pm.shape[0]

    def body(pm_ref, pos_ref, invf_ref, gq_ref, gkv_ref, wuq_ref, wukv_ref, dqc_ref, dkc_ref, dv_ref,
             dpm_ref, dwuq_ref, dwukv_ref, dgq_ref, dgkv_ref):
        @pl.when(pl.program_id(0) == 0)
        def _():
            dwuq_ref[...] = jnp.zeros_like(dwuq_ref)
            dwukv_ref[...] = jnp.zeros_like(dwukv_ref)
            dgq_ref[...] = jnp.zeros_like(dgq_ref)
            dgkv_ref[...] = jnp.zeros_like(dgkv_ref)

        cos, ssin, lane = _rope_tables(pos_ref[...], invf_ref[...])
        cq, cqh, cq_rstd = _rms_fwd(pm_ref[:, 0:MQR], gq_ref[...])
        ckv, ckvh, ckv_rstd = _rms_fwd(pm_ref[:, 512:768], gkv_ref[...])
        dqn, dqr, dkn = [], [], []
        dkr = jnp.zeros((tm, 128), F32)
        for h in range(MH):
            dqn.append(dqc_ref[:, 256 * h:256 * h + 128].astype(BF))
            dqr.append(_rope(dqc_ref[:, 256 * h + 128:256 * h + 256], cos, ssin, lane, -1.0).astype(BF))
            dkn.append(dkc_ref[:, 256 * h:256 * h + 128].astype(BF))
            dkr = dkr + dkc_ref[:, 256 * h + 128:256 * h + 256]
        dqf = jnp.concatenate(dqn + dqr, axis=1)
        dkvf = jnp.concatenate(dkn + [dv_ref[...].astype(BF)], axis=1)
        dwuq_ref[...] += _dot_tn(cq.astype(BF), dqf)
        dwukv_ref[...] += _dot_tn(ckv.astype(BF), dkvf)
        dcq, dgq = _rms_bwd(_dot_nt(dqf, wuq_ref[...]), cqh, cq_rstd, gq_ref[...])
        dckv, dgkv = _rms_bwd(_dot_nt(dkvf, wukv_ref[...]), ckvh, ckv_rstd, gkv_ref[...])
        dgq_ref[...] += dgq
        dgkv_ref[...] += dgkv
        dpm_ref[:, 0:MQR] = dcq.astype(BF)
        dpm_ref[:, 384:512] = _rope(dkr, cos, ssin, lane, -1.0).astype(BF)
        dpm_ref[:, 512:768] = dckv.astype(BF)

    full = lambda shp: pl.BlockSpec(shp, lambda i: (0,) * len(shp))
    row = lambda w: pl.BlockSpec((tm, w), lambda i: (i, 0))
    return pl.pallas_call(
        body, name="mla_prep_bwd",
        out_shape=(jax.ShapeDtypeStruct((T, PM_W), BF), jax.ShapeDtypeStruct((MQR, 2048), F32),
                   jax.ShapeDtypeStruct((MKR, 2048), F32), jax.ShapeDtypeStruct((1, MQR), F32),
                   jax.ShapeDtypeStruct((1, MKR), F32)),
        grid=(T // tm,),
        in_specs=[row(PM_W), row(1), full((1, 128)), full((1, MQR)), full((1, MKR)),
                  full((MQR, 2048)), full((MKR, 2048)), row(MH * 256), row(MH * 256), row(MH * MV)],
        out_specs=(row(PM_W), full((MQR, 2048)), full((MKR, 2048)), full((1, MQR)), full((1, MKR))),
        compiler_params=_params(("arbitrary",)),
    )(pm, pos, invf, gq, gkv, wuq, wukv, dqc, dkc, dv)


def _flash_fwd(qc, kc, v, *, nseq, S, tq, ride=None):
    T = qc.shape[0]
    nq = S // tq
    hp = FLASH_HP_FWD

    def body(q_ref, k_ref, v_ref, o_ref, lse_ref):
        i = pl.program_id(2)
        causal = _iota((tq, tq), 0) >= _iota((tq, tq), 1)

        def step(j, carry, masked):
            rows = pl.ds(pl.multiple_of(j * tq, tq), tq)
            hs = range(hp)
            s = [_dot_nt(q_ref[:, 256 * hh:256 * hh + 256], k_ref[rows, 256 * hh:256 * hh + 256]) for hh in hs]
            p, stats = [], []
            for hh in hs:
                m, l, _ = carry[hh]
                sh = jnp.where(causal, s[hh], NEG) if masked else s[hh]
                m_new = jnp.maximum(m, jnp.max(sh, axis=-1, keepdims=True))
                ph = jnp.exp2(sh - m_new)
                a = jnp.exp2(m - m_new)
                stats.append((m_new, a * l + jnp.sum(ph, axis=-1, keepdims=True), a))
                p.append(ph.astype(BF))
            pv = [_dot(p[hh], v_ref[rows, MV * hh:MV * hh + MV]) for hh in hs]
            return tuple((stats[hh][0], stats[hh][1], stats[hh][2] * carry[hh][2] + pv[hh]) for hh in hs)

        init = ((jnp.full((tq, 1), NEG, F32), jnp.zeros((tq, 1), F32), jnp.zeros((tq, MV), F32)),) * hp
        carry = lax.fori_loop(0, i, lambda j, c: step(j, c, False), init)
        for hh, (m, l, acc) in enumerate(step(i, carry, True)):
            o_ref[:, MV * hh:MV * hh + MV] = (acc / l).astype(BF)
            lse_ref[:, 128 * hh:128 * hh + 128] = jnp.broadcast_to(m + jnp.log2(l), (tq, 128))

    return _call(
        body, name="flash_fwd", ride=ride, sem=("parallel", "parallel", "arbitrary"), args=(qc, kc, v),
        out_shape=(jax.ShapeDtypeStruct((T, MH * MV), BF), jax.ShapeDtypeStruct((T, MH * 128), F32)),
        grid=(nseq, MH // hp, nq),
        in_specs=[pl.BlockSpec((tq, 256 * hp), lambda b_, h, i: (b_ * nq + i, h)),
                  pl.BlockSpec((S, 256 * hp), lambda b_, h, i: (b_, h)),
                  pl.BlockSpec((S, MV * hp), lambda b_, h, i: (b_, h))],
        out_specs=(pl.BlockSpec((tq, MV * hp), lambda b_, h, i: (b_ * nq + i, h)),
                   pl.BlockSpec((tq, 128 * hp), lambda b_, h, i: (b_ * nq + i, h))))


def _flash_bwd(qc, kc, v, o, do, lse, *, nseq, S, tq, ride=None):
    T = qc.shape[0]
    nq = S // tq

    def body(q_ref, k_ref, v_ref, o_ref, do_ref, lse_ref, dq_ref, dk_ref, dv_ref, dq_scr, delta_scr):
        j = pl.program_id(2)

        @pl.when(j == 0)
        def _():
            dq_scr[...] = jnp.zeros_like(dq_scr)
            for hh in range(FLASH_HP):
                od = o_ref[:, MV * hh:MV * hh + MV].astype(F32) * do_ref[:, MV * hh:MV * hh + MV].astype(F32)
                delta_scr[:, 128 * hh:128 * hh + 128] = jnp.broadcast_to(jnp.sum(od, axis=-1, keepdims=True), (S, 128))

        causal = _iota((tq, tq), 0) >= _iota((tq, tq), 1)

        def step(i, carry, masked):
            rows = pl.ds(pl.multiple_of(i * tq, tq), tq)
            hs = range(FLASH_HP)
            qs = [slice(256 * hh, 256 * hh + 256) for hh in hs]
            vs = [slice(MV * hh, MV * hh + MV) for hh in hs]
            ls = [slice(128 * hh, 128 * hh + 1) for hh in hs]
            s = [_dot_nt(q_ref[rows, qs[hh]], k_ref[:, qs[hh]]) for hh in hs]
            dp = [_dot_nt(do_ref[rows, vs[hh]], v_ref[:, vs[hh]]) for hh in hs]
            pb, ds = [], []
            for hh in hs:
                p = jnp.exp2(s[hh] - lse_ref[rows, ls[hh]])
                if masked:
                    p = jnp.where(causal, p, 0.0)
                pb.append(p.astype(BF))
                ds.append((p * (dp[hh] - delta_scr[rows, ls[hh]])).astype(BF))
            dv = [carry[hh][1] + _dot_tn(pb[hh], do_ref[rows, vs[hh]]) for hh in hs]
            dk = [carry[hh][0] + _dot_tn(ds[hh], q_ref[rows, qs[hh]]) for hh in hs]
            for hh in hs:
                dq_scr[rows, qs[hh]] += _dot(ds[hh], k_ref[:, qs[hh]])
            return tuple((dk[hh], dv[hh]) for hh in hs)

        init = ((jnp.zeros((tq, 256), F32), jnp.zeros((tq, MV), F32)),) * FLASH_HP
        carry = step(j, init, True)
        carry = lax.fori_loop(j + 1, nq, lambda i, c: step(i, c, False), carry)
        for hh, (dk, dv) in enumerate(carry):
            dk_ref[:, 256 * hh:256 * hh + 256] = dk * (1.0 / LOG2E)
            dv_ref[:, MV * hh:MV * hh + MV] = dv

        @pl.when(j == nq - 1)
        def _():
            dq_ref[...] = dq_scr[...] * QK_SCALE

    hp = FLASH_HP
    seq = lambda w: pl.BlockSpec((S, w * hp), lambda b_, h, j: (b_, h))
    blk = lambda w: pl.BlockSpec((tq, w * hp), lambda b_, h, j: (b_ * nq + j, h))
    return _call(
        body, name="flash_bwd", ride=ride, sem=("parallel", "parallel", "arbitrary"), args=(qc, kc, v, o, do, lse),
        out_shape=(jax.ShapeDtypeStruct((T, MH * 256), F32), jax.ShapeDtypeStruct((T, MH * 256), F32),
                   jax.ShapeDtypeStruct((T, MH * MV), F32)),
        grid=(nseq, MH // hp, nq),
        in_specs=[seq(256), blk(256), blk(MV), seq(MV), seq(MV), seq(128)],
        out_specs=(seq(256), blk(256), blk(MV)),
        scratch_shapes=[pltpu.VMEM((S, 256 * hp), F32), pltpu.VMEM((S, 128 * hp), F32)])


def _ln_fwd(pre, g, b):
    mu = jnp.mean(pre, axis=-1, keepdims=True)
    xc = pre - mu
    rstd = lax.rsqrt(jnp.mean(xc * xc, axis=-1, keepdims=True) + LN_EPS)
    xhat = xc * rstd
    return xhat * g + b, xhat, rstd


def _ln_bwd(dy, xhat, rstd, g):
    dxh = dy * g
    dx = rstd * (dxh - jnp.mean(dxh, axis=-1, keepdims=True) - xhat * jnp.mean(dxh * xhat, axis=-1, keepdims=True))
    return dx, jnp.sum(dy * xhat, axis=0, keepdims=True), jnp.sum(dy, axis=0, keepdims=True)


def _post_attn_fwd(zg, attn, pt, x, wgo, wmo, wout, g1, b1, *, tm):
    T = x.shape[0]

    def body(zg_ref, at_ref, pt_ref, x_ref, wgo_ref, wmo_ref, wout_ref, g_ref, b_ref,
             yg_ref, ym_ref, mix_ref, pre_ref, h_ref, hb_ref):
        yg = _dot(zg_ref[...], wgo_ref[...])
        ym = _dot(at_ref[...], wmo_ref[...])
        mix = (_sigmoid(pt_ref[:, 0:D]) * yg + _sigmoid(pt_ref[:, D:2 * D]) * ym).astype(BF)
        pre = ALPHA * x_ref[...] + _dot(mix, wout_ref[...])
        h, _, _ = _ln_fwd(pre, g_ref[...], b_ref[...])
        yg_ref[...] = yg.astype(BF)
        ym_ref[...] = ym.astype(BF)
        mix_ref[...] = mix
        pre_ref[...] = pre
        h_ref[...] = h
        hb_ref[...] = h.astype(BF)

    full = lambda shp: pl.BlockSpec(shp, lambda i: (0,) * len(shp))
    row = lambda w: pl.BlockSpec((tm, w), lambda i: (i, 0))
    sd = lambda dt: jax.ShapeDtypeStruct((T, D), dt)
    return pl.pallas_call(
        body, name="post_attn_fwd",
        out_shape=(sd(BF), sd(BF), sd(BF), sd(F32), sd(F32), sd(BF)),
        grid=(T // tm,),
        in_specs=[row(D), row(D), row(PT_W), row(D), full((D, D)), full((D, D)), full((D, D)),
                  full((1, D)), full((1, D))],
        out_specs=(row(D),) * 6,
        compiler_params=_params(("parallel",)),
    )(zg, attn, pt, x, wgo, wmo, wout, g1, b1)


def _post_attn_bwd(dh, pre, pt, yg, ym, wgo, wmo, wout, g1, *, tm):
    T = dh.shape[0]

    def body(dh_ref, pre_ref, pt_ref, yg_ref, ym_ref, wgo_ref, wmo_ref, wout_ref, g_ref,
             dx_ref, dpreb_ref, dpt_ref, dygb_ref, dymb_ref, dzg_ref, dat_ref, dg_ref, db_ref):
        @pl.when(pl.program_id(0) == 0)
        def _():
            dg_ref[...] = jnp.zeros_like(dg_ref)
            db_ref[...] = jnp.zeros_like(db_ref)

        pre = pre_ref[...]
        mu = jnp.mean(pre, axis=-1, keepdims=True)
        xc = pre - mu
        rstd = lax.rsqrt(jnp.mean(xc * xc, axis=-1, keepdims=True) + LN_EPS)
        dpre, dg, db = _ln_bwd(dh_ref[...], xc * rstd, rstd, g_ref[...])
        dg_ref[...] += dg
        db_ref[...] += db
        dx_ref[...] = ALPHA * dpre
        dpreb = dpre.astype(BF)
        dpreb_ref[...] = dpreb
        dmix = _dot_nt(dpreb, wout_ref[...])
        sa = _sigmoid(pt_ref[:, 0:D])
        sb = _sigmoid(pt_ref[:, D:2 * D])
        dpt_ref[:, 0:D] = (dmix * yg_ref[...].astype(F32) * (sa * (1.0 - sa))).astype(BF)
        dpt_ref[:, D:2 * D] = (dmix * ym_ref[...].astype(F32) * (sb * (1.0 - sb))).astype(BF)
        dyg = (dmix * sa).astype(BF)
        dym = (dmix * sb).astype(BF)
        dygb_ref[...] = dyg
        dymb_ref[...] = dym
        dzg_ref[...] = _dot_nt(dyg, wgo_ref[...]).astype(BF)
        dat_ref[...] = _dot_nt(dym, wmo_ref[...]).astype(BF)

    full = lambda shp: pl.BlockSpec(shp, lambda i: (0,) * len(shp))
    row = lambda w: pl.BlockSpec((tm, w), lambda i: (i, 0))
    sd = lambda w, dt: jax.ShapeDtypeStruct((T, w), dt)
    return pl.pallas_call(
        body, name="post_attn_bwd",
        out_shape=(sd(D, F32), sd(D, BF), sd(PT_W, BF), sd(D, BF), sd(D, BF), sd(D, BF), sd(D, BF),
                   jax.ShapeDtypeStruct((1, D), F32), jax.ShapeDtypeStruct((1, D), F32)),
        grid=(T // tm,),
        in_specs=[row(D), row(D), row(PT_W), row(D), row(D), full((D, D)), full((D, D)), full((D, D)),
                  full((1, D))],
        out_specs=(row(D), row(D), row(PT_W), row(D), row(D), row(D), row(D), full((1, D)), full((1, D))),
        compiler_params=_params(("arbitrary",)),
    )(dh, pre, pt, yg, ym, wgo, wmo, wout, g1)


def _shift_down(u, prev, k):
    r = pltpu.roll(u, k, 0)
    p = pltpu.roll(prev, k, 0)
    head = jnp.where(_iota(p.shape, 0) < k, p, r[0:8, :])
    return jnp.concatenate([head, r[8:, :]], axis=0)


def _shift_up(u, nxt, k):
    n = u.shape[0]
    r = pltpu.roll(u, n - k, 0)
    p = pltpu.roll(nxt, 8 - k, 0)
    tail = jnp.where(_iota(p.shape, 0) >= 8 - k, p, r[n - 8:, :])
    return jnp.concatenate([r[:n - 8, :], tail], axis=0)


def _conv3(u, prev, w_ref, b_ref):
    return (w_ref[0:1, :] * _shift_down(u, prev, 2) + w_ref[1:2, :] * _shift_down(u, prev, 1)
            + w_ref[2:3, :] * u + b_ref[...])


def _ffn_up_fwd(hb, wug, wuv, cw, cb, *, S, tm, tn):
    T = hb.shape[0]
    nj, nbs = DFF // tn, S // tm

    def body(h_ref, wg_ref, wv_ref, cwg_ref, cwv_ref, cbg_ref, cbv_ref,
             ug_ref, uv_ref, ucg_ref, ucv_ref, f_ref, pg_scr, pv_scr):
        @pl.when(pl.program_id(1) % nbs == 0)
        def _():
            pg_scr[...] = jnp.zeros_like(pg_scr)
            pv_scr[...] = jnp.zeros_like(pv_scr)

        h = h_ref[...]
        ug = _dot(h, wg_ref[...])
        uv = _dot(h, wv_ref[...])
        ucg = _conv3(ug, pg_scr[...], cwg_ref, cbg_ref)
        ucv = _conv3(uv, pv_scr[...], cwv_ref, cbv_ref)
        pg_scr[...] = ug[tm - 8:, :]
        pv_scr[...] = uv[tm - 8:, :]
        ug_ref[...] = ug.astype(BF)
        uv_ref[...] = uv.astype(BF)
        ucg_ref[...] = ucg
        ucv_ref[...] = ucv
        f_ref[...] = (ucg * _sigmoid(ucg) * ucv).astype(BF)

    tile = pl.BlockSpec((tm, tn), lambda j, i: (i, j))
    return pl.pallas_call(
        body, name="ffn_up_fwd",
        out_shape=(jax.ShapeDtypeStruct((T, DFF), BF), jax.ShapeDtypeStruct((T, DFF), BF),
                   jax.ShapeDtypeStruct((T, DFF), F32), jax.ShapeDtypeStruct((T, DFF), F32),
                   jax.ShapeDtypeStruct((T, DFF), BF)),
        grid=(nj, T // tm),
        in_specs=[pl.BlockSpec((tm, D), lambda j, i: (i, 0)),
                  pl.BlockSpec((D, tn), lambda j, i: (0, j)), pl.BlockSpec((D, tn), lambda j, i: (0, j)),
                  pl.BlockSpec((3, tn), lambda j, i: (0, j)), pl.BlockSpec((3, tn), lambda j, i: (0, j + nj)),
                  pl.BlockSpec((1, tn), lambda j, i: (0, j)), pl.BlockSpec((1, tn), lambda j, i: (0, j + nj))],
        out_specs=(tile, tile, tile, tile, tile),
        scratch_shapes=[pltpu.VMEM((8, tn), F32), pltpu.VMEM((8, tn), F32)],
        compiler_params=_params(("parallel", "arbitrary")),
    )(hb, wug, wuv, cw, cw, cb, cb)


def _ffn_bwd(dpreb, wd, ug, uv, ucg, ucv, cw, *, S, tm, tn):
    T = dpreb.shape[0]
    nj, nb, nbs = DFF // tn, T // tm, S // tm
    r_, c_ = lax.broadcasted_iota(jnp.int32, (tm, tm), 0), lax.broadcasted_iota(jnp.int32, (tm, tm), 1)
    s1, s2 = (c_ == r_ + 1).astype(BF), (c_ == r_ + 2).astype(BF)

    def body(dp_ref, wd_ref, ug_ref, uv_ref, ucg_ref, ucv_ref, cwg_ref, cwv_ref, s1_ref, s2_ref,
             dug_ref, duv_ref, dcg_ref, dcv_ref, ng_scr, nv_scr):
        ii = pl.program_id(1)
        i = nb - 1 - ii
        tail_row = _iota((8, tn), 0)

        @pl.when(ii == 0)
        def _():
            dcg_ref[...] = jnp.zeros_like(dcg_ref)
            dcv_ref[...] = jnp.zeros_like(dcv_ref)

        @pl.when(i % nbs == nbs - 1)
        def _():
            ng_scr[...] = jnp.zeros_like(ng_scr)
            nv_scr[...] = jnp.zeros_like(nv_scr)

        df = _dot_nt(dp_ref[...], wd_ref[...])
        ucg = ucg_ref[...]
        sg = _sigmoid(ucg)
        ducg = df * ucv_ref[...] * (sg * (1.0 + ucg * (1.0 - sg)))
        ducv = df * (ucg * sg)

        def finish(duc, u_ref, w, nxt_scr, du_ref, dc_ref):
            nxt = nxt_scr[...]
            db = duc.astype(BF)

            def shifted(s_ref, k):
                r = _dot(s_ref[...], db)
                tail = jnp.where(tail_row >= 8 - k, pltpu.roll(nxt, 8 - k, 0), r[tm - 8:, :])
                return jnp.concatenate([r[:tm - 8, :], tail], axis=0)

            up1 = shifted(s1_ref, 1)
            up2 = shifted(s2_ref, 2)
            du_ref[...] = (w[2:3, :] * duc + w[1:2, :] * up1 + w[0:1, :] * up2).astype(BF)
            nxt_scr[...] = duc[0:8, :]
            u = u_ref[...].astype(F32)
            for row, z in enumerate((u * up2, u * up1, u * duc, duc)):
                dc_ref[row:row + 1, :] += jnp.sum(z, axis=0, keepdims=True)

        finish(ducg, ug_ref, cwg_ref, ng_scr, dug_ref, dcg_ref)
        finish(ducv, uv_ref, cwv_ref, nv_scr, duv_ref, dcv_ref)

    tile = pl.BlockSpec((tm, tn), lambda j, ii: (nb - 1 - ii, j))
    acc = pl.BlockSpec((8, tn), lambda j, ii: (0, j))
    return pl.pallas_call(
        body, name="ffn_bwd",
        out_shape=(jax.ShapeDtypeStruct((T, DFF), BF), jax.ShapeDtypeStruct((T, DFF), BF),
                   jax.ShapeDtypeStruct((8, DFF), F32), jax.ShapeDtypeStruct((8, DFF), F32)),
        grid=(nj, nb),
        in_specs=[pl.BlockSpec((tm, D), lambda j, ii: (nb - 1 - ii, 0)),
                  pl.BlockSpec((tn, D), lambda j, ii: (j, 0)),
                  tile, tile, tile, tile,
                  pl.BlockSpec((3, tn), lambda j, ii: (0, j)), pl.BlockSpec((3, tn), lambda j, ii: (0, j + nj)),
                  pl.BlockSpec((tm, tm), lambda j, ii: (0, 0)), pl.BlockSpec((tm, tm), lambda j, ii: (0, 0))],
        out_specs=(tile, tile, acc, acc),
        scratch_shapes=[pltpu.VMEM((8, tn), F32), pltpu.VMEM((8, tn), F32)],
        compiler_params=_params(("parallel", "arbitrary")),
    )(dpreb, wd, ug, uv, ucg, ucv, cw, cw, s1, s2)


def _down_ln2_loss(f_in, wd, h, target, g2, b2, *, tm):
    T = h.shape[0]

    def body(f_ref, wd_ref, h_ref, t_ref, g_ref, b_ref, dpb_ref, dh_ref, loss_ref, dg_ref, db_ref):
        @pl.when(pl.program_id(0) == 0)
        def _():
            loss_ref[...] = jnp.zeros_like(loss_ref)
            dg_ref[...] = jnp.zeros_like(dg_ref)
            db_ref[...] = jnp.zeros_like(db_ref)

        halves = [pl.ds(s * (tm // 2), tm // 2) for s in range(2)]
        f = [_dot(f_ref[hs, :], wd_ref[...]) for hs in halves]
        for hs, fh in zip(halves, f):
            pre = ALPHA * h_ref[hs, :] + fh
            out, xhat, rstd = _ln_fwd(pre, g_ref[...], b_ref[...])
            diff = out - t_ref[hs, :]
            loss_ref[...] += 0.5 * jnp.sum(jnp.mean(diff * diff, axis=-1, keepdims=True))
            dpre, dg, db = _ln_bwd(diff * (1.0 / D), xhat, rstd, g_ref[...])
            dg_ref[...] += dg
            db_ref[...] += db
            dpb_ref[hs, :] = dpre.astype(BF)
            dh_ref[hs, :] = ALPHA * dpre

    full = lambda shp: pl.BlockSpec(shp, lambda i: (0,) * len(shp))
    row = lambda w: pl.BlockSpec((tm, w), lambda i: (i, 0))
    return pl.pallas_call(
        body, name="down_ln2_loss",
        out_shape=(jax.ShapeDtypeStruct((T, D), BF), jax.ShapeDtypeStruct((T, D), F32),
                   jax.ShapeDtypeStruct((8, 128), F32), jax.ShapeDtypeStruct((1, D), F32),
                   jax.ShapeDtypeStruct((1, D), F32)),
        grid=(T // tm,),
        in_specs=[row(DFF), full((DFF, D)), row(D), row(D), full((1, D)), full((1, D))],
        out_specs=(row(D), row(D), full((8, 128)), full((1, D)), full((1, D))),
        compiler_params=_params(("arbitrary",)),
    )(f_in, wd, h, target, g2, b2)


def _adamw(parts, w, m, v, *, name):
    n, R, C = parts.shape
    tr, tc = R, C
    for cand in range(min(R, 256), 15, -1):
        if R % cand == 0 and cand % 16 == 0:
            tr = cand
            break
    if tr == R and R * C > 65536 and C % 256 == 0:
        tc = 256
    c1 = 1.0 - ADAM_B1 ** ADAM_STEP
    c2 = 1.0 - ADAM_B2 ** ADAM_STEP

    def body(p_ref, w_ref, m_ref, v_ref, g_ref, d_ref, nm_ref, nv_ref):
        g = p_ref[0].astype(F32)
        for s in range(1, n):
            g = g + p_ref[s].astype(F32)
        nm = ADAM_B1 * m_ref[...] + (1.0 - ADAM_B1) * g
        nv = ADAM_B2 * v_ref[...] + (1.0 - ADAM_B2) * (g * g)
        g_ref[...] = g
        nm_ref[...] = nm
        nv_ref[...] = nv
        d_ref[...] = -ADAM_LR * ((nm / c1) / (jnp.sqrt(nv / c2) + ADAM_EPS) + ADAM_WD * w_ref[...])

    blk = pl.BlockSpec((tr, tc), lambda i, j: (i, j))
    sd = jax.ShapeDtypeStruct((R, C), F32)
    return pl.pallas_call(
        body, name=name,
        out_shape=(sd, sd, sd, sd),
        grid=(R // tr, C // tc),
        in_specs=[pl.BlockSpec((n, tr, tc), lambda i, j: (0, i, j)), blk, blk, blk],
        out_specs=(blk, blk, blk, blk),
        compiler_params=_params(("parallel", "parallel")),
    )(parts, w, m, v)


class _Exchange:
    def __init__(self, items):
        self.items = [(src if sc else [(src, 0)], sc) for src, sc in items]
        self.arrays = [arr for srcs, _ in self.items for arr, _ in srcs]
        self.n = len(self.items)
        self.n_in = len(self.arrays)

    def out_shape(self):
        return tuple(jax.ShapeDtypeStruct((NDEV,) + (srcs[0][0].shape[1:] if sc else srcs[0][0].shape),
                                          srcs[0][0].dtype) for srcs, sc in self.items)

    def scratch(self):
        return [pltpu.SemaphoreType.DMA((self.n, NDEV - 1)), pltpu.SemaphoreType.DMA((self.n, NDEV - 1)),
                pltpu.SemaphoreType.DMA((self.n,))]

    def _emit(self, ins, outs, sems, phase):
        send_sems, recv_sems, loc_sems = sems
        x, y, c = lax.axis_index("x"), lax.axis_index("y"), lax.axis_index("c")
        me = 4 * x + 2 * y + c
        flip = lambda p, d: 1 - p if d else p

        def inside(p, lo, n):
            return None if (lo, n) == (0, NDEV) else jnp.logical_and(p >= lo, p < lo + n)

        def when(cond, fn):
            if cond is None:
                fn()
            else:
                pl.when(cond)(fn)

        pos = 0
        for a, (srcs, sc) in enumerate(self.items):
            refs = ins[pos:pos + len(srcs)]
            pos += len(srcs)
            ranges = [(lo, arr.shape[0]) if sc else (0, NDEV) for arr, lo in srcs]
            mine = [inside(me, lo, n) for lo, n in ranges]
            i_receive = None if None in mine else functools.reduce(jnp.logical_or, mine)
            for ref, (lo, n), cond in zip(refs, ranges, mine):
                def local(ref=ref, lo=lo):
                    cp = pltpu.make_async_copy(ref.at[me - lo] if sc else ref, outs[a].at[me], loc_sems.at[a])
                    cp.start() if phase == 0 else cp.wait()
                if phase != 1:
                    when(cond, local)
            for k in range(1, NDEV):
                px, py, pc = flip(x, k & 4), flip(y, k & 2), flip(c, k & 1)
                peer = 4 * px + 2 * py + pc
                mk = functools.partial(pltpu.make_async_remote_copy,
                                       send_sem=send_sems.at[a, k - 1], recv_sem=recv_sems.at[a, k - 1],
                                       device_id=(px, py, pc), device_id_type=MESH_ID)
                if phase == 1:
                    def arrival(mk=mk, peer=peer):
                        mk(src_ref=refs[0].at[0] if sc else refs[0], dst_ref=outs[a].at[peer]).wait_recv()
                    when(i_receive, arrival)
                    continue
                for ref, (lo, n) in zip(refs, ranges):
                    def send(mk=mk, ref=ref, lo=lo, peer=peer):
                        cp = mk(src_ref=ref.at[peer - lo] if sc else ref, dst_ref=outs[a].at[me])
                        cp.start() if phase == 0 else cp.wait_send()
                    when(inside(peer, lo, n), send)

    def start(self, ins, outs, sems):
        self._emit(ins, outs, sems, 0)

    def wait(self, ins, outs, sems):
        self._emit(ins, outs, sems, 1)
        self._emit(ins, outs, sems, 2)


def _call(body, *, name, grid, in_specs, out_specs, out_shape, args, scratch_shapes=(), sem=None, ride=None):
    if ride is None:
        return pl.pallas_call(body, name=name, grid=grid, in_specs=list(in_specs), out_specs=tuple(out_specs),
                              out_shape=tuple(out_shape), scratch_shapes=list(scratch_shapes),
                              compiler_params=_params(sem))(*args)
    n_in, n_out, n_scr, ne, ne_in = len(args), len(out_shape), len(scratch_shapes), ride.n, ride.n_in

    def ride_body(*refs):
        ins, ex_in = refs[:n_in], refs[n_in:n_in + ne_in]
        o0 = n_in + ne_in
        outs, ex_out = refs[o0:o0 + n_out], refs[o0 + n_out:o0 + n_out + ne]
        scr = refs[o0 + n_out + ne:o0 + n_out + ne + n_scr]
        sems = refs[o0 + n_out + ne + n_scr:]
        first = functools.reduce(jnp.logical_and, [pl.program_id(d) == 0 for d in range(len(grid))])
        last = functools.reduce(jnp.logical_and, [pl.program_id(d) == grid[d] - 1 for d in range(len(grid))])

        @pl.when(first)
        def _():
            ride.start(ex_in, ex_out, sems)

        body(*ins, *outs, *scr)

        @pl.when(last)
        def _():
            ride.wait(ex_in, ex_out, sems)

    anyspec = pl.BlockSpec(memory_space=pl.ANY)
    res = pl.pallas_call(
        ride_body, name=name, grid=grid,
        in_specs=list(in_specs) + [anyspec] * ne_in,
        out_specs=tuple(out_specs) + (anyspec,) * ne,
        out_shape=tuple(out_shape) + ride.out_shape(),
        scratch_shapes=list(scratch_shapes) + ride.scratch(),
        compiler_params=_params(("arbitrary",) * len(grid)),
    )(*args, *ride.arrays)
    return tuple(res[:n_out]), tuple(res[n_out:])


def _gather_two_level(arrays, *, name):
    n = len(arrays)

    def body(*refs):
        ins, outs = refs[:n], refs[n:2 * n]
        send_sems, recv_sems, loc_sems = refs[2 * n:]
        x, y, c = lax.axis_index("x"), lax.axis_index("y"), lax.axis_index("c")
        sibling = (x, y, 1 - c)
        chips = [(1 - x, y), (x, 1 - y), (1 - x, 1 - y)]
        idx = lambda px, py, pc: 4 * px + 2 * py + pc
        me = idx(x, y, c)

        def copy(a, k, block, to, src=None):
            return pltpu.make_async_remote_copy(
                src_ref=outs[a].at[block] if src is None else src, dst_ref=outs[a].at[block],
                send_sem=send_sems.at[a, k], recv_sem=recv_sems.at[a, k], device_id=to, device_id_type=MESH_ID)

        local = [pltpu.make_async_copy(ins[a], outs[a].at[me], loc_sems.at[a]) for a in range(n)]
        sent = []
        for a in range(n):
            sent.append(copy(a, 0, me, sibling, src=ins[a]))
            sent += [copy(a, 1 + j, me, (*chip, c), src=ins[a]) for j, chip in enumerate(chips)]
        for cp in local + sent:
            cp.start()
        for j, chip in enumerate(chips):
            for a in range(n):
                copy(a, 1 + j, idx(*chip, c), sibling).wait_recv()
                passed = copy(a, 4 + j, idx(*chip, c), sibling)
                passed.start()
                sent.append(passed)
        for a in range(n):
            copy(a, 0, idx(x, y, 1 - c), sibling).wait_recv()
            for j, chip in enumerate(chips):
                copy(a, 4 + j, idx(*chip, 1 - c), sibling).wait_recv()
        for cp in sent:
            cp.wait_send()
        for cp in local:
            cp.wait()

    anyspec = pl.BlockSpec(memory_space=pl.ANY)
    return pl.pallas_call(
        body, name=name,
        out_shape=tuple(jax.ShapeDtypeStruct((NDEV,) + a.shape, a.dtype) for a in arrays),
        in_specs=[anyspec] * n, out_specs=(anyspec,) * n,
        scratch_shapes=[pltpu.SemaphoreType.DMA((n, NDEV - 1)), pltpu.SemaphoreType.DMA((n, NDEV - 1)),
                        pltpu.SemaphoreType.DMA((n,))],
    )(*arrays)


def _tri_consts():
    r = lax.broadcasted_iota(jnp.int32, (GC, GC), 0)
    c = lax.broadcasted_iota(jnp.int32, (GC, GC), 1)
    return (r >= c).astype(BF), (r <= c).astype(BF)


def _local_step(x, positions, target, w, hooks=None):
    g = {}

    def run(host, fn, *a, **kw):
        h = None if hooks is None else hooks.get(host)
        if h is None:
            return fn(*a, **kw)
        out, received = fn(*a, ride=_Exchange(h[0](w, g)), **kw)
        h[1](received, w, g)
        return out

    nseq, S, _ = x.shape
    T = nseq * S
    tm = min(256, S)
    tq = min(512, S)
    x2 = x.reshape(T, D)
    pos = positions.reshape(T, 1)
    half = ROPE // 2
    inv = THETA ** (-jnp.arange(half, dtype=F32) / half)
    invf = jnp.concatenate([inv, inv, jnp.zeros((64,), F32)]).reshape(1, 128)
    ltri, utri = _tri_consts()

    pt, xb = _matmul(x2, w["w_tt"], "nt", name="proj_t", tm=1024, tn=1024, tk=1024, emit_a=True)
    pg = run("proj_g", _matmul, xb, w["w_gt"], "nt", name="proj_g", tm=1024, tn=640, tk=1024)
    pm = _matmul(xb, w["w_mt"], "nt", name="proj_m", tm=1024, tn=768, tk=1024)
    o, zg, states = _gla_fwd(pg, w["wg"], w["bg"], w["gn"], ltri, nseq=nseq, S=S, tm=tm)
    qc, kc, v = _mla_prep_fwd(pm, pos, invf, w["gq"], w["gkv"], w["wuq"], w["wukv"], tm=tm)
    attn, lse = run("flash_fwd", _flash_fwd, qc, kc, v, nseq=nseq, S=S, tq=tq)
    yg, ym, mix, pre1, h1, h1b = _post_attn_fwd(zg, attn, pt, x2, w["wgo"], w["wmo"], w["wout"],
                                                w["g1"], w["b1"], tm=tm)
    ug, uv, ucg, ucv, f_in = _ffn_up_fwd(h1b, w["wug"], w["wuv"], w["cw"], w["cb"], S=S, tm=tm, tn=1408)
    dpre2b, dh1, loss8, dg2, db2 = _down_ln2_loss(f_in, w["wd"], h1, target.reshape(T, D), w["g2"], w["b2"],
                                                  tm=min(2 * tm, S))

    dug, duv, dcg, dcv = _ffn_bwd(dpre2b, w["wd"], ug, uv, ucg, ucv, w["cw"], S=S, tm=tm, tn=1408)
    g["g2"], g["b2"], g["loss"] = dg2, db2, loss8[0:1, 0:1]
    g["cw"] = jnp.concatenate([dcg[0:3], dcv[0:3]], axis=1)
    g["cb"] = jnp.concatenate([dcg[3:4], dcv[3:4]], axis=1)
    g["wd"] = _matmul(f_in, dpre2b, "tn", name="dw_down", out_dtype=BF, tm=1408, tn=1024, tk=1024)
    g["wugt"] = _matmul(dug, h1b, "tn", name="dw_up_g", out_dtype=BF, tm=1408, tn=1024, tk=1024)
    g["wuvt"] = _matmul(duv, h1b, "tn", name="dw_up_v", out_dtype=BF, tm=1408, tn=1024, tk=1024)
    dh1 = _matmul_sum(dh1, [(dug, w["wugt"], 1408), (duv, w["wuvt"], 1408)], name="dh1", tm=512)
    dx, dpre1b, dpt, dygb, dymb, dzg, dattn, dg1, db1 = _post_attn_bwd(
        dh1, pre1, pt, yg, ym, w["wgo"], w["wmo"], w["wout"], w["g1"], tm=tm)
    g["g1"], g["b1"] = dg1, db1
    g["wout"] = _matmul(mix, dpre1b, "tn", name="dw_out", out_dtype=BF, tm=1024, tn=1024, tk=1024)
    g["wgo"] = _matmul(zg, dygb, "tn", name="dw_gla_o", out_dtype=BF, tm=1024, tn=1024, tk=1024)
    g["wmo"] = _matmul(attn, dymb, "tn", name="dw_mla_o", out_dtype=BF, tm=1024, tn=1024, tk=1024)
    dqc, dkc, dv = run("flash_bwd", _flash_bwd, qc, kc, v, attn, dattn, lse, nseq=nseq, S=S, tq=tq)
    dpm, g["wuq"], g["wukv"], g["gq"], g["gkv"] = _mla_prep_bwd(
        pm, pos, invf, w["gq"], w["gkv"], w["wuq"], w["wukv"], dqc, dkc, dv, tm=tm)
    g["w_mt"] = _matmul(dpm, xb, "tn", name="dw_in_m", out_dtype=BF, tm=768, tn=1024, tk=1024)
    g["w_tt"] = _matmul(dpt, xb, "tn", name="dw_in_t", out_dtype=BF, tm=1024, tn=1024, tk=1024)
    dpg, g["wg"], g["bg"], g["gn"] = run("gla_bwd", _gla_bwd, pg, w["wg"], w["bg"], w["gn"], ltri, utri, o, states,
                                         dzg, nseq=nseq, S=S, tm=tm)
    g["w_gt"] = _matmul(dpg, xb, "tn", name="dw_in_g", out_dtype=BF, tm=640, tn=1024, tk=1024)
    dx = run("dx", _matmul_sum, dx, [(dpg, w["w_gt"], 640), (dpm, w["w_mt"], 768)], name="dx_gm")
    dx = _matmul_sum(dx, [(dpt, w["w_tt"], 1024)], name="dx_t")
    return loss8[0, 0], dx.reshape(nseq, S, D), g


_IN_SPLITS = (512, 512, 1024, 16, 1024, 384, 256, 64, 1024, 1024)


def _w_in_to_groups(wt):
    offs = [0]
    for s in _IN_SPLITS:
        offs.append(offs[-1] + s)
    q, k, v, r, og, cq, ckv, kr, ga, gb = [wt[offs[i]:offs[i + 1]] for i in range(10)]
    z = lambda n: jnp.zeros((n, wt.shape[1]), wt.dtype)
    return (jnp.concatenate([q, k, v, og, r, z(112)], axis=0),
            jnp.concatenate([cq, kr, z(64), ckv], axis=0),
            jnp.concatenate([ga, gb], axis=0))


def _groups_to_w_in(g_g, g_m, g_t):
    q, k, v, og, r = g_g[0:512], g_g[512:1024], g_g[1024:2048], g_g[2048:3072], g_g[3072:3088]
    cq, kr, ckv = g_m[0:384], g_m[384:448], g_m[512:768]
    return jnp.concatenate([q, k, v, r, og, cq, ckv, kr, g_t], axis=0)


_W_IN_LO = 5
_W_IN_SPLIT = _W_IN_LO * 730 - 3472


def _w_in_rows_lo(g_g, g_m):
    q, k, v, og, r = g_g[0:512], g_g[512:1024], g_g[1024:2048], g_g[2048:3072], g_g[3072:3088]
    return jnp.concatenate([q, k, v, r, og, g_m[0:384], g_m[512:768]], axis=0)[:3472 + _W_IN_SPLIT]


def _w_in_rows_hi(g_m, g_t):
    return jnp.concatenate([g_m[512:768], g_m[384:448], g_t], axis=0)[_W_IN_SPLIT:]


def _uq_to_kernel(wuq):
    w3 = wuq.reshape(MQR, MH, NOPE + ROPE)
    rope = jnp.concatenate([w3[:, :, NOPE:], jnp.zeros((MQR, MH, 64), wuq.dtype)], axis=2)
    return jnp.concatenate([w3[:, :, :NOPE].reshape(MQR, MH * 128), rope.reshape(MQR, MH * 128)], axis=1)


def _uq_from_kernel(g):
    nope = g[:, :1024].reshape(MQR, MH, 128)
    rope = g[:, 1024:].reshape(MQR, MH, 128)[:, :, :ROPE]
    return jnp.concatenate([nope, rope], axis=2)


def _ukv_to_kernel(wukv):
    w3 = wukv.reshape(MKR, MH, NOPE + MV)
    return jnp.concatenate([w3[:, :, :NOPE].reshape(MKR, MH * 128), w3[:, :, NOPE:].reshape(MKR, MH * 128)], axis=1)


def _ukv_from_kernel(g):
    return jnp.concatenate([g[:, :1024].reshape(MKR, MH, 128), g[:, 1024:].reshape(MKR, MH, 128)], axis=2)


def _cols_gathered(a):
    return a.transpose(1, 0, 2).reshape(a.shape[1], NDEV * a.shape[2])


def _cols_scattered(a):
    R = a.shape[0]
    return a.reshape(R, NDEV, a.shape[1] // NDEV).transpose(1, 0, 2)


_SMALL = (("gla_b_gate", 512), ("gla_norm_g", 256), ("mla_q_norm_g", 384), ("mla_kv_norm_g", 256),
          ("ln1_g", 1024), ("ln1_b", 1024), ("conv_b", 5632), ("ln2_g", 1024), ("ln2_b", 1024))
_SMALL_ROWS = 88
_SMALL_USED = sum(sz for _, sz in _SMALL)


def _pack_small(d):
    flat = jnp.concatenate([d[n].reshape(-1) for n, _ in _SMALL] + ([d['loss'].reshape(-1)] if 'loss' in d else []))
    return jnp.pad(flat, (0, _SMALL_ROWS * 128 - flat.shape[0])).reshape(_SMALL_ROWS, 128)


def _unpack_small(a):
    flat = a.reshape(-1)
    out, off = {}, 0
    for n, sz in _SMALL:
        out[n] = flat[off:off + sz].reshape(1, sz)
        off += sz
    return out


_NAMES = ['w_in', 'gla_w_gate_up', 'gla_b_gate', 'gla_norm_g', 'w_gla_o', 'mla_q_norm_g', 'mla_w_uq',
          'mla_kv_norm_g', 'mla_w_ukv', 'w_mla_o', 'w_out', 'ln1_g', 'ln1_b', 'w_up', 'conv_w', 'conv_b',
          'w_down', 'ln2_g', 'ln2_b']
_SHARDED = ['w_in', 'w_up', 'w_down', 'w_gla_o', 'w_mla_o', 'w_out', 'mla_w_uq', 'mla_w_ukv', 'gla_w_gate_up',
            'conv_w']


def kernel(x, positions, w_in, gla_w_gate_up, gla_b_gate, gla_norm_g, w_gla_o, mla_q_norm_g, mla_w_uq, mla_kv_norm_g, mla_w_ukv, w_mla_o, w_out, ln1_g, ln1_b, w_up, conv_w, conv_b, w_down, ln2_g, ln2_b, loss_target, m_w_in, m_gla_w_gate_up, m_gla_b_gate, m_gla_norm_g, m_w_gla_o, m_mla_q_norm_g, m_mla_w_uq, m_mla_kv_norm_g, m_mla_w_ukv, m_w_mla_o, m_w_out, m_ln1_g, m_ln1_b, m_w_up, m_conv_w, m_conv_b, m_w_down, m_ln2_g, m_ln2_b, v_w_in, v_gla_w_gate_up, v_gla_b_gate, v_gla_norm_g, v_w_gla_o, v_mla_q_norm_g, v_mla_w_uq, v_mla_kv_norm_g, v_mla_w_ukv, v_w_mla_o, v_w_out, v_ln1_g, v_ln1_b, v_w_up, v_conv_w, v_conv_b, v_w_down, v_ln2_g, v_ln2_b):
    W = dict(w_in=w_in, gla_w_gate_up=gla_w_gate_up, gla_b_gate=gla_b_gate, gla_norm_g=gla_norm_g, w_gla_o=w_gla_o, mla_q_norm_g=mla_q_norm_g, mla_w_uq=mla_w_uq, mla_kv_norm_g=mla_kv_norm_g, mla_w_ukv=mla_w_ukv, w_mla_o=w_mla_o, w_out=w_out, ln1_g=ln1_g, ln1_b=ln1_b, w_up=w_up, conv_w=conv_w, conv_b=conv_b, w_down=w_down, ln2_g=ln2_g, ln2_b=ln2_b)
    M = dict(w_in=m_w_in, gla_w_gate_up=m_gla_w_gate_up, gla_b_gate=m_gla_b_gate, gla_norm_g=m_gla_norm_g, w_gla_o=m_w_gla_o, mla_q_norm_g=m_mla_q_norm_g, mla_w_uq=m_mla_w_uq, mla_kv_norm_g=m_mla_kv_norm_g, mla_w_ukv=m_mla_w_ukv, w_mla_o=m_w_mla_o, w_out=m_w_out, ln1_g=m_ln1_g, ln1_b=m_ln1_b, w_up=m_w_up, conv_w=m_conv_w, conv_b=m_conv_b, w_down=m_w_down, ln2_g=m_ln2_g, ln2_b=m_ln2_b)
    V = dict(w_in=v_w_in, gla_w_gate_up=v_gla_w_gate_up, gla_b_gate=v_gla_b_gate, gla_norm_g=v_gla_norm_g, w_gla_o=v_w_gla_o, mla_q_norm_g=v_mla_q_norm_g, mla_w_uq=v_mla_w_uq, mla_kv_norm_g=v_mla_kv_norm_g, mla_w_ukv=v_mla_w_ukv, w_mla_o=v_w_mla_o, w_out=v_w_out, ln1_g=v_ln1_g, ln1_b=v_ln1_b, w_up=v_w_up, conv_w=v_conv_w, conv_b=v_conv_b, w_down=v_w_down, ln2_g=v_ln2_g, ln2_b=v_ln2_b)

    tshard = lambda d, n: d[n][0].T
    shard = lambda n: (W[n][0].astype(BF), False)
    first = ['w_in', 'mla_w_uq', 'mla_w_ukv', 'gla_w_gate_up']
    G = dict(zip(first, _gather_two_level(
        [tshard(W, 'w_in').astype(BF)] + [shard(n)[0] for n in first[1:]], name="gather_w0")))
    w_gt, w_mt, w_tt = _w_in_to_groups(G['w_in'].reshape(NDEV * 730, D))
    kw = dict(
        w_gt=w_gt, w_mt=w_mt, w_tt=w_tt,
        wg=jnp.pad(_cols_gathered(G['gla_w_gate_up']), ((0, 128 - GR), (0, 0))), bg=W['gla_b_gate'],
        gn=W['gla_norm_g'], gq=W['mla_q_norm_g'], gkv=W['mla_kv_norm_g'],
        wuq=_uq_to_kernel(_cols_gathered(G['mla_w_uq'])), wukv=_ukv_to_kernel(_cols_gathered(G['mla_w_ukv'])),
        g1=W['ln1_g'], b1=W['ln1_b'], g2=W['ln2_g'], b2=W['ln2_b'], cb=W['conv_b'],
    )
    received = {}

    def got_out_proj(ex, w, g):
        w.update(wgo=ex[0].reshape(D, D), wmo=ex[1].reshape(D, D), wout=ex[2].reshape(D, D))

    def got_ffn(ex, w, g):
        w_upt = ex[0].reshape(2 * DFF, D)
        w.update(wugt=w_upt[:DFF], wuvt=w_upt[DFF:], wug=w_upt[:DFF].T, wuv=w_upt[DFF:].T,
                 wd=ex[1].reshape(DFF, D), cw=_cols_gathered(ex[2]))

    slab = lambda a, lo=0: ([(a.astype(BF), lo)], True)
    rows = lambda a, n=NDEV: a.reshape(n, a.shape[0] // n, a.shape[1])

    def keep(names):
        return lambda ex, w, g: received.update(zip(names, ex))

    def small_grads(g):
        return _pack_small(dict(gla_b_gate=g['bg'], gla_norm_g=g['gn'], mla_q_norm_g=g['gq'], mla_kv_norm_g=g['gkv'],
                                ln1_g=g['g1'], ln1_b=g['b1'], conv_b=g['cb'], ln2_g=g['g2'], ln2_b=g['b2'],
                                loss=g['loss']))

    hooks = {
        "proj_g": (lambda w, g: [shard('w_gla_o'), shard('w_mla_o'), shard('w_out')], got_out_proj),
        "flash_fwd": (lambda w, g: [(tshard(W, 'w_up').astype(BF), False), shard('w_down'), (W['conv_w'][0], False)],
                      got_ffn),
        "flash_bwd": (lambda w, g: [slab(rows(g['wd'])),
                                    ([(rows(g['wugt'], 4), 0), (rows(g['wuvt'], 4), 4)], True),
                                    slab(rows(g['wout'])), slab(rows(g['wgo'])), slab(rows(g['wmo']))],
                      keep(['w_down', 'w_up', 'w_out', 'w_gla_o', 'w_mla_o'])),
        "gla_bwd": (lambda w, g: [slab(_uq_from_kernel(g['wuq']).transpose(1, 0, 2)),
                                  slab(_ukv_from_kernel(g['wukv']).transpose(1, 0, 2)),
                                  slab(rows(_w_in_rows_hi(g['w_mt'], g['w_tt']), NDEV - _W_IN_LO), _W_IN_LO)],
                    keep(['mla_w_uq', 'mla_w_ukv', 'w_in_hi'])),
        "dx": (lambda w, g: [slab(rows(_w_in_rows_lo(g['w_gt'], g['w_mt']), _W_IN_LO)),
                             ([(_cols_scattered(g['wg'][:GR]), 0)], True), ([(_cols_scattered(g['cw']), 0)], True),
                             (small_grads(g), False)],
               keep(['w_in_lo', 'gla_w_gate_up', 'conv_w', 'small'])),
    }

    _, grad_x, _ = _local_step(x, positions, loss_target, kw, hooks)

    grads, deltas, new_m, new_v = {}, {}, {}, {}
    small_parts = received['small']
    loss = jnp.sum(small_parts.reshape(NDEV, -1)[:, _SMALL_USED])
    me = 4 * lax.axis_index("x") + 2 * lax.axis_index("y") + lax.axis_index("c")
    received['w_in'] = jnp.where(me >= _W_IN_LO, received['w_in_hi'], received['w_in_lo'])
    for n in _SHARDED:
        shp = W[n].shape
        if n in ('w_in', 'w_up'):
            out = _adamw(received[n], tshard(W, n), tshard(M, n), tshard(V, n), name="adamw_" + n)
            grads[n], deltas[n], new_m[n], new_v[n] = [t.T.reshape(shp) for t in out]
            continue
        out = _adamw(received[n], W[n][0], M[n][0], V[n][0], name="adamw_" + n)
        grads[n], deltas[n], new_m[n], new_v[n] = [t.reshape(shp) for t in out]
    out = _adamw(small_parts, _pack_small(W), _pack_small(M), _pack_small(V), name="adamw_small")
    for dst, packed in zip((grads, deltas, new_m, new_v), out):
        dst.update(_unpack_small(packed))

    return (loss, grad_x, *[grads[n] for n in _NAMES], *[deltas[n] for n in _NAMES],
            *[new_m[n] for n in _NAMES], *[new_v[n] for n in _NAMES])
```

```python
import functools

import jax
import jax.numpy as jnp
from jax import lax
from jax.experimental import pallas as pl
from jax.experimental.pallas import tpu as pltpu

F32 = jnp.float32
BF = jnp.bfloat16

D = 1024
GH, GDK, GDV, GR, GTAU, GC = 4, 128, 256, 16, 16.0, 64
MH, MQR, MKR, NOPE, ROPE, MV = 8, 384, 256, 128, 64, 128
THETA = 10000.0
DFF = 2816
ALPHA = 2.0 ** 0.25
LN_EPS = 1e-5
RMS_EPS = 1e-6
NDEV = 8
ADAM_LR, ADAM_B1, ADAM_B2, ADAM_EPS, ADAM_WD, ADAM_STEP = 0.001, 0.9, 0.999, 1e-08, 0.01, 10

PG_W = 3200
PM_W = 768
PT_W = 2048
NEG = -1e30
MESH_ID = pl.DeviceIdType.MESH
VMEM_MB = 1024 * 1024


V7X_VMEM_LIMIT_MB = 48
TOKEN_TM = 256
FLASH_TQ = 512
FFN_TN = 1408


def _params(sem):
    return pltpu.CompilerParams(dimension_semantics=sem, vmem_limit_bytes=V7X_VMEM_LIMIT_MB * VMEM_MB)


def _dot(a, b):
    return lax.dot_general(a, b, (((1,), (0,)), ((), ())), preferred_element_type=F32)


def _dot_nt(a, b):
    return lax.dot_general(a, b, (((1,), (1,)), ((), ())), preferred_element_type=F32)


def _dot_tn(a, b):
    return lax.dot_general(a, b, (((0,), (0,)), ((), ())), preferred_element_type=F32)


def _iota(shape, dim):
    return lax.broadcasted_iota(jnp.int32, shape, dim)


FLASH_HP = 2
FLASH_HP_FWD = 4
QK_SCALE = (NOPE + ROPE) ** -0.5
LOG2E = 1.4426950408889634
QK_SCALE_LOG2 = QK_SCALE * LOG2E


def _sigmoid(x):
    return 0.5 * jnp.tanh(0.5 * x) + 0.5


def _tri_mm(tri_bf, x):
    hi = x.astype(BF)
    r1 = x - hi.astype(F32)
    mid = r1.astype(BF)
    lo = (r1 - mid.astype(F32)).astype(BF)
    return _dot(tri_bf, hi) + _dot(tri_bf, mid) + _dot(tri_bf, lo)


def _matmul(a, b, mode, *, name, c_in=None, out_dtype=F32, tm=512, tn=512, tk=512, ride=None, emit_a=False):
    if mode == "nn":
        (M, K), (_, N) = a.shape, b.shape
    elif mode == "nt":
        (M, K), (N, _) = a.shape, b.shape
    else:
        (K, M), (_, N) = a.shape, b.shape
    tm, tn, tk = min(tm, M), min(tn, N), min(tk, K)
    assert M % tm == 0 and N % tn == 0 and K % tk == 0, (name, M, N, K, tm, tn, tk)
    nk = K // tk
    assert not emit_a or (nk == 1 and mode != "tn" and c_in is None and ride is None)
    dot = {"nn": _dot, "nt": _dot_nt, "tn": _dot_tn}[mode]

    def body(*refs):
        if emit_a:
            a_ref, b_ref, o_ref, xa_ref, acc_ref = refs
        elif c_in is None:
            a_ref, b_ref, o_ref, acc_ref = refs
        else:
            a_ref, b_ref, c_ref, o_ref, acc_ref = refs
        k = pl.program_id(2)

        @pl.when(k == 0)
        def _():
            if c_in is None:
                acc_ref[...] = jnp.zeros_like(acc_ref)
            else:
                acc_ref[...] = c_ref[...].astype(F32)

        if emit_a:
            @pl.when(pl.program_id(1) == 0)
            def _():
                xa_ref[...] = a_ref[...].astype(BF)

        acc_ref[...] += dot(a_ref[...].astype(BF), b_ref[...].astype(BF))

        @pl.when(k == nk - 1)
        def _():
            o_ref[...] = acc_ref[...].astype(out_dtype)

    if mode == "tn":
        a_spec = pl.BlockSpec((tk, tm), lambda i, j, k: (k, i))
    else:
        a_spec = pl.BlockSpec((tm, tk), lambda i, j, k: (i, k))
    if mode == "nt":
        b_spec = pl.BlockSpec((tn, tk), lambda i, j, k: (j, k))
    else:
        b_spec = pl.BlockSpec((tk, tn), lambda i, j, k: (k, j))
    in_specs = [a_spec, b_spec]
    args = [a, b]
    if c_in is not None:
        in_specs.append(pl.BlockSpec((tm, tn), lambda i, j, k: (i, j)))
        args.append(c_in)
    out_shape = (jax.ShapeDtypeStruct((M, N), out_dtype),)
    out_specs = (pl.BlockSpec((tm, tn), lambda i, j, k: (i, j)),)
    if emit_a:
        out_shape += (jax.ShapeDtypeStruct((M, K), BF),)
        out_specs += (pl.BlockSpec((tm, tk), lambda i, j, k: (i, k)),)
    res = _call(
        body, name=name, out_shape=out_shape, grid=(M // tm, N // tn, nk), in_specs=in_specs, out_specs=out_specs,
        scratch_shapes=[pltpu.VMEM((tm, tn), F32)],
        sem=("parallel", "arbitrary", "arbitrary"), args=args, ride=ride)
    if emit_a:
        return res[0], res[1]
    return res[0] if ride is None else (res[0][0], res[1])


def _matmul_sum(c_in, parts, *, name, tm=1024, ride=None):
    M, N = c_in.shape
    tm = min(tm, M)
    n_p = len(parts)
    counts = [a.shape[1] // tk for a, _, tk in parts]
    starts = [sum(counts[:p]) for p in range(n_p)]
    nk = sum(counts)

    def body(*refs):
        a_refs, w_refs = refs[:n_p], refs[n_p:2 * n_p]
        c_ref, o_ref, acc_ref = refs[2 * n_p:]
        k = pl.program_id(1)

        @pl.when(k == 0)
        def _():
            acc_ref[...] = c_ref[...]

        for p in range(n_p):
            @pl.when(jnp.logical_and(k >= starts[p], k < starts[p] + counts[p]))
            def _(p=p):
                acc_ref[...] += _dot(a_refs[p][...].astype(BF), w_refs[p][...].astype(BF))

        @pl.when(k == nk - 1)
        def _():
            o_ref[...] = acc_ref[...]

    def kidx(p):
        return lambda k: jnp.clip(k - starts[p], 0, counts[p] - 1)

    in_specs = [pl.BlockSpec((tm, tk), lambda i, k, f=kidx(p): (i, f(k))) for p, (_, _, tk) in enumerate(parts)]
    in_specs += [pl.BlockSpec((tk, N), lambda i, k, f=kidx(p): (f(k), 0)) for p, (_, _, tk) in enumerate(parts)]
    in_specs.append(pl.BlockSpec((tm, N), lambda i, k: (i, 0)))
    res = _call(
        body, name=name, out_shape=(jax.ShapeDtypeStruct((M, N), F32),), grid=(M // tm, nk),
        in_specs=in_specs, out_specs=(pl.BlockSpec((tm, N), lambda i, k: (i, 0)),),
        scratch_shapes=[pltpu.VMEM((tm, N), F32)], sem=("parallel", "arbitrary"),
        args=[a for a, _, _ in parts] + [w for _, w, _ in parts] + [c_in], ride=ride)
    return res[0] if ride is None else (res[0][0], res[1])


def _gla_gate(pg_ref, rows, wg_ref, bg_ref):
    r = pg_ref[rows, 3072:3200].astype(BF)
    logit = _dot(r, wg_ref[...]) + bg_ref[...]
    la = (jnp.minimum(logit, 0.0) - jnp.log(1.0 + jnp.exp(-jnp.abs(logit)))) * (1.0 / GTAU)
    return r, logit, la


def _gla_fwd(pg, wg, bg, gn, ltri, *, nseq, S, tm):
    T = pg.shape[0]
    nb, nc = S // tm, tm // GC
    qscale = GDK ** -0.5

    def body(pg_ref, wg_ref, bg_ref, gn_ref, l_ref, o_ref, zg_ref, st_ref, st_scr):
        @pl.when(pl.program_id(1) == 0)
        def _():
            st_scr[...] = jnp.zeros_like(st_scr)

        ltri_v = l_ref[...]
        causal = _iota((GC, GC), 0) >= _iota((GC, GC), 1)
        last_row = _iota((GC, GDK), 0) == GC - 1
        g = gn_ref[...]

        def chunk(c, carry):
            rows = pl.ds(pl.multiple_of(c * GC, GC), GC)
            _, _, la = _gla_gate(pg_ref, rows, wg_ref, bg_ref)
            b = _tri_mm(ltri_v, la)
            hs = range(GH)
            v, q_in, k_st, dec, st, a_raw, o_st, kv = [], [], [], [], [], [], [], []
            for h in hs:
                q = pg_ref[rows, h * GDK:(h + 1) * GDK]
                k = pg_ref[rows, 512 + h * GDK:512 + (h + 1) * GDK]
                v.append(pg_ref[rows, 1024 + h * GDV:1024 + (h + 1) * GDV].astype(BF))
                bh = b[:, h * GDK:(h + 1) * GDK]
                bl = jnp.sum(jnp.where(last_row, bh, 0.0), axis=0, keepdims=True)
                q_in.append((q * (qscale * jnp.exp(bh))).astype(BF))
                k_in = (k * jnp.exp(-bh)).astype(BF)
                k_st.append((k * jnp.exp(bl - bh)).astype(BF))
                dec.append(jnp.exp(bl))
                st.append(st_scr[h])
                st_ref[c, h] = st[h]
                a_raw.append(_dot_nt(q_in[h], k_in))
            for h in hs:
                o_st.append(_dot_nt(q_in[h], st[h].astype(BF)))
                kv.append(_dot_tn(v[h], k_st[h]))
            att = [jnp.where(causal, a_raw[h], 0.0).astype(BF) for h in hs]
            o = [_dot(att[h], v[h]) + o_st[h] for h in hs]
            for h in hs:
                st_scr[h] = st[h] * dec[h] + kv[h]
                og = pg_ref[rows, 2048 + h * GDV:2048 + (h + 1) * GDV]
                rstd = lax.rsqrt(jnp.mean(o[h] * o[h], axis=-1, keepdims=True) + RMS_EPS)
                o_ref[rows, h * GDV:(h + 1) * GDV] = o[h]
                zg_ref[rows, h * GDV:(h + 1) * GDV] = (o[h] * rstd * g * (og * _sigmoid(og))).astype(BF)
            return carry

        lax.fori_loop(0, nc, chunk, 0, unroll=True)

    full = lambda shp: pl.BlockSpec(shp, lambda b_, i: (0,) * len(shp))
    return pl.pallas_call(
        body, name="gla_fwd",
        out_shape=(jax.ShapeDtypeStruct((T, GH * GDV), F32),
                   jax.ShapeDtypeStruct((T, GH * GDV), BF),
                   jax.ShapeDtypeStruct((T // GC, GH, GDV, GDK), F32)),
        grid=(nseq, nb),
        in_specs=[pl.BlockSpec((tm, PG_W), lambda b_, i: (b_ * nb + i, 0)),
                  full((128, 512)), full((1, 512)), full((1, GDV)), full((GC, GC))],
        out_specs=(pl.BlockSpec((tm, GH * GDV), lambda b_, i: (b_ * nb + i, 0)),
                   pl.BlockSpec((tm, GH * GDV), lambda b_, i: (b_ * nb + i, 0)),
                   pl.BlockSpec((nc, GH, GDV, GDK), lambda b_, i: (b_ * nb + i, 0, 0, 0))),
        scratch_shapes=[pltpu.VMEM((GH, GDV, GDK), F32)],
        compiler_params=_params(("parallel", "arbitrary")),
    )(pg, wg, bg, gn, ltri)


def _gla_bwd(pg, wg, bg, gn, ltri, utri, o, states, dzg, *, nseq, S, tm, ride=None):
    T = pg.shape[0]
    nb, nc = S // tm, tm // GC
    qscale = GDK ** -0.5

    def body(pg_ref, wg_ref, bg_ref, gn_ref, l_ref, u_ref, o_ref, st_ref, dzg_ref,
             dpg_ref, dwg_ref, dbg_ref, dgn_ref, dst_scr):
        first = jnp.logical_and(pl.program_id(0) == 0, pl.program_id(1) == 0)

        @pl.when(first)
        def _():
            dwg_ref[...] = jnp.zeros_like(dwg_ref)
            dbg_ref[...] = jnp.zeros_like(dbg_ref)
            dgn_ref[...] = jnp.zeros_like(dgn_ref)

        @pl.when(pl.program_id(1) == 0)
        def _():
            dst_scr[...] = jnp.zeros_like(dst_scr)

        ltri_v = l_ref[...]
        utri_v = u_ref[...]
        causal = _iota((GC, GC), 0) >= _iota((GC, GC), 1)
        last_row = _iota((GC, GDK), 0) == GC - 1
        g = gn_ref[...]

        def chunk(cc, carry):
            c = nc - 1 - cc
            rows = pl.ds(pl.multiple_of(c * GC, GC), GC)
            r, logit, la = _gla_gate(pg_ref, rows, wg_ref, bg_ref)
            b = _tri_mm(ltri_v, la)
            hs = range(GH)
            L = lambda: [None] * GH
            vb, eb, enb, ek, dec, q_in, k_in, k_st, q_inb, k_inb, st, dst, dob = (L() for _ in range(13))
            a_raw, da_raw, dq_st, dks, dv_st, dst_new, dbs, dgn = (L() for _ in range(8))
            for h in hs:
                q = pg_ref[rows, h * GDK:(h + 1) * GDK]
                k = pg_ref[rows, 512 + h * GDK:512 + (h + 1) * GDK]
                vb[h] = pg_ref[rows, 1024 + h * GDV:1024 + (h + 1) * GDV].astype(BF)
                og = pg_ref[rows, 2048 + h * GDV:2048 + (h + 1) * GDV]
                oh = o_ref[rows, h * GDV:(h + 1) * GDV]
                dz = dzg_ref[rows, h * GDV:(h + 1) * GDV].astype(F32)
                bh = b[:, h * GDK:(h + 1) * GDK]
                bl = jnp.sum(jnp.where(last_row, bh, 0.0), axis=0, keepdims=True)
                eb[h] = qscale * jnp.exp(bh)
                enb[h] = jnp.exp(-bh)
                ek[h] = jnp.exp(bl - bh)
                dec[h] = jnp.exp(bl)
                q_in[h], k_in[h], k_st[h] = q * eb[h], k * enb[h], k * ek[h]
                q_inb[h], k_inb[h] = q_in[h].astype(BF), k_in[h].astype(BF)
                st[h] = st_ref[c, h]
                dst[h] = dst_scr[h]
                rstd = lax.rsqrt(jnp.mean(oh * oh, axis=-1, keepdims=True) + RMS_EPS)
                ohat = oh * rstd
                sg = _sigmoid(og)
                don = dz * (og * sg)
                dpg_ref[rows, 2048 + h * GDV:2048 + (h + 1) * GDV] = (
                    dz * (ohat * g) * (sg * (1.0 + og * (1.0 - sg)))).astype(BF)
                dgn[h] = jnp.sum(don * ohat, axis=0, keepdims=True)
                gd = don * g
                dob[h] = (rstd * (gd - ohat * jnp.mean(gd * ohat, axis=-1, keepdims=True))).astype(BF)
                a_raw[h] = _dot_nt(q_inb[h], k_inb[h])
                da_raw[h] = _dot_nt(dob[h], vb[h])
            dgn_ref[...] += dgn[0] + dgn[1] + dgn[2] + dgn[3]
            for h in hs:
                dstb = dst[h].astype(BF)
                dq_st[h] = _dot(dob[h], st[h].astype(BF))
                dks[h] = _dot(vb[h], dstb)
                dv_st[h] = _dot_nt(k_st[h].astype(BF), dstb)
                dst_new[h] = _dot_tn(dob[h], q_inb[h])
            att = [jnp.where(causal, a_raw[h], 0.0).astype(BF) for h in hs]
            da = [jnp.where(causal, da_raw[h], 0.0).astype(BF) for h in hs]
            dqi = [_dot(da[h], k_inb[h]) + dq_st[h] for h in hs]
            dki = [_dot_tn(da[h], q_inb[h]) for h in hs]
            dv = [_dot_tn(att[h], dob[h]) + dv_st[h] for h in hs]
            for h in hs:
                dd = jnp.sum(dst[h] * st[h], axis=0, keepdims=True)
                dst_scr[h] = dst[h] * dec[h] + dst_new[h]
                kk = dks[h] * k_st[h]
                dbl = jnp.sum(kk, axis=0, keepdims=True) + dd * dec[h]
                db = dqi[h] * q_in[h] - dki[h] * k_in[h] - kk
                dbs[h] = db + jnp.where(last_row, dbl, 0.0)
                dpg_ref[rows, h * GDK:(h + 1) * GDK] = (dqi[h] * eb[h]).astype(BF)
                dpg_ref[rows, 512 + h * GDK:512 + (h + 1) * GDK] = (dki[h] * enb[h] + dks[h] * ek[h]).astype(BF)
                dpg_ref[rows, 1024 + h * GDV:1024 + (h + 1) * GDV] = dv[h].astype(BF)
            dla = _tri_mm(utri_v, jnp.concatenate(dbs, axis=1))
            dlogit = dla * (1.0 / GTAU) * _sigmoid(-logit)
            dlb = dlogit.astype(BF)
            dpg_ref[rows, 3072:3200] = _dot_nt(dlb, wg_ref[...]).astype(BF)
            dwg_ref[...] += _dot_tn(r, dlb)
            dbg_ref[...] += jnp.sum(dlogit, axis=0, keepdims=True)
            return carry

        lax.fori_loop(0, nc, chunk, 0, unroll=True)

    full = lambda shp: pl.BlockSpec(shp, lambda b_, i: (0,) * len(shp))
    rev = lambda b_, i: (b_ * nb + nb - 1 - i, 0)
    return _call(
        body, name="gla_bwd", ride=ride, sem=("arbitrary", "arbitrary"),
        args=(pg, wg, bg, gn, ltri, utri, o, states, dzg),
        out_shape=(jax.ShapeDtypeStruct((T, PG_W), BF),
                   jax.ShapeDtypeStruct((128, 512), F32),
                   jax.ShapeDtypeStruct((1, 512), F32),
                   jax.ShapeDtypeStruct((1, GDV), F32)),
        grid=(nseq, nb),
        in_specs=[pl.BlockSpec((tm, PG_W), rev),
                  full((128, 512)), full((1, 512)), full((1, GDV)), full((GC, GC)), full((GC, GC)),
                  pl.BlockSpec((tm, GH * GDV), rev),
                  pl.BlockSpec((nc, GH, GDV, GDK), lambda b_, i: (b_ * nb + nb - 1 - i, 0, 0, 0)),
                  pl.BlockSpec((tm, GH * GDV), rev)],
        out_specs=(pl.BlockSpec((tm, PG_W), rev), full((128, 512)), full((1, 512)), full((1, GDV))),
        scratch_shapes=[pltpu.VMEM((GH, GDV, GDK), F32)])


def _rope_tables(pos, invf):
    ang = pos.astype(F32) * invf
    lane = _iota(ang.shape, 1)
    sin = jnp.sin(ang)
    ssin = jnp.where(lane < 32, -sin, jnp.where(lane < 64, sin, 0.0))
    return jnp.cos(ang), ssin, lane


def _rope(x, cos, ssin, lane, sign):
    rot = jnp.where(lane < 32, pltpu.roll(x, 96, 1), pltpu.roll(x, 32, 1))
    return x * cos + sign * (rot * ssin)


def _rms_fwd(x, g):
    rstd = lax.rsqrt(jnp.mean(x * x, axis=-1, keepdims=True) + RMS_EPS)
    return x * rstd * g, x * rstd, rstd


def _rms_bwd(dy, xhat, rstd, g):
    gd = dy * g
    return rstd * (gd - xhat * jnp.mean(gd * xhat, axis=-1, keepdims=True)), jnp.sum(dy * xhat, axis=0, keepdims=True)


def _mla_prep_fwd(pm, pos, invf, gq, gkv, wuq, wukv, *, tm):
    T = pm.shape[0]

    def body(pm_ref, pos_ref, invf_ref, gq_ref, gkv_ref, wuq_ref, wukv_ref, qc_ref, kc_ref, v_ref):
        cos, ssin, lane = _rope_tables(pos_ref[...], invf_ref[...])
        cq, _, _ = _rms_fwd(pm_ref[:, 0:MQR], gq_ref[...])
        ckv, _, _ = _rms_fwd(pm_ref[:, 512:768], gkv_ref[...])
        qf = _dot(cq.astype(BF), wuq_ref[...])
        kvf = _dot(ckv.astype(BF), wukv_ref[...])
        kr = _rope(pm_ref[:, 384:512], cos, ssin, lane, 1.0).astype(BF)
        for h in range(MH):
            qc_ref[:, 256 * h:256 * h + 128] = (QK_SCALE_LOG2 * qf[:, 128 * h:128 * h + 128]).astype(BF)
            qr = qf[:, 1024 + 128 * h:1024 + 128 * h + 128]
            qc_ref[:, 256 * h + 128:256 * h + 256] = (QK_SCALE_LOG2 * _rope(qr, cos, ssin, lane, 1.0)).astype(BF)
            kc_ref[:, 256 * h:256 * h + 128] = kvf[:, 128 * h:128 * h + 128].astype(BF)
            kc_ref[:, 256 * h + 128:256 * h + 256] = kr
        v_ref[...] = kvf[:, 1024:2048].astype(BF)

    full = lambda shp: pl.BlockSpec(shp, lambda i: (0,) * len(shp))
    row = lambda w: pl.BlockSpec((tm, w), lambda i: (i, 0))
    return pl.pallas_call(
        body, name="mla_prep_fwd",
        out_shape=(jax.ShapeDtypeStruct((T, MH * 256), BF), jax.ShapeDtypeStruct((T, MH * 256), BF),
                   jax.ShapeDtypeStruct((T, MH * MV), BF)),
        grid=(T // tm,),
        in_specs=[row(PM_W), row(1), full((1, 128)), full((1, MQR)), full((1, MKR)),
                  full((MQR, 2048)), full((MKR, 2048))],
        out_specs=(row(MH * 256), row(MH * 256), row(MH * MV)),
        compiler_params=_params(("parallel",)),
    )(pm, pos, invf, gq, gkv, wuq, wukv)


def _mla_prep_bwd(pm, pos, invf, gq, gkv, wuq, wukv, dqc, dkc, dv, *, tm):
    T = pm.shape[0]

    def body(pm_ref, pos_ref, invf_ref, gq_ref, gkv_ref, wuq_ref, wukv_ref, dqc_ref, dkc_ref, dv_ref,
             dpm_ref, dwuq_ref, dwukv_ref, dgq_ref, dgkv_ref):
        @pl.when(pl.program_id(0) == 0)
        def _():
            dwuq_ref[...] = jnp.zeros_like(dwuq_ref)
            dwukv_ref[...] = jnp.zeros_like(dwukv_ref)
            dgq_ref[...] = jnp.zeros_like(dgq_ref)
            dgkv_ref[...] = jnp.zeros_like(dgkv_ref)

        cos, ssin, lane = _rope_tables(pos_ref[...], invf_ref[...])
        cq, cqh, cq_rstd = _rms_fwd(pm_ref[:, 0:MQR], gq_ref[...])
        ckv, ckvh, ckv_rstd = _rms_fwd(pm_ref[:, 512:768], gkv_ref[...])
        dqn, dqr, dkn = [], [], []
        dkr = jnp.zeros((tm, 128), F32)
        for h in range(MH):
            dqn.append(dqc_ref[:, 256 * h:256 * h + 128].astype(BF))
            dqr.append(_rope(dqc_ref[:, 256 * h + 128:256 * h + 256], cos, ssin, lane, -1.0).astype(BF))
            dkn.append(dkc_ref[:, 256 * h:256 * h + 128].astype(BF))
            dkr = dkr + dkc_ref[:, 256 * h + 128:256 * h + 256]
        dqf = jnp.concatenate(dqn + dqr, axis=1)
        dkvf = jnp.concatenate(dkn + [dv_ref[...].astype(BF)], axis=1)
        dwuq_ref[...] += _dot_tn(cq.astype(BF), dqf)
        dwukv_ref[...] += _dot_tn(ckv.astype(BF), dkvf)
        dcq, dgq = _rms_bwd(_dot_nt(dqf, wuq_ref[...]), cqh, cq_rstd, gq_ref[...])
        dckv, dgkv = _rms_bwd(_dot_nt(dkvf, wukv_ref[...]), ckvh, ckv_rstd, gkv_ref[...])
        dgq_ref[...] += dgq
        dgkv_ref[...] += dgkv
        dpm_ref[:, 0:MQR] = dcq.astype(BF)
        dpm_ref[:, 384:512] = _rope(dkr, cos, ssin, lane, -1.0).astype(BF)
        dpm_ref[:, 512:768] = dckv.astype(BF)

    full = lambda shp: pl.BlockSpec(shp, lambda i: (0,) * len(shp))
    row = lambda w: pl.BlockSpec((tm, w), lambda i: (i, 0))
    return pl.pallas_call(
        body, name="mla_prep_bwd",
        out_shape=(jax.ShapeDtypeStruct((T, PM_W), BF), jax.ShapeDtypeStruct((MQR, 2048), F32),
                   jax.ShapeDtypeStruct((MKR, 2048), F32), jax.ShapeDtypeStruct((1, MQR), F32),
                   jax.ShapeDtypeStruct((1, MKR), F32)),
        grid=(T // tm,),
        in_specs=[row(PM_W), row(1), full((1, 128)), full((1, MQR)), full((1, MKR)),
                  full((MQR, 2048)), full((MKR, 2048)), row(MH * 256), row(MH * 256), row(MH * MV)],
        out_specs=(row(PM_W), full((MQR, 2048)), full((MKR, 2048)), full((1, MQR)), full((1, MKR))),
        compiler_params=_params(("arbitrary",)),
    )(pm, pos, invf, gq, gkv, wuq, wukv, dqc, dkc, dv)


def _flash_fwd(qc, kc, v, *, nseq, S, tq, ride=None):
    T = qc.shape[0]
    nq = S // tq
    hp = FLASH_HP_FWD

    def body(q_ref, k_ref, v_ref, o_ref, lse_ref):
        i = pl.program_id(2)
        causal = _iota((tq, tq), 0) >= _iota((tq, tq), 1)

        def step(j, carry, masked):
            rows = pl.ds(pl.multiple_of(j * tq, tq), tq)
            hs = range(hp)
            s = [_dot_nt(q_ref[:, 256 * hh:256 * hh + 256], k_ref[rows, 256 * hh:256 * hh + 256]) for hh in hs]
            p, stats = [], []
            for hh in hs:
                m, l, _ = carry[hh]
                sh = jnp.where(causal, s[hh], NEG) if masked else s[hh]
                m_new = jnp.maximum(m, jnp.max(sh, axis=-1, keepdims=True))
                ph = jnp.exp2(sh - m_new)
                a = jnp.exp2(m - m_new)
                stats.append((m_new, a * l + jnp.sum(ph, axis=-1, keepdims=True), a))
                p.append(ph.astype(BF))
            pv = [_dot(p[hh], v_ref[rows, MV * hh:MV * hh + MV]) for hh in hs]
            return tuple((stats[hh][0], stats[hh][1], stats[hh][2] * carry[hh][2] + pv[hh]) for hh in hs)

        init = ((jnp.full((tq, 1), NEG, F32), jnp.zeros((tq, 1), F32), jnp.zeros((tq, MV), F32)),) * hp
        carry = lax.fori_loop(0, i, lambda j, c: step(j, c, False), init)
        for hh, (m, l, acc) in enumerate(step(i, carry, True)):
            o_ref[:, MV * hh:MV * hh + MV] = (acc / l).astype(BF)
            lse_ref[:, 128 * hh:128 * hh + 128] = jnp.broadcast_to(m + jnp.log2(l), (tq, 128))

    return _call(
        body, name="flash_fwd", ride=ride, sem=("parallel", "parallel", "arbitrary"), args=(qc, kc, v),
        out_shape=(jax.ShapeDtypeStruct((T, MH * MV), BF), jax.ShapeDtypeStruct((T, MH * 128), F32)),
        grid=(nseq, MH // hp, nq),
        in_specs=[pl.BlockSpec((tq, 256 * hp), lambda b_, h, i: (b_ * nq + i, h)),
                  pl.BlockSpec((S, 256 * hp), lambda b_, h, i: (b_, h)),
                  pl.BlockSpec((S, MV * hp), lambda b_, h, i: (b_, h))],
        out_specs=(pl.BlockSpec((tq, MV * hp), lambda b_, h, i: (b_ * nq + i, h)),
                   pl.BlockSpec((tq, 128 * hp), lambda b_, h, i: (b_ * nq + i, h))))


def _flash_bwd(qc, kc, v, o, do, lse, *, nseq, S, tq, ride=None):
    T = qc.shape[0]
    nq = S // tq

    def body(q_ref, k_ref, v_ref, o_ref, do_ref, lse_ref, dq_ref, dk_ref, dv_ref, dq_scr, delta_scr):
        j = pl.program_id(2)

        @pl.when(j == 0)
        def _():
            dq_scr[...] = jnp.zeros_like(dq_scr)
            for hh in range(FLASH_HP):
                od = o_ref[:, MV * hh:MV * hh + MV].astype(F32) * do_ref[:, MV * hh:MV * hh + MV].astype(F32)
                delta_scr[:, 128 * hh:128 * hh + 128] = jnp.broadcast_to(jnp.sum(od, axis=-1, keepdims=True), (S, 128))

        causal = _iota((tq, tq), 0) >= _iota((tq, tq), 1)

        def step(i, carry, masked):
            rows = pl.ds(pl.multiple_of(i * tq, tq), tq)
            hs = range(FLASH_HP)
            qs = [slice(256 * hh, 256 * hh + 256) for hh in hs]
            vs = [slice(MV * hh, MV * hh + MV) for hh in hs]
            ls = [slice(128 * hh, 128 * hh + 1) for hh in hs]
            s = [_dot_nt(q_ref[rows, qs[hh]], k_ref[:, qs[hh]]) for hh in hs]
            dp = [_dot_nt(do_ref[rows, vs[hh]], v_ref[:, vs[hh]]) for hh in hs]
            pb, ds = [], []
            for hh in hs:
                p = jnp.exp2(s[hh] - lse_ref[rows, ls[hh]])
                if masked:
                    p = jnp.where(causal, p, 0.0)
                pb.append(p.astype(BF))
                ds.append((p * (dp[hh] - delta_scr[rows, ls[hh]])).astype(BF))
            dv = [carry[hh][1] + _dot_tn(pb[hh], do_ref[rows, vs[hh]]) for hh in hs]
            dk = [carry[hh][0] + _dot_tn(ds[hh], q_ref[rows, qs[hh]]) for hh in hs]
            for hh in hs:
                dq_scr[rows, qs[hh]] += _dot(ds[hh], k_ref[:, qs[hh]])
            return tuple((dk[hh], dv[hh]) for hh in hs)

        init = ((jnp.zeros((tq, 256), F32), jnp.zeros((tq, MV), F32)),) * FLASH_HP
        carry = step(j, init, True)
        carry = lax.fori_loop(j + 1, nq, lambda i, c: step(i, c, False), carry)
        for hh, (dk, dv) in enumerate(carry):
            dk_ref[:, 256 * hh:256 * hh + 256] = dk * (1.0 / LOG2E)
            dv_ref[:, MV * hh:MV * hh + MV] = dv

        @pl.when(j == nq - 1)
        def _():
            dq_ref[...] = dq_scr[...] * QK_SCALE

    hp = FLASH_HP
    seq = lambda w: pl.BlockSpec((S, w * hp), lambda b_, h, j: (b_, h))
    blk = lambda w: pl.BlockSpec((tq, w * hp), lambda b_, h, j: (b_ * nq + j, h))
    return _call(
        body, name="flash_bwd", ride=ride, sem=("parallel", "parallel", "arbitrary"), args=(qc, kc, v, o, do, lse),
        out_shape=(jax.ShapeDtypeStruct((T, MH * 256), F32), jax.ShapeDtypeStruct((T, MH * 256), F32),
                   jax.ShapeDtypeStruct((T, MH * MV), F32)),
        grid=(nseq, MH // hp, nq),
        in_specs=[seq(256), blk(256), blk(MV), seq(MV), seq(MV), seq(128)],
        out_specs=(seq(256), blk(256), blk(MV)),
        scratch_shapes=[pltpu.VMEM((S, 256 * hp), F32), pltpu.VMEM((S, 128 * hp), F32)])


def _ln_fwd(pre, g, b):
    mu = jnp.mean(pre, axis=-1, keepdims=True)
    xc = pre - mu
    rstd = lax.rsqrt(jnp.mean(xc * xc, axis=-1, keepdims=True) + LN_EPS)
    xhat = xc * rstd
    return xhat * g + b, xhat, rstd


def _ln_bwd(dy, xhat, rstd, g):
    dxh = dy * g
    dx = rstd * (dxh - jnp.mean(dxh, axis=-1, keepdims=True) - xhat * jnp.mean(dxh * xhat, axis=-1, keepdims=True))
    return dx, jnp.sum(dy * xhat, axis=0, keepdims=True), jnp.sum(dy, axis=0, keepdims=True)


def _post_attn_fwd(zg, attn, pt, x, wgo, wmo, wout, g1, b1, *, tm):
    T = x.shape[0]

    def body(zg_ref, at_ref, pt_ref, x_ref, wgo_ref, wmo_ref, wout_ref, g_ref, b_ref,
             yg_ref, ym_ref, mix_ref, pre_ref, h_ref, hb_ref):
        yg = _dot(zg_ref[...], wgo_ref[...])
        ym = _dot(at_ref[...], wmo_ref[...])
        mix = (_sigmoid(pt_ref[:, 0:D]) * yg + _sigmoid(pt_ref[:, D:2 * D]) * ym).astype(BF)
        pre = ALPHA * x_ref[...] + _dot(mix, wout_ref[...])
        h, _, _ = _ln_fwd(pre, g_ref[...], b_ref[...])
        yg_ref[...] = yg.astype(BF)
        ym_ref[...] = ym.astype(BF)
        mix_ref[...] = mix
        pre_ref[...] = pre
        h_ref[...] = h
        hb_ref[...] = h.astype(BF)

    full = lambda shp: pl.BlockSpec(shp, lambda i: (0,) * len(shp))
    row = lambda w: pl.BlockSpec((tm, w), lambda i: (i, 0))
    sd = lambda dt: jax.ShapeDtypeStruct((T, D), dt)
    return pl.pallas_call(
        body, name="post_attn_fwd",
        out_shape=(sd(BF), sd(BF), sd(BF), sd(F32), sd(F32), sd(BF)),
        grid=(T // tm,),
        in_specs=[row(D), row(D), row(PT_W), row(D), full((D, D)), full((D, D)), full((D, D)),
                  full((1, D)), full((1, D))],
        out_specs=(row(D),) * 6,
        compiler_params=_params(("parallel",)),
    )(zg, attn, pt, x, wgo, wmo, wout, g1, b1)


def _post_attn_bwd(dh, pre, pt, yg, ym, wgo, wmo, wout, g1, *, tm):
    T = dh.shape[0]

    def body(dh_ref, pre_ref, pt_ref, yg_ref, ym_ref, wgo_ref, wmo_ref, wout_ref, g_ref,
             dx_ref, dpreb_ref, dpt_ref, dygb_ref, dymb_ref, dzg_ref, dat_ref, dg_ref, db_ref):
        @pl.when(pl.program_id(0) == 0)
        def _():
            dg_ref[...] = jnp.zeros_like(dg_ref)
            db_ref[...] = jnp.zeros_like(db_ref)

        pre = pre_ref[...]
        mu = jnp.mean(pre, axis=-1, keepdims=True)
        xc = pre - mu
        rstd = lax.rsqrt(jnp.mean(xc * xc, axis=-1, keepdims=True) + LN_EPS)
        dpre, dg, db = _ln_bwd(dh_ref[...], xc * rstd, rstd, g_ref[...])
        dg_ref[...] += dg
        db_ref[...] += db
        dx_ref[...] = ALPHA * dpre
        dpreb = dpre.astype(BF)
        dpreb_ref[...] = dpreb
        dmix = _dot_nt(dpreb, wout_ref[...])
        sa = _sigmoid(pt_ref[:, 0:D])
        sb = _sigmoid(pt_ref[:, D:2 * D])
        dpt_ref[:, 0:D] = (dmix * yg_ref[...].astype(F32) * (sa * (1.0 - sa))).astype(BF)
        dpt_ref[:, D:2 * D] = (dmix * ym_ref[...].astype(F32) * (sb * (1.0 - sb))).astype(BF)
        dyg = (dmix * sa).astype(BF)
        dym = (dmix * sb).astype(BF)
        dygb_ref[...] = dyg
        dymb_ref[...] = dym
        dzg_ref[...] = _dot_nt(dyg, wgo_ref[...]).astype(BF)
        dat_ref[...] = _dot_nt(dym, wmo_ref[...]).astype(BF)

    full = lambda shp: pl.BlockSpec(shp, lambda i: (0,) * len(shp))
    row = lambda w: pl.BlockSpec((tm, w), lambda i: (i, 0))
    sd = lambda w, dt: jax.ShapeDtypeStruct((T, w), dt)
    return pl.pallas_call(
        body, name="post_attn_bwd",
        out_shape=(sd(D, F32), sd(D, BF), sd(PT_W, BF), sd(D, BF), sd(D, BF), sd(D, BF), sd(D, BF),
                   jax.ShapeDtypeStruct((1, D), F32), jax.ShapeDtypeStruct((1, D), F32)),
        grid=(T // tm,),
        in_specs=[row(D), row(D), row(PT_W), row(D), row(D), full((D, D)), full((D, D)), full((D, D)),
                  full((1, D))],
        out_specs=(row(D), row(D), row(PT_W), row(D), row(D), row(D), row(D), full((1, D)), full((1, D))),
        compiler_params=_params(("arbitrary",)),
    )(dh, pre, pt, yg, ym, wgo, wmo, wout, g1)


def _shift_down(u, prev, k):
    r = pltpu.roll(u, k, 0)
    p = pltpu.roll(prev, k, 0)
    head = jnp.where(_iota(p.shape, 0) < k, p, r[0:8, :])
    return jnp.concatenate([head, r[8:, :]], axis=0)


def _conv3(u, prev, w_ref, b_ref):
    return (w_ref[0:1, :] * _shift_down(u, prev, 2) + w_ref[1:2, :] * _shift_down(u, prev, 1)
            + w_ref[2:3, :] * u + b_ref[...])


def _ffn_up_fwd(hb, wug, wuv, cw, cb, *, S, tm, tn):
    T = hb.shape[0]
    nj, nbs = DFF // tn, S // tm

    def body(h_ref, wg_ref, wv_ref, cwg_ref, cwv_ref, cbg_ref, cbv_ref,
             ug_ref, uv_ref, ucg_ref, ucv_ref, f_ref, pg_scr, pv_scr):
        @pl.when(pl.program_id(1) % nbs == 0)
        def _():
            pg_scr[...] = jnp.zeros_like(pg_scr)
            pv_scr[...] = jnp.zeros_like(pv_scr)

        h = h_ref[...]
        ug = _dot(h, wg_ref[...])
        uv = _dot(h, wv_ref[...])
        ucg = _conv3(ug, pg_scr[...], cwg_ref, cbg_ref)
        ucv = _conv3(uv, pv_scr[...], cwv_ref, cbv_ref)
        pg_scr[...] = ug[tm - 8:, :]
        pv_scr[...] = uv[tm - 8:, :]
        ug_ref[...] = ug.astype(BF)
        uv_ref[...] = uv.astype(BF)
        ucg_ref[...] = ucg
        ucv_ref[...] = ucv
        f_ref[...] = (ucg * _sigmoid(ucg) * ucv).astype(BF)

    tile = pl.BlockSpec((tm, tn), lambda j, i: (i, j))
    return pl.pallas_call(
        body, name="ffn_up_fwd",
        out_shape=(jax.ShapeDtypeStruct((T, DFF), BF), jax.ShapeDtypeStruct((T, DFF), BF),
                   jax.ShapeDtypeStruct((T, DFF), F32), jax.ShapeDtypeStruct((T, DFF), F32),
                   jax.ShapeDtypeStruct((T, DFF), BF)),
        grid=(nj, T // tm),
        in_specs=[pl.BlockSpec((tm, D), lambda j, i: (i, 0)),
                  pl.BlockSpec((D, tn), lambda j, i: (0, j)), pl.BlockSpec((D, tn), lambda j, i: (0, j)),
                  pl.BlockSpec((3, tn), lambda j, i: (0, j)), pl.BlockSpec((3, tn), lambda j, i: (0, j + nj)),
                  pl.BlockSpec((1, tn), lambda j, i: (0, j)), pl.BlockSpec((1, tn), lambda j, i: (0, j + nj))],
        out_specs=(tile, tile, tile, tile, tile),
        scratch_shapes=[pltpu.VMEM((8, tn), F32), pltpu.VMEM((8, tn), F32)],
        compiler_params=_params(("parallel", "arbitrary")),
    )(hb, wug, wuv, cw, cw, cb, cb)


def _ffn_bwd(dpreb, wd, ug, uv, ucg, ucv, cw, *, S, tm, tn):
    T = dpreb.shape[0]
    nj, nb, nbs = DFF // tn, T // tm, S // tm
    r_, c_ = lax.broadcasted_iota(jnp.int32, (tm, tm), 0), lax.broadcasted_iota(jnp.int32, (tm, tm), 1)
    s1, s2 = (c_ == r_ + 1).astype(BF), (c_ == r_ + 2).astype(BF)

    def body(dp_ref, wd_ref, ug_ref, uv_ref, ucg_ref, ucv_ref, cwg_ref, cwv_ref, s1_ref, s2_ref,
             dug_ref, duv_ref, dcg_ref, dcv_ref, ng_scr, nv_scr):
        ii = pl.program_id(1)
        i = nb - 1 - ii
        tail_row = _iota((8, tn), 0)

        @pl.when(ii == 0)
        def _():
            dcg_ref[...] = jnp.zeros_like(dcg_ref)
            dcv_ref[...] = jnp.zeros_like(dcv_ref)

        @pl.when(i % nbs == nbs - 1)
        def _():
            ng_scr[...] = jnp.zeros_like(ng_scr)
            nv_scr[...] = jnp.zeros_like(nv_scr)

        df = _dot_nt(dp_ref[...], wd_ref[...])
        ucg = ucg_ref[...]
        sg = _sigmoid(ucg)
        ducg = df * ucv_ref[...] * (sg * (1.0 + ucg * (1.0 - sg)))
        ducv = df * (ucg * sg)

        def finish(duc, u_ref, w, nxt_scr, du_ref, dc_ref):
            nxt = nxt_scr[...]
            db = duc.astype(BF)

            def shifted(s_ref, k):
                r = _dot(s_ref[...], db)
                tail = jnp.where(tail_row >= 8 - k, pltpu.roll(nxt, 8 - k, 0), r[tm - 8:, :])
                return jnp.concatenate([r[:tm - 8, :], tail], axis=0)

            up1 = shifted(s1_ref, 1)
            up2 = shifted(s2_ref, 2)
            du_ref[...] = (w[2:3, :] * duc + w[1:2, :] * up1 + w[0:1, :] * up2).astype(BF)
            nxt_scr[...] = duc[0:8, :]
            u = u_ref[...].astype(F32)
            for row, z in enumerate((u * up2, u * up1, u * duc, duc)):
                dc_ref[row:row + 1, :] += jnp.sum(z, axis=0, keepdims=True)

        finish(ducg, ug_ref, cwg_ref, ng_scr, dug_ref, dcg_ref)
        finish(ducv, uv_ref, cwv_ref, nv_scr, duv_ref, dcv_ref)

    tile = pl.BlockSpec((tm, tn), lambda j, ii: (nb - 1 - ii, j))
    acc = pl.BlockSpec((8, tn), lambda j, ii: (0, j))
    return pl.pallas_call(
        body, name="ffn_bwd",
        out_shape=(jax.ShapeDtypeStruct((T, DFF), BF), jax.ShapeDtypeStruct((T, DFF), BF),
                   jax.ShapeDtypeStruct((8, DFF), F32), jax.ShapeDtypeStruct((8, DFF), F32)),
        grid=(nj, nb),
        in_specs=[pl.BlockSpec((tm, D), lambda j, ii: (nb - 1 - ii, 0)),
                  pl.BlockSpec((tn, D), lambda j, ii: (j, 0)),
                  tile, tile, tile, tile,
                  pl.BlockSpec((3, tn), lambda j, ii: (0, j)), pl.BlockSpec((3, tn), lambda j, ii: (0, j + nj)),
                  pl.BlockSpec((tm, tm), lambda j, ii: (0, 0)), pl.BlockSpec((tm, tm), lambda j, ii: (0, 0))],
        out_specs=(tile, tile, acc, acc),
        scratch_shapes=[pltpu.VMEM((8, tn), F32), pltpu.VMEM((8, tn), F32)],
        compiler_params=_params(("parallel", "arbitrary")),
    )(dpreb, wd, ug, uv, ucg, ucv, cw, cw, s1, s2)


def _down_ln2_loss(f_in, wd, h, target, g2, b2, *, tm):
    T = h.shape[0]

    def body(f_ref, wd_ref, h_ref, t_ref, g_ref, b_ref, dpb_ref, dh_ref, loss_ref, dg_ref, db_ref):
        @pl.when(pl.program_id(0) == 0)
        def _():
            loss_ref[...] = jnp.zeros_like(loss_ref)
            dg_ref[...] = jnp.zeros_like(dg_ref)
            db_ref[...] = jnp.zeros_like(db_ref)

        halves = [pl.ds(s * (tm // 2), tm // 2) for s in range(2)]
        f = [_dot(f_ref[hs, :], wd_ref[...]) for hs in halves]
        for hs, fh in zip(halves, f):
            pre = ALPHA * h_ref[hs, :] + fh
            out, xhat, rstd = _ln_fwd(pre, g_ref[...], b_ref[...])
            diff = out - t_ref[hs, :]
            loss_ref[...] += 0.5 * jnp.sum(jnp.mean(diff * diff, axis=-1, keepdims=True))
            dpre, dg, db = _ln_bwd(diff * (1.0 / D), xhat, rstd, g_ref[...])
            dg_ref[...] += dg
            db_ref[...] += db
            dpb_ref[hs, :] = dpre.astype(BF)
            dh_ref[hs, :] = ALPHA * dpre

    full = lambda shp: pl.BlockSpec(shp, lambda i: (0,) * len(shp))
    row = lambda w: pl.BlockSpec((tm, w), lambda i: (i, 0))
    return pl.pallas_call(
        body, name="down_ln2_loss",
        out_shape=(jax.ShapeDtypeStruct((T, D), BF), jax.ShapeDtypeStruct((T, D), F32),
                   jax.ShapeDtypeStruct((8, 128), F32), jax.ShapeDtypeStruct((1, D), F32),
                   jax.ShapeDtypeStruct((1, D), F32)),
        grid=(T // tm,),
        in_specs=[row(DFF), full((DFF, D)), row(D), row(D), full((1, D)), full((1, D))],
        out_specs=(row(D), row(D), full((8, 128)), full((1, D)), full((1, D))),
        compiler_params=_params(("arbitrary",)),
    )(f_in, wd, h, target, g2, b2)


def _adamw(parts, w, m, v, *, name):
    n, R, C = parts.shape
    tr, tc = R, C
    for cand in range(min(R, 256), 15, -1):
        if R % cand == 0 and cand % 16 == 0:
            tr = cand
            break
    if tr == R and R * C > 65536 and C % 256 == 0:
        tc = 256
    c1 = 1.0 - ADAM_B1 ** ADAM_STEP
    c2 = 1.0 - ADAM_B2 ** ADAM_STEP

    def body(p_ref, w_ref, m_ref, v_ref, g_ref, d_ref, nm_ref, nv_ref):
        g = p_ref[0].astype(F32)
        for s in range(1, n):
            g = g + p_ref[s].astype(F32)
        nm = ADAM_B1 * m_ref[...] + (1.0 - ADAM_B1) * g
        nv = ADAM_B2 * v_ref[...] + (1.0 - ADAM_B2) * (g * g)
        g_ref[...] = g
        nm_ref[...] = nm
        nv_ref[...] = nv
        d_ref[...] = -ADAM_LR * ((nm / c1) / (jnp.sqrt(nv / c2) + ADAM_EPS) + ADAM_WD * w_ref[...])

    blk = pl.BlockSpec((tr, tc), lambda i, j: (i, j))
    sd = jax.ShapeDtypeStruct((R, C), F32)
    return pl.pallas_call(
        body, name=name,
        out_shape=(sd, sd, sd, sd),
        grid=(R // tr, C // tc),
        in_specs=[pl.BlockSpec((n, tr, tc), lambda i, j: (0, i, j)), blk, blk, blk],
        out_specs=(blk, blk, blk, blk),
        compiler_params=_params(("parallel", "parallel")),
    )(parts, w, m, v)


class _Exchange:
    def __init__(self, items):
        self.items = [(src if sc else [(src, 0)], sc) for src, sc in items]
        self.arrays = [arr for srcs, _ in self.items for arr, _ in srcs]
        self.n = len(self.items)
        self.n_in = len(self.arrays)

    def out_shape(self):
        return tuple(jax.ShapeDtypeStruct((NDEV,) + (srcs[0][0].shape[1:] if sc else srcs[0][0].shape),
                                          srcs[0][0].dtype) for srcs, sc in self.items)

    def scratch(self):
        return [pltpu.SemaphoreType.DMA((self.n, NDEV - 1)), pltpu.SemaphoreType.DMA((self.n, NDEV - 1)),
                pltpu.SemaphoreType.DMA((self.n,))]

    def _emit(self, ins, outs, sems, phase):
        send_sems, recv_sems, loc_sems = sems
        x, y, c = lax.axis_index("x"), lax.axis_index("y"), lax.axis_index("c")
        me = 4 * x + 2 * y + c
        flip = lambda p, d: 1 - p if d else p

        def inside(p, lo, n):
            return None if (lo, n) == (0, NDEV) else jnp.logical_and(p >= lo, p < lo + n)

        def when(cond, fn):
            if cond is None:
                fn()
            else:
                pl.when(cond)(fn)

        pos = 0
        for a, (srcs, sc) in enumerate(self.items):
            refs = ins[pos:pos + len(srcs)]
            pos += len(srcs)
            ranges = [(lo, arr.shape[0]) if sc else (0, NDEV) for arr, lo in srcs]
            mine = [inside(me, lo, n) for lo, n in ranges]
            i_receive = None if None in mine else functools.reduce(jnp.logical_or, mine)
            for ref, (lo, n), cond in zip(refs, ranges, mine):
                def local(ref=ref, lo=lo):
                    cp = pltpu.make_async_copy(ref.at[me - lo] if sc else ref, outs[a].at[me], loc_sems.at[a])
                    cp.start() if phase == 0 else cp.wait()
                if phase != 1:
                    when(cond, local)
            for k in range(1, NDEV):
                px, py, pc = flip(x, k & 4), flip(y, k & 2), flip(c, k & 1)
                peer = 4 * px + 2 * py + pc
                mk = functools.partial(pltpu.make_async_remote_copy,
                                       send_sem=send_sems.at[a, k - 1], recv_sem=recv_sems.at[a, k - 1],
                                       device_id=(px, py, pc), device_id_type=MESH_ID)
                if phase == 1:
                    def arrival(mk=mk, peer=peer):
                        mk(src_ref=refs[0].at[0] if sc else refs[0], dst_ref=outs[a].at[peer]).wait_recv()
                    when(i_receive, arrival)
                    continue
                for ref, (lo, n) in zip(refs, ranges):
                    def send(mk=mk, ref=ref, lo=lo, peer=peer):
                        cp = mk(src_ref=ref.at[peer - lo] if sc else ref, dst_ref=outs[a].at[me])
                        cp.start() if phase == 0 else cp.wait_send()
                    when(inside(peer, lo, n), send)

    def start(self, ins, outs, sems):
        self._emit(ins, outs, sems, 0)

    def wait(self, ins, outs, sems):
        self._emit(ins, outs, sems, 1)
        self._emit(ins, outs, sems, 2)


def _call(body, *, name, grid, in_specs, out_specs, out_shape, args, scratch_shapes=(), sem=None, ride=None):
    if ride is None:
        return pl.pallas_call(body, name=name, grid=grid, in_specs=list(in_specs), out_specs=tuple(out_specs),
                              out_shape=tuple(out_shape), scratch_shapes=list(scratch_shapes),
                              compiler_params=_params(sem))(*args)
    n_in, n_out, n_scr, ne, ne_in = len(args), len(out_shape), len(scratch_shapes), ride.n, ride.n_in

    def ride_body(*refs):
        ins, ex_in = refs[:n_in], refs[n_in:n_in + ne_in]
        o0 = n_in + ne_in
        outs, ex_out = refs[o0:o0 + n_out], refs[o0 + n_out:o0 + n_out + ne]
        scr = refs[o0 + n_out + ne:o0 + n_out + ne + n_scr]
        sems = refs[o0 + n_out + ne + n_scr:]
        first = functools.reduce(jnp.logical_and, [pl.program_id(d) == 0 for d in range(len(grid))])
        last = functools.reduce(jnp.logical_and, [pl.program_id(d) == grid[d] - 1 for d in range(len(grid))])

        @pl.when(first)
        def _():
            ride.start(ex_in, ex_out, sems)

        body(*ins, *outs, *scr)

        @pl.when(last)
        def _():
            ride.wait(ex_in, ex_out, sems)

    anyspec = pl.BlockSpec(memory_space=pl.ANY)
    res = pl.pallas_call(
        ride_body, name=name, grid=grid,
        in_specs=list(in_specs) + [anyspec] * ne_in,
        out_specs=tuple(out_specs) + (anyspec,) * ne,
        out_shape=tuple(out_shape) + ride.out_shape(),
        scratch_shapes=list(scratch_shapes) + ride.scratch(),
        compiler_params=_params(("arbitrary",) * len(grid)),
    )(*args, *ride.arrays)
    return tuple(res[:n_out]), tuple(res[n_out:])


def _gather_two_level(arrays, *, name):
    n = len(arrays)

    def body(*refs):
        ins, outs = refs[:n], refs[n:2 * n]
        send_sems, recv_sems, loc_sems = refs[2 * n:]
        x, y, c = lax.axis_index("x"), lax.axis_index("y"), lax.axis_index("c")
        sibling = (x, y, 1 - c)
        chips = [(1 - x, y), (x, 1 - y), (1 - x, 1 - y)]
        idx = lambda px, py, pc: 4 * px + 2 * py + pc
        me = idx(x, y, c)

        def copy(a, k, block, to, src=None):
            return pltpu.make_async_remote_copy(
                src_ref=outs[a].at[block] if src is None else src, dst_ref=outs[a].at[block],
                send_sem=send_sems.at[a, k], recv_sem=recv_sems.at[a, k], device_id=to, device_id_type=MESH_ID)

        local = [pltpu.make_async_copy(ins[a], outs[a].at[me], loc_sems.at[a]) for a in range(n)]
        sent = []
        for a in range(n):
            sent.append(copy(a, 0, me, sibling, src=ins[a]))
            sent += [copy(a, 1 + j, me, (*chip, c), src=ins[a]) for j, chip in enumerate(chips)]
        for cp in local + sent:
            cp.start()
        for j, chip in enumerate(chips):
            for a in range(n):
                copy(a, 1 + j, idx(*chip, c), sibling).wait_recv()
                passed = copy(a, 4 + j, idx(*chip, c), sibling)
                passed.start()
                sent.append(passed)
        for a in range(n):
            copy(a, 0, idx(x, y, 1 - c), sibling).wait_recv()
            for j, chip in enumerate(chips):
                copy(a, 4 + j, idx(*chip, 1 - c), sibling).wait_recv()
        for cp in sent:
            cp.wait_send()
        for cp in local:
            cp.wait()

    anyspec = pl.BlockSpec(memory_space=pl.ANY)
    return pl.pallas_call(
        body, name=name,
        out_shape=tuple(jax.ShapeDtypeStruct((NDEV,) + a.shape, a.dtype) for a in arrays),
        in_specs=[anyspec] * n, out_specs=(anyspec,) * n,
        scratch_shapes=[pltpu.SemaphoreType.DMA((n, NDEV - 1)), pltpu.SemaphoreType.DMA((n, NDEV - 1)),
                        pltpu.SemaphoreType.DMA((n,))],
    )(*arrays)


def _tri_consts():
    r = lax.broadcasted_iota(jnp.int32, (GC, GC), 0)
    c = lax.broadcasted_iota(jnp.int32, (GC, GC), 1)
    return (r >= c).astype(BF), (r <= c).astype(BF)


def _local_step(x, positions, target, w, hooks=None):
    g = {}

    def run(host, fn, *a, **kw):
        h = None if hooks is None else hooks.get(host)
        if h is None:
            return fn(*a, **kw)
        out, received = fn(*a, ride=_Exchange(h[0](w, g)), **kw)
        h[1](received, w, g)
        return out

    nseq, S, _ = x.shape
    T = nseq * S
    tm = min(TOKEN_TM, S)
    tq = min(FLASH_TQ, S)
    x2 = x.reshape(T, D)
    pos = positions.reshape(T, 1)
    half = ROPE // 2
    inv = THETA ** (-jnp.arange(half, dtype=F32) / half)
    invf = jnp.concatenate([inv, inv, jnp.zeros((64,), F32)]).reshape(1, 128)
    ltri, utri = _tri_consts()

    pt, xb = _matmul(x2, w["w_tt"], "nt", name="proj_t", tm=1024, tn=1024, tk=1024, emit_a=True)
    pg = run("proj_g", _matmul, xb, w["w_gt"], "nt", name="proj_g", tm=1024, tn=640, tk=1024)
    pm = _matmul(xb, w["w_mt"], "nt", name="proj_m", tm=1024, tn=768, tk=1024)
    o, zg, states = _gla_fwd(pg, w["wg"], w["bg"], w["gn"], ltri, nseq=nseq, S=S, tm=tm)
    qc, kc, v = _mla_prep_fwd(pm, pos, invf, w["gq"], w["gkv"], w["wuq"], w["wukv"], tm=tm)
    attn, lse = run("flash_fwd", _flash_fwd, qc, kc, v, nseq=nseq, S=S, tq=tq)
    yg, ym, mix, pre1, h1, h1b = _post_attn_fwd(zg, attn, pt, x2, w["wgo"], w["wmo"], w["wout"],
                                                w["g1"], w["b1"], tm=tm)
    ug, uv, ucg, ucv, f_in = _ffn_up_fwd(h1b, w["wug"], w["wuv"], w["cw"], w["cb"], S=S, tm=tm, tn=FFN_TN)
    dpre2b, dh1, loss8, dg2, db2 = _down_ln2_loss(f_in, w["wd"], h1, target.reshape(T, D), w["g2"], w["b2"],
                                                  tm=min(2 * tm, S))

    dug, duv, dcg, dcv = _ffn_bwd(dpre2b, w["wd"], ug, uv, ucg, ucv, w["cw"], S=S, tm=tm, tn=FFN_TN)
    g["g2"], g["b2"], g["loss"] = dg2, db2, loss8[0:1, 0:1]
    g["cw"] = jnp.concatenate([dcg[0:3], dcv[0:3]], axis=1)
    g["cb"] = jnp.concatenate([dcg[3:4], dcv[3:4]], axis=1)
    g["wd"] = _matmul(f_in, dpre2b, "tn", name="dw_down", out_dtype=BF, tm=1408, tn=1024, tk=1024)
    g["wugt"] = _matmul(dug, h1b, "tn", name="dw_up_g", out_dtype=BF, tm=1408, tn=1024, tk=1024)
    g["wuvt"] = _matmul(duv, h1b, "tn", name="dw_up_v", out_dtype=BF, tm=1408, tn=1024, tk=1024)
    dh1 = _matmul(dug, w["wugt"], "nn", name="dh1_g", c_in=dh1, tm=1024, tn=1024, tk=1408)
    dh1 = _matmul(duv, w["wuvt"], "nn", name="dh1_v", c_in=dh1, tm=1024, tn=1024, tk=1408)
    dx, dpre1b, dpt, dygb, dymb, dzg, dattn, dg1, db1 = _post_attn_bwd(
        dh1, pre1, pt, yg, ym, w["wgo"], w["wmo"], w["wout"], w["g1"], tm=tm)
    g["g1"], g["b1"] = dg1, db1
    g["wout"] = _matmul(mix, dpre1b, "tn", name="dw_out", out_dtype=BF, tm=1024, tn=1024, tk=1024)
    g["wgo"] = _matmul(zg, dygb, "tn", name="dw_gla_o", out_dtype=BF, tm=1024, tn=1024, tk=1024)
    g["wmo"] = _matmul(attn, dymb, "tn", name="dw_mla_o", out_dtype=BF, tm=1024, tn=1024, tk=1024)
    dqc, dkc, dv = run("flash_bwd", _flash_bwd, qc, kc, v, attn, dattn, lse, nseq=nseq, S=S, tq=tq)
    dpm, g["wuq"], g["wukv"], g["gq"], g["gkv"] = _mla_prep_bwd(
        pm, pos, invf, w["gq"], w["gkv"], w["wuq"], w["wukv"], dqc, dkc, dv, tm=tm)
    g["w_mt"] = _matmul(dpm, xb, "tn", name="dw_in_m", out_dtype=BF, tm=768, tn=1024, tk=1024)
    g["w_tt"] = _matmul(dpt, xb, "tn", name="dw_in_t", out_dtype=BF, tm=1024, tn=1024, tk=1024)
    dpg, g["wg"], g["bg"], g["gn"] = run("gla_bwd", _gla_bwd, pg, w["wg"], w["bg"], w["gn"], ltri, utri, o, states,
                                         dzg, nseq=nseq, S=S, tm=tm)
    g["w_gt"] = _matmul(dpg, xb, "tn", name="dw_in_g", out_dtype=BF, tm=640, tn=1024, tk=1024)
    dx = run("dx", _matmul_sum, dx, [(dpg, w["w_gt"], 640), (dpm, w["w_mt"], 768)], name="dx_gm")
    dx = _matmul_sum(dx, [(dpt, w["w_tt"], 1024)], name="dx_t")
    return loss8[0, 0], dx.reshape(nseq, S, D), g


_IN_SPLITS = (512, 512, 1024, 16, 1024, 384, 256, 64, 1024, 1024)


def _w_in_to_groups(wt):
    offs = [0]
    for s in _IN_SPLITS:
        offs.append(offs[-1] + s)
    q, k, v, r, og, cq, ckv, kr, ga, gb = [wt[offs[i]:offs[i + 1]] for i in range(10)]
    z = lambda n: jnp.zeros((n, wt.shape[1]), wt.dtype)
    return (jnp.concatenate([q, k, v, og, r, z(112)], axis=0),
            jnp.concatenate([cq, kr, z(64), ckv], axis=0),
            jnp.concatenate([ga, gb], axis=0))


W_IN_BLOCK = sum(_IN_SPLITS) // NDEV
_KV_LATENT_ROW = sum(_IN_SPLITS[:6])
_W_IN_LO = 5
_W_IN_SPLIT = _W_IN_LO * W_IN_BLOCK - _KV_LATENT_ROW


def _w_in_rows_lo(g_g, g_m):
    q, k, v, og, r = g_g[0:512], g_g[512:1024], g_g[1024:2048], g_g[2048:3072], g_g[3072:3088]
    return jnp.concatenate([q, k, v, r, og, g_m[0:384], g_m[512:768]], axis=0)[:_W_IN_LO * W_IN_BLOCK]


def _w_in_rows_hi(g_m, g_t):
    return jnp.concatenate([g_m[512:768], g_m[384:448], g_t], axis=0)[_W_IN_SPLIT:]


def _uq_to_kernel(wuq):
    w3 = wuq.reshape(MQR, MH, NOPE + ROPE)
    rope = jnp.concatenate([w3[:, :, NOPE:], jnp.zeros((MQR, MH, 64), wuq.dtype)], axis=2)
    return jnp.concatenate([w3[:, :, :NOPE].reshape(MQR, MH * 128), rope.reshape(MQR, MH * 128)], axis=1)


def _uq_from_kernel(g):
    nope = g[:, :1024].reshape(MQR, MH, 128)
    rope = g[:, 1024:].reshape(MQR, MH, 128)[:, :, :ROPE]
    return jnp.concatenate([nope, rope], axis=2)


def _ukv_to_kernel(wukv):
    w3 = wukv.reshape(MKR, MH, NOPE + MV)
    return jnp.concatenate([w3[:, :, :NOPE].reshape(MKR, MH * 128), w3[:, :, NOPE:].reshape(MKR, MH * 128)], axis=1)


def _ukv_from_kernel(g):
    return jnp.concatenate([g[:, :1024].reshape(MKR, MH, 128), g[:, 1024:].reshape(MKR, MH, 128)], axis=2)


def _cols_gathered(a):
    return a.transpose(1, 0, 2).reshape(a.shape[1], NDEV * a.shape[2])


def _cols_scattered(a):
    R = a.shape[0]
    return a.reshape(R, NDEV, a.shape[1] // NDEV).transpose(1, 0, 2)


_SMALL = (("gla_b_gate", 512), ("gla_norm_g", 256), ("mla_q_norm_g", 384), ("mla_kv_norm_g", 256),
          ("ln1_g", 1024), ("ln1_b", 1024), ("conv_b", 5632), ("ln2_g", 1024), ("ln2_b", 1024))
_SMALL_ROWS = 88
_SMALL_USED = sum(sz for _, sz in _SMALL)


def _pack_small(d):
    flat = jnp.concatenate([d[n].reshape(-1) for n, _ in _SMALL] + ([d['loss'].reshape(-1)] if 'loss' in d else []))
    return jnp.pad(flat, (0, _SMALL_ROWS * 128 - flat.shape[0])).reshape(_SMALL_ROWS, 128)


def _unpack_small(a):
    flat = a.reshape(-1)
    out, off = {}, 0
    for n, sz in _SMALL:
        out[n] = flat[off:off + sz].reshape(1, sz)
        off += sz
    return out


_NAMES = ['w_in', 'gla_w_gate_up', 'gla_b_gate', 'gla_norm_g', 'w_gla_o', 'mla_q_norm_g', 'mla_w_uq',
          'mla_kv_norm_g', 'mla_w_ukv', 'w_mla_o', 'w_out', 'ln1_g', 'ln1_b', 'w_up', 'conv_w', 'conv_b',
          'w_down', 'ln2_g', 'ln2_b']
_SHARDED = ['w_in', 'w_up', 'w_down', 'w_gla_o', 'w_mla_o', 'w_out', 'mla_w_uq', 'mla_w_ukv', 'gla_w_gate_up',
            'conv_w']


def kernel(x, positions, w_in, gla_w_gate_up, gla_b_gate, gla_norm_g, w_gla_o, mla_q_norm_g, mla_w_uq, mla_kv_norm_g, mla_w_ukv, w_mla_o, w_out, ln1_g, ln1_b, w_up, conv_w, conv_b, w_down, ln2_g, ln2_b, loss_target, m_w_in, m_gla_w_gate_up, m_gla_b_gate, m_gla_norm_g, m_w_gla_o, m_mla_q_norm_g, m_mla_w_uq, m_mla_kv_norm_g, m_mla_w_ukv, m_w_mla_o, m_w_out, m_ln1_g, m_ln1_b, m_w_up, m_conv_w, m_conv_b, m_w_down, m_ln2_g, m_ln2_b, v_w_in, v_gla_w_gate_up, v_gla_b_gate, v_gla_norm_g, v_w_gla_o, v_mla_q_norm_g, v_mla_w_uq, v_mla_kv_norm_g, v_mla_w_ukv, v_w_mla_o, v_w_out, v_ln1_g, v_ln1_b, v_w_up, v_conv_w, v_conv_b, v_w_down, v_ln2_g, v_ln2_b):
    W = dict(w_in=w_in, gla_w_gate_up=gla_w_gate_up, gla_b_gate=gla_b_gate, gla_norm_g=gla_norm_g, w_gla_o=w_gla_o, mla_q_norm_g=mla_q_norm_g, mla_w_uq=mla_w_uq, mla_kv_norm_g=mla_kv_norm_g, mla_w_ukv=mla_w_ukv, w_mla_o=w_mla_o, w_out=w_out, ln1_g=ln1_g, ln1_b=ln1_b, w_up=w_up, conv_w=conv_w, conv_b=conv_b, w_down=w_down, ln2_g=ln2_g, ln2_b=ln2_b)
    M = dict(w_in=m_w_in, gla_w_gate_up=m_gla_w_gate_up, gla_b_gate=m_gla_b_gate, gla_norm_g=m_gla_norm_g, w_gla_o=m_w_gla_o, mla_q_norm_g=m_mla_q_norm_g, mla_w_uq=m_mla_w_uq, mla_kv_norm_g=m_mla_kv_norm_g, mla_w_ukv=m_mla_w_ukv, w_mla_o=m_w_mla_o, w_out=m_w_out, ln1_g=m_ln1_g, ln1_b=m_ln1_b, w_up=m_w_up, conv_w=m_conv_w, conv_b=m_conv_b, w_down=m_w_down, ln2_g=m_ln2_g, ln2_b=m_ln2_b)
    V = dict(w_in=v_w_in, gla_w_gate_up=v_gla_w_gate_up, gla_b_gate=v_gla_b_gate, gla_norm_g=v_gla_norm_g, w_gla_o=v_w_gla_o, mla_q_norm_g=v_mla_q_norm_g, mla_w_uq=v_mla_w_uq, mla_kv_norm_g=v_mla_kv_norm_g, mla_w_ukv=v_mla_w_ukv, w_mla_o=v_w_mla_o, w_out=v_w_out, ln1_g=v_ln1_g, ln1_b=v_ln1_b, w_up=v_w_up, conv_w=v_conv_w, conv_b=v_conv_b, w_down=v_w_down, ln2_g=v_ln2_g, ln2_b=v_ln2_b)

    tshard = lambda d, n: d[n][0].T
    shard = lambda n: (W[n][0].astype(BF), False)
    first = ['w_in', 'mla_w_uq', 'mla_w_ukv', 'gla_w_gate_up']
    G = dict(zip(first, _gather_two_level(
        [tshard(W, 'w_in').astype(BF)] + [shard(n)[0] for n in first[1:]], name="gather_w0")))
    w_gt, w_mt, w_tt = _w_in_to_groups(G['w_in'].reshape(NDEV * W_IN_BLOCK, D))
    kw = dict(
        w_gt=w_gt, w_mt=w_mt, w_tt=w_tt,
        wg=jnp.pad(_cols_gathered(G['gla_w_gate_up']), ((0, 128 - GR), (0, 0))), bg=W['gla_b_gate'],
        gn=W['gla_norm_g'], gq=W['mla_q_norm_g'], gkv=W['mla_kv_norm_g'],
        wuq=_uq_to_kernel(_cols_gathered(G['mla_w_uq'])), wukv=_ukv_to_kernel(_cols_gathered(G['mla_w_ukv'])),
        g1=W['ln1_g'], b1=W['ln1_b'], g2=W['ln2_g'], b2=W['ln2_b'], cb=W['conv_b'],
    )
    received = {}

    def got_out_proj(ex, w, g):
        w.update(wgo=ex[0].reshape(D, D), wmo=ex[1].reshape(D, D), wout=ex[2].reshape(D, D))

    def got_ffn(ex, w, g):
        w_upt = ex[0].reshape(2 * DFF, D)
        w.update(wugt=w_upt[:DFF], wuvt=w_upt[DFF:], wug=w_upt[:DFF].T, wuv=w_upt[DFF:].T,
                 wd=ex[1].reshape(DFF, D), cw=_cols_gathered(ex[2]))

    slab = lambda a, lo=0: ([(a.astype(BF), lo)], True)
    rows = lambda a, n=NDEV: a.reshape(n, a.shape[0] // n, a.shape[1])

    def keep(names):
        return lambda ex, w, g: received.update(zip(names, ex))

    def small_grads(g):
        return _pack_small(dict(gla_b_gate=g['bg'], gla_norm_g=g['gn'], mla_q_norm_g=g['gq'], mla_kv_norm_g=g['gkv'],
                                ln1_g=g['g1'], ln1_b=g['b1'], conv_b=g['cb'], ln2_g=g['g2'], ln2_b=g['b2'],
                                loss=g['loss']))

    hooks = {
        "proj_g": (lambda w, g: [shard('w_gla_o'), shard('w_mla_o'), shard('w_out')], got_out_proj),
        "flash_fwd": (lambda w, g: [(tshard(W, 'w_up').astype(BF), False), shard('w_down'), (W['conv_w'][0], False)],
                      got_ffn),
        "flash_bwd": (lambda w, g: [slab(rows(g['wd'])),
                                    ([(rows(g['wugt'], 4), 0), (rows(g['wuvt'], 4), 4)], True),
                                    slab(rows(g['wout'])), slab(rows(g['wgo'])), slab(rows(g['wmo']))],
                      keep(['w_down', 'w_up', 'w_out', 'w_gla_o', 'w_mla_o'])),
        "gla_bwd": (lambda w, g: [slab(_uq_from_kernel(g['wuq']).transpose(1, 0, 2)),
                                  slab(_ukv_from_kernel(g['wukv']).transpose(1, 0, 2)),
                                  slab(rows(_w_in_rows_hi(g['w_mt'], g['w_tt']), NDEV - _W_IN_LO), _W_IN_LO)],
                    keep(['mla_w_uq', 'mla_w_ukv', 'w_in_hi'])),
        "dx": (lambda w, g: [slab(rows(_w_in_rows_lo(g['w_gt'], g['w_mt']), _W_IN_LO)),
                             ([(_cols_scattered(g['wg'][:GR]), 0)], True), ([(_cols_scattered(g['cw']), 0)], True),
                             (small_grads(g), False)],
               keep(['w_in_lo', 'gla_w_gate_up', 'conv_w', 'small'])),
    }

    _, grad_x, _ = _local_step(x, positions, loss_target, kw, hooks)

    grads, deltas, new_m, new_v = {}, {}, {}, {}
    small_parts = received['small']
    loss = jnp.sum(small_parts.reshape(NDEV, -1)[:, _SMALL_USED])
    me = 4 * lax.axis_index("x") + 2 * lax.axis_index("y") + lax.axis_index("c")
    received['w_in'] = jnp.where(me >= _W_IN_LO, received['w_in_hi'], received['w_in_lo'])
    for n in _SHARDED:
        shp = W[n].shape
        if n in ('w_in', 'w_up'):
            out = _adamw(received[n], tshard(W, n), tshard(M, n), tshard(V, n), name="adamw_" + n)
            grads[n], deltas[n], new_m[n], new_v[n] = [t.T.reshape(shp) for t in out]
            continue
        out = _adamw(received[n], W[n][0], M[n][0], V[n][0], name="adamw_" + n)
        grads[n], deltas[n], new_m[n], new_v[n] = [t.reshape(shp) for t in out]
    out = _adamw(small_parts, _pack_small(W), _pack_small(M), _pack_small(V), name="adamw_small")
    for dst, packed in zip((grads, deltas, new_m, new_v), out):
        dst.update(_unpack_small(packed))

    return (loss, grad_x, *[grads[n] for n in _NAMES], *[deltas[n] for n in _NAMES],
            *[new_m[n] for n in _NAMES], *[new_v[n] for n in _NAMES])
```

```python
import functools

import jax
import jax.numpy as jnp
from jax import lax
from jax.experimental import pallas as pl
from jax.experimental.pallas import tpu as pltpu

F32 = jnp.float32
BF = jnp.bfloat16

D = 1024
GH, GDK, GDV, GR, GTAU, GC = 4, 128, 256, 16, 16.0, 64
MH, MQR, MKR, NOPE, ROPE, MV = 8, 384, 256, 128, 64, 128
THETA = 10000.0
DFF = 2816
ALPHA = 2.0 ** 0.25
LN_EPS = 1e-5
RMS_EPS = 1e-6
NDEV = 8
ADAM_LR, ADAM_B1, ADAM_B2, ADAM_EPS, ADAM_WD, ADAM_STEP = 0.001, 0.9, 0.999, 1e-08, 0.01, 10

PG_W = 3200
PM_W = 768
PT_W = 2048
NEG = -1e30
MESH_ID = pl.DeviceIdType.MESH
VMEM_MB = 1024 * 1024


V7X_VMEM_LIMIT_MB = 48
TOKEN_TM = 256
FLASH_TQ = 512
FFN_TN = 1408


def _params(sem):
    return pltpu.CompilerParams(dimension_semantics=sem, vmem_limit_bytes=V7X_VMEM_LIMIT_MB * VMEM_MB)


def _dot(a, b):
    return lax.dot_general(a, b, (((1,), (0,)), ((), ())), preferred_element_type=F32)


def _dot_nt(a, b):
    return lax.dot_general(a, b, (((1,), (1,)), ((), ())), preferred_element_type=F32)


def _dot_tn(a, b):
    return lax.dot_general(a, b, (((0,), (0,)), ((), ())), preferred_element_type=F32)


def _iota(shape, dim):
    return lax.broadcasted_iota(jnp.int32, shape, dim)


FLASH_HP = 2
FLASH_HP_FWD = 4
QK_SCALE = (NOPE + ROPE) ** -0.5
LOG2E = 1.4426950408889634
QK_SCALE_LOG2 = QK_SCALE * LOG2E


def _sigmoid(x):
    return 0.5 * jnp.tanh(0.5 * x) + 0.5


def _tri_mm(tri_bf, x):
    hi = x.astype(BF)
    r1 = x - hi.astype(F32)
    mid = r1.astype(BF)
    lo = (r1 - mid.astype(F32)).astype(BF)
    return _dot(tri_bf, hi) + _dot(tri_bf, mid) + _dot(tri_bf, lo)


def _matmul(a, b, mode, *, name, c_in=None, out_dtype=F32, tm=512, tn=512, tk=512, ride=None, emit_a=False):
    if mode == "nn":
        (M, K), (_, N) = a.shape, b.shape
    elif mode == "nt":
        (M, K), (N, _) = a.shape, b.shape
    else:
        (K, M), (_, N) = a.shape, b.shape
    tm, tn, tk = min(tm, M), min(tn, N), min(tk, K)
    assert M % tm == 0 and N % tn == 0 and K % tk == 0, (name, M, N, K, tm, tn, tk)
    nk = K // tk
    assert not emit_a or (nk == 1 and mode != "tn" and c_in is None and ride is None)
    dot = {"nn": _dot, "nt": _dot_nt, "tn": _dot_tn}[mode]

    def body(*refs):
        if emit_a:
            a_ref, b_ref, o_ref, xa_ref, acc_ref = refs
        elif c_in is None:
            a_ref, b_ref, o_ref, acc_ref = refs
        else:
            a_ref, b_ref, c_ref, o_ref, acc_ref = refs
        k = pl.program_id(2)

        @pl.when(k == 0)
        def _():
            if c_in is None:
                acc_ref[...] = jnp.zeros_like(acc_ref)
            else:
                acc_ref[...] = c_ref[...].astype(F32)

        if emit_a:
            @pl.when(pl.program_id(1) == 0)
            def _():
                xa_ref[...] = a_ref[...].astype(BF)

        acc_ref[...] += dot(a_ref[...].astype(BF), b_ref[...].astype(BF))

        @pl.when(k == nk - 1)
        def _():
            o_ref[...] = acc_ref[...].astype(out_dtype)

    if mode == "tn":
        a_spec = pl.BlockSpec((tk, tm), lambda i, j, k: (k, i))
    else:
        a_spec = pl.BlockSpec((tm, tk), lambda i, j, k: (i, k))
    if mode == "nt":
        b_spec = pl.BlockSpec((tn, tk), lambda i, j, k: (j, k))
    else:
        b_spec = pl.BlockSpec((tk, tn), lambda i, j, k: (k, j))
    in_specs = [a_spec, b_spec]
    args = [a, b]
    if c_in is not None:
        in_specs.append(pl.BlockSpec((tm, tn), lambda i, j, k: (i, j)))
        args.append(c_in)
    out_shape = (jax.ShapeDtypeStruct((M, N), out_dtype),)
    out_specs = (pl.BlockSpec((tm, tn), lambda i, j, k: (i, j)),)
    if emit_a:
        out_shape += (jax.ShapeDtypeStruct((M, K), BF),)
        out_specs += (pl.BlockSpec((tm, tk), lambda i, j, k: (i, k)),)
    res = _call(
        body, name=name, out_shape=out_shape, grid=(M // tm, N // tn, nk), in_specs=in_specs, out_specs=out_specs,
        scratch_shapes=[pltpu.VMEM((tm, tn), F32)],
        sem=("parallel", "arbitrary", "arbitrary"), args=args, ride=ride)
    if emit_a:
        return res[0], res[1]
    return res[0] if ride is None else (res[0][0], res[1])


def _matmul_sum(c_in, parts, *, name, tm=1024, ride=None):
    M, N = c_in.shape
    tm = min(tm, M)
    n_p = len(parts)
    counts = [a.shape[1] // tk for a, _, tk in parts]
    starts = [sum(counts[:p]) for p in range(n_p)]
    nk = sum(counts)

    def body(*refs):
        a_refs, w_refs = refs[:n_p], refs[n_p:2 * n_p]
        c_ref, o_ref, acc_ref = refs[2 * n_p:]
        k = pl.program_id(1)

        @pl.when(k == 0)
        def _():
            acc_ref[...] = c_ref[...]

        for p in range(n_p):
            @pl.when(jnp.logical_and(k >= starts[p], k < starts[p] + counts[p]))
            def _(p=p):
                acc_ref[...] += _dot(a_refs[p][...].astype(BF), w_refs[p][...].astype(BF))

        @pl.when(k == nk - 1)
        def _():
            o_ref[...] = acc_ref[...]

    def kidx(p):
        return lambda k: jnp.clip(k - starts[p], 0, counts[p] - 1)

    in_specs = [pl.BlockSpec((tm, tk), lambda i, k, f=kidx(p): (i, f(k))) for p, (_, _, tk) in enumerate(parts)]
    in_specs += [pl.BlockSpec((tk, N), lambda i, k, f=kidx(p): (f(k), 0)) for p, (_, _, tk) in enumerate(parts)]
    in_specs.append(pl.BlockSpec((tm, N), lambda i, k: (i, 0)))
    res = _call(
        body, name=name, out_shape=(jax.ShapeDtypeStruct((M, N), F32),), grid=(M // tm, nk),
        in_specs=in_specs, out_specs=(pl.BlockSpec((tm, N), lambda i, k: (i, 0)),),
        scratch_shapes=[pltpu.VMEM((tm, N), F32)], sem=("parallel", "arbitrary"),
        args=[a for a, _, _ in parts] + [w for _, w, _ in parts] + [c_in], ride=ride)
    return res[0] if ride is None else (res[0][0], res[1])


def _gla_gate(pg_ref, rows, wg_ref, bg_ref):
    r = pg_ref[rows, 3072:3200].astype(BF)
    logit = _dot(r, wg_ref[...]) + bg_ref[...]
    la = (jnp.minimum(logit, 0.0) - jnp.log(1.0 + jnp.exp(-jnp.abs(logit)))) * (1.0 / GTAU)
    return r, logit, la


def _gla_fwd(pg, wg, bg, gn, ltri, *, nseq, S, tm):
    T = pg.shape[0]
    nb, nc = S // tm, tm // GC
    qscale = GDK ** -0.5

    def body(pg_ref, wg_ref, bg_ref, gn_ref, l_ref, o_ref, zg_ref, st_ref, st_scr):
        @pl.when(pl.program_id(1) == 0)
        def _():
            st_scr[...] = jnp.zeros_like(st_scr)

        ltri_v = l_ref[...]
        causal = _iota((GC, GC), 0) >= _iota((GC, GC), 1)
        last_row = _iota((GC, GDK), 0) == GC - 1
        g = gn_ref[...]

        def chunk(c, carry):
            rows = pl.ds(pl.multiple_of(c * GC, GC), GC)
            _, _, la = _gla_gate(pg_ref, rows, wg_ref, bg_ref)
            b = _tri_mm(ltri_v, la)
            hs = range(GH)
            v, q_in, k_st, dec, st, a_raw, o_st, kv = [], [], [], [], [], [], [], []
            for h in hs:
                q = pg_ref[rows, h * GDK:(h + 1) * GDK]
                k = pg_ref[rows, 512 + h * GDK:512 + (h + 1) * GDK]
                v.append(pg_ref[rows, 1024 + h * GDV:1024 + (h + 1) * GDV].astype(BF))
                bh = b[:, h * GDK:(h + 1) * GDK]
                bl = jnp.sum(jnp.where(last_row, bh, 0.0), axis=0, keepdims=True)
                q_in.append((q * (qscale * jnp.exp(bh))).astype(BF))
                k_in = (k * jnp.exp(-bh)).astype(BF)
                k_st.append((k * jnp.exp(bl - bh)).astype(BF))
                dec.append(jnp.exp(bl))
                st.append(st_scr[h])
                st_ref[c, h] = st[h]
                a_raw.append(_dot_nt(q_in[h], k_in))
            for h in hs:
                o_st.append(_dot_nt(q_in[h], st[h].astype(BF)))
                kv.append(_dot_tn(v[h], k_st[h]))
            att = [jnp.where(causal, a_raw[h], 0.0).astype(BF) for h in hs]
            o = [_dot(att[h], v[h]) + o_st[h] for h in hs]
            for h in hs:
                st_scr[h] = st[h] * dec[h] + kv[h]
                og = pg_ref[rows, 2048 + h * GDV:2048 + (h + 1) * GDV]
                rstd = lax.rsqrt(jnp.mean(o[h] * o[h], axis=-1, keepdims=True) + RMS_EPS)
                o_ref[rows, h * GDV:(h + 1) * GDV] = o[h]
                zg_ref[rows, h * GDV:(h + 1) * GDV] = (o[h] * rstd * g * (og * _sigmoid(og))).astype(BF)
            return carry

        lax.fori_loop(0, nc, chunk, 0, unroll=True)

    full = lambda shp: pl.BlockSpec(shp, lambda b_, i: (0,) * len(shp))
    return pl.pallas_call(
        body, name="gla_fwd",
        out_shape=(jax.ShapeDtypeStruct((T, GH * GDV), F32),
                   jax.ShapeDtypeStruct((T, GH * GDV), BF),
                   jax.ShapeDtypeStruct((T // GC, GH, GDV, GDK), F32)),
        grid=(nseq, nb),
        in_specs=[pl.BlockSpec((tm, PG_W), lambda b_, i: (b_ * nb + i, 0)),
                  full((128, 512)), full((1, 512)), full((1, GDV)), full((GC, GC))],
        out_specs=(pl.BlockSpec((tm, GH * GDV), lambda b_, i: (b_ * nb + i, 0)),
                   pl.BlockSpec((tm, GH * GDV), lambda b_, i: (b_ * nb + i, 0)),
                   pl.BlockSpec((nc, GH, GDV, GDK), lambda b_, i: (b_ * nb + i, 0, 0, 0))),
        scratch_shapes=[pltpu.VMEM((GH, GDV, GDK), F32)],
        compiler_params=_params(("parallel", "arbitrary")),
    )(pg, wg, bg, gn, ltri)


def _gla_bwd(pg, wg, bg, gn, ltri, utri, o, states, dzg, *, nseq, S, tm, ride=None):
    T = pg.shape[0]
    nb, nc = S // tm, tm // GC
    qscale = GDK ** -0.5

    def body(pg_ref, wg_ref, bg_ref, gn_ref, l_ref, u_ref, o_ref, st_ref, dzg_ref,
             dpg_ref, dwg_ref, dbg_ref, dgn_ref, dst_scr):
        first = jnp.logical_and(pl.program_id(0) == 0, pl.program_id(1) == 0)

        @pl.when(first)
        def _():
            dwg_ref[...] = jnp.zeros_like(dwg_ref)
            dbg_ref[...] = jnp.zeros_like(dbg_ref)
            dgn_ref[...] = jnp.zeros_like(dgn_ref)

        @pl.when(pl.program_id(1) == 0)
        def _():
            dst_scr[...] = jnp.zeros_like(dst_scr)

        ltri_v = l_ref[...]
        utri_v = u_ref[...]
        causal = _iota((GC, GC), 0) >= _iota((GC, GC), 1)
        last_row = _iota((GC, GDK), 0) == GC - 1
        g = gn_ref[...]

        def chunk(cc, carry):
            c = nc - 1 - cc
            rows = pl.ds(pl.multiple_of(c * GC, GC), GC)
            r, logit, la = _gla_gate(pg_ref, rows, wg_ref, bg_ref)
            b = _tri_mm(ltri_v, la)
            hs = range(GH)
            L = lambda: [None] * GH
            vb, eb, enb, ek, dec, q_in, k_in, k_st, q_inb, k_inb, st, dst, dob = (L() for _ in range(13))
            a_raw, da_raw, dq_st, dks, dv_st, dst_new, dbs, dgn = (L() for _ in range(8))
            for h in hs:
                q = pg_ref[rows, h * GDK:(h + 1) * GDK]
                k = pg_ref[rows, 512 + h * GDK:512 + (h + 1) * GDK]
                vb[h] = pg_ref[rows, 1024 + h * GDV:1024 + (h + 1) * GDV].astype(BF)
                og = pg_ref[rows, 2048 + h * GDV:2048 + (h + 1) * GDV]
                oh = o_ref[rows, h * GDV:(h + 1) * GDV]
                dz = dzg_ref[rows, h * GDV:(h + 1) * GDV].astype(F32)
                bh = b[:, h * GDK:(h + 1) * GDK]
                bl = jnp.sum(jnp.where(last_row, bh, 0.0), axis=0, keepdims=True)
                eb[h] = qscale * jnp.exp(bh)
                enb[h] = jnp.exp(-bh)
                ek[h] = jnp.exp(bl - bh)
                dec[h] = jnp.exp(bl)
                q_in[h], k_in[h], k_st[h] = q * eb[h], k * enb[h], k * ek[h]
                q_inb[h], k_inb[h] = q_in[h].astype(BF), k_in[h].astype(BF)
                st[h] = st_ref[c, h]
                dst[h] = dst_scr[h]
                rstd = lax.rsqrt(jnp.mean(oh * oh, axis=-1, keepdims=True) + RMS_EPS)
                ohat = oh * rstd
                sg = _sigmoid(og)
                don = dz * (og * sg)
                dpg_ref[rows, 2048 + h * GDV:2048 + (h + 1) * GDV] = (
                    dz * (ohat * g) * (sg * (1.0 + og * (1.0 - sg)))).astype(BF)
                dgn[h] = jnp.sum(don * ohat, axis=0, keepdims=True)
                gd = don * g
                dob[h] = (rstd * (gd - ohat * jnp.mean(gd * ohat, axis=-1, keepdims=True))).astype(BF)
                a_raw[h] = _dot_nt(q_inb[h], k_inb[h])
                da_raw[h] = _dot_nt(dob[h], vb[h])
            dgn_ref[...] += dgn[0] + dgn[1] + dgn[2] + dgn[3]
            for h in hs:
                dstb = dst[h].astype(BF)
                dq_st[h] = _dot(dob[h], st[h].astype(BF))
                dks[h] = _dot(vb[h], dstb)
                dv_st[h] = _dot_nt(k_st[h].astype(BF), dstb)
                dst_new[h] = _dot_tn(dob[h], q_inb[h])
            att = [jnp.where(causal, a_raw[h], 0.0).astype(BF) for h in hs]
            da = [jnp.where(causal, da_raw[h], 0.0).astype(BF) for h in hs]
            dqi = [_dot(da[h], k_inb[h]) + dq_st[h] for h in hs]
            dki = [_dot_tn(da[h], q_inb[h]) for h in hs]
            dv = [_dot_tn(att[h], dob[h]) + dv_st[h] for h in hs]
            for h in hs:
                dd = jnp.sum(dst[h] * st[h], axis=0, keepdims=True)
                dst_scr[h] = dst[h] * dec[h] + dst_new[h]
                kk = dks[h] * k_st[h]
                dbl = jnp.sum(kk, axis=0, keepdims=True) + dd * dec[h]
                db = dqi[h] * q_in[h] - dki[h] * k_in[h] - kk
                dbs[h] = db + jnp.where(last_row, dbl, 0.0)
                dpg_ref[rows, h * GDK:(h + 1) * GDK] = (dqi[h] * eb[h]).astype(BF)
                dpg_ref[rows, 512 + h * GDK:512 + (h + 1) * GDK] = (dki[h] * enb[h] + dks[h] * ek[h]).astype(BF)
                dpg_ref[rows, 1024 + h * GDV:1024 + (h + 1) * GDV] = dv[h].astype(BF)
            dla = _tri_mm(utri_v, jnp.concatenate(dbs, axis=1))
            dlogit = dla * (1.0 / GTAU) * _sigmoid(-logit)
            dlb = dlogit.astype(BF)
            dpg_ref[rows, 3072:3200] = _dot_nt(dlb, wg_ref[...]).astype(BF)
            dwg_ref[...] += _dot_tn(r, dlb)
            dbg_ref[...] += jnp.sum(dlogit, axis=0, keepdims=True)
            return carry

        lax.fori_loop(0, nc, chunk, 0, unroll=True)

    full = lambda shp: pl.BlockSpec(shp, lambda b_, i: (0,) * len(shp))
    rev = lambda b_, i: (b_ * nb + nb - 1 - i, 0)
    return _call(
        body, name="gla_bwd", ride=ride, sem=("arbitrary", "arbitrary"),
        args=(pg, wg, bg, gn, ltri, utri, o, states, dzg),
        out_shape=(jax.ShapeDtypeStruct((T, PG_W), BF),
                   jax.ShapeDtypeStruct((128, 512), F32),
                   jax.ShapeDtypeStruct((1, 512), F32),
                   jax.ShapeDtypeStruct((1, GDV), F32)),
        grid=(nseq, nb),
        in_specs=[pl.BlockSpec((tm, PG_W), rev),
                  full((128, 512)), full((1, 512)), full((1, GDV)), full((GC, GC)), full((GC, GC)),
                  pl.BlockSpec((tm, GH * GDV), rev),
                  pl.BlockSpec((nc, GH, GDV, GDK), lambda b_, i: (b_ * nb + nb - 1 - i, 0, 0, 0)),
                  pl.BlockSpec((tm, GH * GDV), rev)],
        out_specs=(pl.BlockSpec((tm, PG_W), rev), full((128, 512)), full((1, 512)), full((1, GDV))),
        scratch_shapes=[pltpu.VMEM((GH, GDV, GDK), F32)])


def _rope_tables(pos, invf):
    ang = pos.astype(F32) * invf
    lane = _iota(ang.shape, 1)
    sin = jnp.sin(ang)
    ssin = jnp.where(lane < 32, -sin, jnp.where(lane < 64, sin, 0.0))
    return jnp.cos(ang), ssin, lane


def _rope(x, cos, ssin, lane, sign):
    rot = jnp.where(lane < 32, pltpu.roll(x, 96, 1), pltpu.roll(x, 32, 1))
    return x * cos + sign * (rot * ssin)


def _rms_fwd(x, g):
    rstd = lax.rsqrt(jnp.mean(x * x, axis=-1, keepdims=True) + RMS_EPS)
    return x * rstd * g, x * rstd, rstd


def _rms_bwd(dy, xhat, rstd, g):
    gd = dy * g
    return rstd * (gd - xhat * jnp.mean(gd * xhat, axis=-1, keepdims=True)), jnp.sum(dy * xhat, axis=0, keepdims=True)


def _mla_prep_fwd(pm, pos, invf, gq, gkv, wuq, wukv, *, tm):
    T = pm.shape[0]

    def body(pm_ref, pos_ref, invf_ref, gq_ref, gkv_ref, wuq_ref, wukv_ref, qc_ref, kc_ref, v_ref):
        cos, ssin, lane = _rope_tables(pos_ref[...], invf_ref[...])
        cq, _, _ = _rms_fwd(pm_ref[:, 0:MQR], gq_ref[...])
        ckv, _, _ = _rms_fwd(pm_ref[:, 512:768], gkv_ref[...])
        qf = _dot(cq.astype(BF), wuq_ref[...])
        kvf = _dot(ckv.astype(BF), wukv_ref[...])
        kr = _rope(pm_ref[:, 384:512], cos, ssin, lane, 1.0).astype(BF)
        for h in range(MH):
            qc_ref[:, 256 * h:256 * h + 128] = (QK_SCALE_LOG2 * qf[:, 128 * h:128 * h + 128]).astype(BF)
            qr = qf[:, 1024 + 128 * h:1024 + 128 * h + 128]
            qc_ref[:, 256 * h + 128:256 * h + 256] = (QK_SCALE_LOG2 * _rope(qr, cos, ssin, lane, 1.0)).astype(BF)
            kc_ref[:, 256 * h:256 * h + 128] = kvf[:, 128 * h:128 * h + 128].astype(BF)
            kc_ref[:, 256 * h + 128:256 * h + 256] = kr
        v_ref[...] = kvf[:, 1024:2048].astype(BF)

    full = lambda shp: pl.BlockSpec(shp, lambda i: (0,) * len(shp))
    row = lambda w: pl.BlockSpec((tm, w), lambda i: (i, 0))
    return pl.pallas_call(
        body, name="mla_prep_fwd",
        out_shape=(jax.ShapeDtypeStruct((T, MH * 256), BF), jax.ShapeDtypeStruct((T, MH * 256), BF),
                   jax.ShapeDtypeStruct((T, MH * MV), BF)),
        grid=(T // tm,),
        in_specs=[row(PM_W), row(1), full((1, 128)), full((1, MQR)), full((1, MKR)),
                  full((MQR, 2048)), full((MKR, 2048))],
        out_specs=(row(MH * 256), row(MH * 256), row(MH * MV)),
        compiler_params=_params(("parallel",)),
    )(pm, pos, invf, gq, gkv, wuq, wukv)


def _mla_prep_bwd(pm, pos, invf, gq, gkv, wuq, wukv, dqc, dkc, dv, *, tm):
    T = pm.shape[0]

    def body(pm_ref, pos_ref, invf_ref, gq_ref, gkv_ref, wuq_ref, wukv_ref, dqc_ref, dkc_ref, dv_ref,
             dpm_ref, dwuq_ref, dwukv_ref, dgq_ref, dgkv_ref):
        @pl.when(pl.program_id(0) == 0)
        def _():
            dwuq_ref[...] = jnp.zeros_like(dwuq_ref)
            dwukv_ref[...] = jnp.zeros_like(dwukv_ref)
            dgq_ref[...] = jnp.zeros_like(dgq_ref)
            dgkv_ref[...] = jnp.zeros_like(dgkv_ref)

        cos, ssin, lane = _rope_tables(pos_ref[...], invf_ref[...])
        cq, cqh, cq_rstd = _rms_fwd(pm_ref[:, 0:MQR], gq_ref[...])
        ckv, ckvh, ckv_rstd = _rms_fwd(pm_ref[:, 512:768], gkv_ref[...])
        dqn, dqr, dkn = [], [], []
        dkr = jnp.zeros((tm, 128), F32)
        for h in range(MH):
            dqn.append(dqc_ref[:, 256 * h:256 * h + 128].astype(BF))
            dqr.append(_rope(dqc_ref[:, 256 * h + 128:256 * h + 256], cos, ssin, lane, -1.0).astype(BF))
            dkn.append(dkc_ref[:, 256 * h:256 * h + 128].astype(BF))
            dkr = dkr + dkc_ref[:, 256 * h + 128:256 * h + 256]
        dqf = jnp.concatenate(dqn + dqr, axis=1)
        dkvf = jnp.concatenate(dkn + [dv_ref[...].astype(BF)], axis=1)
        dwuq_ref[...] += _dot_tn(cq.astype(BF), dqf)
        dwukv_ref[...] += _dot_tn(ckv.astype(BF), dkvf)
        dcq, dgq = _rms_bwd(_dot_nt(dqf, wuq_ref[...]), cqh, cq_rstd, gq_ref[...])
        dckv, dgkv = _rms_bwd(_dot_nt(dkvf, wukv_ref[...]), ckvh, ckv_rstd, gkv_ref[...])
        dgq_ref[...] += dgq
        dgkv_ref[...] += dgkv
        dpm_ref[:, 0:MQR] = dcq.astype(BF)
        dpm_ref[:, 384:512] = _rope(dkr, cos, ssin, lane, -1.0).astype(BF)
        dpm_ref[:, 512:768] = dckv.astype(BF)

    full = lambda shp: pl.BlockSpec(shp, lambda i: (0,) * len(shp))
    row = lambda w: pl.BlockSpec((tm, w), lambda i: (i, 0))
    return pl.pallas_call(
        body, name="mla_prep_bwd",
        out_shape=(jax.ShapeDtypeStruct((T, PM_W), BF), jax.ShapeDtypeStruct((MQR, 2048), F32),
                   jax.ShapeDtypeStruct((MKR, 2048), F32), jax.ShapeDtypeStruct((1, MQR), F32),
                   jax.ShapeDtypeStruct((1, MKR), F32)),
        grid=(T // tm,),
        in_specs=[row(PM_W), row(1), full((1, 128)), full((1, MQR)), full((1, MKR)),
                  full((MQR, 2048)), full((MKR, 2048)), row(MH * 256), row(MH * 256), row(MH * MV)],
        out_specs=(row(PM_W), full((MQR, 2048)), full((MKR, 2048)), full((1, MQR)), full((1, MKR))),
        compiler_params=_params(("arbitrary",)),
    )(pm, pos, invf, gq, gkv, wuq, wukv, dqc, dkc, dv)


def _flash_fwd(qc, kc, v, *, nseq, S, tq, ride=None):
    T = qc.shape[0]
    nq = S // tq
    hp = FLASH_HP_FWD

    def body(q_ref, k_ref, v_ref, o_ref, lse_ref):
        i = pl.program_id(2)
        causal = _iota((tq, tq), 0) >= _iota((tq, tq), 1)

        def step(j, carry, masked):
            rows = pl.ds(pl.multiple_of(j * tq, tq), tq)
            hs = range(hp)
            s = [_dot_nt(q_ref[:, 256 * hh:256 * hh + 256], k_ref[rows, 256 * hh:256 * hh + 256]) for hh in hs]
            p, stats = [], []
            for hh in hs:
                m, l, _ = carry[hh]
                sh = jnp.where(causal, s[hh], NEG) if masked else s[hh]
                m_new = jnp.maximum(m, jnp.max(sh, axis=-1, keepdims=True))
                ph = jnp.exp2(sh - m_new)
                a = jnp.exp2(m - m_new)
                stats.append((m_new, a * l + jnp.sum(ph, axis=-1, keepdims=True), a))
                p.append(ph.astype(BF))
            pv = [_dot(p[hh], v_ref[rows, MV * hh:MV * hh + MV]) for hh in hs]
            return tuple((stats[hh][0], stats[hh][1], stats[hh][2] * carry[hh][2] + pv[hh]) for hh in hs)

        init = ((jnp.full((tq, 1), NEG, F32), jnp.zeros((tq, 1), F32), jnp.zeros((tq, MV), F32)),) * hp
        carry = lax.fori_loop(0, i, lambda j, c: step(j, c, False), init)
        for hh, (m, l, acc) in enumerate(step(i, carry, True)):
            o_ref[:, MV * hh:MV * hh + MV] = (acc / l).astype(BF)
            lse_ref[:, 128 * hh:128 * hh + 128] = jnp.broadcast_to(m + jnp.log2(l), (tq, 128))

    return _call(
        body, name="flash_fwd", ride=ride, sem=("parallel", "parallel", "arbitrary"), args=(qc, kc, v),
        out_shape=(jax.ShapeDtypeStruct((T, MH * MV), BF), jax.ShapeDtypeStruct((T, MH * 128), F32)),
        grid=(nseq, MH // hp, nq),
        in_specs=[pl.BlockSpec((tq, 256 * hp), lambda b_, h, i: (b_ * nq + i, h)),
                  pl.BlockSpec((S, 256 * hp), lambda b_, h, i: (b_, h)),
                  pl.BlockSpec((S, MV * hp), lambda b_, h, i: (b_, h))],
        out_specs=(pl.BlockSpec((tq, MV * hp), lambda b_, h, i: (b_ * nq + i, h)),
                   pl.BlockSpec((tq, 128 * hp), lambda b_, h, i: (b_ * nq + i, h))))


def _flash_bwd(qc, kc, v, o, do, lse, *, nseq, S, tq, ride=None):
    T = qc.shape[0]
    nq = S // tq

    def body(q_ref, k_ref, v_ref, o_ref, do_ref, lse_ref, dq_ref, dk_ref, dv_ref, dq_scr, delta_scr):
        j = pl.program_id(2)

        @pl.when(j == 0)
        def _():
            dq_scr[...] = jnp.zeros_like(dq_scr)
            for hh in range(FLASH_HP):
                od = o_ref[:, MV * hh:MV * hh + MV].astype(F32) * do_ref[:, MV * hh:MV * hh + MV].astype(F32)
                delta_scr[:, 128 * hh:128 * hh + 128] = jnp.broadcast_to(jnp.sum(od, axis=-1, keepdims=True), (S, 128))

        causal = _iota((tq, tq), 0) >= _iota((tq, tq), 1)

        def step(i, carry, masked):
            rows = pl.ds(pl.multiple_of(i * tq, tq), tq)
            hs = range(FLASH_HP)
            qs = [slice(256 * hh, 256 * hh + 256) for hh in hs]
            vs = [slice(MV * hh, MV * hh + MV) for hh in hs]
            ls = [slice(128 * hh, 128 * hh + 1) for hh in hs]
            s = [_dot_nt(q_ref[rows, qs[hh]], k_ref[:, qs[hh]]) for hh in hs]
            dp = [_dot_nt(do_ref[rows, vs[hh]], v_ref[:, vs[hh]]) for hh in hs]
            pb, ds = [], []
            for hh in hs:
                p = jnp.exp2(s[hh] - lse_ref[rows, ls[hh]])
                if masked:
                    p = jnp.where(causal, p, 0.0)
                pb.append(p.astype(BF))
                ds.append((p * (dp[hh] - delta_scr[rows, ls[hh]])).astype(BF))
            dv = [carry[hh][1] + _dot_tn(pb[hh], do_ref[rows, vs[hh]]) for hh in hs]
            dk = [carry[hh][0] + _dot_tn(ds[hh], q_ref[rows, qs[hh]]) for hh in hs]
            for hh in hs:
                dq_scr[rows, qs[hh]] += _dot(ds[hh], k_ref[:, qs[hh]])
            return tuple((dk[hh], dv[hh]) for hh in hs)

        init = ((jnp.zeros((tq, 256), F32), jnp.zeros((tq, MV), F32)),) * FLASH_HP
        carry = step(j, init, True)
        carry = lax.fori_loop(j + 1, nq, lambda i, c: step(i, c, False), carry)
        for hh, (dk, dv) in enumerate(carry):
            dk_ref[:, 256 * hh:256 * hh + 256] = dk * (1.0 / LOG2E)
            dv_ref[:, MV * hh:MV * hh + MV] = dv

        @pl.when(j == nq - 1)
        def _():
            dq_ref[...] = dq_scr[...] * QK_SCALE

    hp = FLASH_HP
    seq = lambda w: pl.BlockSpec((S, w * hp), lambda b_, h, j: (b_, h))
    blk = lambda w: pl.BlockSpec((tq, w * hp), lambda b_, h, j: (b_ * nq + j, h))
    return _call(
        body, name="flash_bwd", ride=ride, sem=("parallel", "parallel", "arbitrary"), args=(qc, kc, v, o, do, lse),
        out_shape=(jax.ShapeDtypeStruct((T, MH * 256), F32), jax.ShapeDtypeStruct((T, MH * 256), F32),
                   jax.ShapeDtypeStruct((T, MH * MV), F32)),
        grid=(nseq, MH // hp, nq),
        in_specs=[seq(256), blk(256), blk(MV), seq(MV), seq(MV), seq(128)],
        out_specs=(seq(256), blk(256), blk(MV)),
        scratch_shapes=[pltpu.VMEM((S, 256 * hp), F32), pltpu.VMEM((S, 128 * hp), F32)])


def _ln_fwd(pre, g, b):
    mu = jnp.mean(pre, axis=-1, keepdims=True)
    xc = pre - mu
    rstd = lax.rsqrt(jnp.mean(xc * xc, axis=-1, keepdims=True) + LN_EPS)
    xhat = xc * rstd
    return xhat * g + b, xhat, rstd


def _ln_bwd(dy, xhat, rstd, g):
    dxh = dy * g
    dx = rstd * (dxh - jnp.mean(dxh, axis=-1, keepdims=True) - xhat * jnp.mean(dxh * xhat, axis=-1, keepdims=True))
    return dx, jnp.sum(dy * xhat, axis=0, keepdims=True), jnp.sum(dy, axis=0, keepdims=True)


def _post_attn_fwd(zg, attn, pt, x, wgo, wmo, wout, g1, b1, *, tm):
    T = x.shape[0]

    def body(zg_ref, at_ref, pt_ref, x_ref, wgo_ref, wmo_ref, wout_ref, g_ref, b_ref,
             yg_ref, ym_ref, mix_ref, pre_ref, h_ref, hb_ref):
        yg = _dot(zg_ref[...], wgo_ref[...])
        ym = _dot(at_ref[...], wmo_ref[...])
        mix = (_sigmoid(pt_ref[:, 0:D].astype(F32)) * yg + _sigmoid(pt_ref[:, D:2 * D].astype(F32)) * ym).astype(BF)
        pre = ALPHA * x_ref[...] + _dot(mix, wout_ref[...])
        h, _, _ = _ln_fwd(pre, g_ref[...], b_ref[...])
        yg_ref[...] = yg.astype(BF)
        ym_ref[...] = ym.astype(BF)
        mix_ref[...] = mix
        pre_ref[...] = pre
        h_ref[...] = h
        hb_ref[...] = h.astype(BF)

    full = lambda shp: pl.BlockSpec(shp, lambda i: (0,) * len(shp))
    row = lambda w: pl.BlockSpec((tm, w), lambda i: (i, 0))
    sd = lambda dt: jax.ShapeDtypeStruct((T, D), dt)
    return pl.pallas_call(
        body, name="post_attn_fwd",
        out_shape=(sd(BF), sd(BF), sd(BF), sd(F32), sd(F32), sd(BF)),
        grid=(T // tm,),
        in_specs=[row(D), row(D), row(PT_W), row(D), full((D, D)), full((D, D)), full((D, D)),
                  full((1, D)), full((1, D))],
        out_specs=(row(D),) * 6,
        compiler_params=_params(("parallel",)),
    )(zg, attn, pt, x, wgo, wmo, wout, g1, b1)


def _post_attn_bwd(dh, pre, pt, yg, ym, wgo, wmo, wout, g1, *, tm):
    T = dh.shape[0]

    def body(dh_ref, pre_ref, pt_ref, yg_ref, ym_ref, wgo_ref, wmo_ref, wout_ref, g_ref,
             dx_ref, dpreb_ref, dpt_ref, dygb_ref, dymb_ref, dzg_ref, dat_ref, dg_ref, db_ref):
        @pl.when(pl.program_id(0) == 0)
        def _():
            dg_ref[...] = jnp.zeros_like(dg_ref)
            db_ref[...] = jnp.zeros_like(db_ref)

        pre = pre_ref[...]
        mu = jnp.mean(pre, axis=-1, keepdims=True)
        xc = pre - mu
        rstd = lax.rsqrt(jnp.mean(xc * xc, axis=-1, keepdims=True) + LN_EPS)
        dpre, dg, db = _ln_bwd(dh_ref[...], xc * rstd, rstd, g_ref[...])
        dg_ref[...] += dg
        db_ref[...] += db
        dx_ref[...] = ALPHA * dpre
        dpreb = dpre.astype(BF)
        dpreb_ref[...] = dpreb
        dmix = _dot_nt(dpreb, wout_ref[...])
        sa = _sigmoid(pt_ref[:, 0:D].astype(F32))
        sb = _sigmoid(pt_ref[:, D:2 * D].astype(F32))
        dpt_ref[:, 0:D] = (dmix * yg_ref[...].astype(F32) * (sa * (1.0 - sa))).astype(BF)
        dpt_ref[:, D:2 * D] = (dmix * ym_ref[...].astype(F32) * (sb * (1.0 - sb))).astype(BF)
        dyg = (dmix * sa).astype(BF)
        dym = (dmix * sb).astype(BF)
        dygb_ref[...] = dyg
        dymb_ref[...] = dym
        dzg_ref[...] = _dot_nt(dyg, wgo_ref[...]).astype(BF)
        dat_ref[...] = _dot_nt(dym, wmo_ref[...]).astype(BF)

    full = lambda shp: pl.BlockSpec(shp, lambda i: (0,) * len(shp))
    row = lambda w: pl.BlockSpec((tm, w), lambda i: (i, 0))
    sd = lambda w, dt: jax.ShapeDtypeStruct((T, w), dt)
    return pl.pallas_call(
        body, name="post_attn_bwd",
        out_shape=(sd(D, F32), sd(D, BF), sd(PT_W, BF), sd(D, BF), sd(D, BF), sd(D, BF), sd(D, BF),
                   jax.ShapeDtypeStruct((1, D), F32), jax.ShapeDtypeStruct((1, D), F32)),
        grid=(T // tm,),
        in_specs=[row(D), row(D), row(PT_W), row(D), row(D), full((D, D)), full((D, D)), full((D, D)),
                  full((1, D))],
        out_specs=(row(D), row(D), row(PT_W), row(D), row(D), row(D), row(D), full((1, D)), full((1, D))),
        compiler_params=_params(("arbitrary",)),
    )(dh, pre, pt, yg, ym, wgo, wmo, wout, g1)


def _shift_down(u, prev, k):
    r = pltpu.roll(u, k, 0)
    p = pltpu.roll(prev, k, 0)
    head = jnp.where(_iota(p.shape, 0) < k, p, r[0:8, :])
    return jnp.concatenate([head, r[8:, :]], axis=0)


def _conv3(u, prev, w_ref, b_ref):
    return (w_ref[0:1, :] * _shift_down(u, prev, 2) + w_ref[1:2, :] * _shift_down(u, prev, 1)
            + w_ref[2:3, :] * u + b_ref[...])


def _ffn_up_fwd(hb, wug, wuv, cw, cb, *, S, tm, tn):
    T = hb.shape[0]
    nj, nbs = DFF // tn, S // tm

    def body(h_ref, wg_ref, wv_ref, cwg_ref, cwv_ref, cbg_ref, cbv_ref,
             ug_ref, uv_ref, ucg_ref, ucv_ref, f_ref, pg_scr, pv_scr):
        @pl.when(pl.program_id(1) % nbs == 0)
        def _():
            pg_scr[...] = jnp.zeros_like(pg_scr)
            pv_scr[...] = jnp.zeros_like(pv_scr)

        h = h_ref[...]
        ug = _dot(h, wg_ref[...])
        uv = _dot(h, wv_ref[...])
        ucg = _conv3(ug, pg_scr[...], cwg_ref, cbg_ref)
        ucv = _conv3(uv, pv_scr[...], cwv_ref, cbv_ref)
        pg_scr[...] = ug[tm - 8:, :]
        pv_scr[...] = uv[tm - 8:, :]
        ug_ref[...] = ug.astype(BF)
        uv_ref[...] = uv.astype(BF)
        ucg_ref[...] = ucg
        ucv_ref[...] = ucv
        f_ref[...] = (ucg * _sigmoid(ucg) * ucv).astype(BF)

    tile = pl.BlockSpec((tm, tn), lambda j, i: (i, j))
    return pl.pallas_call(
        body, name="ffn_up_fwd",
        out_shape=(jax.ShapeDtypeStruct((T, DFF), BF), jax.ShapeDtypeStruct((T, DFF), BF),
                   jax.ShapeDtypeStruct((T, DFF), F32), jax.ShapeDtypeStruct((T, DFF), F32),
                   jax.ShapeDtypeStruct((T, DFF), BF)),
        grid=(nj, T // tm),
        in_specs=[pl.BlockSpec((tm, D), lambda j, i: (i, 0)),
                  pl.BlockSpec((D, tn), lambda j, i: (0, j)), pl.BlockSpec((D, tn), lambda j, i: (0, j)),
                  pl.BlockSpec((3, tn), lambda j, i: (0, j)), pl.BlockSpec((3, tn), lambda j, i: (0, j + nj)),
                  pl.BlockSpec((1, tn), lambda j, i: (0, j)), pl.BlockSpec((1, tn), lambda j, i: (0, j + nj))],
        out_specs=(tile, tile, tile, tile, tile),
        scratch_shapes=[pltpu.VMEM((8, tn), F32), pltpu.VMEM((8, tn), F32)],
        compiler_params=_params(("parallel", "arbitrary")),
    )(hb, wug, wuv, cw, cw, cb, cb)


def _ffn_bwd(dpreb, wd, ug, uv, ucg, ucv, cw, *, S, tm, tn):
    T = dpreb.shape[0]
    nj, nb, nbs = DFF // tn, T // tm, S // tm
    r_, c_ = lax.broadcasted_iota(jnp.int32, (tm, tm), 0), lax.broadcasted_iota(jnp.int32, (tm, tm), 1)
    s1, s2 = (c_ == r_ + 1).astype(BF), (c_ == r_ + 2).astype(BF)

    def body(dp_ref, wd_ref, ug_ref, uv_ref, ucg_ref, ucv_ref, cwg_ref, cwv_ref, s1_ref, s2_ref,
             dug_ref, duv_ref, dcg_ref, dcv_ref, ng_scr, nv_scr):
        ii = pl.program_id(1)
        i = nb - 1 - ii
        tail_row = _iota((8, tn), 0)

        @pl.when(ii == 0)
        def _():
            dcg_ref[...] = jnp.zeros_like(dcg_ref)
            dcv_ref[...] = jnp.zeros_like(dcv_ref)

        @pl.when(i % nbs == nbs - 1)
        def _():
            ng_scr[...] = jnp.zeros_like(ng_scr)
            nv_scr[...] = jnp.zeros_like(nv_scr)

        df = _dot_nt(dp_ref[...], wd_ref[...])
        ucg = ucg_ref[...]
        sg = _sigmoid(ucg)
        ducg = df * ucv_ref[...] * (sg * (1.0 + ucg * (1.0 - sg)))
        ducv = df * (ucg * sg)

        def finish(duc, u_ref, w, nxt_scr, du_ref, dc_ref):
            nxt = nxt_scr[...]
            db = duc.astype(BF)

            def shifted(s_ref, k):
                r = _dot(s_ref[...], db)
                tail = jnp.where(tail_row >= 8 - k, pltpu.roll(nxt, 8 - k, 0), r[tm - 8:, :])
                return jnp.concatenate([r[:tm - 8, :], tail], axis=0)

            up1 = shifted(s1_ref, 1)
            up2 = shifted(s2_ref, 2)
            du_ref[...] = (w[2:3, :] * duc + w[1:2, :] * up1 + w[0:1, :] * up2).astype(BF)
            nxt_scr[...] = duc[0:8, :]
            u = u_ref[...].astype(F32)
            for row, z in enumerate((u * up2, u * up1, u * duc, duc)):
                dc_ref[row:row + 1, :] += jnp.sum(z, axis=0, keepdims=True)

        finish(ducg, ug_ref, cwg_ref, ng_scr, dug_ref, dcg_ref)
        finish(ducv, uv_ref, cwv_ref, nv_scr, duv_ref, dcv_ref)

    tile = pl.BlockSpec((tm, tn), lambda j, ii: (nb - 1 - ii, j))
    acc = pl.BlockSpec((8, tn), lambda j, ii: (0, j))
    return pl.pallas_call(
        body, name="ffn_bwd",
        out_shape=(jax.ShapeDtypeStruct((T, DFF), BF), jax.ShapeDtypeStruct((T, DFF), BF),
                   jax.ShapeDtypeStruct((8, DFF), F32), jax.ShapeDtypeStruct((8, DFF), F32)),
        grid=(nj, nb),
        in_specs=[pl.BlockSpec((tm, D), lambda j, ii: (nb - 1 - ii, 0)),
                  pl.BlockSpec((tn, D), lambda j, ii: (j, 0)),
                  tile, tile, tile, tile,
                  pl.BlockSpec((3, tn), lambda j, ii: (0, j)), pl.BlockSpec((3, tn), lambda j, ii: (0, j + nj)),
                  pl.BlockSpec((tm, tm), lambda j, ii: (0, 0)), pl.BlockSpec((tm, tm), lambda j, ii: (0, 0))],
        out_specs=(tile, tile, acc, acc),
        scratch_shapes=[pltpu.VMEM((8, tn), F32), pltpu.VMEM((8, tn), F32)],
        compiler_params=_params(("parallel", "arbitrary")),
    )(dpreb, wd, ug, uv, ucg, ucv, cw, cw, s1, s2)


def _down_ln2_loss(f_in, wd, h, target, g2, b2, *, tm):
    T = h.shape[0]

    def body(f_ref, wd_ref, h_ref, t_ref, g_ref, b_ref, dpb_ref, dh_ref, loss_ref, dg_ref, db_ref):
        @pl.when(pl.program_id(0) == 0)
        def _():
            loss_ref[...] = jnp.zeros_like(loss_ref)
            dg_ref[...] = jnp.zeros_like(dg_ref)
            db_ref[...] = jnp.zeros_like(db_ref)

        halves = [pl.ds(s * (tm // 2), tm // 2) for s in range(2)]
        f = [_dot(f_ref[hs, :], wd_ref[...]) for hs in halves]
        for hs, fh in zip(halves, f):
            pre = ALPHA * h_ref[hs, :] + fh
            out, xhat, rstd = _ln_fwd(pre, g_ref[...], b_ref[...])
            diff = out - t_ref[hs, :]
            loss_ref[...] += 0.5 * jnp.sum(jnp.mean(diff * diff, axis=-1, keepdims=True))
            dpre, dg, db = _ln_bwd(diff * (1.0 / D), xhat, rstd, g_ref[...])
            dg_ref[...] += dg
            db_ref[...] += db
            dpb_ref[hs, :] = dpre.astype(BF)
            dh_ref[hs, :] = ALPHA * dpre

    full = lambda shp: pl.BlockSpec(shp, lambda i: (0,) * len(shp))
    row = lambda w: pl.BlockSpec((tm, w), lambda i: (i, 0))
    return pl.pallas_call(
        body, name="down_ln2_loss",
        out_shape=(jax.ShapeDtypeStruct((T, D), BF), jax.ShapeDtypeStruct((T, D), F32),
                   jax.ShapeDtypeStruct((8, 128), F32), jax.ShapeDtypeStruct((1, D), F32),
                   jax.ShapeDtypeStruct((1, D), F32)),
        grid=(T // tm,),
        in_specs=[row(DFF), full((DFF, D)), row(D), row(D), full((1, D)), full((1, D))],
        out_specs=(row(D), row(D), full((8, 128)), full((1, D)), full((1, D))),
        compiler_params=_params(("arbitrary",)),
    )(f_in, wd, h, target, g2, b2)


def _adamw(parts, w, m, v, *, name):
    n, R, C = parts.shape
    tr, tc = R, C
    for cand in range(min(R, 256), 15, -1):
        if R % cand == 0 and cand % 16 == 0:
            tr = cand
            break
    if tr == R and R * C > 65536 and C % 256 == 0:
        tc = 256
    c1 = 1.0 - ADAM_B1 ** ADAM_STEP
    c2 = 1.0 - ADAM_B2 ** ADAM_STEP

    def body(p_ref, w_ref, m_ref, v_ref, g_ref, d_ref, nm_ref, nv_ref):
        g = p_ref[0].astype(F32)
        for s in range(1, n):
            g = g + p_ref[s].astype(F32)
        nm = ADAM_B1 * m_ref[...] + (1.0 - ADAM_B1) * g
        nv = ADAM_B2 * v_ref[...] + (1.0 - ADAM_B2) * (g * g)
        g_ref[...] = g
        nm_ref[...] = nm
        nv_ref[...] = nv
        d_ref[...] = -ADAM_LR * ((nm / c1) / (jnp.sqrt(nv / c2) + ADAM_EPS) + ADAM_WD * w_ref[...])

    blk = pl.BlockSpec((tr, tc), lambda i, j: (i, j))
    sd = jax.ShapeDtypeStruct((R, C), F32)
    return pl.pallas_call(
        body, name=name,
        out_shape=(sd, sd, sd, sd),
        grid=(R // tr, C // tc),
        in_specs=[pl.BlockSpec((n, tr, tc), lambda i, j: (0, i, j)), blk, blk, blk],
        out_specs=(blk, blk, blk, blk),
        compiler_params=_params(("parallel", "parallel")),
    )(parts, w, m, v)


class _Exchange:
    def __init__(self, items):
        self.items = [(src if sc else [(src, 0)], sc) for src, sc in items]
        self.arrays = [arr for srcs, _ in self.items for arr, _ in srcs]
        self.n = len(self.items)
        self.n_in = len(self.arrays)

    def out_shape(self):
        return tuple(jax.ShapeDtypeStruct((NDEV,) + (srcs[0][0].shape[1:] if sc else srcs[0][0].shape),
                                          srcs[0][0].dtype) for srcs, sc in self.items)

    def scratch(self):
        return [pltpu.SemaphoreType.DMA((self.n, NDEV - 1)), pltpu.SemaphoreType.DMA((self.n, NDEV - 1)),
                pltpu.SemaphoreType.DMA((self.n,))]

    def _emit(self, ins, outs, sems, phase):
        send_sems, recv_sems, loc_sems = sems
        x, y, c = lax.axis_index("x"), lax.axis_index("y"), lax.axis_index("c")
        me = 4 * x + 2 * y + c
        flip = lambda p, d: 1 - p if d else p

        def inside(p, lo, n):
            return None if (lo, n) == (0, NDEV) else jnp.logical_and(p >= lo, p < lo + n)

        def when(cond, fn):
            if cond is None:
                fn()
            else:
                pl.when(cond)(fn)

        pos = 0
        for a, (srcs, sc) in enumerate(self.items):
            refs = ins[pos:pos + len(srcs)]
            pos += len(srcs)
            ranges = [(lo, arr.shape[0]) if sc else (0, NDEV) for arr, lo in srcs]
            mine = [inside(me, lo, n) for lo, n in ranges]
            i_receive = None if None in mine else functools.reduce(jnp.logical_or, mine)
            for ref, (lo, n), cond in zip(refs, ranges, mine):
                def local(ref=ref, lo=lo):
                    cp = pltpu.make_async_copy(ref.at[me - lo] if sc else ref, outs[a].at[me], loc_sems.at[a])
                    cp.start() if phase == 0 else cp.wait()
                if phase != 1:
                    when(cond, local)
            for k in range(1, NDEV):
                px, py, pc = flip(x, k & 4), flip(y, k & 2), flip(c, k & 1)
                peer = 4 * px + 2 * py + pc
                mk = functools.partial(pltpu.make_async_remote_copy,
                                       send_sem=send_sems.at[a, k - 1], recv_sem=recv_sems.at[a, k - 1],
                                       device_id=(px, py, pc), device_id_type=MESH_ID)
                if phase == 1:
                    def arrival(mk=mk, peer=peer):
                        mk(src_ref=refs[0].at[0] if sc else refs[0], dst_ref=outs[a].at[peer]).wait_recv()
                    when(i_receive, arrival)
                    continue
                for ref, (lo, n) in zip(refs, ranges):
                    def send(mk=mk, ref=ref, lo=lo, peer=peer):
                        cp = mk(src_ref=ref.at[peer - lo] if sc else ref, dst_ref=outs[a].at[me])
                        cp.start() if phase == 0 else cp.wait_send()
                    when(inside(peer, lo, n), send)

    def start(self, ins, outs, sems):
        self._emit(ins, outs, sems, 0)

    def wait(self, ins, outs, sems):
        self._emit(ins, outs, sems, 1)
        self._emit(ins, outs, sems, 2)


def _call(body, *, name, grid, in_specs, out_specs, out_shape, args, scratch_shapes=(), sem=None, ride=None):
    if ride is None:
        return pl.pallas_call(body, name=name, grid=grid, in_specs=list(in_specs), out_specs=tuple(out_specs),
                              out_shape=tuple(out_shape), scratch_shapes=list(scratch_shapes),
                              compiler_params=_params(sem))(*args)
    n_in, n_out, n_scr, ne, ne_in = len(args), len(out_shape), len(scratch_shapes), ride.n, ride.n_in

    def ride_body(*refs):
        ins, ex_in = refs[:n_in], refs[n_in:n_in + ne_in]
        o0 = n_in + ne_in
        outs, ex_out = refs[o0:o0 + n_out], refs[o0 + n_out:o0 + n_out + ne]
        scr = refs[o0 + n_out + ne:o0 + n_out + ne + n_scr]
        sems = refs[o0 + n_out + ne + n_scr:]
        first = functools.reduce(jnp.logical_and, [pl.program_id(d) == 0 for d in range(len(grid))])
        last = functools.reduce(jnp.logical_and, [pl.program_id(d) == grid[d] - 1 for d in range(len(grid))])

        @pl.when(first)
        def _():
            ride.start(ex_in, ex_out, sems)

        body(*ins, *outs, *scr)

        @pl.when(last)
        def _():
            ride.wait(ex_in, ex_out, sems)

    anyspec = pl.BlockSpec(memory_space=pl.ANY)
    res = pl.pallas_call(
        ride_body, name=name, grid=grid,
        in_specs=list(in_specs) + [anyspec] * ne_in,
        out_specs=tuple(out_specs) + (anyspec,) * ne,
        out_shape=tuple(out_shape) + ride.out_shape(),
        scratch_shapes=list(scratch_shapes) + ride.scratch(),
        compiler_params=_params(("arbitrary",) * len(grid)),
    )(*args, *ride.arrays)
    return tuple(res[:n_out]), tuple(res[n_out:])


def _gather_two_level(arrays, *, name):
    n = len(arrays)

    def body(*refs):
        ins, outs = refs[:n], refs[n:2 * n]
        send_sems, recv_sems, loc_sems = refs[2 * n:]
        x, y, c = lax.axis_index("x"), lax.axis_index("y"), lax.axis_index("c")
        sibling = (x, y, 1 - c)
        chips = [(1 - x, y), (x, 1 - y), (1 - x, 1 - y)]
        idx = lambda px, py, pc: 4 * px + 2 * py + pc
        me = idx(x, y, c)

        def copy(a, k, block, to, src=None):
            return pltpu.make_async_remote_copy(
                src_ref=outs[a].at[block] if src is None else src, dst_ref=outs[a].at[block],
                send_sem=send_sems.at[a, k], recv_sem=recv_sems.at[a, k], device_id=to, device_id_type=MESH_ID)

        local = [pltpu.make_async_copy(ins[a], outs[a].at[me], loc_sems.at[a]) for a in range(n)]
        sent = []
        for a in range(n):
            sent.append(copy(a, 0, me, sibling, src=ins[a]))
            sent += [copy(a, 1 + j, me, (*chip, c), src=ins[a]) for j, chip in enumerate(chips)]
        for cp in local + sent:
            cp.start()
        for j, chip in enumerate(chips):
            for a in range(n):
                copy(a, 1 + j, idx(*chip, c), sibling).wait_recv()
                passed = copy(a, 4 + j, idx(*chip, c), sibling)
                passed.start()
                sent.append(passed)
        for a in range(n):
            copy(a, 0, idx(x, y, 1 - c), sibling).wait_recv()
            for j, chip in enumerate(chips):
                copy(a, 4 + j, idx(*chip, 1 - c), sibling).wait_recv()
        for cp in sent:
            cp.wait_send()
        for cp in local:
            cp.wait()

    anyspec = pl.BlockSpec(memory_space=pl.ANY)
    return pl.pallas_call(
        body, name=name,
        out_shape=tuple(jax.ShapeDtypeStruct((NDEV,) + a.shape, a.dtype) for a in arrays),
        in_specs=[anyspec] * n, out_specs=(anyspec,) * n,
        scratch_shapes=[pltpu.SemaphoreType.DMA((n, NDEV - 1)), pltpu.SemaphoreType.DMA((n, NDEV - 1)),
                        pltpu.SemaphoreType.DMA((n,))],
    )(*arrays)


def _tri_consts():
    r = lax.broadcasted_iota(jnp.int32, (GC, GC), 0)
    c = lax.broadcasted_iota(jnp.int32, (GC, GC), 1)
    return (r >= c).astype(BF), (r <= c).astype(BF)


def _local_step(x, positions, target, w, hooks=None):
    g = {}

    def run(host, fn, *a, **kw):
        h = None if hooks is None else hooks.get(host)
        if h is None:
            return fn(*a, **kw)
        out, received = fn(*a, ride=_Exchange(h[0](w, g)), **kw)
        h[1](received, w, g)
        return out

    nseq, S, _ = x.shape
    T = nseq * S
    tm = min(TOKEN_TM, S)
    tq = min(FLASH_TQ, S)
    x2 = x.reshape(T, D)
    pos = positions.reshape(T, 1)
    half = ROPE // 2
    inv = THETA ** (-jnp.arange(half, dtype=F32) / half)
    invf = jnp.concatenate([inv, inv, jnp.zeros((64,), F32)]).reshape(1, 128)
    ltri, utri = _tri_consts()

    pt, xb = _matmul(x2, w["w_tt"], "nt", name="proj_t", out_dtype=BF, tm=1024, tn=1024, tk=1024, emit_a=True)
    pg = run("proj_g", _matmul, xb, w["w_gt"], "nt", name="proj_g", tm=1024, tn=640, tk=1024)
    pm = _matmul(xb, w["w_mt"], "nt", name="proj_m", tm=1024, tn=768, tk=1024)
    o, zg, states = _gla_fwd(pg, w["wg"], w["bg"], w["gn"], ltri, nseq=nseq, S=S, tm=tm)
    qc, kc, v = _mla_prep_fwd(pm, pos, invf, w["gq"], w["gkv"], w["wuq"], w["wukv"], tm=tm)
    attn, lse = run("flash_fwd", _flash_fwd, qc, kc, v, nseq=nseq, S=S, tq=tq)
    yg, ym, mix, pre1, h1, h1b = _post_attn_fwd(zg, attn, pt, x2, w["wgo"], w["wmo"], w["wout"],
                                                w["g1"], w["b1"], tm=tm)
    ug, uv, ucg, ucv, f_in = _ffn_up_fwd(h1b, w["wug"], w["wuv"], w["cw"], w["cb"], S=S, tm=tm, tn=FFN_TN)
    dpre2b, dh1, loss8, dg2, db2 = _down_ln2_loss(f_in, w["wd"], h1, target.reshape(T, D), w["g2"], w["b2"],
                                                  tm=min(2 * tm, S))

    dug, duv, dcg, dcv = _ffn_bwd(dpre2b, w["wd"], ug, uv, ucg, ucv, w["cw"], S=S, tm=tm, tn=FFN_TN)
    g["g2"], g["b2"], g["loss"] = dg2, db2, loss8[0:1, 0:1]
    g["cw"] = jnp.concatenate([dcg[0:3], dcv[0:3]], axis=1)
    g["cb"] = jnp.concatenate([dcg[3:4], dcv[3:4]], axis=1)
    g["wd"] = _matmul(f_in, dpre2b, "tn", name="dw_down", out_dtype=BF, tm=1408, tn=1024, tk=1024)
    g["wugt"] = _matmul(dug, h1b, "tn", name="dw_up_g", out_dtype=BF, tm=1408, tn=1024, tk=1024)
    g["wuvt"] = _matmul(duv, h1b, "tn", name="dw_up_v", out_dtype=BF, tm=1408, tn=1024, tk=1024)
    dh1 = _matmul(dug, w["wugt"], "nn", name="dh1_g", c_in=dh1, tm=1024, tn=1024, tk=1408)
    dh1 = _matmul(duv, w["wuvt"], "nn", name="dh1_v", c_in=dh1, tm=1024, tn=1024, tk=1408)
    dx, dpre1b, dpt, dygb, dymb, dzg, dattn, dg1, db1 = _post_attn_bwd(
        dh1, pre1, pt, yg, ym, w["wgo"], w["wmo"], w["wout"], w["g1"], tm=tm)
    g["g1"], g["b1"] = dg1, db1
    g["wout"] = _matmul(mix, dpre1b, "tn", name="dw_out", out_dtype=BF, tm=1024, tn=1024, tk=1024)
    g["wgo"] = _matmul(zg, dygb, "tn", name="dw_gla_o", out_dtype=BF, tm=1024, tn=1024, tk=1024)
    g["wmo"] = _matmul(attn, dymb, "tn", name="dw_mla_o", out_dtype=BF, tm=1024, tn=1024, tk=1024)
    dqc, dkc, dv = run("flash_bwd", _flash_bwd, qc, kc, v, attn, dattn, lse, nseq=nseq, S=S, tq=tq)
    dpm, g["wuq"], g["wukv"], g["gq"], g["gkv"] = _mla_prep_bwd(
        pm, pos, invf, w["gq"], w["gkv"], w["wuq"], w["wukv"], dqc, dkc, dv, tm=tm)
    g["w_mt"] = _matmul(dpm, xb, "tn", name="dw_in_m", out_dtype=BF, tm=768, tn=1024, tk=1024)
    g["w_tt"] = _matmul(dpt, xb, "tn", name="dw_in_t", out_dtype=BF, tm=1024, tn=1024, tk=1024)
    dpg, g["wg"], g["bg"], g["gn"] = run("gla_bwd", _gla_bwd, pg, w["wg"], w["bg"], w["gn"], ltri, utri, o, states,
                                         dzg, nseq=nseq, S=S, tm=tm)
    g["w_gt"] = _matmul(dpg, xb, "tn", name="dw_in_g", out_dtype=BF, tm=640, tn=1024, tk=1024)
    dx = run("dx", _matmul_sum, dx, [(dpg, w["w_gt"], 640), (dpm, w["w_mt"], 768)], name="dx_gm")
    dx = _matmul_sum(dx, [(dpt, w["w_tt"], 1024)], name="dx_t")
    return loss8[0, 0], dx.reshape(nseq, S, D), g


_IN_SPLITS = (512, 512, 1024, 16, 1024, 384, 256, 64, 1024, 1024)


def _w_in_to_groups(wt):
    offs = [0]
    for s in _IN_SPLITS:
        offs.append(offs[-1] + s)
    q, k, v, r, og, cq, ckv, kr, ga, gb = [wt[offs[i]:offs[i + 1]] for i in range(10)]
    z = lambda n: jnp.zeros((n, wt.shape[1]), wt.dtype)
    return (jnp.concatenate([q, k, v, og, r, z(112)], axis=0),
            jnp.concatenate([cq, kr, z(64), ckv], axis=0),
            jnp.concatenate([ga, gb], axis=0))


W_IN_BLOCK = sum(_IN_SPLITS) // NDEV
_KV_LATENT_ROW = sum(_IN_SPLITS[:6])
_W_IN_LO = 5
_W_IN_SPLIT = _W_IN_LO * W_IN_BLOCK - _KV_LATENT_ROW


def _w_in_rows_lo(g_g, g_m):
    q, k, v, og, r = g_g[0:512], g_g[512:1024], g_g[1024:2048], g_g[2048:3072], g_g[3072:3088]
    return jnp.concatenate([q, k, v, r, og, g_m[0:384], g_m[512:768]], axis=0)[:_W_IN_LO * W_IN_BLOCK]


def _w_in_rows_hi(g_m, g_t):
    return jnp.concatenate([g_m[512:768], g_m[384:448], g_t], axis=0)[_W_IN_SPLIT:]


def _uq_to_kernel(wuq):
    w3 = wuq.reshape(MQR, MH, NOPE + ROPE)
    rope = jnp.concatenate([w3[:, :, NOPE:], jnp.zeros((MQR, MH, 64), wuq.dtype)], axis=2)
    return jnp.concatenate([w3[:, :, :NOPE].reshape(MQR, MH * 128), rope.reshape(MQR, MH * 128)], axis=1)


def _uq_from_kernel(g):
    nope = g[:, :1024].reshape(MQR, MH, 128)
    rope = g[:, 1024:].reshape(MQR, MH, 128)[:, :, :ROPE]
    return jnp.concatenate([nope, rope], axis=2)


def _ukv_to_kernel(wukv):
    w3 = wukv.reshape(MKR, MH, NOPE + MV)
    return jnp.concatenate([w3[:, :, :NOPE].reshape(MKR, MH * 128), w3[:, :, NOPE:].reshape(MKR, MH * 128)], axis=1)


def _ukv_from_kernel(g):
    return jnp.concatenate([g[:, :1024].reshape(MKR, MH, 128), g[:, 1024:].reshape(MKR, MH, 128)], axis=2)


def _cols_gathered(a):
    return a.transpose(1, 0, 2).reshape(a.shape[1], NDEV * a.shape[2])


def _cols_scattered(a):
    R = a.shape[0]
    return a.reshape(R, NDEV, a.shape[1] // NDEV).transpose(1, 0, 2)


_SMALL = (("gla_b_gate", 512), ("gla_norm_g", 256), ("mla_q_norm_g", 384), ("mla_kv_norm_g", 256),
          ("ln1_g", 1024), ("ln1_b", 1024), ("conv_b", 5632), ("ln2_g", 1024), ("ln2_b", 1024))
_SMALL_ROWS = 88
_SMALL_USED = sum(sz for _, sz in _SMALL)


def _pack_small(d):
    flat = jnp.concatenate([d[n].reshape(-1) for n, _ in _SMALL] + ([d['loss'].reshape(-1)] if 'loss' in d else []))
    return jnp.pad(flat, (0, _SMALL_ROWS * 128 - flat.shape[0])).reshape(_SMALL_ROWS, 128)


def _unpack_small(a):
    flat = a.reshape(-1)
    out, off = {}, 0
    for n, sz in _SMALL:
        out[n] = flat[off:off + sz].reshape(1, sz)
        off += sz
    return out


_NAMES = ['w_in', 'gla_w_gate_up', 'gla_b_gate', 'gla_norm_g', 'w_gla_o', 'mla_q_norm_g', 'mla_w_uq',
          'mla_kv_norm_g', 'mla_w_ukv', 'w_mla_o', 'w_out', 'ln1_g', 'ln1_b', 'w_up', 'conv_w', 'conv_b',
          'w_down', 'ln2_g', 'ln2_b']
_SHARDED = ['w_in', 'w_up', 'w_down', 'w_gla_o', 'w_mla_o', 'w_out', 'mla_w_uq', 'mla_w_ukv', 'gla_w_gate_up',
            'conv_w']


def kernel(x, positions, w_in, gla_w_gate_up, gla_b_gate, gla_norm_g, w_gla_o, mla_q_norm_g, mla_w_uq, mla_kv_norm_g, mla_w_ukv, w_mla_o, w_out, ln1_g, ln1_b, w_up, conv_w, conv_b, w_down, ln2_g, ln2_b, loss_target, m_w_in, m_gla_w_gate_up, m_gla_b_gate, m_gla_norm_g, m_w_gla_o, m_mla_q_norm_g, m_mla_w_uq, m_mla_kv_norm_g, m_mla_w_ukv, m_w_mla_o, m_w_out, m_ln1_g, m_ln1_b, m_w_up, m_conv_w, m_conv_b, m_w_down, m_ln2_g, m_ln2_b, v_w_in, v_gla_w_gate_up, v_gla_b_gate, v_gla_norm_g, v_w_gla_o, v_mla_q_norm_g, v_mla_w_uq, v_mla_kv_norm_g, v_mla_w_ukv, v_w_mla_o, v_w_out, v_ln1_g, v_ln1_b, v_w_up, v_conv_w, v_conv_b, v_w_down, v_ln2_g, v_ln2_b):
    W = dict(w_in=w_in, gla_w_gate_up=gla_w_gate_up, gla_b_gate=gla_b_gate, gla_norm_g=gla_norm_g, w_gla_o=w_gla_o, mla_q_norm_g=mla_q_norm_g, mla_w_uq=mla_w_uq, mla_kv_norm_g=mla_kv_norm_g, mla_w_ukv=mla_w_ukv, w_mla_o=w_mla_o, w_out=w_out, ln1_g=ln1_g, ln1_b=ln1_b, w_up=w_up, conv_w=conv_w, conv_b=conv_b, w_down=w_down, ln2_g=ln2_g, ln2_b=ln2_b)
    M = dict(w_in=m_w_in, gla_w_gate_up=m_gla_w_gate_up, gla_b_gate=m_gla_b_gate, gla_norm_g=m_gla_norm_g, w_gla_o=m_w_gla_o, mla_q_norm_g=m_mla_q_norm_g, mla_w_uq=m_mla_w_uq, mla_kv_norm_g=m_mla_kv_norm_g, mla_w_ukv=m_mla_w_ukv, w_mla_o=m_w_mla_o, w_out=m_w_out, ln1_g=m_ln1_g, ln1_b=m_ln1_b, w_up=m_w_up, conv_w=m_conv_w, conv_b=m_conv_b, w_down=m_w_down, ln2_g=m_ln2_g, ln2_b=m_ln2_b)
    V = dict(w_in=v_w_in, gla_w_gate_up=v_gla_w_gate_up, gla_b_gate=v_gla_b_gate, gla_norm_g=v_gla_norm_g, w_gla_o=v_w_gla_o, mla_q_norm_g=v_mla_q_norm_g, mla_w_uq=v_mla_w_uq, mla_kv_norm_g=v_mla_kv_norm_g, mla_w_ukv=v_mla_w_ukv, w_mla_o=v_w_mla_o, w_out=v_w_out, ln1_g=v_ln1_g, ln1_b=v_ln1_b, w_up=v_w_up, conv_w=v_conv_w, conv_b=v_conv_b, w_down=v_w_down, ln2_g=v_ln2_g, ln2_b=v_ln2_b)

    tshard = lambda d, n: d[n][0].T
    shard = lambda n: (W[n][0].astype(BF), False)
    first = ['w_in', 'mla_w_uq', 'mla_w_ukv', 'gla_w_gate_up']
    G = dict(zip(first, _gather_two_level(
        [tshard(W, 'w_in').astype(BF)] + [shard(n)[0] for n in first[1:]], name="gather_w0")))
    w_gt, w_mt, w_tt = _w_in_to_groups(G['w_in'].reshape(NDEV * W_IN_BLOCK, D))
    kw = dict(
        w_gt=w_gt, w_mt=w_mt, w_tt=w_tt,
        wg=jnp.pad(_cols_gathered(G['gla_w_gate_up']), ((0, 128 - GR), (0, 0))), bg=W['gla_b_gate'],
        gn=W['gla_norm_g'], gq=W['mla_q_norm_g'], gkv=W['mla_kv_norm_g'],
        wuq=_uq_to_kernel(_cols_gathered(G['mla_w_uq'])), wukv=_ukv_to_kernel(_cols_gathered(G['mla_w_ukv'])),
        g1=W['ln1_g'], b1=W['ln1_b'], g2=W['ln2_g'], b2=W['ln2_b'], cb=W['conv_b'],
    )
    received = {}

    def got_out_proj(ex, w, g):
        w.update(wgo=ex[0].reshape(D, D), wmo=ex[1].reshape(D, D), wout=ex[2].reshape(D, D))

    def got_ffn(ex, w, g):
        w_upt = ex[0].reshape(2 * DFF, D)
        w.update(wugt=w_upt[:DFF], wuvt=w_upt[DFF:], wug=w_upt[:DFF].T, wuv=w_upt[DFF:].T,
                 wd=ex[1].reshape(DFF, D), cw=_cols_gathered(ex[2]))

    slab = lambda a, lo=0: ([(a.astype(BF), lo)], True)
    rows = lambda a, n=NDEV: a.reshape(n, a.shape[0] // n, a.shape[1])

    def keep(names):
        return lambda ex, w, g: received.update(zip(names, ex))

    def small_grads(g):
        return _pack_small(dict(gla_b_gate=g['bg'], gla_norm_g=g['gn'], mla_q_norm_g=g['gq'], mla_kv_norm_g=g['gkv'],
                                ln1_g=g['g1'], ln1_b=g['b1'], conv_b=g['cb'], ln2_g=g['g2'], ln2_b=g['b2'],
                                loss=g['loss']))

    hooks = {
        "proj_g": (lambda w, g: [shard('w_gla_o'), shard('w_mla_o'), shard('w_out')], got_out_proj),
        "flash_fwd": (lambda w, g: [(tshard(W, 'w_up').astype(BF), False), shard('w_down'), (W['conv_w'][0], False)],
                      got_ffn),
        "flash_bwd": (lambda w, g: [slab(rows(g['wd'])),
                                    ([(rows(g['wugt'], 4), 0), (rows(g['wuvt'], 4), 4)], True),
                                    slab(rows(g['wout'])), slab(rows(g['wgo'])), slab(rows(g['wmo']))],
                      keep(['w_down', 'w_up', 'w_out', 'w_gla_o', 'w_mla_o'])),
        "gla_bwd": (lambda w, g: [slab(_uq_from_kernel(g['wuq']).transpose(1, 0, 2)),
                                  slab(_ukv_from_kernel(g['wukv']).transpose(1, 0, 2)),
                                  slab(rows(_w_in_rows_hi(g['w_mt'], g['w_tt']), NDEV - _W_IN_LO), _W_IN_LO)],
                    keep(['mla_w_uq', 'mla_w_ukv', 'w_in_hi'])),
        "dx": (lambda w, g: [slab(rows(_w_in_rows_lo(g['w_gt'], g['w_mt']), _W_IN_LO)),
                             ([(_cols_scattered(g['wg'][:GR]), 0)], True), ([(_cols_scattered(g['cw']), 0)], True),
                             (small_grads(g), False)],
               keep(['w_in_lo', 'gla_w_gate_up', 'conv_w', 'small'])),
    }

    _, grad_x, _ = _local_step(x, positions, loss_target, kw, hooks)

    grads, deltas, new_m, new_v = {}, {}, {}, {}
    small_parts = received['small']
    loss = jnp.sum(small_parts.reshape(NDEV, -1)[:, _SMALL_USED])
    me = 4 * lax.axis_index("x") + 2 * lax.axis_index("y") + lax.axis_index("c")
    received['w_in'] = jnp.where(me >= _W_IN_LO, received['w_in_hi'], received['w_in_lo'])
    for n in _SHARDED:
        shp = W[n].shape
        if n in ('w_in', 'w_up'):
            out = _adamw(received[n], tshard(W, n), tshard(M, n), tshard(V, n), name="adamw_" + n)
            grads[n], deltas[n], new_m[n], new_v[n] = [t.T.reshape(shp) for t in out]
            continue
        out = _adamw(received[n], W[n][0], M[n][0], V[n][0], name="adamw_" + n)
        grads[n], deltas[n], new_m[n], new_v[n] = [t.reshape(shp) for t in out]
    out = _adamw(small_parts, _pack_small(W), _pack_small(M), _pack_small(V), name="adamw_small")
    for dst, packed in zip((grads, deltas, new_m, new_v), out):
        dst.update(_unpack_small(packed))

    return (loss, grad_x, *[grads[n] for n in _NAMES], *[deltas[n] for n in _NAMES],
            *[new_m[n] for n in _NAMES], *[new_v[n] for n in _NAMES])
```

```python
import functools

import jax
import jax.numpy as jnp
from jax import lax
from jax.experimental import pallas as pl
from jax.experimental.pallas import tpu as pltpu

F32 = jnp.float32
BF = jnp.bfloat16

D = 1024
GH, GDK, GDV, GR, GTAU, GC = 4, 128, 256, 16, 16.0, 64
MH, MQR, MKR, NOPE, ROPE, MV = 8, 384, 256, 128, 64, 128
THETA = 10000.0
DFF = 2816
ALPHA = 2.0 ** 0.25
LN_EPS = 1e-5
RMS_EPS = 1e-6
NDEV = 8
ADAM_LR, ADAM_B1, ADAM_B2, ADAM_EPS, ADAM_WD, ADAM_STEP = 0.001, 0.9, 0.999, 1e-08, 0.01, 10

PG_W = 3200
PM_W = 768
PT_W = 2048
NEG = -1e30
MESH_ID = pl.DeviceIdType.MESH
VMEM_MB = 1024 * 1024


V7X_VMEM_LIMIT_MB = 48
TOKEN_TM = 256
FLASH_TQ = 512
FFN_TN = 1408


def _params(sem):
    return pltpu.CompilerParams(dimension_semantics=sem, vmem_limit_bytes=V7X_VMEM_LIMIT_MB * VMEM_MB)


def _dot(a, b):
    return lax.dot_general(a, b, (((1,), (0,)), ((), ())), preferred_element_type=F32)


def _dot_nt(a, b):
    return lax.dot_general(a, b, (((1,), (1,)), ((), ())), preferred_element_type=F32)


def _dot_tn(a, b):
    return lax.dot_general(a, b, (((0,), (0,)), ((), ())), preferred_element_type=F32)


def _iota(shape, dim):
    return lax.broadcasted_iota(jnp.int32, shape, dim)


FLASH_HP = 2
FLASH_HP_FWD = 4
QK_SCALE = (NOPE + ROPE) ** -0.5
LOG2E = 1.4426950408889634
QK_SCALE_LOG2 = QK_SCALE * LOG2E


def _sigmoid(x):
    return 0.5 * jnp.tanh(0.5 * x) + 0.5


def _tri_mm(tri_bf, x):
    hi = x.astype(BF)
    r1 = x - hi.astype(F32)
    mid = r1.astype(BF)
    lo = (r1 - mid.astype(F32)).astype(BF)
    return _dot(tri_bf, hi) + _dot(tri_bf, mid) + _dot(tri_bf, lo)


def _matmul(a, b, mode, *, name, c_in=None, out_dtype=F32, tm=512, tn=512, tk=512, ride=None, emit_a=False):
    if mode == "nn":
        (M, K), (_, N) = a.shape, b.shape
    elif mode == "nt":
        (M, K), (N, _) = a.shape, b.shape
    else:
        (K, M), (_, N) = a.shape, b.shape
    tm, tn, tk = min(tm, M), min(tn, N), min(tk, K)
    assert M % tm == 0 and N % tn == 0 and K % tk == 0, (name, M, N, K, tm, tn, tk)
    nk = K // tk
    assert not emit_a or (nk == 1 and mode != "tn" and c_in is None and ride is None)
    dot = {"nn": _dot, "nt": _dot_nt, "tn": _dot_tn}[mode]

    def body(*refs):
        if emit_a:
            a_ref, b_ref, o_ref, xa_ref, acc_ref = refs
        elif c_in is None:
            a_ref, b_ref, o_ref, acc_ref = refs
        else:
            a_ref, b_ref, c_ref, o_ref, acc_ref = refs
        k = pl.program_id(2)

        @pl.when(k == 0)
        def _():
            if c_in is None:
                acc_ref[...] = jnp.zeros_like(acc_ref)
            else:
                acc_ref[...] = c_ref[...].astype(F32)

        if emit_a:
            @pl.when(pl.program_id(1) == 0)
            def _():
                xa_ref[...] = a_ref[...].astype(BF)

        acc_ref[...] += dot(a_ref[...].astype(BF), b_ref[...].astype(BF))

        @pl.when(k == nk - 1)
        def _():
            o_ref[...] = acc_ref[...].astype(out_dtype)

    if mode == "tn":
        a_spec = pl.BlockSpec((tk, tm), lambda i, j, k: (k, i))
    else:
        a_spec = pl.BlockSpec((tm, tk), lambda i, j, k: (i, k))
    if mode == "nt":
        b_spec = pl.BlockSpec((tn, tk), lambda i, j, k: (j, k))
    else:
        b_spec = pl.BlockSpec((tk, tn), lambda i, j, k: (k, j))
    in_specs = [a_spec, b_spec]
    args = [a, b]
    if c_in is not None:
        in_specs.append(pl.BlockSpec((tm, tn), lambda i, j, k: (i, j)))
        args.append(c_in)
    out_shape = (jax.ShapeDtypeStruct((M, N), out_dtype),)
    out_specs = (pl.BlockSpec((tm, tn), lambda i, j, k: (i, j)),)
    if emit_a:
        out_shape += (jax.ShapeDtypeStruct((M, K), BF),)
        out_specs += (pl.BlockSpec((tm, tk), lambda i, j, k: (i, k)),)
    res = _call(
        body, name=name, out_shape=out_shape, grid=(M // tm, N // tn, nk), in_specs=in_specs, out_specs=out_specs,
        scratch_shapes=[pltpu.VMEM((tm, tn), F32)],
        sem=("parallel", "arbitrary", "arbitrary"), args=args, ride=ride)
    if emit_a:
        return res[0], res[1]
    return res[0] if ride is None else (res[0][0], res[1])


def _matmul_sum(c_in, parts, *, name, tm=1024, ride=None):
    M, N = c_in.shape
    tm = min(tm, M)
    n_p = len(parts)
    counts = [a.shape[1] // tk for a, _, tk in parts]
    starts = [sum(counts[:p]) for p in range(n_p)]
    nk = sum(counts)

    def body(*refs):
        a_refs, w_refs = refs[:n_p], refs[n_p:2 * n_p]
        c_ref, o_ref, acc_ref = refs[2 * n_p:]
        k = pl.program_id(1)

        @pl.when(k == 0)
        def _():
            acc_ref[...] = c_ref[...]

        for p in range(n_p):
            @pl.when(jnp.logical_and(k >= starts[p], k < starts[p] + counts[p]))
            def _(p=p):
                acc_ref[...] += _dot(a_refs[p][...].astype(BF), w_refs[p][...].astype(BF))

        @pl.when(k == nk - 1)
        def _():
            o_ref[...] = acc_ref[...]

    def kidx(p):
        return lambda k: jnp.clip(k - starts[p], 0, counts[p] - 1)

    in_specs = [pl.BlockSpec((tm, tk), lambda i, k, f=kidx(p): (i, f(k))) for p, (_, _, tk) in enumerate(parts)]
    in_specs += [pl.BlockSpec((tk, N), lambda i, k, f=kidx(p): (f(k), 0)) for p, (_, _, tk) in enumerate(parts)]
    in_specs.append(pl.BlockSpec((tm, N), lambda i, k: (i, 0)))
    res = _call(
        body, name=name, out_shape=(jax.ShapeDtypeStruct((M, N), F32),), grid=(M // tm, nk),
        in_specs=in_specs, out_specs=(pl.BlockSpec((tm, N), lambda i, k: (i, 0)),),
        scratch_shapes=[pltpu.VMEM((tm, N), F32)], sem=("parallel", "arbitrary"),
        args=[a for a, _, _ in parts] + [w for _, w, _ in parts] + [c_in], ride=ride)
    return res[0] if ride is None else (res[0][0], res[1])


def _gla_gate(pg_ref, rows, wg_ref, bg_ref):
    r = pg_ref[rows, 3072:3200].astype(BF)
    logit = _dot(r, wg_ref[...]) + bg_ref[...]
    la = (jnp.minimum(logit, 0.0) - jnp.log(1.0 + jnp.exp(-jnp.abs(logit)))) * (1.0 / GTAU)
    return r, logit, la


def _gla_fwd(pg, wg, bg, gn, ltri, *, nseq, S, tm):
    T = pg.shape[0]
    nb, nc = S // tm, tm // GC
    qscale = GDK ** -0.5

    def body(pg_ref, wg_ref, bg_ref, gn_ref, l_ref, o_ref, zg_ref, st_ref, st_scr):
        @pl.when(pl.program_id(1) == 0)
        def _():
            st_scr[...] = jnp.zeros_like(st_scr)

        ltri_v = l_ref[...]
        causal = _iota((GC, GC), 0) >= _iota((GC, GC), 1)
        last_row = _iota((GC, GDK), 0) == GC - 1
        g = gn_ref[...]

        def chunk(c, carry):
            rows = pl.ds(pl.multiple_of(c * GC, GC), GC)
            _, _, la = _gla_gate(pg_ref, rows, wg_ref, bg_ref)
            b = _tri_mm(ltri_v, la)
            hs = range(GH)
            v, q_in, k_st, dec, st, a_raw, o_st, kv = [], [], [], [], [], [], [], []
            for h in hs:
                q = pg_ref[rows, h * GDK:(h + 1) * GDK]
                k = pg_ref[rows, 512 + h * GDK:512 + (h + 1) * GDK]
                v.append(pg_ref[rows, 1024 + h * GDV:1024 + (h + 1) * GDV].astype(BF))
                bh = b[:, h * GDK:(h + 1) * GDK]
                bl = jnp.sum(jnp.where(last_row, bh, 0.0), axis=0, keepdims=True)
                q_in.append((q * (qscale * jnp.exp(bh))).astype(BF))
                k_in = (k * jnp.exp(-bh)).astype(BF)
                k_st.append((k * jnp.exp(bl - bh)).astype(BF))
                dec.append(jnp.exp(bl))
                st.append(st_scr[h])
                st_ref[c, h] = st[h]
                a_raw.append(_dot_nt(q_in[h], k_in))
            for h in hs:
                o_st.append(_dot_nt(q_in[h], st[h].astype(BF)))
                kv.append(_dot_tn(v[h], k_st[h]))
            att = [jnp.where(causal, a_raw[h], 0.0).astype(BF) for h in hs]
            o = [_dot(att[h], v[h]) + o_st[h] for h in hs]
            for h in hs:
                st_scr[h] = st[h] * dec[h] + kv[h]
                og = pg_ref[rows, 2048 + h * GDV:2048 + (h + 1) * GDV]
                rstd = lax.rsqrt(jnp.mean(o[h] * o[h], axis=-1, keepdims=True) + RMS_EPS)
                o_ref[rows, h * GDV:(h + 1) * GDV] = o[h]
                zg_ref[rows, h * GDV:(h + 1) * GDV] = (o[h] * rstd * g * (og * _sigmoid(og))).astype(BF)
            return carry

        lax.fori_loop(0, nc, chunk, 0, unroll=True)

    full = lambda shp: pl.BlockSpec(shp, lambda b_, i: (0,) * len(shp))
    return pl.pallas_call(
        body, name="gla_fwd",
        out_shape=(jax.ShapeDtypeStruct((T, GH * GDV), F32),
                   jax.ShapeDtypeStruct((T, GH * GDV), BF),
                   jax.ShapeDtypeStruct((T // GC, GH, GDV, GDK), F32)),
        grid=(nseq, nb),
        in_specs=[pl.BlockSpec((tm, PG_W), lambda b_, i: (b_ * nb + i, 0)),
                  full((128, 512)), full((1, 512)), full((1, GDV)), full((GC, GC))],
        out_specs=(pl.BlockSpec((tm, GH * GDV), lambda b_, i: (b_ * nb + i, 0)),
                   pl.BlockSpec((tm, GH * GDV), lambda b_, i: (b_ * nb + i, 0)),
                   pl.BlockSpec((nc, GH, GDV, GDK), lambda b_, i: (b_ * nb + i, 0, 0, 0))),
        scratch_shapes=[pltpu.VMEM((GH, GDV, GDK), F32)],
        compiler_params=_params(("parallel", "arbitrary")),
    )(pg, wg, bg, gn, ltri)


def _gla_bwd(pg, wg, bg, gn, ltri, utri, o, states, dzg, *, nseq, S, tm, ride=None):
    T = pg.shape[0]
    nb, nc = S // tm, tm // GC
    qscale = GDK ** -0.5

    def body(pg_ref, wg_ref, bg_ref, gn_ref, l_ref, u_ref, o_ref, st_ref, dzg_ref,
             dpg_ref, dwg_ref, dbg_ref, dgn_ref, dst_scr):
        first = jnp.logical_and(pl.program_id(0) == 0, pl.program_id(1) == 0)

        @pl.when(first)
        def _():
            dwg_ref[...] = jnp.zeros_like(dwg_ref)
            dbg_ref[...] = jnp.zeros_like(dbg_ref)
            dgn_ref[...] = jnp.zeros_like(dgn_ref)

        @pl.when(pl.program_id(1) == 0)
        def _():
            dst_scr[...] = jnp.zeros_like(dst_scr)

        ltri_v = l_ref[...]
        utri_v = u_ref[...]
        causal = _iota((GC, GC), 0) >= _iota((GC, GC), 1)
        last_row = _iota((GC, GDK), 0) == GC - 1
        g = gn_ref[...]

        def chunk(cc, carry):
            c = nc - 1 - cc
            rows = pl.ds(pl.multiple_of(c * GC, GC), GC)
            r, logit, la = _gla_gate(pg_ref, rows, wg_ref, bg_ref)
            b = _tri_mm(ltri_v, la)
            hs = range(GH)
            L = lambda: [None] * GH
            vb, eb, enb, ek, dec, q_in, k_in, k_st, q_inb, k_inb, st, dst, dob = (L() for _ in range(13))
            a_raw, da_raw, dq_st, dks, dv_st, dst_new, dbs, dgn = (L() for _ in range(8))
            for h in hs:
                q = pg_ref[rows, h * GDK:(h + 1) * GDK]
                k = pg_ref[rows, 512 + h * GDK:512 + (h + 1) * GDK]
                vb[h] = pg_ref[rows, 1024 + h * GDV:1024 + (h + 1) * GDV].astype(BF)
                og = pg_ref[rows, 2048 + h * GDV:2048 + (h + 1) * GDV]
                oh = o_ref[rows, h * GDV:(h + 1) * GDV]
                dz = dzg_ref[rows, h * GDV:(h + 1) * GDV].astype(F32)
                bh = b[:, h * GDK:(h + 1) * GDK]
                bl = jnp.sum(jnp.where(last_row, bh, 0.0), axis=0, keepdims=True)
                eb[h] = qscale * jnp.exp(bh)
                enb[h] = jnp.exp(-bh)
                ek[h] = jnp.exp(bl - bh)
                dec[h] = jnp.exp(bl)
                q_in[h], k_in[h], k_st[h] = q * eb[h], k * enb[h], k * ek[h]
                q_inb[h], k_inb[h] = q_in[h].astype(BF), k_in[h].astype(BF)
                st[h] = st_ref[c, h]
                dst[h] = dst_scr[h]
                rstd = lax.rsqrt(jnp.mean(oh * oh, axis=-1, keepdims=True) + RMS_EPS)
                ohat = oh * rstd
                sg = _sigmoid(og)
                don = dz * (og * sg)
                dpg_ref[rows, 2048 + h * GDV:2048 + (h + 1) * GDV] = (
                    dz * (ohat * g) * (sg * (1.0 + og * (1.0 - sg)))).astype(BF)
                dgn[h] = jnp.sum(don * ohat, axis=0, keepdims=True)
                gd = don * g
                dob[h] = (rstd * (gd - ohat * jnp.mean(gd * ohat, axis=-1, keepdims=True))).astype(BF)
                a_raw[h] = _dot_nt(q_inb[h], k_inb[h])
                da_raw[h] = _dot_nt(dob[h], vb[h])
            dgn_ref[...] += dgn[0] + dgn[1] + dgn[2] + dgn[3]
            for h in hs:
                dstb = dst[h].astype(BF)
                dq_st[h] = _dot(dob[h], st[h].astype(BF))
                dks[h] = _dot(vb[h], dstb)
                dv_st[h] = _dot_nt(k_st[h].astype(BF), dstb)
                dst_new[h] = _dot_tn(dob[h], q_inb[h])
            att = [jnp.where(causal, a_raw[h], 0.0).astype(BF) for h in hs]
            da = [jnp.where(causal, da_raw[h], 0.0).astype(BF) for h in hs]
            dqi = [_dot(da[h], k_inb[h]) + dq_st[h] for h in hs]
            dki = [_dot_tn(da[h], q_inb[h]) for h in hs]
            dv = [_dot_tn(att[h], dob[h]) + dv_st[h] for h in hs]
            for h in hs:
                dd = jnp.sum(dst[h] * st[h], axis=0, keepdims=True)
                dst_scr[h] = dst[h] * dec[h] + dst_new[h]
                kk = dks[h] * k_st[h]
                dbl = jnp.sum(kk, axis=0, keepdims=True) + dd * dec[h]
                db = dqi[h] * q_in[h] - dki[h] * k_in[h] - kk
                dbs[h] = db + jnp.where(last_row, dbl, 0.0)
                dpg_ref[rows, h * GDK:(h + 1) * GDK] = (dqi[h] * eb[h]).astype(BF)
                dpg_ref[rows, 512 + h * GDK:512 + (h + 1) * GDK] = (dki[h] * enb[h] + dks[h] * ek[h]).astype(BF)
                dpg_ref[rows, 1024 + h * GDV:1024 + (h + 1) * GDV] = dv[h].astype(BF)
            dla = _tri_mm(utri_v, jnp.concatenate(dbs, axis=1))
            dlogit = dla * (1.0 / GTAU) * _sigmoid(-logit)
            dlb = dlogit.astype(BF)
            dpg_ref[rows, 3072:3200] = _dot_nt(dlb, wg_ref[...]).astype(BF)
            dwg_ref[...] += _dot_tn(r, dlb)
            dbg_ref[...] += jnp.sum(dlogit, axis=0, keepdims=True)
            return carry

        lax.fori_loop(0, nc, chunk, 0, unroll=True)

    full = lambda shp: pl.BlockSpec(shp, lambda b_, i: (0,) * len(shp))
    rev = lambda b_, i: (b_ * nb + nb - 1 - i, 0)
    return _call(
        body, name="gla_bwd", ride=ride, sem=("arbitrary", "arbitrary"),
        args=(pg, wg, bg, gn, ltri, utri, o, states, dzg),
        out_shape=(jax.ShapeDtypeStruct((T, PG_W), BF),
                   jax.ShapeDtypeStruct((128, 512), F32),
                   jax.ShapeDtypeStruct((1, 512), F32),
                   jax.ShapeDtypeStruct((1, GDV), F32)),
        grid=(nseq, nb),
        in_specs=[pl.BlockSpec((tm, PG_W), rev),
                  full((128, 512)), full((1, 512)), full((1, GDV)), full((GC, GC)), full((GC, GC)),
                  pl.BlockSpec((tm, GH * GDV), rev),
                  pl.BlockSpec((nc, GH, GDV, GDK), lambda b_, i: (b_ * nb + nb - 1 - i, 0, 0, 0)),
                  pl.BlockSpec((tm, GH * GDV), rev)],
        out_specs=(pl.BlockSpec((tm, PG_W), rev), full((128, 512)), full((1, 512)), full((1, GDV))),
        scratch_shapes=[pltpu.VMEM((GH, GDV, GDK), F32)])


def _rope_tables(pos, invf):
    ang = pos.astype(F32) * invf
    lane = _iota(ang.shape, 1)
    sin = jnp.sin(ang)
    ssin = jnp.where(lane < 32, -sin, jnp.where(lane < 64, sin, 0.0))
    return jnp.cos(ang), ssin, lane


def _rope(x, cos, ssin, lane, sign):
    rot = jnp.where(lane < 32, pltpu.roll(x, 96, 1), pltpu.roll(x, 32, 1))
    return x * cos + sign * (rot * ssin)


def _rms_fwd(x, g):
    rstd = lax.rsqrt(jnp.mean(x * x, axis=-1, keepdims=True) + RMS_EPS)
    return x * rstd * g, x * rstd, rstd


def _rms_bwd(dy, xhat, rstd, g):
    gd = dy * g
    return rstd * (gd - xhat * jnp.mean(gd * xhat, axis=-1, keepdims=True)), jnp.sum(dy * xhat, axis=0, keepdims=True)


def _mla_prep_fwd(pm, pos, invf, gq, gkv, wuq, wukv, *, tm):
    T = pm.shape[0]

    def body(pm_ref, pos_ref, invf_ref, gq_ref, gkv_ref, wuq_ref, wukv_ref, qc_ref, kc_ref, v_ref):
        cos, ssin, lane = _rope_tables(pos_ref[...], invf_ref[...])
        cq, _, _ = _rms_fwd(pm_ref[:, 0:MQR], gq_ref[...])
        ckv, _, _ = _rms_fwd(pm_ref[:, 512:768], gkv_ref[...])
        qf = _dot(cq.astype(BF), wuq_ref[...])
        kvf = _dot(ckv.astype(BF), wukv_ref[...])
        kr = _rope(pm_ref[:, 384:512], cos, ssin, lane, 1.0).astype(BF)
        for h in range(MH):
            qc_ref[:, 256 * h:256 * h + 128] = (QK_SCALE_LOG2 * qf[:, 128 * h:128 * h + 128]).astype(BF)
            qr = qf[:, 1024 + 128 * h:1024 + 128 * h + 128]
            qc_ref[:, 256 * h + 128:256 * h + 256] = (QK_SCALE_LOG2 * _rope(qr, cos, ssin, lane, 1.0)).astype(BF)
            kc_ref[:, 256 * h:256 * h + 128] = kvf[:, 128 * h:128 * h + 128].astype(BF)
            kc_ref[:, 256 * h + 128:256 * h + 256] = kr
        v_ref[...] = kvf[:, 1024:2048].astype(BF)

    full = lambda shp: pl.BlockSpec(shp, lambda i: (0,) * len(shp))
    row = lambda w: pl.BlockSpec((tm, w), lambda i: (i, 0))
    return pl.pallas_call(
        body, name="mla_prep_fwd",
        out_shape=(jax.ShapeDtypeStruct((T, MH * 256), BF), jax.ShapeDtypeStruct((T, MH * 256), BF),
                   jax.ShapeDtypeStruct((T, MH * MV), BF)),
        grid=(T // tm,),
        in_specs=[row(PM_W), row(1), full((1, 128)), full((1, MQR)), full((1, MKR)),
                  full((MQR, 2048)), full((MKR, 2048))],
        out_specs=(row(MH * 256), row(MH * 256), row(MH * MV)),
        compiler_params=_params(("parallel",)),
    )(pm, pos, invf, gq, gkv, wuq, wukv)


def _mla_prep_bwd(pm, pos, invf, gq, gkv, wuq, wukv, dqc, dkc, dv, *, tm):
    T = pm.shape[0]

    def body(pm_ref, pos_ref, invf_ref, gq_ref, gkv_ref, wuq_ref, wukv_ref, dqc_ref, dkc_ref, dv_ref,
             dpm_ref, dwuq_ref, dwukv_ref, dgq_ref, dgkv_ref):
        @pl.when(pl.program_id(0) == 0)
        def _():
            dwuq_ref[...] = jnp.zeros_like(dwuq_ref)
            dwukv_ref[...] = jnp.zeros_like(dwukv_ref)
            dgq_ref[...] = jnp.zeros_like(dgq_ref)
            dgkv_ref[...] = jnp.zeros_like(dgkv_ref)

        cos, ssin, lane = _rope_tables(pos_ref[...], invf_ref[...])
        cq, cqh, cq_rstd = _rms_fwd(pm_ref[:, 0:MQR], gq_ref[...])
        ckv, ckvh, ckv_rstd = _rms_fwd(pm_ref[:, 512:768], gkv_ref[...])
        dqn, dqr, dkn = [], [], []
        dkr = jnp.zeros((tm, 128), F32)
        for h in range(MH):
            dqn.append(dqc_ref[:, 256 * h:256 * h + 128].astype(BF))
            dqr.append(_rope(dqc_ref[:, 256 * h + 128:256 * h + 256], cos, ssin, lane, -1.0).astype(BF))
            dkn.append(dkc_ref[:, 256 * h:256 * h + 128].astype(BF))
            dkr = dkr + dkc_ref[:, 256 * h + 128:256 * h + 256]
        dqf = jnp.concatenate(dqn + dqr, axis=1)
        dkvf = jnp.concatenate(dkn + [dv_ref[...].astype(BF)], axis=1)
        dwuq_ref[...] += _dot_tn(cq.astype(BF), dqf)
        dwukv_ref[...] += _dot_tn(ckv.astype(BF), dkvf)
        dcq, dgq = _rms_bwd(_dot_nt(dqf, wuq_ref[...]), cqh, cq_rstd, gq_ref[...])
        dckv, dgkv = _rms_bwd(_dot_nt(dkvf, wukv_ref[...]), ckvh, ckv_rstd, gkv_ref[...])
        dgq_ref[...] += dgq
        dgkv_ref[...] += dgkv
        dpm_ref[:, 0:MQR] = dcq.astype(BF)
        dpm_ref[:, 384:512] = _rope(dkr, cos, ssin, lane, -1.0).astype(BF)
        dpm_ref[:, 512:768] = dckv.astype(BF)

    full = lambda shp: pl.BlockSpec(shp, lambda i: (0,) * len(shp))
    row = lambda w: pl.BlockSpec((tm, w), lambda i: (i, 0))
    return pl.pallas_call(
        body, name="mla_prep_bwd",
        out_shape=(jax.ShapeDtypeStruct((T, PM_W), BF), jax.ShapeDtypeStruct((MQR, 2048), F32),
                   jax.ShapeDtypeStruct((MKR, 2048), F32), jax.ShapeDtypeStruct((1, MQR), F32),
                   jax.ShapeDtypeStruct((1, MKR), F32)),
        grid=(T // tm,),
        in_specs=[row(PM_W), row(1), full((1, 128)), full((1, MQR)), full((1, MKR)),
                  full((MQR, 2048)), full((MKR, 2048)), row(MH * 256), row(MH * 256), row(MH * MV)],
        out_specs=(row(PM_W), full((MQR, 2048)), full((MKR, 2048)), full((1, MQR)), full((1, MKR))),
        compiler_params=_params(("arbitrary",)),
    )(pm, pos, invf, gq, gkv, wuq, wukv, dqc, dkc, dv)


def _flash_fwd(qc, kc, v, *, nseq, S, tq, ride=None):
    T = qc.shape[0]
    nq = S // tq
    hp = FLASH_HP_FWD

    def body(q_ref, k_ref, v_ref, o_ref, lse_ref):
        i = pl.program_id(2)
        causal = _iota((tq, tq), 0) >= _iota((tq, tq), 1)

        def step(j, carry, masked):
            rows = pl.ds(pl.multiple_of(j * tq, tq), tq)
            hs = range(hp)
            s = [_dot_nt(q_ref[:, 256 * hh:256 * hh + 256], k_ref[rows, 256 * hh:256 * hh + 256]) for hh in hs]
            p, stats = [], []
            for hh in hs:
                m, l, _ = carry[hh]
                sh = jnp.where(causal, s[hh], NEG) if masked else s[hh]
                m_new = jnp.maximum(m, jnp.max(sh, axis=-1, keepdims=True))
                ph = jnp.exp2(sh - m_new)
                a = jnp.exp2(m - m_new)
                stats.append((m_new, a * l + jnp.sum(ph, axis=-1, keepdims=True), a))
                p.append(ph.astype(BF))
            pv = [_dot(p[hh], v_ref[rows, MV * hh:MV * hh + MV]) for hh in hs]
            return tuple((stats[hh][0], stats[hh][1], stats[hh][2] * carry[hh][2] + pv[hh]) for hh in hs)

        init = ((jnp.full((tq, 1), NEG, F32), jnp.zeros((tq, 1), F32), jnp.zeros((tq, MV), F32)),) * hp
        carry = lax.fori_loop(0, i, lambda j, c: step(j, c, False), init)
        for hh, (m, l, acc) in enumerate(step(i, carry, True)):
            o_ref[:, MV * hh:MV * hh + MV] = (acc / l).astype(BF)
            lse_ref[:, 128 * hh:128 * hh + 128] = jnp.broadcast_to(m + jnp.log2(l), (tq, 128))

    return _call(
        body, name="flash_fwd", ride=ride, sem=("parallel", "parallel", "arbitrary"), args=(qc, kc, v),
        out_shape=(jax.ShapeDtypeStruct((T, MH * MV), BF), jax.ShapeDtypeStruct((T, MH * 128), F32)),
        grid=(nseq, MH // hp, nq),
        in_specs=[pl.BlockSpec((tq, 256 * hp), lambda b_, h, i: (b_ * nq + i, h)),
                  pl.BlockSpec((S, 256 * hp), lambda b_, h, i: (b_, h)),
                  pl.BlockSpec((S, MV * hp), lambda b_, h, i: (b_, h))],
        out_specs=(pl.BlockSpec((tq, MV * hp), lambda b_, h, i: (b_ * nq + i, h)),
                   pl.BlockSpec((tq, 128 * hp), lambda b_, h, i: (b_ * nq + i, h))))


def _flash_bwd(qc, kc, v, o, do, lse, *, nseq, S, tq, ride=None):
    T = qc.shape[0]
    nq = S // tq

    def body(q_ref, k_ref, v_ref, o_ref, do_ref, lse_ref, dq_ref, dk_ref, dv_ref, dq_scr, delta_scr):
        j = pl.program_id(2)

        @pl.when(j == 0)
        def _():
            dq_scr[...] = jnp.zeros_like(dq_scr)
            for hh in range(FLASH_HP):
                od = o_ref[:, MV * hh:MV * hh + MV].astype(F32) * do_ref[:, MV * hh:MV * hh + MV].astype(F32)
                delta_scr[:, 128 * hh:128 * hh + 128] = jnp.broadcast_to(jnp.sum(od, axis=-1, keepdims=True), (S, 128))

        causal = _iota((tq, tq), 0) >= _iota((tq, tq), 1)

        def step(i, carry, masked):
            rows = pl.ds(pl.multiple_of(i * tq, tq), tq)
            hs = range(FLASH_HP)
            qs = [slice(256 * hh, 256 * hh + 256) for hh in hs]
            vs = [slice(MV * hh, MV * hh + MV) for hh in hs]
            ls = [slice(128 * hh, 128 * hh + 1) for hh in hs]
            s = [_dot_nt(q_ref[rows, qs[hh]], k_ref[:, qs[hh]]) for hh in hs]
            dp = [_dot_nt(do_ref[rows, vs[hh]], v_ref[:, vs[hh]]) for hh in hs]
            pb, ds = [], []
            for hh in hs:
                p = jnp.exp2(s[hh] - lse_ref[rows, ls[hh]])
                if masked:
                    p = jnp.where(causal, p, 0.0)
                pb.append(p.astype(BF))
                ds.append((p * (dp[hh] - delta_scr[rows, ls[hh]])).astype(BF))
            dv = [carry[hh][1] + _dot_tn(pb[hh], do_ref[rows, vs[hh]]) for hh in hs]
            dk = [carry[hh][0] + _dot_tn(ds[hh], q_ref[rows, qs[hh]]) for hh in hs]
            for hh in hs:
                dq_scr[rows, qs[hh]] += _dot(ds[hh], k_ref[:, qs[hh]])
            return tuple((dk[hh], dv[hh]) for hh in hs)

        init = ((jnp.zeros((tq, 256), F32), jnp.zeros((tq, MV), F32)),) * FLASH_HP
        carry = step(j, init, True)
        carry = lax.fori_loop(j + 1, nq, lambda i, c: step(i, c, False), carry)
        for hh, (dk, dv) in enumerate(carry):
            dk_ref[:, 256 * hh:256 * hh + 256] = dk * (1.0 / LOG2E)
            dv_ref[:, MV * hh:MV * hh + MV] = dv

        @pl.when(j == nq - 1)
        def _():
            dq_ref[...] = dq_scr[...] * QK_SCALE

    hp = FLASH_HP
    seq = lambda w: pl.BlockSpec((S, w * hp), lambda b_, h, j: (b_, h))
    blk = lambda w: pl.BlockSpec((tq, w * hp), lambda b_, h, j: (b_ * nq + j, h))
    return _call(
        body, name="flash_bwd", ride=ride, sem=("parallel", "parallel", "arbitrary"), args=(qc, kc, v, o, do, lse),
        out_shape=(jax.ShapeDtypeStruct((T, MH * 256), F32), jax.ShapeDtypeStruct((T, MH * 256), F32),
                   jax.ShapeDtypeStruct((T, MH * MV), F32)),
        grid=(nseq, MH // hp, nq),
        in_specs=[seq(256), blk(256), blk(MV), seq(MV), seq(MV), seq(128)],
        out_specs=(seq(256), blk(256), blk(MV)),
        scratch_shapes=[pltpu.VMEM((S, 256 * hp), F32), pltpu.VMEM((S, 128 * hp), F32)])


def _ln_fwd(pre, g, b):
    mu = jnp.mean(pre, axis=-1, keepdims=True)
    xc = pre - mu
    rstd = lax.rsqrt(jnp.mean(xc * xc, axis=-1, keepdims=True) + LN_EPS)
    xhat = xc * rstd
    return xhat * g + b, xhat, rstd


def _ln_bwd(dy, xhat, rstd, g):
    dxh = dy * g
    dx = rstd * (dxh - jnp.mean(dxh, axis=-1, keepdims=True) - xhat * jnp.mean(dxh * xhat, axis=-1, keepdims=True))
    return dx, jnp.sum(dy * xhat, axis=0, keepdims=True), jnp.sum(dy, axis=0, keepdims=True)


def _post_attn_fwd(zg, attn, pt, x, wgo, wmo, wout, g1, b1, *, tm):
    T = x.shape[0]

    def body(zg_ref, at_ref, pt_ref, x_ref, wgo_ref, wmo_ref, wout_ref, g_ref, b_ref,
             yg_ref, ym_ref, mix_ref, pre_ref, hb_ref):
        yg = _dot(zg_ref[...], wgo_ref[...])
        ym = _dot(at_ref[...], wmo_ref[...])
        mix = (_sigmoid(pt_ref[:, 0:D].astype(F32)) * yg + _sigmoid(pt_ref[:, D:2 * D].astype(F32)) * ym).astype(BF)
        pre = ALPHA * x_ref[...] + _dot(mix, wout_ref[...])
        h, _, _ = _ln_fwd(pre, g_ref[...], b_ref[...])
        yg_ref[...] = yg.astype(BF)
        ym_ref[...] = ym.astype(BF)
        mix_ref[...] = mix
        pre_ref[...] = pre
        hb_ref[...] = h.astype(BF)

    full = lambda shp: pl.BlockSpec(shp, lambda i: (0,) * len(shp))
    row = lambda w: pl.BlockSpec((tm, w), lambda i: (i, 0))
    sd = lambda dt: jax.ShapeDtypeStruct((T, D), dt)
    return pl.pallas_call(
        body, name="post_attn_fwd",
        out_shape=(sd(BF), sd(BF), sd(BF), sd(F32), sd(BF)),
        grid=(T // tm,),
        in_specs=[row(D), row(D), row(PT_W), row(D), full((D, D)), full((D, D)), full((D, D)),
                  full((1, D)), full((1, D))],
        out_specs=(row(D),) * 5,
        compiler_params=_params(("parallel",)),
    )(zg, attn, pt, x, wgo, wmo, wout, g1, b1)


def _post_attn_bwd(dh, pre, pt, yg, ym, wgo, wmo, wout, g1, *, tm):
    T = dh.shape[0]

    def body(dh_ref, pre_ref, pt_ref, yg_ref, ym_ref, wgo_ref, wmo_ref, wout_ref, g_ref,
             dx_ref, dpreb_ref, dpt_ref, dygb_ref, dymb_ref, dzg_ref, dat_ref, dg_ref, db_ref):
        @pl.when(pl.program_id(0) == 0)
        def _():
            dg_ref[...] = jnp.zeros_like(dg_ref)
            db_ref[...] = jnp.zeros_like(db_ref)

        pre = pre_ref[...]
        mu = jnp.mean(pre, axis=-1, keepdims=True)
        xc = pre - mu
        rstd = lax.rsqrt(jnp.mean(xc * xc, axis=-1, keepdims=True) + LN_EPS)
        dpre, dg, db = _ln_bwd(dh_ref[...], xc * rstd, rstd, g_ref[...])
        dg_ref[...] += dg
        db_ref[...] += db
        dx_ref[...] = ALPHA * dpre
        dpreb = dpre.astype(BF)
        dpreb_ref[...] = dpreb
        dmix = _dot_nt(dpreb, wout_ref[...])
        sa = _sigmoid(pt_ref[:, 0:D].astype(F32))
        sb = _sigmoid(pt_ref[:, D:2 * D].astype(F32))
        dpt_ref[:, 0:D] = (dmix * yg_ref[...].astype(F32) * (sa * (1.0 - sa))).astype(BF)
        dpt_ref[:, D:2 * D] = (dmix * ym_ref[...].astype(F32) * (sb * (1.0 - sb))).astype(BF)
        dyg = (dmix * sa).astype(BF)
        dym = (dmix * sb).astype(BF)
        dygb_ref[...] = dyg
        dymb_ref[...] = dym
        dzg_ref[...] = _dot_nt(dyg, wgo_ref[...]).astype(BF)
        dat_ref[...] = _dot_nt(dym, wmo_ref[...]).astype(BF)

    full = lambda shp: pl.BlockSpec(shp, lambda i: (0,) * len(shp))
    row = lambda w: pl.BlockSpec((tm, w), lambda i: (i, 0))
    sd = lambda w, dt: jax.ShapeDtypeStruct((T, w), dt)
    return pl.pallas_call(
        body, name="post_attn_bwd",
        out_shape=(sd(D, F32), sd(D, BF), sd(PT_W, BF), sd(D, BF), sd(D, BF), sd(D, BF), sd(D, BF),
                   jax.ShapeDtypeStruct((1, D), F32), jax.ShapeDtypeStruct((1, D), F32)),
        grid=(T // tm,),
        in_specs=[row(D), row(D), row(PT_W), row(D), row(D), full((D, D)), full((D, D)), full((D, D)),
                  full((1, D))],
        out_specs=(row(D), row(D), row(PT_W), row(D), row(D), row(D), row(D), full((1, D)), full((1, D))),
        compiler_params=_params(("arbitrary",)),
    )(dh, pre, pt, yg, ym, wgo, wmo, wout, g1)


def _shift_down(u, prev, k):
    r = pltpu.roll(u, k, 0)
    p = pltpu.roll(prev, k, 0)
    head = jnp.where(_iota(p.shape, 0) < k, p, r[0:8, :])
    return jnp.concatenate([head, r[8:, :]], axis=0)


def _conv3(u, prev, w_ref, b_ref):
    return (w_ref[0:1, :] * _shift_down(u, prev, 2) + w_ref[1:2, :] * _shift_down(u, prev, 1)
            + w_ref[2:3, :] * u + b_ref[...])


def _ffn_up_fwd(hb, wug, wuv, cw, cb, *, S, tm, tn):
    T = hb.shape[0]
    nj, nbs = DFF // tn, S // tm

    def body(h_ref, wg_ref, wv_ref, cwg_ref, cwv_ref, cbg_ref, cbv_ref,
             ug_ref, uv_ref, ucg_ref, ucv_ref, f_ref, pg_scr, pv_scr):
        @pl.when(pl.program_id(1) % nbs == 0)
        def _():
            pg_scr[...] = jnp.zeros_like(pg_scr)
            pv_scr[...] = jnp.zeros_like(pv_scr)

        h = h_ref[...]
        ug = _dot(h, wg_ref[...])
        uv = _dot(h, wv_ref[...])
        ucg = _conv3(ug, pg_scr[...], cwg_ref, cbg_ref)
        ucv = _conv3(uv, pv_scr[...], cwv_ref, cbv_ref)
        pg_scr[...] = ug[tm - 8:, :]
        pv_scr[...] = uv[tm - 8:, :]
        ug_ref[...] = ug.astype(BF)
        uv_ref[...] = uv.astype(BF)
        ucg_ref[...] = ucg
        ucv_ref[...] = ucv
        f_ref[...] = (ucg * _sigmoid(ucg) * ucv).astype(BF)

    tile = pl.BlockSpec((tm, tn), lambda j, i: (i, j))
    return pl.pallas_call(
        body, name="ffn_up_fwd",
        out_shape=(jax.ShapeDtypeStruct((T, DFF), BF), jax.ShapeDtypeStruct((T, DFF), BF),
                   jax.ShapeDtypeStruct((T, DFF), F32), jax.ShapeDtypeStruct((T, DFF), F32),
                   jax.ShapeDtypeStruct((T, DFF), BF)),
        grid=(nj, T // tm),
        in_specs=[pl.BlockSpec((tm, D), lambda j, i: (i, 0)),
                  pl.BlockSpec((D, tn), lambda j, i: (0, j)), pl.BlockSpec((D, tn), lambda j, i: (0, j)),
                  pl.BlockSpec((3, tn), lambda j, i: (0, j)), pl.BlockSpec((3, tn), lambda j, i: (0, j + nj)),
                  pl.BlockSpec((1, tn), lambda j, i: (0, j)), pl.BlockSpec((1, tn), lambda j, i: (0, j + nj))],
        out_specs=(tile, tile, tile, tile, tile),
        scratch_shapes=[pltpu.VMEM((8, tn), F32), pltpu.VMEM((8, tn), F32)],
        compiler_params=_params(("parallel", "arbitrary")),
    )(hb, wug, wuv, cw, cw, cb, cb)


def _ffn_bwd(dpreb, wd, ug, uv, ucg, ucv, cw, *, S, tm, tn):
    T = dpreb.shape[0]
    nj, nb, nbs = DFF // tn, T // tm, S // tm
    r_, c_ = lax.broadcasted_iota(jnp.int32, (tm, tm), 0), lax.broadcasted_iota(jnp.int32, (tm, tm), 1)
    s1, s2 = (c_ == r_ + 1).astype(BF), (c_ == r_ + 2).astype(BF)

    def body(dp_ref, wd_ref, ug_ref, uv_ref, ucg_ref, ucv_ref, cwg_ref, cwv_ref, s1_ref, s2_ref,
             dug_ref, duv_ref, dcg_ref, dcv_ref, ng_scr, nv_scr):
        ii = pl.program_id(1)
        i = nb - 1 - ii
        tail_row = _iota((8, tn), 0)

        @pl.when(ii == 0)
        def _():
            dcg_ref[...] = jnp.zeros_like(dcg_ref)
            dcv_ref[...] = jnp.zeros_like(dcv_ref)

        @pl.when(i % nbs == nbs - 1)
        def _():
            ng_scr[...] = jnp.zeros_like(ng_scr)
            nv_scr[...] = jnp.zeros_like(nv_scr)

        df = _dot_nt(dp_ref[...], wd_ref[...])
        ucg = ucg_ref[...]
        sg = _sigmoid(ucg)
        ducg = df * ucv_ref[...] * (sg * (1.0 + ucg * (1.0 - sg)))
        ducv = df * (ucg * sg)

        def finish(duc, u_ref, w, nxt_scr, du_ref, dc_ref):
            nxt = nxt_scr[...]
            db = duc.astype(BF)

            def shifted(s_ref, k):
                r = _dot(s_ref[...], db)
                tail = jnp.where(tail_row >= 8 - k, pltpu.roll(nxt, 8 - k, 0), r[tm - 8:, :])
                return jnp.concatenate([r[:tm - 8, :], tail], axis=0)

            up1 = shifted(s1_ref, 1)
            up2 = shifted(s2_ref, 2)
            du_ref[...] = (w[2:3, :] * duc + w[1:2, :] * up1 + w[0:1, :] * up2).astype(BF)
            nxt_scr[...] = duc[0:8, :]
            u = u_ref[...].astype(F32)
            for row, z in enumerate((u * up2, u * up1, u * duc, duc)):
                dc_ref[row:row + 1, :] += jnp.sum(z, axis=0, keepdims=True)

        finish(ducg, ug_ref, cwg_ref, ng_scr, dug_ref, dcg_ref)
        finish(ducv, uv_ref, cwv_ref, nv_scr, duv_ref, dcv_ref)

    tile = pl.BlockSpec((tm, tn), lambda j, ii: (nb - 1 - ii, j))
    acc = pl.BlockSpec((8, tn), lambda j, ii: (0, j))
    return pl.pallas_call(
        body, name="ffn_bwd",
        out_shape=(jax.ShapeDtypeStruct((T, DFF), BF), jax.ShapeDtypeStruct((T, DFF), BF),
                   jax.ShapeDtypeStruct((8, DFF), F32), jax.ShapeDtypeStruct((8, DFF), F32)),
        grid=(nj, nb),
        in_specs=[pl.BlockSpec((tm, D), lambda j, ii: (nb - 1 - ii, 0)),
                  pl.BlockSpec((tn, D), lambda j, ii: (j, 0)),
                  tile, tile, tile, tile,
                  pl.BlockSpec((3, tn), lambda j, ii: (0, j)), pl.BlockSpec((3, tn), lambda j, ii: (0, j + nj)),
                  pl.BlockSpec((tm, tm), lambda j, ii: (0, 0)), pl.BlockSpec((tm, tm), lambda j, ii: (0, 0))],
        out_specs=(tile, tile, acc, acc),
        scratch_shapes=[pltpu.VMEM((8, tn), F32), pltpu.VMEM((8, tn), F32)],
        compiler_params=_params(("parallel", "arbitrary")),
    )(dpreb, wd, ug, uv, ucg, ucv, cw, cw, s1, s2)


def _down_ln2_loss(f_in, wd, pre1, target, g1, b1, g2, b2, *, tm):
    T = pre1.shape[0]

    def body(f_ref, wd_ref, p1_ref, t_ref, g1_ref, b1_ref, g_ref, b_ref, dpb_ref, dh_ref, loss_ref, dg_ref, db_ref):
        @pl.when(pl.program_id(0) == 0)
        def _():
            loss_ref[...] = jnp.zeros_like(loss_ref)
            dg_ref[...] = jnp.zeros_like(dg_ref)
            db_ref[...] = jnp.zeros_like(db_ref)

        halves = [pl.ds(s * (tm // 2), tm // 2) for s in range(2)]
        f = [_dot(f_ref[hs, :], wd_ref[...]) for hs in halves]
        for hs, fh in zip(halves, f):
            h, _, _ = _ln_fwd(p1_ref[hs, :], g1_ref[...], b1_ref[...])
            pre = ALPHA * h + fh
            out, xhat, rstd = _ln_fwd(pre, g_ref[...], b_ref[...])
            diff = out - t_ref[hs, :]
            loss_ref[...] += 0.5 * jnp.sum(jnp.mean(diff * diff, axis=-1, keepdims=True))
            dpre, dg, db = _ln_bwd(diff * (1.0 / D), xhat, rstd, g_ref[...])
            dg_ref[...] += dg
            db_ref[...] += db
            dpb_ref[hs, :] = dpre.astype(BF)
            dh_ref[hs, :] = ALPHA * dpre

    full = lambda shp: pl.BlockSpec(shp, lambda i: (0,) * len(shp))
    row = lambda w: pl.BlockSpec((tm, w), lambda i: (i, 0))
    return pl.pallas_call(
        body, name="down_ln2_loss",
        out_shape=(jax.ShapeDtypeStruct((T, D), BF), jax.ShapeDtypeStruct((T, D), F32),
                   jax.ShapeDtypeStruct((8, 128), F32), jax.ShapeDtypeStruct((1, D), F32),
                   jax.ShapeDtypeStruct((1, D), F32)),
        grid=(T // tm,),
        in_specs=[row(DFF), full((DFF, D)), row(D), row(D), full((1, D)), full((1, D)), full((1, D)), full((1, D))],
        out_specs=(row(D), row(D), full((8, 128)), full((1, D)), full((1, D))),
        compiler_params=_params(("arbitrary",)),
    )(f_in, wd, pre1, target, g1, b1, g2, b2)


def _adamw(parts, w, m, v, *, name):
    n, R, C = parts.shape
    tr, tc = R, C
    for cand in range(min(R, 256), 15, -1):
        if R % cand == 0 and cand % 16 == 0:
            tr = cand
            break
    if tr == R and R * C > 65536 and C % 256 == 0:
        tc = 256
    c1 = 1.0 - ADAM_B1 ** ADAM_STEP
    c2 = 1.0 - ADAM_B2 ** ADAM_STEP

    def body(p_ref, w_ref, m_ref, v_ref, g_ref, d_ref, nm_ref, nv_ref):
        g = p_ref[0].astype(F32)
        for s in range(1, n):
            g = g + p_ref[s].astype(F32)
        nm = ADAM_B1 * m_ref[...] + (1.0 - ADAM_B1) * g
        nv = ADAM_B2 * v_ref[...] + (1.0 - ADAM_B2) * (g * g)
        g_ref[...] = g
        nm_ref[...] = nm
        nv_ref[...] = nv
        d_ref[...] = -ADAM_LR * ((nm / c1) / (jnp.sqrt(nv / c2) + ADAM_EPS) + ADAM_WD * w_ref[...])

    blk = pl.BlockSpec((tr, tc), lambda i, j: (i, j))
    sd = jax.ShapeDtypeStruct((R, C), F32)
    return pl.pallas_call(
        body, name=name,
        out_shape=(sd, sd, sd, sd),
        grid=(R // tr, C // tc),
        in_specs=[pl.BlockSpec((n, tr, tc), lambda i, j: (0, i, j)), blk, blk, blk],
        out_specs=(blk, blk, blk, blk),
        compiler_params=_params(("parallel", "parallel")),
    )(parts, w, m, v)


class _Exchange:
    def __init__(self, items):
        self.items = [(src if sc else [(src, 0)], sc) for src, sc in items]
        self.arrays = [arr for srcs, _ in self.items for arr, _ in srcs]
        self.n = len(self.items)
        self.n_in = len(self.arrays)

    def out_shape(self):
        return tuple(jax.ShapeDtypeStruct((NDEV,) + (srcs[0][0].shape[1:] if sc else srcs[0][0].shape),
                                          srcs[0][0].dtype) for srcs, sc in self.items)

    def scratch(self):
        return [pltpu.SemaphoreType.DMA((self.n, NDEV - 1)), pltpu.SemaphoreType.DMA((self.n, NDEV - 1)),
                pltpu.SemaphoreType.DMA((self.n,))]

    def _emit(self, ins, outs, sems, phase):
        send_sems, recv_sems, loc_sems = sems
        x, y, c = lax.axis_index("x"), lax.axis_index("y"), lax.axis_index("c")
        me = 4 * x + 2 * y + c
        flip = lambda p, d: 1 - p if d else p

        def inside(p, lo, n):
            return None if (lo, n) == (0, NDEV) else jnp.logical_and(p >= lo, p < lo + n)

        def when(cond, fn):
            if cond is None:
                fn()
            else:
                pl.when(cond)(fn)

        pos = 0
        for a, (srcs, sc) in enumerate(self.items):
            refs = ins[pos:pos + len(srcs)]
            pos += len(srcs)
            ranges = [(lo, arr.shape[0]) if sc else (0, NDEV) for arr, lo in srcs]
            mine = [inside(me, lo, n) for lo, n in ranges]
            i_receive = None if None in mine else functools.reduce(jnp.logical_or, mine)
            for ref, (lo, n), cond in zip(refs, ranges, mine):
                def local(ref=ref, lo=lo):
                    cp = pltpu.make_async_copy(ref.at[me - lo] if sc else ref, outs[a].at[me], loc_sems.at[a])
                    cp.start() if phase == 0 else cp.wait()
                if phase != 1:
                    when(cond, local)
            for k in range(1, NDEV):
                px, py, pc = flip(x, k & 4), flip(y, k & 2), flip(c, k & 1)
                peer = 4 * px + 2 * py + pc
                mk = functools.partial(pltpu.make_async_remote_copy,
                                       send_sem=send_sems.at[a, k - 1], recv_sem=recv_sems.at[a, k - 1],
                                       device_id=(px, py, pc), device_id_type=MESH_ID)
                if phase == 1:
                    def arrival(mk=mk, peer=peer):
                        mk(src_ref=refs[0].at[0] if sc else refs[0], dst_ref=outs[a].at[peer]).wait_recv()
                    when(i_receive, arrival)
                    continue
                for ref, (lo, n) in zip(refs, ranges):
                    def send(mk=mk, ref=ref, lo=lo, peer=peer):
                        cp = mk(src_ref=ref.at[peer - lo] if sc else ref, dst_ref=outs[a].at[me])
                        cp.start() if phase == 0 else cp.wait_send()
                    when(inside(peer, lo, n), send)

    def start(self, ins, outs, sems):
        self._emit(ins, outs, sems, 0)

    def wait(self, ins, outs, sems):
        self._emit(ins, outs, sems, 1)
        self._emit(ins, outs, sems, 2)


def _call(body, *, name, grid, in_specs, out_specs, out_shape, args, scratch_shapes=(), sem=None, ride=None):
    if ride is None:
        return pl.pallas_call(body, name=name, grid=grid, in_specs=list(in_specs), out_specs=tuple(out_specs),
                              out_shape=tuple(out_shape), scratch_shapes=list(scratch_shapes),
                              compiler_params=_params(sem))(*args)
    n_in, n_out, n_scr, ne, ne_in = len(args), len(out_shape), len(scratch_shapes), ride.n, ride.n_in

    def ride_body(*refs):
        ins, ex_in = refs[:n_in], refs[n_in:n_in + ne_in]
        o0 = n_in + ne_in
        outs, ex_out = refs[o0:o0 + n_out], refs[o0 + n_out:o0 + n_out + ne]
        scr = refs[o0 + n_out + ne:o0 + n_out + ne + n_scr]
        sems = refs[o0 + n_out + ne + n_scr:]
        first = functools.reduce(jnp.logical_and, [pl.program_id(d) == 0 for d in range(len(grid))])
        last = functools.reduce(jnp.logical_and, [pl.program_id(d) == grid[d] - 1 for d in range(len(grid))])

        @pl.when(first)
        def _():
            ride.start(ex_in, ex_out, sems)

        body(*ins, *outs, *scr)

        @pl.when(last)
        def _():
            ride.wait(ex_in, ex_out, sems)

    anyspec = pl.BlockSpec(memory_space=pl.ANY)
    res = pl.pallas_call(
        ride_body, name=name, grid=grid,
        in_specs=list(in_specs) + [anyspec] * ne_in,
        out_specs=tuple(out_specs) + (anyspec,) * ne,
        out_shape=tuple(out_shape) + ride.out_shape(),
        scratch_shapes=list(scratch_shapes) + ride.scratch(),
        compiler_params=_params(("arbitrary",) * len(grid)),
    )(*args, *ride.arrays)
    return tuple(res[:n_out]), tuple(res[n_out:])


def _gather_two_level(arrays, *, name):
    n = len(arrays)

    def body(*refs):
        ins, outs = refs[:n], refs[n:2 * n]
        send_sems, recv_sems, loc_sems = refs[2 * n:]
        x, y, c = lax.axis_index("x"), lax.axis_index("y"), lax.axis_index("c")
        sibling = (x, y, 1 - c)
        chips = [(1 - x, y), (x, 1 - y), (1 - x, 1 - y)]
        idx = lambda px, py, pc: 4 * px + 2 * py + pc
        me = idx(x, y, c)

        def copy(a, k, block, to, src=None):
            return pltpu.make_async_remote_copy(
                src_ref=outs[a].at[block] if src is None else src, dst_ref=outs[a].at[block],
                send_sem=send_sems.at[a, k], recv_sem=recv_sems.at[a, k], device_id=to, device_id_type=MESH_ID)

        local = [pltpu.make_async_copy(ins[a], outs[a].at[me], loc_sems.at[a]) for a in range(n)]
        sent = []
        for a in range(n):
            sent.append(copy(a, 0, me, sibling, src=ins[a]))
            sent += [copy(a, 1 + j, me, (*chip, c), src=ins[a]) for j, chip in enumerate(chips)]
        for cp in local + sent:
            cp.start()
        for j, chip in enumerate(chips):
            for a in range(n):
                copy(a, 1 + j, idx(*chip, c), sibling).wait_recv()
                passed = copy(a, 4 + j, idx(*chip, c), sibling)
                passed.start()
                sent.append(passed)
        for a in range(n):
            copy(a, 0, idx(x, y, 1 - c), sibling).wait_recv()
            for j, chip in enumerate(chips):
                copy(a, 4 + j, idx(*chip, 1 - c), sibling).wait_recv()
        for cp in sent:
            cp.wait_send()
        for cp in local:
            cp.wait()

    anyspec = pl.BlockSpec(memory_space=pl.ANY)
    return pl.pallas_call(
        body, name=name,
        out_shape=tuple(jax.ShapeDtypeStruct((NDEV,) + a.shape, a.dtype) for a in arrays),
        in_specs=[anyspec] * n, out_specs=(anyspec,) * n,
        scratch_shapes=[pltpu.SemaphoreType.DMA((n, NDEV - 1)), pltpu.SemaphoreType.DMA((n, NDEV - 1)),
                        pltpu.SemaphoreType.DMA((n,))],
    )(*arrays)


def _tri_consts():
    r = lax.broadcasted_iota(jnp.int32, (GC, GC), 0)
    c = lax.broadcasted_iota(jnp.int32, (GC, GC), 1)
    return (r >= c).astype(BF), (r <= c).astype(BF)


def _local_step(x, positions, target, w, hooks=None):
    g = {}

    def run(host, fn, *a, **kw):
        h = None if hooks is None else hooks.get(host)
        if h is None:
            return fn(*a, **kw)
        out, received = fn(*a, ride=_Exchange(h[0](w, g)), **kw)
        h[1](received, w, g)
        return out

    nseq, S, _ = x.shape
    T = nseq * S
    tm = min(TOKEN_TM, S)
    tq = min(FLASH_TQ, S)
    x2 = x.reshape(T, D)
    pos = positions.reshape(T, 1)
    half = ROPE // 2
    inv = THETA ** (-jnp.arange(half, dtype=F32) / half)
    invf = jnp.concatenate([inv, inv, jnp.zeros((64,), F32)]).reshape(1, 128)
    ltri, utri = _tri_consts()

    pt, xb = _matmul(x2, w["w_tt"], "nt", name="proj_t", out_dtype=BF, tm=1024, tn=1024, tk=1024, emit_a=True)
    pg = run("proj_g", _matmul, xb, w["w_gt"], "nt", name="proj_g", tm=1024, tn=640, tk=1024)
    pm = _matmul(xb, w["w_mt"], "nt", name="proj_m", tm=1024, tn=768, tk=1024)
    o, zg, states = _gla_fwd(pg, w["wg"], w["bg"], w["gn"], ltri, nseq=nseq, S=S, tm=tm)
    qc, kc, v = _mla_prep_fwd(pm, pos, invf, w["gq"], w["gkv"], w["wuq"], w["wukv"], tm=tm)
    attn, lse = run("flash_fwd", _flash_fwd, qc, kc, v, nseq=nseq, S=S, tq=tq)
    yg, ym, mix, pre1, h1b = _post_attn_fwd(zg, attn, pt, x2, w["wgo"], w["wmo"], w["wout"],
                                            w["g1"], w["b1"], tm=tm)
    ug, uv, ucg, ucv, f_in = _ffn_up_fwd(h1b, w["wug"], w["wuv"], w["cw"], w["cb"], S=S, tm=tm, tn=FFN_TN)
    dpre2b, dh1, loss8, dg2, db2 = _down_ln2_loss(f_in, w["wd"], pre1, target.reshape(T, D), w["g1"], w["b1"],
                                                  w["g2"], w["b2"], tm=min(2 * tm, S))

    dug, duv, dcg, dcv = _ffn_bwd(dpre2b, w["wd"], ug, uv, ucg, ucv, w["cw"], S=S, tm=tm, tn=FFN_TN)
    g["g2"], g["b2"], g["loss"] = dg2, db2, loss8[0:1, 0:1]
    g["cw"] = jnp.concatenate([dcg[0:3], dcv[0:3]], axis=1)
    g["cb"] = jnp.concatenate([dcg[3:4], dcv[3:4]], axis=1)
    g["wd"] = _matmul(f_in, dpre2b, "tn", name="dw_down", out_dtype=BF, tm=1408, tn=1024, tk=1024)
    g["wugt"] = _matmul(dug, h1b, "tn", name="dw_up_g", out_dtype=BF, tm=1408, tn=1024, tk=1024)
    g["wuvt"] = _matmul(duv, h1b, "tn", name="dw_up_v", out_dtype=BF, tm=1408, tn=1024, tk=1024)
    dh1 = _matmul(dug, w["wugt"], "nn", name="dh1_g", c_in=dh1, tm=1024, tn=1024, tk=1408)
    dh1 = _matmul(duv, w["wuvt"], "nn", name="dh1_v", c_in=dh1, tm=1024, tn=1024, tk=1408)
    dx, dpre1b, dpt, dygb, dymb, dzg, dattn, dg1, db1 = _post_attn_bwd(
        dh1, pre1, pt, yg, ym, w["wgo"], w["wmo"], w["wout"], w["g1"], tm=tm)
    g["g1"], g["b1"] = dg1, db1
    g["wout"] = _matmul(mix, dpre1b, "tn", name="dw_out", out_dtype=BF, tm=1024, tn=1024, tk=1024)
    g["wgo"] = _matmul(zg, dygb, "tn", name="dw_gla_o", out_dtype=BF, tm=1024, tn=1024, tk=1024)
    g["wmo"] = _matmul(attn, dymb, "tn", name="dw_mla_o", out_dtype=BF, tm=1024, tn=1024, tk=1024)
    dqc, dkc, dv = run("flash_bwd", _flash_bwd, qc, kc, v, attn, dattn, lse, nseq=nseq, S=S, tq=tq)
    dpm, g["wuq"], g["wukv"], g["gq"], g["gkv"] = _mla_prep_bwd(
        pm, pos, invf, w["gq"], w["gkv"], w["wuq"], w["wukv"], dqc, dkc, dv, tm=tm)
    g["w_mt"] = _matmul(dpm, xb, "tn", name="dw_in_m", out_dtype=BF, tm=768, tn=1024, tk=1024)
    g["w_tt"] = _matmul(dpt, xb, "tn", name="dw_in_t", out_dtype=BF, tm=1024, tn=1024, tk=1024)
    dpg, g["wg"], g["bg"], g["gn"] = run("gla_bwd", _gla_bwd, pg, w["wg"], w["bg"], w["gn"], ltri, utri, o, states,
                                         dzg, nseq=nseq, S=S, tm=tm)
    g["w_gt"] = _matmul(dpg, xb, "tn", name="dw_in_g", out_dtype=BF, tm=640, tn=1024, tk=1024)
    dx = run("dx", _matmul_sum, dx, [(dpg, w["w_gt"], 640), (dpm, w["w_mt"], 768)], name="dx_gm")
    dx = _matmul_sum(dx, [(dpt, w["w_tt"], 1024)], name="dx_t")
    return loss8[0, 0], dx.reshape(nseq, S, D), g


_IN_SPLITS = (512, 512, 1024, 16, 1024, 384, 256, 64, 1024, 1024)


def _w_in_to_groups(wt):
    offs = [0]
    for s in _IN_SPLITS:
        offs.append(offs[-1] + s)
    q, k, v, r, og, cq, ckv, kr, ga, gb = [wt[offs[i]:offs[i + 1]] for i in range(10)]
    z = lambda n: jnp.zeros((n, wt.shape[1]), wt.dtype)
    return (jnp.concatenate([q, k, v, og, r, z(112)], axis=0),
            jnp.concatenate([cq, kr, z(64), ckv], axis=0),
            jnp.concatenate([ga, gb], axis=0))


W_IN_BLOCK = sum(_IN_SPLITS) // NDEV
_KV_LATENT_ROW = sum(_IN_SPLITS[:6])
_W_IN_LO = 5
_W_IN_SPLIT = _W_IN_LO * W_IN_BLOCK - _KV_LATENT_ROW


def _w_in_rows_lo(g_g, g_m):
    q, k, v, og, r = g_g[0:512], g_g[512:1024], g_g[1024:2048], g_g[2048:3072], g_g[3072:3088]
    return jnp.concatenate([q, k, v, r, og, g_m[0:384], g_m[512:768]], axis=0)[:_W_IN_LO * W_IN_BLOCK]


def _w_in_rows_hi(g_m, g_t):
    return jnp.concatenate([g_m[512:768], g_m[384:448], g_t], axis=0)[_W_IN_SPLIT:]


def _uq_to_kernel(wuq):
    w3 = wuq.reshape(MQR, MH, NOPE + ROPE)
    rope = jnp.concatenate([w3[:, :, NOPE:], jnp.zeros((MQR, MH, 64), wuq.dtype)], axis=2)
    return jnp.concatenate([w3[:, :, :NOPE].reshape(MQR, MH * 128), rope.reshape(MQR, MH * 128)], axis=1)


def _uq_from_kernel(g):
    nope = g[:, :1024].reshape(MQR, MH, 128)
    rope = g[:, 1024:].reshape(MQR, MH, 128)[:, :, :ROPE]
    return jnp.concatenate([nope, rope], axis=2)


def _ukv_to_kernel(wukv):
    w3 = wukv.reshape(MKR, MH, NOPE + MV)
    return jnp.concatenate([w3[:, :, :NOPE].reshape(MKR, MH * 128), w3[:, :, NOPE:].reshape(MKR, MH * 128)], axis=1)


def _ukv_from_kernel(g):
    return jnp.concatenate([g[:, :1024].reshape(MKR, MH, 128), g[:, 1024:].reshape(MKR, MH, 128)], axis=2)


def _cols_gathered(a):
    return a.transpose(1, 0, 2).reshape(a.shape[1], NDEV * a.shape[2])


def _cols_scattered(a):
    R = a.shape[0]
    return a.reshape(R, NDEV, a.shape[1] // NDEV).transpose(1, 0, 2)


_SMALL = (("gla_b_gate", 512), ("gla_norm_g", 256), ("mla_q_norm_g", 384), ("mla_kv_norm_g", 256),
          ("ln1_g", 1024), ("ln1_b", 1024), ("conv_b", 5632), ("ln2_g", 1024), ("ln2_b", 1024))
_SMALL_ROWS = 88
_SMALL_USED = sum(sz for _, sz in _SMALL)


def _pack_small(d):
    flat = jnp.concatenate([d[n].reshape(-1) for n, _ in _SMALL] + ([d['loss'].reshape(-1)] if 'loss' in d else []))
    return jnp.pad(flat, (0, _SMALL_ROWS * 128 - flat.shape[0])).reshape(_SMALL_ROWS, 128)


def _unpack_small(a):
    flat = a.reshape(-1)
    out, off = {}, 0
    for n, sz in _SMALL:
        out[n] = flat[off:off + sz].reshape(1, sz)
        off += sz
    return out


_NAMES = ['w_in', 'gla_w_gate_up', 'gla_b_gate', 'gla_norm_g', 'w_gla_o', 'mla_q_norm_g', 'mla_w_uq',
          'mla_kv_norm_g', 'mla_w_ukv', 'w_mla_o', 'w_out', 'ln1_g', 'ln1_b', 'w_up', 'conv_w', 'conv_b',
          'w_down', 'ln2_g', 'ln2_b']
_SHARDED = ['w_in', 'w_up', 'w_down', 'w_gla_o', 'w_mla_o', 'w_out', 'mla_w_uq', 'mla_w_ukv', 'gla_w_gate_up',
            'conv_w']


def kernel(x, positions, w_in, gla_w_gate_up, gla_b_gate, gla_norm_g, w_gla_o, mla_q_norm_g, mla_w_uq, mla_kv_norm_g, mla_w_ukv, w_mla_o, w_out, ln1_g, ln1_b, w_up, conv_w, conv_b, w_down, ln2_g, ln2_b, loss_target, m_w_in, m_gla_w_gate_up, m_gla_b_gate, m_gla_norm_g, m_w_gla_o, m_mla_q_norm_g, m_mla_w_uq, m_mla_kv_norm_g, m_mla_w_ukv, m_w_mla_o, m_w_out, m_ln1_g, m_ln1_b, m_w_up, m_conv_w, m_conv_b, m_w_down, m_ln2_g, m_ln2_b, v_w_in, v_gla_w_gate_up, v_gla_b_gate, v_gla_norm_g, v_w_gla_o, v_mla_q_norm_g, v_mla_w_uq, v_mla_kv_norm_g, v_mla_w_ukv, v_w_mla_o, v_w_out, v_ln1_g, v_ln1_b, v_w_up, v_conv_w, v_conv_b, v_w_down, v_ln2_g, v_ln2_b):
    W = dict(w_in=w_in, gla_w_gate_up=gla_w_gate_up, gla_b_gate=gla_b_gate, gla_norm_g=gla_norm_g, w_gla_o=w_gla_o, mla_q_norm_g=mla_q_norm_g, mla_w_uq=mla_w_uq, mla_kv_norm_g=mla_kv_norm_g, mla_w_ukv=mla_w_ukv, w_mla_o=w_mla_o, w_out=w_out, ln1_g=ln1_g, ln1_b=ln1_b, w_up=w_up, conv_w=conv_w, conv_b=conv_b, w_down=w_down, ln2_g=ln2_g, ln2_b=ln2_b)
    M = dict(w_in=m_w_in, gla_w_gate_up=m_gla_w_gate_up, gla_b_gate=m_gla_b_gate, gla_norm_g=m_gla_norm_g, w_gla_o=m_w_gla_o, mla_q_norm_g=m_mla_q_norm_g, mla_w_uq=m_mla_w_uq, mla_kv_norm_g=m_mla_kv_norm_g, mla_w_ukv=m_mla_w_ukv, w_mla_o=m_w_mla_o, w_out=m_w_out, ln1_g=m_ln1_g, ln1_b=m_ln1_b, w_up=m_w_up, conv_w=m_conv_w, conv_b=m_conv_b, w_down=m_w_down, ln2_g=m_ln2_g, ln2_b=m_ln2_b)
    V = dict(w_in=v_w_in, gla_w_gate_up=v_gla_w_gate_up, gla_b_gate=v_gla_b_gate, gla_norm_g=v_gla_norm_g, w_gla_o=v_w_gla_o, mla_q_norm_g=v_mla_q_norm_g, mla_w_uq=v_mla_w_uq, mla_kv_norm_g=v_mla_kv_norm_g, mla_w_ukv=v_mla_w_ukv, w_mla_o=v_w_mla_o, w_out=v_w_out, ln1_g=v_ln1_g, ln1_b=v_ln1_b, w_up=v_w_up, conv_w=v_conv_w, conv_b=v_conv_b, w_down=v_w_down, ln2_g=v_ln2_g, ln2_b=v_ln2_b)

    tshard = lambda d, n: d[n][0].T
    shard = lambda n: (W[n][0].astype(BF), False)
    (w_in_t,) = _gather_two_level([tshard(W, 'w_in').astype(BF)], name="gather_w0")
    w_gt, w_mt, w_tt = _w_in_to_groups(w_in_t.reshape(NDEV * W_IN_BLOCK, D))
    kw = dict(
        w_gt=w_gt, w_mt=w_mt, w_tt=w_tt, bg=W['gla_b_gate'],
        gn=W['gla_norm_g'], gq=W['mla_q_norm_g'], gkv=W['mla_kv_norm_g'],
        g1=W['ln1_g'], b1=W['ln1_b'], g2=W['ln2_g'], b2=W['ln2_b'], cb=W['conv_b'],
    )
    received = {}

    def got_out_proj(ex, w, g):
        w.update(wgo=ex[0].reshape(D, D), wmo=ex[1].reshape(D, D), wout=ex[2].reshape(D, D),
                 wuq=_uq_to_kernel(_cols_gathered(ex[3])), wukv=_ukv_to_kernel(_cols_gathered(ex[4])),
                 wg=jnp.pad(_cols_gathered(ex[5]), ((0, 128 - GR), (0, 0))))

    def got_ffn(ex, w, g):
        w_upt = ex[0].reshape(2 * DFF, D)
        w.update(wugt=w_upt[:DFF], wuvt=w_upt[DFF:], wug=w_upt[:DFF].T, wuv=w_upt[DFF:].T,
                 wd=ex[1].reshape(DFF, D), cw=_cols_gathered(ex[2]))

    slab = lambda a, lo=0: ([(a.astype(BF), lo)], True)
    rows = lambda a, n=NDEV: a.reshape(n, a.shape[0] // n, a.shape[1])

    def keep(names):
        return lambda ex, w, g: received.update(zip(names, ex))

    def small_grads(g):
        return _pack_small(dict(gla_b_gate=g['bg'], gla_norm_g=g['gn'], mla_q_norm_g=g['gq'], mla_kv_norm_g=g['gkv'],
                                ln1_g=g['g1'], ln1_b=g['b1'], conv_b=g['cb'], ln2_g=g['g2'], ln2_b=g['b2'],
                                loss=g['loss']))

    hooks = {
        "proj_g": (lambda w, g: [shard('w_gla_o'), shard('w_mla_o'), shard('w_out'), shard('mla_w_uq'),
                                 shard('mla_w_ukv'), shard('gla_w_gate_up')], got_out_proj),
        "flash_fwd": (lambda w, g: [(tshard(W, 'w_up').astype(BF), False), shard('w_down'), (W['conv_w'][0], False)],
                      got_ffn),
        "flash_bwd": (lambda w, g: [slab(rows(g['wd'])),
                                    ([(rows(g['wugt'], 4), 0), (rows(g['wuvt'], 4), 4)], True),
                                    slab(rows(g['wout'])), slab(rows(g['wgo'])), slab(rows(g['wmo']))],
                      keep(['w_down', 'w_up', 'w_out', 'w_gla_o', 'w_mla_o'])),
        "gla_bwd": (lambda w, g: [slab(_uq_from_kernel(g['wuq']).transpose(1, 0, 2)),
                                  slab(_ukv_from_kernel(g['wukv']).transpose(1, 0, 2)),
                                  slab(rows(_w_in_rows_hi(g['w_mt'], g['w_tt']), NDEV - _W_IN_LO), _W_IN_LO)],
                    keep(['mla_w_uq', 'mla_w_ukv', 'w_in_hi'])),
        "dx": (lambda w, g: [slab(rows(_w_in_rows_lo(g['w_gt'], g['w_mt']), _W_IN_LO)),
                             ([(_cols_scattered(g['wg'][:GR]), 0)], True), ([(_cols_scattered(g['cw']), 0)], True),
                             (small_grads(g), False)],
               keep(['w_in_lo', 'gla_w_gate_up', 'conv_w', 'small'])),
    }

    _, grad_x, _ = _local_step(x, positions, loss_target, kw, hooks)

    grads, deltas, new_m, new_v = {}, {}, {}, {}
    small_parts = received['small']
    loss = jnp.sum(small_parts.reshape(NDEV, -1)[:, _SMALL_USED])
    me = 4 * lax.axis_index("x") + 2 * lax.axis_index("y") + lax.axis_index("c")
    received['w_in'] = jnp.where(me >= _W_IN_LO, received['w_in_hi'], received['w_in_lo'])
    for n in _SHARDED:
        shp = W[n].shape
        if n in ('w_in', 'w_up'):
            out = _adamw(received[n], tshard(W, n), tshard(M, n), tshard(V, n), name="adamw_" + n)
            grads[n], deltas[n], new_m[n], new_v[n] = [t.T.reshape(shp) for t in out]
            continue
        out = _adamw(received[n], W[n][0], M[n][0], V[n][0], name="adamw_" + n)
        grads[n], deltas[n], new_m[n], new_v[n] = [t.reshape(shp) for t in out]
    out = _adamw(small_parts, _pack_small(W), _pack_small(M), _pack_small(V), name="adamw_small")
    for dst, packed in zip((grads, deltas, new_m, new_v), out):
        dst.update(_unpack_small(packed))

    return (loss, grad_x, *[grads[n] for n in _NAMES], *[deltas[n] for n in _NAMES],
            *[new_m[n] for n in _NAMES], *[new_v[n] for n in _NAMES])
```

```python
import functools

import jax
import jax.numpy as jnp
from jax import lax
from jax.experimental import pallas as pl
from jax.experimental.pallas import tpu as pltpu

F32 = jnp.float32
BF = jnp.bfloat16

D = 1024
GH, GDK, GDV, GR, GTAU, GC = 4, 128, 256, 16, 16.0, 64
MH, MQR, MKR, NOPE, ROPE, MV = 8, 384, 256, 128, 64, 128
THETA = 10000.0
DFF = 2816
ALPHA = 2.0 ** 0.25
LN_EPS = 1e-5
RMS_EPS = 1e-6
NDEV = 8
ADAM_LR, ADAM_B1, ADAM_B2, ADAM_EPS, ADAM_WD, ADAM_STEP = 0.001, 0.9, 0.999, 1e-08, 0.01, 10

PG_W = 3200
PM_W = 768
PT_W = 2048
NEG = -1e30
MESH_ID = pl.DeviceIdType.MESH
VMEM_MB = 1024 * 1024


V7X_VMEM_LIMIT_MB = 48
TOKEN_TM = 256
FLASH_TQ = 512
FFN_TN = 1408


def _params(sem):
    return pltpu.CompilerParams(dimension_semantics=sem, vmem_limit_bytes=V7X_VMEM_LIMIT_MB * VMEM_MB)


def _dot(a, b):
    return lax.dot_general(a, b, (((1,), (0,)), ((), ())), preferred_element_type=F32)


def _dot_nt(a, b):
    return lax.dot_general(a, b, (((1,), (1,)), ((), ())), preferred_element_type=F32)


def _dot_tn(a, b):
    return lax.dot_general(a, b, (((0,), (0,)), ((), ())), preferred_element_type=F32)


def _iota(shape, dim):
    return lax.broadcasted_iota(jnp.int32, shape, dim)


FLASH_HP = 2
FLASH_HP_FWD = 4
QK_SCALE = (NOPE + ROPE) ** -0.5
LOG2E = 1.4426950408889634
QK_SCALE_LOG2 = QK_SCALE * LOG2E


def _sigmoid(x):
    return 0.5 * jnp.tanh(0.5 * x) + 0.5


def _tri_mm(tri_bf, x):
    hi = x.astype(BF)
    r1 = x - hi.astype(F32)
    mid = r1.astype(BF)
    lo = (r1 - mid.astype(F32)).astype(BF)
    return _dot(tri_bf, hi) + _dot(tri_bf, mid) + _dot(tri_bf, lo)


def _matmul(a, b, mode, *, name, c_in=None, out_dtype=F32, tm=512, tn=512, tk=512, ride=None, emit_a=False):
    if mode == "nn":
        (M, K), (_, N) = a.shape, b.shape
    elif mode == "nt":
        (M, K), (N, _) = a.shape, b.shape
    else:
        (K, M), (_, N) = a.shape, b.shape
    tm, tn, tk = min(tm, M), min(tn, N), min(tk, K)
    assert M % tm == 0 and N % tn == 0 and K % tk == 0, (name, M, N, K, tm, tn, tk)
    nk = K // tk
    assert not emit_a or (nk == 1 and mode != "tn" and c_in is None and ride is None)
    dot = {"nn": _dot, "nt": _dot_nt, "tn": _dot_tn}[mode]

    def body(*refs):
        if emit_a:
            a_ref, b_ref, o_ref, xa_ref, acc_ref = refs
        elif c_in is None:
            a_ref, b_ref, o_ref, acc_ref = refs
        else:
            a_ref, b_ref, c_ref, o_ref, acc_ref = refs
        k = pl.program_id(2)

        @pl.when(k == 0)
        def _():
            if c_in is None:
                acc_ref[...] = jnp.zeros_like(acc_ref)
            else:
                acc_ref[...] = c_ref[...].astype(F32)

        if emit_a:
            @pl.when(pl.program_id(1) == 0)
            def _():
                xa_ref[...] = a_ref[...].astype(BF)

        acc_ref[...] += dot(a_ref[...].astype(BF), b_ref[...].astype(BF))

        @pl.when(k == nk - 1)
        def _():
            o_ref[...] = acc_ref[...].astype(out_dtype)

    if mode == "tn":
        a_spec = pl.BlockSpec((tk, tm), lambda i, j, k: (k, i))
    else:
        a_spec = pl.BlockSpec((tm, tk), lambda i, j, k: (i, k))
    if mode == "nt":
        b_spec = pl.BlockSpec((tn, tk), lambda i, j, k: (j, k))
    else:
        b_spec = pl.BlockSpec((tk, tn), lambda i, j, k: (k, j))
    in_specs = [a_spec, b_spec]
    args = [a, b]
    if c_in is not None:
        in_specs.append(pl.BlockSpec((tm, tn), lambda i, j, k: (i, j)))
        args.append(c_in)
    out_shape = (jax.ShapeDtypeStruct((M, N), out_dtype),)
    out_specs = (pl.BlockSpec((tm, tn), lambda i, j, k: (i, j)),)
    if emit_a:
        out_shape += (jax.ShapeDtypeStruct((M, K), BF),)
        out_specs += (pl.BlockSpec((tm, tk), lambda i, j, k: (i, k)),)
    res = _call(
        body, name=name, out_shape=out_shape, grid=(M // tm, N // tn, nk), in_specs=in_specs, out_specs=out_specs,
        scratch_shapes=[pltpu.VMEM((tm, tn), F32)],
        sem=("parallel", "arbitrary", "arbitrary"), args=args, ride=ride)
    if emit_a:
        return res[0], res[1]
    return res[0] if ride is None else (res[0][0], res[1])


def _matmul_sum(c_in, parts, *, name, tm=1024, ride=None):
    M, N = c_in.shape
    tm = min(tm, M)
    n_p = len(parts)
    counts = [a.shape[1] // tk for a, _, tk in parts]
    starts = [sum(counts[:p]) for p in range(n_p)]
    nk = sum(counts)

    def body(*refs):
        a_refs, w_refs = refs[:n_p], refs[n_p:2 * n_p]
        c_ref, o_ref, acc_ref = refs[2 * n_p:]
        k = pl.program_id(1)

        @pl.when(k == 0)
        def _():
            acc_ref[...] = c_ref[...]

        for p in range(n_p):
            @pl.when(jnp.logical_and(k >= starts[p], k < starts[p] + counts[p]))
            def _(p=p):
                acc_ref[...] += _dot(a_refs[p][...].astype(BF), w_refs[p][...].astype(BF))

        @pl.when(k == nk - 1)
        def _():
            o_ref[...] = acc_ref[...]

    def kidx(p):
        return lambda k: jnp.clip(k - starts[p], 0, counts[p] - 1)

    in_specs = [pl.BlockSpec((tm, tk), lambda i, k, f=kidx(p): (i, f(k))) for p, (_, _, tk) in enumerate(parts)]
    in_specs += [pl.BlockSpec((tk, N), lambda i, k, f=kidx(p): (f(k), 0)) for p, (_, _, tk) in enumerate(parts)]
    in_specs.append(pl.BlockSpec((tm, N), lambda i, k: (i, 0)))
    res = _call(
        body, name=name, out_shape=(jax.ShapeDtypeStruct((M, N), F32),), grid=(M // tm, nk),
        in_specs=in_specs, out_specs=(pl.BlockSpec((tm, N), lambda i, k: (i, 0)),),
        scratch_shapes=[pltpu.VMEM((tm, N), F32)], sem=("parallel", "arbitrary"),
        args=[a for a, _, _ in parts] + [w for _, w, _ in parts] + [c_in], ride=ride)
    return res[0] if ride is None else (res[0][0], res[1])


def _gla_gate(pg_ref, rows, wg_ref, bg_ref):
    r = pg_ref[rows, 3072:3200].astype(BF)
    logit = _dot(r, wg_ref[...]) + bg_ref[...]
    la = (jnp.minimum(logit, 0.0) - jnp.log(1.0 + jnp.exp(-jnp.abs(logit)))) * (1.0 / GTAU)
    return r, logit, la


def _gla_fwd(pg, wg, bg, gn, ltri, *, nseq, S, tm, ride=None):
    T = pg.shape[0]
    nb, nc = S // tm, tm // GC
    qscale = GDK ** -0.5

    def body(pg_ref, wg_ref, bg_ref, gn_ref, l_ref, o_ref, zg_ref, st_ref, st_scr):
        @pl.when(pl.program_id(1) == 0)
        def _():
            st_scr[...] = jnp.zeros_like(st_scr)

        ltri_v = l_ref[...]
        causal = _iota((GC, GC), 0) >= _iota((GC, GC), 1)
        last_row = _iota((GC, GDK), 0) == GC - 1
        g = gn_ref[...]

        def chunk(c, carry):
            rows = pl.ds(pl.multiple_of(c * GC, GC), GC)
            _, _, la = _gla_gate(pg_ref, rows, wg_ref, bg_ref)
            b = _tri_mm(ltri_v, la)
            hs = range(GH)
            v, q_in, k_st, dec, st, a_raw, o_st, kv = [], [], [], [], [], [], [], []
            for h in hs:
                q = pg_ref[rows, h * GDK:(h + 1) * GDK]
                k = pg_ref[rows, 512 + h * GDK:512 + (h + 1) * GDK]
                v.append(pg_ref[rows, 1024 + h * GDV:1024 + (h + 1) * GDV].astype(BF))
                bh = b[:, h * GDK:(h + 1) * GDK]
                bl = jnp.sum(jnp.where(last_row, bh, 0.0), axis=0, keepdims=True)
                q_in.append((q * (qscale * jnp.exp(bh))).astype(BF))
                k_in = (k * jnp.exp(-bh)).astype(BF)
                k_st.append((k * jnp.exp(bl - bh)).astype(BF))
                dec.append(jnp.exp(bl))
                st.append(st_scr[h])
                st_ref[c, h] = st[h]
                a_raw.append(_dot_nt(q_in[h], k_in))
            for h in hs:
                o_st.append(_dot_nt(q_in[h], st[h].astype(BF)))
                kv.append(_dot_tn(v[h], k_st[h]))
            att = [jnp.where(causal, a_raw[h], 0.0).astype(BF) for h in hs]
            o = [_dot(att[h], v[h]) + o_st[h] for h in hs]
            for h in hs:
                st_scr[h] = st[h] * dec[h] + kv[h]
                og = pg_ref[rows, 2048 + h * GDV:2048 + (h + 1) * GDV]
                rstd = lax.rsqrt(jnp.mean(o[h] * o[h], axis=-1, keepdims=True) + RMS_EPS)
                o_ref[rows, h * GDV:(h + 1) * GDV] = o[h]
                zg_ref[rows, h * GDV:(h + 1) * GDV] = (o[h] * rstd * g * (og * _sigmoid(og))).astype(BF)
            return carry

        lax.fori_loop(0, nc, chunk, 0, unroll=True)

    full = lambda shp: pl.BlockSpec(shp, lambda b_, i: (0,) * len(shp))
    return _call(
        body, name="gla_fwd", ride=ride, sem=("parallel", "arbitrary"), args=(pg, wg, bg, gn, ltri),
        out_shape=(jax.ShapeDtypeStruct((T, GH * GDV), F32),
                   jax.ShapeDtypeStruct((T, GH * GDV), BF),
                   jax.ShapeDtypeStruct((T // GC, GH, GDV, GDK), F32)),
        grid=(nseq, nb),
        in_specs=[pl.BlockSpec((tm, PG_W), lambda b_, i: (b_ * nb + i, 0)),
                  full((128, 512)), full((1, 512)), full((1, GDV)), full((GC, GC))],
        out_specs=(pl.BlockSpec((tm, GH * GDV), lambda b_, i: (b_ * nb + i, 0)),
                   pl.BlockSpec((tm, GH * GDV), lambda b_, i: (b_ * nb + i, 0)),
                   pl.BlockSpec((nc, GH, GDV, GDK), lambda b_, i: (b_ * nb + i, 0, 0, 0))),
        scratch_shapes=[pltpu.VMEM((GH, GDV, GDK), F32)])


def _gla_bwd(pg, wg, bg, gn, ltri, utri, o, states, dzg, *, nseq, S, tm, ride=None):
    T = pg.shape[0]
    nb, nc = S // tm, tm // GC
    qscale = GDK ** -0.5

    def body(pg_ref, wg_ref, bg_ref, gn_ref, l_ref, u_ref, o_ref, st_ref, dzg_ref,
             dpg_ref, dwg_ref, dbg_ref, dgn_ref, dst_scr):
        first = jnp.logical_and(pl.program_id(0) == 0, pl.program_id(1) == 0)

        @pl.when(first)
        def _():
            dwg_ref[...] = jnp.zeros_like(dwg_ref)
            dbg_ref[...] = jnp.zeros_like(dbg_ref)
            dgn_ref[...] = jnp.zeros_like(dgn_ref)

        @pl.when(pl.program_id(1) == 0)
        def _():
            dst_scr[...] = jnp.zeros_like(dst_scr)

        ltri_v = l_ref[...]
        utri_v = u_ref[...]
        causal = _iota((GC, GC), 0) >= _iota((GC, GC), 1)
        last_row = _iota((GC, GDK), 0) == GC - 1
        g = gn_ref[...]

        def chunk(cc, carry):
            c = nc - 1 - cc
            rows = pl.ds(pl.multiple_of(c * GC, GC), GC)
            r, logit, la = _gla_gate(pg_ref, rows, wg_ref, bg_ref)
            b = _tri_mm(ltri_v, la)
            hs = range(GH)
            L = lambda: [None] * GH
            vb, eb, enb, ek, dec, q_in, k_in, k_st, q_inb, k_inb, st, dst, dob = (L() for _ in range(13))
            a_raw, da_raw, dq_st, dks, dv_st, dst_new, dbs, dgn = (L() for _ in range(8))
            for h in hs:
                q = pg_ref[rows, h * GDK:(h + 1) * GDK]
                k = pg_ref[rows, 512 + h * GDK:512 + (h + 1) * GDK]
                vb[h] = pg_ref[rows, 1024 + h * GDV:1024 + (h + 1) * GDV].astype(BF)
                og = pg_ref[rows, 2048 + h * GDV:2048 + (h + 1) * GDV]
                oh = o_ref[rows, h * GDV:(h + 1) * GDV]
                dz = dzg_ref[rows, h * GDV:(h + 1) * GDV].astype(F32)
                bh = b[:, h * GDK:(h + 1) * GDK]
                bl = jnp.sum(jnp.where(last_row, bh, 0.0), axis=0, keepdims=True)
                eb[h] = qscale * jnp.exp(bh)
                enb[h] = jnp.exp(-bh)
                ek[h] = jnp.exp(bl - bh)
                dec[h] = jnp.exp(bl)
                q_in[h], k_in[h], k_st[h] = q * eb[h], k * enb[h], k * ek[h]
                q_inb[h], k_inb[h] = q_in[h].astype(BF), k_in[h].astype(BF)
                st[h] = st_ref[c, h]
                dst[h] = dst_scr[h]
                rstd = lax.rsqrt(jnp.mean(oh * oh, axis=-1, keepdims=True) + RMS_EPS)
                ohat = oh * rstd
                sg = _sigmoid(og)
                don = dz * (og * sg)
                dpg_ref[rows, 2048 + h * GDV:2048 + (h + 1) * GDV] = (
                    dz * (ohat * g) * (sg * (1.0 + og * (1.0 - sg)))).astype(BF)
                dgn[h] = jnp.sum(don * ohat, axis=0, keepdims=True)
                gd = don * g
                dob[h] = (rstd * (gd - ohat * jnp.mean(gd * ohat, axis=-1, keepdims=True))).astype(BF)
                a_raw[h] = _dot_nt(q_inb[h], k_inb[h])
                da_raw[h] = _dot_nt(dob[h], vb[h])
            dgn_ref[...] += dgn[0] + dgn[1] + dgn[2] + dgn[3]
            for h in hs:
                dstb = dst[h].astype(BF)
                dq_st[h] = _dot(dob[h], st[h].astype(BF))
                dks[h] = _dot(vb[h], dstb)
                dv_st[h] = _dot_nt(k_st[h].astype(BF), dstb)
                dst_new[h] = _dot_tn(dob[h], q_inb[h])
            att = [jnp.where(causal, a_raw[h], 0.0).astype(BF) for h in hs]
            da = [jnp.where(causal, da_raw[h], 0.0).astype(BF) for h in hs]
            dqi = [_dot(da[h], k_inb[h]) + dq_st[h] for h in hs]
            dki = [_dot_tn(da[h], q_inb[h]) for h in hs]
            dv = [_dot_tn(att[h], dob[h]) + dv_st[h] for h in hs]
            for h in hs:
                dd = jnp.sum(dst[h] * st[h], axis=0, keepdims=True)
                dst_scr[h] = dst[h] * dec[h] + dst_new[h]
                kk = dks[h] * k_st[h]
                dbl = jnp.sum(kk, axis=0, keepdims=True) + dd * dec[h]
                db = dqi[h] * q_in[h] - dki[h] * k_in[h] - kk
                dbs[h] = db + jnp.where(last_row, dbl, 0.0)
                dpg_ref[rows, h * GDK:(h + 1) * GDK] = (dqi[h] * eb[h]).astype(BF)
                dpg_ref[rows, 512 + h * GDK:512 + (h + 1) * GDK] = (dki[h] * enb[h] + dks[h] * ek[h]).astype(BF)
                dpg_ref[rows, 1024 + h * GDV:1024 + (h + 1) * GDV] = dv[h].astype(BF)
            dla = _tri_mm(utri_v, jnp.concatenate(dbs, axis=1))
            dlogit = dla * (1.0 / GTAU) * _sigmoid(-logit)
            dlb = dlogit.astype(BF)
            dpg_ref[rows, 3072:3200] = _dot_nt(dlb, wg_ref[...]).astype(BF)
            dwg_ref[...] += _dot_tn(r, dlb)
            dbg_ref[...] += jnp.sum(dlogit, axis=0, keepdims=True)
            return carry

        lax.fori_loop(0, nc, chunk, 0, unroll=True)

    full = lambda shp: pl.BlockSpec(shp, lambda b_, i: (0,) * len(shp))
    rev = lambda b_, i: (b_ * nb + nb - 1 - i, 0)
    return _call(
        body, name="gla_bwd", ride=ride, sem=("arbitrary", "arbitrary"),
        args=(pg, wg, bg, gn, ltri, utri, o, states, dzg),
        out_shape=(jax.ShapeDtypeStruct((T, PG_W), BF),
                   jax.ShapeDtypeStruct((128, 512), F32),
                   jax.ShapeDtypeStruct((1, 512), F32),
                   jax.ShapeDtypeStruct((1, GDV), F32)),
        grid=(nseq, nb),
        in_specs=[pl.BlockSpec((tm, PG_W), rev),
                  full((128, 512)), full((1, 512)), full((1, GDV)), full((GC, GC)), full((GC, GC)),
                  pl.BlockSpec((tm, GH * GDV), rev),
                  pl.BlockSpec((nc, GH, GDV, GDK), lambda b_, i: (b_ * nb + nb - 1 - i, 0, 0, 0)),
                  pl.BlockSpec((tm, GH * GDV), rev)],
        out_specs=(pl.BlockSpec((tm, PG_W), rev), full((128, 512)), full((1, 512)), full((1, GDV))),
        scratch_shapes=[pltpu.VMEM((GH, GDV, GDK), F32)])


def _rope_tables(pos, invf):
    ang = pos.astype(F32) * invf
    lane = _iota(ang.shape, 1)
    sin = jnp.sin(ang)
    ssin = jnp.where(lane < 32, -sin, jnp.where(lane < 64, sin, 0.0))
    return jnp.cos(ang), ssin, lane


def _rope(x, cos, ssin, lane, sign):
    rot = jnp.where(lane < 32, pltpu.roll(x, 96, 1), pltpu.roll(x, 32, 1))
    return x * cos + sign * (rot * ssin)


def _rms_fwd(x, g):
    rstd = lax.rsqrt(jnp.mean(x * x, axis=-1, keepdims=True) + RMS_EPS)
    return x * rstd * g, x * rstd, rstd


def _rms_bwd(dy, xhat, rstd, g):
    gd = dy * g
    return rstd * (gd - xhat * jnp.mean(gd * xhat, axis=-1, keepdims=True)), jnp.sum(dy * xhat, axis=0, keepdims=True)


def _mla_prep_fwd(pm, pos, invf, gq, gkv, wuq, wukv, *, tm):
    T = pm.shape[0]

    def body(pm_ref, pos_ref, invf_ref, gq_ref, gkv_ref, wuq_ref, wukv_ref, qc_ref, kc_ref, v_ref):
        cos, ssin, lane = _rope_tables(pos_ref[...], invf_ref[...])
        cq, _, _ = _rms_fwd(pm_ref[:, 0:MQR], gq_ref[...])
        ckv, _, _ = _rms_fwd(pm_ref[:, 512:768], gkv_ref[...])
        qf = _dot(cq.astype(BF), wuq_ref[...])
        kvf = _dot(ckv.astype(BF), wukv_ref[...])
        kr = _rope(pm_ref[:, 384:512], cos, ssin, lane, 1.0).astype(BF)
        for h in range(MH):
            qc_ref[:, 256 * h:256 * h + 128] = (QK_SCALE_LOG2 * qf[:, 128 * h:128 * h + 128]).astype(BF)
            qr = qf[:, 1024 + 128 * h:1024 + 128 * h + 128]
            qc_ref[:, 256 * h + 128:256 * h + 256] = (QK_SCALE_LOG2 * _rope(qr, cos, ssin, lane, 1.0)).astype(BF)
            kc_ref[:, 256 * h:256 * h + 128] = kvf[:, 128 * h:128 * h + 128].astype(BF)
            kc_ref[:, 256 * h + 128:256 * h + 256] = kr
        v_ref[...] = kvf[:, 1024:2048].astype(BF)

    full = lambda shp: pl.BlockSpec(shp, lambda i: (0,) * len(shp))
    row = lambda w: pl.BlockSpec((tm, w), lambda i: (i, 0))
    return pl.pallas_call(
        body, name="mla_prep_fwd",
        out_shape=(jax.ShapeDtypeStruct((T, MH * 256), BF), jax.ShapeDtypeStruct((T, MH * 256), BF),
                   jax.ShapeDtypeStruct((T, MH * MV), BF)),
        grid=(T // tm,),
        in_specs=[row(PM_W), row(1), full((1, 128)), full((1, MQR)), full((1, MKR)),
                  full((MQR, 2048)), full((MKR, 2048))],
        out_specs=(row(MH * 256), row(MH * 256), row(MH * MV)),
        compiler_params=_params(("parallel",)),
    )(pm, pos, invf, gq, gkv, wuq, wukv)


def _mla_prep_bwd(pm, pos, invf, gq, gkv, wuq, wukv, dqc, dkc, dv, *, tm):
    T = pm.shape[0]

    def body(pm_ref, pos_ref, invf_ref, gq_ref, gkv_ref, wuq_ref, wukv_ref, dqc_ref, dkc_ref, dv_ref,
             dpm_ref, dwuq_ref, dwukv_ref, dgq_ref, dgkv_ref):
        @pl.when(pl.program_id(0) == 0)
        def _():
            dwuq_ref[...] = jnp.zeros_like(dwuq_ref)
            dwukv_ref[...] = jnp.zeros_like(dwukv_ref)
            dgq_ref[...] = jnp.zeros_like(dgq_ref)
            dgkv_ref[...] = jnp.zeros_like(dgkv_ref)

        cos, ssin, lane = _rope_tables(pos_ref[...], invf_ref[...])
        cq, cqh, cq_rstd = _rms_fwd(pm_ref[:, 0:MQR], gq_ref[...])
        ckv, ckvh, ckv_rstd = _rms_fwd(pm_ref[:, 512:768], gkv_ref[...])
        dqn, dqr, dkn = [], [], []
        dkr = jnp.zeros((tm, 128), F32)
        for h in range(MH):
            dqn.append(dqc_ref[:, 256 * h:256 * h + 128].astype(BF))
            dqr.append(_rope(dqc_ref[:, 256 * h + 128:256 * h + 256], cos, ssin, lane, -1.0).astype(BF))
            dkn.append(dkc_ref[:, 256 * h:256 * h + 128].astype(BF))
            dkr = dkr + dkc_ref[:, 256 * h + 128:256 * h + 256]
        dqf = jnp.concatenate(dqn + dqr, axis=1)
        dkvf = jnp.concatenate(dkn + [dv_ref[...].astype(BF)], axis=1)
        dwuq_ref[...] += _dot_tn(cq.astype(BF), dqf)
        dwukv_ref[...] += _dot_tn(ckv.astype(BF), dkvf)
        dcq, dgq = _rms_bwd(_dot_nt(dqf, wuq_ref[...]), cqh, cq_rstd, gq_ref[...])
        dckv, dgkv = _rms_bwd(_dot_nt(dkvf, wukv_ref[...]), ckvh, ckv_rstd, gkv_ref[...])
        dgq_ref[...] += dgq
        dgkv_ref[...] += dgkv
        dpm_ref[:, 0:MQR] = dcq.astype(BF)
        dpm_ref[:, 384:512] = _rope(dkr, cos, ssin, lane, -1.0).astype(BF)
        dpm_ref[:, 512:768] = dckv.astype(BF)

    full = lambda shp: pl.BlockSpec(shp, lambda i: (0,) * len(shp))
    row = lambda w: pl.BlockSpec((tm, w), lambda i: (i, 0))
    return pl.pallas_call(
        body, name="mla_prep_bwd",
        out_shape=(jax.ShapeDtypeStruct((T, PM_W), BF), jax.ShapeDtypeStruct((MQR, 2048), F32),
                   jax.ShapeDtypeStruct((MKR, 2048), F32), jax.ShapeDtypeStruct((1, MQR), F32),
                   jax.ShapeDtypeStruct((1, MKR), F32)),
        grid=(T // tm,),
        in_specs=[row(PM_W), row(1), full((1, 128)), full((1, MQR)), full((1, MKR)),
                  full((MQR, 2048)), full((MKR, 2048)), row(MH * 256), row(MH * 256), row(MH * MV)],
        out_specs=(row(PM_W), full((MQR, 2048)), full((MKR, 2048)), full((1, MQR)), full((1, MKR))),
        compiler_params=_params(("arbitrary",)),
    )(pm, pos, invf, gq, gkv, wuq, wukv, dqc, dkc, dv)


def _flash_fwd(qc, kc, v, *, nseq, S, tq, ride=None):
    T = qc.shape[0]
    nq = S // tq
    hp = FLASH_HP_FWD

    def body(q_ref, k_ref, v_ref, o_ref, lse_ref):
        i = pl.program_id(2)
        causal = _iota((tq, tq), 0) >= _iota((tq, tq), 1)

        def step(j, carry, masked):
            rows = pl.ds(pl.multiple_of(j * tq, tq), tq)
            hs = range(hp)
            s = [_dot_nt(q_ref[:, 256 * hh:256 * hh + 256], k_ref[rows, 256 * hh:256 * hh + 256]) for hh in hs]
            p, stats = [], []
            for hh in hs:
                m, l, _ = carry[hh]
                sh = jnp.where(causal, s[hh], NEG) if masked else s[hh]
                m_new = jnp.maximum(m, jnp.max(sh, axis=-1, keepdims=True))
                ph = jnp.exp2(sh - m_new)
                a = jnp.exp2(m - m_new)
                stats.append((m_new, a * l + jnp.sum(ph, axis=-1, keepdims=True), a))
                p.append(ph.astype(BF))
            pv = [_dot(p[hh], v_ref[rows, MV * hh:MV * hh + MV]) for hh in hs]
            return tuple((stats[hh][0], stats[hh][1], stats[hh][2] * carry[hh][2] + pv[hh]) for hh in hs)

        init = ((jnp.full((tq, 1), NEG, F32), jnp.zeros((tq, 1), F32), jnp.zeros((tq, MV), F32)),) * hp
        carry = lax.fori_loop(0, i, lambda j, c: step(j, c, False), init)
        for hh, (m, l, acc) in enumerate(step(i, carry, True)):
            o_ref[:, MV * hh:MV * hh + MV] = (acc / l).astype(BF)
            lse_ref[:, 128 * hh:128 * hh + 128] = jnp.broadcast_to(m + jnp.log2(l), (tq, 128))

    return _call(
        body, name="flash_fwd", ride=ride, sem=("parallel", "parallel", "arbitrary"), args=(qc, kc, v),
        out_shape=(jax.ShapeDtypeStruct((T, MH * MV), BF), jax.ShapeDtypeStruct((T, MH * 128), F32)),
        grid=(nseq, MH // hp, nq),
        in_specs=[pl.BlockSpec((tq, 256 * hp), lambda b_, h, i: (b_ * nq + i, h)),
                  pl.BlockSpec((S, 256 * hp), lambda b_, h, i: (b_, h)),
                  pl.BlockSpec((S, MV * hp), lambda b_, h, i: (b_, h))],
        out_specs=(pl.BlockSpec((tq, MV * hp), lambda b_, h, i: (b_ * nq + i, h)),
                   pl.BlockSpec((tq, 128 * hp), lambda b_, h, i: (b_ * nq + i, h))))


def _flash_bwd(qc, kc, v, o, do, lse, *, nseq, S, tq, ride=None):
    T = qc.shape[0]
    nq = S // tq

    def body(q_ref, k_ref, v_ref, o_ref, do_ref, lse_ref, dq_ref, dk_ref, dv_ref, dq_scr, delta_scr):
        j = pl.program_id(2)

        @pl.when(j == 0)
        def _():
            dq_scr[...] = jnp.zeros_like(dq_scr)
            for hh in range(FLASH_HP):
                od = o_ref[:, MV * hh:MV * hh + MV].astype(F32) * do_ref[:, MV * hh:MV * hh + MV].astype(F32)
                delta_scr[:, 128 * hh:128 * hh + 128] = jnp.broadcast_to(jnp.sum(od, axis=-1, keepdims=True), (S, 128))

        causal = _iota((tq, tq), 0) >= _iota((tq, tq), 1)

        def step(i, carry, masked):
            rows = pl.ds(pl.multiple_of(i * tq, tq), tq)
            hs = range(FLASH_HP)
            qs = [slice(256 * hh, 256 * hh + 256) for hh in hs]
            vs = [slice(MV * hh, MV * hh + MV) for hh in hs]
            ls = [slice(128 * hh, 128 * hh + 1) for hh in hs]
            s = [_dot_nt(q_ref[rows, qs[hh]], k_ref[:, qs[hh]]) for hh in hs]
            dp = [_dot_nt(do_ref[rows, vs[hh]], v_ref[:, vs[hh]]) for hh in hs]
            pb, ds = [], []
            for hh in hs:
                p = jnp.exp2(s[hh] - lse_ref[rows, ls[hh]])
                if masked:
                    p = jnp.where(causal, p, 0.0)
                pb.append(p.astype(BF))
                ds.append((p * (dp[hh] - delta_scr[rows, ls[hh]])).astype(BF))
            dv = [carry[hh][1] + _dot_tn(pb[hh], do_ref[rows, vs[hh]]) for hh in hs]
            dk = [carry[hh][0] + _dot_tn(ds[hh], q_ref[rows, qs[hh]]) for hh in hs]
            for hh in hs:
                dq_scr[rows, qs[hh]] += _dot(ds[hh], k_ref[:, qs[hh]])
            return tuple((dk[hh], dv[hh]) for hh in hs)

        init = ((jnp.zeros((tq, 256), F32), jnp.zeros((tq, MV), F32)),) * FLASH_HP
        carry = step(j, init, True)
        carry = lax.fori_loop(j + 1, nq, lambda i, c: step(i, c, False), carry)
        for hh, (dk, dv) in enumerate(carry):
            dk_ref[:, 256 * hh:256 * hh + 256] = dk * (1.0 / LOG2E)
            dv_ref[:, MV * hh:MV * hh + MV] = dv

        @pl.when(j == nq - 1)
        def _():
            dq_ref[...] = dq_scr[...] * QK_SCALE

    hp = FLASH_HP
    seq = lambda w: pl.BlockSpec((S, w * hp), lambda b_, h, j: (b_, h))
    blk = lambda w: pl.BlockSpec((tq, w * hp), lambda b_, h, j: (b_ * nq + j, h))
    return _call(
        body, name="flash_bwd", ride=ride, sem=("parallel", "parallel", "arbitrary"), args=(qc, kc, v, o, do, lse),
        out_shape=(jax.ShapeDtypeStruct((T, MH * 256), F32), jax.ShapeDtypeStruct((T, MH * 256), F32),
                   jax.ShapeDtypeStruct((T, MH * MV), F32)),
        grid=(nseq, MH // hp, nq),
        in_specs=[seq(256), blk(256), blk(MV), seq(MV), seq(MV), seq(128)],
        out_specs=(seq(256), blk(256), blk(MV)),
        scratch_shapes=[pltpu.VMEM((S, 256 * hp), F32), pltpu.VMEM((S, 128 * hp), F32)])


def _ln_fwd(pre, g, b):
    mu = jnp.mean(pre, axis=-1, keepdims=True)
    xc = pre - mu
    rstd = lax.rsqrt(jnp.mean(xc * xc, axis=-1, keepdims=True) + LN_EPS)
    xhat = xc * rstd
    return xhat * g + b, xhat, rstd


def _ln_bwd(dy, xhat, rstd, g):
    dxh = dy * g
    dx = rstd * (dxh - jnp.mean(dxh, axis=-1, keepdims=True) - xhat * jnp.mean(dxh * xhat, axis=-1, keepdims=True))
    return dx, jnp.sum(dy * xhat, axis=0, keepdims=True), jnp.sum(dy, axis=0, keepdims=True)


def _post_attn_fwd(zg, attn, pt, x, wgo, wmo, wout, g1, b1, *, tm, ride=None):
    T = x.shape[0]

    def body(zg_ref, at_ref, pt_ref, x_ref, wgo_ref, wmo_ref, wout_ref, g_ref, b_ref,
             yg_ref, ym_ref, mix_ref, pre_ref, hb_ref):
        yg = _dot(zg_ref[...], wgo_ref[...])
        ym = _dot(at_ref[...], wmo_ref[...])
        mix = (_sigmoid(pt_ref[:, 0:D].astype(F32)) * yg + _sigmoid(pt_ref[:, D:2 * D].astype(F32)) * ym).astype(BF)
        pre = ALPHA * x_ref[...] + _dot(mix, wout_ref[...])
        h, _, _ = _ln_fwd(pre, g_ref[...], b_ref[...])
        yg_ref[...] = yg.astype(BF)
        ym_ref[...] = ym.astype(BF)
        mix_ref[...] = mix
        pre_ref[...] = pre
        hb_ref[...] = h.astype(BF)

    full = lambda shp: pl.BlockSpec(shp, lambda i: (0,) * len(shp))
    row = lambda w: pl.BlockSpec((tm, w), lambda i: (i, 0))
    sd = lambda dt: jax.ShapeDtypeStruct((T, D), dt)
    return _call(
        body, name="post_attn_fwd", ride=ride, sem=("parallel",), args=(zg, attn, pt, x, wgo, wmo, wout, g1, b1),
        out_shape=(sd(BF), sd(BF), sd(BF), sd(F32), sd(BF)),
        grid=(T // tm,),
        in_specs=[row(D), row(D), row(PT_W), row(D), full((D, D)), full((D, D)), full((D, D)),
                  full((1, D)), full((1, D))],
        out_specs=(row(D),) * 5)


def _post_attn_bwd(dh, pre, pt, yg, ym, wgo, wmo, wout, g1, *, tm):
    T = dh.shape[0]

    def body(dh_ref, pre_ref, pt_ref, yg_ref, ym_ref, wgo_ref, wmo_ref, wout_ref, g_ref,
             dx_ref, dpreb_ref, dpt_ref, dygb_ref, dymb_ref, dzg_ref, dat_ref, dg_ref, db_ref):
        @pl.when(pl.program_id(0) == 0)
        def _():
            dg_ref[...] = jnp.zeros_like(dg_ref)
            db_ref[...] = jnp.zeros_like(db_ref)

        pre = pre_ref[...]
        mu = jnp.mean(pre, axis=-1, keepdims=True)
        xc = pre - mu
        rstd = lax.rsqrt(jnp.mean(xc * xc, axis=-1, keepdims=True) + LN_EPS)
        dpre, dg, db = _ln_bwd(dh_ref[...], xc * rstd, rstd, g_ref[...])
        dg_ref[...] += dg
        db_ref[...] += db
        dx_ref[...] = ALPHA * dpre
        dpreb = dpre.astype(BF)
        dpreb_ref[...] = dpreb
        dmix = _dot_nt(dpreb, wout_ref[...])
        sa = _sigmoid(pt_ref[:, 0:D].astype(F32))
        sb = _sigmoid(pt_ref[:, D:2 * D].astype(F32))
        dpt_ref[:, 0:D] = (dmix * yg_ref[...].astype(F32) * (sa * (1.0 - sa))).astype(BF)
        dpt_ref[:, D:2 * D] = (dmix * ym_ref[...].astype(F32) * (sb * (1.0 - sb))).astype(BF)
        dyg = (dmix * sa).astype(BF)
        dym = (dmix * sb).astype(BF)
        dygb_ref[...] = dyg
        dymb_ref[...] = dym
        dzg_ref[...] = _dot_nt(dyg, wgo_ref[...]).astype(BF)
        dat_ref[...] = _dot_nt(dym, wmo_ref[...]).astype(BF)

    full = lambda shp: pl.BlockSpec(shp, lambda i: (0,) * len(shp))
    row = lambda w: pl.BlockSpec((tm, w), lambda i: (i, 0))
    sd = lambda w, dt: jax.ShapeDtypeStruct((T, w), dt)
    return pl.pallas_call(
        body, name="post_attn_bwd",
        out_shape=(sd(D, F32), sd(D, BF), sd(PT_W, BF), sd(D, BF), sd(D, BF), sd(D, BF), sd(D, BF),
                   jax.ShapeDtypeStruct((1, D), F32), jax.ShapeDtypeStruct((1, D), F32)),
        grid=(T // tm,),
        in_specs=[row(D), row(D), row(PT_W), row(D), row(D), full((D, D)), full((D, D)), full((D, D)),
                  full((1, D))],
        out_specs=(row(D), row(D), row(PT_W), row(D), row(D), row(D), row(D), full((1, D)), full((1, D))),
        compiler_params=_params(("arbitrary",)),
    )(dh, pre, pt, yg, ym, wgo, wmo, wout, g1)


def _shift_down(u, prev, k):
    r = pltpu.roll(u, k, 0)
    p = pltpu.roll(prev, k, 0)
    head = jnp.where(_iota(p.shape, 0) < k, p, r[0:8, :])
    return jnp.concatenate([head, r[8:, :]], axis=0)


def _conv3(u, prev, w_ref, b_ref):
    return (w_ref[0:1, :] * _shift_down(u, prev, 2) + w_ref[1:2, :] * _shift_down(u, prev, 1)
            + w_ref[2:3, :] * u + b_ref[...])


def _ffn_up_fwd(hb, wug, wuv, cw, cb, *, S, tm, tn):
    T = hb.shape[0]
    nj, nbs = DFF // tn, S // tm

    def body(h_ref, wg_ref, wv_ref, cwg_ref, cwv_ref, cbg_ref, cbv_ref,
             ug_ref, uv_ref, ucg_ref, ucv_ref, f_ref, pg_scr, pv_scr):
        @pl.when(pl.program_id(1) % nbs == 0)
        def _():
            pg_scr[...] = jnp.zeros_like(pg_scr)
            pv_scr[...] = jnp.zeros_like(pv_scr)

        h = h_ref[...]
        ug = _dot(h, wg_ref[...])
        uv = _dot(h, wv_ref[...])
        ucg = _conv3(ug, pg_scr[...], cwg_ref, cbg_ref)
        ucv = _conv3(uv, pv_scr[...], cwv_ref, cbv_ref)
        pg_scr[...] = ug[tm - 8:, :]
        pv_scr[...] = uv[tm - 8:, :]
        ug_ref[...] = ug.astype(BF)
        uv_ref[...] = uv.astype(BF)
        ucg_ref[...] = ucg
        ucv_ref[...] = ucv
        f_ref[...] = (ucg * _sigmoid(ucg) * ucv).astype(BF)

    tile = pl.BlockSpec((tm, tn), lambda j, i: (i, j))
    return pl.pallas_call(
        body, name="ffn_up_fwd",
        out_shape=(jax.ShapeDtypeStruct((T, DFF), BF), jax.ShapeDtypeStruct((T, DFF), BF),
                   jax.ShapeDtypeStruct((T, DFF), F32), jax.ShapeDtypeStruct((T, DFF), F32),
                   jax.ShapeDtypeStruct((T, DFF), BF)),
        grid=(nj, T // tm),
        in_specs=[pl.BlockSpec((tm, D), lambda j, i: (i, 0)),
                  pl.BlockSpec((D, tn), lambda j, i: (0, j)), pl.BlockSpec((D, tn), lambda j, i: (0, j)),
                  pl.BlockSpec((3, tn), lambda j, i: (0, j)), pl.BlockSpec((3, tn), lambda j, i: (0, j + nj)),
                  pl.BlockSpec((1, tn), lambda j, i: (0, j)), pl.BlockSpec((1, tn), lambda j, i: (0, j + nj))],
        out_specs=(tile, tile, tile, tile, tile),
        scratch_shapes=[pltpu.VMEM((8, tn), F32), pltpu.VMEM((8, tn), F32)],
        compiler_params=_params(("parallel", "arbitrary")),
    )(hb, wug, wuv, cw, cw, cb, cb)


def _ffn_bwd(dpreb, wd, ug, uv, ucg, ucv, cw, *, S, tm, tn):
    T = dpreb.shape[0]
    nj, nb, nbs = DFF // tn, T // tm, S // tm
    r_, c_ = lax.broadcasted_iota(jnp.int32, (tm, tm), 0), lax.broadcasted_iota(jnp.int32, (tm, tm), 1)
    s1, s2 = (c_ == r_ + 1).astype(BF), (c_ == r_ + 2).astype(BF)

    def body(dp_ref, wd_ref, ug_ref, uv_ref, ucg_ref, ucv_ref, cwg_ref, cwv_ref, s1_ref, s2_ref,
             dug_ref, duv_ref, dcg_ref, dcv_ref, ng_scr, nv_scr):
        ii = pl.program_id(1)
        i = nb - 1 - ii
        tail_row = _iota((8, tn), 0)

        @pl.when(ii == 0)
        def _():
            dcg_ref[...] = jnp.zeros_like(dcg_ref)
            dcv_ref[...] = jnp.zeros_like(dcv_ref)

        @pl.when(i % nbs == nbs - 1)
        def _():
            ng_scr[...] = jnp.zeros_like(ng_scr)
            nv_scr[...] = jnp.zeros_like(nv_scr)

        df = _dot_nt(dp_ref[...], wd_ref[...])
        ucg = ucg_ref[...]
        sg = _sigmoid(ucg)
        ducg = df * ucv_ref[...] * (sg * (1.0 + ucg * (1.0 - sg)))
        ducv = df * (ucg * sg)

        def finish(duc, u_ref, w, nxt_scr, du_ref, dc_ref):
            nxt = nxt_scr[...]
            db = duc.astype(BF)

            def shifted(s_ref, k):
                r = _dot(s_ref[...], db)
                tail = jnp.where(tail_row >= 8 - k, pltpu.roll(nxt, 8 - k, 0), r[tm - 8:, :])
                return jnp.concatenate([r[:tm - 8, :], tail], axis=0)

            up1 = shifted(s1_ref, 1)
            up2 = shifted(s2_ref, 2)
            du_ref[...] = (w[2:3, :] * duc + w[1:2, :] * up1 + w[0:1, :] * up2).astype(BF)
            nxt_scr[...] = duc[0:8, :]
            u = u_ref[...].astype(F32)
            for row, z in enumerate((u * up2, u * up1, u * duc, duc)):
                dc_ref[row:row + 1, :] += jnp.sum(z, axis=0, keepdims=True)

        finish(ducg, ug_ref, cwg_ref, ng_scr, dug_ref, dcg_ref)
        finish(ducv, uv_ref, cwv_ref, nv_scr, duv_ref, dcv_ref)

    tile = pl.BlockSpec((tm, tn), lambda j, ii: (nb - 1 - ii, j))
    acc = pl.BlockSpec((8, tn), lambda j, ii: (0, j))
    return pl.pallas_call(
        body, name="ffn_bwd",
        out_shape=(jax.ShapeDtypeStruct((T, DFF), BF), jax.ShapeDtypeStruct((T, DFF), BF),
                   jax.ShapeDtypeStruct((8, DFF), F32), jax.ShapeDtypeStruct((8, DFF), F32)),
        grid=(nj, nb),
        in_specs=[pl.BlockSpec((tm, D), lambda j, ii: (nb - 1 - ii, 0)),
                  pl.BlockSpec((tn, D), lambda j, ii: (j, 0)),
                  tile, tile, tile, tile,
                  pl.BlockSpec((3, tn), lambda j, ii: (0, j)), pl.BlockSpec((3, tn), lambda j, ii: (0, j + nj)),
                  pl.BlockSpec((tm, tm), lambda j, ii: (0, 0)), pl.BlockSpec((tm, tm), lambda j, ii: (0, 0))],
        out_specs=(tile, tile, acc, acc),
        scratch_shapes=[pltpu.VMEM((8, tn), F32), pltpu.VMEM((8, tn), F32)],
        compiler_params=_params(("parallel", "arbitrary")),
    )(dpreb, wd, ug, uv, ucg, ucv, cw, cw, s1, s2)


def _down_ln2_loss(f_in, wd, pre1, target, g1, b1, g2, b2, *, tm):
    T = pre1.shape[0]

    def body(f_ref, wd_ref, p1_ref, t_ref, g1_ref, b1_ref, g_ref, b_ref, dpb_ref, dh_ref, loss_ref, dg_ref, db_ref):
        @pl.when(pl.program_id(0) == 0)
        def _():
            loss_ref[...] = jnp.zeros_like(loss_ref)
            dg_ref[...] = jnp.zeros_like(dg_ref)
            db_ref[...] = jnp.zeros_like(db_ref)

        halves = [pl.ds(s * (tm // 2), tm // 2) for s in range(2)]
        f = [_dot(f_ref[hs, :], wd_ref[...]) for hs in halves]
        for hs, fh in zip(halves, f):
            h, _, _ = _ln_fwd(p1_ref[hs, :], g1_ref[...], b1_ref[...])
            pre = ALPHA * h + fh
            out, xhat, rstd = _ln_fwd(pre, g_ref[...], b_ref[...])
            diff = out - t_ref[hs, :]
            loss_ref[...] += 0.5 * jnp.sum(jnp.mean(diff * diff, axis=-1, keepdims=True))
            dpre, dg, db = _ln_bwd(diff * (1.0 / D), xhat, rstd, g_ref[...])
            dg_ref[...] += dg
            db_ref[...] += db
            dpb_ref[hs, :] = dpre.astype(BF)
            dh_ref[hs, :] = ALPHA * dpre

    full = lambda shp: pl.BlockSpec(shp, lambda i: (0,) * len(shp))
    row = lambda w: pl.BlockSpec((tm, w), lambda i: (i, 0))
    return pl.pallas_call(
        body, name="down_ln2_loss",
        out_shape=(jax.ShapeDtypeStruct((T, D), BF), jax.ShapeDtypeStruct((T, D), F32),
                   jax.ShapeDtypeStruct((8, 128), F32), jax.ShapeDtypeStruct((1, D), F32),
                   jax.ShapeDtypeStruct((1, D), F32)),
        grid=(T // tm,),
        in_specs=[row(DFF), full((DFF, D)), row(D), row(D), full((1, D)), full((1, D)), full((1, D)), full((1, D))],
        out_specs=(row(D), row(D), full((8, 128)), full((1, D)), full((1, D))),
        compiler_params=_params(("arbitrary",)),
    )(f_in, wd, pre1, target, g1, b1, g2, b2)


def _adamw(parts, w, m, v, *, name):
    n, R, C = parts.shape
    tr, tc = R, C
    for cand in range(min(R, 256), 15, -1):
        if R % cand == 0 and cand % 16 == 0:
            tr = cand
            break
    if tr == R and R * C > 65536 and C % 256 == 0:
        tc = 256
    c1 = 1.0 - ADAM_B1 ** ADAM_STEP
    c2 = 1.0 - ADAM_B2 ** ADAM_STEP

    def body(p_ref, w_ref, m_ref, v_ref, g_ref, d_ref, nm_ref, nv_ref):
        g = p_ref[0].astype(F32)
        for s in range(1, n):
            g = g + p_ref[s].astype(F32)
        nm = ADAM_B1 * m_ref[...] + (1.0 - ADAM_B1) * g
        nv = ADAM_B2 * v_ref[...] + (1.0 - ADAM_B2) * (g * g)
        g_ref[...] = g
        nm_ref[...] = nm
        nv_ref[...] = nv
        d_ref[...] = -ADAM_LR * ((nm / c1) / (jnp.sqrt(nv / c2) + ADAM_EPS) + ADAM_WD * w_ref[...])

    blk = pl.BlockSpec((tr, tc), lambda i, j: (i, j))
    sd = jax.ShapeDtypeStruct((R, C), F32)
    return pl.pallas_call(
        body, name=name,
        out_shape=(sd, sd, sd, sd),
        grid=(R // tr, C // tc),
        in_specs=[pl.BlockSpec((n, tr, tc), lambda i, j: (0, i, j)), blk, blk, blk],
        out_specs=(blk, blk, blk, blk),
        compiler_params=_params(("parallel", "parallel")),
    )(parts, w, m, v)


class _Exchange:
    def __init__(self, items):
        self.items = [(src if sc else [(src, 0)], sc) for src, sc in items]
        self.arrays = [arr for srcs, _ in self.items for arr, _ in srcs]
        self.n = len(self.items)
        self.n_in = len(self.arrays)

    def out_shape(self):
        return tuple(jax.ShapeDtypeStruct((NDEV,) + (srcs[0][0].shape[1:] if sc else srcs[0][0].shape),
                                          srcs[0][0].dtype) for srcs, sc in self.items)

    def scratch(self):
        return [pltpu.SemaphoreType.DMA((self.n, NDEV - 1)), pltpu.SemaphoreType.DMA((self.n, NDEV - 1)),
                pltpu.SemaphoreType.DMA((self.n,))]

    def _emit(self, ins, outs, sems, phase):
        send_sems, recv_sems, loc_sems = sems
        x, y, c = lax.axis_index("x"), lax.axis_index("y"), lax.axis_index("c")
        me = 4 * x + 2 * y + c
        flip = lambda p, d: 1 - p if d else p

        def inside(p, lo, n):
            return None if (lo, n) == (0, NDEV) else jnp.logical_and(p >= lo, p < lo + n)

        def when(cond, fn):
            if cond is None:
                fn()
            else:
                pl.when(cond)(fn)

        pos = 0
        for a, (srcs, sc) in enumerate(self.items):
            refs = ins[pos:pos + len(srcs)]
            pos += len(srcs)
            ranges = [(lo, arr.shape[0]) if sc else (0, NDEV) for arr, lo in srcs]
            mine = [inside(me, lo, n) for lo, n in ranges]
            i_receive = None if None in mine else functools.reduce(jnp.logical_or, mine)
            for ref, (lo, n), cond in zip(refs, ranges, mine):
                def local(ref=ref, lo=lo):
                    cp = pltpu.make_async_copy(ref.at[me - lo] if sc else ref, outs[a].at[me], loc_sems.at[a])
                    cp.start() if phase == 0 else cp.wait()
                if phase != 1:
                    when(cond, local)
            for k in range(1, NDEV):
                px, py, pc = flip(x, k & 4), flip(y, k & 2), flip(c, k & 1)
                peer = 4 * px + 2 * py + pc
                mk = functools.partial(pltpu.make_async_remote_copy,
                                       send_sem=send_sems.at[a, k - 1], recv_sem=recv_sems.at[a, k - 1],
                                       device_id=(px, py, pc), device_id_type=MESH_ID)
                if phase == 1:
                    def arrival(mk=mk, peer=peer):
                        mk(src_ref=refs[0].at[0] if sc else refs[0], dst_ref=outs[a].at[peer]).wait_recv()
                    when(i_receive, arrival)
                    continue
                for ref, (lo, n) in zip(refs, ranges):
                    def send(mk=mk, ref=ref, lo=lo, peer=peer):
                        cp = mk(src_ref=ref.at[peer - lo] if sc else ref, dst_ref=outs[a].at[me])
                        cp.start() if phase == 0 else cp.wait_send()
                    when(inside(peer, lo, n), send)

    def start(self, ins, outs, sems):
        self._emit(ins, outs, sems, 0)

    def wait(self, ins, outs, sems):
        self._emit(ins, outs, sems, 1)
        self._emit(ins, outs, sems, 2)


def _call(body, *, name, grid, in_specs, out_specs, out_shape, args, scratch_shapes=(), sem=None, ride=None):
    if ride is None:
        return pl.pallas_call(body, name=name, grid=grid, in_specs=list(in_specs), out_specs=tuple(out_specs),
                              out_shape=tuple(out_shape), scratch_shapes=list(scratch_shapes),
                              compiler_params=_params(sem))(*args)
    n_in, n_out, n_scr, ne, ne_in = len(args), len(out_shape), len(scratch_shapes), ride.n, ride.n_in

    def ride_body(*refs):
        ins, ex_in = refs[:n_in], refs[n_in:n_in + ne_in]
        o0 = n_in + ne_in
        outs, ex_out = refs[o0:o0 + n_out], refs[o0 + n_out:o0 + n_out + ne]
        scr = refs[o0 + n_out + ne:o0 + n_out + ne + n_scr]
        sems = refs[o0 + n_out + ne + n_scr:]
        first = functools.reduce(jnp.logical_and, [pl.program_id(d) == 0 for d in range(len(grid))])
        last = functools.reduce(jnp.logical_and, [pl.program_id(d) == grid[d] - 1 for d in range(len(grid))])

        @pl.when(first)
        def _():
            ride.start(ex_in, ex_out, sems)

        body(*ins, *outs, *scr)

        @pl.when(last)
        def _():
            ride.wait(ex_in, ex_out, sems)

    anyspec = pl.BlockSpec(memory_space=pl.ANY)
    res = pl.pallas_call(
        ride_body, name=name, grid=grid,
        in_specs=list(in_specs) + [anyspec] * ne_in,
        out_specs=tuple(out_specs) + (anyspec,) * ne,
        out_shape=tuple(out_shape) + ride.out_shape(),
        scratch_shapes=list(scratch_shapes) + ride.scratch(),
        compiler_params=_params(("arbitrary",) * len(grid)),
    )(*args, *ride.arrays)
    return tuple(res[:n_out]), tuple(res[n_out:])


def _gather_two_level(arrays, *, name):
    n = len(arrays)

    def body(*refs):
        ins, outs = refs[:n], refs[n:2 * n]
        send_sems, recv_sems, loc_sems = refs[2 * n:]
        x, y, c = lax.axis_index("x"), lax.axis_index("y"), lax.axis_index("c")
        sibling = (x, y, 1 - c)
        chips = [(1 - x, y), (x, 1 - y), (1 - x, 1 - y)]
        idx = lambda px, py, pc: 4 * px + 2 * py + pc
        me = idx(x, y, c)

        def copy(a, k, block, to, src=None):
            return pltpu.make_async_remote_copy(
                src_ref=outs[a].at[block] if src is None else src, dst_ref=outs[a].at[block],
                send_sem=send_sems.at[a, k], recv_sem=recv_sems.at[a, k], device_id=to, device_id_type=MESH_ID)

        local = [pltpu.make_async_copy(ins[a], outs[a].at[me], loc_sems.at[a]) for a in range(n)]
        sent = []
        for a in range(n):
            sent.append(copy(a, 0, me, sibling, src=ins[a]))
            sent += [copy(a, 1 + j, me, (*chip, c), src=ins[a]) for j, chip in enumerate(chips)]
        for cp in local + sent:
            cp.start()
        for j, chip in enumerate(chips):
            for a in range(n):
                copy(a, 1 + j, idx(*chip, c), sibling).wait_recv()
                passed = copy(a, 4 + j, idx(*chip, c), sibling)
                passed.start()
                sent.append(passed)
        for a in range(n):
            copy(a, 0, idx(x, y, 1 - c), sibling).wait_recv()
            for j, chip in enumerate(chips):
                copy(a, 4 + j, idx(*chip, 1 - c), sibling).wait_recv()
        for cp in sent:
            cp.wait_send()
        for cp in local:
            cp.wait()

    anyspec = pl.BlockSpec(memory_space=pl.ANY)
    return pl.pallas_call(
        body, name=name,
        out_shape=tuple(jax.ShapeDtypeStruct((NDEV,) + a.shape, a.dtype) for a in arrays),
        in_specs=[anyspec] * n, out_specs=(anyspec,) * n,
        scratch_shapes=[pltpu.SemaphoreType.DMA((n, NDEV - 1)), pltpu.SemaphoreType.DMA((n, NDEV - 1)),
                        pltpu.SemaphoreType.DMA((n,))],
    )(*arrays)


def _tri_consts():
    r = lax.broadcasted_iota(jnp.int32, (GC, GC), 0)
    c = lax.broadcasted_iota(jnp.int32, (GC, GC), 1)
    return (r >= c).astype(BF), (r <= c).astype(BF)


def _local_step(x, positions, target, w, hooks=None):
    g = {}

    def run(host, fn, *a, **kw):
        h = None if hooks is None else hooks.get(host)
        if h is None:
            return fn(*a, **kw)
        out, received = fn(*a, ride=_Exchange(h[0](w, g)), **kw)
        h[1](received, w, g)
        return out

    nseq, S, _ = x.shape
    T = nseq * S
    tm = min(TOKEN_TM, S)
    tq = min(FLASH_TQ, S)
    x2 = x.reshape(T, D)
    pos = positions.reshape(T, 1)
    half = ROPE // 2
    inv = THETA ** (-jnp.arange(half, dtype=F32) / half)
    invf = jnp.concatenate([inv, inv, jnp.zeros((64,), F32)]).reshape(1, 128)
    ltri, utri = _tri_consts()

    pt, xb = _matmul(x2, w["w_tt"], "nt", name="proj_t", out_dtype=BF, tm=1024, tn=1024, tk=1024, emit_a=True)
    pg = run("proj_g", _matmul, xb, w["w_gt"], "nt", name="proj_g", tm=1024, tn=640, tk=1024)
    pm = _matmul(xb, w["w_mt"], "nt", name="proj_m", tm=1024, tn=768, tk=1024)
    o, zg, states = run("gla_fwd", _gla_fwd, pg, w["wg"], w["bg"], w["gn"], ltri, nseq=nseq, S=S, tm=tm)
    qc, kc, v = _mla_prep_fwd(pm, pos, invf, w["gq"], w["gkv"], w["wuq"], w["wukv"], tm=tm)
    attn, lse = run("flash_fwd", _flash_fwd, qc, kc, v, nseq=nseq, S=S, tq=tq)
    yg, ym, mix, pre1, h1b = run("post_attn_fwd", _post_attn_fwd, zg, attn, pt, x2, w["wgo"], w["wmo"], w["wout"],
                                 w["g1"], w["b1"], tm=tm)
    ug, uv, ucg, ucv, f_in = _ffn_up_fwd(h1b, w["wug"], w["wuv"], w["cw"], w["cb"], S=S, tm=tm, tn=FFN_TN)
    dpre2b, dh1, loss8, dg2, db2 = _down_ln2_loss(f_in, w["wd"], pre1, target.reshape(T, D), w["g1"], w["b1"],
                                                  w["g2"], w["b2"], tm=min(2 * tm, S))

    dug, duv, dcg, dcv = _ffn_bwd(dpre2b, w["wd"], ug, uv, ucg, ucv, w["cw"], S=S, tm=tm, tn=FFN_TN)
    g["g2"], g["b2"], g["loss"] = dg2, db2, loss8[0:1, 0:1]
    g["cw"] = jnp.concatenate([dcg[0:3], dcv[0:3]], axis=1)
    g["cb"] = jnp.concatenate([dcg[3:4], dcv[3:4]], axis=1)
    g["wd"] = _matmul(f_in, dpre2b, "tn", name="dw_down", out_dtype=BF, tm=1408, tn=1024, tk=1024)
    g["wugt"] = _matmul(dug, h1b, "tn", name="dw_up_g", out_dtype=BF, tm=1408, tn=1024, tk=1024)
    g["wuvt"] = _matmul(duv, h1b, "tn", name="dw_up_v", out_dtype=BF, tm=1408, tn=1024, tk=1024)
    dh1 = _matmul(dug, w["wugt"], "nn", name="dh1_g", c_in=dh1, tm=1024, tn=1024, tk=1408)
    dh1 = _matmul(duv, w["wuvt"], "nn", name="dh1_v", c_in=dh1, tm=1024, tn=1024, tk=1408)
    dx, dpre1b, dpt, dygb, dymb, dzg, dattn, dg1, db1 = _post_attn_bwd(
        dh1, pre1, pt, yg, ym, w["wgo"], w["wmo"], w["wout"], w["g1"], tm=tm)
    g["g1"], g["b1"] = dg1, db1
    g["wout"] = _matmul(mix, dpre1b, "tn", name="dw_out", out_dtype=BF, tm=1024, tn=1024, tk=1024)
    g["wgo"] = _matmul(zg, dygb, "tn", name="dw_gla_o", out_dtype=BF, tm=1024, tn=1024, tk=1024)
    g["wmo"] = _matmul(attn, dymb, "tn", name="dw_mla_o", out_dtype=BF, tm=1024, tn=1024, tk=1024)
    dqc, dkc, dv = run("flash_bwd", _flash_bwd, qc, kc, v, attn, dattn, lse, nseq=nseq, S=S, tq=tq)
    dpm, g["wuq"], g["wukv"], g["gq"], g["gkv"] = _mla_prep_bwd(
        pm, pos, invf, w["gq"], w["gkv"], w["wuq"], w["wukv"], dqc, dkc, dv, tm=tm)
    g["w_mt"] = _matmul(dpm, xb, "tn", name="dw_in_m", out_dtype=BF, tm=768, tn=1024, tk=1024)
    g["w_tt"] = _matmul(dpt, xb, "tn", name="dw_in_t", out_dtype=BF, tm=1024, tn=1024, tk=1024)
    dpg, g["wg"], g["bg"], g["gn"] = run("gla_bwd", _gla_bwd, pg, w["wg"], w["bg"], w["gn"], ltri, utri, o, states,
                                         dzg, nseq=nseq, S=S, tm=tm)
    g["w_gt"] = _matmul(dpg, xb, "tn", name="dw_in_g", out_dtype=BF, tm=640, tn=1024, tk=1024)
    dx = run("dx", _matmul_sum, dx, [(dpg, w["w_gt"], 640), (dpm, w["w_mt"], 768)], name="dx_gm")
    dx = _matmul_sum(dx, [(dpt, w["w_tt"], 1024)], name="dx_t")
    return loss8[0, 0], dx.reshape(nseq, S, D), g


_IN_SPLITS = (512, 512, 1024, 16, 1024, 384, 256, 64, 1024, 1024)


def _w_in_to_groups(wt):
    offs = [0]
    for s in _IN_SPLITS:
        offs.append(offs[-1] + s)
    q, k, v, r, og, cq, ckv, kr, ga, gb = [wt[offs[i]:offs[i + 1]] for i in range(10)]
    z = lambda n: jnp.zeros((n, wt.shape[1]), wt.dtype)
    return (jnp.concatenate([q, k, v, og, r, z(112)], axis=0),
            jnp.concatenate([cq, kr, z(64), ckv], axis=0),
            jnp.concatenate([ga, gb], axis=0))


W_IN_BLOCK = sum(_IN_SPLITS) // NDEV
_KV_LATENT_ROW = sum(_IN_SPLITS[:6])
_W_IN_LO = 5
_W_IN_SPLIT = _W_IN_LO * W_IN_BLOCK - _KV_LATENT_ROW


def _w_in_rows_lo(g_g, g_m):
    q, k, v, og, r = g_g[0:512], g_g[512:1024], g_g[1024:2048], g_g[2048:3072], g_g[3072:3088]
    return jnp.concatenate([q, k, v, r, og, g_m[0:384], g_m[512:768]], axis=0)[:_W_IN_LO * W_IN_BLOCK]


def _w_in_rows_hi(g_m, g_t):
    return jnp.concatenate([g_m[512:768], g_m[384:448], g_t], axis=0)[_W_IN_SPLIT:]


def _uq_to_kernel(wuq):
    w3 = wuq.reshape(MQR, MH, NOPE + ROPE)
    rope = jnp.concatenate([w3[:, :, NOPE:], jnp.zeros((MQR, MH, 64), wuq.dtype)], axis=2)
    return jnp.concatenate([w3[:, :, :NOPE].reshape(MQR, MH * 128), rope.reshape(MQR, MH * 128)], axis=1)


def _uq_from_kernel(g):
    nope = g[:, :1024].reshape(MQR, MH, 128)
    rope = g[:, 1024:].reshape(MQR, MH, 128)[:, :, :ROPE]
    return jnp.concatenate([nope, rope], axis=2)


def _ukv_to_kernel(wukv):
    w3 = wukv.reshape(MKR, MH, NOPE + MV)
    return jnp.concatenate([w3[:, :, :NOPE].reshape(MKR, MH * 128), w3[:, :, NOPE:].reshape(MKR, MH * 128)], axis=1)


def _ukv_from_kernel(g):
    return jnp.concatenate([g[:, :1024].reshape(MKR, MH, 128), g[:, 1024:].reshape(MKR, MH, 128)], axis=2)


def _cols_gathered(a):
    return a.transpose(1, 0, 2).reshape(a.shape[1], NDEV * a.shape[2])


def _cols_scattered(a):
    R = a.shape[0]
    return a.reshape(R, NDEV, a.shape[1] // NDEV).transpose(1, 0, 2)


_SMALL = (("gla_b_gate", 512), ("gla_norm_g", 256), ("mla_q_norm_g", 384), ("mla_kv_norm_g", 256),
          ("ln1_g", 1024), ("ln1_b", 1024), ("conv_b", 5632), ("ln2_g", 1024), ("ln2_b", 1024))
_SMALL_ROWS = 88
_SMALL_USED = sum(sz for _, sz in _SMALL)


def _pack_small(d):
    flat = jnp.concatenate([d[n].reshape(-1) for n, _ in _SMALL] + ([d['loss'].reshape(-1)] if 'loss' in d else []))
    return jnp.pad(flat, (0, _SMALL_ROWS * 128 - flat.shape[0])).reshape(_SMALL_ROWS, 128)


def _unpack_small(a):
    flat = a.reshape(-1)
    out, off = {}, 0
    for n, sz in _SMALL:
        out[n] = flat[off:off + sz].reshape(1, sz)
        off += sz
    return out


_NAMES = ['w_in', 'gla_w_gate_up', 'gla_b_gate', 'gla_norm_g', 'w_gla_o', 'mla_q_norm_g', 'mla_w_uq',
          'mla_kv_norm_g', 'mla_w_ukv', 'w_mla_o', 'w_out', 'ln1_g', 'ln1_b', 'w_up', 'conv_w', 'conv_b',
          'w_down', 'ln2_g', 'ln2_b']
_SHARDED = ['w_in', 'w_up', 'w_down', 'w_gla_o', 'w_mla_o', 'w_out', 'mla_w_uq', 'mla_w_ukv', 'gla_w_gate_up',
            'conv_w']


def kernel(x, positions, w_in, gla_w_gate_up, gla_b_gate, gla_norm_g, w_gla_o, mla_q_norm_g, mla_w_uq, mla_kv_norm_g, mla_w_ukv, w_mla_o, w_out, ln1_g, ln1_b, w_up, conv_w, conv_b, w_down, ln2_g, ln2_b, loss_target, m_w_in, m_gla_w_gate_up, m_gla_b_gate, m_gla_norm_g, m_w_gla_o, m_mla_q_norm_g, m_mla_w_uq, m_mla_kv_norm_g, m_mla_w_ukv, m_w_mla_o, m_w_out, m_ln1_g, m_ln1_b, m_w_up, m_conv_w, m_conv_b, m_w_down, m_ln2_g, m_ln2_b, v_w_in, v_gla_w_gate_up, v_gla_b_gate, v_gla_norm_g, v_w_gla_o, v_mla_q_norm_g, v_mla_w_uq, v_mla_kv_norm_g, v_mla_w_ukv, v_w_mla_o, v_w_out, v_ln1_g, v_ln1_b, v_w_up, v_conv_w, v_conv_b, v_w_down, v_ln2_g, v_ln2_b):
    W = dict(w_in=w_in, gla_w_gate_up=gla_w_gate_up, gla_b_gate=gla_b_gate, gla_norm_g=gla_norm_g, w_gla_o=w_gla_o, mla_q_norm_g=mla_q_norm_g, mla_w_uq=mla_w_uq, mla_kv_norm_g=mla_kv_norm_g, mla_w_ukv=mla_w_ukv, w_mla_o=w_mla_o, w_out=w_out, ln1_g=ln1_g, ln1_b=ln1_b, w_up=w_up, conv_w=conv_w, conv_b=conv_b, w_down=w_down, ln2_g=ln2_g, ln2_b=ln2_b)
    M = dict(w_in=m_w_in, gla_w_gate_up=m_gla_w_gate_up, gla_b_gate=m_gla_b_gate, gla_norm_g=m_gla_norm_g, w_gla_o=m_w_gla_o, mla_q_norm_g=m_mla_q_norm_g, mla_w_uq=m_mla_w_uq, mla_kv_norm_g=m_mla_kv_norm_g, mla_w_ukv=m_mla_w_ukv, w_mla_o=m_w_mla_o, w_out=m_w_out, ln1_g=m_ln1_g, ln1_b=m_ln1_b, w_up=m_w_up, conv_w=m_conv_w, conv_b=m_conv_b, w_down=m_w_down, ln2_g=m_ln2_g, ln2_b=m_ln2_b)
    V = dict(w_in=v_w_in, gla_w_gate_up=v_gla_w_gate_up, gla_b_gate=v_gla_b_gate, gla_norm_g=v_gla_norm_g, w_gla_o=v_w_gla_o, mla_q_norm_g=v_mla_q_norm_g, mla_w_uq=v_mla_w_uq, mla_kv_norm_g=v_mla_kv_norm_g, mla_w_ukv=v_mla_w_ukv, w_mla_o=v_w_mla_o, w_out=v_w_out, ln1_g=v_ln1_g, ln1_b=v_ln1_b, w_up=v_w_up, conv_w=v_conv_w, conv_b=v_conv_b, w_down=v_w_down, ln2_g=v_ln2_g, ln2_b=v_ln2_b)

    tshard = lambda d, n: d[n][0].T
    shard = lambda n: (W[n][0].astype(BF), False)
    (w_in_t,) = _gather_two_level([tshard(W, 'w_in').astype(BF)], name="gather_w0")
    w_gt, w_mt, w_tt = _w_in_to_groups(w_in_t.reshape(NDEV * W_IN_BLOCK, D))
    kw = dict(
        w_gt=w_gt, w_mt=w_mt, w_tt=w_tt, bg=W['gla_b_gate'],
        gn=W['gla_norm_g'], gq=W['mla_q_norm_g'], gkv=W['mla_kv_norm_g'],
        g1=W['ln1_g'], b1=W['ln1_b'], g2=W['ln2_g'], b2=W['ln2_b'], cb=W['conv_b'],
    )
    received = {}

    def got_mixers(ex, w, g):
        w.update(wuq=_uq_to_kernel(_cols_gathered(ex[0])), wukv=_ukv_to_kernel(_cols_gathered(ex[1])),
                 wg=jnp.pad(_cols_gathered(ex[2]), ((0, 128 - GR), (0, 0))))

    def got_out_proj(ex, w, g):
        w.update(wgo=ex[0].reshape(D, D), wmo=ex[1].reshape(D, D), wout=ex[2].reshape(D, D))

    def got_up(ex, w, g):
        w_upt = ex[0].reshape(2 * DFF, D)
        w.update(wugt=w_upt[:DFF], wuvt=w_upt[DFF:], wug=w_upt[:DFF].T, wuv=w_upt[DFF:].T)

    def got_down(ex, w, g):
        w.update(wd=ex[0].reshape(DFF, D), cw=_cols_gathered(ex[1]))

    slab = lambda a, lo=0: ([(a.astype(BF), lo)], True)
    rows = lambda a, n=NDEV: a.reshape(n, a.shape[0] // n, a.shape[1])

    def keep(names):
        return lambda ex, w, g: received.update(zip(names, ex))

    def small_grads(g):
        return _pack_small(dict(gla_b_gate=g['bg'], gla_norm_g=g['gn'], mla_q_norm_g=g['gq'], mla_kv_norm_g=g['gkv'],
                                ln1_g=g['g1'], ln1_b=g['b1'], conv_b=g['cb'], ln2_g=g['g2'], ln2_b=g['b2'],
                                loss=g['loss']))

    hooks = {
        "proj_g": (lambda w, g: [shard('mla_w_uq'), shard('mla_w_ukv'), shard('gla_w_gate_up')], got_mixers),
        "gla_fwd": (lambda w, g: [shard('w_gla_o'), shard('w_mla_o'), shard('w_out')], got_out_proj),
        "flash_fwd": (lambda w, g: [(tshard(W, 'w_up').astype(BF), False)], got_up),
        "post_attn_fwd": (lambda w, g: [shard('w_down'), (W['conv_w'][0], False)], got_down),
        "flash_bwd": (lambda w, g: [slab(rows(g['wd'])),
                                    ([(rows(g['wugt'], 4), 0), (rows(g['wuvt'], 4), 4)], True),
                                    slab(rows(g['wout'])), slab(rows(g['wgo'])), slab(rows(g['wmo']))],
                      keep(['w_down', 'w_up', 'w_out', 'w_gla_o', 'w_mla_o'])),
        "gla_bwd": (lambda w, g: [slab(_uq_from_kernel(g['wuq']).transpose(1, 0, 2)),
                                  slab(_ukv_from_kernel(g['wukv']).transpose(1, 0, 2)),
                                  slab(rows(_w_in_rows_hi(g['w_mt'], g['w_tt']), NDEV - _W_IN_LO), _W_IN_LO)],
                    keep(['mla_w_uq', 'mla_w_ukv', 'w_in_hi'])),
        "dx": (lambda w, g: [slab(rows(_w_in_rows_lo(g['w_gt'], g['w_mt']), _W_IN_LO)),
                             ([(_cols_scattered(g['wg'][:GR]), 0)], True), ([(_cols_scattered(g['cw']), 0)], True),
                             (small_grads(g), False)],
               keep(['w_in_lo', 'gla_w_gate_up', 'conv_w', 'small'])),
    }

    _, grad_x, _ = _local_step(x, positions, loss_target, kw, hooks)

    grads, deltas, new_m, new_v = {}, {}, {}, {}
    small_parts = received['small']
    loss = jnp.sum(small_parts.reshape(NDEV, -1)[:, _SMALL_USED])
    me = 4 * lax.axis_index("x") + 2 * lax.axis_index("y") + lax.axis_index("c")
    received['w_in'] = jnp.where(me >= _W_IN_LO, received['w_in_hi'], received['w_in_lo'])
    for n in _SHARDED:
        shp = W[n].shape
        if n in ('w_in', 'w_up'):
            out = _adamw(received[n], tshard(W, n), tshard(M, n), tshard(V, n), name="adamw_" + n)
            grads[n], deltas[n], new_m[n], new_v[n] = [t.T.reshape(shp) for t in out]
            continue
        out = _adamw(received[n], W[n][0], M[n][0], V[n][0], name="adamw_" + n)
        grads[n], deltas[n], new_m[n], new_v[n] = [t.reshape(shp) for t in out]
    out = _adamw(small_parts, _pack_small(W), _pack_small(M), _pack_small(V), name="adamw_small")
    for dst, packed in zip((grads, deltas, new_m, new_v), out):
        dst.update(_unpack_small(packed))

    return (loss, grad_x, *[grads[n] for n in _NAMES], *[deltas[n] for n in _NAMES],
            *[new_m[n] for n in _NAMES], *[new_v[n] for n in _NAMES])
```

```python
import functools

import jax
import jax.numpy as jnp
from jax import lax
from jax.experimental import pallas as pl
from jax.experimental.pallas import tpu as pltpu

F32 = jnp.float32
BF = jnp.bfloat16

D = 1024
GH, GDK, GDV, GR, GTAU, GC = 4, 128, 256, 16, 16.0, 64
MH, MQR, MKR, NOPE, ROPE, MV = 8, 384, 256, 128, 64, 128
THETA = 10000.0
DFF = 2816
ALPHA = 2.0 ** 0.25
LN_EPS = 1e-5
RMS_EPS = 1e-6
NDEV = 8
ADAM_LR, ADAM_B1, ADAM_B2, ADAM_EPS, ADAM_WD, ADAM_STEP = 0.001, 0.9, 0.999, 1e-08, 0.01, 10

PG_W = 3200
PM_W = 768
PT_W = 2048
NEG = -1e30
MESH_ID = pl.DeviceIdType.MESH
VMEM_MB = 1024 * 1024


V7X_VMEM_LIMIT_MB = 48
TOKEN_TM = 256
FLASH_TQ = 512
FFN_TN = 1408


def _in_hbm(arrays):
    return [pltpu.with_memory_space_constraint(a, pltpu.HBM) for a in arrays]


def _out_hbm(shapes):
    return tuple(pltpu.HBM(s.shape, s.dtype) for s in shapes)


def _pallas_hbm(body, *, out_shape, **kw):
    return pl.pallas_call(body, out_shape=_out_hbm(out_shape), **kw)


def _params(sem):
    return pltpu.CompilerParams(dimension_semantics=sem, vmem_limit_bytes=V7X_VMEM_LIMIT_MB * VMEM_MB)


def _dot(a, b):
    return lax.dot_general(a, b, (((1,), (0,)), ((), ())), preferred_element_type=F32)


def _dot_nt(a, b):
    return lax.dot_general(a, b, (((1,), (1,)), ((), ())), preferred_element_type=F32)


def _dot_tn(a, b):
    return lax.dot_general(a, b, (((0,), (0,)), ((), ())), preferred_element_type=F32)


def _iota(shape, dim):
    return lax.broadcasted_iota(jnp.int32, shape, dim)


FLASH_HP = 2
FLASH_HP_FWD = 4
QK_SCALE = (NOPE + ROPE) ** -0.5
LOG2E = 1.4426950408889634
QK_SCALE_LOG2 = QK_SCALE * LOG2E


def _sigmoid(x):
    return 0.5 * jnp.tanh(0.5 * x) + 0.5


def _tri_mm(tri_bf, x):
    hi = x.astype(BF)
    r1 = x - hi.astype(F32)
    mid = r1.astype(BF)
    lo = (r1 - mid.astype(F32)).astype(BF)
    return _dot(tri_bf, hi) + _dot(tri_bf, mid) + _dot(tri_bf, lo)


def _matmul(a, b, mode, *, name, c_in=None, out_dtype=F32, tm=512, tn=512, tk=512, ride=None, emit_a=False):
    if mode == "nn":
        (M, K), (_, N) = a.shape, b.shape
    elif mode == "nt":
        (M, K), (N, _) = a.shape, b.shape
    else:
        (K, M), (_, N) = a.shape, b.shape
    tm, tn, tk = min(tm, M), min(tn, N), min(tk, K)
    assert M % tm == 0 and N % tn == 0 and K % tk == 0, (name, M, N, K, tm, tn, tk)
    nk = K // tk
    assert not emit_a or (nk == 1 and mode != "tn" and c_in is None and ride is None)
    dot = {"nn": _dot, "nt": _dot_nt, "tn": _dot_tn}[mode]

    def body(*refs):
        if emit_a:
            a_ref, b_ref, o_ref, xa_ref, acc_ref = refs
        elif c_in is None:
            a_ref, b_ref, o_ref, acc_ref = refs
        else:
            a_ref, b_ref, c_ref, o_ref, acc_ref = refs
        k = pl.program_id(2)

        @pl.when(k == 0)
        def _():
            if c_in is None:
                acc_ref[...] = jnp.zeros_like(acc_ref)
            else:
                acc_ref[...] = c_ref[...].astype(F32)

        if emit_a:
            @pl.when(pl.program_id(1) == 0)
            def _():
                xa_ref[...] = a_ref[...].astype(BF)

        acc_ref[...] += dot(a_ref[...].astype(BF), b_ref[...].astype(BF))

        @pl.when(k == nk - 1)
        def _():
            o_ref[...] = acc_ref[...].astype(out_dtype)

    if mode == "tn":
        a_spec = pl.BlockSpec((tk, tm), lambda i, j, k: (k, i))
    else:
        a_spec = pl.BlockSpec((tm, tk), lambda i, j, k: (i, k))
    if mode == "nt":
        b_spec = pl.BlockSpec((tn, tk), lambda i, j, k: (j, k))
    else:
        b_spec = pl.BlockSpec((tk, tn), lambda i, j, k: (k, j))
    in_specs = [a_spec, b_spec]
    args = [a, b]
    if c_in is not None:
        in_specs.append(pl.BlockSpec((tm, tn), lambda i, j, k: (i, j)))
        args.append(c_in)
    out_shape = (jax.ShapeDtypeStruct((M, N), out_dtype),)
    out_specs = (pl.BlockSpec((tm, tn), lambda i, j, k: (i, j)),)
    if emit_a:
        out_shape += (jax.ShapeDtypeStruct((M, K), BF),)
        out_specs += (pl.BlockSpec((tm, tk), lambda i, j, k: (i, k)),)
    res = _call(
        body, name=name, out_shape=out_shape, grid=(M // tm, N // tn, nk), in_specs=in_specs, out_specs=out_specs,
        scratch_shapes=[pltpu.VMEM((tm, tn), F32)],
        sem=("parallel", "arbitrary", "arbitrary"), args=args, ride=ride)
    if emit_a:
        return res[0], res[1]
    return res[0] if ride is None else (res[0][0], res[1])


def _matmul_sum(c_in, parts, *, name, tm=1024, ride=None):
    M, N = c_in.shape
    tm = min(tm, M)
    n_p = len(parts)
    counts = [a.shape[1] // tk for a, _, tk in parts]
    starts = [sum(counts[:p]) for p in range(n_p)]
    nk = sum(counts)

    def body(*refs):
        a_refs, w_refs = refs[:n_p], refs[n_p:2 * n_p]
        c_ref, o_ref, acc_ref = refs[2 * n_p:]
        k = pl.program_id(1)

        @pl.when(k == 0)
        def _():
            acc_ref[...] = c_ref[...]

        for p in range(n_p):
            @pl.when(jnp.logical_and(k >= starts[p], k < starts[p] + counts[p]))
            def _(p=p):
                acc_ref[...] += _dot(a_refs[p][...].astype(BF), w_refs[p][...].astype(BF))

        @pl.when(k == nk - 1)
        def _():
            o_ref[...] = acc_ref[...]

    def kidx(p):
        return lambda k: jnp.clip(k - starts[p], 0, counts[p] - 1)

    in_specs = [pl.BlockSpec((tm, tk), lambda i, k, f=kidx(p): (i, f(k))) for p, (_, _, tk) in enumerate(parts)]
    in_specs += [pl.BlockSpec((tk, N), lambda i, k, f=kidx(p): (f(k), 0)) for p, (_, _, tk) in enumerate(parts)]
    in_specs.append(pl.BlockSpec((tm, N), lambda i, k: (i, 0)))
    res = _call(
        body, name=name, out_shape=(jax.ShapeDtypeStruct((M, N), F32),), grid=(M // tm, nk),
        in_specs=in_specs, out_specs=(pl.BlockSpec((tm, N), lambda i, k: (i, 0)),),
        scratch_shapes=[pltpu.VMEM((tm, N), F32)], sem=("parallel", "arbitrary"),
        args=[a for a, _, _ in parts] + [w for _, w, _ in parts] + [c_in], ride=ride)
    return res[0] if ride is None else (res[0][0], res[1])


def _gla_gate(pg_ref, rows, wg_ref, bg_ref):
    r = pg_ref[rows, 3072:3200].astype(BF)
    logit = _dot(r, wg_ref[...]) + bg_ref[...]
    la = (jnp.minimum(logit, 0.0) - jnp.log(1.0 + jnp.exp(-jnp.abs(logit)))) * (1.0 / GTAU)
    return r, logit, la


def _gla_fwd(pg, wg, bg, gn, ltri, *, nseq, S, tm, ride=None):
    T = pg.shape[0]
    nb, nc = S // tm, tm // GC
    qscale = GDK ** -0.5

    def body(pg_ref, wg_ref, bg_ref, gn_ref, l_ref, o_ref, zg_ref, st_ref, st_scr):
        @pl.when(pl.program_id(1) == 0)
        def _():
            st_scr[...] = jnp.zeros_like(st_scr)

        ltri_v = l_ref[...]
        causal = _iota((GC, GC), 0) >= _iota((GC, GC), 1)
        last_row = _iota((GC, GDK), 0) == GC - 1
        g = gn_ref[...]

        def chunk(c, carry):
            rows = pl.ds(pl.multiple_of(c * GC, GC), GC)
            _, _, la = _gla_gate(pg_ref, rows, wg_ref, bg_ref)
            b = _tri_mm(ltri_v, la)
            hs = range(GH)
            v, q_in, k_st, dec, st, a_raw, o_st, kv = [], [], [], [], [], [], [], []
            for h in hs:
                q = pg_ref[rows, h * GDK:(h + 1) * GDK]
                k = pg_ref[rows, 512 + h * GDK:512 + (h + 1) * GDK]
                v.append(pg_ref[rows, 1024 + h * GDV:1024 + (h + 1) * GDV].astype(BF))
                bh = b[:, h * GDK:(h + 1) * GDK]
                bl = jnp.sum(jnp.where(last_row, bh, 0.0), axis=0, keepdims=True)
                q_in.append((q * (qscale * jnp.exp(bh))).astype(BF))
                k_in = (k * jnp.exp(-bh)).astype(BF)
                k_st.append((k * jnp.exp(bl - bh)).astype(BF))
                dec.append(jnp.exp(bl))
                st.append(st_scr[h])
                st_ref[c, h] = st[h]
                a_raw.append(_dot_nt(q_in[h], k_in))
            for h in hs:
                o_st.append(_dot_nt(q_in[h], st[h].astype(BF)))
                kv.append(_dot_tn(v[h], k_st[h]))
            att = [jnp.where(causal, a_raw[h], 0.0).astype(BF) for h in hs]
            o = [_dot(att[h], v[h]) + o_st[h] for h in hs]
            for h in hs:
                st_scr[h] = st[h] * dec[h] + kv[h]
                og = pg_ref[rows, 2048 + h * GDV:2048 + (h + 1) * GDV]
                rstd = lax.rsqrt(jnp.mean(o[h] * o[h], axis=-1, keepdims=True) + RMS_EPS)
                o_ref[rows, h * GDV:(h + 1) * GDV] = o[h]
                zg_ref[rows, h * GDV:(h + 1) * GDV] = (o[h] * rstd * g * (og * _sigmoid(og))).astype(BF)
            return carry

        lax.fori_loop(0, nc, chunk, 0, unroll=True)

    full = lambda shp: pl.BlockSpec(shp, lambda b_, i: (0,) * len(shp))
    return _call(
        body, name="gla_fwd", ride=ride, sem=("parallel", "arbitrary"), args=(pg, wg, bg, gn, ltri),
        out_shape=(jax.ShapeDtypeStruct((T, GH * GDV), F32),
                   jax.ShapeDtypeStruct((T, GH * GDV), BF),
                   jax.ShapeDtypeStruct((T // GC, GH, GDV, GDK), F32)),
        grid=(nseq, nb),
        in_specs=[pl.BlockSpec((tm, PG_W), lambda b_, i: (b_ * nb + i, 0)),
                  full((128, 512)), full((1, 512)), full((1, GDV)), full((GC, GC))],
        out_specs=(pl.BlockSpec((tm, GH * GDV), lambda b_, i: (b_ * nb + i, 0)),
                   pl.BlockSpec((tm, GH * GDV), lambda b_, i: (b_ * nb + i, 0)),
                   pl.BlockSpec((nc, GH, GDV, GDK), lambda b_, i: (b_ * nb + i, 0, 0, 0))),
        scratch_shapes=[pltpu.VMEM((GH, GDV, GDK), F32)])


def _gla_bwd(pg, wg, bg, gn, ltri, utri, o, states, dzg, *, nseq, S, tm, ride=None):
    T = pg.shape[0]
    nb, nc = S // tm, tm // GC
    qscale = GDK ** -0.5

    def body(pg_ref, wg_ref, bg_ref, gn_ref, l_ref, u_ref, o_ref, st_ref, dzg_ref,
             dpg_ref, dwg_ref, dbg_ref, dgn_ref, dst_scr):
        first = jnp.logical_and(pl.program_id(0) == 0, pl.program_id(1) == 0)

        @pl.when(first)
        def _():
            dwg_ref[...] = jnp.zeros_like(dwg_ref)
            dbg_ref[...] = jnp.zeros_like(dbg_ref)
            dgn_ref[...] = jnp.zeros_like(dgn_ref)

        @pl.when(pl.program_id(1) == 0)
        def _():
            dst_scr[...] = jnp.zeros_like(dst_scr)

        ltri_v = l_ref[...]
        utri_v = u_ref[...]
        causal = _iota((GC, GC), 0) >= _iota((GC, GC), 1)
        last_row = _iota((GC, GDK), 0) == GC - 1
        g = gn_ref[...]

        def chunk(cc, carry):
            c = nc - 1 - cc
            rows = pl.ds(pl.multiple_of(c * GC, GC), GC)
            r, logit, la = _gla_gate(pg_ref, rows, wg_ref, bg_ref)
            b = _tri_mm(ltri_v, la)
            hs = range(GH)
            L = lambda: [None] * GH
            vb, eb, enb, ek, dec, q_in, k_in, k_st, q_inb, k_inb, st, dst, dob = (L() for _ in range(13))
            a_raw, da_raw, dq_st, dks, dv_st, dst_new, dbs, dgn = (L() for _ in range(8))
            for h in hs:
                q = pg_ref[rows, h * GDK:(h + 1) * GDK]
                k = pg_ref[rows, 512 + h * GDK:512 + (h + 1) * GDK]
                vb[h] = pg_ref[rows, 1024 + h * GDV:1024 + (h + 1) * GDV].astype(BF)
                og = pg_ref[rows, 2048 + h * GDV:2048 + (h + 1) * GDV]
                oh = o_ref[rows, h * GDV:(h + 1) * GDV]
                dz = dzg_ref[rows, h * GDV:(h + 1) * GDV].astype(F32)
                bh = b[:, h * GDK:(h + 1) * GDK]
                bl = jnp.sum(jnp.where(last_row, bh, 0.0), axis=0, keepdims=True)
                eb[h] = qscale * jnp.exp(bh)
                enb[h] = jnp.exp(-bh)
                ek[h] = jnp.exp(bl - bh)
                dec[h] = jnp.exp(bl)
                q_in[h], k_in[h], k_st[h] = q * eb[h], k * enb[h], k * ek[h]
                q_inb[h], k_inb[h] = q_in[h].astype(BF), k_in[h].astype(BF)
                st[h] = st_ref[c, h]
                dst[h] = dst_scr[h]
                rstd = lax.rsqrt(jnp.mean(oh * oh, axis=-1, keepdims=True) + RMS_EPS)
                ohat = oh * rstd
                sg = _sigmoid(og)
                don = dz * (og * sg)
                dpg_ref[rows, 2048 + h * GDV:2048 + (h + 1) * GDV] = (
                    dz * (ohat * g) * (sg * (1.0 + og * (1.0 - sg)))).astype(BF)
                dgn[h] = jnp.sum(don * ohat, axis=0, keepdims=True)
                gd = don * g
                dob[h] = (rstd * (gd - ohat * jnp.mean(gd * ohat, axis=-1, keepdims=True))).astype(BF)
                a_raw[h] = _dot_nt(q_inb[h], k_inb[h])
                da_raw[h] = _dot_nt(dob[h], vb[h])
            dgn_ref[...] += dgn[0] + dgn[1] + dgn[2] + dgn[3]
            for h in hs:
                dstb = dst[h].astype(BF)
                dq_st[h] = _dot(dob[h], st[h].astype(BF))
                dks[h] = _dot(vb[h], dstb)
                dv_st[h] = _dot_nt(k_st[h].astype(BF), dstb)
                dst_new[h] = _dot_tn(dob[h], q_inb[h])
            att = [jnp.where(causal, a_raw[h], 0.0).astype(BF) for h in hs]
            da = [jnp.where(causal, da_raw[h], 0.0).astype(BF) for h in hs]
            dqi = [_dot(da[h], k_inb[h]) + dq_st[h] for h in hs]
            dki = [_dot_tn(da[h], q_inb[h]) for h in hs]
            dv = [_dot_tn(att[h], dob[h]) + dv_st[h] for h in hs]
            for h in hs:
                dd = jnp.sum(dst[h] * st[h], axis=0, keepdims=True)
                dst_scr[h] = dst[h] * dec[h] + dst_new[h]
                kk = dks[h] * k_st[h]
                dbl = jnp.sum(kk, axis=0, keepdims=True) + dd * dec[h]
                db = dqi[h] * q_in[h] - dki[h] * k_in[h] - kk
                dbs[h] = db + jnp.where(last_row, dbl, 0.0)
                dpg_ref[rows, h * GDK:(h + 1) * GDK] = (dqi[h] * eb[h]).astype(BF)
                dpg_ref[rows, 512 + h * GDK:512 + (h + 1) * GDK] = (dki[h] * enb[h] + dks[h] * ek[h]).astype(BF)
                dpg_ref[rows, 1024 + h * GDV:1024 + (h + 1) * GDV] = dv[h].astype(BF)
            dla = _tri_mm(utri_v, jnp.concatenate(dbs, axis=1))
            dlogit = dla * (1.0 / GTAU) * _sigmoid(-logit)
            dlb = dlogit.astype(BF)
            dpg_ref[rows, 3072:3200] = _dot_nt(dlb, wg_ref[...]).astype(BF)
            dwg_ref[...] += _dot_tn(r, dlb)
            dbg_ref[...] += jnp.sum(dlogit, axis=0, keepdims=True)
            return carry

        lax.fori_loop(0, nc, chunk, 0, unroll=True)

    full = lambda shp: pl.BlockSpec(shp, lambda b_, i: (0,) * len(shp))
    rev = lambda b_, i: (b_ * nb + nb - 1 - i, 0)
    return _call(
        body, name="gla_bwd", ride=ride, sem=("arbitrary", "arbitrary"),
        args=(pg, wg, bg, gn, ltri, utri, o, states, dzg),
        out_shape=(jax.ShapeDtypeStruct((T, PG_W), BF),
                   jax.ShapeDtypeStruct((128, 512), F32),
                   jax.ShapeDtypeStruct((1, 512), F32),
                   jax.ShapeDtypeStruct((1, GDV), F32)),
        grid=(nseq, nb),
        in_specs=[pl.BlockSpec((tm, PG_W), rev),
                  full((128, 512)), full((1, 512)), full((1, GDV)), full((GC, GC)), full((GC, GC)),
                  pl.BlockSpec((tm, GH * GDV), rev),
                  pl.BlockSpec((nc, GH, GDV, GDK), lambda b_, i: (b_ * nb + nb - 1 - i, 0, 0, 0)),
                  pl.BlockSpec((tm, GH * GDV), rev)],
        out_specs=(pl.BlockSpec((tm, PG_W), rev), full((128, 512)), full((1, 512)), full((1, GDV))),
        scratch_shapes=[pltpu.VMEM((GH, GDV, GDK), F32)])


def _rope_tables(pos, invf):
    ang = pos.astype(F32) * invf
    lane = _iota(ang.shape, 1)
    sin = jnp.sin(ang)
    ssin = jnp.where(lane < 32, -sin, jnp.where(lane < 64, sin, 0.0))
    return jnp.cos(ang), ssin, lane


def _rope(x, cos, ssin, lane, sign):
    rot = jnp.where(lane < 32, pltpu.roll(x, 96, 1), pltpu.roll(x, 32, 1))
    return x * cos + sign * (rot * ssin)


def _rms_fwd(x, g):
    rstd = lax.rsqrt(jnp.mean(x * x, axis=-1, keepdims=True) + RMS_EPS)
    return x * rstd * g, x * rstd, rstd


def _rms_bwd(dy, xhat, rstd, g):
    gd = dy * g
    return rstd * (gd - xhat * jnp.mean(gd * xhat, axis=-1, keepdims=True)), jnp.sum(dy * xhat, axis=0, keepdims=True)


def _mla_prep_fwd(pm, pos, invf, gq, gkv, wuq, wukv, *, tm):
    T = pm.shape[0]

    def body(pm_ref, pos_ref, invf_ref, gq_ref, gkv_ref, wuq_ref, wukv_ref, qc_ref, kc_ref, v_ref):
        cos, ssin, lane = _rope_tables(pos_ref[...], invf_ref[...])
        cq, _, _ = _rms_fwd(pm_ref[:, 0:MQR], gq_ref[...])
        ckv, _, _ = _rms_fwd(pm_ref[:, 512:768], gkv_ref[...])
        qf = _dot(cq.astype(BF), wuq_ref[...])
        kvf = _dot(ckv.astype(BF), wukv_ref[...])
        kr = _rope(pm_ref[:, 384:512], cos, ssin, lane, 1.0).astype(BF)
        for h in range(MH):
            qc_ref[:, 256 * h:256 * h + 128] = (QK_SCALE_LOG2 * qf[:, 128 * h:128 * h + 128]).astype(BF)
            qr = qf[:, 1024 + 128 * h:1024 + 128 * h + 128]
            qc_ref[:, 256 * h + 128:256 * h + 256] = (QK_SCALE_LOG2 * _rope(qr, cos, ssin, lane, 1.0)).astype(BF)
            kc_ref[:, 256 * h:256 * h + 128] = kvf[:, 128 * h:128 * h + 128].astype(BF)
            kc_ref[:, 256 * h + 128:256 * h + 256] = kr
        v_ref[...] = kvf[:, 1024:2048].astype(BF)

    full = lambda shp: pl.BlockSpec(shp, lambda i: (0,) * len(shp))
    row = lambda w: pl.BlockSpec((tm, w), lambda i: (i, 0))
    return _pallas_hbm(
        body, name="mla_prep_fwd",
        out_shape=(jax.ShapeDtypeStruct((T, MH * 256), BF), jax.ShapeDtypeStruct((T, MH * 256), BF),
                   jax.ShapeDtypeStruct((T, MH * MV), BF)),
        grid=(T // tm,),
        in_specs=[row(PM_W), row(1), full((1, 128)), full((1, MQR)), full((1, MKR)),
                  full((MQR, 2048)), full((MKR, 2048))],
        out_specs=(row(MH * 256), row(MH * 256), row(MH * MV)),
        compiler_params=_params(("parallel",)),
    )(*_in_hbm((pm, pos, invf, gq, gkv, wuq, wukv)))


def _mla_prep_bwd(pm, pos, invf, gq, gkv, wuq, wukv, dqc, dkc, dv, *, tm):
    T = pm.shape[0]

    def body(pm_ref, pos_ref, invf_ref, gq_ref, gkv_ref, wuq_ref, wukv_ref, dqc_ref, dkc_ref, dv_ref,
             dpm_ref, dwuq_ref, dwukv_ref, dgq_ref, dgkv_ref):
        @pl.when(pl.program_id(0) == 0)
        def _():
            dwuq_ref[...] = jnp.zeros_like(dwuq_ref)
            dwukv_ref[...] = jnp.zeros_like(dwukv_ref)
            dgq_ref[...] = jnp.zeros_like(dgq_ref)
            dgkv_ref[...] = jnp.zeros_like(dgkv_ref)

        cos, ssin, lane = _rope_tables(pos_ref[...], invf_ref[...])
        cq, cqh, cq_rstd = _rms_fwd(pm_ref[:, 0:MQR], gq_ref[...])
        ckv, ckvh, ckv_rstd = _rms_fwd(pm_ref[:, 512:768], gkv_ref[...])
        dqn, dqr, dkn = [], [], []
        dkr = jnp.zeros((tm, 128), F32)
        for h in range(MH):
            dqn.append(dqc_ref[:, 256 * h:256 * h + 128].astype(BF))
            dqr.append(_rope(dqc_ref[:, 256 * h + 128:256 * h + 256], cos, ssin, lane, -1.0).astype(BF))
            dkn.append(dkc_ref[:, 256 * h:256 * h + 128].astype(BF))
            dkr = dkr + dkc_ref[:, 256 * h + 128:256 * h + 256]
        dqf = jnp.concatenate(dqn + dqr, axis=1)
        dkvf = jnp.concatenate(dkn + [dv_ref[...].astype(BF)], axis=1)
        dwuq_ref[...] += _dot_tn(cq.astype(BF), dqf)
        dwukv_ref[...] += _dot_tn(ckv.astype(BF), dkvf)
        dcq, dgq = _rms_bwd(_dot_nt(dqf, wuq_ref[...]), cqh, cq_rstd, gq_ref[...])
        dckv, dgkv = _rms_bwd(_dot_nt(dkvf, wukv_ref[...]), ckvh, ckv_rstd, gkv_ref[...])
        dgq_ref[...] += dgq
        dgkv_ref[...] += dgkv
        dpm_ref[:, 0:MQR] = dcq.astype(BF)
        dpm_ref[:, 384:512] = _rope(dkr, cos, ssin, lane, -1.0).astype(BF)
        dpm_ref[:, 512:768] = dckv.astype(BF)

    full = lambda shp: pl.BlockSpec(shp, lambda i: (0,) * len(shp))
    row = lambda w: pl.BlockSpec((tm, w), lambda i: (i, 0))
    return _pallas_hbm(
        body, name="mla_prep_bwd",
        out_shape=(jax.ShapeDtypeStruct((T, PM_W), BF), jax.ShapeDtypeStruct((MQR, 2048), F32),
                   jax.ShapeDtypeStruct((MKR, 2048), F32), jax.ShapeDtypeStruct((1, MQR), F32),
                   jax.ShapeDtypeStruct((1, MKR), F32)),
        grid=(T // tm,),
        in_specs=[row(PM_W), row(1), full((1, 128)), full((1, MQR)), full((1, MKR)),
                  full((MQR, 2048)), full((MKR, 2048)), row(MH * 256), row(MH * 256), row(MH * MV)],
        out_specs=(row(PM_W), full((MQR, 2048)), full((MKR, 2048)), full((1, MQR)), full((1, MKR))),
        compiler_params=_params(("arbitrary",)),
    )(*_in_hbm((pm, pos, invf, gq, gkv, wuq, wukv, dqc, dkc, dv)))


def _flash_fwd(qc, kc, v, *, nseq, S, tq, ride=None):
    T = qc.shape[0]
    nq = S // tq
    hp = FLASH_HP_FWD

    def body(q_ref, k_ref, v_ref, o_ref, lse_ref):
        i = pl.program_id(2)
        causal = _iota((tq, tq), 0) >= _iota((tq, tq), 1)

        def step(j, carry, masked):
            rows = pl.ds(pl.multiple_of(j * tq, tq), tq)
            hs = range(hp)
            s = [_dot_nt(q_ref[:, 256 * hh:256 * hh + 256], k_ref[rows, 256 * hh:256 * hh + 256]) for hh in hs]
            p, stats = [], []
            for hh in hs:
                m, l, _ = carry[hh]
                sh = jnp.where(causal, s[hh], NEG) if masked else s[hh]
                m_new = jnp.maximum(m, jnp.max(sh, axis=-1, keepdims=True))
                ph = jnp.exp2(sh - m_new)
                a = jnp.exp2(m - m_new)
                stats.append((m_new, a * l + jnp.sum(ph, axis=-1, keepdims=True), a))
                p.append(ph.astype(BF))
            pv = [_dot(p[hh], v_ref[rows, MV * hh:MV * hh + MV]) for hh in hs]
            return tuple((stats[hh][0], stats[hh][1], stats[hh][2] * carry[hh][2] + pv[hh]) for hh in hs)

        init = ((jnp.full((tq, 1), NEG, F32), jnp.zeros((tq, 1), F32), jnp.zeros((tq, MV), F32)),) * hp
        carry = lax.fori_loop(0, i, lambda j, c: step(j, c, False), init)
        for hh, (m, l, acc) in enumerate(step(i, carry, True)):
            o_ref[:, MV * hh:MV * hh + MV] = (acc / l).astype(BF)
            lse_ref[:, 128 * hh:128 * hh + 128] = jnp.broadcast_to(m + jnp.log2(l), (tq, 128))

    return _call(
        body, name="flash_fwd", ride=ride, sem=("parallel", "parallel", "arbitrary"), args=(qc, kc, v),
        out_shape=(jax.ShapeDtypeStruct((T, MH * MV), BF), jax.ShapeDtypeStruct((T, MH * 128), F32)),
        grid=(nseq, MH // hp, nq),
        in_specs=[pl.BlockSpec((tq, 256 * hp), lambda b_, h, i: (b_ * nq + i, h)),
                  pl.BlockSpec((S, 256 * hp), lambda b_, h, i: (b_, h)),
                  pl.BlockSpec((S, MV * hp), lambda b_, h, i: (b_, h))],
        out_specs=(pl.BlockSpec((tq, MV * hp), lambda b_, h, i: (b_ * nq + i, h)),
                   pl.BlockSpec((tq, 128 * hp), lambda b_, h, i: (b_ * nq + i, h))))


def _flash_bwd(qc, kc, v, o, do, lse, *, nseq, S, tq, ride=None):
    T = qc.shape[0]
    nq = S // tq

    def body(q_ref, k_ref, v_ref, o_ref, do_ref, lse_ref, dq_ref, dk_ref, dv_ref, dq_scr, delta_scr):
        j = pl.program_id(2)

        @pl.when(j == 0)
        def _():
            dq_scr[...] = jnp.zeros_like(dq_scr)
            for hh in range(FLASH_HP):
                od = o_ref[:, MV * hh:MV * hh + MV].astype(F32) * do_ref[:, MV * hh:MV * hh + MV].astype(F32)
                delta_scr[:, 128 * hh:128 * hh + 128] = jnp.broadcast_to(jnp.sum(od, axis=-1, keepdims=True), (S, 128))

        causal = _iota((tq, tq), 0) >= _iota((tq, tq), 1)

        def step(i, carry, masked):
            rows = pl.ds(pl.multiple_of(i * tq, tq), tq)
            hs = range(FLASH_HP)
            qs = [slice(256 * hh, 256 * hh + 256) for hh in hs]
            vs = [slice(MV * hh, MV * hh + MV) for hh in hs]
            ls = [slice(128 * hh, 128 * hh + 1) for hh in hs]
            s = [_dot_nt(q_ref[rows, qs[hh]], k_ref[:, qs[hh]]) for hh in hs]
            dp = [_dot_nt(do_ref[rows, vs[hh]], v_ref[:, vs[hh]]) for hh in hs]
            pb, ds = [], []
            for hh in hs:
                p = jnp.exp2(s[hh] - lse_ref[rows, ls[hh]])
                if masked:
                    p = jnp.where(causal, p, 0.0)
                pb.append(p.astype(BF))
                ds.append((p * (dp[hh] - delta_scr[rows, ls[hh]])).astype(BF))
            dv = [carry[hh][1] + _dot_tn(pb[hh], do_ref[rows, vs[hh]]) for hh in hs]
            dk = [carry[hh][0] + _dot_tn(ds[hh], q_ref[rows, qs[hh]]) for hh in hs]
            for hh in hs:
                dq_scr[rows, qs[hh]] += _dot(ds[hh], k_ref[:, qs[hh]])
            return tuple((dk[hh], dv[hh]) for hh in hs)

        init = ((jnp.zeros((tq, 256), F32), jnp.zeros((tq, MV), F32)),) * FLASH_HP
        carry = step(j, init, True)
        carry = lax.fori_loop(j + 1, nq, lambda i, c: step(i, c, False), carry)
        for hh, (dk, dv) in enumerate(carry):
            dk_ref[:, 256 * hh:256 * hh + 256] = dk * (1.0 / LOG2E)
            dv_ref[:, MV * hh:MV * hh + MV] = dv

        @pl.when(j == nq - 1)
        def _():
            dq_ref[...] = dq_scr[...] * QK_SCALE

    hp = FLASH_HP
    seq = lambda w: pl.BlockSpec((S, w * hp), lambda b_, h, j: (b_, h))
    blk = lambda w: pl.BlockSpec((tq, w * hp), lambda b_, h, j: (b_ * nq + j, h))
    return _call(
        body, name="flash_bwd", ride=ride, sem=("parallel", "parallel", "arbitrary"), args=(qc, kc, v, o, do, lse),
        out_shape=(jax.ShapeDtypeStruct((T, MH * 256), F32), jax.ShapeDtypeStruct((T, MH * 256), F32),
                   jax.ShapeDtypeStruct((T, MH * MV), F32)),
        grid=(nseq, MH // hp, nq),
        in_specs=[seq(256), blk(256), blk(MV), seq(MV), seq(MV), seq(128)],
        out_specs=(seq(256), blk(256), blk(MV)),
        scratch_shapes=[pltpu.VMEM((S, 256 * hp), F32), pltpu.VMEM((S, 128 * hp), F32)])


def _ln_fwd(pre, g, b):
    mu = jnp.mean(pre, axis=-1, keepdims=True)
    xc = pre - mu
    rstd = lax.rsqrt(jnp.mean(xc * xc, axis=-1, keepdims=True) + LN_EPS)
    xhat = xc * rstd
    return xhat * g + b, xhat, rstd


def _ln_bwd(dy, xhat, rstd, g):
    dxh = dy * g
    dx = rstd * (dxh - jnp.mean(dxh, axis=-1, keepdims=True) - xhat * jnp.mean(dxh * xhat, axis=-1, keepdims=True))
    return dx, jnp.sum(dy * xhat, axis=0, keepdims=True), jnp.sum(dy, axis=0, keepdims=True)


def _post_attn_fwd(zg, attn, pt, x, wgo, wmo, wout, g1, b1, *, tm, ride=None):
    T = x.shape[0]

    def body(zg_ref, at_ref, pt_ref, x_ref, wgo_ref, wmo_ref, wout_ref, g_ref, b_ref,
             yg_ref, ym_ref, mix_ref, pre_ref, hb_ref):
        yg = _dot(zg_ref[...], wgo_ref[...])
        ym = _dot(at_ref[...], wmo_ref[...])
        mix = (_sigmoid(pt_ref[:, 0:D].astype(F32)) * yg + _sigmoid(pt_ref[:, D:2 * D].astype(F32)) * ym).astype(BF)
        pre = ALPHA * x_ref[...] + _dot(mix, wout_ref[...])
        h, _, _ = _ln_fwd(pre, g_ref[...], b_ref[...])
        yg_ref[...] = yg.astype(BF)
        ym_ref[...] = ym.astype(BF)
        mix_ref[...] = mix
        pre_ref[...] = pre
        hb_ref[...] = h.astype(BF)

    full = lambda shp: pl.BlockSpec(shp, lambda i: (0,) * len(shp))
    row = lambda w: pl.BlockSpec((tm, w), lambda i: (i, 0))
    sd = lambda dt: jax.ShapeDtypeStruct((T, D), dt)
    return _call(
        body, name="post_attn_fwd", ride=ride, sem=("parallel",), args=(zg, attn, pt, x, wgo, wmo, wout, g1, b1),
        out_shape=(sd(BF), sd(BF), sd(BF), sd(F32), sd(BF)),
        grid=(T // tm,),
        in_specs=[row(D), row(D), row(PT_W), row(D), full((D, D)), full((D, D)), full((D, D)),
                  full((1, D)), full((1, D))],
        out_specs=(row(D),) * 5)


def _post_attn_bwd(dh, pre, pt, yg, ym, wgo, wmo, wout, g1, *, tm):
    T = dh.shape[0]

    def body(dh_ref, pre_ref, pt_ref, yg_ref, ym_ref, wgo_ref, wmo_ref, wout_ref, g_ref,
             dx_ref, dpreb_ref, dpt_ref, dygb_ref, dymb_ref, dzg_ref, dat_ref, dg_ref, db_ref):
        @pl.when(pl.program_id(0) == 0)
        def _():
            dg_ref[...] = jnp.zeros_like(dg_ref)
            db_ref[...] = jnp.zeros_like(db_ref)

        pre = pre_ref[...]
        mu = jnp.mean(pre, axis=-1, keepdims=True)
        xc = pre - mu
        rstd = lax.rsqrt(jnp.mean(xc * xc, axis=-1, keepdims=True) + LN_EPS)
        dpre, dg, db = _ln_bwd(dh_ref[...], xc * rstd, rstd, g_ref[...])
        dg_ref[...] += dg
        db_ref[...] += db
        dx_ref[...] = ALPHA * dpre
        dpreb = dpre.astype(BF)
        dpreb_ref[...] = dpreb
        dmix = _dot_nt(dpreb, wout_ref[...])
        sa = _sigmoid(pt_ref[:, 0:D].astype(F32))
        sb = _sigmoid(pt_ref[:, D:2 * D].astype(F32))
        dpt_ref[:, 0:D] = (dmix * yg_ref[...].astype(F32) * (sa * (1.0 - sa))).astype(BF)
        dpt_ref[:, D:2 * D] = (dmix * ym_ref[...].astype(F32) * (sb * (1.0 - sb))).astype(BF)
        dyg = (dmix * sa).astype(BF)
        dym = (dmix * sb).astype(BF)
        dygb_ref[...] = dyg
        dymb_ref[...] = dym
        dzg_ref[...] = _dot_nt(dyg, wgo_ref[...]).astype(BF)
        dat_ref[...] = _dot_nt(dym, wmo_ref[...]).astype(BF)

    full = lambda shp: pl.BlockSpec(shp, lambda i: (0,) * len(shp))
    row = lambda w: pl.BlockSpec((tm, w), lambda i: (i, 0))
    sd = lambda w, dt: jax.ShapeDtypeStruct((T, w), dt)
    return _pallas_hbm(
        body, name="post_attn_bwd",
        out_shape=(sd(D, F32), sd(D, BF), sd(PT_W, BF), sd(D, BF), sd(D, BF), sd(D, BF), sd(D, BF),
                   jax.ShapeDtypeStruct((1, D), F32), jax.ShapeDtypeStruct((1, D), F32)),
        grid=(T // tm,),
        in_specs=[row(D), row(D), row(PT_W), row(D), row(D), full((D, D)), full((D, D)), full((D, D)),
                  full((1, D))],
        out_specs=(row(D), row(D), row(PT_W), row(D), row(D), row(D), row(D), full((1, D)), full((1, D))),
        compiler_params=_params(("arbitrary",)),
    )(*_in_hbm((dh, pre, pt, yg, ym, wgo, wmo, wout, g1)))


def _shift_down(u, prev, k):
    r = pltpu.roll(u, k, 0)
    p = pltpu.roll(prev, k, 0)
    head = jnp.where(_iota(p.shape, 0) < k, p, r[0:8, :])
    return jnp.concatenate([head, r[8:, :]], axis=0)


def _conv3(u, prev, w_ref, b_ref):
    return (w_ref[0:1, :] * _shift_down(u, prev, 2) + w_ref[1:2, :] * _shift_down(u, prev, 1)
            + w_ref[2:3, :] * u + b_ref[...])


def _ffn_up_fwd(hb, wug, wuv, cw, cb, *, S, tm, tn):
    T = hb.shape[0]
    nj, nbs = DFF // tn, S // tm

    def body(h_ref, wg_ref, wv_ref, cwg_ref, cwv_ref, cbg_ref, cbv_ref,
             ug_ref, uv_ref, ucg_ref, ucv_ref, f_ref, pg_scr, pv_scr):
        @pl.when(pl.program_id(1) % nbs == 0)
        def _():
            pg_scr[...] = jnp.zeros_like(pg_scr)
            pv_scr[...] = jnp.zeros_like(pv_scr)

        h = h_ref[...]
        ug = _dot(h, wg_ref[...])
        uv = _dot(h, wv_ref[...])
        ucg = _conv3(ug, pg_scr[...], cwg_ref, cbg_ref)
        ucv = _conv3(uv, pv_scr[...], cwv_ref, cbv_ref)
        pg_scr[...] = ug[tm - 8:, :]
        pv_scr[...] = uv[tm - 8:, :]
        ug_ref[...] = ug.astype(BF)
        uv_ref[...] = uv.astype(BF)
        ucg_ref[...] = ucg
        ucv_ref[...] = ucv
        f_ref[...] = (ucg * _sigmoid(ucg) * ucv).astype(BF)

    tile = pl.BlockSpec((tm, tn), lambda j, i: (i, j))
    return _pallas_hbm(
        body, name="ffn_up_fwd",
        out_shape=(jax.ShapeDtypeStruct((T, DFF), BF), jax.ShapeDtypeStruct((T, DFF), BF),
                   jax.ShapeDtypeStruct((T, DFF), F32), jax.ShapeDtypeStruct((T, DFF), F32),
                   jax.ShapeDtypeStruct((T, DFF), BF)),
        grid=(nj, T // tm),
        in_specs=[pl.BlockSpec((tm, D), lambda j, i: (i, 0)),
                  pl.BlockSpec((D, tn), lambda j, i: (0, j)), pl.BlockSpec((D, tn), lambda j, i: (0, j)),
                  pl.BlockSpec((3, tn), lambda j, i: (0, j)), pl.BlockSpec((3, tn), lambda j, i: (0, j + nj)),
                  pl.BlockSpec((1, tn), lambda j, i: (0, j)), pl.BlockSpec((1, tn), lambda j, i: (0, j + nj))],
        out_specs=(tile, tile, tile, tile, tile),
        scratch_shapes=[pltpu.VMEM((8, tn), F32), pltpu.VMEM((8, tn), F32)],
        compiler_params=_params(("parallel", "arbitrary")),
    )(*_in_hbm((hb, wug, wuv, cw, cw, cb, cb)))


def _ffn_bwd(dpreb, wd, ug, uv, ucg, ucv, cw, *, S, tm, tn):
    T = dpreb.shape[0]
    nj, nb, nbs = DFF // tn, T // tm, S // tm
    r_, c_ = lax.broadcasted_iota(jnp.int32, (tm, tm), 0), lax.broadcasted_iota(jnp.int32, (tm, tm), 1)
    s1, s2 = (c_ == r_ + 1).astype(BF), (c_ == r_ + 2).astype(BF)

    def body(dp_ref, wd_ref, ug_ref, uv_ref, ucg_ref, ucv_ref, cwg_ref, cwv_ref, s1_ref, s2_ref,
             dug_ref, duv_ref, dcg_ref, dcv_ref, ng_scr, nv_scr):
        ii = pl.program_id(1)
        i = nb - 1 - ii
        tail_row = _iota((8, tn), 0)

        @pl.when(ii == 0)
        def _():
            dcg_ref[...] = jnp.zeros_like(dcg_ref)
            dcv_ref[...] = jnp.zeros_like(dcv_ref)

        @pl.when(i % nbs == nbs - 1)
        def _():
            ng_scr[...] = jnp.zeros_like(ng_scr)
            nv_scr[...] = jnp.zeros_like(nv_scr)

        df = _dot_nt(dp_ref[...], wd_ref[...])
        ucg = ucg_ref[...]
        sg = _sigmoid(ucg)
        ducg = df * ucv_ref[...] * (sg * (1.0 + ucg * (1.0 - sg)))
        ducv = df * (ucg * sg)

        def finish(duc, u_ref, w, nxt_scr, du_ref, dc_ref):
            nxt = nxt_scr[...]
            db = duc.astype(BF)

            def shifted(s_ref, k):
                r = _dot(s_ref[...], db)
                tail = jnp.where(tail_row >= 8 - k, pltpu.roll(nxt, 8 - k, 0), r[tm - 8:, :])
                return jnp.concatenate([r[:tm - 8, :], tail], axis=0)

            up1 = shifted(s1_ref, 1)
            up2 = shifted(s2_ref, 2)
            du_ref[...] = (w[2:3, :] * duc + w[1:2, :] * up1 + w[0:1, :] * up2).astype(BF)
            nxt_scr[...] = duc[0:8, :]
            u = u_ref[...].astype(F32)
            for row, z in enumerate((u * up2, u * up1, u * duc, duc)):
                dc_ref[row:row + 1, :] += jnp.sum(z, axis=0, keepdims=True)

        finish(ducg, ug_ref, cwg_ref, ng_scr, dug_ref, dcg_ref)
        finish(ducv, uv_ref, cwv_ref, nv_scr, duv_ref, dcv_ref)

    tile = pl.BlockSpec((tm, tn), lambda j, ii: (nb - 1 - ii, j))
    acc = pl.BlockSpec((8, tn), lambda j, ii: (0, j))
    return _pallas_hbm(
        body, name="ffn_bwd",
        out_shape=(jax.ShapeDtypeStruct((T, DFF), BF), jax.ShapeDtypeStruct((T, DFF), BF),
                   jax.ShapeDtypeStruct((8, DFF), F32), jax.ShapeDtypeStruct((8, DFF), F32)),
        grid=(nj, nb),
        in_specs=[pl.BlockSpec((tm, D), lambda j, ii: (nb - 1 - ii, 0)),
                  pl.BlockSpec((tn, D), lambda j, ii: (j, 0)),
                  tile, tile, tile, tile,
                  pl.BlockSpec((3, tn), lambda j, ii: (0, j)), pl.BlockSpec((3, tn), lambda j, ii: (0, j + nj)),
                  pl.BlockSpec((tm, tm), lambda j, ii: (0, 0)), pl.BlockSpec((tm, tm), lambda j, ii: (0, 0))],
        out_specs=(tile, tile, acc, acc),
        scratch_shapes=[pltpu.VMEM((8, tn), F32), pltpu.VMEM((8, tn), F32)],
        compiler_params=_params(("parallel", "arbitrary")),
    )(*_in_hbm((dpreb, wd, ug, uv, ucg, ucv, cw, cw, s1, s2)))


def _down_ln2_loss(f_in, wd, pre1, target, g1, b1, g2, b2, *, tm):
    T = pre1.shape[0]

    def body(f_ref, wd_ref, p1_ref, t_ref, g1_ref, b1_ref, g_ref, b_ref, dpb_ref, dh_ref, loss_ref, dg_ref, db_ref):
        @pl.when(pl.program_id(0) == 0)
        def _():
            loss_ref[...] = jnp.zeros_like(loss_ref)
            dg_ref[...] = jnp.zeros_like(dg_ref)
            db_ref[...] = jnp.zeros_like(db_ref)

        halves = [pl.ds(s * (tm // 2), tm // 2) for s in range(2)]
        f = [_dot(f_ref[hs, :], wd_ref[...]) for hs in halves]
        for hs, fh in zip(halves, f):
            h, _, _ = _ln_fwd(p1_ref[hs, :], g1_ref[...], b1_ref[...])
            pre = ALPHA * h + fh
            out, xhat, rstd = _ln_fwd(pre, g_ref[...], b_ref[...])
            diff = out - t_ref[hs, :]
            loss_ref[...] += 0.5 * jnp.sum(jnp.mean(diff * diff, axis=-1, keepdims=True))
            dpre, dg, db = _ln_bwd(diff * (1.0 / D), xhat, rstd, g_ref[...])
            dg_ref[...] += dg
            db_ref[...] += db
            dpb_ref[hs, :] = dpre.astype(BF)
            dh_ref[hs, :] = ALPHA * dpre

    full = lambda shp: pl.BlockSpec(shp, lambda i: (0,) * len(shp))
    row = lambda w: pl.BlockSpec((tm, w), lambda i: (i, 0))
    return _pallas_hbm(
        body, name="down_ln2_loss",
        out_shape=(jax.ShapeDtypeStruct((T, D), BF), jax.ShapeDtypeStruct((T, D), F32),
                   jax.ShapeDtypeStruct((8, 128), F32), jax.ShapeDtypeStruct((1, D), F32),
                   jax.ShapeDtypeStruct((1, D), F32)),
        grid=(T // tm,),
        in_specs=[row(DFF), full((DFF, D)), row(D), row(D), full((1, D)), full((1, D)), full((1, D)), full((1, D))],
        out_specs=(row(D), row(D), full((8, 128)), full((1, D)), full((1, D))),
        compiler_params=_params(("arbitrary",)),
    )(*_in_hbm((f_in, wd, pre1, target, g1, b1, g2, b2)))


def _adamw(parts, w, m, v, *, name):
    n, R, C = parts.shape
    tr, tc = R, C
    for cand in range(min(R, 256), 15, -1):
        if R % cand == 0 and cand % 16 == 0:
            tr = cand
            break
    if tr == R and R * C > 65536 and C % 256 == 0:
        tc = 256
    c1 = 1.0 - ADAM_B1 ** ADAM_STEP
    c2 = 1.0 - ADAM_B2 ** ADAM_STEP

    def body(p_ref, w_ref, m_ref, v_ref, g_ref, d_ref, nm_ref, nv_ref):
        g = p_ref[0].astype(F32)
        for s in range(1, n):
            g = g + p_ref[s].astype(F32)
        nm = ADAM_B1 * m_ref[...] + (1.0 - ADAM_B1) * g
        nv = ADAM_B2 * v_ref[...] + (1.0 - ADAM_B2) * (g * g)
        g_ref[...] = g
        nm_ref[...] = nm
        nv_ref[...] = nv
        d_ref[...] = -ADAM_LR * ((nm / c1) / (jnp.sqrt(nv / c2) + ADAM_EPS) + ADAM_WD * w_ref[...])

    blk = pl.BlockSpec((tr, tc), lambda i, j: (i, j))
    sd = jax.ShapeDtypeStruct((R, C), F32)
    return _pallas_hbm(
        body, name=name,
        out_shape=(sd, sd, sd, sd),
        grid=(R // tr, C // tc),
        in_specs=[pl.BlockSpec((n, tr, tc), lambda i, j: (0, i, j)), blk, blk, blk],
        out_specs=(blk, blk, blk, blk),
        compiler_params=_params(("parallel", "parallel")),
    )(*_in_hbm((parts, w, m, v)))


class _Exchange:
    def __init__(self, items):
        self.items = [(src if sc else [(src, 0)], sc) for src, sc in items]
        self.arrays = [arr for srcs, _ in self.items for arr, _ in srcs]
        self.n = len(self.items)
        self.n_in = len(self.arrays)

    def out_shape(self):
        return tuple(jax.ShapeDtypeStruct((NDEV,) + (srcs[0][0].shape[1:] if sc else srcs[0][0].shape),
                                          srcs[0][0].dtype) for srcs, sc in self.items)

    def scratch(self):
        return [pltpu.SemaphoreType.DMA((self.n, NDEV - 1)), pltpu.SemaphoreType.DMA((self.n, NDEV - 1)),
                pltpu.SemaphoreType.DMA((self.n,))]

    def _emit(self, ins, outs, sems, phase):
        send_sems, recv_sems, loc_sems = sems
        x, y, c = lax.axis_index("x"), lax.axis_index("y"), lax.axis_index("c")
        me = 4 * x + 2 * y + c
        flip = lambda p, d: 1 - p if d else p

        def inside(p, lo, n):
            return None if (lo, n) == (0, NDEV) else jnp.logical_and(p >= lo, p < lo + n)

        def when(cond, fn):
            if cond is None:
                fn()
            else:
                pl.when(cond)(fn)

        pos = 0
        for a, (srcs, sc) in enumerate(self.items):
            refs = ins[pos:pos + len(srcs)]
            pos += len(srcs)
            ranges = [(lo, arr.shape[0]) if sc else (0, NDEV) for arr, lo in srcs]
            mine = [inside(me, lo, n) for lo, n in ranges]
            i_receive = None if None in mine else functools.reduce(jnp.logical_or, mine)
            for ref, (lo, n), cond in zip(refs, ranges, mine):
                def local(ref=ref, lo=lo):
                    cp = pltpu.make_async_copy(ref.at[me - lo] if sc else ref, outs[a].at[me], loc_sems.at[a])
                    cp.start() if phase == 0 else cp.wait()
                if phase != 1:
                    when(cond, local)
            for k in range(1, NDEV):
                px, py, pc = flip(x, k & 4), flip(y, k & 2), flip(c, k & 1)
                peer = 4 * px + 2 * py + pc
                mk = functools.partial(pltpu.make_async_remote_copy,
                                       send_sem=send_sems.at[a, k - 1], recv_sem=recv_sems.at[a, k - 1],
                                       device_id=(px, py, pc), device_id_type=MESH_ID)
                if phase == 1:
                    def arrival(mk=mk, peer=peer):
                        mk(src_ref=refs[0].at[0] if sc else refs[0], dst_ref=outs[a].at[peer]).wait_recv()
                    when(i_receive, arrival)
                    continue
                for ref, (lo, n) in zip(refs, ranges):
                    def send(mk=mk, ref=ref, lo=lo, peer=peer):
                        cp = mk(src_ref=ref.at[peer - lo] if sc else ref, dst_ref=outs[a].at[me])
                        cp.start() if phase == 0 else cp.wait_send()
                    when(inside(peer, lo, n), send)

    def start(self, ins, outs, sems):
        self._emit(ins, outs, sems, 0)

    def wait(self, ins, outs, sems):
        self._emit(ins, outs, sems, 1)
        self._emit(ins, outs, sems, 2)


def _call(body, *, name, grid, in_specs, out_specs, out_shape, args, scratch_shapes=(), sem=None, ride=None):
    if ride is None:
        return pl.pallas_call(body, name=name, grid=grid, in_specs=list(in_specs), out_specs=tuple(out_specs),
                              out_shape=_out_hbm(out_shape), scratch_shapes=list(scratch_shapes),
                              compiler_params=_params(sem))(*_in_hbm(args))
    n_in, n_out, n_scr, ne, ne_in = len(args), len(out_shape), len(scratch_shapes), ride.n, ride.n_in

    def ride_body(*refs):
        ins, ex_in = refs[:n_in], refs[n_in:n_in + ne_in]
        o0 = n_in + ne_in
        outs, ex_out = refs[o0:o0 + n_out], refs[o0 + n_out:o0 + n_out + ne]
        scr = refs[o0 + n_out + ne:o0 + n_out + ne + n_scr]
        sems = refs[o0 + n_out + ne + n_scr:]
        first = functools.reduce(jnp.logical_and, [pl.program_id(d) == 0 for d in range(len(grid))])
        last = functools.reduce(jnp.logical_and, [pl.program_id(d) == grid[d] - 1 for d in range(len(grid))])

        @pl.when(first)
        def _():
            ride.start(ex_in, ex_out, sems)

        body(*ins, *outs, *scr)

        @pl.when(last)
        def _():
            ride.wait(ex_in, ex_out, sems)

    anyspec = pl.BlockSpec(memory_space=pl.ANY)
    res = pl.pallas_call(
        ride_body, name=name, grid=grid,
        in_specs=list(in_specs) + [anyspec] * ne_in,
        out_specs=tuple(out_specs) + (anyspec,) * ne,
        out_shape=_out_hbm(tuple(out_shape) + ride.out_shape()),
        scratch_shapes=list(scratch_shapes) + ride.scratch(),
        compiler_params=_params(("arbitrary",) * len(grid)),
    )(*_in_hbm(args), *_in_hbm(ride.arrays))
    return tuple(res[:n_out]), tuple(res[n_out:])


def _gather_two_level(arrays, *, name):
    n = len(arrays)

    def body(*refs):
        ins, outs = refs[:n], refs[n:2 * n]
        send_sems, recv_sems, loc_sems = refs[2 * n:]
        x, y, c = lax.axis_index("x"), lax.axis_index("y"), lax.axis_index("c")
        sibling = (x, y, 1 - c)
        chips = [(1 - x, y), (x, 1 - y), (1 - x, 1 - y)]
        idx = lambda px, py, pc: 4 * px + 2 * py + pc
        me = idx(x, y, c)

        def copy(a, k, block, to, src=None):
            return pltpu.make_async_remote_copy(
                src_ref=outs[a].at[block] if src is None else src, dst_ref=outs[a].at[block],
                send_sem=send_sems.at[a, k], recv_sem=recv_sems.at[a, k], device_id=to, device_id_type=MESH_ID)

        local = [pltpu.make_async_copy(ins[a], outs[a].at[me], loc_sems.at[a]) for a in range(n)]
        sent = []
        for a in range(n):
            sent.append(copy(a, 0, me, sibling, src=ins[a]))
            sent += [copy(a, 1 + j, me, (*chip, c), src=ins[a]) for j, chip in enumerate(chips)]
        for cp in local + sent:
            cp.start()
        for j, chip in enumerate(chips):
            for a in range(n):
                copy(a, 1 + j, idx(*chip, c), sibling).wait_recv()
                passed = copy(a, 4 + j, idx(*chip, c), sibling)
                passed.start()
                sent.append(passed)
        for a in range(n):
            copy(a, 0, idx(x, y, 1 - c), sibling).wait_recv()
            for j, chip in enumerate(chips):
                copy(a, 4 + j, idx(*chip, 1 - c), sibling).wait_recv()
        for cp in sent:
            cp.wait_send()
        for cp in local:
            cp.wait()

    anyspec = pl.BlockSpec(memory_space=pl.ANY)
    return _pallas_hbm(
        body, name=name,
        out_shape=tuple(jax.ShapeDtypeStruct((NDEV,) + a.shape, a.dtype) for a in arrays),
        in_specs=[anyspec] * n, out_specs=(anyspec,) * n,
        scratch_shapes=[pltpu.SemaphoreType.DMA((n, NDEV - 1)), pltpu.SemaphoreType.DMA((n, NDEV - 1)),
                        pltpu.SemaphoreType.DMA((n,))],
    )(*_in_hbm(arrays))


def _tri_consts():
    r = lax.broadcasted_iota(jnp.int32, (GC, GC), 0)
    c = lax.broadcasted_iota(jnp.int32, (GC, GC), 1)
    return (r >= c).astype(BF), (r <= c).astype(BF)


def _local_step(x, positions, target, w, hooks=None):
    g = {}

    def run(host, fn, *a, **kw):
        h = None if hooks is None else hooks.get(host)
        if h is None:
            return fn(*a, **kw)
        out, received = fn(*a, ride=_Exchange(h[0](w, g)), **kw)
        h[1](received, w, g)
        return out

    nseq, S, _ = x.shape
    T = nseq * S
    tm = min(TOKEN_TM, S)
    tq = min(FLASH_TQ, S)
    x2 = x.reshape(T, D)
    pos = positions.reshape(T, 1)
    half = ROPE // 2
    inv = THETA ** (-jnp.arange(half, dtype=F32) / half)
    invf = jnp.concatenate([inv, inv, jnp.zeros((64,), F32)]).reshape(1, 128)
    ltri, utri = _tri_consts()

    pt, xb = _matmul(x2, w["w_tt"], "nt", name="proj_t", out_dtype=BF, tm=1024, tn=1024, tk=1024, emit_a=True)
    pg = run("proj_g", _matmul, xb, w["w_gt"], "nt", name="proj_g", tm=1024, tn=640, tk=1024)
    pm = _matmul(xb, w["w_mt"], "nt", name="proj_m", tm=1024, tn=768, tk=1024)
    o, zg, states = run("gla_fwd", _gla_fwd, pg, w["wg"], w["bg"], w["gn"], ltri, nseq=nseq, S=S, tm=tm)
    qc, kc, v = _mla_prep_fwd(pm, pos, invf, w["gq"], w["gkv"], w["wuq"], w["wukv"], tm=tm)
    attn, lse = run("flash_fwd", _flash_fwd, qc, kc, v, nseq=nseq, S=S, tq=tq)
    yg, ym, mix, pre1, h1b = run("post_attn_fwd", _post_attn_fwd, zg, attn, pt, x2, w["wgo"], w["wmo"], w["wout"],
                                 w["g1"], w["b1"], tm=tm)
    ug, uv, ucg, ucv, f_in = _ffn_up_fwd(h1b, w["wug"], w["wuv"], w["cw"], w["cb"], S=S, tm=tm, tn=FFN_TN)
    dpre2b, dh1, loss8, dg2, db2 = _down_ln2_loss(f_in, w["wd"], pre1, target.reshape(T, D), w["g1"], w["b1"],
                                                  w["g2"], w["b2"], tm=min(2 * tm, S))

    dug, duv, dcg, dcv = _ffn_bwd(dpre2b, w["wd"], ug, uv, ucg, ucv, w["cw"], S=S, tm=tm, tn=FFN_TN)
    g["g2"], g["b2"], g["loss"] = dg2, db2, loss8[0:1, 0:1]
    g["cw"] = jnp.concatenate([dcg[0:3], dcv[0:3]], axis=1)
    g["cb"] = jnp.concatenate([dcg[3:4], dcv[3:4]], axis=1)
    g["wd"] = _matmul(f_in, dpre2b, "tn", name="dw_down", out_dtype=BF, tm=1408, tn=1024, tk=1024)
    g["wugt"] = _matmul(dug, h1b, "tn", name="dw_up_g", out_dtype=BF, tm=1408, tn=1024, tk=1024)
    g["wuvt"] = _matmul(duv, h1b, "tn", name="dw_up_v", out_dtype=BF, tm=1408, tn=1024, tk=1024)
    dh1 = _matmul(dug, w["wugt"], "nn", name="dh1_g", c_in=dh1, tm=1024, tn=1024, tk=1408)
    dh1 = _matmul(duv, w["wuvt"], "nn", name="dh1_v", c_in=dh1, tm=1024, tn=1024, tk=1408)
    dx, dpre1b, dpt, dygb, dymb, dzg, dattn, dg1, db1 = _post_attn_bwd(
        dh1, pre1, pt, yg, ym, w["wgo"], w["wmo"], w["wout"], w["g1"], tm=tm)
    g["g1"], g["b1"] = dg1, db1
    g["wout"] = _matmul(mix, dpre1b, "tn", name="dw_out", out_dtype=BF, tm=1024, tn=1024, tk=1024)
    g["wgo"] = _matmul(zg, dygb, "tn", name="dw_gla_o", out_dtype=BF, tm=1024, tn=1024, tk=1024)
    g["wmo"] = _matmul(attn, dymb, "tn", name="dw_mla_o", out_dtype=BF, tm=1024, tn=1024, tk=1024)
    dqc, dkc, dv = run("flash_bwd", _flash_bwd, qc, kc, v, attn, dattn, lse, nseq=nseq, S=S, tq=tq)
    dpm, g["wuq"], g["wukv"], g["gq"], g["gkv"] = _mla_prep_bwd(
        pm, pos, invf, w["gq"], w["gkv"], w["wuq"], w["wukv"], dqc, dkc, dv, tm=tm)
    g["w_mt"] = _matmul(dpm, xb, "tn", name="dw_in_m", out_dtype=BF, tm=768, tn=1024, tk=1024)
    g["w_tt"] = _matmul(dpt, xb, "tn", name="dw_in_t", out_dtype=BF, tm=1024, tn=1024, tk=1024)
    dpg, g["wg"], g["bg"], g["gn"] = run("gla_bwd", _gla_bwd, pg, w["wg"], w["bg"], w["gn"], ltri, utri, o, states,
                                         dzg, nseq=nseq, S=S, tm=tm)
    g["w_gt"] = _matmul(dpg, xb, "tn", name="dw_in_g", out_dtype=BF, tm=640, tn=1024, tk=1024)
    dx = run("dx", _matmul_sum, dx, [(dpg, w["w_gt"], 640), (dpm, w["w_mt"], 768)], name="dx_gm")
    dx = _matmul_sum(dx, [(dpt, w["w_tt"], 1024)], name="dx_t")
    return loss8[0, 0], dx.reshape(nseq, S, D), g


_IN_SPLITS = (512, 512, 1024, 16, 1024, 384, 256, 64, 1024, 1024)


def _w_in_to_groups(wt):
    offs = [0]
    for s in _IN_SPLITS:
        offs.append(offs[-1] + s)
    q, k, v, r, og, cq, ckv, kr, ga, gb = [wt[offs[i]:offs[i + 1]] for i in range(10)]
    z = lambda n: jnp.zeros((n, wt.shape[1]), wt.dtype)
    return (jnp.concatenate([q, k, v, og, r, z(112)], axis=0),
            jnp.concatenate([cq, kr, z(64), ckv], axis=0),
            jnp.concatenate([ga, gb], axis=0))


W_IN_BLOCK = sum(_IN_SPLITS) // NDEV
_KV_LATENT_ROW = sum(_IN_SPLITS[:6])
_W_IN_LO = 5
_W_IN_SPLIT = _W_IN_LO * W_IN_BLOCK - _KV_LATENT_ROW


def _w_in_rows_lo(g_g, g_m):
    q, k, v, og, r = g_g[0:512], g_g[512:1024], g_g[1024:2048], g_g[2048:3072], g_g[3072:3088]
    return jnp.concatenate([q, k, v, r, og, g_m[0:384], g_m[512:768]], axis=0)[:_W_IN_LO * W_IN_BLOCK]


def _w_in_rows_hi(g_m, g_t):
    return jnp.concatenate([g_m[512:768], g_m[384:448], g_t], axis=0)[_W_IN_SPLIT:]


def _uq_to_kernel(wuq):
    w3 = wuq.reshape(MQR, MH, NOPE + ROPE)
    rope = jnp.concatenate([w3[:, :, NOPE:], jnp.zeros((MQR, MH, 64), wuq.dtype)], axis=2)
    return jnp.concatenate([w3[:, :, :NOPE].reshape(MQR, MH * 128), rope.reshape(MQR, MH * 128)], axis=1)


def _uq_from_kernel(g):
    nope = g[:, :1024].reshape(MQR, MH, 128)
    rope = g[:, 1024:].reshape(MQR, MH, 128)[:, :, :ROPE]
    return jnp.concatenate([nope, rope], axis=2)


def _ukv_to_kernel(wukv):
    w3 = wukv.reshape(MKR, MH, NOPE + MV)
    return jnp.concatenate([w3[:, :, :NOPE].reshape(MKR, MH * 128), w3[:, :, NOPE:].reshape(MKR, MH * 128)], axis=1)


def _ukv_from_kernel(g):
    return jnp.concatenate([g[:, :1024].reshape(MKR, MH, 128), g[:, 1024:].reshape(MKR, MH, 128)], axis=2)


def _cols_gathered(a):
    return a.transpose(1, 0, 2).reshape(a.shape[1], NDEV * a.shape[2])


def _cols_scattered(a):
    R = a.shape[0]
    return a.reshape(R, NDEV, a.shape[1] // NDEV).transpose(1, 0, 2)


_SMALL = (("gla_b_gate", 512), ("gla_norm_g", 256), ("mla_q_norm_g", 384), ("mla_kv_norm_g", 256),
          ("ln1_g", 1024), ("ln1_b", 1024), ("conv_b", 5632), ("ln2_g", 1024), ("ln2_b", 1024))
_SMALL_ROWS = 88
_SMALL_USED = sum(sz for _, sz in _SMALL)


def _pack_small(d):
    flat = jnp.concatenate([d[n].reshape(-1) for n, _ in _SMALL] + ([d['loss'].reshape(-1)] if 'loss' in d else []))
    return jnp.pad(flat, (0, _SMALL_ROWS * 128 - flat.shape[0])).reshape(_SMALL_ROWS, 128)


def _unpack_small(a):
    flat = a.reshape(-1)
    out, off = {}, 0
    for n, sz in _SMALL:
        out[n] = flat[off:off + sz].reshape(1, sz)
        off += sz
    return out


_NAMES = ['w_in', 'gla_w_gate_up', 'gla_b_gate', 'gla_norm_g', 'w_gla_o', 'mla_q_norm_g', 'mla_w_uq',
          'mla_kv_norm_g', 'mla_w_ukv', 'w_mla_o', 'w_out', 'ln1_g', 'ln1_b', 'w_up', 'conv_w', 'conv_b',
          'w_down', 'ln2_g', 'ln2_b']
_SHARDED = ['w_in', 'w_up', 'w_down', 'w_gla_o', 'w_mla_o', 'w_out', 'mla_w_uq', 'mla_w_ukv', 'gla_w_gate_up',
            'conv_w']


def kernel(x, positions, w_in, gla_w_gate_up, gla_b_gate, gla_norm_g, w_gla_o, mla_q_norm_g, mla_w_uq, mla_kv_norm_g, mla_w_ukv, w_mla_o, w_out, ln1_g, ln1_b, w_up, conv_w, conv_b, w_down, ln2_g, ln2_b, loss_target, m_w_in, m_gla_w_gate_up, m_gla_b_gate, m_gla_norm_g, m_w_gla_o, m_mla_q_norm_g, m_mla_w_uq, m_mla_kv_norm_g, m_mla_w_ukv, m_w_mla_o, m_w_out, m_ln1_g, m_ln1_b, m_w_up, m_conv_w, m_conv_b, m_w_down, m_ln2_g, m_ln2_b, v_w_in, v_gla_w_gate_up, v_gla_b_gate, v_gla_norm_g, v_w_gla_o, v_mla_q_norm_g, v_mla_w_uq, v_mla_kv_norm_g, v_mla_w_ukv, v_w_mla_o, v_w_out, v_ln1_g, v_ln1_b, v_w_up, v_conv_w, v_conv_b, v_w_down, v_ln2_g, v_ln2_b):
    W = dict(w_in=w_in, gla_w_gate_up=gla_w_gate_up, gla_b_gate=gla_b_gate, gla_norm_g=gla_norm_g, w_gla_o=w_gla_o, mla_q_norm_g=mla_q_norm_g, mla_w_uq=mla_w_uq, mla_kv_norm_g=mla_kv_norm_g, mla_w_ukv=mla_w_ukv, w_mla_o=w_mla_o, w_out=w_out, ln1_g=ln1_g, ln1_b=ln1_b, w_up=w_up, conv_w=conv_w, conv_b=conv_b, w_down=w_down, ln2_g=ln2_g, ln2_b=ln2_b)
    M = dict(w_in=m_w_in, gla_w_gate_up=m_gla_w_gate_up, gla_b_gate=m_gla_b_gate, gla_norm_g=m_gla_norm_g, w_gla_o=m_w_gla_o, mla_q_norm_g=m_mla_q_norm_g, mla_w_uq=m_mla_w_uq, mla_kv_norm_g=m_mla_kv_norm_g, mla_w_ukv=m_mla_w_ukv, w_mla_o=m_w_mla_o, w_out=m_w_out, ln1_g=m_ln1_g, ln1_b=m_ln1_b, w_up=m_w_up, conv_w=m_conv_w, conv_b=m_conv_b, w_down=m_w_down, ln2_g=m_ln2_g, ln2_b=m_ln2_b)
    V = dict(w_in=v_w_in, gla_w_gate_up=v_gla_w_gate_up, gla_b_gate=v_gla_b_gate, gla_norm_g=v_gla_norm_g, w_gla_o=v_w_gla_o, mla_q_norm_g=v_mla_q_norm_g, mla_w_uq=v_mla_w_uq, mla_kv_norm_g=v_mla_kv_norm_g, mla_w_ukv=v_mla_w_ukv, w_mla_o=v_w_mla_o, w_out=v_w_out, ln1_g=v_ln1_g, ln1_b=v_ln1_b, w_up=v_w_up, conv_w=v_conv_w, conv_b=v_conv_b, w_down=v_w_down, ln2_g=v_ln2_g, ln2_b=v_ln2_b)

    tshard = lambda d, n: d[n][0].T
    shard = lambda n: (W[n][0].astype(BF), False)
    (w_in_t,) = _gather_two_level([tshard(W, 'w_in').astype(BF)], name="gather_w0")
    w_gt, w_mt, w_tt = _w_in_to_groups(w_in_t.reshape(NDEV * W_IN_BLOCK, D))
    kw = dict(
        w_gt=w_gt, w_mt=w_mt, w_tt=w_tt, bg=W['gla_b_gate'],
        gn=W['gla_norm_g'], gq=W['mla_q_norm_g'], gkv=W['mla_kv_norm_g'],
        g1=W['ln1_g'], b1=W['ln1_b'], g2=W['ln2_g'], b2=W['ln2_b'], cb=W['conv_b'],
    )
    received = {}

    def got_mixers(ex, w, g):
        w.update(wuq=_uq_to_kernel(_cols_gathered(ex[0])), wukv=_ukv_to_kernel(_cols_gathered(ex[1])),
                 wg=jnp.pad(_cols_gathered(ex[2]), ((0, 128 - GR), (0, 0))))

    def got_out_proj(ex, w, g):
        w.update(wgo=ex[0].reshape(D, D), wmo=ex[1].reshape(D, D), wout=ex[2].reshape(D, D))

    def got_up(ex, w, g):
        w_upt = ex[0].reshape(2 * DFF, D)
        w.update(wugt=w_upt[:DFF], wuvt=w_upt[DFF:], wug=w_upt[:DFF].T, wuv=w_upt[DFF:].T)

    def got_down(ex, w, g):
        w.update(wd=ex[0].reshape(DFF, D), cw=_cols_gathered(ex[1]))

    slab = lambda a, lo=0: ([(a.astype(BF), lo)], True)
    rows = lambda a, n=NDEV: a.reshape(n, a.shape[0] // n, a.shape[1])

    def keep(names):
        return lambda ex, w, g: received.update(zip(names, ex))

    def small_grads(g):
        return _pack_small(dict(gla_b_gate=g['bg'], gla_norm_g=g['gn'], mla_q_norm_g=g['gq'], mla_kv_norm_g=g['gkv'],
                                ln1_g=g['g1'], ln1_b=g['b1'], conv_b=g['cb'], ln2_g=g['g2'], ln2_b=g['b2'],
                                loss=g['loss']))

    hooks = {
        "proj_g": (lambda w, g: [shard('mla_w_uq'), shard('mla_w_ukv'), shard('gla_w_gate_up')], got_mixers),
        "gla_fwd": (lambda w, g: [shard('w_gla_o'), shard('w_mla_o'), shard('w_out')], got_out_proj),
        "flash_fwd": (lambda w, g: [(tshard(W, 'w_up').astype(BF), False)], got_up),
        "post_attn_fwd": (lambda w, g: [shard('w_down'), (W['conv_w'][0], False)], got_down),
        "flash_bwd": (lambda w, g: [slab(rows(g['wd'])),
                                    ([(rows(g['wugt'], 4), 0), (rows(g['wuvt'], 4), 4)], True),
                                    slab(rows(g['wout'])), slab(rows(g['wgo'])), slab(rows(g['wmo']))],
                      keep(['w_down', 'w_up', 'w_out', 'w_gla_o', 'w_mla_o'])),
        "gla_bwd": (lambda w, g: [slab(_uq_from_kernel(g['wuq']).transpose(1, 0, 2)),
                                  slab(_ukv_from_kernel(g['wukv']).transpose(1, 0, 2)),
                                  slab(rows(_w_in_rows_hi(g['w_mt'], g['w_tt']), NDEV - _W_IN_LO), _W_IN_LO)],
                    keep(['mla_w_uq', 'mla_w_ukv', 'w_in_hi'])),
        "dx": (lambda w, g: [slab(rows(_w_in_rows_lo(g['w_gt'], g['w_mt']), _W_IN_LO)),
                             ([(_cols_scattered(g['wg'][:GR]), 0)], True), ([(_cols_scattered(g['cw']), 0)], True),
                             (small_grads(g), False)],
               keep(['w_in_lo', 'gla_w_gate_up', 'conv_w', 'small'])),
    }

    _, grad_x, _ = _local_step(x, positions, loss_target, kw, hooks)

    grads, deltas, new_m, new_v = {}, {}, {}, {}
    small_parts = received['small']
    loss = jnp.sum(small_parts.reshape(NDEV, -1)[:, _SMALL_USED])
    me = 4 * lax.axis_index("x") + 2 * lax.axis_index("y") + lax.axis_index("c")
    received['w_in'] = jnp.where(me >= _W_IN_LO, received['w_in_hi'], received['w_in_lo'])
    for n in _SHARDED:
        shp = W[n].shape
        if n in ('w_in', 'w_up'):
            out = _adamw(received[n], tshard(W, n), tshard(M, n), tshard(V, n), name="adamw_" + n)
            grads[n], deltas[n], new_m[n], new_v[n] = [t.T.reshape(shp) for t in out]
            continue
        out = _adamw(received[n], W[n][0], M[n][0], V[n][0], name="adamw_" + n)
        grads[n], deltas[n], new_m[n], new_v[n] = [t.reshape(shp) for t in out]
    out = _adamw(small_parts, _pack_small(W), _pack_small(M), _pack_small(V), name="adamw_small")
    for dst, packed in zip((grads, deltas, new_m, new_v), out):
        dst.update(_unpack_small(packed))

    return (loss, grad_x, *[grads[n] for n in _NAMES], *[deltas[n] for n in _NAMES],
            *[new_m[n] for n in _NAMES], *[new_v[n] for n in _NAMES])
```

```python
import functools

import jax
import jax.numpy as jnp
from jax import lax
from jax.experimental import pallas as pl
from jax.experimental.pallas import tpu as pltpu

F32 = jnp.float32
BF = jnp.bfloat16

D = 1024
GH, GDK, GDV, GR, GTAU, GC = 4, 128, 256, 16, 16.0, 64
MH, MQR, MKR, NOPE, ROPE, MV = 8, 384, 256, 128, 64, 128
THETA = 10000.0
DFF = 2816
ALPHA = 2.0 ** 0.25
LN_EPS = 1e-5
RMS_EPS = 1e-6
NDEV = 8
ADAM_LR, ADAM_B1, ADAM_B2, ADAM_EPS, ADAM_WD, ADAM_STEP = 0.001, 0.9, 0.999, 1e-08, 0.01, 10

PG_W = 3200
PM_W = 768
PT_W = 2048
NEG = -1e30
MESH_ID = pl.DeviceIdType.MESH
VMEM_MB = 1024 * 1024


V7X_VMEM_LIMIT_MB = 32
FLASH_BWD_VMEM_LIMIT_MB = 44
TOKEN_TM = 256
FLASH_TQ = 512
FFN_TN = 1408


def _params(sem, vmem_mb=V7X_VMEM_LIMIT_MB):
    return pltpu.CompilerParams(dimension_semantics=sem, vmem_limit_bytes=vmem_mb * VMEM_MB)


def _dot(a, b):
    return lax.dot_general(a, b, (((1,), (0,)), ((), ())), preferred_element_type=F32)


def _dot_nt(a, b):
    return lax.dot_general(a, b, (((1,), (1,)), ((), ())), preferred_element_type=F32)


def _dot_tn(a, b):
    return lax.dot_general(a, b, (((0,), (0,)), ((), ())), preferred_element_type=F32)


def _iota(shape, dim):
    return lax.broadcasted_iota(jnp.int32, shape, dim)


FLASH_HP = 2
FLASH_HP_FWD = 4
QK_SCALE = (NOPE + ROPE) ** -0.5
LOG2E = 1.4426950408889634
QK_SCALE_LOG2 = QK_SCALE * LOG2E


def _sigmoid(x):
    return 0.5 * jnp.tanh(0.5 * x) + 0.5


def _tri_mm(tri_bf, x):
    hi = x.astype(BF)
    r1 = x - hi.astype(F32)
    mid = r1.astype(BF)
    lo = (r1 - mid.astype(F32)).astype(BF)
    return _dot(tri_bf, hi) + _dot(tri_bf, mid) + _dot(tri_bf, lo)


def _matmul(a, b, mode, *, name, c_in=None, out_dtype=F32, tm=512, tn=512, tk=512, ride=None, emit_a=False):
    if mode == "nn":
        (M, K), (_, N) = a.shape, b.shape
    elif mode == "nt":
        (M, K), (N, _) = a.shape, b.shape
    else:
        (K, M), (_, N) = a.shape, b.shape
    tm, tn, tk = min(tm, M), min(tn, N), min(tk, K)
    assert M % tm == 0 and N % tn == 0 and K % tk == 0, (name, M, N, K, tm, tn, tk)
    nk = K // tk
    assert not emit_a or (nk == 1 and mode != "tn" and c_in is None and ride is None)
    dot = {"nn": _dot, "nt": _dot_nt, "tn": _dot_tn}[mode]

    def body(*refs):
        if emit_a:
            a_ref, b_ref, o_ref, xa_ref, acc_ref = refs
        elif c_in is None:
            a_ref, b_ref, o_ref, acc_ref = refs
        else:
            a_ref, b_ref, c_ref, o_ref, acc_ref = refs
        k = pl.program_id(2)

        @pl.when(k == 0)
        def _():
            if c_in is None:
                acc_ref[...] = jnp.zeros_like(acc_ref)
            else:
                acc_ref[...] = c_ref[...].astype(F32)

        if emit_a:
            @pl.when(pl.program_id(1) == 0)
            def _():
                xa_ref[...] = a_ref[...].astype(BF)

        acc_ref[...] += dot(a_ref[...].astype(BF), b_ref[...].astype(BF))

        @pl.when(k == nk - 1)
        def _():
            o_ref[...] = acc_ref[...].astype(out_dtype)

    if mode == "tn":
        a_spec = pl.BlockSpec((tk, tm), lambda i, j, k: (k, i))
    else:
        a_spec = pl.BlockSpec((tm, tk), lambda i, j, k: (i, k))
    if mode == "nt":
        b_spec = pl.BlockSpec((tn, tk), lambda i, j, k: (j, k))
    else:
        b_spec = pl.BlockSpec((tk, tn), lambda i, j, k: (k, j))
    in_specs = [a_spec, b_spec]
    args = [a, b]
    if c_in is not None:
        in_specs.append(pl.BlockSpec((tm, tn), lambda i, j, k: (i, j)))
        args.append(c_in)
    out_shape = (jax.ShapeDtypeStruct((M, N), out_dtype),)
    out_specs = (pl.BlockSpec((tm, tn), lambda i, j, k: (i, j)),)
    if emit_a:
        out_shape += (jax.ShapeDtypeStruct((M, K), BF),)
        out_specs += (pl.BlockSpec((tm, tk), lambda i, j, k: (i, k)),)
    res = _call(
        body, name=name, out_shape=out_shape, grid=(M // tm, N // tn, nk), in_specs=in_specs, out_specs=out_specs,
        scratch_shapes=[pltpu.VMEM((tm, tn), F32)],
        sem=("parallel", "arbitrary", "arbitrary"), args=args, ride=ride)
    if emit_a:
        return res[0], res[1]
    return res[0] if ride is None else (res[0][0], res[1])


def _matmul_sum(c_in, parts, *, name, tm=1024, ride=None):
    M, N = c_in.shape
    tm = min(tm, M)
    n_p = len(parts)
    counts = [a.shape[1] // tk for a, _, tk in parts]
    starts = [sum(counts[:p]) for p in range(n_p)]
    nk = sum(counts)

    def body(*refs):
        a_refs, w_refs = refs[:n_p], refs[n_p:2 * n_p]
        c_ref, o_ref, acc_ref = refs[2 * n_p:]
        k = pl.program_id(1)

        @pl.when(k == 0)
        def _():
            acc_ref[...] = c_ref[...]

        for p in range(n_p):
            @pl.when(jnp.logical_and(k >= starts[p], k < starts[p] + counts[p]))
            def _(p=p):
                acc_ref[...] += _dot(a_refs[p][...].astype(BF), w_refs[p][...].astype(BF))

        @pl.when(k == nk - 1)
        def _():
            o_ref[...] = acc_ref[...]

    def kidx(p):
        return lambda k: jnp.clip(k - starts[p], 0, counts[p] - 1)

    in_specs = [pl.BlockSpec((tm, tk), lambda i, k, f=kidx(p): (i, f(k))) for p, (_, _, tk) in enumerate(parts)]
    in_specs += [pl.BlockSpec((tk, N), lambda i, k, f=kidx(p): (f(k), 0)) for p, (_, _, tk) in enumerate(parts)]
    in_specs.append(pl.BlockSpec((tm, N), lambda i, k: (i, 0)))
    res = _call(
        body, name=name, out_shape=(jax.ShapeDtypeStruct((M, N), F32),), grid=(M // tm, nk),
        in_specs=in_specs, out_specs=(pl.BlockSpec((tm, N), lambda i, k: (i, 0)),),
        scratch_shapes=[pltpu.VMEM((tm, N), F32)], sem=("parallel", "arbitrary"),
        args=[a for a, _, _ in parts] + [w for _, w, _ in parts] + [c_in], ride=ride)
    return res[0] if ride is None else (res[0][0], res[1])


def _gla_gate(pg_ref, rows, wg_ref, bg_ref):
    r = pg_ref[rows, 3072:3200].astype(BF)
    logit = _dot(r, wg_ref[...]) + bg_ref[...]
    la = (jnp.minimum(logit, 0.0) - jnp.log(1.0 + jnp.exp(-jnp.abs(logit)))) * (1.0 / GTAU)
    return r, logit, la


def _gla_fwd(pg, wg, bg, gn, ltri, *, nseq, S, tm, ride=None):
    T = pg.shape[0]
    nb, nc = S // tm, tm // GC
    qscale = GDK ** -0.5

    def body(pg_ref, wg_ref, bg_ref, gn_ref, l_ref, o_ref, zg_ref, st_ref, st_scr):
        @pl.when(pl.program_id(1) == 0)
        def _():
            st_scr[...] = jnp.zeros_like(st_scr)

        ltri_v = l_ref[...]
        causal = _iota((GC, GC), 0) >= _iota((GC, GC), 1)
        last_row = _iota((GC, GDK), 0) == GC - 1
        g = gn_ref[...]

        def chunk(c, carry):
            rows = pl.ds(pl.multiple_of(c * GC, GC), GC)
            _, _, la = _gla_gate(pg_ref, rows, wg_ref, bg_ref)
            b = _tri_mm(ltri_v, la)
            hs = range(GH)
            v, q_in, k_st, dec, st, a_raw, o_st, kv = [], [], [], [], [], [], [], []
            for h in hs:
                q = pg_ref[rows, h * GDK:(h + 1) * GDK]
                k = pg_ref[rows, 512 + h * GDK:512 + (h + 1) * GDK]
                v.append(pg_ref[rows, 1024 + h * GDV:1024 + (h + 1) * GDV].astype(BF))
                bh = b[:, h * GDK:(h + 1) * GDK]
                bl = jnp.sum(jnp.where(last_row, bh, 0.0), axis=0, keepdims=True)
                q_in.append((q * (qscale * jnp.exp(bh))).astype(BF))
                k_in = (k * jnp.exp(-bh)).astype(BF)
                k_st.append((k * jnp.exp(bl - bh)).astype(BF))
                dec.append(jnp.exp(bl))
                st.append(st_scr[h])
                st_ref[c, h] = st[h]
                a_raw.append(_dot_nt(q_in[h], k_in))
            for h in hs:
                o_st.append(_dot_nt(q_in[h], st[h].astype(BF)))
                kv.append(_dot_tn(v[h], k_st[h]))
            att = [jnp.where(causal, a_raw[h], 0.0).astype(BF) for h in hs]
            o = [_dot(att[h], v[h]) + o_st[h] for h in hs]
            for h in hs:
                st_scr[h] = st[h] * dec[h] + kv[h]
                og = pg_ref[rows, 2048 + h * GDV:2048 + (h + 1) * GDV]
                rstd = lax.rsqrt(jnp.mean(o[h] * o[h], axis=-1, keepdims=True) + RMS_EPS)
                o_ref[rows, h * GDV:(h + 1) * GDV] = o[h]
                zg_ref[rows, h * GDV:(h + 1) * GDV] = (o[h] * rstd * g * (og * _sigmoid(og))).astype(BF)
            return carry

        lax.fori_loop(0, nc, chunk, 0, unroll=True)

    full = lambda shp: pl.BlockSpec(shp, lambda b_, i: (0,) * len(shp))
    return _call(
        body, name="gla_fwd", ride=ride, sem=("parallel", "arbitrary"), args=(pg, wg, bg, gn, ltri),
        out_shape=(jax.ShapeDtypeStruct((T, GH * GDV), F32),
                   jax.ShapeDtypeStruct((T, GH * GDV), BF),
                   jax.ShapeDtypeStruct((T // GC, GH, GDV, GDK), F32)),
        grid=(nseq, nb),
        in_specs=[pl.BlockSpec((tm, PG_W), lambda b_, i: (b_ * nb + i, 0)),
                  full((128, 512)), full((1, 512)), full((1, GDV)), full((GC, GC))],
        out_specs=(pl.BlockSpec((tm, GH * GDV), lambda b_, i: (b_ * nb + i, 0)),
                   pl.BlockSpec((tm, GH * GDV), lambda b_, i: (b_ * nb + i, 0)),
                   pl.BlockSpec((nc, GH, GDV, GDK), lambda b_, i: (b_ * nb + i, 0, 0, 0))),
        scratch_shapes=[pltpu.VMEM((GH, GDV, GDK), F32)])


def _gla_bwd(pg, wg, bg, gn, ltri, utri, o, states, dzg, *, nseq, S, tm, ride=None):
    T = pg.shape[0]
    nb, nc = S // tm, tm // GC
    qscale = GDK ** -0.5

    def body(pg_ref, wg_ref, bg_ref, gn_ref, l_ref, u_ref, o_ref, st_ref, dzg_ref,
             dpg_ref, dwg_ref, dbg_ref, dgn_ref, dst_scr):
        first = jnp.logical_and(pl.program_id(0) == 0, pl.program_id(1) == 0)

        @pl.when(first)
        def _():
            dwg_ref[...] = jnp.zeros_like(dwg_ref)
            dbg_ref[...] = jnp.zeros_like(dbg_ref)
            dgn_ref[...] = jnp.zeros_like(dgn_ref)

        @pl.when(pl.program_id(1) == 0)
        def _():
            dst_scr[...] = jnp.zeros_like(dst_scr)

        ltri_v = l_ref[...]
        utri_v = u_ref[...]
        causal = _iota((GC, GC), 0) >= _iota((GC, GC), 1)
        last_row = _iota((GC, GDK), 0) == GC - 1
        g = gn_ref[...]

        def chunk(cc, carry):
            c = nc - 1 - cc
            rows = pl.ds(pl.multiple_of(c * GC, GC), GC)
            r, logit, la = _gla_gate(pg_ref, rows, wg_ref, bg_ref)
            b = _tri_mm(ltri_v, la)
            hs = range(GH)
            L = lambda: [None] * GH
            vb, eb, enb, ek, dec, q_in, k_in, k_st, q_inb, k_inb, st, dst, dob = (L() for _ in range(13))
            a_raw, da_raw, dq_st, dks, dv_st, dst_new, dbs, dgn = (L() for _ in range(8))
            for h in hs:
                q = pg_ref[rows, h * GDK:(h + 1) * GDK]
                k = pg_ref[rows, 512 + h * GDK:512 + (h + 1) * GDK]
                vb[h] = pg_ref[rows, 1024 + h * GDV:1024 + (h + 1) * GDV].astype(BF)
                og = pg_ref[rows, 2048 + h * GDV:2048 + (h + 1) * GDV]
                oh = o_ref[rows, h * GDV:(h + 1) * GDV]
                dz = dzg_ref[rows, h * GDV:(h + 1) * GDV].astype(F32)
                bh = b[:, h * GDK:(h + 1) * GDK]
                bl = jnp.sum(jnp.where(last_row, bh, 0.0), axis=0, keepdims=True)
                eb[h] = qscale * jnp.exp(bh)
                enb[h] = jnp.exp(-bh)
                ek[h] = jnp.exp(bl - bh)
                dec[h] = jnp.exp(bl)
                q_in[h], k_in[h], k_st[h] = q * eb[h], k * enb[h], k * ek[h]
                q_inb[h], k_inb[h] = q_in[h].astype(BF), k_in[h].astype(BF)
                st[h] = st_ref[c, h]
                dst[h] = dst_scr[h]
                rstd = lax.rsqrt(jnp.mean(oh * oh, axis=-1, keepdims=True) + RMS_EPS)
                ohat = oh * rstd
                sg = _sigmoid(og)
                don = dz * (og * sg)
                dpg_ref[rows, 2048 + h * GDV:2048 + (h + 1) * GDV] = (
                    dz * (ohat * g) * (sg * (1.0 + og * (1.0 - sg)))).astype(BF)
                dgn[h] = jnp.sum(don * ohat, axis=0, keepdims=True)
                gd = don * g
                dob[h] = (rstd * (gd - ohat * jnp.mean(gd * ohat, axis=-1, keepdims=True))).astype(BF)
                a_raw[h] = _dot_nt(q_inb[h], k_inb[h])
                da_raw[h] = _dot_nt(dob[h], vb[h])
            dgn_ref[...] += dgn[0] + dgn[1] + dgn[2] + dgn[3]
            for h in hs:
                dstb = dst[h].astype(BF)
                dq_st[h] = _dot(dob[h], st[h].astype(BF))
                dks[h] = _dot(vb[h], dstb)
                dv_st[h] = _dot_nt(k_st[h].astype(BF), dstb)
                dst_new[h] = _dot_tn(dob[h], q_inb[h])
            att = [jnp.where(causal, a_raw[h], 0.0).astype(BF) for h in hs]
            da = [jnp.where(causal, da_raw[h], 0.0).astype(BF) for h in hs]
            dqi = [_dot(da[h], k_inb[h]) + dq_st[h] for h in hs]
            dki = [_dot_tn(da[h], q_inb[h]) for h in hs]
            dv = [_dot_tn(att[h], dob[h]) + dv_st[h] for h in hs]
            for h in hs:
                dd = jnp.sum(dst[h] * st[h], axis=0, keepdims=True)
                dst_scr[h] = dst[h] * dec[h] + dst_new[h]
                kk = dks[h] * k_st[h]
                dbl = jnp.sum(kk, axis=0, keepdims=True) + dd * dec[h]
                db = dqi[h] * q_in[h] - dki[h] * k_in[h] - kk
                dbs[h] = db + jnp.where(last_row, dbl, 0.0)
                dpg_ref[rows, h * GDK:(h + 1) * GDK] = (dqi[h] * eb[h]).astype(BF)
                dpg_ref[rows, 512 + h * GDK:512 + (h + 1) * GDK] = (dki[h] * enb[h] + dks[h] * ek[h]).astype(BF)
                dpg_ref[rows, 1024 + h * GDV:1024 + (h + 1) * GDV] = dv[h].astype(BF)
            dla = _tri_mm(utri_v, jnp.concatenate(dbs, axis=1))
            dlogit = dla * (1.0 / GTAU) * _sigmoid(-logit)
            dlb = dlogit.astype(BF)
            dpg_ref[rows, 3072:3200] = _dot_nt(dlb, wg_ref[...]).astype(BF)
            dwg_ref[...] += _dot_tn(r, dlb)
            dbg_ref[...] += jnp.sum(dlogit, axis=0, keepdims=True)
            return carry

        lax.fori_loop(0, nc, chunk, 0, unroll=True)

    full = lambda shp: pl.BlockSpec(shp, lambda b_, i: (0,) * len(shp))
    rev = lambda b_, i: (b_ * nb + nb - 1 - i, 0)
    return _call(
        body, name="gla_bwd", ride=ride, sem=("arbitrary", "arbitrary"),
        args=(pg, wg, bg, gn, ltri, utri, o, states, dzg),
        out_shape=(jax.ShapeDtypeStruct((T, PG_W), BF),
                   jax.ShapeDtypeStruct((128, 512), F32),
                   jax.ShapeDtypeStruct((1, 512), F32),
                   jax.ShapeDtypeStruct((1, GDV), F32)),
        grid=(nseq, nb),
        in_specs=[pl.BlockSpec((tm, PG_W), rev),
                  full((128, 512)), full((1, 512)), full((1, GDV)), full((GC, GC)), full((GC, GC)),
                  pl.BlockSpec((tm, GH * GDV), rev),
                  pl.BlockSpec((nc, GH, GDV, GDK), lambda b_, i: (b_ * nb + nb - 1 - i, 0, 0, 0)),
                  pl.BlockSpec((tm, GH * GDV), rev)],
        out_specs=(pl.BlockSpec((tm, PG_W), rev), full((128, 512)), full((1, 512)), full((1, GDV))),
        scratch_shapes=[pltpu.VMEM((GH, GDV, GDK), F32)])


def _rope_tables(pos, invf):
    ang = pos.astype(F32) * invf
    lane = _iota(ang.shape, 1)
    sin = jnp.sin(ang)
    ssin = jnp.where(lane < 32, -sin, jnp.where(lane < 64, sin, 0.0))
    return jnp.cos(ang), ssin, lane


def _rope(x, cos, ssin, lane, sign):
    rot = jnp.where(lane < 32, pltpu.roll(x, 96, 1), pltpu.roll(x, 32, 1))
    return x * cos + sign * (rot * ssin)


def _rms_fwd(x, g):
    rstd = lax.rsqrt(jnp.mean(x * x, axis=-1, keepdims=True) + RMS_EPS)
    return x * rstd * g, x * rstd, rstd


def _rms_bwd(dy, xhat, rstd, g):
    gd = dy * g
    return rstd * (gd - xhat * jnp.mean(gd * xhat, axis=-1, keepdims=True)), jnp.sum(dy * xhat, axis=0, keepdims=True)


def _mla_prep_fwd(pm, pos, invf, gq, gkv, wuq, wukv, *, tm):
    T = pm.shape[0]

    def body(pm_ref, pos_ref, invf_ref, gq_ref, gkv_ref, wuq_ref, wukv_ref, qc_ref, kc_ref, v_ref):
        cos, ssin, lane = _rope_tables(pos_ref[...], invf_ref[...])
        cq, _, _ = _rms_fwd(pm_ref[:, 0:MQR], gq_ref[...])
        ckv, _, _ = _rms_fwd(pm_ref[:, 512:768], gkv_ref[...])
        qf = _dot(cq.astype(BF), wuq_ref[...])
        kvf = _dot(ckv.astype(BF), wukv_ref[...])
        kr = _rope(pm_ref[:, 384:512], cos, ssin, lane, 1.0).astype(BF)
        for h in range(MH):
            qc_ref[:, 256 * h:256 * h + 128] = (QK_SCALE_LOG2 * qf[:, 128 * h:128 * h + 128]).astype(BF)
            qr = qf[:, 1024 + 128 * h:1024 + 128 * h + 128]
            qc_ref[:, 256 * h + 128:256 * h + 256] = (QK_SCALE_LOG2 * _rope(qr, cos, ssin, lane, 1.0)).astype(BF)
            kc_ref[:, 256 * h:256 * h + 128] = kvf[:, 128 * h:128 * h + 128].astype(BF)
            kc_ref[:, 256 * h + 128:256 * h + 256] = kr
        v_ref[...] = kvf[:, 1024:2048].astype(BF)

    full = lambda shp: pl.BlockSpec(shp, lambda i: (0,) * len(shp))
    row = lambda w: pl.BlockSpec((tm, w), lambda i: (i, 0))
    return pl.pallas_call(
        body, name="mla_prep_fwd",
        out_shape=(jax.ShapeDtypeStruct((T, MH * 256), BF), jax.ShapeDtypeStruct((T, MH * 256), BF),
                   jax.ShapeDtypeStruct((T, MH * MV), BF)),
        grid=(T // tm,),
        in_specs=[row(PM_W), row(1), full((1, 128)), full((1, MQR)), full((1, MKR)),
                  full((MQR, 2048)), full((MKR, 2048))],
        out_specs=(row(MH * 256), row(MH * 256), row(MH * MV)),
        compiler_params=_params(("parallel",)),
    )(pm, pos, invf, gq, gkv, wuq, wukv)


def _mla_prep_bwd(pm, pos, invf, gq, gkv, wuq, wukv, dqc, dkc, dv, *, tm):
    T = pm.shape[0]

    def body(pm_ref, pos_ref, invf_ref, gq_ref, gkv_ref, wuq_ref, wukv_ref, dqc_ref, dkc_ref, dv_ref,
             dpm_ref, dwuq_ref, dwukv_ref, dgq_ref, dgkv_ref):
        @pl.when(pl.program_id(0) == 0)
        def _():
            dwuq_ref[...] = jnp.zeros_like(dwuq_ref)
            dwukv_ref[...] = jnp.zeros_like(dwukv_ref)
            dgq_ref[...] = jnp.zeros_like(dgq_ref)
            dgkv_ref[...] = jnp.zeros_like(dgkv_ref)

        cos, ssin, lane = _rope_tables(pos_ref[...], invf_ref[...])
        cq, cqh, cq_rstd = _rms_fwd(pm_ref[:, 0:MQR], gq_ref[...])
        ckv, ckvh, ckv_rstd = _rms_fwd(pm_ref[:, 512:768], gkv_ref[...])
        dqn, dqr, dkn = [], [], []
        dkr = jnp.zeros((tm, 128), F32)
        for h in range(MH):
            dqn.append(dqc_ref[:, 256 * h:256 * h + 128].astype(BF))
            dqr.append(_rope(dqc_ref[:, 256 * h + 128:256 * h + 256], cos, ssin, lane, -1.0).astype(BF))
            dkn.append(dkc_ref[:, 256 * h:256 * h + 128].astype(BF))
            dkr = dkr + dkc_ref[:, 256 * h + 128:256 * h + 256]
        dqf = jnp.concatenate(dqn + dqr, axis=1)
        dkvf = jnp.concatenate(dkn + [dv_ref[...].astype(BF)], axis=1)
        dwuq_ref[...] += _dot_tn(cq.astype(BF), dqf)
        dwukv_ref[...] += _dot_tn(ckv.astype(BF), dkvf)
        dcq, dgq = _rms_bwd(_dot_nt(dqf, wuq_ref[...]), cqh, cq_rstd, gq_ref[...])
        dckv, dgkv = _rms_bwd(_dot_nt(dkvf, wukv_ref[...]), ckvh, ckv_rstd, gkv_ref[...])
        dgq_ref[...] += dgq
        dgkv_ref[...] += dgkv
        dpm_ref[:, 0:MQR] = dcq.astype(BF)
        dpm_ref[:, 384:512] = _rope(dkr, cos, ssin, lane, -1.0).astype(BF)
        dpm_ref[:, 512:768] = dckv.astype(BF)

    full = lambda shp: pl.BlockSpec(shp, lambda i: (0,) * len(shp))
    row = lambda w: pl.BlockSpec((tm, w), lambda i: (i, 0))
    return pl.pallas_call(
        body, name="mla_prep_bwd",
        out_shape=(jax.ShapeDtypeStruct((T, PM_W), BF), jax.ShapeDtypeStruct((MQR, 2048), F32),
                   jax.ShapeDtypeStruct((MKR, 2048), F32), jax.ShapeDtypeStruct((1, MQR), F32),
                   jax.ShapeDtypeStruct((1, MKR), F32)),
        grid=(T // tm,),
        in_specs=[row(PM_W), row(1), full((1, 128)), full((1, MQR)), full((1, MKR)),
                  full((MQR, 2048)), full((MKR, 2048)), row(MH * 256), row(MH * 256), row(MH * MV)],
        out_specs=(row(PM_W), full((MQR, 2048)), full((MKR, 2048)), full((1, MQR)), full((1, MKR))),
        compiler_params=_params(("arbitrary",)),
    )(pm, pos, invf, gq, gkv, wuq, wukv, dqc, dkc, dv)


def _flash_fwd(qc, kc, v, *, nseq, S, tq, ride=None):
    T = qc.shape[0]
    nq = S // tq
    hp = FLASH_HP_FWD

    def body(q_ref, k_ref, v_ref, o_ref, lse_ref):
        i = pl.program_id(2)
        causal = _iota((tq, tq), 0) >= _iota((tq, tq), 1)

        def step(j, carry, masked):
            rows = pl.ds(pl.multiple_of(j * tq, tq), tq)
            hs = range(hp)
            s = [_dot_nt(q_ref[:, 256 * hh:256 * hh + 256], k_ref[rows, 256 * hh:256 * hh + 256]) for hh in hs]
            p, stats = [], []
            for hh in hs:
                m, l, _ = carry[hh]
                sh = jnp.where(causal, s[hh], NEG) if masked else s[hh]
                m_new = jnp.maximum(m, jnp.max(sh, axis=-1, keepdims=True))
                ph = jnp.exp2(sh - m_new)
                a = jnp.exp2(m - m_new)
                stats.append((m_new, a * l + jnp.sum(ph, axis=-1, keepdims=True), a))
                p.append(ph.astype(BF))
            pv = [_dot(p[hh], v_ref[rows, MV * hh:MV * hh + MV]) for hh in hs]
            return tuple((stats[hh][0], stats[hh][1], stats[hh][2] * carry[hh][2] + pv[hh]) for hh in hs)

        init = ((jnp.full((tq, 1), NEG, F32), jnp.zeros((tq, 1), F32), jnp.zeros((tq, MV), F32)),) * hp
        carry = lax.fori_loop(0, i, lambda j, c: step(j, c, False), init)
        for hh, (m, l, acc) in enumerate(step(i, carry, True)):
            o_ref[:, MV * hh:MV * hh + MV] = (acc / l).astype(BF)
            lse_ref[:, 128 * hh:128 * hh + 128] = jnp.broadcast_to(m + jnp.log2(l), (tq, 128))

    return _call(
        body, name="flash_fwd", ride=ride, sem=("parallel", "parallel", "arbitrary"), args=(qc, kc, v),
        out_shape=(jax.ShapeDtypeStruct((T, MH * MV), BF), jax.ShapeDtypeStruct((T, MH * 128), F32)),
        grid=(nseq, MH // hp, nq),
        in_specs=[pl.BlockSpec((tq, 256 * hp), lambda b_, h, i: (b_ * nq + i, h)),
                  pl.BlockSpec((S, 256 * hp), lambda b_, h, i: (b_, h)),
                  pl.BlockSpec((S, MV * hp), lambda b_, h, i: (b_, h))],
        out_specs=(pl.BlockSpec((tq, MV * hp), lambda b_, h, i: (b_ * nq + i, h)),
                   pl.BlockSpec((tq, 128 * hp), lambda b_, h, i: (b_ * nq + i, h))))


def _flash_bwd(qc, kc, v, o, do, lse, *, nseq, S, tq, ride=None):
    T = qc.shape[0]
    nq = S // tq

    def body(q_ref, k_ref, v_ref, o_ref, do_ref, lse_ref, dq_ref, dk_ref, dv_ref, dq_scr, delta_scr):
        j = pl.program_id(2)

        @pl.when(j == 0)
        def _():
            dq_scr[...] = jnp.zeros_like(dq_scr)
            for hh in range(FLASH_HP):
                od = o_ref[:, MV * hh:MV * hh + MV].astype(F32) * do_ref[:, MV * hh:MV * hh + MV].astype(F32)
                delta_scr[:, 128 * hh:128 * hh + 128] = jnp.broadcast_to(jnp.sum(od, axis=-1, keepdims=True), (S, 128))

        causal = _iota((tq, tq), 0) >= _iota((tq, tq), 1)

        def step(i, carry, masked):
            rows = pl.ds(pl.multiple_of(i * tq, tq), tq)
            hs = range(FLASH_HP)
            qs = [slice(256 * hh, 256 * hh + 256) for hh in hs]
            vs = [slice(MV * hh, MV * hh + MV) for hh in hs]
            ls = [slice(128 * hh, 128 * hh + 1) for hh in hs]
            s = [_dot_nt(q_ref[rows, qs[hh]], k_ref[:, qs[hh]]) for hh in hs]
            dp = [_dot_nt(do_ref[rows, vs[hh]], v_ref[:, vs[hh]]) for hh in hs]
            pb, ds = [], []
            for hh in hs:
                p = jnp.exp2(s[hh] - lse_ref[rows, ls[hh]])
                if masked:
                    p = jnp.where(causal, p, 0.0)
                pb.append(p.astype(BF))
                ds.append((p * (dp[hh] - delta_scr[rows, ls[hh]])).astype(BF))
            dv = [carry[hh][1] + _dot_tn(pb[hh], do_ref[rows, vs[hh]]) for hh in hs]
            dk = [carry[hh][0] + _dot_tn(ds[hh], q_ref[rows, qs[hh]]) for hh in hs]
            for hh in hs:
                dq_scr[rows, qs[hh]] += _dot(ds[hh], k_ref[:, qs[hh]])
            return tuple((dk[hh], dv[hh]) for hh in hs)

        init = ((jnp.zeros((tq, 256), F32), jnp.zeros((tq, MV), F32)),) * FLASH_HP
        carry = step(j, init, True)
        carry = lax.fori_loop(j + 1, nq, lambda i, c: step(i, c, False), carry)
        for hh, (dk, dv) in enumerate(carry):
            dk_ref[:, 256 * hh:256 * hh + 256] = dk * (1.0 / LOG2E)
            dv_ref[:, MV * hh:MV * hh + MV] = dv

        @pl.when(j == nq - 1)
        def _():
            dq_ref[...] = dq_scr[...] * QK_SCALE

    hp = FLASH_HP
    seq = lambda w: pl.BlockSpec((S, w * hp), lambda b_, h, j: (b_, h))
    blk = lambda w: pl.BlockSpec((tq, w * hp), lambda b_, h, j: (b_ * nq + j, h))
    return _call(
        body, name="flash_bwd", ride=ride, sem=("parallel", "parallel", "arbitrary"), args=(qc, kc, v, o, do, lse),
        vmem_mb=FLASH_BWD_VMEM_LIMIT_MB,
        out_shape=(jax.ShapeDtypeStruct((T, MH * 256), F32), jax.ShapeDtypeStruct((T, MH * 256), F32),
                   jax.ShapeDtypeStruct((T, MH * MV), F32)),
        grid=(nseq, MH // hp, nq),
        in_specs=[seq(256), blk(256), blk(MV), seq(MV), seq(MV), seq(128)],
        out_specs=(seq(256), blk(256), blk(MV)),
        scratch_shapes=[pltpu.VMEM((S, 256 * hp), F32), pltpu.VMEM((S, 128 * hp), F32)])


def _ln_fwd(pre, g, b):
    mu = jnp.mean(pre, axis=-1, keepdims=True)
    xc = pre - mu
    rstd = lax.rsqrt(jnp.mean(xc * xc, axis=-1, keepdims=True) + LN_EPS)
    xhat = xc * rstd
    return xhat * g + b, xhat, rstd


def _ln_bwd(dy, xhat, rstd, g):
    dxh = dy * g
    dx = rstd * (dxh - jnp.mean(dxh, axis=-1, keepdims=True) - xhat * jnp.mean(dxh * xhat, axis=-1, keepdims=True))
    return dx, jnp.sum(dy * xhat, axis=0, keepdims=True), jnp.sum(dy, axis=0, keepdims=True)


def _post_attn_fwd(zg, attn, pt, x, wgo, wmo, wout, g1, b1, *, tm, ride=None):
    T = x.shape[0]

    def body(zg_ref, at_ref, pt_ref, x_ref, wgo_ref, wmo_ref, wout_ref, g_ref, b_ref,
             yg_ref, ym_ref, mix_ref, pre_ref, hb_ref):
        yg = _dot(zg_ref[...], wgo_ref[...])
        ym = _dot(at_ref[...], wmo_ref[...])
        mix = (_sigmoid(pt_ref[:, 0:D].astype(F32)) * yg + _sigmoid(pt_ref[:, D:2 * D].astype(F32)) * ym).astype(BF)
        pre = ALPHA * x_ref[...] + _dot(mix, wout_ref[...])
        h, _, _ = _ln_fwd(pre, g_ref[...], b_ref[...])
        yg_ref[...] = yg.astype(BF)
        ym_ref[...] = ym.astype(BF)
        mix_ref[...] = mix
        pre_ref[...] = pre
        hb_ref[...] = h.astype(BF)

    full = lambda shp: pl.BlockSpec(shp, lambda i: (0,) * len(shp))
    row = lambda w: pl.BlockSpec((tm, w), lambda i: (i, 0))
    sd = lambda dt: jax.ShapeDtypeStruct((T, D), dt)
    return _call(
        body, name="post_attn_fwd", ride=ride, sem=("parallel",), args=(zg, attn, pt, x, wgo, wmo, wout, g1, b1),
        out_shape=(sd(BF), sd(BF), sd(BF), sd(F32), sd(BF)),
        grid=(T // tm,),
        in_specs=[row(D), row(D), row(PT_W), row(D), full((D, D)), full((D, D)), full((D, D)),
                  full((1, D)), full((1, D))],
        out_specs=(row(D),) * 5)


def _post_attn_bwd(dh, pre, pt, yg, ym, wgo, wmo, wout, g1, *, tm):
    T = dh.shape[0]

    def body(dh_ref, pre_ref, pt_ref, yg_ref, ym_ref, wgo_ref, wmo_ref, wout_ref, g_ref,
             dx_ref, dpreb_ref, dpt_ref, dygb_ref, dymb_ref, dzg_ref, dat_ref, dg_ref, db_ref):
        @pl.when(pl.program_id(0) == 0)
        def _():
            dg_ref[...] = jnp.zeros_like(dg_ref)
            db_ref[...] = jnp.zeros_like(db_ref)

        pre = pre_ref[...]
        mu = jnp.mean(pre, axis=-1, keepdims=True)
        xc = pre - mu
        rstd = lax.rsqrt(jnp.mean(xc * xc, axis=-1, keepdims=True) + LN_EPS)
        dpre, dg, db = _ln_bwd(dh_ref[...], xc * rstd, rstd, g_ref[...])
        dg_ref[...] += dg
        db_ref[...] += db
        dx_ref[...] = ALPHA * dpre
        dpreb = dpre.astype(BF)
        dpreb_ref[...] = dpreb
        dmix = _dot_nt(dpreb, wout_ref[...])
        sa = _sigmoid(pt_ref[:, 0:D].astype(F32))
        sb = _sigmoid(pt_ref[:, D:2 * D].astype(F32))
        dpt_ref[:, 0:D] = (dmix * yg_ref[...].astype(F32) * (sa * (1.0 - sa))).astype(BF)
        dpt_ref[:, D:2 * D] = (dmix * ym_ref[...].astype(F32) * (sb * (1.0 - sb))).astype(BF)
        dyg = (dmix * sa).astype(BF)
        dym = (dmix * sb).astype(BF)
        dygb_ref[...] = dyg
        dymb_ref[...] = dym
        dzg_ref[...] = _dot_nt(dyg, wgo_ref[...]).astype(BF)
        dat_ref[...] = _dot_nt(dym, wmo_ref[...]).astype(BF)

    full = lambda shp: pl.BlockSpec(shp, lambda i: (0,) * len(shp))
    row = lambda w: pl.BlockSpec((tm, w), lambda i: (i, 0))
    sd = lambda w, dt: jax.ShapeDtypeStruct((T, w), dt)
    return pl.pallas_call(
        body, name="post_attn_bwd",
        out_shape=(sd(D, F32), sd(D, BF), sd(PT_W, BF), sd(D, BF), sd(D, BF), sd(D, BF), sd(D, BF),
                   jax.ShapeDtypeStruct((1, D), F32), jax.ShapeDtypeStruct((1, D), F32)),
        grid=(T // tm,),
        in_specs=[row(D), row(D), row(PT_W), row(D), row(D), full((D, D)), full((D, D)), full((D, D)),
                  full((1, D))],
        out_specs=(row(D), row(D), row(PT_W), row(D), row(D), row(D), row(D), full((1, D)), full((1, D))),
        compiler_params=_params(("arbitrary",)),
    )(dh, pre, pt, yg, ym, wgo, wmo, wout, g1)


def _shift_down(u, prev, k):
    r = pltpu.roll(u, k, 0)
    p = pltpu.roll(prev, k, 0)
    head = jnp.where(_iota(p.shape, 0) < k, p, r[0:8, :])
    return jnp.concatenate([head, r[8:, :]], axis=0)


def _conv3(u, prev, w_ref, b_ref):
    return (w_ref[0:1, :] * _shift_down(u, prev, 2) + w_ref[1:2, :] * _shift_down(u, prev, 1)
            + w_ref[2:3, :] * u + b_ref[...])


def _ffn_up_fwd(hb, wug, wuv, cw, cb, *, S, tm, tn):
    T = hb.shape[0]
    nj, nbs = DFF // tn, S // tm

    def body(h_ref, wg_ref, wv_ref, cwg_ref, cwv_ref, cbg_ref, cbv_ref,
             ug_ref, uv_ref, ucg_ref, ucv_ref, f_ref, pg_scr, pv_scr):
        @pl.when(pl.program_id(1) % nbs == 0)
        def _():
            pg_scr[...] = jnp.zeros_like(pg_scr)
            pv_scr[...] = jnp.zeros_like(pv_scr)

        h = h_ref[...]
        ug = _dot(h, wg_ref[...])
        uv = _dot(h, wv_ref[...])
        ucg = _conv3(ug, pg_scr[...], cwg_ref, cbg_ref)
        ucv = _conv3(uv, pv_scr[...], cwv_ref, cbv_ref)
        pg_scr[...] = ug[tm - 8:, :]
        pv_scr[...] = uv[tm - 8:, :]
        ug_ref[...] = ug.astype(BF)
        uv_ref[...] = uv.astype(BF)
        ucg_ref[...] = ucg
        ucv_ref[...] = ucv
        f_ref[...] = (ucg * _sigmoid(ucg) * ucv).astype(BF)

    tile = pl.BlockSpec((tm, tn), lambda j, i: (i, j))
    return pl.pallas_call(
        body, name="ffn_up_fwd",
        out_shape=(jax.ShapeDtypeStruct((T, DFF), BF), jax.ShapeDtypeStruct((T, DFF), BF),
                   jax.ShapeDtypeStruct((T, DFF), F32), jax.ShapeDtypeStruct((T, DFF), F32),
                   jax.ShapeDtypeStruct((T, DFF), BF)),
        grid=(nj, T // tm),
        in_specs=[pl.BlockSpec((tm, D), lambda j, i: (i, 0)),
                  pl.BlockSpec((D, tn), lambda j, i: (0, j)), pl.BlockSpec((D, tn), lambda j, i: (0, j)),
                  pl.BlockSpec((3, tn), lambda j, i: (0, j)), pl.BlockSpec((3, tn), lambda j, i: (0, j + nj)),
                  pl.BlockSpec((1, tn), lambda j, i: (0, j)), pl.BlockSpec((1, tn), lambda j, i: (0, j + nj))],
        out_specs=(tile, tile, tile, tile, tile),
        scratch_shapes=[pltpu.VMEM((8, tn), F32), pltpu.VMEM((8, tn), F32)],
        compiler_params=_params(("parallel", "arbitrary")),
    )(hb, wug, wuv, cw, cw, cb, cb)


def _ffn_bwd(dpreb, wd, ug, uv, ucg, ucv, cw, *, S, tm, tn):
    T = dpreb.shape[0]
    nj, nb, nbs = DFF // tn, T // tm, S // tm
    r_, c_ = lax.broadcasted_iota(jnp.int32, (tm, tm), 0), lax.broadcasted_iota(jnp.int32, (tm, tm), 1)
    s1, s2 = (c_ == r_ + 1).astype(BF), (c_ == r_ + 2).astype(BF)

    def body(dp_ref, wd_ref, ug_ref, uv_ref, ucg_ref, ucv_ref, cwg_ref, cwv_ref, s1_ref, s2_ref,
             dug_ref, duv_ref, dcg_ref, dcv_ref, ng_scr, nv_scr):
        ii = pl.program_id(1)
        i = nb - 1 - ii
        tail_row = _iota((8, tn), 0)

        @pl.when(ii == 0)
        def _():
            dcg_ref[...] = jnp.zeros_like(dcg_ref)
            dcv_ref[...] = jnp.zeros_like(dcv_ref)

        @pl.when(i % nbs == nbs - 1)
        def _():
            ng_scr[...] = jnp.zeros_like(ng_scr)
            nv_scr[...] = jnp.zeros_like(nv_scr)

        df = _dot_nt(dp_ref[...], wd_ref[...])
        ucg = ucg_ref[...]
        sg = _sigmoid(ucg)
        ducg = df * ucv_ref[...] * (sg * (1.0 + ucg * (1.0 - sg)))
        ducv = df * (ucg * sg)

        def finish(duc, u_ref, w, nxt_scr, du_ref, dc_ref):
            nxt = nxt_scr[...]
            db = duc.astype(BF)

            def shifted(s_ref, k):
                r = _dot(s_ref[...], db)
                tail = jnp.where(tail_row >= 8 - k, pltpu.roll(nxt, 8 - k, 0), r[tm - 8:, :])
                return jnp.concatenate([r[:tm - 8, :], tail], axis=0)

            up1 = shifted(s1_ref, 1)
            up2 = shifted(s2_ref, 2)
            du_ref[...] = (w[2:3, :] * duc + w[1:2, :] * up1 + w[0:1, :] * up2).astype(BF)
            nxt_scr[...] = duc[0:8, :]
            u = u_ref[...].astype(F32)
            for row, z in enumerate((u * up2, u * up1, u * duc, duc)):
                dc_ref[row:row + 1, :] += jnp.sum(z, axis=0, keepdims=True)

        finish(ducg, ug_ref, cwg_ref, ng_scr, dug_ref, dcg_ref)
        finish(ducv, uv_ref, cwv_ref, nv_scr, duv_ref, dcv_ref)

    tile = pl.BlockSpec((tm, tn), lambda j, ii: (nb - 1 - ii, j))
    acc = pl.BlockSpec((8, tn), lambda j, ii: (0, j))
    return pl.pallas_call(
        body, name="ffn_bwd",
        out_shape=(jax.ShapeDtypeStruct((T, DFF), BF), jax.ShapeDtypeStruct((T, DFF), BF),
                   jax.ShapeDtypeStruct((8, DFF), F32), jax.ShapeDtypeStruct((8, DFF), F32)),
        grid=(nj, nb),
        in_specs=[pl.BlockSpec((tm, D), lambda j, ii: (nb - 1 - ii, 0)),
                  pl.BlockSpec((tn, D), lambda j, ii: (j, 0)),
                  tile, tile, tile, tile,
                  pl.BlockSpec((3, tn), lambda j, ii: (0, j)), pl.BlockSpec((3, tn), lambda j, ii: (0, j + nj)),
                  pl.BlockSpec((tm, tm), lambda j, ii: (0, 0)), pl.BlockSpec((tm, tm), lambda j, ii: (0, 0))],
        out_specs=(tile, tile, acc, acc),
        scratch_shapes=[pltpu.VMEM((8, tn), F32), pltpu.VMEM((8, tn), F32)],
        compiler_params=_params(("parallel", "arbitrary")),
    )(dpreb, wd, ug, uv, ucg, ucv, cw, cw, s1, s2)


def _down_ln2_loss(f_in, wd, pre1, target, g1, b1, g2, b2, *, tm):
    T = pre1.shape[0]

    def body(f_ref, wd_ref, p1_ref, t_ref, g1_ref, b1_ref, g_ref, b_ref, dpb_ref, dh_ref, loss_ref, dg_ref, db_ref):
        @pl.when(pl.program_id(0) == 0)
        def _():
            loss_ref[...] = jnp.zeros_like(loss_ref)
            dg_ref[...] = jnp.zeros_like(dg_ref)
            db_ref[...] = jnp.zeros_like(db_ref)

        halves = [pl.ds(s * (tm // 2), tm // 2) for s in range(2)]
        f = [_dot(f_ref[hs, :], wd_ref[...]) for hs in halves]
        for hs, fh in zip(halves, f):
            h, _, _ = _ln_fwd(p1_ref[hs, :], g1_ref[...], b1_ref[...])
            pre = ALPHA * h + fh
            out, xhat, rstd = _ln_fwd(pre, g_ref[...], b_ref[...])
            diff = out - t_ref[hs, :]
            loss_ref[...] += 0.5 * jnp.sum(jnp.mean(diff * diff, axis=-1, keepdims=True))
            dpre, dg, db = _ln_bwd(diff * (1.0 / D), xhat, rstd, g_ref[...])
            dg_ref[...] += dg
            db_ref[...] += db
            dpb_ref[hs, :] = dpre.astype(BF)
            dh_ref[hs, :] = ALPHA * dpre

    full = lambda shp: pl.BlockSpec(shp, lambda i: (0,) * len(shp))
    row = lambda w: pl.BlockSpec((tm, w), lambda i: (i, 0))
    return pl.pallas_call(
        body, name="down_ln2_loss",
        out_shape=(jax.ShapeDtypeStruct((T, D), BF), jax.ShapeDtypeStruct((T, D), F32),
                   jax.ShapeDtypeStruct((8, 128), F32), jax.ShapeDtypeStruct((1, D), F32),
                   jax.ShapeDtypeStruct((1, D), F32)),
        grid=(T // tm,),
        in_specs=[row(DFF), full((DFF, D)), row(D), row(D), full((1, D)), full((1, D)), full((1, D)), full((1, D))],
        out_specs=(row(D), row(D), full((8, 128)), full((1, D)), full((1, D))),
        compiler_params=_params(("arbitrary",)),
    )(f_in, wd, pre1, target, g1, b1, g2, b2)


def _adamw(parts, w, m, v, *, name):
    n, R, C = parts.shape
    tr, tc = R, C
    for cand in range(min(R, 256), 15, -1):
        if R % cand == 0 and cand % 16 == 0:
            tr = cand
            break
    if tr == R and R * C > 65536 and C % 256 == 0:
        tc = 256
    c1 = 1.0 - ADAM_B1 ** ADAM_STEP
    c2 = 1.0 - ADAM_B2 ** ADAM_STEP

    def body(p_ref, w_ref, m_ref, v_ref, g_ref, d_ref, nm_ref, nv_ref):
        g = p_ref[0].astype(F32)
        for s in range(1, n):
            g = g + p_ref[s].astype(F32)
        nm = ADAM_B1 * m_ref[...] + (1.0 - ADAM_B1) * g
        nv = ADAM_B2 * v_ref[...] + (1.0 - ADAM_B2) * (g * g)
        g_ref[...] = g
        nm_ref[...] = nm
        nv_ref[...] = nv
        d_ref[...] = -ADAM_LR * ((nm / c1) / (jnp.sqrt(nv / c2) + ADAM_EPS) + ADAM_WD * w_ref[...])

    blk = pl.BlockSpec((tr, tc), lambda i, j: (i, j))
    sd = jax.ShapeDtypeStruct((R, C), F32)
    return pl.pallas_call(
        body, name=name,
        out_shape=(sd, sd, sd, sd),
        grid=(R // tr, C // tc),
        in_specs=[pl.BlockSpec((n, tr, tc), lambda i, j: (0, i, j)), blk, blk, blk],
        out_specs=(blk, blk, blk, blk),
        compiler_params=_params(("parallel", "parallel")),
    )(parts, w, m, v)


class _Exchange:
    def __init__(self, items):
        self.items = [(src if sc else [(src, 0)], sc) for src, sc in items]
        self.arrays = [arr for srcs, _ in self.items for arr, _ in srcs]
        self.n = len(self.items)
        self.n_in = len(self.arrays)

    def out_shape(self):
        return tuple(jax.ShapeDtypeStruct((NDEV,) + (srcs[0][0].shape[1:] if sc else srcs[0][0].shape),
                                          srcs[0][0].dtype) for srcs, sc in self.items)

    def scratch(self):
        return [pltpu.SemaphoreType.DMA((self.n, NDEV - 1)), pltpu.SemaphoreType.DMA((self.n, NDEV - 1)),
                pltpu.SemaphoreType.DMA((self.n,))]

    def _emit(self, ins, outs, sems, phase):
        send_sems, recv_sems, loc_sems = sems
        x, y, c = lax.axis_index("x"), lax.axis_index("y"), lax.axis_index("c")
        me = 4 * x + 2 * y + c
        flip = lambda p, d: 1 - p if d else p

        def inside(p, lo, n):
            return None if (lo, n) == (0, NDEV) else jnp.logical_and(p >= lo, p < lo + n)

        def when(cond, fn):
            if cond is None:
                fn()
            else:
                pl.when(cond)(fn)

        pos = 0
        for a, (srcs, sc) in enumerate(self.items):
            refs = ins[pos:pos + len(srcs)]
            pos += len(srcs)
            ranges = [(lo, arr.shape[0]) if sc else (0, NDEV) for arr, lo in srcs]
            mine = [inside(me, lo, n) for lo, n in ranges]
            i_receive = None if None in mine else functools.reduce(jnp.logical_or, mine)
            for ref, (lo, n), cond in zip(refs, ranges, mine):
                def local(ref=ref, lo=lo):
                    cp = pltpu.make_async_copy(ref.at[me - lo] if sc else ref, outs[a].at[me], loc_sems.at[a])
                    cp.start() if phase == 0 else cp.wait()
                if phase != 1:
                    when(cond, local)
            for k in range(1, NDEV):
                px, py, pc = flip(x, k & 4), flip(y, k & 2), flip(c, k & 1)
                peer = 4 * px + 2 * py + pc
                mk = functools.partial(pltpu.make_async_remote_copy,
                                       send_sem=send_sems.at[a, k - 1], recv_sem=recv_sems.at[a, k - 1],
                                       device_id=(px, py, pc), device_id_type=MESH_ID)
                if phase == 1:
                    def arrival(mk=mk, peer=peer):
                        mk(src_ref=refs[0].at[0] if sc else refs[0], dst_ref=outs[a].at[peer]).wait_recv()
                    when(i_receive, arrival)
                    continue
                for ref, (lo, n) in zip(refs, ranges):
                    def send(mk=mk, ref=ref, lo=lo, peer=peer):
                        cp = mk(src_ref=ref.at[peer - lo] if sc else ref, dst_ref=outs[a].at[me])
                        cp.start() if phase == 0 else cp.wait_send()
                    when(inside(peer, lo, n), send)

    def start(self, ins, outs, sems):
        self._emit(ins, outs, sems, 0)

    def wait(self, ins, outs, sems):
        self._emit(ins, outs, sems, 1)
        self._emit(ins, outs, sems, 2)


def _call(body, *, name, grid, in_specs, out_specs, out_shape, args, scratch_shapes=(), sem=None, ride=None,
          vmem_mb=V7X_VMEM_LIMIT_MB):
    if ride is None:
        return pl.pallas_call(body, name=name, grid=grid, in_specs=list(in_specs), out_specs=tuple(out_specs),
                              out_shape=tuple(out_shape), scratch_shapes=list(scratch_shapes),
                              compiler_params=_params(sem, vmem_mb))(*args)
    n_in, n_out, n_scr, ne, ne_in = len(args), len(out_shape), len(scratch_shapes), ride.n, ride.n_in

    def ride_body(*refs):
        ins, ex_in = refs[:n_in], refs[n_in:n_in + ne_in]
        o0 = n_in + ne_in
        outs, ex_out = refs[o0:o0 + n_out], refs[o0 + n_out:o0 + n_out + ne]
        scr = refs[o0 + n_out + ne:o0 + n_out + ne + n_scr]
        sems = refs[o0 + n_out + ne + n_scr:]
        first = functools.reduce(jnp.logical_and, [pl.program_id(d) == 0 for d in range(len(grid))])
        last = functools.reduce(jnp.logical_and, [pl.program_id(d) == grid[d] - 1 for d in range(len(grid))])

        @pl.when(first)
        def _():
            ride.start(ex_in, ex_out, sems)

        body(*ins, *outs, *scr)

        @pl.when(last)
        def _():
            ride.wait(ex_in, ex_out, sems)

    anyspec = pl.BlockSpec(memory_space=pl.ANY)
    res = pl.pallas_call(
        ride_body, name=name, grid=grid,
        in_specs=list(in_specs) + [anyspec] * ne_in,
        out_specs=tuple(out_specs) + (anyspec,) * ne,
        out_shape=tuple(out_shape) + ride.out_shape(),
        scratch_shapes=list(scratch_shapes) + ride.scratch(),
        compiler_params=_params(("arbitrary",) * len(grid), vmem_mb),
    )(*args, *ride.arrays)
    return tuple(res[:n_out]), tuple(res[n_out:])


def _gather_two_level(arrays, *, name):
    n = len(arrays)

    def body(*refs):
        ins, outs = refs[:n], refs[n:2 * n]
        send_sems, recv_sems, loc_sems = refs[2 * n:]
        x, y, c = lax.axis_index("x"), lax.axis_index("y"), lax.axis_index("c")
        sibling = (x, y, 1 - c)
        chips = [(1 - x, y), (x, 1 - y), (1 - x, 1 - y)]
        idx = lambda px, py, pc: 4 * px + 2 * py + pc
        me = idx(x, y, c)

        def copy(a, k, block, to, src=None):
            return pltpu.make_async_remote_copy(
                src_ref=outs[a].at[block] if src is None else src, dst_ref=outs[a].at[block],
                send_sem=send_sems.at[a, k], recv_sem=recv_sems.at[a, k], device_id=to, device_id_type=MESH_ID)

        local = [pltpu.make_async_copy(ins[a], outs[a].at[me], loc_sems.at[a]) for a in range(n)]
        sent = []
        for a in range(n):
            sent.append(copy(a, 0, me, sibling, src=ins[a]))
            sent += [copy(a, 1 + j, me, (*chip, c), src=ins[a]) for j, chip in enumerate(chips)]
        for cp in local + sent:
            cp.start()
        for j, chip in enumerate(chips):
            for a in range(n):
                copy(a, 1 + j, idx(*chip, c), sibling).wait_recv()
                passed = copy(a, 4 + j, idx(*chip, c), sibling)
                passed.start()
                sent.append(passed)
        for a in range(n):
            copy(a, 0, idx(x, y, 1 - c), sibling).wait_recv()
            for j, chip in enumerate(chips):
                copy(a, 4 + j, idx(*chip, 1 - c), sibling).wait_recv()
        for cp in sent:
            cp.wait_send()
        for cp in local:
            cp.wait()

    anyspec = pl.BlockSpec(memory_space=pl.ANY)
    return pl.pallas_call(
        body, name=name,
        out_shape=tuple(jax.ShapeDtypeStruct((NDEV,) + a.shape, a.dtype) for a in arrays),
        in_specs=[anyspec] * n, out_specs=(anyspec,) * n,
        scratch_shapes=[pltpu.SemaphoreType.DMA((n, NDEV - 1)), pltpu.SemaphoreType.DMA((n, NDEV - 1)),
                        pltpu.SemaphoreType.DMA((n,))],
    )(*arrays)


def _tri_consts():
    r = lax.broadcasted_iota(jnp.int32, (GC, GC), 0)
    c = lax.broadcasted_iota(jnp.int32, (GC, GC), 1)
    return (r >= c).astype(BF), (r <= c).astype(BF)


def _local_step(x, positions, target, w, hooks=None):
    g = {}

    def run(host, fn, *a, **kw):
        h = None if hooks is None else hooks.get(host)
        if h is None:
            return fn(*a, **kw)
        out, received = fn(*a, ride=_Exchange(h[0](w, g)), **kw)
        h[1](received, w, g)
        return out

    nseq, S, _ = x.shape
    T = nseq * S
    tm = min(TOKEN_TM, S)
    tq = min(FLASH_TQ, S)
    x2 = x.reshape(T, D)
    pos = positions.reshape(T, 1)
    half = ROPE // 2
    inv = THETA ** (-jnp.arange(half, dtype=F32) / half)
    invf = jnp.concatenate([inv, inv, jnp.zeros((64,), F32)]).reshape(1, 128)
    ltri, utri = _tri_consts()

    pt, xb = _matmul(x2, w["w_tt"], "nt", name="proj_t", out_dtype=BF, tm=1024, tn=1024, tk=1024, emit_a=True)
    pg = run("proj_g", _matmul, xb, w["w_gt"], "nt", name="proj_g", tm=1024, tn=640, tk=1024)
    pm = _matmul(xb, w["w_mt"], "nt", name="proj_m", tm=1024, tn=768, tk=1024)
    o, zg, states = run("gla_fwd", _gla_fwd, pg, w["wg"], w["bg"], w["gn"], ltri, nseq=nseq, S=S, tm=tm)
    qc, kc, v = _mla_prep_fwd(pm, pos, invf, w["gq"], w["gkv"], w["wuq"], w["wukv"], tm=tm)
    attn, lse = run("flash_fwd", _flash_fwd, qc, kc, v, nseq=nseq, S=S, tq=tq)
    yg, ym, mix, pre1, h1b = run("post_attn_fwd", _post_attn_fwd, zg, attn, pt, x2, w["wgo"], w["wmo"], w["wout"],
                                 w["g1"], w["b1"], tm=tm)
    ug, uv, ucg, ucv, f_in = _ffn_up_fwd(h1b, w["wug"], w["wuv"], w["cw"], w["cb"], S=S, tm=tm, tn=FFN_TN)
    dpre2b, dh1, loss8, dg2, db2 = _down_ln2_loss(f_in, w["wd"], pre1, target.reshape(T, D), w["g1"], w["b1"],
                                                  w["g2"], w["b2"], tm=min(2 * tm, S))

    dug, duv, dcg, dcv = _ffn_bwd(dpre2b, w["wd"], ug, uv, ucg, ucv, w["cw"], S=S, tm=tm, tn=FFN_TN)
    g["g2"], g["b2"], g["loss"] = dg2, db2, loss8[0:1, 0:1]
    g["cw"] = jnp.concatenate([dcg[0:3], dcv[0:3]], axis=1)
    g["cb"] = jnp.concatenate([dcg[3:4], dcv[3:4]], axis=1)
    g["wd"] = _matmul(f_in, dpre2b, "tn", name="dw_down", out_dtype=BF, tm=1408, tn=1024, tk=1024)
    g["wugt"] = _matmul(dug, h1b, "tn", name="dw_up_g", out_dtype=BF, tm=1408, tn=1024, tk=1024)
    g["wuvt"] = _matmul(duv, h1b, "tn", name="dw_up_v", out_dtype=BF, tm=1408, tn=1024, tk=1024)
    dh1 = _matmul(dug, w["wugt"], "nn", name="dh1_g", c_in=dh1, tm=1024, tn=1024, tk=1408)
    dh1 = _matmul(duv, w["wuvt"], "nn", name="dh1_v", c_in=dh1, tm=1024, tn=1024, tk=1408)
    dx, dpre1b, dpt, dygb, dymb, dzg, dattn, dg1, db1 = _post_attn_bwd(
        dh1, pre1, pt, yg, ym, w["wgo"], w["wmo"], w["wout"], w["g1"], tm=tm)
    g["g1"], g["b1"] = dg1, db1
    g["wout"] = _matmul(mix, dpre1b, "tn", name="dw_out", out_dtype=BF, tm=1024, tn=1024, tk=1024)
    g["wgo"] = _matmul(zg, dygb, "tn", name="dw_gla_o", out_dtype=BF, tm=1024, tn=1024, tk=1024)
    g["wmo"] = _matmul(attn, dymb, "tn", name="dw_mla_o", out_dtype=BF, tm=1024, tn=1024, tk=1024)
    dqc, dkc, dv = run("flash_bwd", _flash_bwd, qc, kc, v, attn, dattn, lse, nseq=nseq, S=S, tq=tq)
    dpm, g["wuq"], g["wukv"], g["gq"], g["gkv"] = _mla_prep_bwd(
        pm, pos, invf, w["gq"], w["gkv"], w["wuq"], w["wukv"], dqc, dkc, dv, tm=tm)
    g["w_mt"] = _matmul(dpm, xb, "tn", name="dw_in_m", out_dtype=BF, tm=768, tn=1024, tk=1024)
    g["w_tt"] = _matmul(dpt, xb, "tn", name="dw_in_t", out_dtype=BF, tm=1024, tn=1024, tk=1024)
    dpg, g["wg"], g["bg"], g["gn"] = run("gla_bwd", _gla_bwd, pg, w["wg"], w["bg"], w["gn"], ltri, utri, o, states,
                                         dzg, nseq=nseq, S=S, tm=tm)
    g["w_gt"] = _matmul(dpg, xb, "tn", name="dw_in_g", out_dtype=BF, tm=640, tn=1024, tk=1024)
    dx = run("dx", _matmul_sum, dx, [(dpg, w["w_gt"], 640), (dpm, w["w_mt"], 768)], name="dx_gm")
    dx = _matmul_sum(dx, [(dpt, w["w_tt"], 1024)], name="dx_t")
    return loss8[0, 0], dx.reshape(nseq, S, D), g


_IN_SPLITS = (512, 512, 1024, 16, 1024, 384, 256, 64, 1024, 1024)


def _w_in_to_groups(wt):
    offs = [0]
    for s in _IN_SPLITS:
        offs.append(offs[-1] + s)
    q, k, v, r, og, cq, ckv, kr, ga, gb = [wt[offs[i]:offs[i + 1]] for i in range(10)]
    z = lambda n: jnp.zeros((n, wt.shape[1]), wt.dtype)
    return (jnp.concatenate([q, k, v, og, r, z(112)], axis=0),
            jnp.concatenate([cq, kr, z(64), ckv], axis=0),
            jnp.concatenate([ga, gb], axis=0))


W_IN_BLOCK = sum(_IN_SPLITS) // NDEV
_KV_LATENT_ROW = sum(_IN_SPLITS[:6])
_W_IN_LO = 5
_W_IN_SPLIT = _W_IN_LO * W_IN_BLOCK - _KV_LATENT_ROW


def _w_in_rows_lo(g_g, g_m):
    q, k, v, og, r = g_g[0:512], g_g[512:1024], g_g[1024:2048], g_g[2048:3072], g_g[3072:3088]
    return jnp.concatenate([q, k, v, r, og, g_m[0:384], g_m[512:768]], axis=0)[:_W_IN_LO * W_IN_BLOCK]


def _w_in_rows_hi(g_m, g_t):
    return jnp.concatenate([g_m[512:768], g_m[384:448], g_t], axis=0)[_W_IN_SPLIT:]


def _uq_to_kernel(wuq):
    w3 = wuq.reshape(MQR, MH, NOPE + ROPE)
    rope = jnp.concatenate([w3[:, :, NOPE:], jnp.zeros((MQR, MH, 64), wuq.dtype)], axis=2)
    return jnp.concatenate([w3[:, :, :NOPE].reshape(MQR, MH * 128), rope.reshape(MQR, MH * 128)], axis=1)


def _uq_from_kernel(g):
    nope = g[:, :1024].reshape(MQR, MH, 128)
    rope = g[:, 1024:].reshape(MQR, MH, 128)[:, :, :ROPE]
    return jnp.concatenate([nope, rope], axis=2)


def _ukv_to_kernel(wukv):
    w3 = wukv.reshape(MKR, MH, NOPE + MV)
    return jnp.concatenate([w3[:, :, :NOPE].reshape(MKR, MH * 128), w3[:, :, NOPE:].reshape(MKR, MH * 128)], axis=1)


def _ukv_from_kernel(g):
    return jnp.concatenate([g[:, :1024].reshape(MKR, MH, 128), g[:, 1024:].reshape(MKR, MH, 128)], axis=2)


def _cols_gathered(a):
    return a.transpose(1, 0, 2).reshape(a.shape[1], NDEV * a.shape[2])


def _cols_scattered(a):
    R = a.shape[0]
    return a.reshape(R, NDEV, a.shape[1] // NDEV).transpose(1, 0, 2)


_SMALL = (("gla_b_gate", 512), ("gla_norm_g", 256), ("mla_q_norm_g", 384), ("mla_kv_norm_g", 256),
          ("ln1_g", 1024), ("ln1_b", 1024), ("conv_b", 5632), ("ln2_g", 1024), ("ln2_b", 1024))
_SMALL_ROWS = 88
_SMALL_USED = sum(sz for _, sz in _SMALL)


def _pack_small(d):
    flat = jnp.concatenate([d[n].reshape(-1) for n, _ in _SMALL] + ([d['loss'].reshape(-1)] if 'loss' in d else []))
    return jnp.pad(flat, (0, _SMALL_ROWS * 128 - flat.shape[0])).reshape(_SMALL_ROWS, 128)


def _unpack_small(a):
    flat = a.reshape(-1)
    out, off = {}, 0
    for n, sz in _SMALL:
        out[n] = flat[off:off + sz].reshape(1, sz)
        off += sz
    return out


_NAMES = ['w_in', 'gla_w_gate_up', 'gla_b_gate', 'gla_norm_g', 'w_gla_o', 'mla_q_norm_g', 'mla_w_uq',
          'mla_kv_norm_g', 'mla_w_ukv', 'w_mla_o', 'w_out', 'ln1_g', 'ln1_b', 'w_up', 'conv_w', 'conv_b',
          'w_down', 'ln2_g', 'ln2_b']
_SHARDED = ['w_in', 'w_up', 'w_down', 'w_gla_o', 'w_mla_o', 'w_out', 'mla_w_uq', 'mla_w_ukv', 'gla_w_gate_up',
            'conv_w']


def kernel(x, positions, w_in, gla_w_gate_up, gla_b_gate, gla_norm_g, w_gla_o, mla_q_norm_g, mla_w_uq, mla_kv_norm_g, mla_w_ukv, w_mla_o, w_out, ln1_g, ln1_b, w_up, conv_w, conv_b, w_down, ln2_g, ln2_b, loss_target, m_w_in, m_gla_w_gate_up, m_gla_b_gate, m_gla_norm_g, m_w_gla_o, m_mla_q_norm_g, m_mla_w_uq, m_mla_kv_norm_g, m_mla_w_ukv, m_w_mla_o, m_w_out, m_ln1_g, m_ln1_b, m_w_up, m_conv_w, m_conv_b, m_w_down, m_ln2_g, m_ln2_b, v_w_in, v_gla_w_gate_up, v_gla_b_gate, v_gla_norm_g, v_w_gla_o, v_mla_q_norm_g, v_mla_w_uq, v_mla_kv_norm_g, v_mla_w_ukv, v_w_mla_o, v_w_out, v_ln1_g, v_ln1_b, v_w_up, v_conv_w, v_conv_b, v_w_down, v_ln2_g, v_ln2_b):
    W = dict(w_in=w_in, gla_w_gate_up=gla_w_gate_up, gla_b_gate=gla_b_gate, gla_norm_g=gla_norm_g, w_gla_o=w_gla_o, mla_q_norm_g=mla_q_norm_g, mla_w_uq=mla_w_uq, mla_kv_norm_g=mla_kv_norm_g, mla_w_ukv=mla_w_ukv, w_mla_o=w_mla_o, w_out=w_out, ln1_g=ln1_g, ln1_b=ln1_b, w_up=w_up, conv_w=conv_w, conv_b=conv_b, w_down=w_down, ln2_g=ln2_g, ln2_b=ln2_b)
    M = dict(w_in=m_w_in, gla_w_gate_up=m_gla_w_gate_up, gla_b_gate=m_gla_b_gate, gla_norm_g=m_gla_norm_g, w_gla_o=m_w_gla_o, mla_q_norm_g=m_mla_q_norm_g, mla_w_uq=m_mla_w_uq, mla_kv_norm_g=m_mla_kv_norm_g, mla_w_ukv=m_mla_w_ukv, w_mla_o=m_w_mla_o, w_out=m_w_out, ln1_g=m_ln1_g, ln1_b=m_ln1_b, w_up=m_w_up, conv_w=m_conv_w, conv_b=m_conv_b, w_down=m_w_down, ln2_g=m_ln2_g, ln2_b=m_ln2_b)
    V = dict(w_in=v_w_in, gla_w_gate_up=v_gla_w_gate_up, gla_b_gate=v_gla_b_gate, gla_norm_g=v_gla_norm_g, w_gla_o=v_w_gla_o, mla_q_norm_g=v_mla_q_norm_g, mla_w_uq=v_mla_w_uq, mla_kv_norm_g=v_mla_kv_norm_g, mla_w_ukv=v_mla_w_ukv, w_mla_o=v_w_mla_o, w_out=v_w_out, ln1_g=v_ln1_g, ln1_b=v_ln1_b, w_up=v_w_up, conv_w=v_conv_w, conv_b=v_conv_b, w_down=v_w_down, ln2_g=v_ln2_g, ln2_b=v_ln2_b)

    tshard = lambda d, n: d[n][0].T
    shard = lambda n: (W[n][0].astype(BF), False)
    (w_in_t,) = _gather_two_level([tshard(W, 'w_in').astype(BF)], name="gather_w0")
    w_gt, w_mt, w_tt = _w_in_to_groups(w_in_t.reshape(NDEV * W_IN_BLOCK, D))
    kw = dict(
        w_gt=w_gt, w_mt=w_mt, w_tt=w_tt, bg=W['gla_b_gate'],
        gn=W['gla_norm_g'], gq=W['mla_q_norm_g'], gkv=W['mla_kv_norm_g'],
        g1=W['ln1_g'], b1=W['ln1_b'], g2=W['ln2_g'], b2=W['ln2_b'], cb=W['conv_b'],
    )
    received = {}

    def got_mixers(ex, w, g):
        w.update(wuq=_uq_to_kernel(_cols_gathered(ex[0])), wukv=_ukv_to_kernel(_cols_gathered(ex[1])),
                 wg=jnp.pad(_cols_gathered(ex[2]), ((0, 128 - GR), (0, 0))))

    def got_out_proj(ex, w, g):
        w.update(wgo=ex[0].reshape(D, D), wmo=ex[1].reshape(D, D), wout=ex[2].reshape(D, D))

    def got_up(ex, w, g):
        w_upt = ex[0].reshape(2 * DFF, D)
        w.update(wugt=w_upt[:DFF], wuvt=w_upt[DFF:], wug=w_upt[:DFF].T, wuv=w_upt[DFF:].T)

    def got_down(ex, w, g):
        w.update(wd=ex[0].reshape(DFF, D), cw=_cols_gathered(ex[1]))

    slab = lambda a, lo=0: ([(a.astype(BF), lo)], True)
    rows = lambda a, n=NDEV: a.reshape(n, a.shape[0] // n, a.shape[1])

    def keep(names):
        return lambda ex, w, g: received.update(zip(names, ex))

    def small_grads(g):
        return _pack_small(dict(gla_b_gate=g['bg'], gla_norm_g=g['gn'], mla_q_norm_g=g['gq'], mla_kv_norm_g=g['gkv'],
                                ln1_g=g['g1'], ln1_b=g['b1'], conv_b=g['cb'], ln2_g=g['g2'], ln2_b=g['b2'],
                                loss=g['loss']))

    hooks = {
        "proj_g": (lambda w, g: [shard('mla_w_uq'), shard('mla_w_ukv'), shard('gla_w_gate_up')], got_mixers),
        "gla_fwd": (lambda w, g: [shard('w_gla_o'), shard('w_mla_o'), shard('w_out')], got_out_proj),
        "flash_fwd": (lambda w, g: [(tshard(W, 'w_up').astype(BF), False)], got_up),
        "post_attn_fwd": (lambda w, g: [shard('w_down'), (W['conv_w'][0], False)], got_down),
        "flash_bwd": (lambda w, g: [slab(rows(g['wd'])),
                                    ([(rows(g['wugt'], 4), 0), (rows(g['wuvt'], 4), 4)], True),
                                    slab(rows(g['wout'])), slab(rows(g['wgo'])), slab(rows(g['wmo']))],
                      keep(['w_down', 'w_up', 'w_out', 'w_gla_o', 'w_mla_o'])),
        "gla_bwd": (lambda w, g: [slab(_uq_from_kernel(g['wuq']).transpose(1, 0, 2)),
                                  slab(_ukv_from_kernel(g['wukv']).transpose(1, 0, 2)),
                                  slab(rows(_w_in_rows_hi(g['w_mt'], g['w_tt']), NDEV - _W_IN_LO), _W_IN_LO)],
                    keep(['mla_w_uq', 'mla_w_ukv', 'w_in_hi'])),
        "dx": (lambda w, g: [slab(rows(_w_in_rows_lo(g['w_gt'], g['w_mt']), _W_IN_LO)),
                             ([(_cols_scattered(g['wg'][:GR]), 0)], True), ([(_cols_scattered(g['cw']), 0)], True),
                             (small_grads(g), False)],
               keep(['w_in_lo', 'gla_w_gate_up', 'conv_w', 'small'])),
    }

    _, grad_x, _ = _local_step(x, positions, loss_target, kw, hooks)

    grads, deltas, new_m, new_v = {}, {}, {}, {}
    small_parts = received['small']
    loss = jnp.sum(small_parts.reshape(NDEV, -1)[:, _SMALL_USED])
    me = 4 * lax.axis_index("x") + 2 * lax.axis_index("y") + lax.axis_index("c")
    received['w_in'] = jnp.where(me >= _W_IN_LO, received['w_in_hi'], received['w_in_lo'])
    for n in _SHARDED:
        shp = W[n].shape
        if n in ('w_in', 'w_up'):
            out = _adamw(received[n], tshard(W, n), tshard(M, n), tshard(V, n), name="adamw_" + n)
            grads[n], deltas[n], new_m[n], new_v[n] = [t.T.reshape(shp) for t in out]
            continue
        out = _adamw(received[n], W[n][0], M[n][0], V[n][0], name="adamw_" + n)
        grads[n], deltas[n], new_m[n], new_v[n] = [t.reshape(shp) for t in out]
    out = _adamw(small_parts, _pack_small(W), _pack_small(M), _pack_small(V), name="adamw_small")
    for dst, packed in zip((grads, deltas, new_m, new_v), out):
        dst.update(_unpack_small(packed))

    return (loss, grad_x, *[grads[n] for n in _NAMES], *[deltas[n] for n in _NAMES],
            *[new_m[n] for n in _NAMES], *[new_v[n] for n in _NAMES])
```

```python
import functools

import jax
import jax.numpy as jnp
from jax import lax
from jax.experimental import pallas as pl
from jax.experimental.pallas import tpu as pltpu

F32 = jnp.float32
BF = jnp.bfloat16

D = 1024
GH, GDK, GDV, GR, GTAU, GC = 4, 128, 256, 16, 16.0, 64
MH, MQR, MKR, NOPE, ROPE, MV = 8, 384, 256, 128, 64, 128
THETA = 10000.0
DFF = 2816
ALPHA = 2.0 ** 0.25
LN_EPS = 1e-5
RMS_EPS = 1e-6
NDEV = 8
ADAM_LR, ADAM_B1, ADAM_B2, ADAM_EPS, ADAM_WD, ADAM_STEP = 0.001, 0.9, 0.999, 1e-08, 0.01, 10

PG_W = 3200
PM_W = 768
PT_W = 2048
NEG = -1e30
MESH_ID = pl.DeviceIdType.MESH
VMEM_MB = 1024 * 1024


V7X_VMEM_LIMIT_MB = 48
TOKEN_TM = 256
FLASH_TQ = 512
FFN_TN = 1408


def _params(sem):
    return pltpu.CompilerParams(dimension_semantics=sem, vmem_limit_bytes=V7X_VMEM_LIMIT_MB * VMEM_MB)


def _dot(a, b):
    return lax.dot_general(a, b, (((1,), (0,)), ((), ())), preferred_element_type=F32)


def _dot_nt(a, b):
    return lax.dot_general(a, b, (((1,), (1,)), ((), ())), preferred_element_type=F32)


def _dot_tn(a, b):
    return lax.dot_general(a, b, (((0,), (0,)), ((), ())), preferred_element_type=F32)


def _iota(shape, dim):
    return lax.broadcasted_iota(jnp.int32, shape, dim)


FLASH_HP = 2
FLASH_HP_FWD = 4
QK_SCALE = (NOPE + ROPE) ** -0.5
LOG2E = 1.4426950408889634
QK_SCALE_LOG2 = QK_SCALE * LOG2E


def _sigmoid(x):
    return 0.5 * jnp.tanh(0.5 * x) + 0.5


def _tri_mm(tri_bf, x):
    hi = x.astype(BF)
    r1 = x - hi.astype(F32)
    mid = r1.astype(BF)
    lo = (r1 - mid.astype(F32)).astype(BF)
    return _dot(tri_bf, hi) + _dot(tri_bf, mid) + _dot(tri_bf, lo)


def _matmul(a, b, mode, *, name, c_in=None, out_dtype=F32, tm=512, tn=512, tk=512, ride=None, emit_a=False):
    if mode == "nn":
        (M, K), (_, N) = a.shape, b.shape
    elif mode == "nt":
        (M, K), (N, _) = a.shape, b.shape
    else:
        (K, M), (_, N) = a.shape, b.shape
    tm, tn, tk = min(tm, M), min(tn, N), min(tk, K)
    assert M % tm == 0 and N % tn == 0 and K % tk == 0, (name, M, N, K, tm, tn, tk)
    nk = K // tk
    assert not emit_a or (nk == 1 and mode != "tn" and c_in is None and ride is None)
    dot = {"nn": _dot, "nt": _dot_nt, "tn": _dot_tn}[mode]

    def body(*refs):
        if emit_a:
            a_ref, b_ref, o_ref, xa_ref, acc_ref = refs
        elif c_in is None:
            a_ref, b_ref, o_ref, acc_ref = refs
        else:
            a_ref, b_ref, c_ref, o_ref, acc_ref = refs
        k = pl.program_id(2)

        @pl.when(k == 0)
        def _():
            if c_in is None:
                acc_ref[...] = jnp.zeros_like(acc_ref)
            else:
                acc_ref[...] = c_ref[...].astype(F32)

        if emit_a:
            @pl.when(pl.program_id(1) == 0)
            def _():
                xa_ref[...] = a_ref[...].astype(BF)

        acc_ref[...] += dot(a_ref[...].astype(BF), b_ref[...].astype(BF))

        @pl.when(k == nk - 1)
        def _():
            o_ref[...] = acc_ref[...].astype(out_dtype)

    if mode == "tn":
        a_spec = pl.BlockSpec((tk, tm), lambda i, j, k: (k, i))
    else:
        a_spec = pl.BlockSpec((tm, tk), lambda i, j, k: (i, k))
    if mode == "nt":
        b_spec = pl.BlockSpec((tn, tk), lambda i, j, k: (j, k))
    else:
        b_spec = pl.BlockSpec((tk, tn), lambda i, j, k: (k, j))
    in_specs = [a_spec, b_spec]
    args = [a, b]
    if c_in is not None:
        in_specs.append(pl.BlockSpec((tm, tn), lambda i, j, k: (i, j)))
        args.append(c_in)
    out_shape = (jax.ShapeDtypeStruct((M, N), out_dtype),)
    out_specs = (pl.BlockSpec((tm, tn), lambda i, j, k: (i, j)),)
    if emit_a:
        out_shape += (jax.ShapeDtypeStruct((M, K), BF),)
        out_specs += (pl.BlockSpec((tm, tk), lambda i, j, k: (i, k)),)
    res = _call(
        body, name=name, out_shape=out_shape, grid=(M // tm, N // tn, nk), in_specs=in_specs, out_specs=out_specs,
        scratch_shapes=[pltpu.VMEM((tm, tn), F32)],
        sem=("parallel", "arbitrary", "arbitrary"), args=args, ride=ride)
    if emit_a:
        return res[0], res[1]
    return res[0] if ride is None else (res[0][0], res[1])


def _matmul_sum(c_in, parts, *, name, tm=1024, ride=None):
    M, N = c_in.shape
    tm = min(tm, M)
    n_p = len(parts)
    counts = [a.shape[1] // tk for a, _, tk in parts]
    starts = [sum(counts[:p]) for p in range(n_p)]
    nk = sum(counts)

    def body(*refs):
        a_refs, w_refs = refs[:n_p], refs[n_p:2 * n_p]
        c_ref, o_ref, acc_ref = refs[2 * n_p:]
        k = pl.program_id(1)

        @pl.when(k == 0)
        def _():
            acc_ref[...] = c_ref[...]

        for p in range(n_p):
            @pl.when(jnp.logical_and(k >= starts[p], k < starts[p] + counts[p]))
            def _(p=p):
                acc_ref[...] += _dot(a_refs[p][...].astype(BF), w_refs[p][...].astype(BF))

        @pl.when(k == nk - 1)
        def _():
            o_ref[...] = acc_ref[...]

    def kidx(p):
        return lambda k: jnp.clip(k - starts[p], 0, counts[p] - 1)

    in_specs = [pl.BlockSpec((tm, tk), lambda i, k, f=kidx(p): (i, f(k))) for p, (_, _, tk) in enumerate(parts)]
    in_specs += [pl.BlockSpec((tk, N), lambda i, k, f=kidx(p): (f(k), 0)) for p, (_, _, tk) in enumerate(parts)]
    in_specs.append(pl.BlockSpec((tm, N), lambda i, k: (i, 0)))
    res = _call(
        body, name=name, out_shape=(jax.ShapeDtypeStruct((M, N), F32),), grid=(M // tm, nk),
        in_specs=in_specs, out_specs=(pl.BlockSpec((tm, N), lambda i, k: (i, 0)),),
        scratch_shapes=[pltpu.VMEM((tm, N), F32)], sem=("parallel", "arbitrary"),
        args=[a for a, _, _ in parts] + [w for _, w, _ in parts] + [c_in], ride=ride)
    return res[0] if ride is None else (res[0][0], res[1])


def _gla_gate(pg_ref, rows, wg_ref, bg_ref):
    r = pg_ref[rows, 3072:3200].astype(BF)
    logit = _dot(r, wg_ref[...]) + bg_ref[...]
    la = (jnp.minimum(logit, 0.0) - jnp.log(1.0 + jnp.exp(-jnp.abs(logit)))) * (1.0 / GTAU)
    return r, logit, la


def _gla_fwd(pg, wg, bg, gn, ltri, *, nseq, S, tm, ride=None):
    T = pg.shape[0]
    nb, nc = S // tm, tm // GC
    qscale = GDK ** -0.5

    def body(pg_ref, wg_ref, bg_ref, gn_ref, l_ref, o_ref, zg_ref, st_ref, st_scr):
        @pl.when(pl.program_id(1) == 0)
        def _():
            st_scr[...] = jnp.zeros_like(st_scr)

        ltri_v = l_ref[...]
        causal = _iota((GC, GC), 0) >= _iota((GC, GC), 1)
        last_row = _iota((GC, GDK), 0) == GC - 1
        g = gn_ref[...]

        def chunk(c, carry):
            rows = pl.ds(pl.multiple_of(c * GC, GC), GC)
            _, _, la = _gla_gate(pg_ref, rows, wg_ref, bg_ref)
            b = _tri_mm(ltri_v, la)
            hs = range(GH)
            v, q_in, k_st, dec, st, a_raw, o_st, kv = [], [], [], [], [], [], [], []
            for h in hs:
                q = pg_ref[rows, h * GDK:(h + 1) * GDK]
                k = pg_ref[rows, 512 + h * GDK:512 + (h + 1) * GDK]
                v.append(pg_ref[rows, 1024 + h * GDV:1024 + (h + 1) * GDV].astype(BF))
                bh = b[:, h * GDK:(h + 1) * GDK]
                bl = jnp.sum(jnp.where(last_row, bh, 0.0), axis=0, keepdims=True)
                q_in.append((q * (qscale * jnp.exp(bh))).astype(BF))
                k_in = (k * jnp.exp(-bh)).astype(BF)
                k_st.append((k * jnp.exp(bl - bh)).astype(BF))
                dec.append(jnp.exp(bl))
                st.append(st_scr[h])
                st_ref[c, h] = st[h]
                a_raw.append(_dot_nt(q_in[h], k_in))
            for h in hs:
                o_st.append(_dot_nt(q_in[h], st[h].astype(BF)))
                kv.append(_dot_tn(v[h], k_st[h]))
            att = [jnp.where(causal, a_raw[h], 0.0).astype(BF) for h in hs]
            o = [_dot(att[h], v[h]) + o_st[h] for h in hs]
            for h in hs:
                st_scr[h] = st[h] * dec[h] + kv[h]
                og = pg_ref[rows, 2048 + h * GDV:2048 + (h + 1) * GDV]
                rstd = lax.rsqrt(jnp.mean(o[h] * o[h], axis=-1, keepdims=True) + RMS_EPS)
                o_ref[rows, h * GDV:(h + 1) * GDV] = o[h]
                zg_ref[rows, h * GDV:(h + 1) * GDV] = (o[h] * rstd * g * (og * _sigmoid(og))).astype(BF)
            return carry

        lax.fori_loop(0, nc, chunk, 0, unroll=True)

    full = lambda shp: pl.BlockSpec(shp, lambda b_, i: (0,) * len(shp))
    return _call(
        body, name="gla_fwd", ride=ride, sem=("parallel", "arbitrary"), args=(pg, wg, bg, gn, ltri),
        out_shape=(jax.ShapeDtypeStruct((T, GH * GDV), F32),
                   jax.ShapeDtypeStruct((T, GH * GDV), BF),
                   jax.ShapeDtypeStruct((T // GC, GH, GDV, GDK), F32)),
        grid=(nseq, nb),
        in_specs=[pl.BlockSpec((tm, PG_W), lambda b_, i: (b_ * nb + i, 0)),
                  full((128, 512)), full((1, 512)), full((1, GDV)), full((GC, GC))],
        out_specs=(pl.BlockSpec((tm, GH * GDV), lambda b_, i: (b_ * nb + i, 0)),
                   pl.BlockSpec((tm, GH * GDV), lambda b_, i: (b_ * nb + i, 0)),
                   pl.BlockSpec((nc, GH, GDV, GDK), lambda b_, i: (b_ * nb + i, 0, 0, 0))),
        scratch_shapes=[pltpu.VMEM((GH, GDV, GDK), F32)])


def _gla_bwd(pg, wg, bg, gn, ltri, utri, o, states, dzg, *, nseq, S, tm, ride=None):
    T = pg.shape[0]
    nb, nc = S // tm, tm // GC
    qscale = GDK ** -0.5

    def body(pg_ref, wg_ref, bg_ref, gn_ref, l_ref, u_ref, o_ref, st_ref, dzg_ref,
             dpg_ref, dwg_ref, dbg_ref, dgn_ref, dst_scr):
        first = jnp.logical_and(pl.program_id(0) == 0, pl.program_id(1) == 0)

        @pl.when(first)
        def _():
            dwg_ref[...] = jnp.zeros_like(dwg_ref)
            dbg_ref[...] = jnp.zeros_like(dbg_ref)
            dgn_ref[...] = jnp.zeros_like(dgn_ref)

        @pl.when(pl.program_id(1) == 0)
        def _():
            dst_scr[...] = jnp.zeros_like(dst_scr)

        ltri_v = l_ref[...]
        utri_v = u_ref[...]
        causal = _iota((GC, GC), 0) >= _iota((GC, GC), 1)
        last_row = _iota((GC, GDK), 0) == GC - 1
        g = gn_ref[...]

        def chunk(cc, carry):
            c = nc - 1 - cc
            rows = pl.ds(pl.multiple_of(c * GC, GC), GC)
            r, logit, la = _gla_gate(pg_ref, rows, wg_ref, bg_ref)
            b = _tri_mm(ltri_v, la)
            hs = range(GH)
            L = lambda: [None] * GH
            vb, eb, enb, ek, dec, q_in, k_in, k_st, q_inb, k_inb, st, dst, dob = (L() for _ in range(13))
            a_raw, da_raw, dq_st, dks, dv_st, dst_new, dbs, dgn = (L() for _ in range(8))
            for h in hs:
                q = pg_ref[rows, h * GDK:(h + 1) * GDK]
                k = pg_ref[rows, 512 + h * GDK:512 + (h + 1) * GDK]
                vb[h] = pg_ref[rows, 1024 + h * GDV:1024 + (h + 1) * GDV].astype(BF)
                og = pg_ref[rows, 2048 + h * GDV:2048 + (h + 1) * GDV]
                oh = o_ref[rows, h * GDV:(h + 1) * GDV]
                dz = dzg_ref[rows, h * GDV:(h + 1) * GDV].astype(F32)
                bh = b[:, h * GDK:(h + 1) * GDK]
                bl = jnp.sum(jnp.where(last_row, bh, 0.0), axis=0, keepdims=True)
                eb[h] = qscale * jnp.exp(bh)
                enb[h] = jnp.exp(-bh)
                ek[h] = jnp.exp(bl - bh)
                dec[h] = jnp.exp(bl)
                q_in[h], k_in[h], k_st[h] = q * eb[h], k * enb[h], k * ek[h]
                q_inb[h], k_inb[h] = q_in[h].astype(BF), k_in[h].astype(BF)
                st[h] = st_ref[c, h]
                dst[h] = dst_scr[h]
                rstd = lax.rsqrt(jnp.mean(oh * oh, axis=-1, keepdims=True) + RMS_EPS)
                ohat = oh * rstd
                sg = _sigmoid(og)
                don = dz * (og * sg)
                dpg_ref[rows, 2048 + h * GDV:2048 + (h + 1) * GDV] = (
                    dz * (ohat * g) * (sg * (1.0 + og * (1.0 - sg)))).astype(BF)
                dgn[h] = jnp.sum(don * ohat, axis=0, keepdims=True)
                gd = don * g
                dob[h] = (rstd * (gd - ohat * jnp.mean(gd * ohat, axis=-1, keepdims=True))).astype(BF)
                a_raw[h] = _dot_nt(q_inb[h], k_inb[h])
                da_raw[h] = _dot_nt(dob[h], vb[h])
            dgn_ref[...] += dgn[0] + dgn[1] + dgn[2] + dgn[3]
            for h in hs:
                dstb = dst[h].astype(BF)
                dq_st[h] = _dot(dob[h], st[h].astype(BF))
                dks[h] = _dot(vb[h], dstb)
                dv_st[h] = _dot_nt(k_st[h].astype(BF), dstb)
                dst_new[h] = _dot_tn(dob[h], q_inb[h])
            att = [jnp.where(causal, a_raw[h], 0.0).astype(BF) for h in hs]
            da = [jnp.where(causal, da_raw[h], 0.0).astype(BF) for h in hs]
            dqi = [_dot(da[h], k_inb[h]) + dq_st[h] for h in hs]
            dki = [_dot_tn(da[h], q_inb[h]) for h in hs]
            dv = [_dot_tn(att[h], dob[h]) + dv_st[h] for h in hs]
            for h in hs:
                dd = jnp.sum(dst[h] * st[h], axis=0, keepdims=True)
                dst_scr[h] = dst[h] * dec[h] + dst_new[h]
                kk = dks[h] * k_st[h]
                dbl = jnp.sum(kk, axis=0, keepdims=True) + dd * dec[h]
                db = dqi[h] * q_in[h] - dki[h] * k_in[h] - kk
                dbs[h] = db + jnp.where(last_row, dbl, 0.0)
                dpg_ref[rows, h * GDK:(h + 1) * GDK] = (dqi[h] * eb[h]).astype(BF)
                dpg_ref[rows, 512 + h * GDK:512 + (h + 1) * GDK] = (dki[h] * enb[h] + dks[h] * ek[h]).astype(BF)
                dpg_ref[rows, 1024 + h * GDV:1024 + (h + 1) * GDV] = dv[h].astype(BF)
            dla = _tri_mm(utri_v, jnp.concatenate(dbs, axis=1))
            dlogit = dla * (1.0 / GTAU) * _sigmoid(-logit)
            dlb = dlogit.astype(BF)
            dpg_ref[rows, 3072:3200] = _dot_nt(dlb, wg_ref[...]).astype(BF)
            dwg_ref[...] += _dot_tn(r, dlb)
            dbg_ref[...] += jnp.sum(dlogit, axis=0, keepdims=True)
            return carry

        lax.fori_loop(0, nc, chunk, 0, unroll=True)

    full = lambda shp: pl.BlockSpec(shp, lambda b_, i: (0,) * len(shp))
    rev = lambda b_, i: (b_ * nb + nb - 1 - i, 0)
    return _call(
        body, name="gla_bwd", ride=ride, sem=("arbitrary", "arbitrary"),
        args=(pg, wg, bg, gn, ltri, utri, o, states, dzg),
        out_shape=(jax.ShapeDtypeStruct((T, PG_W), BF),
                   jax.ShapeDtypeStruct((128, 512), F32),
                   jax.ShapeDtypeStruct((1, 512), F32),
                   jax.ShapeDtypeStruct((1, GDV), F32)),
        grid=(nseq, nb),
        in_specs=[pl.BlockSpec((tm, PG_W), rev),
                  full((128, 512)), full((1, 512)), full((1, GDV)), full((GC, GC)), full((GC, GC)),
                  pl.BlockSpec((tm, GH * GDV), rev),
                  pl.BlockSpec((nc, GH, GDV, GDK), lambda b_, i: (b_ * nb + nb - 1 - i, 0, 0, 0)),
                  pl.BlockSpec((tm, GH * GDV), rev)],
        out_specs=(pl.BlockSpec((tm, PG_W), rev), full((128, 512)), full((1, 512)), full((1, GDV))),
        scratch_shapes=[pltpu.VMEM((GH, GDV, GDK), F32)])


def _rope_tables(pos, invf):
    ang = pos.astype(F32) * invf
    lane = _iota(ang.shape, 1)
    sin = jnp.sin(ang)
    ssin = jnp.where(lane < 32, -sin, jnp.where(lane < 64, sin, 0.0))
    return jnp.cos(ang), ssin, lane


def _rope(x, cos, ssin, lane, sign):
    rot = jnp.where(lane < 32, pltpu.roll(x, 96, 1), pltpu.roll(x, 32, 1))
    return x * cos + sign * (rot * ssin)


def _rms_fwd(x, g):
    rstd = lax.rsqrt(jnp.mean(x * x, axis=-1, keepdims=True) + RMS_EPS)
    return x * rstd * g, x * rstd, rstd


def _rms_bwd(dy, xhat, rstd, g):
    gd = dy * g
    return rstd * (gd - xhat * jnp.mean(gd * xhat, axis=-1, keepdims=True)), jnp.sum(dy * xhat, axis=0, keepdims=True)


def _mla_prep_fwd(pm, pos, invf, gq, gkv, wuq, wukv, *, tm):
    T = pm.shape[0]

    def body(pm_ref, pos_ref, invf_ref, gq_ref, gkv_ref, wuq_ref, wukv_ref, qc_ref, kc_ref, v_ref):
        cos, ssin, lane = _rope_tables(pos_ref[...], invf_ref[...])
        cq, _, _ = _rms_fwd(pm_ref[:, 0:MQR], gq_ref[...])
        ckv, _, _ = _rms_fwd(pm_ref[:, 512:768], gkv_ref[...])
        qf = _dot(cq.astype(BF), wuq_ref[...])
        kvf = _dot(ckv.astype(BF), wukv_ref[...])
        kr = _rope(pm_ref[:, 384:512], cos, ssin, lane, 1.0).astype(BF)
        for h in range(MH):
            qc_ref[:, 256 * h:256 * h + 128] = (QK_SCALE_LOG2 * qf[:, 128 * h:128 * h + 128]).astype(BF)
            qr = qf[:, 1024 + 128 * h:1024 + 128 * h + 128]
            qc_ref[:, 256 * h + 128:256 * h + 256] = (QK_SCALE_LOG2 * _rope(qr, cos, ssin, lane, 1.0)).astype(BF)
            kc_ref[:, 256 * h:256 * h + 128] = kvf[:, 128 * h:128 * h + 128].astype(BF)
            kc_ref[:, 256 * h + 128:256 * h + 256] = kr
        v_ref[...] = kvf[:, 1024:2048].astype(BF)

    full = lambda shp: pl.BlockSpec(shp, lambda i: (0,) * len(shp))
    row = lambda w: pl.BlockSpec((tm, w), lambda i: (i, 0))
    return pl.pallas_call(
        body, name="mla_prep_fwd",
        out_shape=(jax.ShapeDtypeStruct((T, MH * 256), BF), jax.ShapeDtypeStruct((T, MH * 256), BF),
                   jax.ShapeDtypeStruct((T, MH * MV), BF)),
        grid=(T // tm,),
        in_specs=[row(PM_W), row(1), full((1, 128)), full((1, MQR)), full((1, MKR)),
                  full((MQR, 2048)), full((MKR, 2048))],
        out_specs=(row(MH * 256), row(MH * 256), row(MH * MV)),
        compiler_params=_params(("parallel",)),
    )(pm, pos, invf, gq, gkv, wuq, wukv)


def _mla_prep_bwd(pm, pos, invf, gq, gkv, wuq, wukv, dqc, dkc, dv, *, tm):
    T = pm.shape[0]

    def body(pm_ref, pos_ref, invf_ref, gq_ref, gkv_ref, wuq_ref, wukv_ref, dqc_ref, dkc_ref, dv_ref,
             dpm_ref, dwuq_ref, dwukv_ref, dgq_ref, dgkv_ref):
        @pl.when(pl.program_id(0) == 0)
        def _():
            dwuq_ref[...] = jnp.zeros_like(dwuq_ref)
            dwukv_ref[...] = jnp.zeros_like(dwukv_ref)
            dgq_ref[...] = jnp.zeros_like(dgq_ref)
            dgkv_ref[...] = jnp.zeros_like(dgkv_ref)

        cos, ssin, lane = _rope_tables(pos_ref[...], invf_ref[...])
        cq, cqh, cq_rstd = _rms_fwd(pm_ref[:, 0:MQR], gq_ref[...])
        ckv, ckvh, ckv_rstd = _rms_fwd(pm_ref[:, 512:768], gkv_ref[...])
        dqn, dqr, dkn = [], [], []
        dkr = jnp.zeros((tm, 128), F32)
        for h in range(MH):
            dqn.append(dqc_ref[:, 256 * h:256 * h + 128].astype(BF))
            dqr.append(_rope(dqc_ref[:, 256 * h + 128:256 * h + 256], cos, ssin, lane, -1.0).astype(BF))
            dkn.append(dkc_ref[:, 256 * h:256 * h + 128].astype(BF))
            dkr = dkr + dkc_ref[:, 256 * h + 128:256 * h + 256]
        dqf = jnp.concatenate(dqn + dqr, axis=1)
        dkvf = jnp.concatenate(dkn + [dv_ref[...].astype(BF)], axis=1)
        dwuq_ref[...] += _dot_tn(cq.astype(BF), dqf)
        dwukv_ref[...] += _dot_tn(ckv.astype(BF), dkvf)
        dcq, dgq = _rms_bwd(_dot_nt(dqf, wuq_ref[...]), cqh, cq_rstd, gq_ref[...])
        dckv, dgkv = _rms_bwd(_dot_nt(dkvf, wukv_ref[...]), ckvh, ckv_rstd, gkv_ref[...])
        dgq_ref[...] += dgq
        dgkv_ref[...] += dgkv
        dpm_ref[:, 0:MQR] = dcq.astype(BF)
        dpm_ref[:, 384:512] = _rope(dkr, cos, ssin, lane, -1.0).astype(BF)
        dpm_ref[:, 512:768] = dckv.astype(BF)

    full = lambda shp: pl.BlockSpec(shp, lambda i: (0,) * len(shp))
    row = lambda w: pl.BlockSpec((tm, w), lambda i: (i, 0))
    return pl.pallas_call(
        body, name="mla_prep_bwd",
        out_shape=(jax.ShapeDtypeStruct((T, PM_W), BF), jax.ShapeDtypeStruct((MQR, 2048), F32),
                   jax.ShapeDtypeStruct((MKR, 2048), F32), jax.ShapeDtypeStruct((1, MQR), F32),
                   jax.ShapeDtypeStruct((1, MKR), F32)),
        grid=(T // tm,),
        in_specs=[row(PM_W), row(1), full((1, 128)), full((1, MQR)), full((1, MKR)),
                  full((MQR, 2048)), full((MKR, 2048)), row(MH * 256), row(MH * 256), row(MH * MV)],
        out_specs=(row(PM_W), full((MQR, 2048)), full((MKR, 2048)), full((1, MQR)), full((1, MKR))),
        compiler_params=_params(("arbitrary",)),
    )(pm, pos, invf, gq, gkv, wuq, wukv, dqc, dkc, dv)


def _flash_fwd(qc, kc, v, *, nseq, S, tq, ride=None):
    T = qc.shape[0]
    nq = S // tq
    hp = FLASH_HP_FWD

    def body(q_ref, k_ref, v_ref, o_ref, lse_ref):
        i = pl.program_id(2)
        causal = _iota((tq, tq), 0) >= _iota((tq, tq), 1)

        def step(j, carry, masked):
            rows = pl.ds(pl.multiple_of(j * tq, tq), tq)
            hs = range(hp)
            s = [_dot_nt(q_ref[:, 256 * hh:256 * hh + 256], k_ref[rows, 256 * hh:256 * hh + 256]) for hh in hs]
            p, stats = [], []
            for hh in hs:
                m, l, _ = carry[hh]
                sh = jnp.where(causal, s[hh], NEG) if masked else s[hh]
                m_new = jnp.maximum(m, jnp.max(sh, axis=-1, keepdims=True))
                ph = jnp.exp2(sh - m_new)
                a = jnp.exp2(m - m_new)
                stats.append((m_new, a * l + jnp.sum(ph, axis=-1, keepdims=True), a))
                p.append(ph.astype(BF))
            pv = [_dot(p[hh], v_ref[rows, MV * hh:MV * hh + MV]) for hh in hs]
            return tuple((stats[hh][0], stats[hh][1], stats[hh][2] * carry[hh][2] + pv[hh]) for hh in hs)

        init = ((jnp.full((tq, 1), NEG, F32), jnp.zeros((tq, 1), F32), jnp.zeros((tq, MV), F32)),) * hp
        carry = lax.fori_loop(0, i, lambda j, c: step(j, c, False), init)
        for hh, (m, l, acc) in enumerate(step(i, carry, True)):
            o_ref[:, MV * hh:MV * hh + MV] = (acc / l).astype(BF)
            lse_ref[:, 128 * hh:128 * hh + 128] = jnp.broadcast_to(m + jnp.log2(l), (tq, 128))

    return _call(
        body, name="flash_fwd", ride=ride, sem=("parallel", "parallel", "arbitrary"), args=(qc, kc, v),
        out_shape=(jax.ShapeDtypeStruct((T, MH * MV), BF), jax.ShapeDtypeStruct((T, MH * 128), F32)),
        grid=(nseq, MH // hp, nq),
        in_specs=[pl.BlockSpec((tq, 256 * hp), lambda b_, h, i: (b_ * nq + i, h)),
                  pl.BlockSpec((S, 256 * hp), lambda b_, h, i: (b_, h)),
                  pl.BlockSpec((S, MV * hp), lambda b_, h, i: (b_, h))],
        out_specs=(pl.BlockSpec((tq, MV * hp), lambda b_, h, i: (b_ * nq + i, h)),
                   pl.BlockSpec((tq, 128 * hp), lambda b_, h, i: (b_ * nq + i, h))))


def _flash_bwd(qc, kc, v, o, do, lse, *, nseq, S, tq, ride=None):
    T = qc.shape[0]
    nq = S // tq

    def body(q_ref, k_ref, v_ref, o_ref, do_ref, lse_ref, dq_ref, dk_ref, dv_ref, dq_scr, delta_scr):
        j = pl.program_id(2)

        @pl.when(j == 0)
        def _():
            dq_scr[...] = jnp.zeros_like(dq_scr)
            for hh in range(FLASH_HP):
                od = o_ref[:, MV * hh:MV * hh + MV].astype(F32) * do_ref[:, MV * hh:MV * hh + MV].astype(F32)
                delta_scr[:, 128 * hh:128 * hh + 128] = jnp.broadcast_to(jnp.sum(od, axis=-1, keepdims=True), (S, 128))

        causal = _iota((tq, tq), 0) >= _iota((tq, tq), 1)

        def step(i, carry, masked):
            rows = pl.ds(pl.multiple_of(i * tq, tq), tq)
            hs = range(FLASH_HP)
            qs = [slice(256 * hh, 256 * hh + 256) for hh in hs]
            vs = [slice(MV * hh, MV * hh + MV) for hh in hs]
            ls = [slice(128 * hh, 128 * hh + 1) for hh in hs]
            s = [_dot_nt(q_ref[rows, qs[hh]], k_ref[:, qs[hh]]) for hh in hs]
            dp = [_dot_nt(do_ref[rows, vs[hh]], v_ref[:, vs[hh]]) for hh in hs]
            pb, ds = [], []
            for hh in hs:
                p = jnp.exp2(s[hh] - lse_ref[rows, ls[hh]])
                if masked:
                    p = jnp.where(causal, p, 0.0)
                pb.append(p.astype(BF))
                ds.append((p * (dp[hh] - delta_scr[rows, ls[hh]])).astype(BF))
            dv = [carry[hh][1] + _dot_tn(pb[hh], do_ref[rows, vs[hh]]) for hh in hs]
            dk = [carry[hh][0] + _dot_tn(ds[hh], q_ref[rows, qs[hh]]) for hh in hs]
            for hh in hs:
                dq_scr[rows, qs[hh]] += _dot(ds[hh], k_ref[:, qs[hh]])
            return tuple((dk[hh], dv[hh]) for hh in hs)

        init = ((jnp.zeros((tq, 256), F32), jnp.zeros((tq, MV), F32)),) * FLASH_HP
        carry = step(j, init, True)
        carry = lax.fori_loop(j + 1, nq, lambda i, c: step(i, c, False), carry)
        for hh, (dk, dv) in enumerate(carry):
            dk_ref[:, 256 * hh:256 * hh + 256] = dk * (1.0 / LOG2E)
            dv_ref[:, MV * hh:MV * hh + MV] = dv

        @pl.when(j == nq - 1)
        def _():
            dq_ref[...] = dq_scr[...] * QK_SCALE

    hp = FLASH_HP
    seq = lambda w: pl.BlockSpec((S, w * hp), lambda b_, h, j: (b_, h))
    blk = lambda w: pl.BlockSpec((tq, w * hp), lambda b_, h, j: (b_ * nq + j, h))
    return _call(
        body, name="flash_bwd", ride=ride, sem=("parallel", "parallel", "arbitrary"), args=(qc, kc, v, o, do, lse),
        out_shape=(jax.ShapeDtypeStruct((T, MH * 256), F32), jax.ShapeDtypeStruct((T, MH * 256), F32),
                   jax.ShapeDtypeStruct((T, MH * MV), F32)),
        grid=(nseq, MH // hp, nq),
        in_specs=[seq(256), blk(256), blk(MV), seq(MV), seq(MV), seq(128)],
        out_specs=(seq(256), blk(256), blk(MV)),
        scratch_shapes=[pltpu.VMEM((S, 256 * hp), F32), pltpu.VMEM((S, 128 * hp), F32)])


def _ln_fwd(pre, g, b):
    mu = jnp.mean(pre, axis=-1, keepdims=True)
    xc = pre - mu
    rstd = lax.rsqrt(jnp.mean(xc * xc, axis=-1, keepdims=True) + LN_EPS)
    xhat = xc * rstd
    return xhat * g + b, xhat, rstd


def _ln_bwd(dy, xhat, rstd, g):
    dxh = dy * g
    dx = rstd * (dxh - jnp.mean(dxh, axis=-1, keepdims=True) - xhat * jnp.mean(dxh * xhat, axis=-1, keepdims=True))
    return dx, jnp.sum(dy * xhat, axis=0, keepdims=True), jnp.sum(dy, axis=0, keepdims=True)


def _post_attn_fwd(zg, attn, pt, x, wgo, wmo, wout, g1, b1, *, tm, ride=None):
    T = x.shape[0]

    def body(zg_ref, at_ref, pt_ref, x_ref, wgo_ref, wmo_ref, wout_ref, g_ref, b_ref,
             yg_ref, ym_ref, mix_ref, pre_ref, hb_ref):
        yg = _dot(zg_ref[...], wgo_ref[...])
        ym = _dot(at_ref[...], wmo_ref[...])
        mix = (_sigmoid(pt_ref[:, 0:D].astype(F32)) * yg + _sigmoid(pt_ref[:, D:2 * D].astype(F32)) * ym).astype(BF)
        pre = ALPHA * x_ref[...] + _dot(mix, wout_ref[...])
        h, _, _ = _ln_fwd(pre, g_ref[...], b_ref[...])
        yg_ref[...] = yg.astype(BF)
        ym_ref[...] = ym.astype(BF)
        mix_ref[...] = mix
        pre_ref[...] = pre
        hb_ref[...] = h.astype(BF)

    full = lambda shp: pl.BlockSpec(shp, lambda i: (0,) * len(shp))
    row = lambda w: pl.BlockSpec((tm, w), lambda i: (i, 0))
    sd = lambda dt: jax.ShapeDtypeStruct((T, D), dt)
    return _call(
        body, name="post_attn_fwd", ride=ride, sem=("parallel",), args=(zg, attn, pt, x, wgo, wmo, wout, g1, b1),
        out_shape=(sd(BF), sd(BF), sd(BF), sd(F32), sd(BF)),
        grid=(T // tm,),
        in_specs=[row(D), row(D), row(PT_W), row(D), full((D, D)), full((D, D)), full((D, D)),
                  full((1, D)), full((1, D))],
        out_specs=(row(D),) * 5)


def _post_attn_bwd(dh, pre, pt, yg, ym, wgo, wmo, wout, g1, *, tm):
    T = dh.shape[0]

    def body(dh_ref, pre_ref, pt_ref, yg_ref, ym_ref, wgo_ref, wmo_ref, wout_ref, g_ref,
             dx_ref, dpreb_ref, dpt_ref, dygb_ref, dymb_ref, dzg_ref, dat_ref, dg_ref, db_ref):
        @pl.when(pl.program_id(0) == 0)
        def _():
            dg_ref[...] = jnp.zeros_like(dg_ref)
            db_ref[...] = jnp.zeros_like(db_ref)

        pre = pre_ref[...]
        mu = jnp.mean(pre, axis=-1, keepdims=True)
        xc = pre - mu
        rstd = lax.rsqrt(jnp.mean(xc * xc, axis=-1, keepdims=True) + LN_EPS)
        dpre, dg, db = _ln_bwd(dh_ref[...], xc * rstd, rstd, g_ref[...])
        dg_ref[...] += dg
        db_ref[...] += db
        dx_ref[...] = ALPHA * dpre
        dpreb = dpre.astype(BF)
        dpreb_ref[...] = dpreb
        dmix = _dot_nt(dpreb, wout_ref[...])
        sa = _sigmoid(pt_ref[:, 0:D].astype(F32))
        sb = _sigmoid(pt_ref[:, D:2 * D].astype(F32))
        dpt_ref[:, 0:D] = (dmix * yg_ref[...].astype(F32) * (sa * (1.0 - sa))).astype(BF)
        dpt_ref[:, D:2 * D] = (dmix * ym_ref[...].astype(F32) * (sb * (1.0 - sb))).astype(BF)
        dyg = (dmix * sa).astype(BF)
        dym = (dmix * sb).astype(BF)
        dygb_ref[...] = dyg
        dymb_ref[...] = dym
        dzg_ref[...] = _dot_nt(dyg, wgo_ref[...]).astype(BF)
        dat_ref[...] = _dot_nt(dym, wmo_ref[...]).astype(BF)

    full = lambda shp: pl.BlockSpec(shp, lambda i: (0,) * len(shp))
    row = lambda w: pl.BlockSpec((tm, w), lambda i: (i, 0))
    sd = lambda w, dt: jax.ShapeDtypeStruct((T, w), dt)
    return pl.pallas_call(
        body, name="post_attn_bwd",
        out_shape=(sd(D, F32), sd(D, BF), sd(PT_W, BF), sd(D, BF), sd(D, BF), sd(D, BF), sd(D, BF),
                   jax.ShapeDtypeStruct((1, D), F32), jax.ShapeDtypeStruct((1, D), F32)),
        grid=(T // tm,),
        in_specs=[row(D), row(D), row(PT_W), row(D), row(D), full((D, D)), full((D, D)), full((D, D)),
                  full((1, D))],
        out_specs=(row(D), row(D), row(PT_W), row(D), row(D), row(D), row(D), full((1, D)), full((1, D))),
        compiler_params=_params(("arbitrary",)),
    )(dh, pre, pt, yg, ym, wgo, wmo, wout, g1)


def _shift_down(u, prev, k):
    r = pltpu.roll(u, k, 0)
    p = pltpu.roll(prev, k, 0)
    head = jnp.where(_iota(p.shape, 0) < k, p, r[0:8, :])
    return jnp.concatenate([head, r[8:, :]], axis=0)


def _conv3(u, prev, w_ref, b_ref):
    return (w_ref[0:1, :] * _shift_down(u, prev, 2) + w_ref[1:2, :] * _shift_down(u, prev, 1)
            + w_ref[2:3, :] * u + b_ref[...])


def _ffn_up_fwd(hb, wug, wuv, cw, cb, *, S, tm, tn):
    T = hb.shape[0]
    nj, nbs = DFF // tn, S // tm

    def body(h_ref, wg_ref, wv_ref, cwg_ref, cwv_ref, cbg_ref, cbv_ref,
             ug_ref, uv_ref, ucg_ref, ucv_ref, f_ref, pg_scr, pv_scr):
        @pl.when(pl.program_id(1) % nbs == 0)
        def _():
            pg_scr[...] = jnp.zeros_like(pg_scr)
            pv_scr[...] = jnp.zeros_like(pv_scr)

        h = h_ref[...]
        ug = _dot(h, wg_ref[...])
        uv = _dot(h, wv_ref[...])
        ucg = _conv3(ug, pg_scr[...], cwg_ref, cbg_ref)
        ucv = _conv3(uv, pv_scr[...], cwv_ref, cbv_ref)
        pg_scr[...] = ug[tm - 8:, :]
        pv_scr[...] = uv[tm - 8:, :]
        ug_ref[...] = ug.astype(BF)
        uv_ref[...] = uv.astype(BF)
        ucg_ref[...] = ucg
        ucv_ref[...] = ucv
        f_ref[...] = (ucg * _sigmoid(ucg) * ucv).astype(BF)

    tile = pl.BlockSpec((tm, tn), lambda j, i: (i, j))
    return pl.pallas_call(
        body, name="ffn_up_fwd",
        out_shape=(jax.ShapeDtypeStruct((T, DFF), BF), jax.ShapeDtypeStruct((T, DFF), BF),
                   jax.ShapeDtypeStruct((T, DFF), F32), jax.ShapeDtypeStruct((T, DFF), F32),
                   jax.ShapeDtypeStruct((T, DFF), BF)),
        grid=(nj, T // tm),
        in_specs=[pl.BlockSpec((tm, D), lambda j, i: (i, 0)),
                  pl.BlockSpec((D, tn), lambda j, i: (0, j)), pl.BlockSpec((D, tn), lambda j, i: (0, j)),
                  pl.BlockSpec((3, tn), lambda j, i: (0, j)), pl.BlockSpec((3, tn), lambda j, i: (0, j + nj)),
                  pl.BlockSpec((1, tn), lambda j, i: (0, j)), pl.BlockSpec((1, tn), lambda j, i: (0, j + nj))],
        out_specs=(tile, tile, tile, tile, tile),
        scratch_shapes=[pltpu.VMEM((8, tn), F32), pltpu.VMEM((8, tn), F32)],
        compiler_params=_params(("parallel", "arbitrary")),
    )(hb, wug, wuv, cw, cw, cb, cb)


def _ffn_bwd(dpreb, wd, ug, uv, ucg, ucv, cw, *, S, tm, tn):
    T = dpreb.shape[0]
    nj, nb, nbs = DFF // tn, T // tm, S // tm
    r_, c_ = lax.broadcasted_iota(jnp.int32, (tm, tm), 0), lax.broadcasted_iota(jnp.int32, (tm, tm), 1)
    s1, s2 = (c_ == r_ + 1).astype(BF), (c_ == r_ + 2).astype(BF)

    def body(dp_ref, wd_ref, ug_ref, uv_ref, ucg_ref, ucv_ref, cwg_ref, cwv_ref, s1_ref, s2_ref,
             dug_ref, duv_ref, dcg_ref, dcv_ref, ng_scr, nv_scr):
        ii = pl.program_id(1)
        i = nb - 1 - ii
        tail_row = _iota((8, tn), 0)

        @pl.when(ii == 0)
        def _():
            dcg_ref[...] = jnp.zeros_like(dcg_ref)
            dcv_ref[...] = jnp.zeros_like(dcv_ref)

        @pl.when(i % nbs == nbs - 1)
        def _():
            ng_scr[...] = jnp.zeros_like(ng_scr)
            nv_scr[...] = jnp.zeros_like(nv_scr)

        df = _dot_nt(dp_ref[...], wd_ref[...])
        ucg = ucg_ref[...]
        sg = _sigmoid(ucg)
        ducg = df * ucv_ref[...] * (sg * (1.0 + ucg * (1.0 - sg)))
        ducv = df * (ucg * sg)

        def finish(duc, u_ref, w, nxt_scr, du_ref, dc_ref):
            nxt = nxt_scr[...]
            db = duc.astype(BF)

            def shifted(s_ref, k):
                r = _dot(s_ref[...], db)
                tail = jnp.where(tail_row >= 8 - k, pltpu.roll(nxt, 8 - k, 0), r[tm - 8:, :])
                return jnp.concatenate([r[:tm - 8, :], tail], axis=0)

            up1 = shifted(s1_ref, 1)
            up2 = shifted(s2_ref, 2)
            du_ref[...] = (w[2:3, :] * duc + w[1:2, :] * up1 + w[0:1, :] * up2).astype(BF)
            nxt_scr[...] = duc[0:8, :]
            u = u_ref[...].astype(F32)
            for row, z in enumerate((u * up2, u * up1, u * duc, duc)):
                dc_ref[row:row + 1, :] += jnp.sum(z, axis=0, keepdims=True)

        finish(ducg, ug_ref, cwg_ref, ng_scr, dug_ref, dcg_ref)
        finish(ducv, uv_ref, cwv_ref, nv_scr, duv_ref, dcv_ref)

    tile = pl.BlockSpec((tm, tn), lambda j, ii: (nb - 1 - ii, j))
    acc = pl.BlockSpec((8, tn), lambda j, ii: (0, j))
    return pl.pallas_call(
        body, name="ffn_bwd",
        out_shape=(jax.ShapeDtypeStruct((T, DFF), BF), jax.ShapeDtypeStruct((T, DFF), BF),
                   jax.ShapeDtypeStruct((8, DFF), F32), jax.ShapeDtypeStruct((8, DFF), F32)),
        grid=(nj, nb),
        in_specs=[pl.BlockSpec((tm, D), lambda j, ii: (nb - 1 - ii, 0)),
                  pl.BlockSpec((tn, D), lambda j, ii: (j, 0)),
                  tile, tile, tile, tile,
                  pl.BlockSpec((3, tn), lambda j, ii: (0, j)), pl.BlockSpec((3, tn), lambda j, ii: (0, j + nj)),
                  pl.BlockSpec((tm, tm), lambda j, ii: (0, 0)), pl.BlockSpec((tm, tm), lambda j, ii: (0, 0))],
        out_specs=(tile, tile, acc, acc),
        scratch_shapes=[pltpu.VMEM((8, tn), F32), pltpu.VMEM((8, tn), F32)],
        compiler_params=_params(("parallel", "arbitrary")),
    )(dpreb, wd, ug, uv, ucg, ucv, cw, cw, s1, s2)


def _down_ln2_loss(f_in, wd, pre1, target, g1, b1, g2, b2, *, tm):
    T = pre1.shape[0]

    def body(f_ref, wd_ref, p1_ref, t_ref, g1_ref, b1_ref, g_ref, b_ref, dpb_ref, dh_ref, loss_ref, dg_ref, db_ref):
        @pl.when(pl.program_id(0) == 0)
        def _():
            loss_ref[...] = jnp.zeros_like(loss_ref)
            dg_ref[...] = jnp.zeros_like(dg_ref)
            db_ref[...] = jnp.zeros_like(db_ref)

        halves = [pl.ds(s * (tm // 2), tm // 2) for s in range(2)]
        f = [_dot(f_ref[hs, :], wd_ref[...]) for hs in halves]
        for hs, fh in zip(halves, f):
            h, _, _ = _ln_fwd(p1_ref[hs, :], g1_ref[...], b1_ref[...])
            pre = ALPHA * h + fh
            out, xhat, rstd = _ln_fwd(pre, g_ref[...], b_ref[...])
            diff = out - t_ref[hs, :]
            loss_ref[...] += 0.5 * jnp.sum(jnp.mean(diff * diff, axis=-1, keepdims=True))
            dpre, dg, db = _ln_bwd(diff * (1.0 / D), xhat, rstd, g_ref[...])
            dg_ref[...] += dg
            db_ref[...] += db
            dpb_ref[hs, :] = dpre.astype(BF)
            dh_ref[hs, :] = ALPHA * dpre

    full = lambda shp: pl.BlockSpec(shp, lambda i: (0,) * len(shp))
    row = lambda w: pl.BlockSpec((tm, w), lambda i: (i, 0))
    return pl.pallas_call(
        body, name="down_ln2_loss",
        out_shape=(jax.ShapeDtypeStruct((T, D), BF), jax.ShapeDtypeStruct((T, D), F32),
                   jax.ShapeDtypeStruct((8, 128), F32), jax.ShapeDtypeStruct((1, D), F32),
                   jax.ShapeDtypeStruct((1, D), F32)),
        grid=(T // tm,),
        in_specs=[row(DFF), full((DFF, D)), row(D), row(D), full((1, D)), full((1, D)), full((1, D)), full((1, D))],
        out_specs=(row(D), row(D), full((8, 128)), full((1, D)), full((1, D))),
        compiler_params=_params(("arbitrary",)),
    )(f_in, wd, pre1, target, g1, b1, g2, b2)


def _adamw(parts, w, m, v, *, name):
    n, R, C = parts.shape
    tr, tc = R, C
    for cand in range(min(R, 256), 15, -1):
        if R % cand == 0 and cand % 16 == 0:
            tr = cand
            break
    if tr == R and R * C > 65536 and C % 256 == 0:
        tc = 256
    c1 = 1.0 - ADAM_B1 ** ADAM_STEP
    c2 = 1.0 - ADAM_B2 ** ADAM_STEP

    def body(p_ref, w_ref, m_ref, v_ref, g_ref, d_ref, nm_ref, nv_ref):
        g = p_ref[0].astype(F32)
        for s in range(1, n):
            g = g + p_ref[s].astype(F32)
        nm = ADAM_B1 * m_ref[...] + (1.0 - ADAM_B1) * g
        nv = ADAM_B2 * v_ref[...] + (1.0 - ADAM_B2) * (g * g)
        g_ref[...] = g
        nm_ref[...] = nm
        nv_ref[...] = nv
        d_ref[...] = -ADAM_LR * ((nm / c1) / (jnp.sqrt(nv / c2) + ADAM_EPS) + ADAM_WD * w_ref[...])

    blk = pl.BlockSpec((tr, tc), lambda i, j: (i, j))
    sd = jax.ShapeDtypeStruct((R, C), F32)
    return pl.pallas_call(
        body, name=name,
        out_shape=(sd, sd, sd, sd),
        grid=(R // tr, C // tc),
        in_specs=[pl.BlockSpec((n, tr, tc), lambda i, j: (0, i, j)), blk, blk, blk],
        out_specs=(blk, blk, blk, blk),
        compiler_params=_params(("parallel", "parallel")),
    )(parts, w, m, v)


class _Exchange:
    def __init__(self, items):
        self.items = [(src if sc else [(src, 0)], sc) for src, sc in items]
        self.arrays = [arr for srcs, _ in self.items for arr, _ in srcs]
        self.n = len(self.items)
        self.n_in = len(self.arrays)

    def out_shape(self):
        return tuple(jax.ShapeDtypeStruct((NDEV,) + (srcs[0][0].shape[1:] if sc else srcs[0][0].shape),
                                          srcs[0][0].dtype) for srcs, sc in self.items)

    def scratch(self):
        return [pltpu.SemaphoreType.DMA((self.n, NDEV - 1)), pltpu.SemaphoreType.DMA((self.n, NDEV - 1)),
                pltpu.SemaphoreType.DMA((self.n,))]

    def _emit(self, ins, outs, sems, phase):
        send_sems, recv_sems, loc_sems = sems
        x, y, c = lax.axis_index("x"), lax.axis_index("y"), lax.axis_index("c")
        me = 4 * x + 2 * y + c
        flip = lambda p, d: 1 - p if d else p

        def inside(p, lo, n):
            return None if (lo, n) == (0, NDEV) else jnp.logical_and(p >= lo, p < lo + n)

        def when(cond, fn):
            if cond is None:
                fn()
            else:
                pl.when(cond)(fn)

        pos = 0
        for a, (srcs, sc) in enumerate(self.items):
            refs = ins[pos:pos + len(srcs)]
            pos += len(srcs)
            ranges = [(lo, arr.shape[0]) if sc else (0, NDEV) for arr, lo in srcs]
            mine = [inside(me, lo, n) for lo, n in ranges]
            i_receive = None if None in mine else functools.reduce(jnp.logical_or, mine)
            for ref, (lo, n), cond in zip(refs, ranges, mine):
                def local(ref=ref, lo=lo):
                    cp = pltpu.make_async_copy(ref.at[me - lo] if sc else ref, outs[a].at[me], loc_sems.at[a])
                    cp.start() if phase == 0 else cp.wait()
                if phase != 1:
                    when(cond, local)
            for k in range(1, NDEV):
                px, py, pc = flip(x, k & 4), flip(y, k & 2), flip(c, k & 1)
                peer = 4 * px + 2 * py + pc
                mk = functools.partial(pltpu.make_async_remote_copy,
                                       send_sem=send_sems.at[a, k - 1], recv_sem=recv_sems.at[a, k - 1],
                                       device_id=(px, py, pc), device_id_type=MESH_ID)
                if phase == 1:
                    def arrival(mk=mk, peer=peer):
                        mk(src_ref=refs[0].at[0] if sc else refs[0], dst_ref=outs[a].at[peer]).wait_recv()
                    when(i_receive, arrival)
                    continue
                for ref, (lo, n) in zip(refs, ranges):
                    def send(mk=mk, ref=ref, lo=lo, peer=peer):
                        cp = mk(src_ref=ref.at[peer - lo] if sc else ref, dst_ref=outs[a].at[me])
                        cp.start() if phase == 0 else cp.wait_send()
                    when(inside(peer, lo, n), send)

    def start(self, ins, outs, sems):
        self._emit(ins, outs, sems, 0)

    def wait(self, ins, outs, sems):
        self._emit(ins, outs, sems, 1)
        self._emit(ins, outs, sems, 2)


def _call(body, *, name, grid, in_specs, out_specs, out_shape, args, scratch_shapes=(), sem=None, ride=None):
    if ride is None:
        return pl.pallas_call(body, name=name, grid=grid, in_specs=list(in_specs), out_specs=tuple(out_specs),
                              out_shape=tuple(out_shape), scratch_shapes=list(scratch_shapes),
                              compiler_params=_params(sem))(*args)
    n_in, n_out, n_scr, ne, ne_in = len(args), len(out_shape), len(scratch_shapes), ride.n, ride.n_in

    def ride_body(*refs):
        ins, ex_in = refs[:n_in], refs[n_in:n_in + ne_in]
        o0 = n_in + ne_in
        outs, ex_out = refs[o0:o0 + n_out], refs[o0 + n_out:o0 + n_out + ne]
        scr = refs[o0 + n_out + ne:o0 + n_out + ne + n_scr]
        sems = refs[o0 + n_out + ne + n_scr:]
        first = functools.reduce(jnp.logical_and, [pl.program_id(d) == 0 for d in range(len(grid))])
        last = functools.reduce(jnp.logical_and, [pl.program_id(d) == grid[d] - 1 for d in range(len(grid))])

        @pl.when(first)
        def _():
            ride.start(ex_in, ex_out, sems)

        body(*ins, *outs, *scr)

        @pl.when(last)
        def _():
            ride.wait(ex_in, ex_out, sems)

    anyspec = pl.BlockSpec(memory_space=pl.ANY)
    res = pl.pallas_call(
        ride_body, name=name, grid=grid,
        in_specs=list(in_specs) + [anyspec] * ne_in,
        out_specs=tuple(out_specs) + (anyspec,) * ne,
        out_shape=tuple(out_shape) + ride.out_shape(),
        scratch_shapes=list(scratch_shapes) + ride.scratch(),
        compiler_params=_params(("arbitrary",) * len(grid)),
    )(*args, *ride.arrays)
    return tuple(res[:n_out]), tuple(res[n_out:])


def _gather_two_level(arrays, *, name):
    n = len(arrays)

    def body(*refs):
        ins, outs = refs[:n], refs[n:2 * n]
        send_sems, recv_sems, loc_sems = refs[2 * n:]
        x, y, c = lax.axis_index("x"), lax.axis_index("y"), lax.axis_index("c")
        sibling = (x, y, 1 - c)
        chips = [(1 - x, y), (x, 1 - y), (1 - x, 1 - y)]
        idx = lambda px, py, pc: 4 * px + 2 * py + pc
        me = idx(x, y, c)

        def copy(a, k, block, to, src=None):
            return pltpu.make_async_remote_copy(
                src_ref=outs[a].at[block] if src is None else src, dst_ref=outs[a].at[block],
                send_sem=send_sems.at[a, k], recv_sem=recv_sems.at[a, k], device_id=to, device_id_type=MESH_ID)

        local = [pltpu.make_async_copy(ins[a], outs[a].at[me], loc_sems.at[a]) for a in range(n)]
        sent = []
        for a in range(n):
            sent.append(copy(a, 0, me, sibling, src=ins[a]))
            sent += [copy(a, 1 + j, me, (*chip, c), src=ins[a]) for j, chip in enumerate(chips)]
        for cp in local + sent:
            cp.start()
        for j, chip in enumerate(chips):
            for a in range(n):
                copy(a, 1 + j, idx(*chip, c), sibling).wait_recv()
                passed = copy(a, 4 + j, idx(*chip, c), sibling)
                passed.start()
                sent.append(passed)
        for a in range(n):
            copy(a, 0, idx(x, y, 1 - c), sibling).wait_recv()
            for j, chip in enumerate(chips):
                copy(a, 4 + j, idx(*chip, 1 - c), sibling).wait_recv()
        for cp in sent:
            cp.wait_send()
        for cp in local:
            cp.wait()

    anyspec = pl.BlockSpec(memory_space=pl.ANY)
    return pl.pallas_call(
        body, name=name,
        out_shape=tuple(jax.ShapeDtypeStruct((NDEV,) + a.shape, a.dtype) for a in arrays),
        in_specs=[anyspec] * n, out_specs=(anyspec,) * n,
        scratch_shapes=[pltpu.SemaphoreType.DMA((n, NDEV - 1)), pltpu.SemaphoreType.DMA((n, NDEV - 1)),
                        pltpu.SemaphoreType.DMA((n,))],
    )(*arrays)


def _tri_consts():
    r = lax.broadcasted_iota(jnp.int32, (GC, GC), 0)
    c = lax.broadcasted_iota(jnp.int32, (GC, GC), 1)
    return (r >= c).astype(BF), (r <= c).astype(BF)


def _local_step(x, positions, target, w, hooks=None):
    g = {}

    def run(host, fn, *a, **kw):
        h = None if hooks is None else hooks.get(host)
        if h is None:
            return fn(*a, **kw)
        out, received = fn(*a, ride=_Exchange(h[0](w, g)), **kw)
        h[1](received, w, g)
        return out

    nseq, S, _ = x.shape
    T = nseq * S
    tm = min(TOKEN_TM, S)
    tq = min(FLASH_TQ, S)
    x2 = x.reshape(T, D)
    pos = positions.reshape(T, 1)
    half = ROPE // 2
    inv = THETA ** (-jnp.arange(half, dtype=F32) / half)
    invf = jnp.concatenate([inv, inv, jnp.zeros((64,), F32)]).reshape(1, 128)
    ltri, utri = _tri_consts()

    pt, xb = _matmul(x2, w["w_tt"], "nt", name="proj_t", out_dtype=BF, tm=1024, tn=1024, tk=1024, emit_a=True)
    pg = run("proj_g", _matmul, xb, w["w_gt"], "nt", name="proj_g", tm=1024, tn=640, tk=1024)
    pm = _matmul(xb, w["w_mt"], "nt", name="proj_m", tm=1024, tn=768, tk=1024)
    o, zg, states = run("gla_fwd", _gla_fwd, pg, w["wg"], w["bg"], w["gn"], ltri, nseq=nseq, S=S, tm=tm)
    qc, kc, v = _mla_prep_fwd(pm, pos, invf, w["gq"], w["gkv"], w["wuq"], w["wukv"], tm=min(2 * tm, S))
    attn, lse = run("flash_fwd", _flash_fwd, qc, kc, v, nseq=nseq, S=S, tq=tq)
    yg, ym, mix, pre1, h1b = run("post_attn_fwd", _post_attn_fwd, zg, attn, pt, x2, w["wgo"], w["wmo"], w["wout"],
                                 w["g1"], w["b1"], tm=tm)
    ug, uv, ucg, ucv, f_in = _ffn_up_fwd(h1b, w["wug"], w["wuv"], w["cw"], w["cb"], S=S, tm=tm, tn=FFN_TN)
    dpre2b, dh1, loss8, dg2, db2 = _down_ln2_loss(f_in, w["wd"], pre1, target.reshape(T, D), w["g1"], w["b1"],
                                                  w["g2"], w["b2"], tm=min(2 * tm, S))

    dug, duv, dcg, dcv = _ffn_bwd(dpre2b, w["wd"], ug, uv, ucg, ucv, w["cw"], S=S, tm=tm, tn=FFN_TN)
    g["g2"], g["b2"], g["loss"] = dg2, db2, loss8[0:1, 0:1]
    g["cw"] = jnp.concatenate([dcg[0:3], dcv[0:3]], axis=1)
    g["cb"] = jnp.concatenate([dcg[3:4], dcv[3:4]], axis=1)
    g["wd"] = _matmul(f_in, dpre2b, "tn", name="dw_down", out_dtype=BF, tm=1408, tn=1024, tk=1024)
    g["wugt"] = _matmul(dug, h1b, "tn", name="dw_up_g", out_dtype=BF, tm=1408, tn=1024, tk=1024)
    g["wuvt"] = _matmul(duv, h1b, "tn", name="dw_up_v", out_dtype=BF, tm=1408, tn=1024, tk=1024)
    dh1 = _matmul(dug, w["wugt"], "nn", name="dh1_g", c_in=dh1, tm=1024, tn=1024, tk=1408)
    dh1 = _matmul(duv, w["wuvt"], "nn", name="dh1_v", c_in=dh1, tm=1024, tn=1024, tk=1408)
    dx, dpre1b, dpt, dygb, dymb, dzg, dattn, dg1, db1 = _post_attn_bwd(
        dh1, pre1, pt, yg, ym, w["wgo"], w["wmo"], w["wout"], w["g1"], tm=tm)
    g["g1"], g["b1"] = dg1, db1
    g["wout"] = _matmul(mix, dpre1b, "tn", name="dw_out", out_dtype=BF, tm=1024, tn=1024, tk=1024)
    g["wgo"] = _matmul(zg, dygb, "tn", name="dw_gla_o", out_dtype=BF, tm=1024, tn=1024, tk=1024)
    g["wmo"] = _matmul(attn, dymb, "tn", name="dw_mla_o", out_dtype=BF, tm=1024, tn=1024, tk=1024)
    dqc, dkc, dv = run("flash_bwd", _flash_bwd, qc, kc, v, attn, dattn, lse, nseq=nseq, S=S, tq=tq)
    dpm, g["wuq"], g["wukv"], g["gq"], g["gkv"] = _mla_prep_bwd(
        pm, pos, invf, w["gq"], w["gkv"], w["wuq"], w["wukv"], dqc, dkc, dv, tm=min(2 * tm, S))
    g["w_mt"] = _matmul(dpm, xb, "tn", name="dw_in_m", out_dtype=BF, tm=768, tn=1024, tk=1024)
    g["w_tt"] = _matmul(dpt, xb, "tn", name="dw_in_t", out_dtype=BF, tm=1024, tn=1024, tk=1024)
    dpg, g["wg"], g["bg"], g["gn"] = run("gla_bwd", _gla_bwd, pg, w["wg"], w["bg"], w["gn"], ltri, utri, o, states,
                                         dzg, nseq=nseq, S=S, tm=min(2 * tm, S))
    g["w_gt"] = _matmul(dpg, xb, "tn", name="dw_in_g", out_dtype=BF, tm=640, tn=1024, tk=1024)
    dx = run("dx", _matmul_sum, dx, [(dpg, w["w_gt"], 640), (dpm, w["w_mt"], 768)], name="dx_gm")
    dx = _matmul_sum(dx, [(dpt, w["w_tt"], 1024)], name="dx_t")
    return loss8[0, 0], dx.reshape(nseq, S, D), g


_IN_SPLITS = (512, 512, 1024, 16, 1024, 384, 256, 64, 1024, 1024)


def _w_in_to_groups(wt):
    offs = [0]
    for s in _IN_SPLITS:
        offs.append(offs[-1] + s)
    q, k, v, r, og, cq, ckv, kr, ga, gb = [wt[offs[i]:offs[i + 1]] for i in range(10)]
    z = lambda n: jnp.zeros((n, wt.shape[1]), wt.dtype)
    return (jnp.concatenate([q, k, v, og, r, z(112)], axis=0),
            jnp.concatenate([cq, kr, z(64), ckv], axis=0),
            jnp.concatenate([ga, gb], axis=0))


W_IN_BLOCK = sum(_IN_SPLITS) // NDEV
_KV_LATENT_ROW = sum(_IN_SPLITS[:6])
_W_IN_LO = 5
_W_IN_SPLIT = _W_IN_LO * W_IN_BLOCK - _KV_LATENT_ROW


def _w_in_rows_lo(g_g, g_m):
    q, k, v, og, r = g_g[0:512], g_g[512:1024], g_g[1024:2048], g_g[2048:3072], g_g[3072:3088]
    return jnp.concatenate([q, k, v, r, og, g_m[0:384], g_m[512:768]], axis=0)[:_W_IN_LO * W_IN_BLOCK]


def _w_in_rows_hi(g_m, g_t):
    return jnp.concatenate([g_m[512:768], g_m[384:448], g_t], axis=0)[_W_IN_SPLIT:]


def _uq_to_kernel(wuq):
    w3 = wuq.reshape(MQR, MH, NOPE + ROPE)
    rope = jnp.concatenate([w3[:, :, NOPE:], jnp.zeros((MQR, MH, 64), wuq.dtype)], axis=2)
    return jnp.concatenate([w3[:, :, :NOPE].reshape(MQR, MH * 128), rope.reshape(MQR, MH * 128)], axis=1)


def _uq_from_kernel(g):
    nope = g[:, :1024].reshape(MQR, MH, 128)
    rope = g[:, 1024:].reshape(MQR, MH, 128)[:, :, :ROPE]
    return jnp.concatenate([nope, rope], axis=2)


def _ukv_to_kernel(wukv):
    w3 = wukv.reshape(MKR, MH, NOPE + MV)
    return jnp.concatenate([w3[:, :, :NOPE].reshape(MKR, MH * 128), w3[:, :, NOPE:].reshape(MKR, MH * 128)], axis=1)


def _ukv_from_kernel(g):
    return jnp.concatenate([g[:, :1024].reshape(MKR, MH, 128), g[:, 1024:].reshape(MKR, MH, 128)], axis=2)


def _cols_gathered(a):
    return a.transpose(1, 0, 2).reshape(a.shape[1], NDEV * a.shape[2])


def _cols_scattered(a):
    R = a.shape[0]
    return a.reshape(R, NDEV, a.shape[1] // NDEV).transpose(1, 0, 2)


_SMALL = (("gla_b_gate", 512), ("gla_norm_g", 256), ("mla_q_norm_g", 384), ("mla_kv_norm_g", 256),
          ("ln1_g", 1024), ("ln1_b", 1024), ("conv_b", 5632), ("ln2_g", 1024), ("ln2_b", 1024))
_SMALL_ROWS = 88
_SMALL_USED = sum(sz for _, sz in _SMALL)


def _pack_small(d):
    flat = jnp.concatenate([d[n].reshape(-1) for n, _ in _SMALL] + ([d['loss'].reshape(-1)] if 'loss' in d else []))
    return jnp.pad(flat, (0, _SMALL_ROWS * 128 - flat.shape[0])).reshape(_SMALL_ROWS, 128)


def _unpack_small(a):
    flat = a.reshape(-1)
    out, off = {}, 0
    for n, sz in _SMALL:
        out[n] = flat[off:off + sz].reshape(1, sz)
        off += sz
    return out


_NAMES = ['w_in', 'gla_w_gate_up', 'gla_b_gate', 'gla_norm_g', 'w_gla_o', 'mla_q_norm_g', 'mla_w_uq',
          'mla_kv_norm_g', 'mla_w_ukv', 'w_mla_o', 'w_out', 'ln1_g', 'ln1_b', 'w_up', 'conv_w', 'conv_b',
          'w_down', 'ln2_g', 'ln2_b']
_SHARDED = ['w_in', 'w_up', 'w_down', 'w_gla_o', 'w_mla_o', 'w_out', 'mla_w_uq', 'mla_w_ukv', 'gla_w_gate_up',
            'conv_w']


def kernel(x, positions, w_in, gla_w_gate_up, gla_b_gate, gla_norm_g, w_gla_o, mla_q_norm_g, mla_w_uq, mla_kv_norm_g, mla_w_ukv, w_mla_o, w_out, ln1_g, ln1_b, w_up, conv_w, conv_b, w_down, ln2_g, ln2_b, loss_target, m_w_in, m_gla_w_gate_up, m_gla_b_gate, m_gla_norm_g, m_w_gla_o, m_mla_q_norm_g, m_mla_w_uq, m_mla_kv_norm_g, m_mla_w_ukv, m_w_mla_o, m_w_out, m_ln1_g, m_ln1_b, m_w_up, m_conv_w, m_conv_b, m_w_down, m_ln2_g, m_ln2_b, v_w_in, v_gla_w_gate_up, v_gla_b_gate, v_gla_norm_g, v_w_gla_o, v_mla_q_norm_g, v_mla_w_uq, v_mla_kv_norm_g, v_mla_w_ukv, v_w_mla_o, v_w_out, v_ln1_g, v_ln1_b, v_w_up, v_conv_w, v_conv_b, v_w_down, v_ln2_g, v_ln2_b):
    W = dict(w_in=w_in, gla_w_gate_up=gla_w_gate_up, gla_b_gate=gla_b_gate, gla_norm_g=gla_norm_g, w_gla_o=w_gla_o, mla_q_norm_g=mla_q_norm_g, mla_w_uq=mla_w_uq, mla_kv_norm_g=mla_kv_norm_g, mla_w_ukv=mla_w_ukv, w_mla_o=w_mla_o, w_out=w_out, ln1_g=ln1_g, ln1_b=ln1_b, w_up=w_up, conv_w=conv_w, conv_b=conv_b, w_down=w_down, ln2_g=ln2_g, ln2_b=ln2_b)
    M = dict(w_in=m_w_in, gla_w_gate_up=m_gla_w_gate_up, gla_b_gate=m_gla_b_gate, gla_norm_g=m_gla_norm_g, w_gla_o=m_w_gla_o, mla_q_norm_g=m_mla_q_norm_g, mla_w_uq=m_mla_w_uq, mla_kv_norm_g=m_mla_kv_norm_g, mla_w_ukv=m_mla_w_ukv, w_mla_o=m_w_mla_o, w_out=m_w_out, ln1_g=m_ln1_g, ln1_b=m_ln1_b, w_up=m_w_up, conv_w=m_conv_w, conv_b=m_conv_b, w_down=m_w_down, ln2_g=m_ln2_g, ln2_b=m_ln2_b)
    V = dict(w_in=v_w_in, gla_w_gate_up=v_gla_w_gate_up, gla_b_gate=v_gla_b_gate, gla_norm_g=v_gla_norm_g, w_gla_o=v_w_gla_o, mla_q_norm_g=v_mla_q_norm_g, mla_w_uq=v_mla_w_uq, mla_kv_norm_g=v_mla_kv_norm_g, mla_w_ukv=v_mla_w_ukv, w_mla_o=v_w_mla_o, w_out=v_w_out, ln1_g=v_ln1_g, ln1_b=v_ln1_b, w_up=v_w_up, conv_w=v_conv_w, conv_b=v_conv_b, w_down=v_w_down, ln2_g=v_ln2_g, ln2_b=v_ln2_b)

    tshard = lambda d, n: d[n][0].T
    shard = lambda n: (W[n][0].astype(BF), False)
    (w_in_t,) = _gather_two_level([tshard(W, 'w_in').astype(BF)], name="gather_w0")
    w_gt, w_mt, w_tt = _w_in_to_groups(w_in_t.reshape(NDEV * W_IN_BLOCK, D))
    kw = dict(
        w_gt=w_gt, w_mt=w_mt, w_tt=w_tt, bg=W['gla_b_gate'],
        gn=W['gla_norm_g'], gq=W['mla_q_norm_g'], gkv=W['mla_kv_norm_g'],
        g1=W['ln1_g'], b1=W['ln1_b'], g2=W['ln2_g'], b2=W['ln2_b'], cb=W['conv_b'],
    )
    received = {}

    def got_mixers(ex, w, g):
        w.update(wuq=_uq_to_kernel(_cols_gathered(ex[0])), wukv=_ukv_to_kernel(_cols_gathered(ex[1])),
                 wg=jnp.pad(_cols_gathered(ex[2]), ((0, 128 - GR), (0, 0))))

    def got_out_proj(ex, w, g):
        w.update(wgo=ex[0].reshape(D, D), wmo=ex[1].reshape(D, D), wout=ex[2].reshape(D, D))

    def got_up(ex, w, g):
        w_upt = ex[0].reshape(2 * DFF, D)
        w.update(wugt=w_upt[:DFF], wuvt=w_upt[DFF:], wug=w_upt[:DFF].T, wuv=w_upt[DFF:].T)

    def got_down(ex, w, g):
        w.update(wd=ex[0].reshape(DFF, D), cw=_cols_gathered(ex[1]))

    slab = lambda a, lo=0: ([(a.astype(BF), lo)], True)
    rows = lambda a, n=NDEV: a.reshape(n, a.shape[0] // n, a.shape[1])

    def keep(names):
        return lambda ex, w, g: received.update(zip(names, ex))

    def small_grads(g):
        return _pack_small(dict(gla_b_gate=g['bg'], gla_norm_g=g['gn'], mla_q_norm_g=g['gq'], mla_kv_norm_g=g['gkv'],
                                ln1_g=g['g1'], ln1_b=g['b1'], conv_b=g['cb'], ln2_g=g['g2'], ln2_b=g['b2'],
                                loss=g['loss']))

    hooks = {
        "proj_g": (lambda w, g: [shard('mla_w_uq'), shard('mla_w_ukv'), shard('gla_w_gate_up')], got_mixers),
        "gla_fwd": (lambda w, g: [shard('w_gla_o'), shard('w_mla_o'), shard('w_out')], got_out_proj),
        "flash_fwd": (lambda w, g: [(tshard(W, 'w_up').astype(BF), False)], got_up),
        "post_attn_fwd": (lambda w, g: [shard('w_down'), (W['conv_w'][0], False)], got_down),
        "flash_bwd": (lambda w, g: [slab(rows(g['wd'])),
                                    ([(rows(g['wugt'], 4), 0), (rows(g['wuvt'], 4), 4)], True),
                                    slab(rows(g['wout'])), slab(rows(g['wgo'])), slab(rows(g['wmo']))],
                      keep(['w_down', 'w_up', 'w_out', 'w_gla_o', 'w_mla_o'])),
        "gla_bwd": (lambda w, g: [slab(_uq_from_kernel(g['wuq']).transpose(1, 0, 2)),
                                  slab(_ukv_from_kernel(g['wukv']).transpose(1, 0, 2)),
                                  slab(rows(_w_in_rows_hi(g['w_mt'], g['w_tt']), NDEV - _W_IN_LO), _W_IN_LO)],
                    keep(['mla_w_uq', 'mla_w_ukv', 'w_in_hi'])),
        "dx": (lambda w, g: [slab(rows(_w_in_rows_lo(g['w_gt'], g['w_mt']), _W_IN_LO)),
                             ([(_cols_scattered(g['wg'][:GR]), 0)], True), ([(_cols_scattered(g['cw']), 0)], True),
                             (small_grads(g), False)],
               keep(['w_in_lo', 'gla_w_gate_up', 'conv_w', 'small'])),
    }

    _, grad_x, _ = _local_step(x, positions, loss_target, kw, hooks)

    grads, deltas, new_m, new_v = {}, {}, {}, {}
    small_parts = received['small']
    loss = jnp.sum(small_parts.reshape(NDEV, -1)[:, _SMALL_USED])
    me = 4 * lax.axis_index("x") + 2 * lax.axis_index("y") + lax.axis_index("c")
    received['w_in'] = jnp.where(me >= _W_IN_LO, received['w_in_hi'], received['w_in_lo'])
    for n in _SHARDED:
        shp = W[n].shape
        if n in ('w_in', 'w_up'):
            out = _adamw(received[n], tshard(W, n), tshard(M, n), tshard(V, n), name="adamw_" + n)
            grads[n], deltas[n], new_m[n], new_v[n] = [t.T.reshape(shp) for t in out]
            continue
        out = _adamw(received[n], W[n][0], M[n][0], V[n][0], name="adamw_" + n)
        grads[n], deltas[n], new_m[n], new_v[n] = [t.reshape(shp) for t in out]
    out = _adamw(small_parts, _pack_small(W), _pack_small(M), _pack_small(V), name="adamw_small")
    for dst, packed in zip((grads, deltas, new_m, new_v), out):
        dst.update(_unpack_small(packed))

    return (loss, grad_x, *[grads[n] for n in _NAMES], *[deltas[n] for n in _NAMES],
            *[new_m[n] for n in _NAMES], *[new_v[n] for n in _NAMES])
```

```python
import functools

import jax
import jax.numpy as jnp
from jax import lax
from jax.experimental import pallas as pl
from jax.experimental.pallas import tpu as pltpu

F32 = jnp.float32
BF = jnp.bfloat16

D = 1024
GH, GDK, GDV, GR, GTAU, GC = 4, 128, 256, 16, 16.0, 64
MH, MQR, MKR, NOPE, ROPE, MV = 8, 384, 256, 128, 64, 128
THETA = 10000.0
DFF = 2816
ALPHA = 2.0 ** 0.25
LN_EPS = 1e-5
RMS_EPS = 1e-6
NDEV = 8
ADAM_LR, ADAM_B1, ADAM_B2, ADAM_EPS, ADAM_WD, ADAM_STEP = 0.001, 0.9, 0.999, 1e-08, 0.01, 10

PG_W = 3200
PM_W = 768
PT_W = 2048
NEG = -1e30
MESH_ID = pl.DeviceIdType.MESH
VMEM_MB = 1024 * 1024


V7X_VMEM_LIMIT_MB = 48
TOKEN_TM = 256
FLASH_TQ = 512
FFN_TN = 1408


def _params(sem):
    return pltpu.CompilerParams(dimension_semantics=sem, vmem_limit_bytes=V7X_VMEM_LIMIT_MB * VMEM_MB)


def _dot(a, b):
    return lax.dot_general(a, b, (((1,), (0,)), ((), ())), preferred_element_type=F32)


def _dot_nt(a, b):
    return lax.dot_general(a, b, (((1,), (1,)), ((), ())), preferred_element_type=F32)


def _dot_tn(a, b):
    return lax.dot_general(a, b, (((0,), (0,)), ((), ())), preferred_element_type=F32)


def _iota(shape, dim):
    return lax.broadcasted_iota(jnp.int32, shape, dim)


FLASH_HP = 2
FLASH_HP_FWD = 4
QK_SCALE = (NOPE + ROPE) ** -0.5
LOG2E = 1.4426950408889634
QK_SCALE_LOG2 = QK_SCALE * LOG2E


def _sigmoid(x):
    return 0.5 * jnp.tanh(0.5 * x) + 0.5


def _tri_mm(tri_bf, x):
    hi = x.astype(BF)
    r1 = x - hi.astype(F32)
    mid = r1.astype(BF)
    lo = (r1 - mid.astype(F32)).astype(BF)
    return _dot(tri_bf, hi) + _dot(tri_bf, mid) + _dot(tri_bf, lo)


def _matmul(a, b, mode, *, name, c_in=None, out_dtype=F32, tm=512, tn=512, tk=512, ride=None, emit_a=False):
    if mode == "nn":
        (M, K), (_, N) = a.shape, b.shape
    elif mode == "nt":
        (M, K), (N, _) = a.shape, b.shape
    else:
        (K, M), (_, N) = a.shape, b.shape
    tm, tn, tk = min(tm, M), min(tn, N), min(tk, K)
    assert M % tm == 0 and N % tn == 0 and K % tk == 0, (name, M, N, K, tm, tn, tk)
    nk = K // tk
    assert not emit_a or (nk == 1 and mode != "tn" and c_in is None and ride is None)
    dot = {"nn": _dot, "nt": _dot_nt, "tn": _dot_tn}[mode]

    def body(*refs):
        if emit_a:
            a_ref, b_ref, o_ref, xa_ref, acc_ref = refs
        elif c_in is None:
            a_ref, b_ref, o_ref, acc_ref = refs
        else:
            a_ref, b_ref, c_ref, o_ref, acc_ref = refs
        k = pl.program_id(2)

        @pl.when(k == 0)
        def _():
            if c_in is None:
                acc_ref[...] = jnp.zeros_like(acc_ref)
            else:
                acc_ref[...] = c_ref[...].astype(F32)

        if emit_a:
            @pl.when(pl.program_id(1) == 0)
            def _():
                xa_ref[...] = a_ref[...].astype(BF)

        acc_ref[...] += dot(a_ref[...].astype(BF), b_ref[...].astype(BF))

        @pl.when(k == nk - 1)
        def _():
            o_ref[...] = acc_ref[...].astype(out_dtype)

    if mode == "tn":
        a_spec = pl.BlockSpec((tk, tm), lambda i, j, k: (k, i))
    else:
        a_spec = pl.BlockSpec((tm, tk), lambda i, j, k: (i, k))
    if mode == "nt":
        b_spec = pl.BlockSpec((tn, tk), lambda i, j, k: (j, k))
    else:
        b_spec = pl.BlockSpec((tk, tn), lambda i, j, k: (k, j))
    in_specs = [a_spec, b_spec]
    args = [a, b]
    if c_in is not None:
        in_specs.append(pl.BlockSpec((tm, tn), lambda i, j, k: (i, j)))
        args.append(c_in)
    out_shape = (jax.ShapeDtypeStruct((M, N), out_dtype),)
    out_specs = (pl.BlockSpec((tm, tn), lambda i, j, k: (i, j)),)
    if emit_a:
        out_shape += (jax.ShapeDtypeStruct((M, K), BF),)
        out_specs += (pl.BlockSpec((tm, tk), lambda i, j, k: (i, k)),)
    res = _call(
        body, name=name, out_shape=out_shape, grid=(M // tm, N // tn, nk), in_specs=in_specs, out_specs=out_specs,
        scratch_shapes=[pltpu.VMEM((tm, tn), F32)],
        sem=("parallel", "arbitrary", "arbitrary"), args=args, ride=ride)
    if emit_a:
        return res[0], res[1]
    return res[0] if ride is None else (res[0][0], res[1])


def _matmul_sum(c_in, parts, *, name, tm=1024, ride=None):
    M, N = c_in.shape
    tm = min(tm, M)
    n_p = len(parts)
    counts = [a.shape[1] // tk for a, _, tk in parts]
    starts = [sum(counts[:p]) for p in range(n_p)]
    nk = sum(counts)

    def body(*refs):
        a_refs, w_refs = refs[:n_p], refs[n_p:2 * n_p]
        c_ref, o_ref, acc_ref = refs[2 * n_p:]
        k = pl.program_id(1)

        @pl.when(k == 0)
        def _():
            acc_ref[...] = c_ref[...]

        for p in range(n_p):
            @pl.when(jnp.logical_and(k >= starts[p], k < starts[p] + counts[p]))
            def _(p=p):
                acc_ref[...] += _dot(a_refs[p][...].astype(BF), w_refs[p][...].astype(BF))

        @pl.when(k == nk - 1)
        def _():
            o_ref[...] = acc_ref[...]

    def kidx(p):
        return lambda k: jnp.clip(k - starts[p], 0, counts[p] - 1)

    in_specs = [pl.BlockSpec((tm, tk), lambda i, k, f=kidx(p): (i, f(k))) for p, (_, _, tk) in enumerate(parts)]
    in_specs += [pl.BlockSpec((tk, N), lambda i, k, f=kidx(p): (f(k), 0)) for p, (_, _, tk) in enumerate(parts)]
    in_specs.append(pl.BlockSpec((tm, N), lambda i, k: (i, 0)))
    res = _call(
        body, name=name, out_shape=(jax.ShapeDtypeStruct((M, N), F32),), grid=(M // tm, nk),
        in_specs=in_specs, out_specs=(pl.BlockSpec((tm, N), lambda i, k: (i, 0)),),
        scratch_shapes=[pltpu.VMEM((tm, N), F32)], sem=("parallel", "arbitrary"),
        args=[a for a, _, _ in parts] + [w for _, w, _ in parts] + [c_in], ride=ride)
    return res[0] if ride is None else (res[0][0], res[1])


def _gla_gate(pg_ref, rows, wg_ref, bg_ref):
    r = pg_ref[rows, 3072:3200].astype(BF)
    logit = _dot(r, wg_ref[...]) + bg_ref[...]
    la = (jnp.minimum(logit, 0.0) - jnp.log(1.0 + jnp.exp(-jnp.abs(logit)))) * (1.0 / GTAU)
    return r, logit, la


def _gla_fwd(pg, wg, bg, gn, ltri, *, nseq, S, tm, ride=None):
    T = pg.shape[0]
    nb, nc = S // tm, tm // GC
    qscale = GDK ** -0.5

    def body(pg_ref, wg_ref, bg_ref, gn_ref, l_ref, o_ref, zg_ref, st_ref, st_scr):
        @pl.when(pl.program_id(1) == 0)
        def _():
            st_scr[...] = jnp.zeros_like(st_scr)

        ltri_v = l_ref[...]
        causal = _iota((GC, GC), 0) >= _iota((GC, GC), 1)
        last_row = _iota((GC, GDK), 0) == GC - 1
        g = gn_ref[...]

        def chunk(c, carry):
            rows = pl.ds(pl.multiple_of(c * GC, GC), GC)
            _, _, la = _gla_gate(pg_ref, rows, wg_ref, bg_ref)
            b = _tri_mm(ltri_v, la)
            hs = range(GH)
            v, q_in, k_st, dec, st, a_raw, o_st, kv = [], [], [], [], [], [], [], []
            for h in hs:
                q = pg_ref[rows, h * GDK:(h + 1) * GDK]
                k = pg_ref[rows, 512 + h * GDK:512 + (h + 1) * GDK]
                v.append(pg_ref[rows, 1024 + h * GDV:1024 + (h + 1) * GDV].astype(BF))
                bh = b[:, h * GDK:(h + 1) * GDK]
                bl = jnp.sum(jnp.where(last_row, bh, 0.0), axis=0, keepdims=True)
                q_in.append((q * (qscale * jnp.exp(bh))).astype(BF))
                k_in = (k * jnp.exp(-bh)).astype(BF)
                k_st.append((k * jnp.exp(bl - bh)).astype(BF))
                dec.append(jnp.exp(bl))
                st.append(st_scr[h])
                st_ref[c, h] = st[h]
                a_raw.append(_dot_nt(q_in[h], k_in))
            for h in hs:
                o_st.append(_dot_nt(q_in[h], st[h].astype(BF)))
                kv.append(_dot_tn(v[h], k_st[h]))
            att = [jnp.where(causal, a_raw[h], 0.0).astype(BF) for h in hs]
            o = [_dot(att[h], v[h]) + o_st[h] for h in hs]
            for h in hs:
                st_scr[h] = st[h] * dec[h] + kv[h]
                og = pg_ref[rows, 2048 + h * GDV:2048 + (h + 1) * GDV]
                rstd = lax.rsqrt(jnp.mean(o[h] * o[h], axis=-1, keepdims=True) + RMS_EPS)
                o_ref[rows, h * GDV:(h + 1) * GDV] = o[h]
                zg_ref[rows, h * GDV:(h + 1) * GDV] = (o[h] * rstd * g * (og * _sigmoid(og))).astype(BF)
            return carry

        lax.fori_loop(0, nc, chunk, 0, unroll=True)

    full = lambda shp: pl.BlockSpec(shp, lambda b_, i: (0,) * len(shp))
    return _call(
        body, name="gla_fwd", ride=ride, sem=("parallel", "arbitrary"), args=(pg, wg, bg, gn, ltri),
        out_shape=(jax.ShapeDtypeStruct((T, GH * GDV), F32),
                   jax.ShapeDtypeStruct((T, GH * GDV), BF),
                   jax.ShapeDtypeStruct((T // GC, GH, GDV, GDK), F32)),
        grid=(nseq, nb),
        in_specs=[pl.BlockSpec((tm, PG_W), lambda b_, i: (b_ * nb + i, 0)),
                  full((128, 512)), full((1, 512)), full((1, GDV)), full((GC, GC))],
        out_specs=(pl.BlockSpec((tm, GH * GDV), lambda b_, i: (b_ * nb + i, 0)),
                   pl.BlockSpec((tm, GH * GDV), lambda b_, i: (b_ * nb + i, 0)),
                   pl.BlockSpec((nc, GH, GDV, GDK), lambda b_, i: (b_ * nb + i, 0, 0, 0))),
        scratch_shapes=[pltpu.VMEM((GH, GDV, GDK), F32)])


def _gla_bwd(pg, wg, bg, gn, ltri, utri, o, states, dzg, *, nseq, S, tm, ride=None):
    T = pg.shape[0]
    nb, nc = S // tm, tm // GC
    qscale = GDK ** -0.5

    def body(pg_ref, wg_ref, bg_ref, gn_ref, l_ref, u_ref, o_ref, st_ref, dzg_ref,
             dpg_ref, dwg_ref, dbg_ref, dgn_ref, dst_scr):
        first = jnp.logical_and(pl.program_id(0) == 0, pl.program_id(1) == 0)

        @pl.when(first)
        def _():
            dwg_ref[...] = jnp.zeros_like(dwg_ref)
            dbg_ref[...] = jnp.zeros_like(dbg_ref)
            dgn_ref[...] = jnp.zeros_like(dgn_ref)

        @pl.when(pl.program_id(1) == 0)
        def _():
            dst_scr[...] = jnp.zeros_like(dst_scr)

        ltri_v = l_ref[...]
        utri_v = u_ref[...]
        causal = _iota((GC, GC), 0) >= _iota((GC, GC), 1)
        last_row = _iota((GC, GDK), 0) == GC - 1
        g = gn_ref[...]

        def chunk(cc, carry):
            c = nc - 1 - cc
            rows = pl.ds(pl.multiple_of(c * GC, GC), GC)
            r, logit, la = _gla_gate(pg_ref, rows, wg_ref, bg_ref)
            b = _tri_mm(ltri_v, la)
            hs = range(GH)
            L = lambda: [None] * GH
            vb, eb, enb, ek, dec, q_in, k_in, k_st, q_inb, k_inb, st, dst, dob = (L() for _ in range(13))
            a_raw, da_raw, dq_st, dks, dv_st, dst_new, dbs, dgn = (L() for _ in range(8))
            for h in hs:
                q = pg_ref[rows, h * GDK:(h + 1) * GDK]
                k = pg_ref[rows, 512 + h * GDK:512 + (h + 1) * GDK]
                vb[h] = pg_ref[rows, 1024 + h * GDV:1024 + (h + 1) * GDV].astype(BF)
                og = pg_ref[rows, 2048 + h * GDV:2048 + (h + 1) * GDV]
                oh = o_ref[rows, h * GDV:(h + 1) * GDV]
                dz = dzg_ref[rows, h * GDV:(h + 1) * GDV].astype(F32)
                bh = b[:, h * GDK:(h + 1) * GDK]
                bl = jnp.sum(jnp.where(last_row, bh, 0.0), axis=0, keepdims=True)
                eb[h] = qscale * jnp.exp(bh)
                enb[h] = jnp.exp(-bh)
                ek[h] = jnp.exp(bl - bh)
                dec[h] = jnp.exp(bl)
                q_in[h], k_in[h], k_st[h] = q * eb[h], k * enb[h], k * ek[h]
                q_inb[h], k_inb[h] = q_in[h].astype(BF), k_in[h].astype(BF)
                st[h] = st_ref[c, h]
                dst[h] = dst_scr[h]
                rstd = lax.rsqrt(jnp.mean(oh * oh, axis=-1, keepdims=True) + RMS_EPS)
                ohat = oh * rstd
                sg = _sigmoid(og)
                don = dz * (og * sg)
                dpg_ref[rows, 2048 + h * GDV:2048 + (h + 1) * GDV] = (
                    dz * (ohat * g) * (sg * (1.0 + og * (1.0 - sg)))).astype(BF)
                dgn[h] = jnp.sum(don * ohat, axis=0, keepdims=True)
                gd = don * g
                dob[h] = (rstd * (gd - ohat * jnp.mean(gd * ohat, axis=-1, keepdims=True))).astype(BF)
                a_raw[h] = _dot_nt(q_inb[h], k_inb[h])
                da_raw[h] = _dot_nt(dob[h], vb[h])
            dgn_ref[...] += dgn[0] + dgn[1] + dgn[2] + dgn[3]
            for h in hs:
                dstb = dst[h].astype(BF)
                dq_st[h] = _dot(dob[h], st[h].astype(BF))
                dks[h] = _dot(vb[h], dstb)
                dv_st[h] = _dot_nt(k_st[h].astype(BF), dstb)
                dst_new[h] = _dot_tn(dob[h], q_inb[h])
            att = [jnp.where(causal, a_raw[h], 0.0).astype(BF) for h in hs]
            da = [jnp.where(causal, da_raw[h], 0.0).astype(BF) for h in hs]
            dqi = [_dot(da[h], k_inb[h]) + dq_st[h] for h in hs]
            dki = [_dot_tn(da[h], q_inb[h]) for h in hs]
            dv = [_dot_tn(att[h], dob[h]) + dv_st[h] for h in hs]
            for h in hs:
                dd = jnp.sum(dst[h] * st[h], axis=0, keepdims=True)
                dst_scr[h] = dst[h] * dec[h] + dst_new[h]
                kk = dks[h] * k_st[h]
                dbl = jnp.sum(kk, axis=0, keepdims=True) + dd * dec[h]
                db = dqi[h] * q_in[h] - dki[h] * k_in[h] - kk
                dbs[h] = db + jnp.where(last_row, dbl, 0.0)
                dpg_ref[rows, h * GDK:(h + 1) * GDK] = (dqi[h] * eb[h]).astype(BF)
                dpg_ref[rows, 512 + h * GDK:512 + (h + 1) * GDK] = (dki[h] * enb[h] + dks[h] * ek[h]).astype(BF)
                dpg_ref[rows, 1024 + h * GDV:1024 + (h + 1) * GDV] = dv[h].astype(BF)
            dla = _tri_mm(utri_v, jnp.concatenate(dbs, axis=1))
            dlogit = dla * (1.0 / GTAU) * _sigmoid(-logit)
            dlb = dlogit.astype(BF)
            dpg_ref[rows, 3072:3200] = _dot_nt(dlb, wg_ref[...]).astype(BF)
            dwg_ref[...] += _dot_tn(r, dlb)
            dbg_ref[...] += jnp.sum(dlogit, axis=0, keepdims=True)
            return carry

        lax.fori_loop(0, nc, chunk, 0, unroll=True)

    full = lambda shp: pl.BlockSpec(shp, lambda b_, i: (0,) * len(shp))
    rev = lambda b_, i: (b_ * nb + nb - 1 - i, 0)
    return _call(
        body, name="gla_bwd", ride=ride, sem=("arbitrary", "arbitrary"),
        args=(pg, wg, bg, gn, ltri, utri, o, states, dzg),
        out_shape=(jax.ShapeDtypeStruct((T, PG_W), BF),
                   jax.ShapeDtypeStruct((128, 512), F32),
                   jax.ShapeDtypeStruct((1, 512), F32),
                   jax.ShapeDtypeStruct((1, GDV), F32)),
        grid=(nseq, nb),
        in_specs=[pl.BlockSpec((tm, PG_W), rev),
                  full((128, 512)), full((1, 512)), full((1, GDV)), full((GC, GC)), full((GC, GC)),
                  pl.BlockSpec((tm, GH * GDV), rev),
                  pl.BlockSpec((nc, GH, GDV, GDK), lambda b_, i: (b_ * nb + nb - 1 - i, 0, 0, 0)),
                  pl.BlockSpec((tm, GH * GDV), rev)],
        out_specs=(pl.BlockSpec((tm, PG_W), rev), full((128, 512)), full((1, 512)), full((1, GDV))),
        scratch_shapes=[pltpu.VMEM((GH, GDV, GDK), F32)])


def _rope_tables(pos, invf):
    ang = pos.astype(F32) * invf
    lane = _iota(ang.shape, 1)
    sin = jnp.sin(ang)
    ssin = jnp.where(lane < 32, -sin, jnp.where(lane < 64, sin, 0.0))
    return jnp.cos(ang), ssin, lane


def _rope(x, cos, ssin, lane, sign):
    rot = jnp.where(lane < 32, pltpu.roll(x, 96, 1), pltpu.roll(x, 32, 1))
    return x * cos + sign * (rot * ssin)


def _rms_fwd(x, g):
    rstd = lax.rsqrt(jnp.mean(x * x, axis=-1, keepdims=True) + RMS_EPS)
    return x * rstd * g, x * rstd, rstd


def _rms_bwd(dy, xhat, rstd, g):
    gd = dy * g
    return rstd * (gd - xhat * jnp.mean(gd * xhat, axis=-1, keepdims=True)), jnp.sum(dy * xhat, axis=0, keepdims=True)


def _mla_prep_fwd(pm, pos, invf, gq, gkv, wuq, wukv, *, tm):
    T = pm.shape[0]

    def body(pm_ref, pos_ref, invf_ref, gq_ref, gkv_ref, wuq_ref, wukv_ref, qc_ref, kc_ref, v_ref):
        cos, ssin, lane = _rope_tables(pos_ref[...], invf_ref[...])
        cq, _, _ = _rms_fwd(pm_ref[:, 0:MQR], gq_ref[...])
        ckv, _, _ = _rms_fwd(pm_ref[:, 512:768], gkv_ref[...])
        qf = _dot(cq.astype(BF), wuq_ref[...])
        kvf = _dot(ckv.astype(BF), wukv_ref[...])
        kr = _rope(pm_ref[:, 384:512], cos, ssin, lane, 1.0).astype(BF)
        for h in range(MH):
            qc_ref[:, 256 * h:256 * h + 128] = (QK_SCALE_LOG2 * qf[:, 128 * h:128 * h + 128]).astype(BF)
            qr = qf[:, 1024 + 128 * h:1024 + 128 * h + 128]
            qc_ref[:, 256 * h + 128:256 * h + 256] = (QK_SCALE_LOG2 * _rope(qr, cos, ssin, lane, 1.0)).astype(BF)
            kc_ref[:, 256 * h:256 * h + 128] = kvf[:, 128 * h:128 * h + 128].astype(BF)
            kc_ref[:, 256 * h + 128:256 * h + 256] = kr
        v_ref[...] = kvf[:, 1024:2048].astype(BF)

    full = lambda shp: pl.BlockSpec(shp, lambda i: (0,) * len(shp))
    row = lambda w: pl.BlockSpec((tm, w), lambda i: (i, 0))
    return pl.pallas_call(
        body, name="mla_prep_fwd",
        out_shape=(jax.ShapeDtypeStruct((T, MH * 256), BF), jax.ShapeDtypeStruct((T, MH * 256), BF),
                   jax.ShapeDtypeStruct((T, MH * MV), BF)),
        grid=(T // tm,),
        in_specs=[row(PM_W), row(1), full((1, 128)), full((1, MQR)), full((1, MKR)),
                  full((MQR, 2048)), full((MKR, 2048))],
        out_specs=(row(MH * 256), row(MH * 256), row(MH * MV)),
        compiler_params=_params(("parallel",)),
    )(pm, pos, invf, gq, gkv, wuq, wukv)


def _mla_prep_bwd(pm, pos, invf, gq, gkv, wuq, wukv, dqc, dkc, dv, *, tm):
    T = pm.shape[0]

    def body(pm_ref, pos_ref, invf_ref, gq_ref, gkv_ref, wuq_ref, wukv_ref, dqc_ref, dkc_ref, dv_ref,
             dpm_ref, dwuq_ref, dwukv_ref, dgq_ref, dgkv_ref):
        @pl.when(pl.program_id(0) == 0)
        def _():
            dwuq_ref[...] = jnp.zeros_like(dwuq_ref)
            dwukv_ref[...] = jnp.zeros_like(dwukv_ref)
            dgq_ref[...] = jnp.zeros_like(dgq_ref)
            dgkv_ref[...] = jnp.zeros_like(dgkv_ref)

        cos, ssin, lane = _rope_tables(pos_ref[...], invf_ref[...])
        cq, cqh, cq_rstd = _rms_fwd(pm_ref[:, 0:MQR], gq_ref[...])
        ckv, ckvh, ckv_rstd = _rms_fwd(pm_ref[:, 512:768], gkv_ref[...])
        dqn, dqr, dkn = [], [], []
        dkr = jnp.zeros((tm, 128), F32)
        for h in range(MH):
            dqn.append(dqc_ref[:, 256 * h:256 * h + 128].astype(BF))
            dqr.append(_rope(dqc_ref[:, 256 * h + 128:256 * h + 256], cos, ssin, lane, -1.0).astype(BF))
            dkn.append(dkc_ref[:, 256 * h:256 * h + 128].astype(BF))
            dkr = dkr + dkc_ref[:, 256 * h + 128:256 * h + 256]
        dqf = jnp.concatenate(dqn + dqr, axis=1)
        dkvf = jnp.concatenate(dkn + [dv_ref[...].astype(BF)], axis=1)
        dwuq_ref[...] += _dot_tn(cq.astype(BF), dqf)
        dwukv_ref[...] += _dot_tn(ckv.astype(BF), dkvf)
        dcq, dgq = _rms_bwd(_dot_nt(dqf, wuq_ref[...]), cqh, cq_rstd, gq_ref[...])
        dckv, dgkv = _rms_bwd(_dot_nt(dkvf, wukv_ref[...]), ckvh, ckv_rstd, gkv_ref[...])
        dgq_ref[...] += dgq
        dgkv_ref[...] += dgkv
        dpm_ref[:, 0:MQR] = dcq.astype(BF)
        dpm_ref[:, 384:512] = _rope(dkr, cos, ssin, lane, -1.0).astype(BF)
        dpm_ref[:, 512:768] = dckv.astype(BF)

    full = lambda shp: pl.BlockSpec(shp, lambda i: (0,) * len(shp))
    row = lambda w: pl.BlockSpec((tm, w), lambda i: (i, 0))
    return pl.pallas_call(
        body, name="mla_prep_bwd",
        out_shape=(jax.ShapeDtypeStruct((T, PM_W), BF), jax.ShapeDtypeStruct((MQR, 2048), F32),
                   jax.ShapeDtypeStruct((MKR, 2048), F32), jax.ShapeDtypeStruct((1, MQR), F32),
                   jax.ShapeDtypeStruct((1, MKR), F32)),
        grid=(T // tm,),
        in_specs=[row(PM_W), row(1), full((1, 128)), full((1, MQR)), full((1, MKR)),
                  full((MQR, 2048)), full((MKR, 2048)), row(MH * 256), row(MH * 256), row(MH * MV)],
        out_specs=(row(PM_W), full((MQR, 2048)), full((MKR, 2048)), full((1, MQR)), full((1, MKR))),
        compiler_params=_params(("arbitrary",)),
    )(pm, pos, invf, gq, gkv, wuq, wukv, dqc, dkc, dv)


def _flash_fwd(qc, kc, v, *, nseq, S, tq, ride=None):
    T = qc.shape[0]
    nq = S // tq
    hp = FLASH_HP_FWD

    def body(q_ref, k_ref, v_ref, o_ref, lse_ref):
        i = pl.program_id(2)
        causal = _iota((tq, tq), 0) >= _iota((tq, tq), 1)

        def step(j, carry, masked):
            rows = pl.ds(pl.multiple_of(j * tq, tq), tq)
            hs = range(hp)
            s = [_dot_nt(q_ref[:, 256 * hh:256 * hh + 256], k_ref[rows, 256 * hh:256 * hh + 256]) for hh in hs]
            p, stats = [], []
            for hh in hs:
                m, l, _ = carry[hh]
                sh = jnp.where(causal, s[hh], NEG) if masked else s[hh]
                m_new = jnp.maximum(m, jnp.max(sh, axis=-1, keepdims=True))
                ph = jnp.exp2(sh - m_new)
                a = jnp.exp2(m - m_new)
                stats.append((m_new, a * l + jnp.sum(ph, axis=-1, keepdims=True), a))
                p.append(ph.astype(BF))
            pv = [_dot(p[hh], v_ref[rows, MV * hh:MV * hh + MV]) for hh in hs]
            return tuple((stats[hh][0], stats[hh][1], stats[hh][2] * carry[hh][2] + pv[hh]) for hh in hs)

        init = ((jnp.full((tq, 1), NEG, F32), jnp.zeros((tq, 1), F32), jnp.zeros((tq, MV), F32)),) * hp
        carry = lax.fori_loop(0, i, lambda j, c: step(j, c, False), init)
        for hh, (m, l, acc) in enumerate(step(i, carry, True)):
            o_ref[:, MV * hh:MV * hh + MV] = (acc / l).astype(BF)
            lse_ref[:, 128 * hh:128 * hh + 128] = jnp.broadcast_to(m + jnp.log2(l), (tq, 128))

    return _call(
        body, name="flash_fwd", ride=ride, sem=("parallel", "parallel", "arbitrary"), args=(qc, kc, v),
        out_shape=(jax.ShapeDtypeStruct((T, MH * MV), BF), jax.ShapeDtypeStruct((T, MH * 128), F32)),
        grid=(nseq, MH // hp, nq),
        in_specs=[pl.BlockSpec((tq, 256 * hp), lambda b_, h, i: (b_ * nq + i, h)),
                  pl.BlockSpec((S, 256 * hp), lambda b_, h, i: (b_, h)),
                  pl.BlockSpec((S, MV * hp), lambda b_, h, i: (b_, h))],
        out_specs=(pl.BlockSpec((tq, MV * hp), lambda b_, h, i: (b_ * nq + i, h)),
                   pl.BlockSpec((tq, 128 * hp), lambda b_, h, i: (b_ * nq + i, h))))


def _flash_bwd(qc, kc, v, o, do, lse, *, nseq, S, tq, ride=None):
    T = qc.shape[0]
    nq = S // tq

    def body(q_ref, k_ref, v_ref, o_ref, do_ref, lse_ref, dq_ref, dk_ref, dv_ref, dq_scr, delta_scr):
        j = pl.program_id(2)

        @pl.when(j == 0)
        def _():
            dq_scr[...] = jnp.zeros_like(dq_scr)
            for hh in range(FLASH_HP):
                od = o_ref[:, MV * hh:MV * hh + MV].astype(F32) * do_ref[:, MV * hh:MV * hh + MV].astype(F32)
                delta_scr[:, 128 * hh:128 * hh + 128] = jnp.broadcast_to(jnp.sum(od, axis=-1, keepdims=True), (S, 128))

        causal = _iota((tq, tq), 0) >= _iota((tq, tq), 1)

        def step(i, carry, masked):
            rows = pl.ds(pl.multiple_of(i * tq, tq), tq)
            hs = range(FLASH_HP)
            qs = [slice(256 * hh, 256 * hh + 256) for hh in hs]
            vs = [slice(MV * hh, MV * hh + MV) for hh in hs]
            ls = [slice(128 * hh, 128 * hh + 1) for hh in hs]
            s = [_dot_nt(q_ref[rows, qs[hh]], k_ref[:, qs[hh]]) for hh in hs]
            dp = [_dot_nt(do_ref[rows, vs[hh]], v_ref[:, vs[hh]]) for hh in hs]
            pb, ds = [], []
            for hh in hs:
                p = jnp.exp2(s[hh] - lse_ref[rows, ls[hh]])
                if masked:
                    p = jnp.where(causal, p, 0.0)
                pb.append(p.astype(BF))
                ds.append((p * (dp[hh] - delta_scr[rows, ls[hh]])).astype(BF))
            dv = [carry[hh][1] + _dot_tn(pb[hh], do_ref[rows, vs[hh]]) for hh in hs]
            dk = [carry[hh][0] + _dot_tn(ds[hh], q_ref[rows, qs[hh]]) for hh in hs]
            for hh in hs:
                dq_scr[rows, qs[hh]] += _dot(ds[hh], k_ref[:, qs[hh]])
            return tuple((dk[hh], dv[hh]) for hh in hs)

        init = ((jnp.zeros((tq, 256), F32), jnp.zeros((tq, MV), F32)),) * FLASH_HP
        carry = step(j, init, True)
        carry = lax.fori_loop(j + 1, nq, lambda i, c: step(i, c, False), carry)
        for hh, (dk, dv) in enumerate(carry):
            dk_ref[:, 256 * hh:256 * hh + 256] = dk * (1.0 / LOG2E)
            dv_ref[:, MV * hh:MV * hh + MV] = dv

        @pl.when(j == nq - 1)
        def _():
            dq_ref[...] = dq_scr[...] * QK_SCALE

    hp = FLASH_HP
    seq = lambda w: pl.BlockSpec((S, w * hp), lambda b_, h, j: (b_, h))
    blk = lambda w: pl.BlockSpec((tq, w * hp), lambda b_, h, j: (b_ * nq + j, h))
    return _call(
        body, name="flash_bwd", ride=ride, sem=("parallel", "parallel", "arbitrary"), args=(qc, kc, v, o, do, lse),
        out_shape=(jax.ShapeDtypeStruct((T, MH * 256), F32), jax.ShapeDtypeStruct((T, MH * 256), F32),
                   jax.ShapeDtypeStruct((T, MH * MV), F32)),
        grid=(nseq, MH // hp, nq),
        in_specs=[seq(256), blk(256), blk(MV), seq(MV), seq(MV), seq(128)],
        out_specs=(seq(256), blk(256), blk(MV)),
        scratch_shapes=[pltpu.VMEM((S, 256 * hp), F32), pltpu.VMEM((S, 128 * hp), F32)])


def _ln_fwd(pre, g, b):
    mu = jnp.mean(pre, axis=-1, keepdims=True)
    xc = pre - mu
    rstd = lax.rsqrt(jnp.mean(xc * xc, axis=-1, keepdims=True) + LN_EPS)
    xhat = xc * rstd
    return xhat * g + b, xhat, rstd


def _ln_bwd(dy, xhat, rstd, g):
    dxh = dy * g
    dx = rstd * (dxh - jnp.mean(dxh, axis=-1, keepdims=True) - xhat * jnp.mean(dxh * xhat, axis=-1, keepdims=True))
    return dx, jnp.sum(dy * xhat, axis=0, keepdims=True), jnp.sum(dy, axis=0, keepdims=True)


def _post_attn_fwd(zg, attn, pt, x, wgo, wmo, wout, g1, b1, *, tm, ride=None):
    T = x.shape[0]

    def body(zg_ref, at_ref, pt_ref, x_ref, wgo_ref, wmo_ref, wout_ref, g_ref, b_ref,
             yg_ref, ym_ref, mix_ref, pre_ref, hb_ref):
        yg = _dot(zg_ref[...], wgo_ref[...])
        ym = _dot(at_ref[...], wmo_ref[...])
        mix = (_sigmoid(pt_ref[:, 0:D].astype(F32)) * yg + _sigmoid(pt_ref[:, D:2 * D].astype(F32)) * ym).astype(BF)
        pre = ALPHA * x_ref[...] + _dot(mix, wout_ref[...])
        h, _, _ = _ln_fwd(pre, g_ref[...], b_ref[...])
        yg_ref[...] = yg.astype(BF)
        ym_ref[...] = ym.astype(BF)
        mix_ref[...] = mix
        pre_ref[...] = pre
        hb_ref[...] = h.astype(BF)

    full = lambda shp: pl.BlockSpec(shp, lambda i: (0,) * len(shp))
    row = lambda w: pl.BlockSpec((tm, w), lambda i: (i, 0))
    sd = lambda dt: jax.ShapeDtypeStruct((T, D), dt)
    return _call(
        body, name="post_attn_fwd", ride=ride, sem=("parallel",), args=(zg, attn, pt, x, wgo, wmo, wout, g1, b1),
        out_shape=(sd(BF), sd(BF), sd(BF), sd(F32), sd(BF)),
        grid=(T // tm,),
        in_specs=[row(D), row(D), row(PT_W), row(D), full((D, D)), full((D, D)), full((D, D)),
                  full((1, D)), full((1, D))],
        out_specs=(row(D),) * 5)


def _post_attn_bwd(dh, pre, pt, yg, ym, wgo, wmo, wout, g1, *, tm):
    T = dh.shape[0]

    def body(dh_ref, pre_ref, pt_ref, yg_ref, ym_ref, wgo_ref, wmo_ref, wout_ref, g_ref,
             dx_ref, dpreb_ref, dpt_ref, dygb_ref, dymb_ref, dzg_ref, dat_ref, dg_ref, db_ref):
        @pl.when(pl.program_id(0) == 0)
        def _():
            dg_ref[...] = jnp.zeros_like(dg_ref)
            db_ref[...] = jnp.zeros_like(db_ref)

        pre = pre_ref[...]
        mu = jnp.mean(pre, axis=-1, keepdims=True)
        xc = pre - mu
        rstd = lax.rsqrt(jnp.mean(xc * xc, axis=-1, keepdims=True) + LN_EPS)
        dpre, dg, db = _ln_bwd(dh_ref[...], xc * rstd, rstd, g_ref[...])
        dg_ref[...] += dg
        db_ref[...] += db
        dx_ref[...] = ALPHA * dpre
        dpreb = dpre.astype(BF)
        dpreb_ref[...] = dpreb
        dmix = _dot_nt(dpreb, wout_ref[...])
        sa = _sigmoid(pt_ref[:, 0:D].astype(F32))
        sb = _sigmoid(pt_ref[:, D:2 * D].astype(F32))
        dpt_ref[:, 0:D] = (dmix * yg_ref[...].astype(F32) * (sa * (1.0 - sa))).astype(BF)
        dpt_ref[:, D:2 * D] = (dmix * ym_ref[...].astype(F32) * (sb * (1.0 - sb))).astype(BF)
        dyg = (dmix * sa).astype(BF)
        dym = (dmix * sb).astype(BF)
        dygb_ref[...] = dyg
        dymb_ref[...] = dym
        dzg_ref[...] = _dot_nt(dyg, wgo_ref[...]).astype(BF)
        dat_ref[...] = _dot_nt(dym, wmo_ref[...]).astype(BF)

    full = lambda shp: pl.BlockSpec(shp, lambda i: (0,) * len(shp))
    row = lambda w: pl.BlockSpec((tm, w), lambda i: (i, 0))
    sd = lambda w, dt: jax.ShapeDtypeStruct((T, w), dt)
    return pl.pallas_call(
        body, name="post_attn_bwd",
        out_shape=(sd(D, F32), sd(D, BF), sd(PT_W, BF), sd(D, BF), sd(D, BF), sd(D, BF), sd(D, BF),
                   jax.ShapeDtypeStruct((1, D), F32), jax.ShapeDtypeStruct((1, D), F32)),
        grid=(T // tm,),
        in_specs=[row(D), row(D), row(PT_W), row(D), row(D), full((D, D)), full((D, D)), full((D, D)),
                  full((1, D))],
        out_specs=(row(D), row(D), row(PT_W), row(D), row(D), row(D), row(D), full((1, D)), full((1, D))),
        compiler_params=_params(("arbitrary",)),
    )(dh, pre, pt, yg, ym, wgo, wmo, wout, g1)


def _shift_down(u, prev, k):
    r = pltpu.roll(u, k, 0)
    p = pltpu.roll(prev, k, 0)
    head = jnp.where(_iota(p.shape, 0) < k, p, r[0:8, :])
    return jnp.concatenate([head, r[8:, :]], axis=0)


def _conv3(u, prev, w_ref, b_ref):
    return (w_ref[0:1, :] * _shift_down(u, prev, 2) + w_ref[1:2, :] * _shift_down(u, prev, 1)
            + w_ref[2:3, :] * u + b_ref[...])


def _ffn_up_fwd(hb, wug, wuv, cw, cb, *, S, tm, tn):
    T = hb.shape[0]
    nj, nbs = DFF // tn, S // tm

    def body(h_ref, wg_ref, wv_ref, cwg_ref, cwv_ref, cbg_ref, cbv_ref,
             ug_ref, uv_ref, ucg_ref, ucv_ref, f_ref, pg_scr, pv_scr):
        @pl.when(pl.program_id(1) % nbs == 0)
        def _():
            pg_scr[...] = jnp.zeros_like(pg_scr)
            pv_scr[...] = jnp.zeros_like(pv_scr)

        h = h_ref[...]
        ug = _dot(h, wg_ref[...])
        uv = _dot(h, wv_ref[...])
        ucg = _conv3(ug, pg_scr[...], cwg_ref, cbg_ref)
        ucv = _conv3(uv, pv_scr[...], cwv_ref, cbv_ref)
        pg_scr[...] = ug[tm - 8:, :]
        pv_scr[...] = uv[tm - 8:, :]
        ug_ref[...] = ug.astype(BF)
        uv_ref[...] = uv.astype(BF)
        ucg_ref[...] = ucg
        ucv_ref[...] = ucv
        f_ref[...] = (ucg * _sigmoid(ucg) * ucv).astype(BF)

    tile = pl.BlockSpec((tm, tn), lambda j, i: (i, j))
    return pl.pallas_call(
        body, name="ffn_up_fwd",
        out_shape=(jax.ShapeDtypeStruct((T, DFF), BF), jax.ShapeDtypeStruct((T, DFF), BF),
                   jax.ShapeDtypeStruct((T, DFF), F32), jax.ShapeDtypeStruct((T, DFF), F32),
                   jax.ShapeDtypeStruct((T, DFF), BF)),
        grid=(nj, T // tm),
        in_specs=[pl.BlockSpec((tm, D), lambda j, i: (i, 0)),
                  pl.BlockSpec((D, tn), lambda j, i: (0, j)), pl.BlockSpec((D, tn), lambda j, i: (0, j)),
                  pl.BlockSpec((3, tn), lambda j, i: (0, j)), pl.BlockSpec((3, tn), lambda j, i: (0, j + nj)),
                  pl.BlockSpec((1, tn), lambda j, i: (0, j)), pl.BlockSpec((1, tn), lambda j, i: (0, j + nj))],
        out_specs=(tile, tile, tile, tile, tile),
        scratch_shapes=[pltpu.VMEM((8, tn), F32), pltpu.VMEM((8, tn), F32)],
        compiler_params=_params(("parallel", "arbitrary")),
    )(hb, wug, wuv, cw, cw, cb, cb)


def _ffn_bwd(dpreb, wd, ug, uv, ucg, ucv, cw, *, S, tm, tn):
    T = dpreb.shape[0]
    nj, nb, nbs = DFF // tn, T // tm, S // tm
    r_, c_ = lax.broadcasted_iota(jnp.int32, (tm, tm), 0), lax.broadcasted_iota(jnp.int32, (tm, tm), 1)
    s1, s2 = (c_ == r_ + 1).astype(BF), (c_ == r_ + 2).astype(BF)

    def body(dp_ref, wd_ref, ug_ref, uv_ref, ucg_ref, ucv_ref, cwg_ref, cwv_ref, s1_ref, s2_ref,
             dug_ref, duv_ref, dcg_ref, dcv_ref, ng_scr, nv_scr):
        ii = pl.program_id(1)
        i = nb - 1 - ii
        tail_row = _iota((8, tn), 0)

        @pl.when(ii == 0)
        def _():
            dcg_ref[...] = jnp.zeros_like(dcg_ref)
            dcv_ref[...] = jnp.zeros_like(dcv_ref)

        @pl.when(i % nbs == nbs - 1)
        def _():
            ng_scr[...] = jnp.zeros_like(ng_scr)
            nv_scr[...] = jnp.zeros_like(nv_scr)

        df = _dot_nt(dp_ref[...], wd_ref[...])
        ucg = ucg_ref[...]
        sg = _sigmoid(ucg)
        ducg = df * ucv_ref[...] * (sg * (1.0 + ucg * (1.0 - sg)))
        ducv = df * (ucg * sg)

        def finish(duc, u_ref, w, nxt_scr, du_ref, dc_ref):
            nxt = nxt_scr[...]
            db = duc.astype(BF)

            def shifted(s_ref, k):
                r = _dot(s_ref[...], db)
                tail = jnp.where(tail_row >= 8 - k, pltpu.roll(nxt, 8 - k, 0), r[tm - 8:, :])
                return jnp.concatenate([r[:tm - 8, :], tail], axis=0)

            up1 = shifted(s1_ref, 1)
            up2 = shifted(s2_ref, 2)
            du_ref[...] = (w[2:3, :] * duc + w[1:2, :] * up1 + w[0:1, :] * up2).astype(BF)
            nxt_scr[...] = duc[0:8, :]
            u = u_ref[...].astype(F32)
            for row, z in enumerate((u * up2, u * up1, u * duc, duc)):
                dc_ref[row:row + 1, :] += jnp.sum(z, axis=0, keepdims=True)

        finish(ducg, ug_ref, cwg_ref, ng_scr, dug_ref, dcg_ref)
        finish(ducv, uv_ref, cwv_ref, nv_scr, duv_ref, dcv_ref)

    tile = pl.BlockSpec((tm, tn), lambda j, ii: (nb - 1 - ii, j))
    acc = pl.BlockSpec((8, tn), lambda j, ii: (0, j))
    return pl.pallas_call(
        body, name="ffn_bwd",
        out_shape=(jax.ShapeDtypeStruct((T, DFF), BF), jax.ShapeDtypeStruct((T, DFF), BF),
                   jax.ShapeDtypeStruct((8, DFF), F32), jax.ShapeDtypeStruct((8, DFF), F32)),
        grid=(nj, nb),
        in_specs=[pl.BlockSpec((tm, D), lambda j, ii: (nb - 1 - ii, 0)),
                  pl.BlockSpec((tn, D), lambda j, ii: (j, 0)),
                  tile, tile, tile, tile,
                  pl.BlockSpec((3, tn), lambda j, ii: (0, j)), pl.BlockSpec((3, tn), lambda j, ii: (0, j + nj)),
                  pl.BlockSpec((tm, tm), lambda j, ii: (0, 0)), pl.BlockSpec((tm, tm), lambda j, ii: (0, 0))],
        out_specs=(tile, tile, acc, acc),
        scratch_shapes=[pltpu.VMEM((8, tn), F32), pltpu.VMEM((8, tn), F32)],
        compiler_params=_params(("parallel", "arbitrary")),
    )(dpreb, wd, ug, uv, ucg, ucv, cw, cw, s1, s2)


def _down_ln2_loss(f_in, wd, pre1, target, g1, b1, g2, b2, *, tm):
    T = pre1.shape[0]

    def body(f_ref, wd_ref, p1_ref, t_ref, g1_ref, b1_ref, g_ref, b_ref, dpb_ref, dh_ref, loss_ref, dg_ref, db_ref):
        @pl.when(pl.program_id(0) == 0)
        def _():
            loss_ref[...] = jnp.zeros_like(loss_ref)
            dg_ref[...] = jnp.zeros_like(dg_ref)
            db_ref[...] = jnp.zeros_like(db_ref)

        halves = [pl.ds(s * (tm // 2), tm // 2) for s in range(2)]
        f = [_dot(f_ref[hs, :], wd_ref[...]) for hs in halves]
        for hs, fh in zip(halves, f):
            h, _, _ = _ln_fwd(p1_ref[hs, :], g1_ref[...], b1_ref[...])
            pre = ALPHA * h + fh
            out, xhat, rstd = _ln_fwd(pre, g_ref[...], b_ref[...])
            diff = out - t_ref[hs, :]
            loss_ref[...] += 0.5 * jnp.sum(jnp.mean(diff * diff, axis=-1, keepdims=True))
            dpre, dg, db = _ln_bwd(diff * (1.0 / D), xhat, rstd, g_ref[...])
            dg_ref[...] += dg
            db_ref[...] += db
            dpb_ref[hs, :] = dpre.astype(BF)
            dh_ref[hs, :] = ALPHA * dpre

    full = lambda shp: pl.BlockSpec(shp, lambda i: (0,) * len(shp))
    row = lambda w: pl.BlockSpec((tm, w), lambda i: (i, 0))
    return pl.pallas_call(
        body, name="down_ln2_loss",
        out_shape=(jax.ShapeDtypeStruct((T, D), BF), jax.ShapeDtypeStruct((T, D), F32),
                   jax.ShapeDtypeStruct((8, 128), F32), jax.ShapeDtypeStruct((1, D), F32),
                   jax.ShapeDtypeStruct((1, D), F32)),
        grid=(T // tm,),
        in_specs=[row(DFF), full((DFF, D)), row(D), row(D), full((1, D)), full((1, D)), full((1, D)), full((1, D))],
        out_specs=(row(D), row(D), full((8, 128)), full((1, D)), full((1, D))),
        compiler_params=_params(("arbitrary",)),
    )(f_in, wd, pre1, target, g1, b1, g2, b2)


def _adamw(parts, w, m, v, *, name):
    n, R, C = parts.shape
    tr, tc = R, C
    for cand in range(min(R, 256), 15, -1):
        if R % cand == 0 and cand % 16 == 0:
            tr = cand
            break
    if tr == R and R * C > 65536 and C % 256 == 0:
        tc = 256
    c1 = 1.0 - ADAM_B1 ** ADAM_STEP
    c2 = 1.0 - ADAM_B2 ** ADAM_STEP

    def body(p_ref, w_ref, m_ref, v_ref, g_ref, d_ref, nm_ref, nv_ref):
        g = p_ref[0].astype(F32)
        for s in range(1, n):
            g = g + p_ref[s].astype(F32)
        nm = ADAM_B1 * m_ref[...] + (1.0 - ADAM_B1) * g
        nv = ADAM_B2 * v_ref[...] + (1.0 - ADAM_B2) * (g * g)
        g_ref[...] = g
        nm_ref[...] = nm
        nv_ref[...] = nv
        d_ref[...] = -ADAM_LR * ((nm / c1) / (jnp.sqrt(nv / c2) + ADAM_EPS) + ADAM_WD * w_ref[...])

    blk = pl.BlockSpec((tr, tc), lambda i, j: (i, j))
    sd = jax.ShapeDtypeStruct((R, C), F32)
    return pl.pallas_call(
        body, name=name,
        out_shape=(sd, sd, sd, sd),
        grid=(R // tr, C // tc),
        in_specs=[pl.BlockSpec((n, tr, tc), lambda i, j: (0, i, j)), blk, blk, blk],
        out_specs=(blk, blk, blk, blk),
        compiler_params=_params(("parallel", "parallel")),
    )(parts, w, m, v)


class _Exchange:
    def __init__(self, items):
        self.items = [(src if sc else [(src, 0)], sc) for src, sc in items]
        self.arrays = [arr for srcs, _ in self.items for arr, _ in srcs]
        self.n = len(self.items)
        self.n_in = len(self.arrays)

    def out_shape(self):
        return tuple(jax.ShapeDtypeStruct((NDEV,) + (srcs[0][0].shape[1:] if sc else srcs[0][0].shape),
                                          srcs[0][0].dtype) for srcs, sc in self.items)

    def scratch(self):
        return [pltpu.SemaphoreType.DMA((self.n, NDEV - 1)), pltpu.SemaphoreType.DMA((self.n, NDEV - 1)),
                pltpu.SemaphoreType.DMA((self.n,))]

    def _emit(self, ins, outs, sems, phase):
        send_sems, recv_sems, loc_sems = sems
        x, y, c = lax.axis_index("x"), lax.axis_index("y"), lax.axis_index("c")
        me = 4 * x + 2 * y + c
        flip = lambda p, d: 1 - p if d else p

        def inside(p, lo, n):
            return None if (lo, n) == (0, NDEV) else jnp.logical_and(p >= lo, p < lo + n)

        def when(cond, fn):
            if cond is None:
                fn()
            else:
                pl.when(cond)(fn)

        pos = 0
        for a, (srcs, sc) in enumerate(self.items):
            refs = ins[pos:pos + len(srcs)]
            pos += len(srcs)
            ranges = [(lo, arr.shape[0]) if sc else (0, NDEV) for arr, lo in srcs]
            mine = [inside(me, lo, n) for lo, n in ranges]
            i_receive = None if None in mine else functools.reduce(jnp.logical_or, mine)
            for ref, (lo, n), cond in zip(refs, ranges, mine):
                def local(ref=ref, lo=lo):
                    cp = pltpu.make_async_copy(ref.at[me - lo] if sc else ref, outs[a].at[me], loc_sems.at[a])
                    cp.start() if phase == 0 else cp.wait()
                if phase != 1:
                    when(cond, local)
            for k in range(1, NDEV):
                px, py, pc = flip(x, k & 4), flip(y, k & 2), flip(c, k & 1)
                peer = 4 * px + 2 * py + pc
                mk = functools.partial(pltpu.make_async_remote_copy,
                                       send_sem=send_sems.at[a, k - 1], recv_sem=recv_sems.at[a, k - 1],
                                       device_id=(px, py, pc), device_id_type=MESH_ID)
                if phase == 1:
                    def arrival(mk=mk, peer=peer):
                        mk(src_ref=refs[0].at[0] if sc else refs[0], dst_ref=outs[a].at[peer]).wait_recv()
                    when(i_receive, arrival)
                    continue
                for ref, (lo, n) in zip(refs, ranges):
                    def send(mk=mk, ref=ref, lo=lo, peer=peer):
                        cp = mk(src_ref=ref.at[peer - lo] if sc else ref, dst_ref=outs[a].at[me])
                        cp.start() if phase == 0 else cp.wait_send()
                    when(inside(peer, lo, n), send)

    def start(self, ins, outs, sems):
        self._emit(ins, outs, sems, 0)

    def wait(self, ins, outs, sems):
        self._emit(ins, outs, sems, 1)
        self._emit(ins, outs, sems, 2)


def _call(body, *, name, grid, in_specs, out_specs, out_shape, args, scratch_shapes=(), sem=None, ride=None):
    if ride is None:
        return pl.pallas_call(body, name=name, grid=grid, in_specs=list(in_specs), out_specs=tuple(out_specs),
                              out_shape=tuple(out_shape), scratch_shapes=list(scratch_shapes),
                              compiler_params=_params(sem))(*args)
    n_in, n_out, n_scr, ne, ne_in = len(args), len(out_shape), len(scratch_shapes), ride.n, ride.n_in

    def ride_body(*refs):
        ins, ex_in = refs[:n_in], refs[n_in:n_in + ne_in]
        o0 = n_in + ne_in
        outs, ex_out = refs[o0:o0 + n_out], refs[o0 + n_out:o0 + n_out + ne]
        scr = refs[o0 + n_out + ne:o0 + n_out + ne + n_scr]
        sems = refs[o0 + n_out + ne + n_scr:]
        first = functools.reduce(jnp.logical_and, [pl.program_id(d) == 0 for d in range(len(grid))])
        last = functools.reduce(jnp.logical_and, [pl.program_id(d) == grid[d] - 1 for d in range(len(grid))])

        @pl.when(first)
        def _():
            ride.start(ex_in, ex_out, sems)

        body(*ins, *outs, *scr)

        @pl.when(last)
        def _():
            ride.wait(ex_in, ex_out, sems)

    anyspec = pl.BlockSpec(memory_space=pl.ANY)
    res = pl.pallas_call(
        ride_body, name=name, grid=grid,
        in_specs=list(in_specs) + [anyspec] * ne_in,
        out_specs=tuple(out_specs) + (anyspec,) * ne,
        out_shape=tuple(out_shape) + ride.out_shape(),
        scratch_shapes=list(scratch_shapes) + ride.scratch(),
        compiler_params=_params(("arbitrary",) * len(grid)),
    )(*args, *ride.arrays)
    return tuple(res[:n_out]), tuple(res[n_out:])


def _gather_two_level(arrays, *, name):
    n = len(arrays)

    def body(*refs):
        ins, outs = refs[:n], refs[n:2 * n]
        send_sems, recv_sems, loc_sems = refs[2 * n:]
        x, y, c = lax.axis_index("x"), lax.axis_index("y"), lax.axis_index("c")
        sibling = (x, y, 1 - c)
        chips = [(1 - x, y), (x, 1 - y), (1 - x, 1 - y)]
        idx = lambda px, py, pc: 4 * px + 2 * py + pc
        me = idx(x, y, c)

        def copy(a, k, block, to, src=None):
            return pltpu.make_async_remote_copy(
                src_ref=outs[a].at[block] if src is None else src, dst_ref=outs[a].at[block],
                send_sem=send_sems.at[a, k], recv_sem=recv_sems.at[a, k], device_id=to, device_id_type=MESH_ID)

        local = [pltpu.make_async_copy(ins[a], outs[a].at[me], loc_sems.at[a]) for a in range(n)]
        sent = []
        for a in range(n):
            sent.append(copy(a, 0, me, sibling, src=ins[a]))
            sent += [copy(a, 1 + j, me, (*chip, c), src=ins[a]) for j, chip in enumerate(chips)]
        for cp in local + sent:
            cp.start()
        for j, chip in enumerate(chips):
            for a in range(n):
                copy(a, 1 + j, idx(*chip, c), sibling).wait_recv()
                passed = copy(a, 4 + j, idx(*chip, c), sibling)
                passed.start()
                sent.append(passed)
        for a in range(n):
            copy(a, 0, idx(x, y, 1 - c), sibling).wait_recv()
            for j, chip in enumerate(chips):
                copy(a, 4 + j, idx(*chip, 1 - c), sibling).wait_recv()
        for cp in sent:
            cp.wait_send()
        for cp in local:
            cp.wait()

    anyspec = pl.BlockSpec(memory_space=pl.ANY)
    return pl.pallas_call(
        body, name=name,
        out_shape=tuple(jax.ShapeDtypeStruct((NDEV,) + a.shape, a.dtype) for a in arrays),
        in_specs=[anyspec] * n, out_specs=(anyspec,) * n,
        scratch_shapes=[pltpu.SemaphoreType.DMA((n, NDEV - 1)), pltpu.SemaphoreType.DMA((n, NDEV - 1)),
                        pltpu.SemaphoreType.DMA((n,))],
    )(*arrays)


def _tri_consts():
    r = lax.broadcasted_iota(jnp.int32, (GC, GC), 0)
    c = lax.broadcasted_iota(jnp.int32, (GC, GC), 1)
    return (r >= c).astype(BF), (r <= c).astype(BF)


def _local_step(x, positions, target, w, hooks=None):
    g = {}

    def run(host, fn, *a, **kw):
        h = None if hooks is None else hooks.get(host)
        if h is None:
            return fn(*a, **kw)
        out, received = fn(*a, ride=_Exchange(h[0](w, g)), **kw)
        h[1](received, w, g)
        return out

    nseq, S, _ = x.shape
    T = nseq * S
    tm = min(TOKEN_TM, S)
    tq = min(FLASH_TQ, S)
    x2 = x.reshape(T, D)
    pos = positions.reshape(T, 1)
    half = ROPE // 2
    inv = THETA ** (-jnp.arange(half, dtype=F32) / half)
    invf = jnp.concatenate([inv, inv, jnp.zeros((64,), F32)]).reshape(1, 128)
    ltri, utri = _tri_consts()

    pt, xb = _matmul(x2, w["w_tt"], "nt", name="proj_t", out_dtype=BF, tm=1024, tn=1024, tk=1024, emit_a=True)
    pg = run("proj_g", _matmul, xb, w["w_gt"], "nt", name="proj_g", tm=1024, tn=640, tk=1024)
    pm = _matmul(xb, w["w_mt"], "nt", name="proj_m", tm=1024, tn=768, tk=1024)
    o, zg, states = run("gla_fwd", _gla_fwd, pg, w["wg"], w["bg"], w["gn"], ltri, nseq=nseq, S=S, tm=tm)
    qc, kc, v = _mla_prep_fwd(pm, pos, invf, w["gq"], w["gkv"], w["wuq"], w["wukv"], tm=min(2 * tm, S))
    attn, lse = run("flash_fwd", _flash_fwd, qc, kc, v, nseq=nseq, S=S, tq=tq)
    yg, ym, mix, pre1, h1b = run("post_attn_fwd", _post_attn_fwd, zg, attn, pt, x2, w["wgo"], w["wmo"], w["wout"],
                                 w["g1"], w["b1"], tm=min(2 * tm, S))
    ug, uv, ucg, ucv, f_in = _ffn_up_fwd(h1b, w["wug"], w["wuv"], w["cw"], w["cb"], S=S, tm=min(2 * tm, S),
                                         tn=FFN_TN)
    dpre2b, dh1, loss8, dg2, db2 = _down_ln2_loss(f_in, w["wd"], pre1, target.reshape(T, D), w["g1"], w["b1"],
                                                  w["g2"], w["b2"], tm=min(2 * tm, S))

    dug, duv, dcg, dcv = _ffn_bwd(dpre2b, w["wd"], ug, uv, ucg, ucv, w["cw"], S=S, tm=tm, tn=FFN_TN)
    g["g2"], g["b2"], g["loss"] = dg2, db2, loss8[0:1, 0:1]
    g["cw"] = jnp.concatenate([dcg[0:3], dcv[0:3]], axis=1)
    g["cb"] = jnp.concatenate([dcg[3:4], dcv[3:4]], axis=1)
    g["wd"] = _matmul(f_in, dpre2b, "tn", name="dw_down", out_dtype=BF, tm=1408, tn=1024, tk=1024)
    g["wugt"] = _matmul(dug, h1b, "tn", name="dw_up_g", out_dtype=BF, tm=1408, tn=1024, tk=1024)
    g["wuvt"] = _matmul(duv, h1b, "tn", name="dw_up_v", out_dtype=BF, tm=1408, tn=1024, tk=1024)
    dh1 = _matmul(dug, w["wugt"], "nn", name="dh1_g", c_in=dh1, tm=1024, tn=1024, tk=1408)
    dh1 = _matmul(duv, w["wuvt"], "nn", name="dh1_v", c_in=dh1, tm=1024, tn=1024, tk=1408)
    dx, dpre1b, dpt, dygb, dymb, dzg, dattn, dg1, db1 = _post_attn_bwd(
        dh1, pre1, pt, yg, ym, w["wgo"], w["wmo"], w["wout"], w["g1"], tm=min(2 * tm, S))
    g["g1"], g["b1"] = dg1, db1
    g["wout"] = _matmul(mix, dpre1b, "tn", name="dw_out", out_dtype=BF, tm=1024, tn=1024, tk=1024)
    g["wgo"] = _matmul(zg, dygb, "tn", name="dw_gla_o", out_dtype=BF, tm=1024, tn=1024, tk=1024)
    g["wmo"] = _matmul(attn, dymb, "tn", name="dw_mla_o", out_dtype=BF, tm=1024, tn=1024, tk=1024)
    dqc, dkc, dv = run("flash_bwd", _flash_bwd, qc, kc, v, attn, dattn, lse, nseq=nseq, S=S, tq=tq)
    dpm, g["wuq"], g["wukv"], g["gq"], g["gkv"] = _mla_prep_bwd(
        pm, pos, invf, w["gq"], w["gkv"], w["wuq"], w["wukv"], dqc, dkc, dv, tm=min(2 * tm, S))
    g["w_mt"] = _matmul(dpm, xb, "tn", name="dw_in_m", out_dtype=BF, tm=768, tn=1024, tk=1024)
    g["w_tt"] = _matmul(dpt, xb, "tn", name="dw_in_t", out_dtype=BF, tm=1024, tn=1024, tk=1024)
    dpg, g["wg"], g["bg"], g["gn"] = run("gla_bwd", _gla_bwd, pg, w["wg"], w["bg"], w["gn"], ltri, utri, o, states,
                                         dzg, nseq=nseq, S=S, tm=min(2 * tm, S))
    g["w_gt"] = _matmul(dpg, xb, "tn", name="dw_in_g", out_dtype=BF, tm=640, tn=1024, tk=1024)
    dx = run("dx", _matmul_sum, dx, [(dpg, w["w_gt"], 640), (dpm, w["w_mt"], 768)], name="dx_gm")
    dx = _matmul_sum(dx, [(dpt, w["w_tt"], 1024)], name="dx_t")
    return loss8[0, 0], dx.reshape(nseq, S, D), g


_IN_SPLITS = (512, 512, 1024, 16, 1024, 384, 256, 64, 1024, 1024)


def _w_in_to_groups(wt):
    offs = [0]
    for s in _IN_SPLITS:
        offs.append(offs[-1] + s)
    q, k, v, r, og, cq, ckv, kr, ga, gb = [wt[offs[i]:offs[i + 1]] for i in range(10)]
    z = lambda n: jnp.zeros((n, wt.shape[1]), wt.dtype)
    return (jnp.concatenate([q, k, v, og, r, z(112)], axis=0),
            jnp.concatenate([cq, kr, z(64), ckv], axis=0),
            jnp.concatenate([ga, gb], axis=0))


W_IN_BLOCK = sum(_IN_SPLITS) // NDEV
_KV_LATENT_ROW = sum(_IN_SPLITS[:6])
_W_IN_LO = 5
_W_IN_SPLIT = _W_IN_LO * W_IN_BLOCK - _KV_LATENT_ROW


def _w_in_rows_lo(g_g, g_m):
    q, k, v, og, r = g_g[0:512], g_g[512:1024], g_g[1024:2048], g_g[2048:3072], g_g[3072:3088]
    return jnp.concatenate([q, k, v, r, og, g_m[0:384], g_m[512:768]], axis=0)[:_W_IN_LO * W_IN_BLOCK]


def _w_in_rows_hi(g_m, g_t):
    return jnp.concatenate([g_m[512:768], g_m[384:448], g_t], axis=0)[_W_IN_SPLIT:]


def _uq_to_kernel(wuq):
    w3 = wuq.reshape(MQR, MH, NOPE + ROPE)
    rope = jnp.concatenate([w3[:, :, NOPE:], jnp.zeros((MQR, MH, 64), wuq.dtype)], axis=2)
    return jnp.concatenate([w3[:, :, :NOPE].reshape(MQR, MH * 128), rope.reshape(MQR, MH * 128)], axis=1)


def _uq_from_kernel(g):
    nope = g[:, :1024].reshape(MQR, MH, 128)
    rope = g[:, 1024:].reshape(MQR, MH, 128)[:, :, :ROPE]
    return jnp.concatenate([nope, rope], axis=2)


def _ukv_to_kernel(wukv):
    w3 = wukv.reshape(MKR, MH, NOPE + MV)
    return jnp.concatenate([w3[:, :, :NOPE].reshape(MKR, MH * 128), w3[:, :, NOPE:].reshape(MKR, MH * 128)], axis=1)


def _ukv_from_kernel(g):
    return jnp.concatenate([g[:, :1024].reshape(MKR, MH, 128), g[:, 1024:].reshape(MKR, MH, 128)], axis=2)


def _cols_gathered(a):
    return a.transpose(1, 0, 2).reshape(a.shape[1], NDEV * a.shape[2])


def _cols_scattered(a):
    R = a.shape[0]
    return a.reshape(R, NDEV, a.shape[1] // NDEV).transpose(1, 0, 2)


_SMALL = (("gla_b_gate", 512), ("gla_norm_g", 256), ("mla_q_norm_g", 384), ("mla_kv_norm_g", 256),
          ("ln1_g", 1024), ("ln1_b", 1024), ("conv_b", 5632), ("ln2_g", 1024), ("ln2_b", 1024))
_SMALL_ROWS = 88
_SMALL_USED = sum(sz for _, sz in _SMALL)


def _pack_small(d):
    flat = jnp.concatenate([d[n].reshape(-1) for n, _ in _SMALL] + ([d['loss'].reshape(-1)] if 'loss' in d else []))
    return jnp.pad(flat, (0, _SMALL_ROWS * 128 - flat.shape[0])).reshape(_SMALL_ROWS, 128)


def _unpack_small(a):
    flat = a.reshape(-1)
    out, off = {}, 0
    for n, sz in _SMALL:
        out[n] = flat[off:off + sz].reshape(1, sz)
        off += sz
    return out


_NAMES = ['w_in', 'gla_w_gate_up', 'gla_b_gate', 'gla_norm_g', 'w_gla_o', 'mla_q_norm_g', 'mla_w_uq',
          'mla_kv_norm_g', 'mla_w_ukv', 'w_mla_o', 'w_out', 'ln1_g', 'ln1_b', 'w_up', 'conv_w', 'conv_b',
          'w_down', 'ln2_g', 'ln2_b']
_SHARDED = ['w_in', 'w_up', 'w_down', 'w_gla_o', 'w_mla_o', 'w_out', 'mla_w_uq', 'mla_w_ukv', 'gla_w_gate_up',
            'conv_w']


def kernel(x, positions, w_in, gla_w_gate_up, gla_b_gate, gla_norm_g, w_gla_o, mla_q_norm_g, mla_w_uq, mla_kv_norm_g, mla_w_ukv, w_mla_o, w_out, ln1_g, ln1_b, w_up, conv_w, conv_b, w_down, ln2_g, ln2_b, loss_target, m_w_in, m_gla_w_gate_up, m_gla_b_gate, m_gla_norm_g, m_w_gla_o, m_mla_q_norm_g, m_mla_w_uq, m_mla_kv_norm_g, m_mla_w_ukv, m_w_mla_o, m_w_out, m_ln1_g, m_ln1_b, m_w_up, m_conv_w, m_conv_b, m_w_down, m_ln2_g, m_ln2_b, v_w_in, v_gla_w_gate_up, v_gla_b_gate, v_gla_norm_g, v_w_gla_o, v_mla_q_norm_g, v_mla_w_uq, v_mla_kv_norm_g, v_mla_w_ukv, v_w_mla_o, v_w_out, v_ln1_g, v_ln1_b, v_w_up, v_conv_w, v_conv_b, v_w_down, v_ln2_g, v_ln2_b):
    W = dict(w_in=w_in, gla_w_gate_up=gla_w_gate_up, gla_b_gate=gla_b_gate, gla_norm_g=gla_norm_g, w_gla_o=w_gla_o, mla_q_norm_g=mla_q_norm_g, mla_w_uq=mla_w_uq, mla_kv_norm_g=mla_kv_norm_g, mla_w_ukv=mla_w_ukv, w_mla_o=w_mla_o, w_out=w_out, ln1_g=ln1_g, ln1_b=ln1_b, w_up=w_up, conv_w=conv_w, conv_b=conv_b, w_down=w_down, ln2_g=ln2_g, ln2_b=ln2_b)
    M = dict(w_in=m_w_in, gla_w_gate_up=m_gla_w_gate_up, gla_b_gate=m_gla_b_gate, gla_norm_g=m_gla_norm_g, w_gla_o=m_w_gla_o, mla_q_norm_g=m_mla_q_norm_g, mla_w_uq=m_mla_w_uq, mla_kv_norm_g=m_mla_kv_norm_g, mla_w_ukv=m_mla_w_ukv, w_mla_o=m_w_mla_o, w_out=m_w_out, ln1_g=m_ln1_g, ln1_b=m_ln1_b, w_up=m_w_up, conv_w=m_conv_w, conv_b=m_conv_b, w_down=m_w_down, ln2_g=m_ln2_g, ln2_b=m_ln2_b)
    V = dict(w_in=v_w_in, gla_w_gate_up=v_gla_w_gate_up, gla_b_gate=v_gla_b_gate, gla_norm_g=v_gla_norm_g, w_gla_o=v_w_gla_o, mla_q_norm_g=v_mla_q_norm_g, mla_w_uq=v_mla_w_uq, mla_kv_norm_g=v_mla_kv_norm_g, mla_w_ukv=v_mla_w_ukv, w_mla_o=v_w_mla_o, w_out=v_w_out, ln1_g=v_ln1_g, ln1_b=v_ln1_b, w_up=v_w_up, conv_w=v_conv_w, conv_b=v_conv_b, w_down=v_w_down, ln2_g=v_ln2_g, ln2_b=v_ln2_b)

    tshard = lambda d, n: d[n][0].T
    shard = lambda n: (W[n][0].astype(BF), False)
    (w_in_t,) = _gather_two_level([tshard(W, 'w_in').astype(BF)], name="gather_w0")
    w_gt, w_mt, w_tt = _w_in_to_groups(w_in_t.reshape(NDEV * W_IN_BLOCK, D))
    kw = dict(
        w_gt=w_gt, w_mt=w_mt, w_tt=w_tt, bg=W['gla_b_gate'],
        gn=W['gla_norm_g'], gq=W['mla_q_norm_g'], gkv=W['mla_kv_norm_g'],
        g1=W['ln1_g'], b1=W['ln1_b'], g2=W['ln2_g'], b2=W['ln2_b'], cb=W['conv_b'],
    )
    received = {}

    def got_mixers(ex, w, g):
        w.update(wuq=_uq_to_kernel(_cols_gathered(ex[0])), wukv=_ukv_to_kernel(_cols_gathered(ex[1])),
                 wg=jnp.pad(_cols_gathered(ex[2]), ((0, 128 - GR), (0, 0))))

    def got_out_proj(ex, w, g):
        w.update(wgo=ex[0].reshape(D, D), wmo=ex[1].reshape(D, D), wout=ex[2].reshape(D, D))

    def got_up(ex, w, g):
        w_upt = ex[0].reshape(2 * DFF, D)
        w.update(wugt=w_upt[:DFF], wuvt=w_upt[DFF:], wug=w_upt[:DFF].T, wuv=w_upt[DFF:].T)

    def got_down(ex, w, g):
        w.update(wd=ex[0].reshape(DFF, D), cw=_cols_gathered(ex[1]))

    slab = lambda a, lo=0: ([(a.astype(BF), lo)], True)
    rows = lambda a, n=NDEV: a.reshape(n, a.shape[0] // n, a.shape[1])

    def keep(names):
        return lambda ex, w, g: received.update(zip(names, ex))

    def small_grads(g):
        return _pack_small(dict(gla_b_gate=g['bg'], gla_norm_g=g['gn'], mla_q_norm_g=g['gq'], mla_kv_norm_g=g['gkv'],
                                ln1_g=g['g1'], ln1_b=g['b1'], conv_b=g['cb'], ln2_g=g['g2'], ln2_b=g['b2'],
                                loss=g['loss']))

    hooks = {
        "proj_g": (lambda w, g: [shard('mla_w_uq'), shard('mla_w_ukv'), shard('gla_w_gate_up')], got_mixers),
        "gla_fwd": (lambda w, g: [shard('w_gla_o'), shard('w_mla_o'), shard('w_out')], got_out_proj),
        "flash_fwd": (lambda w, g: [(tshard(W, 'w_up').astype(BF), False)], got_up),
        "post_attn_fwd": (lambda w, g: [shard('w_down'), (W['conv_w'][0], False)], got_down),
        "flash_bwd": (lambda w, g: [slab(rows(g['wd'])),
                                    ([(rows(g['wugt'], 4), 0), (rows(g['wuvt'], 4), 4)], True),
                                    slab(rows(g['wout'])), slab(rows(g['wgo'])), slab(rows(g['wmo']))],
                      keep(['w_down', 'w_up', 'w_out', 'w_gla_o', 'w_mla_o'])),
        "gla_bwd": (lambda w, g: [slab(_uq_from_kernel(g['wuq']).transpose(1, 0, 2)),
                                  slab(_ukv_from_kernel(g['wukv']).transpose(1, 0, 2)),
                                  slab(rows(_w_in_rows_hi(g['w_mt'], g['w_tt']), NDEV - _W_IN_LO), _W_IN_LO)],
                    keep(['mla_w_uq', 'mla_w_ukv', 'w_in_hi'])),
        "dx": (lambda w, g: [slab(rows(_w_in_rows_lo(g['w_gt'], g['w_mt']), _W_IN_LO)),
                             ([(_cols_scattered(g['wg'][:GR]), 0)], True), ([(_cols_scattered(g['cw']), 0)], True),
                             (small_grads(g), False)],
               keep(['w_in_lo', 'gla_w_gate_up', 'conv_w', 'small'])),
    }

    _, grad_x, _ = _local_step(x, positions, loss_target, kw, hooks)

    grads, deltas, new_m, new_v = {}, {}, {}, {}
    small_parts = received['small']
    loss = jnp.sum(small_parts.reshape(NDEV, -1)[:, _SMALL_USED])
    me = 4 * lax.axis_index("x") + 2 * lax.axis_index("y") + lax.axis_index("c")
    received['w_in'] = jnp.where(me >= _W_IN_LO, received['w_in_hi'], received['w_in_lo'])
    for n in _SHARDED:
        shp = W[n].shape
        if n in ('w_in', 'w_up'):
            out = _adamw(received[n], tshard(W, n), tshard(M, n), tshard(V, n), name="adamw_" + n)
            grads[n], deltas[n], new_m[n], new_v[n] = [t.T.reshape(shp) for t in out]
            continue
        out = _adamw(received[n], W[n][0], M[n][0], V[n][0], name="adamw_" + n)
        grads[n], deltas[n], new_m[n], new_v[n] = [t.reshape(shp) for t in out]
    out = _adamw(small_parts, _pack_small(W), _pack_small(M), _pack_small(V), name="adamw_small")
    for dst, packed in zip((grads, deltas, new_m, new_v), out):
        dst.update(_unpack_small(packed))

    return (loss, grad_x, *[grads[n] for n in _NAMES], *[deltas[n] for n in _NAMES],
            *[new_m[n] for n in _NAMES], *[new_v[n] for n in _NAMES])
```

```python
import functools

import jax
import jax.numpy as jnp
from jax import lax
from jax.experimental import pallas as pl
from jax.experimental.pallas import tpu as pltpu

F32 = jnp.float32
BF = jnp.bfloat16

D = 1024
GH, GDK, GDV, GR, GTAU, GC = 4, 128, 256, 16, 16.0, 64
MH, MQR, MKR, NOPE, ROPE, MV = 8, 384, 256, 128, 64, 128
THETA = 10000.0
DFF = 2816
ALPHA = 2.0 ** 0.25
LN_EPS = 1e-5
RMS_EPS = 1e-6
NDEV = 8
ADAM_LR, ADAM_B1, ADAM_B2, ADAM_EPS, ADAM_WD, ADAM_STEP = 0.001, 0.9, 0.999, 1e-08, 0.01, 10

PG_W = 3200
PM_W = 768
PT_W = 2048
NEG = -1e30
MESH_ID = pl.DeviceIdType.MESH
VMEM_MB = 1024 * 1024


V7X_VMEM_LIMIT_MB = 48
TOKEN_TM = 256
FLASH_TQ = 512
FFN_TN = 1408


def _params(sem):
    return pltpu.CompilerParams(dimension_semantics=sem, vmem_limit_bytes=V7X_VMEM_LIMIT_MB * VMEM_MB)


def _dot(a, b):
    return lax.dot_general(a, b, (((1,), (0,)), ((), ())), preferred_element_type=F32)


def _dot_nt(a, b):
    return lax.dot_general(a, b, (((1,), (1,)), ((), ())), preferred_element_type=F32)


def _dot_tn(a, b):
    return lax.dot_general(a, b, (((0,), (0,)), ((), ())), preferred_element_type=F32)


def _iota(shape, dim):
    return lax.broadcasted_iota(jnp.int32, shape, dim)


FLASH_HP = 2
FLASH_HP_FWD = 4
QK_SCALE = (NOPE + ROPE) ** -0.5
LOG2E = 1.4426950408889634
QK_SCALE_LOG2 = QK_SCALE * LOG2E


def _sigmoid(x):
    return 0.5 * jnp.tanh(0.5 * x) + 0.5


def _tri_mm(tri_bf, x):
    hi = x.astype(BF)
    r1 = x - hi.astype(F32)
    mid = r1.astype(BF)
    lo = (r1 - mid.astype(F32)).astype(BF)
    return _dot(tri_bf, hi) + _dot(tri_bf, mid) + _dot(tri_bf, lo)


def _matmul(a, b, mode, *, name, c_in=None, out_dtype=F32, tm=512, tn=512, tk=512, ride=None, emit_a=False):
    if mode == "nn":
        (M, K), (_, N) = a.shape, b.shape
    elif mode == "nt":
        (M, K), (N, _) = a.shape, b.shape
    else:
        (K, M), (_, N) = a.shape, b.shape
    tm, tn, tk = min(tm, M), min(tn, N), min(tk, K)
    assert M % tm == 0 and N % tn == 0 and K % tk == 0, (name, M, N, K, tm, tn, tk)
    nk = K // tk
    assert not emit_a or (nk == 1 and mode != "tn" and c_in is None and ride is None)
    dot = {"nn": _dot, "nt": _dot_nt, "tn": _dot_tn}[mode]

    def body(*refs):
        if emit_a:
            a_ref, b_ref, o_ref, xa_ref, acc_ref = refs
        elif c_in is None:
            a_ref, b_ref, o_ref, acc_ref = refs
        else:
            a_ref, b_ref, c_ref, o_ref, acc_ref = refs
        k = pl.program_id(2)

        @pl.when(k == 0)
        def _():
            if c_in is None:
                acc_ref[...] = jnp.zeros_like(acc_ref)
            else:
                acc_ref[...] = c_ref[...].astype(F32)

        if emit_a:
            @pl.when(pl.program_id(1) == 0)
            def _():
                xa_ref[...] = a_ref[...].astype(BF)

        acc_ref[...] += dot(a_ref[...].astype(BF), b_ref[...].astype(BF))

        @pl.when(k == nk - 1)
        def _():
            o_ref[...] = acc_ref[...].astype(out_dtype)

    if mode == "tn":
        a_spec = pl.BlockSpec((tk, tm), lambda i, j, k: (k, i))
    else:
        a_spec = pl.BlockSpec((tm, tk), lambda i, j, k: (i, k))
    if mode == "nt":
        b_spec = pl.BlockSpec((tn, tk), lambda i, j, k: (j, k))
    else:
        b_spec = pl.BlockSpec((tk, tn), lambda i, j, k: (k, j))
    in_specs = [a_spec, b_spec]
    args = [a, b]
    if c_in is not None:
        in_specs.append(pl.BlockSpec((tm, tn), lambda i, j, k: (i, j)))
        args.append(c_in)
    out_shape = (jax.ShapeDtypeStruct((M, N), out_dtype),)
    out_specs = (pl.BlockSpec((tm, tn), lambda i, j, k: (i, j)),)
    if emit_a:
        out_shape += (jax.ShapeDtypeStruct((M, K), BF),)
        out_specs += (pl.BlockSpec((tm, tk), lambda i, j, k: (i, k)),)
    res = _call(
        body, name=name, out_shape=out_shape, grid=(M // tm, N // tn, nk), in_specs=in_specs, out_specs=out_specs,
        scratch_shapes=[pltpu.VMEM((tm, tn), F32)],
        sem=("parallel", "arbitrary", "arbitrary"), args=args, ride=ride)
    if emit_a:
        return res[0], res[1]
    return res[0] if ride is None else (res[0][0], res[1])


def _matmul_sum(c_in, parts, *, name, tm=1024, ride=None):
    M, N = c_in.shape
    tm = min(tm, M)
    n_p = len(parts)
    counts = [a.shape[1] // tk for a, _, tk in parts]
    starts = [sum(counts[:p]) for p in range(n_p)]
    nk = sum(counts)

    def body(*refs):
        a_refs, w_refs = refs[:n_p], refs[n_p:2 * n_p]
        c_ref, o_ref, acc_ref = refs[2 * n_p:]
        k = pl.program_id(1)

        @pl.when(k == 0)
        def _():
            acc_ref[...] = c_ref[...]

        for p in range(n_p):
            @pl.when(jnp.logical_and(k >= starts[p], k < starts[p] + counts[p]))
            def _(p=p):
                acc_ref[...] += _dot(a_refs[p][...].astype(BF), w_refs[p][...].astype(BF))

        @pl.when(k == nk - 1)
        def _():
            o_ref[...] = acc_ref[...]

    def kidx(p):
        return lambda k: jnp.clip(k - starts[p], 0, counts[p] - 1)

    in_specs = [pl.BlockSpec((tm, tk), lambda i, k, f=kidx(p): (i, f(k))) for p, (_, _, tk) in enumerate(parts)]
    in_specs += [pl.BlockSpec((tk, N), lambda i, k, f=kidx(p): (f(k), 0)) for p, (_, _, tk) in enumerate(parts)]
    in_specs.append(pl.BlockSpec((tm, N), lambda i, k: (i, 0)))
    res = _call(
        body, name=name, out_shape=(jax.ShapeDtypeStruct((M, N), F32),), grid=(M // tm, nk),
        in_specs=in_specs, out_specs=(pl.BlockSpec((tm, N), lambda i, k: (i, 0)),),
        scratch_shapes=[pltpu.VMEM((tm, N), F32)], sem=("parallel", "arbitrary"),
        args=[a for a, _, _ in parts] + [w for _, w, _ in parts] + [c_in], ride=ride)
    return res[0] if ride is None else (res[0][0], res[1])


def _gla_gate(pg_ref, rows, wg_ref, bg_ref):
    r = pg_ref[rows, 3072:3200].astype(BF)
    logit = _dot(r, wg_ref[...]) + bg_ref[...]
    la = (jnp.minimum(logit, 0.0) - jnp.log(1.0 + jnp.exp(-jnp.abs(logit)))) * (1.0 / GTAU)
    return r, logit, la


def _gla_fwd(pg, wg, bg, gn, ltri, *, nseq, S, tm, ride=None):
    T = pg.shape[0]
    nb, nc = S // tm, tm // GC
    qscale = GDK ** -0.5

    def body(pg_ref, wg_ref, bg_ref, gn_ref, l_ref, o_ref, zg_ref, st_ref, st_scr):
        @pl.when(pl.program_id(1) == 0)
        def _():
            st_scr[...] = jnp.zeros_like(st_scr)

        ltri_v = l_ref[...]
        causal = _iota((GC, GC), 0) >= _iota((GC, GC), 1)
        last_row = _iota((GC, GDK), 0) == GC - 1
        g = gn_ref[...]

        def chunk(c, carry):
            rows = pl.ds(pl.multiple_of(c * GC, GC), GC)
            _, _, la = _gla_gate(pg_ref, rows, wg_ref, bg_ref)
            b = _tri_mm(ltri_v, la)
            hs = range(GH)
            v, q_in, k_st, dec, st, a_raw, o_st, kv = [], [], [], [], [], [], [], []
            for h in hs:
                q = pg_ref[rows, h * GDK:(h + 1) * GDK]
                k = pg_ref[rows, 512 + h * GDK:512 + (h + 1) * GDK]
                v.append(pg_ref[rows, 1024 + h * GDV:1024 + (h + 1) * GDV].astype(BF))
                bh = b[:, h * GDK:(h + 1) * GDK]
                bl = jnp.sum(jnp.where(last_row, bh, 0.0), axis=0, keepdims=True)
                q_in.append((q * (qscale * jnp.exp(bh))).astype(BF))
                k_in = (k * jnp.exp(-bh)).astype(BF)
                k_st.append((k * jnp.exp(bl - bh)).astype(BF))
                dec.append(jnp.exp(bl))
                st.append(st_scr[h])
                st_ref[c, h] = st[h]
                a_raw.append(_dot_nt(q_in[h], k_in))
            for h in hs:
                o_st.append(_dot_nt(q_in[h], st[h].astype(BF)))
                kv.append(_dot_tn(v[h], k_st[h]))
            att = [jnp.where(causal, a_raw[h], 0.0).astype(BF) for h in hs]
            o = [_dot(att[h], v[h]) + o_st[h] for h in hs]
            for h in hs:
                st_scr[h] = st[h] * dec[h] + kv[h]
                og = pg_ref[rows, 2048 + h * GDV:2048 + (h + 1) * GDV]
                rstd = lax.rsqrt(jnp.mean(o[h] * o[h], axis=-1, keepdims=True) + RMS_EPS)
                o_ref[rows, h * GDV:(h + 1) * GDV] = o[h]
                zg_ref[rows, h * GDV:(h + 1) * GDV] = (o[h] * rstd * g * (og * _sigmoid(og))).astype(BF)
            return carry

        lax.fori_loop(0, nc, chunk, 0, unroll=True)

    full = lambda shp: pl.BlockSpec(shp, lambda b_, i: (0,) * len(shp))
    return _call(
        body, name="gla_fwd", ride=ride, sem=("parallel", "arbitrary"), args=(pg, wg, bg, gn, ltri),
        out_shape=(jax.ShapeDtypeStruct((T, GH * GDV), F32),
                   jax.ShapeDtypeStruct((T, GH * GDV), BF),
                   jax.ShapeDtypeStruct((T // GC, GH, GDV, GDK), F32)),
        grid=(nseq, nb),
        in_specs=[pl.BlockSpec((tm, PG_W), lambda b_, i: (b_ * nb + i, 0)),
                  full((128, 512)), full((1, 512)), full((1, GDV)), full((GC, GC))],
        out_specs=(pl.BlockSpec((tm, GH * GDV), lambda b_, i: (b_ * nb + i, 0)),
                   pl.BlockSpec((tm, GH * GDV), lambda b_, i: (b_ * nb + i, 0)),
                   pl.BlockSpec((nc, GH, GDV, GDK), lambda b_, i: (b_ * nb + i, 0, 0, 0))),
        scratch_shapes=[pltpu.VMEM((GH, GDV, GDK), F32)])


def _gla_bwd(pg, wg, bg, gn, ltri, utri, o, states, dzg, *, nseq, S, tm, ride=None):
    T = pg.shape[0]
    nb, nc = S // tm, tm // GC
    qscale = GDK ** -0.5

    def body(pg_ref, wg_ref, bg_ref, gn_ref, l_ref, u_ref, o_ref, st_ref, dzg_ref,
             dpg_ref, dwg_ref, dbg_ref, dgn_ref, dst_scr):
        first = jnp.logical_and(pl.program_id(0) == 0, pl.program_id(1) == 0)

        @pl.when(first)
        def _():
            dwg_ref[...] = jnp.zeros_like(dwg_ref)
            dbg_ref[...] = jnp.zeros_like(dbg_ref)
            dgn_ref[...] = jnp.zeros_like(dgn_ref)

        @pl.when(pl.program_id(1) == 0)
        def _():
            dst_scr[...] = jnp.zeros_like(dst_scr)

        ltri_v = l_ref[...]
        utri_v = u_ref[...]
        causal = _iota((GC, GC), 0) >= _iota((GC, GC), 1)
        last_row = _iota((GC, GDK), 0) == GC - 1
        g = gn_ref[...]

        def chunk(cc, carry):
            c = nc - 1 - cc
            rows = pl.ds(pl.multiple_of(c * GC, GC), GC)
            r, logit, la = _gla_gate(pg_ref, rows, wg_ref, bg_ref)
            b = _tri_mm(ltri_v, la)
            hs = range(GH)
            L = lambda: [None] * GH
            vb, eb, enb, ek, dec, q_in, k_in, k_st, q_inb, k_inb, st, dst, dob = (L() for _ in range(13))
            a_raw, da_raw, dq_st, dks, dv_st, dst_new, dbs, dgn = (L() for _ in range(8))
            for h in hs:
                q = pg_ref[rows, h * GDK:(h + 1) * GDK]
                k = pg_ref[rows, 512 + h * GDK:512 + (h + 1) * GDK]
                vb[h] = pg_ref[rows, 1024 + h * GDV:1024 + (h + 1) * GDV].astype(BF)
                og = pg_ref[rows, 2048 + h * GDV:2048 + (h + 1) * GDV]
                oh = o_ref[rows, h * GDV:(h + 1) * GDV]
                dz = dzg_ref[rows, h * GDV:(h + 1) * GDV].astype(F32)
                bh = b[:, h * GDK:(h + 1) * GDK]
                bl = jnp.sum(jnp.where(last_row, bh, 0.0), axis=0, keepdims=True)
                eb[h] = qscale * jnp.exp(bh)
                enb[h] = jnp.exp(-bh)
                ek[h] = jnp.exp(bl - bh)
                dec[h] = jnp.exp(bl)
                q_in[h], k_in[h], k_st[h] = q * eb[h], k * enb[h], k * ek[h]
                q_inb[h], k_inb[h] = q_in[h].astype(BF), k_in[h].astype(BF)
                st[h] = st_ref[c, h]
                dst[h] = dst_scr[h]
                rstd = lax.rsqrt(jnp.mean(oh * oh, axis=-1, keepdims=True) + RMS_EPS)
                ohat = oh * rstd
                sg = _sigmoid(og)
                don = dz * (og * sg)
                dpg_ref[rows, 2048 + h * GDV:2048 + (h + 1) * GDV] = (
                    dz * (ohat * g) * (sg * (1.0 + og * (1.0 - sg)))).astype(BF)
                dgn[h] = jnp.sum(don * ohat, axis=0, keepdims=True)
                gd = don * g
                dob[h] = (rstd * (gd - ohat * jnp.mean(gd * ohat, axis=-1, keepdims=True))).astype(BF)
                a_raw[h] = _dot_nt(q_inb[h], k_inb[h])
                da_raw[h] = _dot_nt(dob[h], vb[h])
            dgn_ref[...] += dgn[0] + dgn[1] + dgn[2] + dgn[3]
            for h in hs:
                dstb = dst[h].astype(BF)
                dq_st[h] = _dot(dob[h], st[h].astype(BF))
                dks[h] = _dot(vb[h], dstb)
                dv_st[h] = _dot_nt(k_st[h].astype(BF), dstb)
                dst_new[h] = _dot_tn(dob[h], q_inb[h])
            att = [jnp.where(causal, a_raw[h], 0.0).astype(BF) for h in hs]
            da = [jnp.where(causal, da_raw[h], 0.0).astype(BF) for h in hs]
            dqi = [_dot(da[h], k_inb[h]) + dq_st[h] for h in hs]
            dki = [_dot_tn(da[h], q_inb[h]) for h in hs]
            dv = [_dot_tn(att[h], dob[h]) + dv_st[h] for h in hs]
            for h in hs:
                dd = jnp.sum(dst[h] * st[h], axis=0, keepdims=True)
                dst_scr[h] = dst[h] * dec[h] + dst_new[h]
                kk = dks[h] * k_st[h]
                dbl = jnp.sum(kk, axis=0, keepdims=True) + dd * dec[h]
                db = dqi[h] * q_in[h] - dki[h] * k_in[h] - kk
                dbs[h] = db + jnp.where(last_row, dbl, 0.0)
                dpg_ref[rows, h * GDK:(h + 1) * GDK] = (dqi[h] * eb[h]).astype(BF)
                dpg_ref[rows, 512 + h * GDK:512 + (h + 1) * GDK] = (dki[h] * enb[h] + dks[h] * ek[h]).astype(BF)
                dpg_ref[rows, 1024 + h * GDV:1024 + (h + 1) * GDV] = dv[h].astype(BF)
            dla = _tri_mm(utri_v, jnp.concatenate(dbs, axis=1))
            dlogit = dla * (1.0 / GTAU) * _sigmoid(-logit)
            dlb = dlogit.astype(BF)
            dpg_ref[rows, 3072:3200] = _dot_nt(dlb, wg_ref[...]).astype(BF)
            dwg_ref[...] += _dot_tn(r, dlb)
            dbg_ref[...] += jnp.sum(dlogit, axis=0, keepdims=True)
            return carry

        lax.fori_loop(0, nc, chunk, 0, unroll=True)

    full = lambda shp: pl.BlockSpec(shp, lambda b_, i: (0,) * len(shp))
    rev = lambda b_, i: (b_ * nb + nb - 1 - i, 0)
    return _call(
        body, name="gla_bwd", ride=ride, sem=("arbitrary", "arbitrary"),
        args=(pg, wg, bg, gn, ltri, utri, o, states, dzg),
        out_shape=(jax.ShapeDtypeStruct((T, PG_W), BF),
                   jax.ShapeDtypeStruct((128, 512), F32),
                   jax.ShapeDtypeStruct((1, 512), F32),
                   jax.ShapeDtypeStruct((1, GDV), F32)),
        grid=(nseq, nb),
        in_specs=[pl.BlockSpec((tm, PG_W), rev),
                  full((128, 512)), full((1, 512)), full((1, GDV)), full((GC, GC)), full((GC, GC)),
                  pl.BlockSpec((tm, GH * GDV), rev),
                  pl.BlockSpec((nc, GH, GDV, GDK), lambda b_, i: (b_ * nb + nb - 1 - i, 0, 0, 0)),
                  pl.BlockSpec((tm, GH * GDV), rev)],
        out_specs=(pl.BlockSpec((tm, PG_W), rev), full((128, 512)), full((1, 512)), full((1, GDV))),
        scratch_shapes=[pltpu.VMEM((GH, GDV, GDK), F32)])


def _rope_tables(pos, invf):
    ang = pos.astype(F32) * invf
    lane = _iota(ang.shape, 1)
    sin = jnp.sin(ang)
    ssin = jnp.where(lane < 32, -sin, jnp.where(lane < 64, sin, 0.0))
    return jnp.cos(ang), ssin, lane


def _rope(x, cos, ssin, lane, sign):
    rot = jnp.where(lane < 32, pltpu.roll(x, 96, 1), pltpu.roll(x, 32, 1))
    return x * cos + sign * (rot * ssin)


def _rms_fwd(x, g):
    rstd = lax.rsqrt(jnp.mean(x * x, axis=-1, keepdims=True) + RMS_EPS)
    return x * rstd * g, x * rstd, rstd


def _rms_bwd(dy, xhat, rstd, g):
    gd = dy * g
    return rstd * (gd - xhat * jnp.mean(gd * xhat, axis=-1, keepdims=True)), jnp.sum(dy * xhat, axis=0, keepdims=True)


def _mla_prep_fwd(pm, pos, invf, gq, gkv, wuq, wukv, *, tm):
    T = pm.shape[0]

    def body(pm_ref, pos_ref, invf_ref, gq_ref, gkv_ref, wuq_ref, wukv_ref, qc_ref, kc_ref, v_ref):
        cos, ssin, lane = _rope_tables(pos_ref[...], invf_ref[...])
        cq, _, _ = _rms_fwd(pm_ref[:, 0:MQR], gq_ref[...])
        ckv, _, _ = _rms_fwd(pm_ref[:, 512:768], gkv_ref[...])
        qf = _dot(cq.astype(BF), wuq_ref[...])
        kvf = _dot(ckv.astype(BF), wukv_ref[...])
        kr = _rope(pm_ref[:, 384:512], cos, ssin, lane, 1.0).astype(BF)
        for h in range(MH):
            qc_ref[:, 256 * h:256 * h + 128] = (QK_SCALE_LOG2 * qf[:, 128 * h:128 * h + 128]).astype(BF)
            qr = qf[:, 1024 + 128 * h:1024 + 128 * h + 128]
            qc_ref[:, 256 * h + 128:256 * h + 256] = (QK_SCALE_LOG2 * _rope(qr, cos, ssin, lane, 1.0)).astype(BF)
            kc_ref[:, 256 * h:256 * h + 128] = kvf[:, 128 * h:128 * h + 128].astype(BF)
            kc_ref[:, 256 * h + 128:256 * h + 256] = kr
        v_ref[...] = kvf[:, 1024:2048].astype(BF)

    full = lambda shp: pl.BlockSpec(shp, lambda i: (0,) * len(shp))
    row = lambda w: pl.BlockSpec((tm, w), lambda i: (i, 0))
    return pl.pallas_call(
        body, name="mla_prep_fwd",
        out_shape=(jax.ShapeDtypeStruct((T, MH * 256), BF), jax.ShapeDtypeStruct((T, MH * 256), BF),
                   jax.ShapeDtypeStruct((T, MH * MV), BF)),
        grid=(T // tm,),
        in_specs=[row(PM_W), row(1), full((1, 128)), full((1, MQR)), full((1, MKR)),
                  full((MQR, 2048)), full((MKR, 2048))],
        out_specs=(row(MH * 256), row(MH * 256), row(MH * MV)),
        compiler_params=_params(("parallel",)),
    )(pm, pos, invf, gq, gkv, wuq, wukv)


def _mla_prep_bwd(pm, pos, invf, gq, gkv, wuq, wukv, dqc, dkc, dv, *, tm):
    T = pm.shape[0]

    def body(pm_ref, pos_ref, invf_ref, gq_ref, gkv_ref, wuq_ref, wukv_ref, dqc_ref, dkc_ref, dv_ref,
             dpm_ref, dwuq_ref, dwukv_ref, dgq_ref, dgkv_ref):
        @pl.when(pl.program_id(0) == 0)
        def _():
            dwuq_ref[...] = jnp.zeros_like(dwuq_ref)
            dwukv_ref[...] = jnp.zeros_like(dwukv_ref)
            dgq_ref[...] = jnp.zeros_like(dgq_ref)
            dgkv_ref[...] = jnp.zeros_like(dgkv_ref)

        cos, ssin, lane = _rope_tables(pos_ref[...], invf_ref[...])
        cq, cqh, cq_rstd = _rms_fwd(pm_ref[:, 0:MQR], gq_ref[...])
        ckv, ckvh, ckv_rstd = _rms_fwd(pm_ref[:, 512:768], gkv_ref[...])
        dqn, dqr, dkn = [], [], []
        dkr = jnp.zeros((tm, 128), F32)
        for h in range(MH):
            dqn.append(dqc_ref[:, 256 * h:256 * h + 128].astype(BF))
            dqr.append(_rope(dqc_ref[:, 256 * h + 128:256 * h + 256], cos, ssin, lane, -1.0).astype(BF))
            dkn.append(dkc_ref[:, 256 * h:256 * h + 128].astype(BF))
            dkr = dkr + dkc_ref[:, 256 * h + 128:256 * h + 256]
        dqf = jnp.concatenate(dqn + dqr, axis=1)
        dkvf = jnp.concatenate(dkn + [dv_ref[...].astype(BF)], axis=1)
        dwuq_ref[...] += _dot_tn(cq.astype(BF), dqf)
        dwukv_ref[...] += _dot_tn(ckv.astype(BF), dkvf)
        dcq, dgq = _rms_bwd(_dot_nt(dqf, wuq_ref[...]), cqh, cq_rstd, gq_ref[...])
        dckv, dgkv = _rms_bwd(_dot_nt(dkvf, wukv_ref[...]), ckvh, ckv_rstd, gkv_ref[...])
        dgq_ref[...] += dgq
        dgkv_ref[...] += dgkv
        dpm_ref[:, 0:MQR] = dcq.astype(BF)
        dpm_ref[:, 384:512] = _rope(dkr, cos, ssin, lane, -1.0).astype(BF)
        dpm_ref[:, 512:768] = dckv.astype(BF)

    full = lambda shp: pl.BlockSpec(shp, lambda i: (0,) * len(shp))
    row = lambda w: pl.BlockSpec((tm, w), lambda i: (i, 0))
    return pl.pallas_call(
        body, name="mla_prep_bwd",
        out_shape=(jax.ShapeDtypeStruct((T, PM_W), BF), jax.ShapeDtypeStruct((MQR, 2048), F32),
                   jax.ShapeDtypeStruct((MKR, 2048), F32), jax.ShapeDtypeStruct((1, MQR), F32),
                   jax.ShapeDtypeStruct((1, MKR), F32)),
        grid=(T // tm,),
        in_specs=[row(PM_W), row(1), full((1, 128)), full((1, MQR)), full((1, MKR)),
                  full((MQR, 2048)), full((MKR, 2048)), row(MH * 256), row(MH * 256), row(MH * MV)],
        out_specs=(row(PM_W), full((MQR, 2048)), full((MKR, 2048)), full((1, MQR)), full((1, MKR))),
        compiler_params=_params(("arbitrary",)),
    )(pm, pos, invf, gq, gkv, wuq, wukv, dqc, dkc, dv)


def _flash_fwd(qc, kc, v, *, nseq, S, tq, ride=None):
    T = qc.shape[0]
    nq = S // tq
    hp = FLASH_HP_FWD

    def body(q_ref, k_ref, v_ref, o_ref, lse_ref):
        i = pl.program_id(2)
        causal = _iota((tq, tq), 0) >= _iota((tq, tq), 1)

        def step(j, carry, masked):
            rows = pl.ds(pl.multiple_of(j * tq, tq), tq)
            hs = range(hp)
            s = [_dot_nt(q_ref[:, 256 * hh:256 * hh + 256], k_ref[rows, 256 * hh:256 * hh + 256]) for hh in hs]
            p, stats = [], []
            for hh in hs:
                m, l, _ = carry[hh]
                sh = jnp.where(causal, s[hh], NEG) if masked else s[hh]
                m_new = jnp.maximum(m, jnp.max(sh, axis=-1, keepdims=True))
                ph = jnp.exp2(sh - m_new)
                a = jnp.exp2(m - m_new)
                stats.append((m_new, a * l + jnp.sum(ph, axis=-1, keepdims=True), a))
                p.append(ph.astype(BF))
            pv = [_dot(p[hh], v_ref[rows, MV * hh:MV * hh + MV]) for hh in hs]
            return tuple((stats[hh][0], stats[hh][1], stats[hh][2] * carry[hh][2] + pv[hh]) for hh in hs)

        init = ((jnp.full((tq, 1), NEG, F32), jnp.zeros((tq, 1), F32), jnp.zeros((tq, MV), F32)),) * hp
        carry = lax.fori_loop(0, i, lambda j, c: step(j, c, False), init)
        for hh, (m, l, acc) in enumerate(step(i, carry, True)):
            o_ref[:, MV * hh:MV * hh + MV] = (acc / l).astype(BF)
            lse_ref[:, 128 * hh:128 * hh + 128] = jnp.broadcast_to(m + jnp.log2(l), (tq, 128))

    return _call(
        body, name="flash_fwd", ride=ride, sem=("parallel", "parallel", "arbitrary"), args=(qc, kc, v),
        out_shape=(jax.ShapeDtypeStruct((T, MH * MV), BF), jax.ShapeDtypeStruct((T, MH * 128), F32)),
        grid=(nseq, MH // hp, nq),
        in_specs=[pl.BlockSpec((tq, 256 * hp), lambda b_, h, i: (b_ * nq + i, h)),
                  pl.BlockSpec((S, 256 * hp), lambda b_, h, i: (b_, h)),
                  pl.BlockSpec((S, MV * hp), lambda b_, h, i: (b_, h))],
        out_specs=(pl.BlockSpec((tq, MV * hp), lambda b_, h, i: (b_ * nq + i, h)),
                   pl.BlockSpec((tq, 128 * hp), lambda b_, h, i: (b_ * nq + i, h))))


def _flash_bwd(qc, kc, v, o, do, lse, *, nseq, S, tq, ride=None):
    T = qc.shape[0]
    nq = S // tq

    def body(q_ref, k_ref, v_ref, o_ref, do_ref, lse_ref, dq_ref, dk_ref, dv_ref, dq_scr, delta_scr):
        j = pl.program_id(2)

        @pl.when(j == 0)
        def _():
            dq_scr[...] = jnp.zeros_like(dq_scr)
            for hh in range(FLASH_HP):
                od = o_ref[:, MV * hh:MV * hh + MV].astype(F32) * do_ref[:, MV * hh:MV * hh + MV].astype(F32)
                delta_scr[:, 128 * hh:128 * hh + 128] = jnp.broadcast_to(jnp.sum(od, axis=-1, keepdims=True), (S, 128))

        causal = _iota((tq, tq), 0) >= _iota((tq, tq), 1)

        def step(i, carry, masked):
            rows = pl.ds(pl.multiple_of(i * tq, tq), tq)
            hs = range(FLASH_HP)
            qs = [slice(256 * hh, 256 * hh + 256) for hh in hs]
            vs = [slice(MV * hh, MV * hh + MV) for hh in hs]
            ls = [slice(128 * hh, 128 * hh + 1) for hh in hs]
            s = [_dot_nt(q_ref[rows, qs[hh]], k_ref[:, qs[hh]]) for hh in hs]
            dp = [_dot_nt(do_ref[rows, vs[hh]], v_ref[:, vs[hh]]) for hh in hs]
            pb, ds = [], []
            for hh in hs:
                p = jnp.exp2(s[hh] - lse_ref[rows, ls[hh]])
                if masked:
                    p = jnp.where(causal, p, 0.0)
                pb.append(p.astype(BF))
                ds.append((p * (dp[hh] - delta_scr[rows, ls[hh]])).astype(BF))
            dv = [carry[hh][1] + _dot_tn(pb[hh], do_ref[rows, vs[hh]]) for hh in hs]
            dk = [carry[hh][0] + _dot_tn(ds[hh], q_ref[rows, qs[hh]]) for hh in hs]
            for hh in hs:
                dq_scr[rows, qs[hh]] += _dot(ds[hh], k_ref[:, qs[hh]])
            return tuple((dk[hh], dv[hh]) for hh in hs)

        init = ((jnp.zeros((tq, 256), F32), jnp.zeros((tq, MV), F32)),) * FLASH_HP
        carry = step(j, init, True)
        carry = lax.fori_loop(j + 1, nq, lambda i, c: step(i, c, False), carry)
        for hh, (dk, dv) in enumerate(carry):
            dk_ref[:, 256 * hh:256 * hh + 256] = dk * (1.0 / LOG2E)
            dv_ref[:, MV * hh:MV * hh + MV] = dv

        @pl.when(j == nq - 1)
        def _():
            dq_ref[...] = dq_scr[...] * QK_SCALE

    hp = FLASH_HP
    seq = lambda w: pl.BlockSpec((S, w * hp), lambda b_, h, j: (b_, h))
    blk = lambda w: pl.BlockSpec((tq, w * hp), lambda b_, h, j: (b_ * nq + j, h))
    return _call(
        body, name="flash_bwd", ride=ride, sem=("parallel", "parallel", "arbitrary"), args=(qc, kc, v, o, do, lse),
        out_shape=(jax.ShapeDtypeStruct((T, MH * 256), F32), jax.ShapeDtypeStruct((T, MH * 256), F32),
                   jax.ShapeDtypeStruct((T, MH * MV), F32)),
        grid=(nseq, MH // hp, nq),
        in_specs=[seq(256), blk(256), blk(MV), seq(MV), seq(MV), seq(128)],
        out_specs=(seq(256), blk(256), blk(MV)),
        scratch_shapes=[pltpu.VMEM((S, 256 * hp), F32), pltpu.VMEM((S, 128 * hp), F32)])


def _ln_fwd(pre, g, b):
    mu = jnp.mean(pre, axis=-1, keepdims=True)
    xc = pre - mu
    rstd = lax.rsqrt(jnp.mean(xc * xc, axis=-1, keepdims=True) + LN_EPS)
    xhat = xc * rstd
    return xhat * g + b, xhat, rstd


def _ln_bwd(dy, xhat, rstd, g):
    dxh = dy * g
    dx = rstd * (dxh - jnp.mean(dxh, axis=-1, keepdims=True) - xhat * jnp.mean(dxh * xhat, axis=-1, keepdims=True))
    return dx, jnp.sum(dy * xhat, axis=0, keepdims=True), jnp.sum(dy, axis=0, keepdims=True)


def _post_attn_fwd(zg, attn, pt, x, wgo, wmo, wout, g1, b1, *, tm, ride=None):
    T = x.shape[0]

    def body(zg_ref, at_ref, pt_ref, x_ref, wgo_ref, wmo_ref, wout_ref, g_ref, b_ref,
             yg_ref, ym_ref, mix_ref, pre_ref, hb_ref):
        yg = _dot(zg_ref[...], wgo_ref[...])
        ym = _dot(at_ref[...], wmo_ref[...])
        mix = (_sigmoid(pt_ref[:, 0:D].astype(F32)) * yg + _sigmoid(pt_ref[:, D:2 * D].astype(F32)) * ym).astype(BF)
        pre = ALPHA * x_ref[...] + _dot(mix, wout_ref[...])
        h, _, _ = _ln_fwd(pre, g_ref[...], b_ref[...])
        yg_ref[...] = yg.astype(BF)
        ym_ref[...] = ym.astype(BF)
        mix_ref[...] = mix
        pre_ref[...] = pre
        hb_ref[...] = h.astype(BF)

    full = lambda shp: pl.BlockSpec(shp, lambda i: (0,) * len(shp))
    row = lambda w: pl.BlockSpec((tm, w), lambda i: (i, 0))
    sd = lambda dt: jax.ShapeDtypeStruct((T, D), dt)
    return _call(
        body, name="post_attn_fwd", ride=ride, sem=("parallel",), args=(zg, attn, pt, x, wgo, wmo, wout, g1, b1),
        out_shape=(sd(BF), sd(BF), sd(BF), sd(F32), sd(BF)),
        grid=(T // tm,),
        in_specs=[row(D), row(D), row(PT_W), row(D), full((D, D)), full((D, D)), full((D, D)),
                  full((1, D)), full((1, D))],
        out_specs=(row(D),) * 5)


def _post_attn_bwd(dh, pre, pt, yg, ym, wgo, wmo, wout, g1, *, tm):
    T = dh.shape[0]

    def body(dh_ref, pre_ref, pt_ref, yg_ref, ym_ref, wgo_ref, wmo_ref, wout_ref, g_ref,
             dx_ref, dpreb_ref, dpt_ref, dygb_ref, dymb_ref, dzg_ref, dat_ref, dg_ref, db_ref):
        @pl.when(pl.program_id(0) == 0)
        def _():
            dg_ref[...] = jnp.zeros_like(dg_ref)
            db_ref[...] = jnp.zeros_like(db_ref)

        pre = pre_ref[...]
        mu = jnp.mean(pre, axis=-1, keepdims=True)
        xc = pre - mu
        rstd = lax.rsqrt(jnp.mean(xc * xc, axis=-1, keepdims=True) + LN_EPS)
        dpre, dg, db = _ln_bwd(dh_ref[...], xc * rstd, rstd, g_ref[...])
        dg_ref[...] += dg
        db_ref[...] += db
        dx_ref[...] = ALPHA * dpre
        dpreb = dpre.astype(BF)
        dpreb_ref[...] = dpreb
        dmix = _dot_nt(dpreb, wout_ref[...])
        sa = _sigmoid(pt_ref[:, 0:D].astype(F32))
        sb = _sigmoid(pt_ref[:, D:2 * D].astype(F32))
        dpt_ref[:, 0:D] = (dmix * yg_ref[...].astype(F32) * (sa * (1.0 - sa))).astype(BF)
        dpt_ref[:, D:2 * D] = (dmix * ym_ref[...].astype(F32) * (sb * (1.0 - sb))).astype(BF)
        dyg = (dmix * sa).astype(BF)
        dym = (dmix * sb).astype(BF)
        dygb_ref[...] = dyg
        dymb_ref[...] = dym
        dzg_ref[...] = _dot_nt(dyg, wgo_ref[...]).astype(BF)
        dat_ref[...] = _dot_nt(dym, wmo_ref[...]).astype(BF)

    full = lambda shp: pl.BlockSpec(shp, lambda i: (0,) * len(shp))
    row = lambda w: pl.BlockSpec((tm, w), lambda i: (i, 0))
    sd = lambda w, dt: jax.ShapeDtypeStruct((T, w), dt)
    return pl.pallas_call(
        body, name="post_attn_bwd",
        out_shape=(sd(D, F32), sd(D, BF), sd(PT_W, BF), sd(D, BF), sd(D, BF), sd(D, BF), sd(D, BF),
                   jax.ShapeDtypeStruct((1, D), F32), jax.ShapeDtypeStruct((1, D), F32)),
        grid=(T // tm,),
        in_specs=[row(D), row(D), row(PT_W), row(D), row(D), full((D, D)), full((D, D)), full((D, D)),
                  full((1, D))],
        out_specs=(row(D), row(D), row(PT_W), row(D), row(D), row(D), row(D), full((1, D)), full((1, D))),
        compiler_params=_params(("arbitrary",)),
    )(dh, pre, pt, yg, ym, wgo, wmo, wout, g1)


def _shift_down(u, prev, k):
    r = pltpu.roll(u, k, 0)
    p = pltpu.roll(prev, k, 0)
    head = jnp.where(_iota(p.shape, 0) < k, p, r[0:8, :])
    return jnp.concatenate([head, r[8:, :]], axis=0)


def _conv3(u, prev, w_ref, b_ref):
    return (w_ref[0:1, :] * _shift_down(u, prev, 2) + w_ref[1:2, :] * _shift_down(u, prev, 1)
            + w_ref[2:3, :] * u + b_ref[...])


def _ffn_up_fwd(hb, wug, wuv, cw, cb, *, S, tm, tn):
    T = hb.shape[0]
    nj, nbs = DFF // tn, S // tm

    def body(h_ref, wg_ref, wv_ref, cwg_ref, cwv_ref, cbg_ref, cbv_ref,
             ug_ref, uv_ref, ucg_ref, ucv_ref, f_ref, pg_scr, pv_scr):
        @pl.when(pl.program_id(1) % nbs == 0)
        def _():
            pg_scr[...] = jnp.zeros_like(pg_scr)
            pv_scr[...] = jnp.zeros_like(pv_scr)

        h = h_ref[...]
        ug = _dot(h, wg_ref[...])
        uv = _dot(h, wv_ref[...])
        ucg = _conv3(ug, pg_scr[...], cwg_ref, cbg_ref)
        ucv = _conv3(uv, pv_scr[...], cwv_ref, cbv_ref)
        pg_scr[...] = ug[tm - 8:, :]
        pv_scr[...] = uv[tm - 8:, :]
        ug_ref[...] = ug.astype(BF)
        uv_ref[...] = uv.astype(BF)
        ucg_ref[...] = ucg
        ucv_ref[...] = ucv
        f_ref[...] = (ucg * _sigmoid(ucg) * ucv).astype(BF)

    tile = pl.BlockSpec((tm, tn), lambda j, i: (i, j))
    return pl.pallas_call(
        body, name="ffn_up_fwd",
        out_shape=(jax.ShapeDtypeStruct((T, DFF), BF), jax.ShapeDtypeStruct((T, DFF), BF),
                   jax.ShapeDtypeStruct((T, DFF), F32), jax.ShapeDtypeStruct((T, DFF), F32),
                   jax.ShapeDtypeStruct((T, DFF), BF)),
        grid=(nj, T // tm),
        in_specs=[pl.BlockSpec((tm, D), lambda j, i: (i, 0)),
                  pl.BlockSpec((D, tn), lambda j, i: (0, j)), pl.BlockSpec((D, tn), lambda j, i: (0, j)),
                  pl.BlockSpec((3, tn), lambda j, i: (0, j)), pl.BlockSpec((3, tn), lambda j, i: (0, j + nj)),
                  pl.BlockSpec((1, tn), lambda j, i: (0, j)), pl.BlockSpec((1, tn), lambda j, i: (0, j + nj))],
        out_specs=(tile, tile, tile, tile, tile),
        scratch_shapes=[pltpu.VMEM((8, tn), F32), pltpu.VMEM((8, tn), F32)],
        compiler_params=_params(("parallel", "arbitrary")),
    )(hb, wug, wuv, cw, cw, cb, cb)


def _ffn_bwd(dpreb, wd, ug, uv, ucg, ucv, cw, *, S, tm, tn):
    T = dpreb.shape[0]
    nj, nb, nbs = DFF // tn, T // tm, S // tm
    r_, c_ = lax.broadcasted_iota(jnp.int32, (tm, tm), 0), lax.broadcasted_iota(jnp.int32, (tm, tm), 1)
    s1, s2 = (c_ == r_ + 1).astype(BF), (c_ == r_ + 2).astype(BF)

    def body(dp_ref, wd_ref, ug_ref, uv_ref, ucg_ref, ucv_ref, cwg_ref, cwv_ref, s1_ref, s2_ref,
             dug_ref, duv_ref, dcg_ref, dcv_ref, ng_scr, nv_scr):
        ii = pl.program_id(1)
        i = nb - 1 - ii
        tail_row = _iota((8, tn), 0)

        @pl.when(ii == 0)
        def _():
            dcg_ref[...] = jnp.zeros_like(dcg_ref)
            dcv_ref[...] = jnp.zeros_like(dcv_ref)

        @pl.when(i % nbs == nbs - 1)
        def _():
            ng_scr[...] = jnp.zeros_like(ng_scr)
            nv_scr[...] = jnp.zeros_like(nv_scr)

        df = _dot_nt(dp_ref[...], wd_ref[...])
        ucg = ucg_ref[...]
        sg = _sigmoid(ucg)
        ducg = df * ucv_ref[...] * (sg * (1.0 + ucg * (1.0 - sg)))
        ducv = df * (ucg * sg)

        def finish(duc, u_ref, w, nxt_scr, du_ref, dc_ref):
            nxt = nxt_scr[...]
            db = duc.astype(BF)

            def shifted(s_ref, k):
                r = _dot(s_ref[...], db)
                tail = jnp.where(tail_row >= 8 - k, pltpu.roll(nxt, 8 - k, 0), r[tm - 8:, :])
                return jnp.concatenate([r[:tm - 8, :], tail], axis=0)

            up1 = shifted(s1_ref, 1)
            up2 = shifted(s2_ref, 2)
            du_ref[...] = (w[2:3, :] * duc + w[1:2, :] * up1 + w[0:1, :] * up2).astype(BF)
            nxt_scr[...] = duc[0:8, :]
            u = u_ref[...].astype(F32)
            for row, z in enumerate((u * up2, u * up1, u * duc, duc)):
                dc_ref[row:row + 1, :] += jnp.sum(z, axis=0, keepdims=True)

        finish(ducg, ug_ref, cwg_ref, ng_scr, dug_ref, dcg_ref)
        finish(ducv, uv_ref, cwv_ref, nv_scr, duv_ref, dcv_ref)

    tile = pl.BlockSpec((tm, tn), lambda j, ii: (nb - 1 - ii, j))
    acc = pl.BlockSpec((8, tn), lambda j, ii: (0, j))
    return pl.pallas_call(
        body, name="ffn_bwd",
        out_shape=(jax.ShapeDtypeStruct((T, DFF), BF), jax.ShapeDtypeStruct((T, DFF), BF),
                   jax.ShapeDtypeStruct((8, DFF), F32), jax.ShapeDtypeStruct((8, DFF), F32)),
        grid=(nj, nb),
        in_specs=[pl.BlockSpec((tm, D), lambda j, ii: (nb - 1 - ii, 0)),
                  pl.BlockSpec((tn, D), lambda j, ii: (j, 0)),
                  tile, tile, tile, tile,
                  pl.BlockSpec((3, tn), lambda j, ii: (0, j)), pl.BlockSpec((3, tn), lambda j, ii: (0, j + nj)),
                  pl.BlockSpec((tm, tm), lambda j, ii: (0, 0)), pl.BlockSpec((tm, tm), lambda j, ii: (0, 0))],
        out_specs=(tile, tile, acc, acc),
        scratch_shapes=[pltpu.VMEM((8, tn), F32), pltpu.VMEM((8, tn), F32)],
        compiler_params=_params(("parallel", "arbitrary")),
    )(dpreb, wd, ug, uv, ucg, ucv, cw, cw, s1, s2)


def _down_ln2_loss(f_in, wd, pre1, target, g1, b1, g2, b2, *, tm):
    T = pre1.shape[0]

    def body(f_ref, wd_ref, p1_ref, t_ref, g1_ref, b1_ref, g_ref, b_ref, dpb_ref, dh_ref, loss_ref, dg_ref, db_ref):
        @pl.when(pl.program_id(0) == 0)
        def _():
            loss_ref[...] = jnp.zeros_like(loss_ref)
            dg_ref[...] = jnp.zeros_like(dg_ref)
            db_ref[...] = jnp.zeros_like(db_ref)

        halves = [pl.ds(s * (tm // 2), tm // 2) for s in range(2)]
        f = [_dot(f_ref[hs, :], wd_ref[...]) for hs in halves]
        for hs, fh in zip(halves, f):
            h, _, _ = _ln_fwd(p1_ref[hs, :], g1_ref[...], b1_ref[...])
            pre = ALPHA * h + fh
            out, xhat, rstd = _ln_fwd(pre, g_ref[...], b_ref[...])
            diff = out - t_ref[hs, :]
            loss_ref[...] += 0.5 * jnp.sum(jnp.mean(diff * diff, axis=-1, keepdims=True))
            dpre, dg, db = _ln_bwd(diff * (1.0 / D), xhat, rstd, g_ref[...])
            dg_ref[...] += dg
            db_ref[...] += db
            dpb_ref[hs, :] = dpre.astype(BF)
            dh_ref[hs, :] = ALPHA * dpre

    full = lambda shp: pl.BlockSpec(shp, lambda i: (0,) * len(shp))
    row = lambda w: pl.BlockSpec((tm, w), lambda i: (i, 0))
    return pl.pallas_call(
        body, name="down_ln2_loss",
        out_shape=(jax.ShapeDtypeStruct((T, D), BF), jax.ShapeDtypeStruct((T, D), F32),
                   jax.ShapeDtypeStruct((8, 128), F32), jax.ShapeDtypeStruct((1, D), F32),
                   jax.ShapeDtypeStruct((1, D), F32)),
        grid=(T // tm,),
        in_specs=[row(DFF), full((DFF, D)), row(D), row(D), full((1, D)), full((1, D)), full((1, D)), full((1, D))],
        out_specs=(row(D), row(D), full((8, 128)), full((1, D)), full((1, D))),
        compiler_params=_params(("arbitrary",)),
    )(f_in, wd, pre1, target, g1, b1, g2, b2)


def _adamw(parts, w, m, v, *, name):
    n, R, C = parts.shape
    tr, tc = R, C
    for cand in range(min(R, 256), 15, -1):
        if R % cand == 0 and cand % 16 == 0:
            tr = cand
            break
    if tr == R and R * C > 65536 and C % 256 == 0:
        tc = 256
    c1 = 1.0 - ADAM_B1 ** ADAM_STEP
    c2 = 1.0 - ADAM_B2 ** ADAM_STEP

    def body(p_ref, w_ref, m_ref, v_ref, g_ref, d_ref, nm_ref, nv_ref):
        g = p_ref[0].astype(F32)
        for s in range(1, n):
            g = g + p_ref[s].astype(F32)
        nm = ADAM_B1 * m_ref[...] + (1.0 - ADAM_B1) * g
        nv = ADAM_B2 * v_ref[...] + (1.0 - ADAM_B2) * (g * g)
        g_ref[...] = g
        nm_ref[...] = nm
        nv_ref[...] = nv
        d_ref[...] = -ADAM_LR * ((nm / c1) / (jnp.sqrt(nv / c2) + ADAM_EPS) + ADAM_WD * w_ref[...])

    blk = pl.BlockSpec((tr, tc), lambda i, j: (i, j))
    sd = jax.ShapeDtypeStruct((R, C), F32)
    return pl.pallas_call(
        body, name=name,
        out_shape=(sd, sd, sd, sd),
        grid=(R // tr, C // tc),
        in_specs=[pl.BlockSpec((n, tr, tc), lambda i, j: (0, i, j)), blk, blk, blk],
        out_specs=(blk, blk, blk, blk),
        compiler_params=_params(("parallel", "parallel")),
    )(parts, w, m, v)


class _Exchange:
    def __init__(self, items):
        self.items = [(src if sc else [(src, 0)], sc) for src, sc in items]
        self.arrays = [arr for srcs, _ in self.items for arr, _ in srcs]
        self.n = len(self.items)
        self.n_in = len(self.arrays)

    def out_shape(self):
        return tuple(jax.ShapeDtypeStruct((NDEV,) + (srcs[0][0].shape[1:] if sc else srcs[0][0].shape),
                                          srcs[0][0].dtype) for srcs, sc in self.items)

    def scratch(self):
        return [pltpu.SemaphoreType.DMA((self.n, NDEV - 1)), pltpu.SemaphoreType.DMA((self.n, NDEV - 1)),
                pltpu.SemaphoreType.DMA((self.n,))]

    def _emit(self, ins, outs, sems, phase):
        send_sems, recv_sems, loc_sems = sems
        x, y, c = lax.axis_index("x"), lax.axis_index("y"), lax.axis_index("c")
        me = 4 * x + 2 * y + c
        flip = lambda p, d: 1 - p if d else p

        def inside(p, lo, n):
            return None if (lo, n) == (0, NDEV) else jnp.logical_and(p >= lo, p < lo + n)

        def when(cond, fn):
            if cond is None:
                fn()
            else:
                pl.when(cond)(fn)

        pos = 0
        for a, (srcs, sc) in enumerate(self.items):
            refs = ins[pos:pos + len(srcs)]
            pos += len(srcs)
            ranges = [(lo, arr.shape[0]) if sc else (0, NDEV) for arr, lo in srcs]
            mine = [inside(me, lo, n) for lo, n in ranges]
            i_receive = None if None in mine else functools.reduce(jnp.logical_or, mine)
            for ref, (lo, n), cond in zip(refs, ranges, mine):
                def local(ref=ref, lo=lo):
                    cp = pltpu.make_async_copy(ref.at[me - lo] if sc else ref, outs[a].at[me], loc_sems.at[a])
                    cp.start() if phase == 0 else cp.wait()
                if phase != 1:
                    when(cond, local)
            for k in range(1, NDEV):
                px, py, pc = flip(x, k & 4), flip(y, k & 2), flip(c, k & 1)
                peer = 4 * px + 2 * py + pc
                mk = functools.partial(pltpu.make_async_remote_copy,
                                       send_sem=send_sems.at[a, k - 1], recv_sem=recv_sems.at[a, k - 1],
                                       device_id=(px, py, pc), device_id_type=MESH_ID)
                if phase == 1:
                    def arrival(mk=mk, peer=peer):
                        mk(src_ref=refs[0].at[0] if sc else refs[0], dst_ref=outs[a].at[peer]).wait_recv()
                    when(i_receive, arrival)
                    continue
                for ref, (lo, n) in zip(refs, ranges):
                    def send(mk=mk, ref=ref, lo=lo, peer=peer):
                        cp = mk(src_ref=ref.at[peer - lo] if sc else ref, dst_ref=outs[a].at[me])
                        cp.start() if phase == 0 else cp.wait_send()
                    when(inside(peer, lo, n), send)

    def start(self, ins, outs, sems):
        self._emit(ins, outs, sems, 0)

    def wait(self, ins, outs, sems):
        self._emit(ins, outs, sems, 1)
        self._emit(ins, outs, sems, 2)


def _call(body, *, name, grid, in_specs, out_specs, out_shape, args, scratch_shapes=(), sem=None, ride=None):
    if ride is None:
        return pl.pallas_call(body, name=name, grid=grid, in_specs=list(in_specs), out_specs=tuple(out_specs),
                              out_shape=tuple(out_shape), scratch_shapes=list(scratch_shapes),
                              compiler_params=_params(sem))(*args)
    n_in, n_out, n_scr, ne, ne_in = len(args), len(out_shape), len(scratch_shapes), ride.n, ride.n_in

    def ride_body(*refs):
        ins, ex_in = refs[:n_in], refs[n_in:n_in + ne_in]
        o0 = n_in + ne_in
        outs, ex_out = refs[o0:o0 + n_out], refs[o0 + n_out:o0 + n_out + ne]
        scr = refs[o0 + n_out + ne:o0 + n_out + ne + n_scr]
        sems = refs[o0 + n_out + ne + n_scr:]
        first = functools.reduce(jnp.logical_and, [pl.program_id(d) == 0 for d in range(len(grid))])
        last = functools.reduce(jnp.logical_and, [pl.program_id(d) == grid[d] - 1 for d in range(len(grid))])

        @pl.when(first)
        def _():
            ride.start(ex_in, ex_out, sems)

        body(*ins, *outs, *scr)

        @pl.when(last)
        def _():
            ride.wait(ex_in, ex_out, sems)

    anyspec = pl.BlockSpec(memory_space=pl.ANY)
    res = pl.pallas_call(
        ride_body, name=name, grid=grid,
        in_specs=list(in_specs) + [anyspec] * ne_in,
        out_specs=tuple(out_specs) + (anyspec,) * ne,
        out_shape=tuple(out_shape) + ride.out_shape(),
        scratch_shapes=list(scratch_shapes) + ride.scratch(),
        compiler_params=_params(("arbitrary",) * len(grid)),
    )(*args, *ride.arrays)
    return tuple(res[:n_out]), tuple(res[n_out:])


def _gather_two_level(arrays, *, name):
    n = len(arrays)

    def body(*refs):
        ins, outs = refs[:n], refs[n:2 * n]
        send_sems, recv_sems, loc_sems = refs[2 * n:]
        x, y, c = lax.axis_index("x"), lax.axis_index("y"), lax.axis_index("c")
        sibling = (x, y, 1 - c)
        chips = [(1 - x, y), (x, 1 - y), (1 - x, 1 - y)]
        idx = lambda px, py, pc: 4 * px + 2 * py + pc
        me = idx(x, y, c)

        def copy(a, k, block, to, src=None):
            return pltpu.make_async_remote_copy(
                src_ref=outs[a].at[block] if src is None else src, dst_ref=outs[a].at[block],
                send_sem=send_sems.at[a, k], recv_sem=recv_sems.at[a, k], device_id=to, device_id_type=MESH_ID)

        local = [pltpu.make_async_copy(ins[a], outs[a].at[me], loc_sems.at[a]) for a in range(n)]
        sent = []
        for a in range(n):
            sent.append(copy(a, 0, me, sibling, src=ins[a]))
            sent += [copy(a, 1 + j, me, (*chip, c), src=ins[a]) for j, chip in enumerate(chips)]
        for cp in local + sent:
            cp.start()
        for j, chip in enumerate(chips):
            for a in range(n):
                copy(a, 1 + j, idx(*chip, c), sibling).wait_recv()
                passed = copy(a, 4 + j, idx(*chip, c), sibling)
                passed.start()
                sent.append(passed)
        for a in range(n):
            copy(a, 0, idx(x, y, 1 - c), sibling).wait_recv()
            for j, chip in enumerate(chips):
                copy(a, 4 + j, idx(*chip, 1 - c), sibling).wait_recv()
        for cp in sent:
            cp.wait_send()
        for cp in local:
            cp.wait()

    anyspec = pl.BlockSpec(memory_space=pl.ANY)
    return pl.pallas_call(
        body, name=name,
        out_shape=tuple(jax.ShapeDtypeStruct((NDEV,) + a.shape, a.dtype) for a in arrays),
        in_specs=[anyspec] * n, out_specs=(anyspec,) * n,
        scratch_shapes=[pltpu.SemaphoreType.DMA((n, NDEV - 1)), pltpu.SemaphoreType.DMA((n, NDEV - 1)),
                        pltpu.SemaphoreType.DMA((n,))],
    )(*arrays)


def _tri_consts():
    r = lax.broadcasted_iota(jnp.int32, (GC, GC), 0)
    c = lax.broadcasted_iota(jnp.int32, (GC, GC), 1)
    return (r >= c).astype(BF), (r <= c).astype(BF)


def _local_step(x, positions, target, w, hooks=None):
    g = {}

    def run(host, fn, *a, **kw):
        h = None if hooks is None else hooks.get(host)
        if h is None:
            return fn(*a, **kw)
        out, received = fn(*a, ride=_Exchange(h[0](w, g)), **kw)
        h[1](received, w, g)
        return out

    nseq, S, _ = x.shape
    T = nseq * S
    tm = min(TOKEN_TM, S)
    tq = min(FLASH_TQ, S)
    x2 = x.reshape(T, D)
    pos = positions.reshape(T, 1)
    half = ROPE // 2
    inv = THETA ** (-jnp.arange(half, dtype=F32) / half)
    invf = jnp.concatenate([inv, inv, jnp.zeros((64,), F32)]).reshape(1, 128)
    ltri, utri = _tri_consts()

    pt, xb = _matmul(x2, w["w_tt"], "nt", name="proj_t", out_dtype=BF, tm=1024, tn=1024, tk=1024, emit_a=True)
    pg = run("proj_g", _matmul, xb, w["w_gt"], "nt", name="proj_g", tm=1024, tn=640, tk=1024)
    pm = _matmul(xb, w["w_mt"], "nt", name="proj_m", tm=1024, tn=768, tk=1024)
    o, zg, states = run("gla_fwd", _gla_fwd, pg, w["wg"], w["bg"], w["gn"], ltri, nseq=nseq, S=S,
                        tm=min(2 * tm, S))
    qc, kc, v = _mla_prep_fwd(pm, pos, invf, w["gq"], w["gkv"], w["wuq"], w["wukv"], tm=min(4 * tm, S))
    attn, lse = run("flash_fwd", _flash_fwd, qc, kc, v, nseq=nseq, S=S, tq=tq)
    yg, ym, mix, pre1, h1b = run("post_attn_fwd", _post_attn_fwd, zg, attn, pt, x2, w["wgo"], w["wmo"], w["wout"],
                                 w["g1"], w["b1"], tm=min(2 * tm, S))
    ug, uv, ucg, ucv, f_in = _ffn_up_fwd(h1b, w["wug"], w["wuv"], w["cw"], w["cb"], S=S, tm=min(2 * tm, S),
                                         tn=FFN_TN)
    dpre2b, dh1, loss8, dg2, db2 = _down_ln2_loss(f_in, w["wd"], pre1, target.reshape(T, D), w["g1"], w["b1"],
                                                  w["g2"], w["b2"], tm=min(2 * tm, S))

    dug, duv, dcg, dcv = _ffn_bwd(dpre2b, w["wd"], ug, uv, ucg, ucv, w["cw"], S=S, tm=tm, tn=FFN_TN)
    g["g2"], g["b2"], g["loss"] = dg2, db2, loss8[0:1, 0:1]
    g["cw"] = jnp.concatenate([dcg[0:3], dcv[0:3]], axis=1)
    g["cb"] = jnp.concatenate([dcg[3:4], dcv[3:4]], axis=1)
    g["wd"] = _matmul(f_in, dpre2b, "tn", name="dw_down", out_dtype=BF, tm=1408, tn=1024, tk=1024)
    g["wugt"] = _matmul(dug, h1b, "tn", name="dw_up_g", out_dtype=BF, tm=1408, tn=1024, tk=1024)
    g["wuvt"] = _matmul(duv, h1b, "tn", name="dw_up_v", out_dtype=BF, tm=1408, tn=1024, tk=1024)
    dh1 = _matmul(dug, w["wugt"], "nn", name="dh1_g", c_in=dh1, tm=1024, tn=1024, tk=1408)
    dh1 = _matmul(duv, w["wuvt"], "nn", name="dh1_v", c_in=dh1, tm=1024, tn=1024, tk=1408)
    dx, dpre1b, dpt, dygb, dymb, dzg, dattn, dg1, db1 = _post_attn_bwd(
        dh1, pre1, pt, yg, ym, w["wgo"], w["wmo"], w["wout"], w["g1"], tm=min(2 * tm, S))
    g["g1"], g["b1"] = dg1, db1
    g["wout"] = _matmul(mix, dpre1b, "tn", name="dw_out", out_dtype=BF, tm=1024, tn=1024, tk=1024)
    g["wgo"] = _matmul(zg, dygb, "tn", name="dw_gla_o", out_dtype=BF, tm=1024, tn=1024, tk=1024)
    g["wmo"] = _matmul(attn, dymb, "tn", name="dw_mla_o", out_dtype=BF, tm=1024, tn=1024, tk=1024)
    dqc, dkc, dv = run("flash_bwd", _flash_bwd, qc, kc, v, attn, dattn, lse, nseq=nseq, S=S, tq=tq)
    dpm, g["wuq"], g["wukv"], g["gq"], g["gkv"] = _mla_prep_bwd(
        pm, pos, invf, w["gq"], w["gkv"], w["wuq"], w["wukv"], dqc, dkc, dv, tm=min(2 * tm, S))
    g["w_mt"] = _matmul(dpm, xb, "tn", name="dw_in_m", out_dtype=BF, tm=768, tn=1024, tk=1024)
    g["w_tt"] = _matmul(dpt, xb, "tn", name="dw_in_t", out_dtype=BF, tm=1024, tn=1024, tk=1024)
    dpg, g["wg"], g["bg"], g["gn"] = run("gla_bwd", _gla_bwd, pg, w["wg"], w["bg"], w["gn"], ltri, utri, o, states,
                                         dzg, nseq=nseq, S=S, tm=min(2 * tm, S))
    g["w_gt"] = _matmul(dpg, xb, "tn", name="dw_in_g", out_dtype=BF, tm=640, tn=1024, tk=1024)
    dx = run("dx", _matmul_sum, dx, [(dpg, w["w_gt"], 640), (dpm, w["w_mt"], 768)], name="dx_gm")
    dx = _matmul_sum(dx, [(dpt, w["w_tt"], 1024)], name="dx_t")
    return loss8[0, 0], dx.reshape(nseq, S, D), g


_IN_SPLITS = (512, 512, 1024, 16, 1024, 384, 256, 64, 1024, 1024)


def _w_in_to_groups(wt):
    offs = [0]
    for s in _IN_SPLITS:
        offs.append(offs[-1] + s)
    q, k, v, r, og, cq, ckv, kr, ga, gb = [wt[offs[i]:offs[i + 1]] for i in range(10)]
    z = lambda n: jnp.zeros((n, wt.shape[1]), wt.dtype)
    return (jnp.concatenate([q, k, v, og, r, z(112)], axis=0),
            jnp.concatenate([cq, kr, z(64), ckv], axis=0),
            jnp.concatenate([ga, gb], axis=0))


W_IN_BLOCK = sum(_IN_SPLITS) // NDEV
_KV_LATENT_ROW = sum(_IN_SPLITS[:6])
_W_IN_LO = 5
_W_IN_SPLIT = _W_IN_LO * W_IN_BLOCK - _KV_LATENT_ROW


def _w_in_rows_lo(g_g, g_m):
    q, k, v, og, r = g_g[0:512], g_g[512:1024], g_g[1024:2048], g_g[2048:3072], g_g[3072:3088]
    return jnp.concatenate([q, k, v, r, og, g_m[0:384], g_m[512:768]], axis=0)[:_W_IN_LO * W_IN_BLOCK]


def _w_in_rows_hi(g_m, g_t):
    return jnp.concatenate([g_m[512:768], g_m[384:448], g_t], axis=0)[_W_IN_SPLIT:]


def _uq_to_kernel(wuq):
    w3 = wuq.reshape(MQR, MH, NOPE + ROPE)
    rope = jnp.concatenate([w3[:, :, NOPE:], jnp.zeros((MQR, MH, 64), wuq.dtype)], axis=2)
    return jnp.concatenate([w3[:, :, :NOPE].reshape(MQR, MH * 128), rope.reshape(MQR, MH * 128)], axis=1)


def _uq_from_kernel(g):
    nope = g[:, :1024].reshape(MQR, MH, 128)
    rope = g[:, 1024:].reshape(MQR, MH, 128)[:, :, :ROPE]
    return jnp.concatenate([nope, rope], axis=2)


def _ukv_to_kernel(wukv):
    w3 = wukv.reshape(MKR, MH, NOPE + MV)
    return jnp.concatenate([w3[:, :, :NOPE].reshape(MKR, MH * 128), w3[:, :, NOPE:].reshape(MKR, MH * 128)], axis=1)


def _ukv_from_kernel(g):
    return jnp.concatenate([g[:, :1024].reshape(MKR, MH, 128), g[:, 1024:].reshape(MKR, MH, 128)], axis=2)


def _cols_gathered(a):
    return a.transpose(1, 0, 2).reshape(a.shape[1], NDEV * a.shape[2])


def _cols_scattered(a):
    R = a.shape[0]
    return a.reshape(R, NDEV, a.shape[1] // NDEV).transpose(1, 0, 2)


_SMALL = (("gla_b_gate", 512), ("gla_norm_g", 256), ("mla_q_norm_g", 384), ("mla_kv_norm_g", 256),
          ("ln1_g", 1024), ("ln1_b", 1024), ("conv_b", 5632), ("ln2_g", 1024), ("ln2_b", 1024))
_SMALL_ROWS = 88
_SMALL_USED = sum(sz for _, sz in _SMALL)


def _pack_small(d):
    flat = jnp.concatenate([d[n].reshape(-1) for n, _ in _SMALL] + ([d['loss'].reshape(-1)] if 'loss' in d else []))
    return jnp.pad(flat, (0, _SMALL_ROWS * 128 - flat.shape[0])).reshape(_SMALL_ROWS, 128)


def _unpack_small(a):
    flat = a.reshape(-1)
    out, off = {}, 0
    for n, sz in _SMALL:
        out[n] = flat[off:off + sz].reshape(1, sz)
        off += sz
    return out


_NAMES = ['w_in', 'gla_w_gate_up', 'gla_b_gate', 'gla_norm_g', 'w_gla_o', 'mla_q_norm_g', 'mla_w_uq',
          'mla_kv_norm_g', 'mla_w_ukv', 'w_mla_o', 'w_out', 'ln1_g', 'ln1_b', 'w_up', 'conv_w', 'conv_b',
          'w_down', 'ln2_g', 'ln2_b']
_SHARDED = ['w_in', 'w_up', 'w_down', 'w_gla_o', 'w_mla_o', 'w_out', 'mla_w_uq', 'mla_w_ukv', 'gla_w_gate_up',
            'conv_w']


def kernel(x, positions, w_in, gla_w_gate_up, gla_b_gate, gla_norm_g, w_gla_o, mla_q_norm_g, mla_w_uq, mla_kv_norm_g, mla_w_ukv, w_mla_o, w_out, ln1_g, ln1_b, w_up, conv_w, conv_b, w_down, ln2_g, ln2_b, loss_target, m_w_in, m_gla_w_gate_up, m_gla_b_gate, m_gla_norm_g, m_w_gla_o, m_mla_q_norm_g, m_mla_w_uq, m_mla_kv_norm_g, m_mla_w_ukv, m_w_mla_o, m_w_out, m_ln1_g, m_ln1_b, m_w_up, m_conv_w, m_conv_b, m_w_down, m_ln2_g, m_ln2_b, v_w_in, v_gla_w_gate_up, v_gla_b_gate, v_gla_norm_g, v_w_gla_o, v_mla_q_norm_g, v_mla_w_uq, v_mla_kv_norm_g, v_mla_w_ukv, v_w_mla_o, v_w_out, v_ln1_g, v_ln1_b, v_w_up, v_conv_w, v_conv_b, v_w_down, v_ln2_g, v_ln2_b):
    W = dict(w_in=w_in, gla_w_gate_up=gla_w_gate_up, gla_b_gate=gla_b_gate, gla_norm_g=gla_norm_g, w_gla_o=w_gla_o, mla_q_norm_g=mla_q_norm_g, mla_w_uq=mla_w_uq, mla_kv_norm_g=mla_kv_norm_g, mla_w_ukv=mla_w_ukv, w_mla_o=w_mla_o, w_out=w_out, ln1_g=ln1_g, ln1_b=ln1_b, w_up=w_up, conv_w=conv_w, conv_b=conv_b, w_down=w_down, ln2_g=ln2_g, ln2_b=ln2_b)
    M = dict(w_in=m_w_in, gla_w_gate_up=m_gla_w_gate_up, gla_b_gate=m_gla_b_gate, gla_norm_g=m_gla_norm_g, w_gla_o=m_w_gla_o, mla_q_norm_g=m_mla_q_norm_g, mla_w_uq=m_mla_w_uq, mla_kv_norm_g=m_mla_kv_norm_g, mla_w_ukv=m_mla_w_ukv, w_mla_o=m_w_mla_o, w_out=m_w_out, ln1_g=m_ln1_g, ln1_b=m_ln1_b, w_up=m_w_up, conv_w=m_conv_w, conv_b=m_conv_b, w_down=m_w_down, ln2_g=m_ln2_g, ln2_b=m_ln2_b)
    V = dict(w_in=v_w_in, gla_w_gate_up=v_gla_w_gate_up, gla_b_gate=v_gla_b_gate, gla_norm_g=v_gla_norm_g, w_gla_o=v_w_gla_o, mla_q_norm_g=v_mla_q_norm_g, mla_w_uq=v_mla_w_uq, mla_kv_norm_g=v_mla_kv_norm_g, mla_w_ukv=v_mla_w_ukv, w_mla_o=v_w_mla_o, w_out=v_w_out, ln1_g=v_ln1_g, ln1_b=v_ln1_b, w_up=v_w_up, conv_w=v_conv_w, conv_b=v_conv_b, w_down=v_w_down, ln2_g=v_ln2_g, ln2_b=v_ln2_b)

    tshard = lambda d, n: d[n][0].T
    shard = lambda n: (W[n][0].astype(BF), False)
    (w_in_t,) = _gather_two_level([tshard(W, 'w_in').astype(BF)], name="gather_w0")
    w_gt, w_mt, w_tt = _w_in_to_groups(w_in_t.reshape(NDEV * W_IN_BLOCK, D))
    kw = dict(
        w_gt=w_gt, w_mt=w_mt, w_tt=w_tt, bg=W['gla_b_gate'],
        gn=W['gla_norm_g'], gq=W['mla_q_norm_g'], gkv=W['mla_kv_norm_g'],
        g1=W['ln1_g'], b1=W['ln1_b'], g2=W['ln2_g'], b2=W['ln2_b'], cb=W['conv_b'],
    )
    received = {}

    def got_mixers(ex, w, g):
        w.update(wuq=_uq_to_kernel(_cols_gathered(ex[0])), wukv=_ukv_to_kernel(_cols_gathered(ex[1])),
                 wg=jnp.pad(_cols_gathered(ex[2]), ((0, 128 - GR), (0, 0))))

    def got_out_proj(ex, w, g):
        w.update(wgo=ex[0].reshape(D, D), wmo=ex[1].reshape(D, D), wout=ex[2].reshape(D, D))

    def got_up(ex, w, g):
        w_upt = ex[0].reshape(2 * DFF, D)
        w.update(wugt=w_upt[:DFF], wuvt=w_upt[DFF:], wug=w_upt[:DFF].T, wuv=w_upt[DFF:].T)

    def got_down(ex, w, g):
        w.update(wd=ex[0].reshape(DFF, D), cw=_cols_gathered(ex[1]))

    slab = lambda a, lo=0: ([(a.astype(BF), lo)], True)
    rows = lambda a, n=NDEV: a.reshape(n, a.shape[0] // n, a.shape[1])

    def keep(names):
        return lambda ex, w, g: received.update(zip(names, ex))

    def small_grads(g):
        return _pack_small(dict(gla_b_gate=g['bg'], gla_norm_g=g['gn'], mla_q_norm_g=g['gq'], mla_kv_norm_g=g['gkv'],
                                ln1_g=g['g1'], ln1_b=g['b1'], conv_b=g['cb'], ln2_g=g['g2'], ln2_b=g['b2'],
                                loss=g['loss']))

    hooks = {
        "proj_g": (lambda w, g: [shard('mla_w_uq'), shard('mla_w_ukv'), shard('gla_w_gate_up')], got_mixers),
        "gla_fwd": (lambda w, g: [shard('w_gla_o'), shard('w_mla_o'), shard('w_out')], got_out_proj),
        "flash_fwd": (lambda w, g: [(tshard(W, 'w_up').astype(BF), False)], got_up),
        "post_attn_fwd": (lambda w, g: [shard('w_down'), (W['conv_w'][0], False)], got_down),
        "flash_bwd": (lambda w, g: [slab(rows(g['wd'])),
                                    ([(rows(g['wugt'], 4), 0), (rows(g['wuvt'], 4), 4)], True),
                                    slab(rows(g['wout'])), slab(rows(g['wgo'])), slab(rows(g['wmo']))],
                      keep(['w_down', 'w_up', 'w_out', 'w_gla_o', 'w_mla_o'])),
        "gla_bwd": (lambda w, g: [slab(_uq_from_kernel(g['wuq']).transpose(1, 0, 2)),
                                  slab(_ukv_from_kernel(g['wukv']).transpose(1, 0, 2)),
                                  slab(rows(_w_in_rows_hi(g['w_mt'], g['w_tt']), NDEV - _W_IN_LO), _W_IN_LO)],
                    keep(['mla_w_uq', 'mla_w_ukv', 'w_in_hi'])),
        "dx": (lambda w, g: [slab(rows(_w_in_rows_lo(g['w_gt'], g['w_mt']), _W_IN_LO)),
                             ([(_cols_scattered(g['wg'][:GR]), 0)], True), ([(_cols_scattered(g['cw']), 0)], True),
                             (small_grads(g), False)],
               keep(['w_in_lo', 'gla_w_gate_up', 'conv_w', 'small'])),
    }

    _, grad_x, _ = _local_step(x, positions, loss_target, kw, hooks)

    grads, deltas, new_m, new_v = {}, {}, {}, {}
    small_parts = received['small']
    loss = jnp.sum(small_parts.reshape(NDEV, -1)[:, _SMALL_USED])
    me = 4 * lax.axis_index("x") + 2 * lax.axis_index("y") + lax.axis_index("c")
    received['w_in'] = jnp.where(me >= _W_IN_LO, received['w_in_hi'], received['w_in_lo'])
    for n in _SHARDED:
        shp = W[n].shape
        if n in ('w_in', 'w_up'):
            out = _adamw(received[n], tshard(W, n), tshard(M, n), tshard(V, n), name="adamw_" + n)
            grads[n], deltas[n], new_m[n], new_v[n] = [t.T.reshape(shp) for t in out]
            continue
        out = _adamw(received[n], W[n][0], M[n][0], V[n][0], name="adamw_" + n)
        grads[n], deltas[n], new_m[n], new_v[n] = [t.reshape(shp) for t in out]
    out = _adamw(small_parts, _pack_small(W), _pack_small(M), _pack_small(V), name="adamw_small")
    for dst, packed in zip((grads, deltas, new_m, new_v), out):
        dst.update(_unpack_small(packed))

    return (loss, grad_x, *[grads[n] for n in _NAMES], *[deltas[n] for n in _NAMES],
            *[new_m[n] for n in _NAMES], *[new_v[n] for n in _NAMES])
```

```python
import functools

import jax
import jax.numpy as jnp
from jax import lax
from jax.experimental import pallas as pl
from jax.experimental.pallas import tpu as pltpu

F32 = jnp.float32
BF = jnp.bfloat16

D = 1024
GH, GDK, GDV, GR, GTAU, GC = 4, 128, 256, 16, 16.0, 64
MH, MQR, MKR, NOPE, ROPE, MV = 8, 384, 256, 128, 64, 128
THETA = 10000.0
DFF = 2816
ALPHA = 2.0 ** 0.25
LN_EPS = 1e-5
RMS_EPS = 1e-6
NDEV = 8
ADAM_LR, ADAM_B1, ADAM_B2, ADAM_EPS, ADAM_WD, ADAM_STEP = 0.001, 0.9, 0.999, 1e-08, 0.01, 10

PG_W = 3200
PM_W = 768
PT_W = 2048
NEG = -1e30
MESH_ID = pl.DeviceIdType.MESH
VMEM_MB = 1024 * 1024


V7X_VMEM_LIMIT_MB = 48
TOKEN_TM = 256
FLASH_TQ = 512
FFN_TN = 1408


def _params(sem):
    return pltpu.CompilerParams(dimension_semantics=sem, vmem_limit_bytes=V7X_VMEM_LIMIT_MB * VMEM_MB)


def _dot(a, b):
    return lax.dot_general(a, b, (((1,), (0,)), ((), ())), preferred_element_type=F32)


def _dot_nt(a, b):
    return lax.dot_general(a, b, (((1,), (1,)), ((), ())), preferred_element_type=F32)


def _dot_tn(a, b):
    return lax.dot_general(a, b, (((0,), (0,)), ((), ())), preferred_element_type=F32)


def _iota(shape, dim):
    return lax.broadcasted_iota(jnp.int32, shape, dim)


FLASH_HP = 2
FLASH_HP_FWD = 4
QK_SCALE = (NOPE + ROPE) ** -0.5
LOG2E = 1.4426950408889634
QK_SCALE_LOG2 = QK_SCALE * LOG2E


def _sigmoid(x):
    return 0.5 * jnp.tanh(0.5 * x) + 0.5


def _tri_mm(tri_bf, x):
    hi = x.astype(BF)
    r1 = x - hi.astype(F32)
    mid = r1.astype(BF)
    lo = (r1 - mid.astype(F32)).astype(BF)
    return _dot(tri_bf, hi) + _dot(tri_bf, mid) + _dot(tri_bf, lo)


def _matmul(a, b, mode, *, name, c_in=None, out_dtype=F32, tm=512, tn=512, tk=512, ride=None, emit_a=False):
    if mode == "nn":
        (M, K), (_, N) = a.shape, b.shape
    elif mode == "nt":
        (M, K), (N, _) = a.shape, b.shape
    else:
        (K, M), (_, N) = a.shape, b.shape
    tm, tn, tk = min(tm, M), min(tn, N), min(tk, K)
    assert M % tm == 0 and N % tn == 0 and K % tk == 0, (name, M, N, K, tm, tn, tk)
    nk = K // tk
    assert not emit_a or (nk == 1 and mode != "tn" and c_in is None and ride is None)
    dot = {"nn": _dot, "nt": _dot_nt, "tn": _dot_tn}[mode]

    def body(*refs):
        if emit_a:
            a_ref, b_ref, o_ref, xa_ref, acc_ref = refs
        elif c_in is None:
            a_ref, b_ref, o_ref, acc_ref = refs
        else:
            a_ref, b_ref, c_ref, o_ref, acc_ref = refs
        k = pl.program_id(2)

        @pl.when(k == 0)
        def _():
            if c_in is None:
                acc_ref[...] = jnp.zeros_like(acc_ref)
            else:
                acc_ref[...] = c_ref[...].astype(F32)

        if emit_a:
            @pl.when(pl.program_id(1) == 0)
            def _():
                xa_ref[...] = a_ref[...].astype(BF)

        acc_ref[...] += dot(a_ref[...].astype(BF), b_ref[...].astype(BF))

        @pl.when(k == nk - 1)
        def _():
            o_ref[...] = acc_ref[...].astype(out_dtype)

    if mode == "tn":
        a_spec = pl.BlockSpec((tk, tm), lambda i, j, k: (k, i))
    else:
        a_spec = pl.BlockSpec((tm, tk), lambda i, j, k: (i, k))
    if mode == "nt":
        b_spec = pl.BlockSpec((tn, tk), lambda i, j, k: (j, k))
    else:
        b_spec = pl.BlockSpec((tk, tn), lambda i, j, k: (k, j))
    in_specs = [a_spec, b_spec]
    args = [a, b]
    if c_in is not None:
        in_specs.append(pl.BlockSpec((tm, tn), lambda i, j, k: (i, j)))
        args.append(c_in)
    out_shape = (jax.ShapeDtypeStruct((M, N), out_dtype),)
    out_specs = (pl.BlockSpec((tm, tn), lambda i, j, k: (i, j)),)
    if emit_a:
        out_shape += (jax.ShapeDtypeStruct((M, K), BF),)
        out_specs += (pl.BlockSpec((tm, tk), lambda i, j, k: (i, k)),)
    res = _call(
        body, name=name, out_shape=out_shape, grid=(M // tm, N // tn, nk), in_specs=in_specs, out_specs=out_specs,
        scratch_shapes=[pltpu.VMEM((tm, tn), F32)],
        sem=("parallel", "arbitrary", "arbitrary"), args=args, ride=ride)
    if emit_a:
        return res[0], res[1]
    return res[0] if ride is None else (res[0][0], res[1])


def _matmul_sum(c_in, parts, *, name, tm=1024, ride=None):
    M, N = c_in.shape
    tm = min(tm, M)
    n_p = len(parts)
    counts = [a.shape[1] // tk for a, _, tk in parts]
    starts = [sum(counts[:p]) for p in range(n_p)]
    nk = sum(counts)

    def body(*refs):
        a_refs, w_refs = refs[:n_p], refs[n_p:2 * n_p]
        c_ref, o_ref, acc_ref = refs[2 * n_p:]
        k = pl.program_id(1)

        @pl.when(k == 0)
        def _():
            acc_ref[...] = c_ref[...]

        for p in range(n_p):
            @pl.when(jnp.logical_and(k >= starts[p], k < starts[p] + counts[p]))
            def _(p=p):
                acc_ref[...] += _dot(a_refs[p][...].astype(BF), w_refs[p][...].astype(BF))

        @pl.when(k == nk - 1)
        def _():
            o_ref[...] = acc_ref[...]

    def kidx(p):
        return lambda k: jnp.clip(k - starts[p], 0, counts[p] - 1)

    in_specs = [pl.BlockSpec((tm, tk), lambda i, k, f=kidx(p): (i, f(k))) for p, (_, _, tk) in enumerate(parts)]
    in_specs += [pl.BlockSpec((tk, N), lambda i, k, f=kidx(p): (f(k), 0)) for p, (_, _, tk) in enumerate(parts)]
    in_specs.append(pl.BlockSpec((tm, N), lambda i, k: (i, 0)))
    res = _call(
        body, name=name, out_shape=(jax.ShapeDtypeStruct((M, N), F32),), grid=(M // tm, nk),
        in_specs=in_specs, out_specs=(pl.BlockSpec((tm, N), lambda i, k: (i, 0)),),
        scratch_shapes=[pltpu.VMEM((tm, N), F32)], sem=("parallel", "arbitrary"),
        args=[a for a, _, _ in parts] + [w for _, w, _ in parts] + [c_in], ride=ride)
    return res[0] if ride is None else (res[0][0], res[1])


def _gla_gate(pg_ref, rows, wg_ref, bg_ref):
    r = pg_ref[rows, 3072:3200].astype(BF)
    logit = _dot(r, wg_ref[...]) + bg_ref[...]
    la = (jnp.minimum(logit, 0.0) - jnp.log(1.0 + jnp.exp(-jnp.abs(logit)))) * (1.0 / GTAU)
    return r, logit, la


def _gla_fwd(pg, wg, bg, gn, ltri, *, nseq, S, tm, ride=None):
    T = pg.shape[0]
    nb, nc = S // tm, tm // GC
    qscale = GDK ** -0.5

    def body(pg_ref, wg_ref, bg_ref, gn_ref, l_ref, o_ref, zg_ref, st_ref, st_scr):
        @pl.when(pl.program_id(1) == 0)
        def _():
            st_scr[...] = jnp.zeros_like(st_scr)

        ltri_v = l_ref[...]
        causal = _iota((GC, GC), 0) >= _iota((GC, GC), 1)
        last_row = _iota((GC, GDK), 0) == GC - 1
        g = gn_ref[...]

        def chunk(c, carry):
            rows = pl.ds(pl.multiple_of(c * GC, GC), GC)
            _, _, la = _gla_gate(pg_ref, rows, wg_ref, bg_ref)
            b = _tri_mm(ltri_v, la)
            hs = range(GH)
            v, q_in, k_st, dec, st, a_raw, o_st, kv = [], [], [], [], [], [], [], []
            for h in hs:
                q = pg_ref[rows, h * GDK:(h + 1) * GDK]
                k = pg_ref[rows, 512 + h * GDK:512 + (h + 1) * GDK]
                v.append(pg_ref[rows, 1024 + h * GDV:1024 + (h + 1) * GDV].astype(BF))
                bh = b[:, h * GDK:(h + 1) * GDK]
                bl = jnp.sum(jnp.where(last_row, bh, 0.0), axis=0, keepdims=True)
                q_in.append((q * (qscale * jnp.exp(bh))).astype(BF))
                k_in = (k * jnp.exp(-bh)).astype(BF)
                k_st.append((k * jnp.exp(bl - bh)).astype(BF))
                dec.append(jnp.exp(bl))
                st.append(st_scr[h])
                st_ref[c, h] = st[h]
                a_raw.append(_dot_nt(q_in[h], k_in))
            for h in hs:
                o_st.append(_dot_nt(q_in[h], st[h].astype(BF)))
                kv.append(_dot_tn(v[h], k_st[h]))
            att = [jnp.where(causal, a_raw[h], 0.0).astype(BF) for h in hs]
            o = [_dot(att[h], v[h]) + o_st[h] for h in hs]
            for h in hs:
                st_scr[h] = st[h] * dec[h] + kv[h]
                og = pg_ref[rows, 2048 + h * GDV:2048 + (h + 1) * GDV]
                rstd = lax.rsqrt(jnp.mean(o[h] * o[h], axis=-1, keepdims=True) + RMS_EPS)
                o_ref[rows, h * GDV:(h + 1) * GDV] = o[h]
                zg_ref[rows, h * GDV:(h + 1) * GDV] = (o[h] * rstd * g * (og * _sigmoid(og))).astype(BF)
            return carry

        lax.fori_loop(0, nc, chunk, 0, unroll=True)

    full = lambda shp: pl.BlockSpec(shp, lambda b_, i: (0,) * len(shp))
    return _call(
        body, name="gla_fwd", ride=ride, sem=("parallel", "arbitrary"), args=(pg, wg, bg, gn, ltri),
        out_shape=(jax.ShapeDtypeStruct((T, GH * GDV), F32),
                   jax.ShapeDtypeStruct((T, GH * GDV), BF),
                   jax.ShapeDtypeStruct((T // GC, GH, GDV, GDK), F32)),
        grid=(nseq, nb),
        in_specs=[pl.BlockSpec((tm, PG_W), lambda b_, i: (b_ * nb + i, 0)),
                  full((128, 512)), full((1, 512)), full((1, GDV)), full((GC, GC))],
        out_specs=(pl.BlockSpec((tm, GH * GDV), lambda b_, i: (b_ * nb + i, 0)),
                   pl.BlockSpec((tm, GH * GDV), lambda b_, i: (b_ * nb + i, 0)),
                   pl.BlockSpec((nc, GH, GDV, GDK), lambda b_, i: (b_ * nb + i, 0, 0, 0))),
        scratch_shapes=[pltpu.VMEM((GH, GDV, GDK), F32)])


def _gla_bwd(pg, wg, bg, gn, ltri, utri, o, states, dzg, *, nseq, S, tm, ride=None):
    T = pg.shape[0]
    nb, nc = S // tm, tm // GC
    qscale = GDK ** -0.5

    def body(pg_ref, wg_ref, bg_ref, gn_ref, l_ref, u_ref, o_ref, st_ref, dzg_ref,
             dpg_ref, dwg_ref, dbg_ref, dgn_ref, dst_scr):
        first = jnp.logical_and(pl.program_id(0) == 0, pl.program_id(1) == 0)

        @pl.when(first)
        def _():
            dwg_ref[...] = jnp.zeros_like(dwg_ref)
            dbg_ref[...] = jnp.zeros_like(dbg_ref)
            dgn_ref[...] = jnp.zeros_like(dgn_ref)

        @pl.when(pl.program_id(1) == 0)
        def _():
            dst_scr[...] = jnp.zeros_like(dst_scr)

        ltri_v = l_ref[...]
        utri_v = u_ref[...]
        causal = _iota((GC, GC), 0) >= _iota((GC, GC), 1)
        last_row = _iota((GC, GDK), 0) == GC - 1
        g = gn_ref[...]

        def chunk(cc, carry):
            c = nc - 1 - cc
            rows = pl.ds(pl.multiple_of(c * GC, GC), GC)
            r, logit, la = _gla_gate(pg_ref, rows, wg_ref, bg_ref)
            b = _tri_mm(ltri_v, la)
            hs = range(GH)
            L = lambda: [None] * GH
            vb, eb, enb, ek, dec, q_in, k_in, k_st, q_inb, k_inb, st, dst, dob = (L() for _ in range(13))
            a_raw, da_raw, dq_st, dks, dv_st, dst_new, dbs, dgn = (L() for _ in range(8))
            for h in hs:
                q = pg_ref[rows, h * GDK:(h + 1) * GDK]
                k = pg_ref[rows, 512 + h * GDK:512 + (h + 1) * GDK]
                vb[h] = pg_ref[rows, 1024 + h * GDV:1024 + (h + 1) * GDV].astype(BF)
                og = pg_ref[rows, 2048 + h * GDV:2048 + (h + 1) * GDV]
                oh = o_ref[rows, h * GDV:(h + 1) * GDV]
                dz = dzg_ref[rows, h * GDV:(h + 1) * GDV].astype(F32)
                bh = b[:, h * GDK:(h + 1) * GDK]
                bl = jnp.sum(jnp.where(last_row, bh, 0.0), axis=0, keepdims=True)
                eb[h] = qscale * jnp.exp(bh)
                enb[h] = jnp.exp(-bh)
                ek[h] = jnp.exp(bl - bh)
                dec[h] = jnp.exp(bl)
                q_in[h], k_in[h], k_st[h] = q * eb[h], k * enb[h], k * ek[h]
                q_inb[h], k_inb[h] = q_in[h].astype(BF), k_in[h].astype(BF)
                st[h] = st_ref[c, h]
                dst[h] = dst_scr[h]
                rstd = lax.rsqrt(jnp.mean(oh * oh, axis=-1, keepdims=True) + RMS_EPS)
                ohat = oh * rstd
                sg = _sigmoid(og)
                don = dz * (og * sg)
                dpg_ref[rows, 2048 + h * GDV:2048 + (h + 1) * GDV] = (
                    dz * (ohat * g) * (sg * (1.0 + og * (1.0 - sg)))).astype(BF)
                dgn[h] = jnp.sum(don * ohat, axis=0, keepdims=True)
                gd = don * g
                dob[h] = (rstd * (gd - ohat * jnp.mean(gd * ohat, axis=-1, keepdims=True))).astype(BF)
                a_raw[h] = _dot_nt(q_inb[h], k_inb[h])
                da_raw[h] = _dot_nt(dob[h], vb[h])
            dgn_ref[...] += dgn[0] + dgn[1] + dgn[2] + dgn[3]
            for h in hs:
                dstb = dst[h].astype(BF)
                dq_st[h] = _dot(dob[h], st[h].astype(BF))
                dks[h] = _dot(vb[h], dstb)
                dv_st[h] = _dot_nt(k_st[h].astype(BF), dstb)
                dst_new[h] = _dot_tn(dob[h], q_inb[h])
            att = [jnp.where(causal, a_raw[h], 0.0).astype(BF) for h in hs]
            da = [jnp.where(causal, da_raw[h], 0.0).astype(BF) for h in hs]
            dqi = [_dot(da[h], k_inb[h]) + dq_st[h] for h in hs]
            dki = [_dot_tn(da[h], q_inb[h]) for h in hs]
            dv = [_dot_tn(att[h], dob[h]) + dv_st[h] for h in hs]
            for h in hs:
                dd = jnp.sum(dst[h] * st[h], axis=0, keepdims=True)
                dst_scr[h] = dst[h] * dec[h] + dst_new[h]
                kk = dks[h] * k_st[h]
                dbl = jnp.sum(kk, axis=0, keepdims=True) + dd * dec[h]
                db = dqi[h] * q_in[h] - dki[h] * k_in[h] - kk
                dbs[h] = db + jnp.where(last_row, dbl, 0.0)
                dpg_ref[rows, h * GDK:(h + 1) * GDK] = (dqi[h] * eb[h]).astype(BF)
                dpg_ref[rows, 512 + h * GDK:512 + (h + 1) * GDK] = (dki[h] * enb[h] + dks[h] * ek[h]).astype(BF)
                dpg_ref[rows, 1024 + h * GDV:1024 + (h + 1) * GDV] = dv[h].astype(BF)
            dla = _tri_mm(utri_v, jnp.concatenate(dbs, axis=1))
            dlogit = dla * (1.0 / GTAU) * _sigmoid(-logit)
            dlb = dlogit.astype(BF)
            dpg_ref[rows, 3072:3200] = _dot_nt(dlb, wg_ref[...]).astype(BF)
            dwg_ref[...] += _dot_tn(r, dlb)
            dbg_ref[...] += jnp.sum(dlogit, axis=0, keepdims=True)
            return carry

        lax.fori_loop(0, nc, chunk, 0, unroll=True)

    full = lambda shp: pl.BlockSpec(shp, lambda b_, i: (0,) * len(shp))
    rev = lambda b_, i: (b_ * nb + nb - 1 - i, 0)
    return _call(
        body, name="gla_bwd", ride=ride, sem=("arbitrary", "arbitrary"),
        args=(pg, wg, bg, gn, ltri, utri, o, states, dzg),
        out_shape=(jax.ShapeDtypeStruct((T, PG_W), BF),
                   jax.ShapeDtypeStruct((128, 512), F32),
                   jax.ShapeDtypeStruct((1, 512), F32),
                   jax.ShapeDtypeStruct((1, GDV), F32)),
        grid=(nseq, nb),
        in_specs=[pl.BlockSpec((tm, PG_W), rev),
                  full((128, 512)), full((1, 512)), full((1, GDV)), full((GC, GC)), full((GC, GC)),
                  pl.BlockSpec((tm, GH * GDV), rev),
                  pl.BlockSpec((nc, GH, GDV, GDK), lambda b_, i: (b_ * nb + nb - 1 - i, 0, 0, 0)),
                  pl.BlockSpec((tm, GH * GDV), rev)],
        out_specs=(pl.BlockSpec((tm, PG_W), rev), full((128, 512)), full((1, 512)), full((1, GDV))),
        scratch_shapes=[pltpu.VMEM((GH, GDV, GDK), F32)])


def _rope_tables(pos, invf):
    ang = pos.astype(F32) * invf
    lane = _iota(ang.shape, 1)
    sin = jnp.sin(ang)
    ssin = jnp.where(lane < 32, -sin, jnp.where(lane < 64, sin, 0.0))
    return jnp.cos(ang), ssin, lane


def _rope(x, cos, ssin, lane, sign):
    rot = jnp.where(lane < 32, pltpu.roll(x, 96, 1), pltpu.roll(x, 32, 1))
    return x * cos + sign * (rot * ssin)


def _rms_fwd(x, g):
    rstd = lax.rsqrt(jnp.mean(x * x, axis=-1, keepdims=True) + RMS_EPS)
    return x * rstd * g, x * rstd, rstd


def _rms_bwd(dy, xhat, rstd, g):
    gd = dy * g
    return rstd * (gd - xhat * jnp.mean(gd * xhat, axis=-1, keepdims=True)), jnp.sum(dy * xhat, axis=0, keepdims=True)


def _mla_prep_fwd(pm, pos, invf, gq, gkv, wuq, wukv, *, tm):
    T = pm.shape[0]

    def body(pm_ref, pos_ref, invf_ref, gq_ref, gkv_ref, wuq_ref, wukv_ref, qc_ref, kc_ref, v_ref):
        cos, ssin, lane = _rope_tables(pos_ref[...], invf_ref[...])
        cq, _, _ = _rms_fwd(pm_ref[:, 0:MQR], gq_ref[...])
        ckv, _, _ = _rms_fwd(pm_ref[:, 512:768], gkv_ref[...])
        qf = _dot(cq.astype(BF), wuq_ref[...])
        kvf = _dot(ckv.astype(BF), wukv_ref[...])
        kr = _rope(pm_ref[:, 384:512], cos, ssin, lane, 1.0).astype(BF)
        for h in range(MH):
            qc_ref[:, 256 * h:256 * h + 128] = (QK_SCALE_LOG2 * qf[:, 128 * h:128 * h + 128]).astype(BF)
            qr = qf[:, 1024 + 128 * h:1024 + 128 * h + 128]
            qc_ref[:, 256 * h + 128:256 * h + 256] = (QK_SCALE_LOG2 * _rope(qr, cos, ssin, lane, 1.0)).astype(BF)
            kc_ref[:, 256 * h:256 * h + 128] = kvf[:, 128 * h:128 * h + 128].astype(BF)
            kc_ref[:, 256 * h + 128:256 * h + 256] = kr
        v_ref[...] = kvf[:, 1024:2048].astype(BF)

    full = lambda shp: pl.BlockSpec(shp, lambda i: (0,) * len(shp))
    row = lambda w: pl.BlockSpec((tm, w), lambda i: (i, 0))
    return pl.pallas_call(
        body, name="mla_prep_fwd",
        out_shape=(jax.ShapeDtypeStruct((T, MH * 256), BF), jax.ShapeDtypeStruct((T, MH * 256), BF),
                   jax.ShapeDtypeStruct((T, MH * MV), BF)),
        grid=(T // tm,),
        in_specs=[row(PM_W), row(1), full((1, 128)), full((1, MQR)), full((1, MKR)),
                  full((MQR, 2048)), full((MKR, 2048))],
        out_specs=(row(MH * 256), row(MH * 256), row(MH * MV)),
        compiler_params=_params(("parallel",)),
    )(pm, pos, invf, gq, gkv, wuq, wukv)


def _mla_prep_bwd(pm, pos, invf, gq, gkv, wuq, wukv, dqc, dkc, dv, *, tm):
    T = pm.shape[0]

    def body(pm_ref, pos_ref, invf_ref, gq_ref, gkv_ref, wuq_ref, wukv_ref, dqc_ref, dkc_ref, dv_ref,
             dpm_ref, dwuq_ref, dwukv_ref, dgq_ref, dgkv_ref):
        @pl.when(pl.program_id(0) == 0)
        def _():
            dwuq_ref[...] = jnp.zeros_like(dwuq_ref)
            dwukv_ref[...] = jnp.zeros_like(dwukv_ref)
            dgq_ref[...] = jnp.zeros_like(dgq_ref)
            dgkv_ref[...] = jnp.zeros_like(dgkv_ref)

        cos, ssin, lane = _rope_tables(pos_ref[...], invf_ref[...])
        cq, cqh, cq_rstd = _rms_fwd(pm_ref[:, 0:MQR], gq_ref[...])
        ckv, ckvh, ckv_rstd = _rms_fwd(pm_ref[:, 512:768], gkv_ref[...])
        dqn, dqr, dkn = [], [], []
        dkr = jnp.zeros((tm, 128), F32)
        for h in range(MH):
            dqn.append(dqc_ref[:, 256 * h:256 * h + 128].astype(BF))
            dqr.append(_rope(dqc_ref[:, 256 * h + 128:256 * h + 256], cos, ssin, lane, -1.0).astype(BF))
            dkn.append(dkc_ref[:, 256 * h:256 * h + 128].astype(BF))
            dkr = dkr + dkc_ref[:, 256 * h + 128:256 * h + 256]
        dqf = jnp.concatenate(dqn + dqr, axis=1)
        dkvf = jnp.concatenate(dkn + [dv_ref[...].astype(BF)], axis=1)
        dwuq_ref[...] += _dot_tn(cq.astype(BF), dqf)
        dwukv_ref[...] += _dot_tn(ckv.astype(BF), dkvf)
        dcq, dgq = _rms_bwd(_dot_nt(dqf, wuq_ref[...]), cqh, cq_rstd, gq_ref[...])
        dckv, dgkv = _rms_bwd(_dot_nt(dkvf, wukv_ref[...]), ckvh, ckv_rstd, gkv_ref[...])
        dgq_ref[...] += dgq
        dgkv_ref[...] += dgkv
        dpm_ref[:, 0:MQR] = dcq.astype(BF)
        dpm_ref[:, 384:512] = _rope(dkr, cos, ssin, lane, -1.0).astype(BF)
        dpm_ref[:, 512:768] = dckv.astype(BF)

    full = lambda shp: pl.BlockSpec(shp, lambda i: (0,) * len(shp))
    row = lambda w: pl.BlockSpec((tm, w), lambda i: (i, 0))
    return pl.pallas_call(
        body, name="mla_prep_bwd",
        out_shape=(jax.ShapeDtypeStruct((T, PM_W), BF), jax.ShapeDtypeStruct((MQR, 2048), F32),
                   jax.ShapeDtypeStruct((MKR, 2048), F32), jax.ShapeDtypeStruct((1, MQR), F32),
                   jax.ShapeDtypeStruct((1, MKR), F32)),
        grid=(T // tm,),
        in_specs=[row(PM_W), row(1), full((1, 128)), full((1, MQR)), full((1, MKR)),
                  full((MQR, 2048)), full((MKR, 2048)), row(MH * 256), row(MH * 256), row(MH * MV)],
        out_specs=(row(PM_W), full((MQR, 2048)), full((MKR, 2048)), full((1, MQR)), full((1, MKR))),
        compiler_params=_params(("arbitrary",)),
    )(pm, pos, invf, gq, gkv, wuq, wukv, dqc, dkc, dv)


def _flash_fwd(qc, kc, v, *, nseq, S, tq, ride=None):
    T = qc.shape[0]
    nq = S // tq
    hp = FLASH_HP_FWD

    def body(q_ref, k_ref, v_ref, o_ref, lse_ref):
        i = pl.program_id(2)
        causal = _iota((tq, tq), 0) >= _iota((tq, tq), 1)

        def step(j, carry, masked):
            rows = pl.ds(pl.multiple_of(j * tq, tq), tq)
            hs = range(hp)
            s = [_dot_nt(q_ref[:, 256 * hh:256 * hh + 256], k_ref[rows, 256 * hh:256 * hh + 256]) for hh in hs]
            p, stats = [], []
            for hh in hs:
                m, l, _ = carry[hh]
                sh = jnp.where(causal, s[hh], NEG) if masked else s[hh]
                m_new = jnp.maximum(m, jnp.max(sh, axis=-1, keepdims=True))
                ph = jnp.exp2(sh - m_new)
                a = jnp.exp2(m - m_new)
                stats.append((m_new, a * l + jnp.sum(ph, axis=-1, keepdims=True), a))
                p.append(ph.astype(BF))
            pv = [_dot(p[hh], v_ref[rows, MV * hh:MV * hh + MV]) for hh in hs]
            return tuple((stats[hh][0], stats[hh][1], stats[hh][2] * carry[hh][2] + pv[hh]) for hh in hs)

        init = ((jnp.full((tq, 1), NEG, F32), jnp.zeros((tq, 1), F32), jnp.zeros((tq, MV), F32)),) * hp
        carry = lax.fori_loop(0, i, lambda j, c: step(j, c, False), init)
        for hh, (m, l, acc) in enumerate(step(i, carry, True)):
            o_ref[:, MV * hh:MV * hh + MV] = (acc / l).astype(BF)
            lse_ref[:, 128 * hh:128 * hh + 128] = jnp.broadcast_to(m + jnp.log2(l), (tq, 128))

    return _call(
        body, name="flash_fwd", ride=ride, sem=("parallel", "parallel", "arbitrary"), args=(qc, kc, v),
        out_shape=(jax.ShapeDtypeStruct((T, MH * MV), BF), jax.ShapeDtypeStruct((T, MH * 128), F32)),
        grid=(nseq, MH // hp, nq),
        in_specs=[pl.BlockSpec((tq, 256 * hp), lambda b_, h, i: (b_ * nq + i, h)),
                  pl.BlockSpec((S, 256 * hp), lambda b_, h, i: (b_, h)),
                  pl.BlockSpec((S, MV * hp), lambda b_, h, i: (b_, h))],
        out_specs=(pl.BlockSpec((tq, MV * hp), lambda b_, h, i: (b_ * nq + i, h)),
                   pl.BlockSpec((tq, 128 * hp), lambda b_, h, i: (b_ * nq + i, h))))


def _flash_bwd(qc, kc, v, o, do, lse, *, nseq, S, tq, ride=None):
    T = qc.shape[0]
    nq = S // tq

    def body(q_ref, k_ref, v_ref, o_ref, do_ref, lse_ref, dq_ref, dk_ref, dv_ref, dq_scr, delta_scr):
        j = pl.program_id(2)

        @pl.when(j == 0)
        def _():
            for hh in range(FLASH_HP):
                od = o_ref[:, MV * hh:MV * hh + MV].astype(F32) * do_ref[:, MV * hh:MV * hh + MV].astype(F32)
                delta_scr[:, 128 * hh:128 * hh + 128] = jnp.broadcast_to(jnp.sum(od, axis=-1, keepdims=True), (S, 128))

        causal = _iota((tq, tq), 0) >= _iota((tq, tq), 1)

        def step(i, carry, masked):
            rows = pl.ds(pl.multiple_of(i * tq, tq), tq)
            hs = range(FLASH_HP)
            qs = [slice(256 * hh, 256 * hh + 256) for hh in hs]
            vs = [slice(MV * hh, MV * hh + MV) for hh in hs]
            ls = [slice(128 * hh, 128 * hh + 1) for hh in hs]
            s = [_dot_nt(q_ref[rows, qs[hh]], k_ref[:, qs[hh]]) for hh in hs]
            dp = [_dot_nt(do_ref[rows, vs[hh]], v_ref[:, vs[hh]]) for hh in hs]
            pb, ds = [], []
            for hh in hs:
                p = jnp.exp2(s[hh] - lse_ref[rows, ls[hh]])
                if masked:
                    p = jnp.where(causal, p, 0.0)
                pb.append(p.astype(BF))
                ds.append((p * (dp[hh] - delta_scr[rows, ls[hh]])).astype(BF))
            dv = [carry[hh][1] + _dot_tn(pb[hh], do_ref[rows, vs[hh]]) for hh in hs]
            dk = [carry[hh][0] + _dot_tn(ds[hh], q_ref[rows, qs[hh]]) for hh in hs]
            for hh in hs:
                dqh = _dot(ds[hh], k_ref[:, qs[hh]])

                @pl.when(j == 0)
                def _(hh=hh, dqh=dqh):
                    dq_scr[rows, qs[hh]] = dqh

                @pl.when(j > 0)
                def _(hh=hh, dqh=dqh):
                    dq_scr[rows, qs[hh]] += dqh
            return tuple((dk[hh], dv[hh]) for hh in hs)

        init = ((jnp.zeros((tq, 256), F32), jnp.zeros((tq, MV), F32)),) * FLASH_HP
        carry = step(j, init, True)
        carry = lax.fori_loop(j + 1, nq, lambda i, c: step(i, c, False), carry)
        for hh, (dk, dv) in enumerate(carry):
            dk_ref[:, 256 * hh:256 * hh + 256] = dk * (1.0 / LOG2E)
            dv_ref[:, MV * hh:MV * hh + MV] = dv

        @pl.when(j == nq - 1)
        def _():
            dq_ref[...] = dq_scr[...] * QK_SCALE

    hp = FLASH_HP
    seq = lambda w: pl.BlockSpec((S, w * hp), lambda b_, h, j: (b_, h))
    blk = lambda w: pl.BlockSpec((tq, w * hp), lambda b_, h, j: (b_ * nq + j, h))
    return _call(
        body, name="flash_bwd", ride=ride, sem=("parallel", "parallel", "arbitrary"), args=(qc, kc, v, o, do, lse),
        out_shape=(jax.ShapeDtypeStruct((T, MH * 256), F32), jax.ShapeDtypeStruct((T, MH * 256), F32),
                   jax.ShapeDtypeStruct((T, MH * MV), F32)),
        grid=(nseq, MH // hp, nq),
        in_specs=[seq(256), blk(256), blk(MV), seq(MV), seq(MV), seq(128)],
        out_specs=(seq(256), blk(256), blk(MV)),
        scratch_shapes=[pltpu.VMEM((S, 256 * hp), F32), pltpu.VMEM((S, 128 * hp), F32)])


def _ln_fwd(pre, g, b):
    mu = jnp.mean(pre, axis=-1, keepdims=True)
    xc = pre - mu
    rstd = lax.rsqrt(jnp.mean(xc * xc, axis=-1, keepdims=True) + LN_EPS)
    xhat = xc * rstd
    return xhat * g + b, xhat, rstd


def _ln_bwd(dy, xhat, rstd, g):
    dxh = dy * g
    dx = rstd * (dxh - jnp.mean(dxh, axis=-1, keepdims=True) - xhat * jnp.mean(dxh * xhat, axis=-1, keepdims=True))
    return dx, jnp.sum(dy * xhat, axis=0, keepdims=True), jnp.sum(dy, axis=0, keepdims=True)


def _post_attn_fwd(zg, attn, pt, x, wgo, wmo, wout, g1, b1, *, tm, ride=None):
    T = x.shape[0]

    def body(zg_ref, at_ref, pt_ref, x_ref, wgo_ref, wmo_ref, wout_ref, g_ref, b_ref,
             yg_ref, ym_ref, mix_ref, pre_ref, hb_ref):
        yg = _dot(zg_ref[...], wgo_ref[...])
        ym = _dot(at_ref[...], wmo_ref[...])
        mix = (_sigmoid(pt_ref[:, 0:D].astype(F32)) * yg + _sigmoid(pt_ref[:, D:2 * D].astype(F32)) * ym).astype(BF)
        pre = ALPHA * x_ref[...] + _dot(mix, wout_ref[...])
        h, _, _ = _ln_fwd(pre, g_ref[...], b_ref[...])
        yg_ref[...] = yg.astype(BF)
        ym_ref[...] = ym.astype(BF)
        mix_ref[...] = mix
        pre_ref[...] = pre
        hb_ref[...] = h.astype(BF)

    full = lambda shp: pl.BlockSpec(shp, lambda i: (0,) * len(shp))
    row = lambda w: pl.BlockSpec((tm, w), lambda i: (i, 0))
    sd = lambda dt: jax.ShapeDtypeStruct((T, D), dt)
    return _call(
        body, name="post_attn_fwd", ride=ride, sem=("parallel",), args=(zg, attn, pt, x, wgo, wmo, wout, g1, b1),
        out_shape=(sd(BF), sd(BF), sd(BF), sd(F32), sd(BF)),
        grid=(T // tm,),
        in_specs=[row(D), row(D), row(PT_W), row(D), full((D, D)), full((D, D)), full((D, D)),
                  full((1, D)), full((1, D))],
        out_specs=(row(D),) * 5)


def _post_attn_bwd(dh, pre, pt, yg, ym, wgo, wmo, wout, g1, *, tm):
    T = dh.shape[0]

    def body(dh_ref, pre_ref, pt_ref, yg_ref, ym_ref, wgo_ref, wmo_ref, wout_ref, g_ref,
             dx_ref, dpreb_ref, dpt_ref, dygb_ref, dymb_ref, dzg_ref, dat_ref, dg_ref, db_ref):
        @pl.when(pl.program_id(0) == 0)
        def _():
            dg_ref[...] = jnp.zeros_like(dg_ref)
            db_ref[...] = jnp.zeros_like(db_ref)

        pre = pre_ref[...]
        mu = jnp.mean(pre, axis=-1, keepdims=True)
        xc = pre - mu
        rstd = lax.rsqrt(jnp.mean(xc * xc, axis=-1, keepdims=True) + LN_EPS)
        dpre, dg, db = _ln_bwd(dh_ref[...], xc * rstd, rstd, g_ref[...])
        dg_ref[...] += dg
        db_ref[...] += db
        dx_ref[...] = ALPHA * dpre
        dpreb = dpre.astype(BF)
        dpreb_ref[...] = dpreb
        dmix = _dot_nt(dpreb, wout_ref[...])
        sa = _sigmoid(pt_ref[:, 0:D].astype(F32))
        sb = _sigmoid(pt_ref[:, D:2 * D].astype(F32))
        dpt_ref[:, 0:D] = (dmix * yg_ref[...].astype(F32) * (sa * (1.0 - sa))).astype(BF)
        dpt_ref[:, D:2 * D] = (dmix * ym_ref[...].astype(F32) * (sb * (1.0 - sb))).astype(BF)
        dyg = (dmix * sa).astype(BF)
        dym = (dmix * sb).astype(BF)
        dygb_ref[...] = dyg
        dymb_ref[...] = dym
        dzg_ref[...] = _dot_nt(dyg, wgo_ref[...]).astype(BF)
        dat_ref[...] = _dot_nt(dym, wmo_ref[...]).astype(BF)

    full = lambda shp: pl.BlockSpec(shp, lambda i: (0,) * len(shp))
    row = lambda w: pl.BlockSpec((tm, w), lambda i: (i, 0))
    sd = lambda w, dt: jax.ShapeDtypeStruct((T, w), dt)
    return pl.pallas_call(
        body, name="post_attn_bwd",
        out_shape=(sd(D, F32), sd(D, BF), sd(PT_W, BF), sd(D, BF), sd(D, BF), sd(D, BF), sd(D, BF),
                   jax.ShapeDtypeStruct((1, D), F32), jax.ShapeDtypeStruct((1, D), F32)),
        grid=(T // tm,),
        in_specs=[row(D), row(D), row(PT_W), row(D), row(D), full((D, D)), full((D, D)), full((D, D)),
                  full((1, D))],
        out_specs=(row(D), row(D), row(PT_W), row(D), row(D), row(D), row(D), full((1, D)), full((1, D))),
        compiler_params=_params(("arbitrary",)),
    )(dh, pre, pt, yg, ym, wgo, wmo, wout, g1)


def _shift_down(u, prev, k):
    r = pltpu.roll(u, k, 0)
    p = pltpu.roll(prev, k, 0)
    head = jnp.where(_iota(p.shape, 0) < k, p, r[0:8, :])
    return jnp.concatenate([head, r[8:, :]], axis=0)


def _conv3(u, prev, w_ref, b_ref):
    return (w_ref[0:1, :] * _shift_down(u, prev, 2) + w_ref[1:2, :] * _shift_down(u, prev, 1)
            + w_ref[2:3, :] * u + b_ref[...])


def _ffn_up_fwd(hb, wug, wuv, cw, cb, *, S, tm, tn):
    T = hb.shape[0]
    nj, nbs = DFF // tn, S // tm

    def body(h_ref, wg_ref, wv_ref, cwg_ref, cwv_ref, cbg_ref, cbv_ref,
             ug_ref, uv_ref, ucg_ref, ucv_ref, f_ref, pg_scr, pv_scr):
        @pl.when(pl.program_id(1) % nbs == 0)
        def _():
            pg_scr[...] = jnp.zeros_like(pg_scr)
            pv_scr[...] = jnp.zeros_like(pv_scr)

        h = h_ref[...]
        ug = _dot(h, wg_ref[...])
        uv = _dot(h, wv_ref[...])
        ucg = _conv3(ug, pg_scr[...], cwg_ref, cbg_ref)
        ucv = _conv3(uv, pv_scr[...], cwv_ref, cbv_ref)
        pg_scr[...] = ug[tm - 8:, :]
        pv_scr[...] = uv[tm - 8:, :]
        ug_ref[...] = ug.astype(BF)
        uv_ref[...] = uv.astype(BF)
        ucg_ref[...] = ucg
        ucv_ref[...] = ucv
        f_ref[...] = (ucg * _sigmoid(ucg) * ucv).astype(BF)

    tile = pl.BlockSpec((tm, tn), lambda j, i: (i, j))
    return pl.pallas_call(
        body, name="ffn_up_fwd",
        out_shape=(jax.ShapeDtypeStruct((T, DFF), BF), jax.ShapeDtypeStruct((T, DFF), BF),
                   jax.ShapeDtypeStruct((T, DFF), F32), jax.ShapeDtypeStruct((T, DFF), F32),
                   jax.ShapeDtypeStruct((T, DFF), BF)),
        grid=(nj, T // tm),
        in_specs=[pl.BlockSpec((tm, D), lambda j, i: (i, 0)),
                  pl.BlockSpec((D, tn), lambda j, i: (0, j)), pl.BlockSpec((D, tn), lambda j, i: (0, j)),
                  pl.BlockSpec((3, tn), lambda j, i: (0, j)), pl.BlockSpec((3, tn), lambda j, i: (0, j + nj)),
                  pl.BlockSpec((1, tn), lambda j, i: (0, j)), pl.BlockSpec((1, tn), lambda j, i: (0, j + nj))],
        out_specs=(tile, tile, tile, tile, tile),
        scratch_shapes=[pltpu.VMEM((8, tn), F32), pltpu.VMEM((8, tn), F32)],
        compiler_params=_params(("parallel", "arbitrary")),
    )(hb, wug, wuv, cw, cw, cb, cb)


def _ffn_bwd(dpreb, wd, ug, uv, ucg, ucv, cw, *, S, tm, tn):
    T = dpreb.shape[0]
    nj, nb, nbs = DFF // tn, T // tm, S // tm
    r_, c_ = lax.broadcasted_iota(jnp.int32, (tm, tm), 0), lax.broadcasted_iota(jnp.int32, (tm, tm), 1)
    s1, s2 = (c_ == r_ + 1).astype(BF), (c_ == r_ + 2).astype(BF)

    def body(dp_ref, wd_ref, ug_ref, uv_ref, ucg_ref, ucv_ref, cwg_ref, cwv_ref, s1_ref, s2_ref,
             dug_ref, duv_ref, dcg_ref, dcv_ref, ng_scr, nv_scr):
        ii = pl.program_id(1)
        i = nb - 1 - ii
        tail_row = _iota((8, tn), 0)

        @pl.when(ii == 0)
        def _():
            dcg_ref[...] = jnp.zeros_like(dcg_ref)
            dcv_ref[...] = jnp.zeros_like(dcv_ref)

        @pl.when(i % nbs == nbs - 1)
        def _():
            ng_scr[...] = jnp.zeros_like(ng_scr)
            nv_scr[...] = jnp.zeros_like(nv_scr)

        df = _dot_nt(dp_ref[...], wd_ref[...])
        ucg = ucg_ref[...]
        sg = _sigmoid(ucg)
        ducg = df * ucv_ref[...] * (sg * (1.0 + ucg * (1.0 - sg)))
        ducv = df * (ucg * sg)

        def finish(duc, u_ref, w, nxt_scr, du_ref, dc_ref):
            nxt = nxt_scr[...]
            db = duc.astype(BF)

            def shifted(s_ref, k):
                r = _dot(s_ref[...], db)
                tail = jnp.where(tail_row >= 8 - k, pltpu.roll(nxt, 8 - k, 0), r[tm - 8:, :])
                return jnp.concatenate([r[:tm - 8, :], tail], axis=0)

            up1 = shifted(s1_ref, 1)
            up2 = shifted(s2_ref, 2)
            du_ref[...] = (w[2:3, :] * duc + w[1:2, :] * up1 + w[0:1, :] * up2).astype(BF)
            nxt_scr[...] = duc[0:8, :]
            u = u_ref[...].astype(F32)
            for row, z in enumerate((u * up2, u * up1, u * duc, duc)):
                dc_ref[row:row + 1, :] += jnp.sum(z, axis=0, keepdims=True)

        finish(ducg, ug_ref, cwg_ref, ng_scr, dug_ref, dcg_ref)
        finish(ducv, uv_ref, cwv_ref, nv_scr, duv_ref, dcv_ref)

    tile = pl.BlockSpec((tm, tn), lambda j, ii: (nb - 1 - ii, j))
    acc = pl.BlockSpec((8, tn), lambda j, ii: (0, j))
    return pl.pallas_call(
        body, name="ffn_bwd",
        out_shape=(jax.ShapeDtypeStruct((T, DFF), BF), jax.ShapeDtypeStruct((T, DFF), BF),
                   jax.ShapeDtypeStruct((8, DFF), F32), jax.ShapeDtypeStruct((8, DFF), F32)),
        grid=(nj, nb),
        in_specs=[pl.BlockSpec((tm, D), lambda j, ii: (nb - 1 - ii, 0)),
                  pl.BlockSpec((tn, D), lambda j, ii: (j, 0)),
                  tile, tile, tile, tile,
                  pl.BlockSpec((3, tn), lambda j, ii: (0, j)), pl.BlockSpec((3, tn), lambda j, ii: (0, j + nj)),
                  pl.BlockSpec((tm, tm), lambda j, ii: (0, 0)), pl.BlockSpec((tm, tm), lambda j, ii: (0, 0))],
        out_specs=(tile, tile, acc, acc),
        scratch_shapes=[pltpu.VMEM((8, tn), F32), pltpu.VMEM((8, tn), F32)],
        compiler_params=_params(("parallel", "arbitrary")),
    )(dpreb, wd, ug, uv, ucg, ucv, cw, cw, s1, s2)


def _down_ln2_loss(f_in, wd, pre1, target, g1, b1, g2, b2, *, tm):
    T = pre1.shape[0]

    def body(f_ref, wd_ref, p1_ref, t_ref, g1_ref, b1_ref, g_ref, b_ref, dpb_ref, dh_ref, loss_ref, dg_ref, db_ref):
        @pl.when(pl.program_id(0) == 0)
        def _():
            loss_ref[...] = jnp.zeros_like(loss_ref)
            dg_ref[...] = jnp.zeros_like(dg_ref)
            db_ref[...] = jnp.zeros_like(db_ref)

        halves = [pl.ds(s * (tm // 2), tm // 2) for s in range(2)]
        f = [_dot(f_ref[hs, :], wd_ref[...]) for hs in halves]
        for hs, fh in zip(halves, f):
            h, _, _ = _ln_fwd(p1_ref[hs, :], g1_ref[...], b1_ref[...])
            pre = ALPHA * h + fh
            out, xhat, rstd = _ln_fwd(pre, g_ref[...], b_ref[...])
            diff = out - t_ref[hs, :]
            loss_ref[...] += 0.5 * jnp.sum(jnp.mean(diff * diff, axis=-1, keepdims=True))
            dpre, dg, db = _ln_bwd(diff * (1.0 / D), xhat, rstd, g_ref[...])
            dg_ref[...] += dg
            db_ref[...] += db
            dpb_ref[hs, :] = dpre.astype(BF)
            dh_ref[hs, :] = ALPHA * dpre

    full = lambda shp: pl.BlockSpec(shp, lambda i: (0,) * len(shp))
    row = lambda w: pl.BlockSpec((tm, w), lambda i: (i, 0))
    return pl.pallas_call(
        body, name="down_ln2_loss",
        out_shape=(jax.ShapeDtypeStruct((T, D), BF), jax.ShapeDtypeStruct((T, D), F32),
                   jax.ShapeDtypeStruct((8, 128), F32), jax.ShapeDtypeStruct((1, D), F32),
                   jax.ShapeDtypeStruct((1, D), F32)),
        grid=(T // tm,),
        in_specs=[row(DFF), full((DFF, D)), row(D), row(D), full((1, D)), full((1, D)), full((1, D)), full((1, D))],
        out_specs=(row(D), row(D), full((8, 128)), full((1, D)), full((1, D))),
        compiler_params=_params(("arbitrary",)),
    )(f_in, wd, pre1, target, g1, b1, g2, b2)


def _adamw(parts, w, m, v, *, name):
    n, R, C = parts.shape
    tr, tc = R, C
    for cand in range(min(R, 256), 15, -1):
        if R % cand == 0 and cand % 16 == 0:
            tr = cand
            break
    if tr == R and R * C > 65536 and C % 256 == 0:
        tc = 256
    c1 = 1.0 - ADAM_B1 ** ADAM_STEP
    c2 = 1.0 - ADAM_B2 ** ADAM_STEP

    def body(p_ref, w_ref, m_ref, v_ref, g_ref, d_ref, nm_ref, nv_ref):
        g = p_ref[0].astype(F32)
        for s in range(1, n):
            g = g + p_ref[s].astype(F32)
        nm = ADAM_B1 * m_ref[...] + (1.0 - ADAM_B1) * g
        nv = ADAM_B2 * v_ref[...] + (1.0 - ADAM_B2) * (g * g)
        g_ref[...] = g
        nm_ref[...] = nm
        nv_ref[...] = nv
        d_ref[...] = -ADAM_LR * ((nm / c1) / (jnp.sqrt(nv / c2) + ADAM_EPS) + ADAM_WD * w_ref[...])

    blk = pl.BlockSpec((tr, tc), lambda i, j: (i, j))
    sd = jax.ShapeDtypeStruct((R, C), F32)
    return pl.pallas_call(
        body, name=name,
        out_shape=(sd, sd, sd, sd),
        grid=(R // tr, C // tc),
        in_specs=[pl.BlockSpec((n, tr, tc), lambda i, j: (0, i, j)), blk, blk, blk],
        out_specs=(blk, blk, blk, blk),
        compiler_params=_params(("parallel", "parallel")),
    )(parts, w, m, v)


class _Exchange:
    def __init__(self, items):
        self.items = [(src if sc else [(src, 0)], sc) for src, sc in items]
        self.arrays = [arr for srcs, _ in self.items for arr, _ in srcs]
        self.n = len(self.items)
        self.n_in = len(self.arrays)

    def out_shape(self):
        return tuple(jax.ShapeDtypeStruct((NDEV,) + (srcs[0][0].shape[1:] if sc else srcs[0][0].shape),
                                          srcs[0][0].dtype) for srcs, sc in self.items)

    def scratch(self):
        return [pltpu.SemaphoreType.DMA((self.n, NDEV - 1)), pltpu.SemaphoreType.DMA((self.n, NDEV - 1)),
                pltpu.SemaphoreType.DMA((self.n,))]

    def _emit(self, ins, outs, sems, phase):
        send_sems, recv_sems, loc_sems = sems
        x, y, c = lax.axis_index("x"), lax.axis_index("y"), lax.axis_index("c")
        me = 4 * x + 2 * y + c
        flip = lambda p, d: 1 - p if d else p

        def inside(p, lo, n):
            return None if (lo, n) == (0, NDEV) else jnp.logical_and(p >= lo, p < lo + n)

        def when(cond, fn):
            if cond is None:
                fn()
            else:
                pl.when(cond)(fn)

        pos = 0
        for a, (srcs, sc) in enumerate(self.items):
            refs = ins[pos:pos + len(srcs)]
            pos += len(srcs)
            ranges = [(lo, arr.shape[0]) if sc else (0, NDEV) for arr, lo in srcs]
            mine = [inside(me, lo, n) for lo, n in ranges]
            i_receive = None if None in mine else functools.reduce(jnp.logical_or, mine)
            for ref, (lo, n), cond in zip(refs, ranges, mine):
                def local(ref=ref, lo=lo):
                    cp = pltpu.make_async_copy(ref.at[me - lo] if sc else ref, outs[a].at[me], loc_sems.at[a])
                    cp.start() if phase == 0 else cp.wait()
                if phase != 1:
                    when(cond, local)
            for k in range(1, NDEV):
                px, py, pc = flip(x, k & 4), flip(y, k & 2), flip(c, k & 1)
                peer = 4 * px + 2 * py + pc
                mk = functools.partial(pltpu.make_async_remote_copy,
                                       send_sem=send_sems.at[a, k - 1], recv_sem=recv_sems.at[a, k - 1],
                                       device_id=(px, py, pc), device_id_type=MESH_ID)
                if phase == 1:
                    def arrival(mk=mk, peer=peer):
                        mk(src_ref=refs[0].at[0] if sc else refs[0], dst_ref=outs[a].at[peer]).wait_recv()
                    when(i_receive, arrival)
                    continue
                for ref, (lo, n) in zip(refs, ranges):
                    def send(mk=mk, ref=ref, lo=lo, peer=peer):
                        cp = mk(src_ref=ref.at[peer - lo] if sc else ref, dst_ref=outs[a].at[me])
                        cp.start() if phase == 0 else cp.wait_send()
                    when(inside(peer, lo, n), send)

    def start(self, ins, outs, sems):
        self._emit(ins, outs, sems, 0)

    def wait(self, ins, outs, sems):
        self._emit(ins, outs, sems, 1)
        self._emit(ins, outs, sems, 2)


def _call(body, *, name, grid, in_specs, out_specs, out_shape, args, scratch_shapes=(), sem=None, ride=None):
    if ride is None:
        return pl.pallas_call(body, name=name, grid=grid, in_specs=list(in_specs), out_specs=tuple(out_specs),
                              out_shape=tuple(out_shape), scratch_shapes=list(scratch_shapes),
                              compiler_params=_params(sem))(*args)
    n_in, n_out, n_scr, ne, ne_in = len(args), len(out_shape), len(scratch_shapes), ride.n, ride.n_in

    def ride_body(*refs):
        ins, ex_in = refs[:n_in], refs[n_in:n_in + ne_in]
        o0 = n_in + ne_in
        outs, ex_out = refs[o0:o0 + n_out], refs[o0 + n_out:o0 + n_out + ne]
        scr = refs[o0 + n_out + ne:o0 + n_out + ne + n_scr]
        sems = refs[o0 + n_out + ne + n_scr:]
        first = functools.reduce(jnp.logical_and, [pl.program_id(d) == 0 for d in range(len(grid))])
        last = functools.reduce(jnp.logical_and, [pl.program_id(d) == grid[d] - 1 for d in range(len(grid))])

        @pl.when(first)
        def _():
            ride.start(ex_in, ex_out, sems)

        body(*ins, *outs, *scr)

        @pl.when(last)
        def _():
            ride.wait(ex_in, ex_out, sems)

    anyspec = pl.BlockSpec(memory_space=pl.ANY)
    res = pl.pallas_call(
        ride_body, name=name, grid=grid,
        in_specs=list(in_specs) + [anyspec] * ne_in,
        out_specs=tuple(out_specs) + (anyspec,) * ne,
        out_shape=tuple(out_shape) + ride.out_shape(),
        scratch_shapes=list(scratch_shapes) + ride.scratch(),
        compiler_params=_params(("arbitrary",) * len(grid)),
    )(*args, *ride.arrays)
    return tuple(res[:n_out]), tuple(res[n_out:])


def _gather_two_level(arrays, *, name):
    n = len(arrays)

    def body(*refs):
        ins, outs = refs[:n], refs[n:2 * n]
        send_sems, recv_sems, loc_sems = refs[2 * n:]
        x, y, c = lax.axis_index("x"), lax.axis_index("y"), lax.axis_index("c")
        sibling = (x, y, 1 - c)
        chips = [(1 - x, y), (x, 1 - y), (1 - x, 1 - y)]
        idx = lambda px, py, pc: 4 * px + 2 * py + pc
        me = idx(x, y, c)

        def copy(a, k, block, to, src=None):
            return pltpu.make_async_remote_copy(
                src_ref=outs[a].at[block] if src is None else src, dst_ref=outs[a].at[block],
                send_sem=send_sems.at[a, k], recv_sem=recv_sems.at[a, k], device_id=to, device_id_type=MESH_ID)

        local = [pltpu.make_async_copy(ins[a], outs[a].at[me], loc_sems.at[a]) for a in range(n)]
        sent = []
        for a in range(n):
            sent.append(copy(a, 0, me, sibling, src=ins[a]))
            sent += [copy(a, 1 + j, me, (*chip, c), src=ins[a]) for j, chip in enumerate(chips)]
        for cp in local + sent:
            cp.start()
        for j, chip in enumerate(chips):
            for a in range(n):
                copy(a, 1 + j, idx(*chip, c), sibling).wait_recv()
                passed = copy(a, 4 + j, idx(*chip, c), sibling)
                passed.start()
                sent.append(passed)
        for a in range(n):
            copy(a, 0, idx(x, y, 1 - c), sibling).wait_recv()
            for j, chip in enumerate(chips):
                copy(a, 4 + j, idx(*chip, 1 - c), sibling).wait_recv()
        for cp in sent:
            cp.wait_send()
        for cp in local:
            cp.wait()

    anyspec = pl.BlockSpec(memory_space=pl.ANY)
    return pl.pallas_call(
        body, name=name,
        out_shape=tuple(jax.ShapeDtypeStruct((NDEV,) + a.shape, a.dtype) for a in arrays),
        in_specs=[anyspec] * n, out_specs=(anyspec,) * n,
        scratch_shapes=[pltpu.SemaphoreType.DMA((n, NDEV - 1)), pltpu.SemaphoreType.DMA((n, NDEV - 1)),
                        pltpu.SemaphoreType.DMA((n,))],
    )(*arrays)


def _tri_consts():
    r = lax.broadcasted_iota(jnp.int32, (GC, GC), 0)
    c = lax.broadcasted_iota(jnp.int32, (GC, GC), 1)
    return (r >= c).astype(BF), (r <= c).astype(BF)


def _local_step(x, positions, target, w, hooks=None):
    g = {}

    def run(host, fn, *a, **kw):
        h = None if hooks is None else hooks.get(host)
        if h is None:
            return fn(*a, **kw)
        out, received = fn(*a, ride=_Exchange(h[0](w, g)), **kw)
        h[1](received, w, g)
        return out

    nseq, S, _ = x.shape
    T = nseq * S
    tm = min(TOKEN_TM, S)
    tq = min(FLASH_TQ, S)
    x2 = x.reshape(T, D)
    pos = positions.reshape(T, 1)
    half = ROPE // 2
    inv = THETA ** (-jnp.arange(half, dtype=F32) / half)
    invf = jnp.concatenate([inv, inv, jnp.zeros((64,), F32)]).reshape(1, 128)
    ltri, utri = _tri_consts()

    pt, xb = _matmul(x2, w["w_tt"], "nt", name="proj_t", out_dtype=BF, tm=1024, tn=1024, tk=1024, emit_a=True)
    pg = run("proj_g", _matmul, xb, w["w_gt"], "nt", name="proj_g", tm=1024, tn=640, tk=1024)
    pm = _matmul(xb, w["w_mt"], "nt", name="proj_m", tm=1024, tn=768, tk=1024)
    o, zg, states = run("gla_fwd", _gla_fwd, pg, w["wg"], w["bg"], w["gn"], ltri, nseq=nseq, S=S,
                        tm=min(2 * tm, S))
    qc, kc, v = _mla_prep_fwd(pm, pos, invf, w["gq"], w["gkv"], w["wuq"], w["wukv"], tm=min(4 * tm, S))
    attn, lse = run("flash_fwd", _flash_fwd, qc, kc, v, nseq=nseq, S=S, tq=tq)
    yg, ym, mix, pre1, h1b = run("post_attn_fwd", _post_attn_fwd, zg, attn, pt, x2, w["wgo"], w["wmo"], w["wout"],
                                 w["g1"], w["b1"], tm=min(2 * tm, S))
    ug, uv, ucg, ucv, f_in = _ffn_up_fwd(h1b, w["wug"], w["wuv"], w["cw"], w["cb"], S=S, tm=min(2 * tm, S),
                                         tn=FFN_TN)
    dpre2b, dh1, loss8, dg2, db2 = _down_ln2_loss(f_in, w["wd"], pre1, target.reshape(T, D), w["g1"], w["b1"],
                                                  w["g2"], w["b2"], tm=min(2 * tm, S))

    dug, duv, dcg, dcv = _ffn_bwd(dpre2b, w["wd"], ug, uv, ucg, ucv, w["cw"], S=S, tm=tm, tn=FFN_TN)
    g["g2"], g["b2"], g["loss"] = dg2, db2, loss8[0:1, 0:1]
    g["cw"] = jnp.concatenate([dcg[0:3], dcv[0:3]], axis=1)
    g["cb"] = jnp.concatenate([dcg[3:4], dcv[3:4]], axis=1)
    g["wd"] = _matmul(f_in, dpre2b, "tn", name="dw_down", out_dtype=BF, tm=1408, tn=1024, tk=1024)
    g["wugt"] = _matmul(dug, h1b, "tn", name="dw_up_g", out_dtype=BF, tm=1408, tn=1024, tk=1024)
    g["wuvt"] = _matmul(duv, h1b, "tn", name="dw_up_v", out_dtype=BF, tm=1408, tn=1024, tk=1024)
    dh1 = _matmul(dug, w["wugt"], "nn", name="dh1_g", c_in=dh1, tm=1024, tn=1024, tk=1408)
    dh1 = _matmul(duv, w["wuvt"], "nn", name="dh1_v", c_in=dh1, tm=1024, tn=1024, tk=1408)
    dx, dpre1b, dpt, dygb, dymb, dzg, dattn, dg1, db1 = _post_attn_bwd(
        dh1, pre1, pt, yg, ym, w["wgo"], w["wmo"], w["wout"], w["g1"], tm=min(2 * tm, S))
    g["g1"], g["b1"] = dg1, db1
    g["wout"] = _matmul(mix, dpre1b, "tn", name="dw_out", out_dtype=BF, tm=1024, tn=1024, tk=1024)
    g["wgo"] = _matmul(zg, dygb, "tn", name="dw_gla_o", out_dtype=BF, tm=1024, tn=1024, tk=1024)
    g["wmo"] = _matmul(attn, dymb, "tn", name="dw_mla_o", out_dtype=BF, tm=1024, tn=1024, tk=1024)
    dqc, dkc, dv = run("flash_bwd", _flash_bwd, qc, kc, v, attn, dattn, lse, nseq=nseq, S=S, tq=tq)
    dpm, g["wuq"], g["wukv"], g["gq"], g["gkv"] = _mla_prep_bwd(
        pm, pos, invf, w["gq"], w["gkv"], w["wuq"], w["wukv"], dqc, dkc, dv, tm=min(2 * tm, S))
    g["w_mt"] = _matmul(dpm, xb, "tn", name="dw_in_m", out_dtype=BF, tm=768, tn=1024, tk=1024)
    g["w_tt"] = _matmul(dpt, xb, "tn", name="dw_in_t", out_dtype=BF, tm=1024, tn=1024, tk=1024)
    dpg, g["wg"], g["bg"], g["gn"] = run("gla_bwd", _gla_bwd, pg, w["wg"], w["bg"], w["gn"], ltri, utri, o, states,
                                         dzg, nseq=nseq, S=S, tm=min(2 * tm, S))
    g["w_gt"] = _matmul(dpg, xb, "tn", name="dw_in_g", out_dtype=BF, tm=640, tn=1024, tk=1024)
    dx = run("dx", _matmul_sum, dx, [(dpg, w["w_gt"], 640), (dpm, w["w_mt"], 768)], name="dx_gm")
    dx = _matmul_sum(dx, [(dpt, w["w_tt"], 1024)], name="dx_t")
    return loss8[0, 0], dx.reshape(nseq, S, D), g


_IN_SPLITS = (512, 512, 1024, 16, 1024, 384, 256, 64, 1024, 1024)


def _w_in_to_groups(wt):
    offs = [0]
    for s in _IN_SPLITS:
        offs.append(offs[-1] + s)
    q, k, v, r, og, cq, ckv, kr, ga, gb = [wt[offs[i]:offs[i + 1]] for i in range(10)]
    z = lambda n: jnp.zeros((n, wt.shape[1]), wt.dtype)
    return (jnp.concatenate([q, k, v, og, r, z(112)], axis=0),
            jnp.concatenate([cq, kr, z(64), ckv], axis=0),
            jnp.concatenate([ga, gb], axis=0))


W_IN_BLOCK = sum(_IN_SPLITS) // NDEV
_KV_LATENT_ROW = sum(_IN_SPLITS[:6])
_W_IN_LO = 5
_W_IN_SPLIT = _W_IN_LO * W_IN_BLOCK - _KV_LATENT_ROW


def _w_in_rows_lo(g_g, g_m):
    q, k, v, og, r = g_g[0:512], g_g[512:1024], g_g[1024:2048], g_g[2048:3072], g_g[3072:3088]
    return jnp.concatenate([q, k, v, r, og, g_m[0:384], g_m[512:768]], axis=0)[:_W_IN_LO * W_IN_BLOCK]


def _w_in_rows_hi(g_m, g_t):
    return jnp.concatenate([g_m[512:768], g_m[384:448], g_t], axis=0)[_W_IN_SPLIT:]


def _uq_to_kernel(wuq):
    w3 = wuq.reshape(MQR, MH, NOPE + ROPE)
    rope = jnp.concatenate([w3[:, :, NOPE:], jnp.zeros((MQR, MH, 64), wuq.dtype)], axis=2)
    return jnp.concatenate([w3[:, :, :NOPE].reshape(MQR, MH * 128), rope.reshape(MQR, MH * 128)], axis=1)


def _uq_from_kernel(g):
    nope = g[:, :1024].reshape(MQR, MH, 128)
    rope = g[:, 1024:].reshape(MQR, MH, 128)[:, :, :ROPE]
    return jnp.concatenate([nope, rope], axis=2)


def _ukv_to_kernel(wukv):
    w3 = wukv.reshape(MKR, MH, NOPE + MV)
    return jnp.concatenate([w3[:, :, :NOPE].reshape(MKR, MH * 128), w3[:, :, NOPE:].reshape(MKR, MH * 128)], axis=1)


def _ukv_from_kernel(g):
    return jnp.concatenate([g[:, :1024].reshape(MKR, MH, 128), g[:, 1024:].reshape(MKR, MH, 128)], axis=2)


def _cols_gathered(a):
    return a.transpose(1, 0, 2).reshape(a.shape[1], NDEV * a.shape[2])


def _cols_scattered(a):
    R = a.shape[0]
    return a.reshape(R, NDEV, a.shape[1] // NDEV).transpose(1, 0, 2)


_SMALL = (("gla_b_gate", 512), ("gla_norm_g", 256), ("mla_q_norm_g", 384), ("mla_kv_norm_g", 256),
          ("ln1_g", 1024), ("ln1_b", 1024), ("conv_b", 5632), ("ln2_g", 1024), ("ln2_b", 1024))
_SMALL_ROWS = 88
_SMALL_USED = sum(sz for _, sz in _SMALL)


def _pack_small(d):
    flat = jnp.concatenate([d[n].reshape(-1) for n, _ in _SMALL] + ([d['loss'].reshape(-1)] if 'loss' in d else []))
    return jnp.pad(flat, (0, _SMALL_ROWS * 128 - flat.shape[0])).reshape(_SMALL_ROWS, 128)


def _unpack_small(a):
    flat = a.reshape(-1)
    out, off = {}, 0
    for n, sz in _SMALL:
        out[n] = flat[off:off + sz].reshape(1, sz)
        off += sz
    return out


_NAMES = ['w_in', 'gla_w_gate_up', 'gla_b_gate', 'gla_norm_g', 'w_gla_o', 'mla_q_norm_g', 'mla_w_uq',
          'mla_kv_norm_g', 'mla_w_ukv', 'w_mla_o', 'w_out', 'ln1_g', 'ln1_b', 'w_up', 'conv_w', 'conv_b',
          'w_down', 'ln2_g', 'ln2_b']
_SHARDED = ['w_in', 'w_up', 'w_down', 'w_gla_o', 'w_mla_o', 'w_out', 'mla_w_uq', 'mla_w_ukv', 'gla_w_gate_up',
            'conv_w']


def kernel(x, positions, w_in, gla_w_gate_up, gla_b_gate, gla_norm_g, w_gla_o, mla_q_norm_g, mla_w_uq, mla_kv_norm_g, mla_w_ukv, w_mla_o, w_out, ln1_g, ln1_b, w_up, conv_w, conv_b, w_down, ln2_g, ln2_b, loss_target, m_w_in, m_gla_w_gate_up, m_gla_b_gate, m_gla_norm_g, m_w_gla_o, m_mla_q_norm_g, m_mla_w_uq, m_mla_kv_norm_g, m_mla_w_ukv, m_w_mla_o, m_w_out, m_ln1_g, m_ln1_b, m_w_up, m_conv_w, m_conv_b, m_w_down, m_ln2_g, m_ln2_b, v_w_in, v_gla_w_gate_up, v_gla_b_gate, v_gla_norm_g, v_w_gla_o, v_mla_q_norm_g, v_mla_w_uq, v_mla_kv_norm_g, v_mla_w_ukv, v_w_mla_o, v_w_out, v_ln1_g, v_ln1_b, v_w_up, v_conv_w, v_conv_b, v_w_down, v_ln2_g, v_ln2_b):
    W = dict(w_in=w_in, gla_w_gate_up=gla_w_gate_up, gla_b_gate=gla_b_gate, gla_norm_g=gla_norm_g, w_gla_o=w_gla_o, mla_q_norm_g=mla_q_norm_g, mla_w_uq=mla_w_uq, mla_kv_norm_g=mla_kv_norm_g, mla_w_ukv=mla_w_ukv, w_mla_o=w_mla_o, w_out=w_out, ln1_g=ln1_g, ln1_b=ln1_b, w_up=w_up, conv_w=conv_w, conv_b=conv_b, w_down=w_down, ln2_g=ln2_g, ln2_b=ln2_b)
    M = dict(w_in=m_w_in, gla_w_gate_up=m_gla_w_gate_up, gla_b_gate=m_gla_b_gate, gla_norm_g=m_gla_norm_g, w_gla_o=m_w_gla_o, mla_q_norm_g=m_mla_q_norm_g, mla_w_uq=m_mla_w_uq, mla_kv_norm_g=m_mla_kv_norm_g, mla_w_ukv=m_mla_w_ukv, w_mla_o=m_w_mla_o, w_out=m_w_out, ln1_g=m_ln1_g, ln1_b=m_ln1_b, w_up=m_w_up, conv_w=m_conv_w, conv_b=m_conv_b, w_down=m_w_down, ln2_g=m_ln2_g, ln2_b=m_ln2_b)
    V = dict(w_in=v_w_in, gla_w_gate_up=v_gla_w_gate_up, gla_b_gate=v_gla_b_gate, gla_norm_g=v_gla_norm_g, w_gla_o=v_w_gla_o, mla_q_norm_g=v_mla_q_norm_g, mla_w_uq=v_mla_w_uq, mla_kv_norm_g=v_mla_kv_norm_g, mla_w_ukv=v_mla_w_ukv, w_mla_o=v_w_mla_o, w_out=v_w_out, ln1_g=v_ln1_g, ln1_b=v_ln1_b, w_up=v_w_up, conv_w=v_conv_w, conv_b=v_conv_b, w_down=v_w_down, ln2_g=v_ln2_g, ln2_b=v_ln2_b)

    tshard = lambda d, n: d[n][0].T
    shard = lambda n: (W[n][0].astype(BF), False)
    (w_in_t,) = _gather_two_level([tshard(W, 'w_in').astype(BF)], name="gather_w0")
    w_gt, w_mt, w_tt = _w_in_to_groups(w_in_t.reshape(NDEV * W_IN_BLOCK, D))
    kw = dict(
        w_gt=w_gt, w_mt=w_mt, w_tt=w_tt, bg=W['gla_b_gate'],
        gn=W['gla_norm_g'], gq=W['mla_q_norm_g'], gkv=W['mla_kv_norm_g'],
        g1=W['ln1_g'], b1=W['ln1_b'], g2=W['ln2_g'], b2=W['ln2_b'], cb=W['conv_b'],
    )
    received = {}

    def got_mixers(ex, w, g):
        w.update(wuq=_uq_to_kernel(_cols_gathered(ex[0])), wukv=_ukv_to_kernel(_cols_gathered(ex[1])),
                 wg=jnp.pad(_cols_gathered(ex[2]), ((0, 128 - GR), (0, 0))))

    def got_out_proj(ex, w, g):
        w.update(wgo=ex[0].reshape(D, D), wmo=ex[1].reshape(D, D), wout=ex[2].reshape(D, D))

    def got_up(ex, w, g):
        w_upt = ex[0].reshape(2 * DFF, D)
        w.update(wugt=w_upt[:DFF], wuvt=w_upt[DFF:], wug=w_upt[:DFF].T, wuv=w_upt[DFF:].T)

    def got_down(ex, w, g):
        w.update(wd=ex[0].reshape(DFF, D), cw=_cols_gathered(ex[1]))

    slab = lambda a, lo=0: ([(a.astype(BF), lo)], True)
    rows = lambda a, n=NDEV: a.reshape(n, a.shape[0] // n, a.shape[1])

    def keep(names):
        return lambda ex, w, g: received.update(zip(names, ex))

    def small_grads(g):
        return _pack_small(dict(gla_b_gate=g['bg'], gla_norm_g=g['gn'], mla_q_norm_g=g['gq'], mla_kv_norm_g=g['gkv'],
                                ln1_g=g['g1'], ln1_b=g['b1'], conv_b=g['cb'], ln2_g=g['g2'], ln2_b=g['b2'],
                                loss=g['loss']))

    hooks = {
        "proj_g": (lambda w, g: [shard('mla_w_uq'), shard('mla_w_ukv'), shard('gla_w_gate_up')], got_mixers),
        "gla_fwd": (lambda w, g: [shard('w_gla_o'), shard('w_mla_o'), shard('w_out')], got_out_proj),
        "flash_fwd": (lambda w, g: [(tshard(W, 'w_up').astype(BF), False)], got_up),
        "post_attn_fwd": (lambda w, g: [shard('w_down'), (W['conv_w'][0], False)], got_down),
        "flash_bwd": (lambda w, g: [slab(rows(g['wd'])),
                                    ([(rows(g['wugt'], 4), 0), (rows(g['wuvt'], 4), 4)], True),
                                    slab(rows(g['wout'])), slab(rows(g['wgo'])), slab(rows(g['wmo']))],
                      keep(['w_down', 'w_up', 'w_out', 'w_gla_o', 'w_mla_o'])),
        "gla_bwd": (lambda w, g: [slab(_uq_from_kernel(g['wuq']).transpose(1, 0, 2)),
                                  slab(_ukv_from_kernel(g['wukv']).transpose(1, 0, 2)),
                                  slab(rows(_w_in_rows_hi(g['w_mt'], g['w_tt']), NDEV - _W_IN_LO), _W_IN_LO)],
                    keep(['mla_w_uq', 'mla_w_ukv', 'w_in_hi'])),
        "dx": (lambda w, g: [slab(rows(_w_in_rows_lo(g['w_gt'], g['w_mt']), _W_IN_LO)),
                             ([(_cols_scattered(g['wg'][:GR]), 0)], True), ([(_cols_scattered(g['cw']), 0)], True),
                             (small_grads(g), False)],
               keep(['w_in_lo', 'gla_w_gate_up', 'conv_w', 'small'])),
    }

    _, grad_x, _ = _local_step(x, positions, loss_target, kw, hooks)

    grads, deltas, new_m, new_v = {}, {}, {}, {}
    small_parts = received['small']
    loss = jnp.sum(small_parts.reshape(NDEV, -1)[:, _SMALL_USED])
    me = 4 * lax.axis_index("x") + 2 * lax.axis_index("y") + lax.axis_index("c")
    received['w_in'] = jnp.where(me >= _W_IN_LO, received['w_in_hi'], received['w_in_lo'])
    for n in _SHARDED:
        shp = W[n].shape
        if n in ('w_in', 'w_up'):
            out = _adamw(received[n], tshard(W, n), tshard(M, n), tshard(V, n), name="adamw_" + n)
            grads[n], deltas[n], new_m[n], new_v[n] = [t.T.reshape(shp) for t in out]
            continue
        out = _adamw(received[n], W[n][0], M[n][0], V[n][0], name="adamw_" + n)
        grads[n], deltas[n], new_m[n], new_v[n] = [t.reshape(shp) for t in out]
    out = _adamw(small_parts, _pack_small(W), _pack_small(M), _pack_small(V), name="adamw_small")
    for dst, packed in zip((grads, deltas, new_m, new_v), out):
        dst.update(_unpack_small(packed))

    return (loss, grad_x, *[grads[n] for n in _NAMES], *[deltas[n] for n in _NAMES],
            *[new_m[n] for n in _NAMES], *[new_v[n] for n in _NAMES])
```
